```python
import jax, jax.numpy as jnp
from jax import lax
import numpy as np

D_MODEL = 1024
BATCH = 8
SEQ = 2048
DEPTH = 4

ATT_HEADS = 16
ATT_KV_HEADS = 4
ATT_GROUP = ATT_HEADS // ATT_KV_HEADS
ATT_HEAD_DIM = 64
WINDOW = 128
DN_HEADS = 8
DN_HEAD_DIM = 128
CONV_K = 4
CHUNK = 64
D_FF = 4 * D_MODEL
ATT_Q_W = ATT_HEADS * ATT_HEAD_DIM
ATT_KV_W = ATT_KV_HEADS * ATT_HEAD_DIM
DN_W = DN_HEADS * DN_HEAD_DIM
IN_SPLITS = (ATT_Q_W, ATT_KV_W, ATT_KV_W, DN_W, DN_W, DN_W, DN_W, DN_HEADS, DN_HEADS, D_MODEL, D_MODEL)
D_IN = ATT_Q_W + 2 * ATT_KV_W + 4 * DN_W + 2 * DN_HEADS + 2 * D_MODEL
ALPHA = (2 * DEPTH) ** 0.25
BETA_INIT = (8 * DEPTH) ** -0.25
LN_EPS = 1e-5
RMS_EPS = 1e-6
ADA_SCALE = 0.2

kernel_name = "hybrid_swa_sink_gdn_parallel_deepnorm_adaln"


def _split_in(p):
    outs = []
    off = 0
    for w in IN_SPLITS:
        outs.append(p[..., off:off + w])
        off += w
    return outs


def _layer_norm(x, g, b):
    xf = x.astype(jnp.float32)
    mu = jnp.mean(xf, axis=-1, keepdims=True)
    var = jnp.mean(jnp.square(xf - mu), axis=-1, keepdims=True)
    return ((xf - mu) * lax.rsqrt(var + LN_EPS) * g.astype(jnp.float32) + b.astype(jnp.float32)).astype(x.dtype)


def _l2norm(t):
    return t * lax.rsqrt(jnp.sum(jnp.square(t), axis=-1, keepdims=True) + RMS_EPS)


def _causal_conv_silu(x, w):
    s = x.shape[1]
    xp = jnp.pad(x, ((0, 0), (CONV_K - 1, 0), (0, 0)))
    y = sum(xp[:, j:j + s] * w[j] for j in range(CONV_K))
    return jax.nn.silu(y)


def _sliding_window_attention(q, k, v, sinks):
    b, s, _ = q.shape
    nb = s // WINDOW
    qb = q.reshape(b, nb, WINDOW, ATT_KV_HEADS, ATT_GROUP, ATT_HEAD_DIM)

    def band(t):
        tp = jnp.pad(t, ((0, 0), (WINDOW, 0), (0, 0)))
        tb = tp.reshape(b, nb + 1, WINDOW, ATT_KV_HEADS, ATT_HEAD_DIM)
        return jnp.concatenate([tb[:, :-1], tb[:, 1:]], axis=2)

    kb, vb = band(k), band(v)
    scores = jnp.einsum('bnqhgd,bnshd->bnhgqs', qb, kb).astype(jnp.float32) * (ATT_HEAD_DIM ** -0.5)
    qi = jnp.arange(WINDOW)[:, None]
    si = jnp.arange(2 * WINDOW)[None, :]
    diff = qi + WINDOW - si
    blk = jnp.arange(nb)[:, None, None]
    valid = (diff >= 0) & (diff < WINDOW) & (blk * WINDOW + si - WINDOW >= 0)
    scores = jnp.where(valid[None, :, None, None], scores, -jnp.inf)
    sink = sinks.astype(jnp.float32).reshape(1, 1, ATT_KV_HEADS, ATT_GROUP, 1, 1)
    m = jnp.maximum(jnp.max(scores, axis=-1, keepdims=True), sink)
    p = jnp.exp(scores - m)
    denom = jnp.sum(p, axis=-1, keepdims=True) + jnp.exp(sink - m)
    probs = (p / denom).astype(v.dtype)
    o = jnp.einsum('bnhgqs,bnshd->bnqhgd', probs, vb)
    return o.reshape(b, s, ATT_Q_W)


def _gated_delta_rule(q, k, v, beta, g):
    b, s, h, d = q.shape
    n = s // CHUNK

    def chunks(t):
        t = t.reshape((b, n, CHUNK, h) + t.shape[3:])
        return jnp.moveaxis(t, 3, 1)

    q, k, v, beta, g = (chunks(t) for t in (q, k, v, beta, g))
    g_cum = jnp.cumsum(g, axis=-1)
    causal = jnp.tril(jnp.ones((CHUNK, CHUNK), dtype=bool))
    strict = jnp.tril(jnp.ones((CHUNK, CHUNK), dtype=bool), -1)
    decay = jnp.exp(jnp.where(causal, g_cum[..., :, None] - g_cum[..., None, :], -jnp.inf))
    kb = k * beta[..., None]
    vb = v * beta[..., None]
    l_mat = jnp.where(strict, jnp.einsum('bhncd,bhnsd->bhncs', kb, k) * decay, 0.0)
    a_mat = l_mat + jnp.eye(CHUNK, dtype=l_mat.dtype)
    rhs = jnp.concatenate([vb, kb * jnp.exp(g_cum)[..., None]], axis=-1)
    sol = lax.linalg.triangular_solve(a_mat, rhs, left_side=True, lower=True, unit_diagonal=True)
    u, w = sol[..., :d], sol[..., d:]
    intra = jnp.einsum('bhncd,bhnsd->bhncs', q, k) * decay
    q_dec = q * jnp.exp(g_cum)[..., None]
    k_dec = k * jnp.exp(g_cum[..., -1:] - g_cum)[..., None]
    last = jnp.exp(g_cum[..., -1])
    xs = tuple(jnp.moveaxis(t, 2, 0) for t in (u, w, intra, q_dec, k_dec, last))

    def step(state, inp):
        u_c, w_c, intra_c, q_c, k_c, last_c = inp
        v_new = u_c - jnp.einsum('bhck,bhkv->bhcv', w_c, state)
        o_c = jnp.einsum('bhck,bhkv->bhcv', q_c, state) + jnp.einsum('bhcs,bhsv->bhcv', intra_c, v_new)
        state = state * last_c[..., None, None] + jnp.einsum('bhck,bhcv->bhkv', k_c, v_new)
        return state, o_c

    s0 = jnp.zeros((b, h, d, d), jnp.float32)
    _, o = lax.scan(step, s0, xs)
    return jnp.transpose(o, (1, 0, 3, 2, 4)).reshape(b, s, h, d)


def _gated_deltanet(dq, dk, dv, z, b_raw, a_raw, conv_w, a_log, dt_bias, norm_w):
    bsz, s, _ = dq.shape
    qkv = _causal_conv_silu(jnp.concatenate([dq, dk, dv], axis=-1), conv_w)
    shp = (bsz, s, DN_HEADS, DN_HEAD_DIM)
    q = _l2norm(qkv[..., :DN_W].reshape(shp).astype(jnp.float32)) * (DN_HEAD_DIM ** -0.5)
    k = _l2norm(qkv[..., DN_W:2 * DN_W].reshape(shp).astype(jnp.float32))
    v = qkv[..., 2 * DN_W:].reshape(shp).astype(jnp.float32)
    beta = jax.nn.sigmoid(b_raw.astype(jnp.float32))
    g = -jnp.exp(a_log.astype(jnp.float32)) * jax.nn.softplus(a_raw.astype(jnp.float32) + dt_bias.astype(jnp.float32))
    o = _gated_delta_rule(q, k, v, beta, g)
    o = o * lax.rsqrt(jnp.mean(jnp.square(o), axis=-1, keepdims=True) + RMS_EPS) * norm_w.astype(jnp.float32)
    o = o * jax.nn.silu(z.reshape(shp).astype(jnp.float32))
    return o.reshape(bsz, s, DN_W).astype(dq.dtype)


def _fwd_setup_inputs(seed: int = 0) -> dict:
    key = jax.random.key(seed)
    ks = jax.random.split(key, 24)
    nrm = jax.random.normal
    L, D = DEPTH, D_MODEL
    dt = jnp.exp(jax.random.uniform(ks[8], (L, DN_HEADS), minval=np.log(1e-3), maxval=np.log(1e-1)))
    return {
        "x": nrm(ks[0], (BATCH, SEQ, D), jnp.float32),
        "c": nrm(ks[1], (BATCH, D), jnp.float32),
        "w_ada": nrm(ks[2], (L, D, 6 * D), jnp.float32) * (ADA_SCALE * D ** -0.5),
        "b_ada": nrm(ks[3], (L, 6 * D), jnp.float32) * 0.01,
        "w_in": nrm(ks[4], (L, D, D_IN), jnp.float32) * D ** -0.5,
        "conv_w": nrm(ks[5], (L, CONV_K, 3 * DN_W), jnp.float32) * CONV_K ** -0.5,
        "a_log": jnp.log(jax.random.uniform(ks[6], (L, DN_HEADS), minval=1.0, maxval=16.0)),
        "dt_bias": dt + jnp.log(-jnp.expm1(-dt)),
        "sinks": nrm(ks[7], (L, ATT_HEADS), jnp.float32),
        "dn_norm_w": 1.0 + 0.02 * nrm(ks[9], (L, DN_HEAD_DIM), jnp.float32),
        "w_oa": nrm(ks[10], (L, ATT_Q_W, D), jnp.float32) * ATT_Q_W ** -0.5,
        "w_ob": nrm(ks[11], (L, DN_W, D), jnp.float32) * DN_W ** -0.5,
        "w_out": nrm(ks[12], (L, D, D), jnp.float32) * (BETA_INIT * D ** -0.5),
        "ln1_g": 1.0 + 0.02 * nrm(ks[13], (L, D), jnp.float32),
        "ln1_b": 0.02 * nrm(ks[14], (L, D), jnp.float32),
        "w_ff1": nrm(ks[15], (L, D, D_FF), jnp.float32) * D ** -0.5,
        "b_ff1": 0.02 * nrm(ks[16], (L, D_FF), jnp.float32),
        "w_ff2": nrm(ks[17], (L, D_FF, D), jnp.float32) * (BETA_INIT * D_FF ** -0.5),
        "b_ff2": 0.02 * nrm(ks[18], (L, D), jnp.float32),
        "ln2_g": 1.0 + 0.02 * nrm(ks[19], (L, D), jnp.float32),
        "ln2_b": 0.02 * nrm(ks[20], (L, D), jnp.float32),
    }


def _fwd_reference(x, c, w_ada, b_ada, w_in, conv_w, a_log, dt_bias, sinks, dn_norm_w, w_oa, w_ob, w_out,
              ln1_g, ln1_b, w_ff1, b_ff1, w_ff2, b_ff2, ln2_g, ln2_b):
    c_act = jax.nn.silu(c)
    for l in range(DEPTH):
        mod = c_act @ w_ada[l] + b_ada[l]
        sh1, sc1, gt1, sh2, sc2, gt2 = jnp.split(mod[:, None, :], 6, axis=-1)
        u = x * (1.0 + sc1) + sh1
        proj = u @ w_in[l]
        qa, ka, va, dq, dk, dv, z, b_raw, a_raw, g_a, g_b = _split_in(proj)
        y_a = _sliding_window_attention(qa, ka, va, sinks[l]) @ w_oa[l]
        y_b = _gated_deltanet(dq, dk, dv, z, b_raw, a_raw, conv_w[l], a_log[l], dt_bias[l], dn_norm_w[l]) @ w_ob[l]
        mixed = (jax.nn.sigmoid(g_a) * y_a + jax.nn.sigmoid(g_b) * y_b) @ w_out[l]
        x = _layer_norm(ALPHA * x + (1.0 + gt1) * mixed, ln1_g[l], ln1_b[l])
        u2 = x * (1.0 + sc2) + sh2
        h = jnp.square(jax.nn.relu(u2 @ w_ff1[l] + b_ff1[l]))
        x = _layer_norm(ALPHA * x + (1.0 + gt2) * (h @ w_ff2[l] + b_ff2[l]), ln2_g[l], ln2_b[l])
    return x


import jax as _jax
import jax.numpy as _jnp

TWIN_FORMAT = 'train_step'
FWD_PARAMS = ['x', 'c', 'w_ada', 'b_ada', 'w_in', 'conv_w', 'a_log', 'dt_bias', 'sinks', 'dn_norm_w', 'w_oa', 'w_ob', 'w_out', 'ln1_g', 'ln1_b', 'w_ff1', 'b_ff1', 'w_ff2', 'b_ff2', 'ln2_g', 'ln2_b']
TWIN_WEIGHTS = ['w_ada', 'b_ada', 'w_in', 'conv_w', 'a_log', 'dt_bias', 'sinks', 'dn_norm_w', 'w_oa', 'w_ob', 'w_out', 'ln1_g', 'ln1_b', 'w_ff1', 'b_ff1', 'w_ff2', 'b_ff2', 'ln2_g', 'ln2_b']
TWIN_DIFF_INPUT = 'x'
TWIN_INPUTS = ['x', 'c', 'w_ada', 'b_ada', 'w_in', 'conv_w', 'a_log', 'dt_bias', 'sinks', 'dn_norm_w', 'w_oa', 'w_ob', 'w_out', 'ln1_g', 'ln1_b', 'w_ff1', 'b_ff1', 'w_ff2', 'b_ff2', 'ln2_g', 'ln2_b', 'loss_target', 'm_w_ada', 'm_b_ada', 'm_w_in', 'm_conv_w', 'm_a_log', 'm_dt_bias', 'm_sinks', 'm_dn_norm_w', 'm_w_oa', 'm_w_ob', 'm_w_out', 'm_ln1_g', 'm_ln1_b', 'm_w_ff1', 'm_b_ff1', 'm_w_ff2', 'm_b_ff2', 'm_ln2_g', 'm_ln2_b', 'v_w_ada', 'v_b_ada', 'v_w_in', 'v_conv_w', 'v_a_log', 'v_dt_bias', 'v_sinks', 'v_dn_norm_w', 'v_w_oa', 'v_w_ob', 'v_w_out', 'v_ln1_g', 'v_ln1_b', 'v_w_ff1', 'v_b_ff1', 'v_w_ff2', 'v_b_ff2', 'v_ln2_g', 'v_ln2_b']
TWIN_OUTPUTS = ['loss', 'grad_x', 'grad_w_ada', 'grad_b_ada', 'grad_w_in', 'grad_conv_w', 'grad_a_log', 'grad_dt_bias', 'grad_sinks', 'grad_dn_norm_w', 'grad_w_oa', 'grad_w_ob', 'grad_w_out', 'grad_ln1_g', 'grad_ln1_b', 'grad_w_ff1', 'grad_b_ff1', 'grad_w_ff2', 'grad_b_ff2', 'grad_ln2_g', 'grad_ln2_b', 'delta_w_ada', 'delta_b_ada', 'delta_w_in', 'delta_conv_w', 'delta_a_log', 'delta_dt_bias', 'delta_sinks', 'delta_dn_norm_w', 'delta_w_oa', 'delta_w_ob', 'delta_w_out', 'delta_ln1_g', 'delta_ln1_b', 'delta_w_ff1', 'delta_b_ff1', 'delta_w_ff2', 'delta_b_ff2', 'delta_ln2_g', 'delta_ln2_b', 'new_m_w_ada', 'new_m_b_ada', 'new_m_w_in', 'new_m_conv_w', 'new_m_a_log', 'new_m_dt_bias', 'new_m_sinks', 'new_m_dn_norm_w', 'new_m_w_oa', 'new_m_w_ob', 'new_m_w_out', 'new_m_ln1_g', 'new_m_ln1_b', 'new_m_w_ff1', 'new_m_b_ff1', 'new_m_w_ff2', 'new_m_b_ff2', 'new_m_ln2_g', 'new_m_ln2_b', 'new_v_w_ada', 'new_v_b_ada', 'new_v_w_in', 'new_v_conv_w', 'new_v_a_log', 'new_v_dt_bias', 'new_v_sinks', 'new_v_dn_norm_w', 'new_v_w_oa', 'new_v_w_ob', 'new_v_w_out', 'new_v_ln1_g', 'new_v_ln1_b', 'new_v_w_ff1', 'new_v_b_ff1', 'new_v_w_ff2', 'new_v_b_ff2', 'new_v_ln2_g', 'new_v_ln2_b']
TWIN_LEAF_KINDS = {'loss': 'loss', 'grad_x': 'grad_x', 'grad_w_ada': 'grad_w', 'grad_b_ada': 'grad_w', 'grad_w_in': 'grad_w', 'grad_conv_w': 'grad_w', 'grad_a_log': 'grad_w', 'grad_dt_bias': 'grad_w', 'grad_sinks': 'grad_w', 'grad_dn_norm_w': 'grad_w', 'grad_w_oa': 'grad_w', 'grad_w_ob': 'grad_w', 'grad_w_out': 'grad_w', 'grad_ln1_g': 'grad_w', 'grad_ln1_b': 'grad_w', 'grad_w_ff1': 'grad_w', 'grad_b_ff1': 'grad_w', 'grad_w_ff2': 'grad_w', 'grad_b_ff2': 'grad_w', 'grad_ln2_g': 'grad_w', 'grad_ln2_b': 'grad_w', 'delta_w_ada': 'delta_w', 'delta_b_ada': 'delta_w', 'delta_w_in': 'delta_w', 'delta_conv_w': 'delta_w', 'delta_a_log': 'delta_w', 'delta_dt_bias': 'delta_w', 'delta_sinks': 'delta_w', 'delta_dn_norm_w': 'delta_w', 'delta_w_oa': 'delta_w', 'delta_w_ob': 'delta_w', 'delta_w_out': 'delta_w', 'delta_ln1_g': 'delta_w', 'delta_ln1_b': 'delta_w', 'delta_w_ff1': 'delta_w', 'delta_b_ff1': 'delta_w', 'delta_w_ff2': 'delta_w', 'delta_b_ff2': 'delta_w', 'delta_ln2_g': 'delta_w', 'delta_ln2_b': 'delta_w', 'new_m_w_ada': 'new_m', 'new_m_b_ada': 'new_m', 'new_m_w_in': 'new_m', 'new_m_conv_w': 'new_m', 'new_m_a_log': 'new_m', 'new_m_dt_bias': 'new_m', 'new_m_sinks': 'new_m', 'new_m_dn_norm_w': 'new_m', 'new_m_w_oa': 'new_m', 'new_m_w_ob': 'new_m', 'new_m_w_out': 'new_m', 'new_m_ln1_g': 'new_m', 'new_m_ln1_b': 'new_m', 'new_m_w_ff1': 'new_m', 'new_m_b_ff1': 'new_m', 'new_m_w_ff2': 'new_m', 'new_m_b_ff2': 'new_m', 'new_m_ln2_g': 'new_m', 'new_m_ln2_b': 'new_m', 'new_v_w_ada': 'new_v', 'new_v_b_ada': 'new_v', 'new_v_w_in': 'new_v', 'new_v_conv_w': 'new_v', 'new_v_a_log': 'new_v', 'new_v_dt_bias': 'new_v', 'new_v_sinks': 'new_v', 'new_v_dn_norm_w': 'new_v', 'new_v_w_oa': 'new_v', 'new_v_w_ob': 'new_v', 'new_v_w_out': 'new_v', 'new_v_ln1_g': 'new_v', 'new_v_ln1_b': 'new_v', 'new_v_w_ff1': 'new_v', 'new_v_b_ff1': 'new_v', 'new_v_w_ff2': 'new_v', 'new_v_b_ff2': 'new_v', 'new_v_ln2_g': 'new_v', 'new_v_ln2_b': 'new_v'}


def _forward(args):
    return _fwd_reference(*[args[k] for k in FWD_PARAMS])


def _output_shape():
    out = _jax.eval_shape(lambda: _forward(_fwd_setup_inputs(0)))
    return out.shape, out.dtype

N_MICROBATCH = 1
ADAM_LR = 0.001
ADAM_B1 = 0.9
ADAM_B2 = 0.999
ADAM_EPS = 1e-08
ADAM_WD = 0.01
ADAM_STEP = 10
PER_EXAMPLE_BATCH_AXIS = {'x': 0, 'c': 0, 'loss_target': 0}
SHARED_INPUTS = []
_WEIGHT_DTYPES = {'w_ada': _jnp.float32, 'b_ada': _jnp.float32, 'w_in': _jnp.float32, 'conv_w': _jnp.float32, 'a_log': _jnp.float32, 'dt_bias': _jnp.float32, 'sinks': _jnp.float32, 'dn_norm_w': _jnp.float32, 'w_oa': _jnp.float32, 'w_ob': _jnp.float32, 'w_out': _jnp.float32, 'ln1_g': _jnp.float32, 'ln1_b': _jnp.float32, 'w_ff1': _jnp.float32, 'b_ff1': _jnp.float32, 'w_ff2': _jnp.float32, 'b_ff2': _jnp.float32, 'ln2_g': _jnp.float32, 'ln2_b': _jnp.float32}
MOMENT_SCALE = {'w_ada': 2.358501e-02, 'b_ada': 4.525477e-02, 'w_in': 7.263026e-03, 'conv_w': 8.532264e-03, 'a_log': 3.511032e-02, 'dt_bias': 3.442868e-02, 'sinks': 2.929244e-03, 'dn_norm_w': 3.130028e-02, 'w_oa': 6.219395e-03, 'w_ob': 1.138127e-02, 'w_out': 3.090350e-02, 'ln1_g': 4.627141e-01, 'ln1_b': 2.887073e-01, 'w_ff1': 2.310837e-02, 'b_ff1': 3.301998e-02, 'w_ff2': 1.188902e-01, 'b_ff2': 1.684886e-01, 'ln2_g': 8.070816e+00, 'ln2_b': 1.902330e+00}


def _to_microbatches(a, axis):
    t = _jnp.moveaxis(a, axis, 0)
    t = t.reshape((N_MICROBATCH, t.shape[0] // N_MICROBATCH) + t.shape[1:])
    return _jnp.moveaxis(t, 1, axis + 1)


def setup_inputs(seed: int = 0) -> dict:
    inp = _fwd_setup_inputs(seed)
    key = _jax.random.fold_in(_jax.random.key(seed), 7919)
    shape, _ = _output_shape()
    out = dict(inp)
    out["loss_target"] = _jax.random.normal(_jax.random.fold_in(key, 0), shape, _jnp.float32)
    for i, name in enumerate(TWIN_WEIGHTS):
        w = inp[name].astype(_jnp.float32)
        if MOMENT_SCALE is None:
            s = _jnp.sqrt(_jnp.mean(_jnp.square(w)) + 1e-30)
        else:
            s = MOMENT_SCALE[name]
        km, kv = _jax.random.split(_jax.random.fold_in(key, i + 1))
        out[name] = w
        out["m_" + name] = s * _jax.random.normal(km, w.shape, _jnp.float32)
        out["v_" + name] = (s * s) * _jax.random.uniform(kv, w.shape, _jnp.float32, 0.5, 1.5)
    if N_MICROBATCH > 1:
        for name, axis in PER_EXAMPLE_BATCH_AXIS.items():
            out[name] = _to_microbatches(out[name], axis)
    return {'x': out['x'], 'c': out['c'], 'w_ada': out['w_ada'], 'b_ada': out['b_ada'], 'w_in': out['w_in'], 'conv_w': out['conv_w'], 'a_log': out['a_log'], 'dt_bias': out['dt_bias'], 'sinks': out['sinks'], 'dn_norm_w': out['dn_norm_w'], 'w_oa': out['w_oa'], 'w_ob': out['w_ob'], 'w_out': out['w_out'], 'ln1_g': out['ln1_g'], 'ln1_b': out['ln1_b'], 'w_ff1': out['w_ff1'], 'b_ff1': out['b_ff1'], 'w_ff2': out['w_ff2'], 'b_ff2': out['b_ff2'], 'ln2_g': out['ln2_g'], 'ln2_b': out['ln2_b'], 'loss_target': out['loss_target'], 'm_w_ada': out['m_w_ada'], 'm_b_ada': out['m_b_ada'], 'm_w_in': out['m_w_in'], 'm_conv_w': out['m_conv_w'], 'm_a_log': out['m_a_log'], 'm_dt_bias': out['m_dt_bias'], 'm_sinks': out['m_sinks'], 'm_dn_norm_w': out['m_dn_norm_w'], 'm_w_oa': out['m_w_oa'], 'm_w_ob': out['m_w_ob'], 'm_w_out': out['m_w_out'], 'm_ln1_g': out['m_ln1_g'], 'm_ln1_b': out['m_ln1_b'], 'm_w_ff1': out['m_w_ff1'], 'm_b_ff1': out['m_b_ff1'], 'm_w_ff2': out['m_w_ff2'], 'm_b_ff2': out['m_b_ff2'], 'm_ln2_g': out['m_ln2_g'], 'm_ln2_b': out['m_ln2_b'], 'v_w_ada': out['v_w_ada'], 'v_b_ada': out['v_b_ada'], 'v_w_in': out['v_w_in'], 'v_conv_w': out['v_conv_w'], 'v_a_log': out['v_a_log'], 'v_dt_bias': out['v_dt_bias'], 'v_sinks': out['v_sinks'], 'v_dn_norm_w': out['v_dn_norm_w'], 'v_w_oa': out['v_w_oa'], 'v_w_ob': out['v_w_ob'], 'v_w_out': out['v_w_out'], 'v_ln1_g': out['v_ln1_g'], 'v_ln1_b': out['v_ln1_b'], 'v_w_ff1': out['v_w_ff1'], 'v_b_ff1': out['v_b_ff1'], 'v_w_ff2': out['v_w_ff2'], 'v_b_ff2': out['v_b_ff2'], 'v_ln2_g': out['v_ln2_g'], 'v_ln2_b': out['v_ln2_b']}


def _loss(weights, diff, rest, loss_target):
    with _jax.named_scope("forward"):
        args = {**rest, TWIN_DIFF_INPUT: diff, **{k: w.astype(_WEIGHT_DTYPES[k]) for k, w in weights.items()}}
        y = _forward(args)
    with _jax.named_scope("loss_head"):
        err = _jnp.square(y.astype(_jnp.float32) - loss_target)
        return 0.5 * _jnp.sum(_jnp.mean(err, axis=-1)) if err.ndim else 0.5 * err


def _adamw(w, g, m, v):
    m = ADAM_B1 * m + (1.0 - ADAM_B1) * g
    v = ADAM_B2 * v + (1.0 - ADAM_B2) * _jnp.square(g)
    m_hat = m / (1.0 - ADAM_B1 ** ADAM_STEP)
    v_hat = v / (1.0 - ADAM_B2 ** ADAM_STEP)
    delta = -ADAM_LR * (m_hat / (_jnp.sqrt(v_hat) + ADAM_EPS) + ADAM_WD * w)
    return delta, m, v


def reference(x, c, w_ada, b_ada, w_in, conv_w, a_log, dt_bias, sinks, dn_norm_w, w_oa, w_ob, w_out, ln1_g, ln1_b, w_ff1, b_ff1, w_ff2, b_ff2, ln2_g, ln2_b, loss_target, m_w_ada, m_b_ada, m_w_in, m_conv_w, m_a_log, m_dt_bias, m_sinks, m_dn_norm_w, m_w_oa, m_w_ob, m_w_out, m_ln1_g, m_ln1_b, m_w_ff1, m_b_ff1, m_w_ff2, m_b_ff2, m_ln2_g, m_ln2_b, v_w_ada, v_b_ada, v_w_in, v_conv_w, v_a_log, v_dt_bias, v_sinks, v_dn_norm_w, v_w_oa, v_w_ob, v_w_out, v_ln1_g, v_ln1_b, v_w_ff1, v_b_ff1, v_w_ff2, v_b_ff2, v_ln2_g, v_ln2_b):
    given = dict(x=x, c=c, w_ada=w_ada, b_ada=b_ada, w_in=w_in, conv_w=conv_w, a_log=a_log, dt_bias=dt_bias, sinks=sinks, dn_norm_w=dn_norm_w, w_oa=w_oa, w_ob=w_ob, w_out=w_out, ln1_g=ln1_g, ln1_b=ln1_b, w_ff1=w_ff1, b_ff1=b_ff1, w_ff2=w_ff2, b_ff2=b_ff2, ln2_g=ln2_g, ln2_b=ln2_b, loss_target=loss_target, m_w_ada=m_w_ada, m_b_ada=m_b_ada, m_w_in=m_w_in, m_conv_w=m_conv_w, m_a_log=m_a_log, m_dt_bias=m_dt_bias, m_sinks=m_sinks, m_dn_norm_w=m_dn_norm_w, m_w_oa=m_w_oa, m_w_ob=m_w_ob, m_w_out=m_w_out, m_ln1_g=m_ln1_g, m_ln1_b=m_ln1_b, m_w_ff1=m_w_ff1, m_b_ff1=m_b_ff1, m_w_ff2=m_w_ff2, m_b_ff2=m_b_ff2, m_ln2_g=m_ln2_g, m_ln2_b=m_ln2_b, v_w_ada=v_w_ada, v_b_ada=v_b_ada, v_w_in=v_w_in, v_conv_w=v_conv_w, v_a_log=v_a_log, v_dt_bias=v_dt_bias, v_sinks=v_sinks, v_dn_norm_w=v_dn_norm_w, v_w_oa=v_w_oa, v_w_ob=v_w_ob, v_w_out=v_w_out, v_ln1_g=v_ln1_g, v_ln1_b=v_ln1_b, v_w_ff1=v_w_ff1, v_b_ff1=v_b_ff1, v_w_ff2=v_w_ff2, v_b_ff2=v_b_ff2, v_ln2_g=v_ln2_g, v_ln2_b=v_ln2_b)
    weights = {n: given[n] for n in TWIN_WEIGHTS}
    shared = {n: given[n] for n in SHARED_INPUTS}
    per_example = {n: given[n] for n in ['x', 'c']}
    grad_fn = _jax.value_and_grad(_loss, argnums=(0, 1))

    def one_microbatch(ex, loss_target):
        ex = dict(ex)
        diff = ex.pop(TWIN_DIFF_INPUT)
        return grad_fn(weights, diff, {**shared, **ex}, loss_target)

    if N_MICROBATCH == 1:
        loss, (grad_w, grad_x) = one_microbatch(per_example, given["loss_target"])
    else:
        def body(carry, xs):
            loss_sum, grad_sum = carry
            l_k, (gw_k, gx_k) = one_microbatch(xs[0], xs[1])
            with _jax.named_scope("update"):
                return (loss_sum + l_k, _jax.tree.map(_jnp.add, grad_sum, gw_k)), gx_k

        init = (_jnp.zeros((), _jnp.float32), _jax.tree.map(_jnp.zeros_like, weights))
        (loss, grad_w), grad_x = _jax.lax.scan(body, init, (per_example, given["loss_target"]))
    with _jax.named_scope("update"):
        delta_w, new_m, new_v = {}, {}, {}
        for n in TWIN_WEIGHTS:
            delta_w[n], new_m[n], new_v[n] = _adamw(weights[n], grad_w[n], given["m_" + n], given["v_" + n])
    return (loss, grad_x, *[grad_w[n] for n in TWIN_WEIGHTS], *[delta_w[n] for n in TWIN_WEIGHTS],
            *[new_m[n] for n in TWIN_WEIGHTS], *[new_v[n] for n in TWIN_WEIGHTS])
```

```python
import functools

import jax
import jax.numpy as jnp
from jax import lax
from jax.experimental import pallas as pl
from jax.experimental.pallas import tpu as pltpu

F32, BF16 = jnp.float32, jnp.bfloat16
HI = lax.Precision.HIGHEST
MESH = pl.DeviceIdType.MESH

D_MODEL = 1024
DEPTH = 4
ATT_KV_HEADS, ATT_GROUP, ATT_HEAD_DIM, WINDOW = 4, 4, 64, 128
DN_HEADS, DN_HEAD_DIM, CONV_K, CHUNK = 8, 128, 4, 64
D_FF = 4 * D_MODEL
D_IN = 7696
ALPHA = (2 * DEPTH) ** 0.25
LN_EPS = 1e-5
RMS_EPS = 1e-6
ADAM_LR, ADAM_B1, ADAM_B2, ADAM_EPS, ADAM_WD, ADAM_STEP = 0.001, 0.9, 0.999, 1e-08, 0.01, 10

N_CHIPS = 4
N_DEV = 8
LANES = 128
D_IN_P = 8192
C_Q, C_DQ, C_DK, C_DV, C_Z, C_GA, C_GB, C_K, C_V, C_BA = 0, 1024, 2048, 3072, 4096, 5120, 6144, 7168, 7424, 7680
NEG = -1e30
VMEM_LIMIT = 56 << 20


def _pc(body, **kw):
    return pl.pallas_call(body, **kw)


def _cparams(sem=None):
    if sem is None:
        return pltpu.CompilerParams(vmem_limit_bytes=VMEM_LIMIT)
    return pltpu.CompilerParams(vmem_limit_bytes=VMEM_LIMIT, dimension_semantics=sem)


def _mm(a, b, mode, out_dtype, name, tm=512, tn=512):
    if mode == "nn":
        (m, k), (_, n) = a.shape, b.shape
        dims = (((1,), (0,)), ((), ()))
    elif mode == "nt":
        (m, k), (n, _) = a.shape, b.shape
        dims = (((1,), (1,)), ((), ()))
    else:
        (k, m), (_, n) = a.shape, b.shape
        dims = (((0,), (0,)), ((), ()))
    tm, tn = min(tm, m), min(tn, n)
    assert m % tm == 0 and n % tn == 0, (name, m, n, tm, tn)
    a_spec = pl.BlockSpec((k, tm), lambda i, j: (0, i)) if mode == "tn" else pl.BlockSpec((tm, k), lambda i, j: (i, 0))
    b_spec = pl.BlockSpec((tn, k), lambda i, j: (j, 0)) if mode == "nt" else pl.BlockSpec((k, tn), lambda i, j: (0, j))

    def body(a_ref, b_ref, o_ref):
        o_ref[...] = lax.dot_general(a_ref[...], b_ref[...], dims, preferred_element_type=F32).astype(o_ref.dtype)

    return _pc(body, name=name, grid=(m // tm, n // tn), in_specs=[a_spec, b_spec],
               out_specs=pl.BlockSpec((tm, tn), lambda i, j: (i, j)),
               out_shape=jax.ShapeDtypeStruct((m, n), out_dtype), compiler_params=_cparams())(a, b)


def _row_specs(rows, tile):
    return [pl.BlockSpec((tile, w), functools.partial(lambda i, cb: (i, cb), cb=cb)) for (_, cb, w) in rows]


def _vec_specs(vecs):
    return [pl.BlockSpec(v.shape, lambda i: (0, 0)) for v in vecs]


def _rowwise(fn, rows, vecs, outs, name, tile=256):
    n = rows[0][0].shape[0]
    tile = min(tile, n)
    nr, nv = len(rows), len(vecs)

    def body(*refs):
        rv = [r[...].astype(F32) for r in refs[:nr]]
        vv = [r[...] for r in refs[nr:nr + nv]]
        for o_ref, val in zip(refs[nr + nv:], fn(*rv, *vv)):
            o_ref[...] = val.astype(o_ref.dtype)

    res = _pc(body, name=name, grid=(n // tile,), in_specs=_row_specs(rows, tile) + _vec_specs(vecs),
              out_specs=[pl.BlockSpec((tile, w), lambda i: (i, 0)) for (w, _) in outs],
              out_shape=[jax.ShapeDtypeStruct((n, w), dt) for (w, dt) in outs],
              compiler_params=_cparams())(*[r[0] for r in rows], *vecs)
    return res


def _rowwise_bwd(fn, rows, vecs, cts, row_dtypes, name, tile=256, add=None):
    n = rows[0][0].shape[0]
    tile = min(tile, n)
    nr, nv, nc = len(rows), len(vecs), len(cts)
    want = [i for i, dt in enumerate(row_dtypes) if dt is not None]
    n_add = 0 if add is None else 1

    def body(*refs):
        rv = [r[...].astype(F32) for r in refs[:nr]]
        vv = [r[...] for r in refs[nr:nr + nv]]
        cv = [r[...].astype(F32) for r in refs[nr + nv:nr + nv + nc]]
        pos = nr + nv + nc
        add_ref = refs[pos] if n_add else None
        pos += n_add
        row_out = refs[pos:pos + len(want)]
        vec_out = refs[pos + len(want):]
        _, vjp = jax.vjp(fn, *rv, *vv)
        grads = vjp(tuple(cv))
        for o_ref, i in zip(row_out, want):
            gval = grads[i]
            if n_add and add[0] == i:
                gval = gval + add_ref[...]
            o_ref[...] = gval.astype(o_ref.dtype)

        @pl.when(pl.program_id(0) == 0)
        def _():
            for o_ref in vec_out:
                o_ref[...] = jnp.zeros_like(o_ref)

        for o_ref, gval in zip(vec_out, grads[nr:]):
            o_ref[...] += gval

    ct_rows = [(c, 0, c.shape[1]) for c in cts]
    add_rows = [(add[1], 0, add[1].shape[1])] if n_add else []
    res = _pc(body, name=name, grid=(n // tile,),
              in_specs=_row_specs(rows, tile) + _vec_specs(vecs) + _row_specs(ct_rows + add_rows, tile),
              out_specs=[pl.BlockSpec((tile, rows[i][2]), lambda i_: (i_, 0)) for i in want] + _vec_specs(vecs),
              out_shape=[jax.ShapeDtypeStruct((n, rows[i][2]), row_dtypes[i]) for i in want]
              + [jax.ShapeDtypeStruct(v.shape, F32) for v in vecs],
              compiler_params=_cparams(("arbitrary",)))(*[r[0] for r in rows], *vecs, *cts, *[a[0] for a in add_rows])
    return res[:len(want)], res[len(want):]


def _whole(a, cb=0, w=None):
    return (a, cb, a.shape[1] if w is None else w)


def _ln(x, g, b):
    mu = jnp.mean(x, axis=-1, keepdims=True)
    var = jnp.mean(jnp.square(x - mu), axis=-1, keepdims=True)
    return (x - mu) * lax.rsqrt(var + LN_EPS) * g + b


def _silu(x):
    return x * jax.nn.sigmoid(x)


def _softplus(x):
    return jnp.maximum(x, 0.0) + jnp.log(1.0 + jnp.exp(-jnp.abs(x)))


def _f_mod(x, sc, sh):
    return (x * (1.0 + sc) + sh,)


def _f_gate(ga, gb, ya, yb):
    return (jax.nn.sigmoid(ga) * ya + jax.nn.sigmoid(gb) * yb,)


def _f_post1(x, mixed, gt, g1, b1, sc2, sh2):
    x1 = _ln(ALPHA * x + (1.0 + gt) * mixed, g1, b1)
    return x1, x1 * (1.0 + sc2) + sh2


def _f_act(hpre, b):
    return (jnp.square(jnp.maximum(hpre + b, 0.0)),)


def _f_post2(x1, ff, gt, bff2, g2, b2):
    return (_ln(ALPHA * x1 + (1.0 + gt) * (ff + bff2), g2, b2),)


def _attn_valid(n):
    qi = lax.broadcasted_iota(jnp.int32, (WINDOW, 2 * WINDOW), 0)
    si = lax.broadcasted_iota(jnp.int32, (WINDOW, 2 * WINDOW), 1)
    diff = qi + WINDOW - si
    return (diff >= 0) & (diff < WINDOW) & (n * WINDOW + si - WINDOW >= 0)


def _attn_block(q4, kp, kc, vp, vc, sk4, valid):
    kband = jnp.concatenate([kp, kc], axis=0).astype(BF16)
    vband = jnp.concatenate([vp, vc], axis=0).astype(BF16)
    outs = []
    for g in range(ATT_GROUP):
        s = lax.dot_general(q4[g].astype(BF16), kband, (((1,), (1,)), ((), ())), preferred_element_type=F32)
        s = jnp.where(valid, s * (ATT_HEAD_DIM ** -0.5), NEG)
        sink = sk4[g]
        m = lax.stop_gradient(jnp.maximum(jnp.max(s, axis=-1, keepdims=True), sink))
        p = jnp.exp(s - m)
        denom = jnp.sum(p, axis=-1, keepdims=True) + jnp.exp(sink - m)
        probs = (p / denom).astype(BF16)
        outs.append(jnp.dot(probs, vband, preferred_element_type=F32))
    return jnp.stack(outs)


def _attn_specs(s):
    nb = s // WINDOW
    q_spec = pl.BlockSpec((1, ATT_GROUP, WINDOW, ATT_HEAD_DIM), lambda h, n: (h, 0, n, 0))
    prev = pl.BlockSpec((1, WINDOW, ATT_HEAD_DIM), lambda h, n: (h, jnp.maximum(n - 1, 0), 0))
    cur = pl.BlockSpec((1, WINDOW, ATT_HEAD_DIM), lambda h, n: (h, n, 0))
    sk = pl.BlockSpec((1, ATT_GROUP, 1, 1), lambda h, n: (h, 0, 0, 0))
    return nb, q_spec, prev, cur, sk


def _attn_fwd(qh, kh, vh, sinks4):
    s = qh.shape[2]
    nb, q_spec, prev, cur, sk = _attn_specs(s)

    def body(q_ref, kp_ref, kc_ref, vp_ref, vc_ref, sk_ref, o_ref):
        valid = _attn_valid(pl.program_id(1))
        o = _attn_block(q_ref[0], kp_ref[0], kc_ref[0], vp_ref[0], vc_ref[0], sk_ref[0], valid)
        o_ref[0] = o.astype(o_ref.dtype)

    return _pc(body, name="attn_fwd", grid=(ATT_KV_HEADS, nb), in_specs=[q_spec, prev, cur, prev, cur, sk],
               out_specs=q_spec, out_shape=jax.ShapeDtypeStruct(qh.shape, BF16), compiler_params=_cparams())(
                   qh, kh, kh, vh, vh, sinks4)


def _attn_bwd(qh, kh, vh, sinks4, doh):
    s = qh.shape[2]
    nb, q_spec, prev, cur, sk = _attn_specs(s)
    acc = pl.BlockSpec((1, s + WINDOW, ATT_HEAD_DIM), lambda h, n: (h, 0, 0))

    def body(q_ref, kp_ref, kc_ref, vp_ref, vc_ref, sk_ref, do_ref, dq_ref, dk_ref, dv_ref, dsk_ref):
        n = pl.program_id(1)
        valid = _attn_valid(n)
        fn = functools.partial(_attn_block, valid=valid)
        _, vjp = jax.vjp(fn, q_ref[0], kp_ref[0], kc_ref[0], vp_ref[0], vc_ref[0], sk_ref[0])
        dq, dkp, dkc, dvp, dvc, dsk = vjp(do_ref[0].astype(F32))
        dq_ref[0] = dq

        @pl.when(n == 0)
        def _():
            dk_ref[...] = jnp.zeros_like(dk_ref)
            dv_ref[...] = jnp.zeros_like(dv_ref)
            dsk_ref[...] = jnp.zeros_like(dsk_ref)

        band = pl.ds(pl.multiple_of(n * WINDOW, WINDOW), 2 * WINDOW)
        dk_ref[0, band, :] += jnp.concatenate([dkp, dkc], axis=0)
        dv_ref[0, band, :] += jnp.concatenate([dvp, dvc], axis=0)
        dsk_ref[0] += dsk

    kv_shape = jax.ShapeDtypeStruct((ATT_KV_HEADS, s + WINDOW, ATT_HEAD_DIM), F32)
    return _pc(body, name="attn_bwd", grid=(ATT_KV_HEADS, nb), in_specs=[q_spec, prev, cur, prev, cur, sk, q_spec],
               out_specs=[q_spec, acc, acc, sk],
               out_shape=[jax.ShapeDtypeStruct(qh.shape, F32), kv_shape, kv_shape, jax.ShapeDtypeStruct(sinks4.shape, F32)],
               compiler_params=_cparams(("arbitrary", "arbitrary")))(qh, kh, kh, vh, vh, sinks4, doh)


def _bdot(a, b, dims=(((1,), (0,)), ((), ()))):
    return lax.dot_general(a.astype(BF16), b.astype(BF16), dims, preferred_element_type=F32)


def _hdot(a, b, dims=(((1,), (0,)), ((), ()))):
    return lax.dot_general(a, b, dims, precision=HI, preferred_element_type=F32)


_NT = (((1,), (1,)), ((), ()))
_TN = (((0,), (0,)), ((), ()))


def _chunk_masks():
    r = lax.broadcasted_iota(jnp.int32, (CHUNK, CHUNK), 0)
    c = lax.broadcasted_iota(jnp.int32, (CHUNK, CHUNK), 1)
    return r >= c, r > c, (r == c).astype(F32)


def _dn_chunk(qc, kc, vc, bc, gcum, state, masks):
    causal, strict, eye = masks
    gb = jnp.broadcast_to(gcum, (CHUNK, CHUNK))
    decay = jnp.exp(jnp.where(causal, gb - gb.T, NEG))
    kb = kc * bc
    vb = vc * bc
    l_mat = jnp.where(strict, _bdot(kb, kc, _NT) * decay, 0.0)
    p = -l_mat
    t = eye + p
    for _ in range(5):
        p = _hdot(p, p)
        t = t + _hdot(p, t)
    eg = jnp.exp(gcum)
    u = _hdot(t, vb)
    w = _hdot(t, kb * eg)
    intra = _bdot(qc, kc, _NT) * decay
    q_dec = qc * eg
    g_last = gcum[CHUNK - 1:CHUNK, :]
    k_dec = kc * jnp.exp(g_last - gcum)
    v_new = u - _bdot(w, state)
    o = _bdot(q_dec, state) + _bdot(intra, v_new)
    new_state = state * jnp.exp(g_last) + _bdot(k_dec, v_new, _TN)
    return o, new_state


def _l2norm(t):
    return t * lax.rsqrt(jnp.sum(jnp.square(t), axis=-1, keepdims=True) + RMS_EPS)


def _dn_pre(aq, ak, av, ba, alog, dtb, h):
    lane = lax.broadcasted_iota(jnp.int32, (1, LANES), 1)
    pick = lambda t, i: jnp.sum(jnp.where(lane == i, t, 0.0), axis=1, keepdims=True)
    q = _l2norm(_silu(aq)) * (DN_HEAD_DIM ** -0.5)
    k = _l2norm(_silu(ak))
    v = _silu(av)
    beta = jax.nn.sigmoid(pick(ba, h))
    g = -jnp.exp(pick(alog, h)) * _softplus(pick(ba, h + DN_HEADS) + pick(dtb, h))
    return q, k, v, beta, g


def _dn_post(o, z, nw):
    o = o * lax.rsqrt(jnp.mean(jnp.square(o), axis=-1, keepdims=True) + RMS_EPS) * nw
    return o * _silu(z)


_PAD = 8
_TOK_TILE = 512


def _pad_front(pad_ref, x_ref, s):
    pad_ref[pl.ds(0, _PAD), :] = jnp.zeros((_PAD, pad_ref.shape[1]), F32)
    pad_ref[pl.ds(_PAD, s), :] = x_ref[...]


def _conv_tile(pad_ref, w4, r0, n):
    acc = None
    for j in range(CONV_K):
        term = pad_ref[pl.ds(r0 + _PAD - (CONV_K - 1) + j, n), :] * w4[j:j + 1, :]
        acc = term if acc is None else acc + term
    return acc


def _conv_tile_bwd(pad_ref, da_ref, w4, r0, n):
    dx, dw = None, []
    da = da_ref[pl.ds(r0, n), :]
    for j in range(CONV_K):
        term = da_ref[pl.ds(r0 + CONV_K - 1 - j, n), :] * w4[j:j + 1, :]
        dx = term if dx is None else dx + term
        dw.append(jnp.sum(da * pad_ref[pl.ds(r0 + _PAD - (CONV_K - 1) + j, n), :], axis=0, keepdims=True))
    return dx, jnp.concatenate(dw, axis=0)


def _dn_gcum(g_c, causal_f):
    return _hdot(causal_f, jnp.broadcast_to(g_c, (CHUNK, LANES)))[:, 0:1]


def _dn_in_specs(s):
    col = lambda base: pl.BlockSpec((s, DN_HEAD_DIM), functools.partial(lambda h, b: (0, b + h), b=base // DN_HEAD_DIM))
    cw = lambda base: pl.BlockSpec((CONV_K, DN_HEAD_DIM), functools.partial(lambda h, b: (0, b + h), b=base))
    row = pl.BlockSpec((1, LANES), lambda h: (0, 0))
    ba = pl.BlockSpec((s, LANES), lambda h: (0, C_BA // LANES))
    return [col(C_DQ), col(C_DK), col(C_DV), col(C_Z), ba, cw(0), cw(DN_HEADS), cw(2 * DN_HEADS), row, row, row]


def _dn_forward_scan(q_s, k_s, v_s, b_s, g_s, gc_s, o_s, states_ref, s):
    masks = _chunk_masks()
    causal_f = masks[0].astype(F32)

    def step(i, state):
        rows = pl.ds(pl.multiple_of(i * CHUNK, CHUNK), CHUNK)
        gcum = _dn_gcum(g_s[rows, :], causal_f)
        gc_s[rows, :] = gcum
        if states_ref is not None:
            states_ref[i] = state
        o, state = _dn_chunk(q_s[rows, :], k_s[rows, :], v_s[rows, :], b_s[rows, :], gcum, state, masks)
        o_s[rows, :] = o
        return state

    lax.fori_loop(0, s // CHUNK, step, jnp.zeros((DN_HEAD_DIM, DN_HEAD_DIM), F32))


def _dn_fwd(proj, conv_w, alog, dtb, nw):
    s = proj.shape[0]
    d = DN_HEAD_DIM

    tt = min(_TOK_TILE, s)

    def body(xq, xk, xv, z, ba, wq, wk, wv, alog_r, dtb_r, nw_r, o_ref,
             padq, padk, padv, q_s, k_s, v_s, b_s, g_s, gc_s, o_s):
        h = pl.program_id(0)
        _pad_front(padq, xq, s)
        _pad_front(padk, xk, s)
        _pad_front(padv, xv, s)
        for r0 in range(0, s, tt):
            rows = pl.ds(r0, tt)
            aq, ak, av = _conv_tile(padq, wq[...], r0, tt), _conv_tile(padk, wk[...], r0, tt), _conv_tile(padv, wv[...], r0, tt)
            q_s[rows, :], k_s[rows, :], v_s[rows, :], b_s[rows, :], g_s[rows, :] = _dn_pre(
                aq, ak, av, ba[rows, :], alog_r[...], dtb_r[...], h)
        _dn_forward_scan(q_s, k_s, v_s, b_s, g_s, gc_s, o_s, None, s)
        for r0 in range(0, s, tt):
            rows = pl.ds(r0, tt)
            o_ref[rows, :] = _dn_post(o_s[rows, :], z[rows, :], nw_r[...]).astype(o_ref.dtype)

    big = pltpu.VMEM((s, d), F32)
    thin = pltpu.VMEM((s, 1), F32)
    padded = pltpu.VMEM((s + _PAD, d), F32)
    return _pc(body, name="dn_fwd", grid=(DN_HEADS,), in_specs=_dn_in_specs(s),
               out_specs=pl.BlockSpec((s, d), lambda h: (0, h)),
               out_shape=jax.ShapeDtypeStruct((s, DN_HEADS * d), BF16),
               scratch_shapes=[padded, padded, padded, big, big, big, thin, thin, thin, big],
               compiler_params=_cparams())(proj, proj, proj, proj, proj, conv_w, conv_w, conv_w, alog, dtb, nw)


def _dn_bwd(proj, conv_w, alog, dtb, nw, dob):
    s = proj.shape[0]
    d = DN_HEAD_DIM
    nchunk = s // CHUNK

    tt = min(_TOK_TILE, s)

    def body(xq, xk, xv, z, ba, wq, wk, wv, alog_r, dtb_r, nw_r, dob_ref,
             dxq, dxk, dxv, dz, dba, dwq, dwk, dwv, dalog, ddtb, dnw,
             padq, padk, padv, q_s, k_s, v_s, b_s, g_s, gc_s, o_s, states, dq_s, dk_s, dv_s, db_s, dg_s):
        h = pl.program_id(0)
        masks = _chunk_masks()
        causal_f = masks[0].astype(F32)
        pre = functools.partial(_dn_pre, h=h)
        _pad_front(padq, xq, s)
        _pad_front(padk, xk, s)
        _pad_front(padv, xv, s)

        def conv_tiles(r0):
            return _conv_tile(padq, wq[...], r0, tt), _conv_tile(padk, wk[...], r0, tt), _conv_tile(padv, wv[...], r0, tt)

        for r0 in range(0, s, tt):
            rows = pl.ds(r0, tt)
            q_s[rows, :], k_s[rows, :], v_s[rows, :], b_s[rows, :], g_s[rows, :] = pre(
                *conv_tiles(r0), ba[rows, :], alog_r[...], dtb_r[...])
        _dn_forward_scan(q_s, k_s, v_s, b_s, g_s, gc_s, o_s, states, s)
        dnw_v = jnp.zeros((1, LANES), F32)
        for r0 in range(0, s, tt):
            rows = pl.ds(r0, tt)
            _, post_vjp = jax.vjp(_dn_post, o_s[rows, :], z[rows, :], nw_r[...])
            do_raw, dz_v, dnw_t = post_vjp(dob_ref[rows, :].astype(F32))
            dz[rows, :] = dz_v.astype(dz.dtype)
            o_s[rows, :] = do_raw
            dnw_v = dnw_v + dnw_t
        chunk = functools.partial(_dn_chunk, masks=masks)

        def step(i, dstate):
            c = nchunk - 1 - i
            rows = pl.ds(pl.multiple_of(c * CHUNK, CHUNK), CHUNK)
            _, vjp = jax.vjp(chunk, q_s[rows, :], k_s[rows, :], v_s[rows, :], b_s[rows, :], gc_s[rows, :], states[c])
            dq_c, dk_c, dv_c, db_c, dgc_c, dstate = vjp((o_s[rows, :], dstate))
            dq_s[rows, :] = dq_c
            dk_s[rows, :] = dk_c
            dv_s[rows, :] = dv_c
            db_s[rows, :] = db_c
            dg_s[rows, :] = _hdot(causal_f, jnp.broadcast_to(dgc_c, (CHUNK, LANES)), _TN)[:, 0:1]
            return dstate

        lax.fori_loop(0, nchunk, step, jnp.zeros((d, d), F32))

        @pl.when(h == 0)
        def _():
            dba[...] = jnp.zeros_like(dba)
            dalog[...] = jnp.zeros_like(dalog)
            ddtb[...] = jnp.zeros_like(ddtb)
            dnw[...] = jnp.zeros_like(dnw)

        dalog_v = jnp.zeros((1, LANES), F32)
        ddtb_v = jnp.zeros((1, LANES), F32)
        for r0 in range(0, s, tt):
            rows = pl.ds(r0, tt)
            _, pre_vjp = jax.vjp(pre, *conv_tiles(r0), ba[rows, :], alog_r[...], dtb_r[...])
            daq, dak, dav, dba_t, dalog_t, ddtb_t = pre_vjp(
                (dq_s[rows, :], dk_s[rows, :], dv_s[rows, :], db_s[rows, :], dg_s[rows, :]))
            dq_s[rows, :], dk_s[rows, :], dv_s[rows, :] = daq, dak, dav
            dba[rows, :] += dba_t
            dalog_v = dalog_v + dalog_t
            ddtb_v = ddtb_v + ddtb_t
        tail = pl.ds(s, _PAD)
        dq_s[tail, :] = dk_s[tail, :] = dv_s[tail, :] = jnp.zeros((_PAD, d), F32)
        for pad, da_s, w_ref, dx_ref, dw_ref in ((padq, dq_s, wq, dxq, dwq), (padk, dk_s, wk, dxk, dwk), (padv, dv_s, wv, dxv, dwv)):
            dw_acc = jnp.zeros((CONV_K, d), F32)
            for r0 in range(0, s, tt):
                dx_t, dw_t = _conv_tile_bwd(pad, da_s, w_ref[...], r0, tt)
                dx_ref[pl.ds(r0, tt), :] = dx_t.astype(dx_ref.dtype)
                dw_acc = dw_acc + dw_t
            dw_ref[...] = dw_acc
        dalog[...] += dalog_v
        ddtb[...] += ddtb_v
        dnw[...] += dnw_v

    big = pltpu.VMEM((s, d), F32)
    thin = pltpu.VMEM((s, 1), F32)
    padded = pltpu.VMEM((s + _PAD, d), F32)
    w_all = DN_HEADS * d
    col_out = lambda: pl.BlockSpec((s, d), lambda h: (0, h))
    cw_out = lambda: pl.BlockSpec((CONV_K, d), lambda h: (0, h))
    row = lambda: pl.BlockSpec((1, LANES), lambda h: (0, 0))
    big_out = jax.ShapeDtypeStruct((s, w_all), BF16)
    cw_shape = jax.ShapeDtypeStruct((CONV_K, w_all), F32)
    row_shape = jax.ShapeDtypeStruct((1, LANES), F32)
    return _pc(body, name="dn_bwd", grid=(DN_HEADS,),
               in_specs=_dn_in_specs(s) + [pl.BlockSpec((s, d), lambda h: (0, h))],
               out_specs=[col_out(), col_out(), col_out(), col_out(), pl.BlockSpec((s, LANES), lambda h: (0, 0)),
                          cw_out(), cw_out(), cw_out(), row(), row(), row()],
               out_shape=[big_out, big_out, big_out, big_out, jax.ShapeDtypeStruct((s, LANES), F32),
                          cw_shape, cw_shape, cw_shape, row_shape, row_shape, row_shape],
               scratch_shapes=[padded, padded, padded, big, big, big, thin, thin, thin, big,
                               pltpu.VMEM((nchunk, d, d), F32), padded, padded, padded, thin, thin],
               compiler_params=_cparams(("arbitrary",)))(
                   proj, proj, proj, proj, proj, conv_w, conv_w, conv_w, alog, dtb, nw, dob)


def _loss_head(y, target, tile=256):
    n, dm = y.shape
    tile = min(tile, n)

    def body(y_ref, t_ref, dy_ref, loss_ref):
        err = y_ref[...] - t_ref[...]
        dy_ref[...] = err * (1.0 / dm)

        @pl.when(pl.program_id(0) == 0)
        def _():
            loss_ref[...] = jnp.zeros_like(loss_ref)

        loss_ref[...] += 0.5 * jnp.sum(jnp.mean(jnp.square(err), axis=-1, keepdims=True), axis=0, keepdims=True)

    blk = pl.BlockSpec((tile, dm), lambda i: (i, 0))
    return _pc(body, name="loss_head", grid=(n // tile,), in_specs=[blk, blk],
               out_specs=[blk, pl.BlockSpec((1, 1), lambda i: (0, 0))],
               out_shape=[jax.ShapeDtypeStruct((n, dm), F32), jax.ShapeDtypeStruct((1, 1), F32)],
               compiler_params=_cparams(("arbitrary",)))(y, target)


def _ada_fwd(c_all, w_ada, b_shard):
    nl, dm, n = w_ada.shape

    def body(c_ref, w_ref, b_ref, o_ref):
        ca = _silu(c_ref[...]).astype(BF16)
        o_ref[0] = jnp.dot(ca, w_ref[0].astype(BF16), preferred_element_type=F32) + b_ref[0]

    return _pc(body, name="ada_fwd", grid=(nl,),
               in_specs=[pl.BlockSpec((N_DEV, dm), lambda l: (0, 0)), pl.BlockSpec((1, dm, n), lambda l: (l, 0, 0)),
                         pl.BlockSpec((1, 1, n), lambda l: (l, 0, 0))],
               out_specs=pl.BlockSpec((1, N_DEV, n), lambda l: (l, 0, 0)),
               out_shape=jax.ShapeDtypeStruct((nl, N_DEV, n), F32), compiler_params=_cparams())(c_all, w_ada, b_shard)


def _ada_bwd(c_all, dmod):
    nl, _, n = dmod.shape
    dm = c_all.shape[1]

    def body(c_ref, d_ref, o_ref):
        o_ref[0] = _hdot(_silu(c_ref[...]), d_ref[0], _TN)

    return _pc(body, name="ada_bwd", grid=(nl,),
               in_specs=[pl.BlockSpec((N_DEV, dm), lambda l: (0, 0)), pl.BlockSpec((1, N_DEV, n), lambda l: (l, 0, 0))],
               out_specs=pl.BlockSpec((1, dm, n), lambda l: (l, 0, 0)),
               out_shape=jax.ShapeDtypeStruct((nl, dm, n), F32), compiler_params=_cparams())(c_all, dmod)


def _adamw(g, w, m, v):
    m = ADAM_B1 * m + (1.0 - ADAM_B1) * g
    v = ADAM_B2 * v + (1.0 - ADAM_B2) * jnp.square(g)
    m_hat = m / (1.0 - ADAM_B1 ** ADAM_STEP)
    v_hat = v / (1.0 - ADAM_B2 ** ADAM_STEP)
    delta = -ADAM_LR * (m_hat / (jnp.sqrt(v_hat) + ADAM_EPS) + ADAM_WD * w)
    return delta, m, v


def _adam_call(parts, w, m, v, name, tile=128):
    shape = w.shape
    flat = lambda t: t.reshape(-1, shape[-1])
    width = shape[-1]

    def fn(*vals):
        g = vals[0] if len(parts) == 1 else vals[0] + vals[1]
        return (g,) + _adamw(g, *vals[len(parts):])

    rows = [_whole(flat(t)) for t in (*parts, w, m, v)]
    outs = _rowwise(fn, rows, [], [(width, F32)] * 4, name, tile=tile)
    return [o.reshape(shape) for o in outs]


def _sum_slots(r, name, tile=128):
    _, n, width = r.shape
    tile = min(tile, n)

    def body(r_ref, o_ref):
        acc = r_ref[0].astype(F32)
        for j in range(1, N_CHIPS):
            acc = acc + r_ref[j].astype(F32)
        o_ref[...] = acc

    return _pc(body, name=name, grid=(n // tile,), in_specs=[pl.BlockSpec((N_CHIPS, tile, width), lambda i: (0, i, 0))],
               out_specs=pl.BlockSpec((tile, width), lambda i: (i, 0)),
               out_shape=jax.ShapeDtypeStruct((n, width), F32), compiler_params=_cparams())(r)


def _small_adam(g_all, w, m, v):
    def body(g_ref, w_ref, m_ref, v_ref, og, od, om, ov):
        g = g_ref[0]
        for j in range(1, N_DEV):
            g = g + g_ref[j]
        og[...] = g
        od[...], om[...], ov[...] = _adamw(g, w_ref[...], m_ref[...], v_ref[...])

    vm = pl.BlockSpec(memory_space=pltpu.VMEM)
    shp = jax.ShapeDtypeStruct(w.shape, F32)
    return _pc(body, name="small_adam", in_specs=[vm] * 4, out_specs=[vm] * 4, out_shape=[shp] * 4,
               compiler_params=_cparams())(g_all, w, m, v)


def _place():
    return lax.axis_index("x"), lax.axis_index("y"), lax.axis_index("c")


def _flip(v, bit):
    return 1 - v if bit else v


def _all_gather8(a):
    r, n = a.shape

    def body(a_ref, o_ref, send_sems, recv_sems):
        x, y, c = _place()
        me = 4 * x + 2 * y + c
        o_ref[me] = a_ref[...]
        copies = []
        for k in range(1, N_DEV):
            peer = (_flip(x, k & 4), _flip(y, k & 2), _flip(c, k & 1))
            copies.append(pltpu.make_async_remote_copy(
                src_ref=a_ref, dst_ref=o_ref.at[me], send_sem=send_sems.at[k - 1], recv_sem=recv_sems.at[k - 1],
                device_id=peer, device_id_type=MESH))
        for cp in copies:
            cp.start()
        for k in range(1, N_DEV):
            px, py, pc_ = _flip(x, k & 4), _flip(y, k & 2), _flip(c, k & 1)
            pltpu.make_async_remote_copy(
                src_ref=a_ref, dst_ref=o_ref.at[4 * px + 2 * py + pc_], send_sem=send_sems.at[k - 1],
                recv_sem=recv_sems.at[k - 1], device_id=(px, py, pc_), device_id_type=MESH).wait_recv()
        for cp in copies:
            cp.wait_send()

    vm = pl.BlockSpec(memory_space=pltpu.VMEM)
    return _pc(body, name="all_gather8", in_specs=[vm], out_specs=vm,
               out_shape=jax.ShapeDtypeStruct((N_DEV, r, n), a.dtype),
               scratch_shapes=[pltpu.SemaphoreType.DMA((N_DEV - 1,)), pltpu.SemaphoreType.DMA((N_DEV - 1,))],
               compiler_params=_cparams())(a)


def _chip_exchange(arrays, gather, name):
    na = len(arrays)

    def body(*refs):
        ins, outs = refs[:na], refs[na:2 * na]
        send_sems, recv_sems, local_sems = refs[2 * na:]
        x, y, c = _place()
        me = 2 * x + y
        locals_, started = [], []
        for i in range(na):
            src_own = ins[i] if gather else ins[i].at[me]
            local = pltpu.make_async_copy(src_own, outs[i].at[me], local_sems.at[i])
            local.start()
            locals_.append(local)
            for j in range(1, N_CHIPS):
                px, py = _flip(x, j & 2), _flip(y, j & 1)
                src = ins[i] if gather else ins[i].at[2 * px + py]
                cp = pltpu.make_async_remote_copy(
                    src_ref=src, dst_ref=outs[i].at[me], send_sem=send_sems.at[i * 3 + j - 1],
                    recv_sem=recv_sems.at[i * 3 + j - 1], device_id=(px, py, c), device_id_type=MESH)
                cp.start()
                started.append(cp)
        for i in range(na):
            for j in range(1, N_CHIPS):
                px, py = _flip(x, j & 2), _flip(y, j & 1)
                src = ins[i] if gather else ins[i].at[me]
                pltpu.make_async_remote_copy(
                    src_ref=src, dst_ref=outs[i].at[2 * px + py], send_sem=send_sems.at[i * 3 + j - 1],
                    recv_sem=recv_sems.at[i * 3 + j - 1], device_id=(px, py, c), device_id_type=MESH).wait_recv()
        for cp in started:
            cp.wait_send()
        for cp in locals_:
            cp.wait()

    hbm = pl.BlockSpec(memory_space=pl.ANY)
    out_shape = [jax.ShapeDtypeStruct(((N_CHIPS,) + a.shape) if gather else a.shape, a.dtype) for a in arrays]
    return _pc(body, name=name, in_specs=[hbm] * na, out_specs=[hbm] * na, out_shape=out_shape,
               scratch_shapes=[pltpu.SemaphoreType.DMA((3 * na,)), pltpu.SemaphoreType.DMA((3 * na,)),
                               pltpu.SemaphoreType.DMA((na,))],
               compiler_params=_cparams())(*arrays)


def _sibling_exchange(arrays, name):
    na = len(arrays)

    def body(*refs):
        ins, outs = refs[:na], refs[na:2 * na]
        send_sems, recv_sems = refs[2 * na:]
        x, y, c = _place()
        copies = [pltpu.make_async_remote_copy(
            src_ref=ins[i], dst_ref=outs[i], send_sem=send_sems.at[i], recv_sem=recv_sems.at[i],
            device_id=(x, y, 1 - c), device_id_type=MESH) for i in range(na)]
        for cp in copies:
            cp.start()
        for cp in copies:
            cp.wait()

    hbm = pl.BlockSpec(memory_space=pl.ANY)
    return _pc(body, name=name, in_specs=[hbm] * na, out_specs=[hbm] * na,
               out_shape=[jax.ShapeDtypeStruct(a.shape, a.dtype) for a in arrays],
               scratch_shapes=[pltpu.SemaphoreType.DMA((na,)), pltpu.SemaphoreType.DMA((na,))],
               compiler_params=_cparams())(*arrays)


def _heads_q(t):
    s = t.shape[0]
    return t.reshape(s, ATT_KV_HEADS, ATT_GROUP, ATT_HEAD_DIM).transpose(1, 2, 0, 3)


def _unheads_q(t):
    s = t.shape[2]
    return t.transpose(2, 0, 1, 3).reshape(s, ATT_KV_HEADS * ATT_GROUP * ATT_HEAD_DIM)


def _heads_kv(t):
    s = t.shape[0]
    return t.reshape(s, ATT_KV_HEADS, ATT_HEAD_DIM).transpose(1, 0, 2)


def _unheads_kv(t):
    s = t.shape[1]
    return t.transpose(1, 0, 2).reshape(s, ATT_KV_HEADS * ATT_HEAD_DIM)


def _row128(v):
    return jnp.pad(v, (0, LANES - v.shape[0])).reshape(1, LANES)


def _layer_fwd(x, p):
    sh1, sc1, gt1, sh2, sc2, gt2 = [p["mod"][i] for i in range(6)]
    (u,) = _rowwise(_f_mod, [_whole(x)], [sc1, sh1], [(D_MODEL, BF16)], "mod1")
    proj = _mm(u, p["w_in"], "nn", F32, "proj", tm=512, tn=512)
    qh = _heads_q(proj[:, C_Q:C_Q + 1024])
    kh = _heads_kv(proj[:, C_K:C_K + 256])
    vh = _heads_kv(proj[:, C_V:C_V + 256])
    sinks4 = p["sinks"].reshape(ATT_KV_HEADS, ATT_GROUP, 1, 1)
    o_a = _unheads_q(_attn_fwd(qh, kh, vh, sinks4))
    o_b = _dn_fwd(proj, p["conv_w"], _row128(p["a_log"]), _row128(p["dt_bias"]), p["dn_norm_w"].reshape(1, LANES))
    y_a = _mm(o_a, p["w_oa"], "nn", F32, "y_a")
    y_b = _mm(o_b, p["w_ob"], "nn", F32, "y_b")
    (gm,) = _rowwise(_f_gate, [(proj, C_GA // 1024, 1024), (proj, C_GB // 1024, 1024), _whole(y_a), _whole(y_b)], [],
                     [(D_MODEL, BF16)], "gate")
    mixed = _mm(gm, p["w_out"], "nn", F32, "mixed")
    x1, u2 = _rowwise(_f_post1, [_whole(x), _whole(mixed)], [gt1, p["ln1_g"], p["ln1_b"], sc2, sh2],
                      [(D_MODEL, F32), (D_MODEL, BF16)], "post1")
    hpre = _mm(u2, p["w_ff1"], "nn", F32, "ff1")
    (h,) = _rowwise(_f_act, [_whole(hpre)], [p["b_ff1"]], [(D_FF, BF16)], "act")
    ff = _mm(h, p["w_ff2"], "nn", F32, "ff2", tm=256)
    (x2,) = _rowwise(_f_post2, [_whole(x1), _whole(ff)], [gt2, p["b_ff2"], p["ln2_g"], p["ln2_b"]],
                     [(D_MODEL, F32)], "post2")
    saved = dict(x=x, u=u, proj=proj, o_a=o_a, o_b=o_b, y_a=y_a, y_b=y_b, gm=gm, mixed=mixed, x1=x1, u2=u2,
                 hpre=hpre, h=h, ff=ff)
    return x2, saved


def _layer_bwd(dx2, p, sv):
    sh1, sc1, gt1, sh2, sc2, gt2 = [p["mod"][i] for i in range(6)]
    g = {}
    (dx1_a, dff), (dgt2, g["b_ff2"], g["ln2_g"], g["ln2_b"]) = _rowwise_bwd(
        _f_post2, [_whole(sv["x1"]), _whole(sv["ff"])], [gt2, p["b_ff2"], p["ln2_g"], p["ln2_b"]], [dx2],
        [F32, BF16], "post2_bwd")
    dh = _mm(dff, p["w_ff2"], "nt", F32, "dh")
    g["w_ff2"] = _mm(sv["h"], dff, "tn", F32, "dw_ff2")
    (dhpre,), (g["b_ff1"],) = _rowwise_bwd(_f_act, [_whole(sv["hpre"])], [p["b_ff1"]], [dh], [BF16], "act_bwd")
    du2 = _mm(dhpre, p["w_ff1"], "nt", F32, "du2", tm=256)
    g["w_ff1"] = _mm(sv["u2"], dhpre, "tn", F32, "dw_ff1")
    (dx_a, dmixed), (dgt1, g["ln1_g"], g["ln1_b"], dsc2, dsh2) = _rowwise_bwd(
        _f_post1, [_whole(sv["x"]), _whole(sv["mixed"])], [gt1, p["ln1_g"], p["ln1_b"], sc2, sh2], [dx1_a, du2],
        [F32, BF16], "post1_bwd")
    dgm = _mm(dmixed, p["w_out"], "nt", F32, "dgm")
    g["w_out"] = _mm(sv["gm"], dmixed, "tn", F32, "dw_out")
    proj = sv["proj"]
    (dga, dgb, dya, dyb), _ = _rowwise_bwd(
        _f_gate, [(proj, C_GA // 1024, 1024), (proj, C_GB // 1024, 1024), _whole(sv["y_a"]), _whole(sv["y_b"])], [],
        [dgm], [BF16, BF16, BF16, BF16], "gate_bwd")
    do_a = _mm(dya, p["w_oa"], "nt", F32, "do_a")
    g["w_oa"] = _mm(sv["o_a"], dya, "tn", F32, "dw_oa")
    do_b = _mm(dyb, p["w_ob"], "nt", F32, "do_b")
    g["w_ob"] = _mm(sv["o_b"], dyb, "tn", F32, "dw_ob")
    ddq, ddk, ddv, ddz, dba, dwq, dwk, dwv, dalog, ddtb, dnw = _dn_bwd(
        proj, p["conv_w"], _row128(p["a_log"]), _row128(p["dt_bias"]), p["dn_norm_w"].reshape(1, LANES), do_b)
    g["conv_w"] = jnp.concatenate([dwq, dwk, dwv], axis=1)
    g["a_log"], g["dt_bias"], g["dn_norm_w"] = dalog[0, :DN_HEADS], ddtb[0, :DN_HEADS], dnw[0]
    qh = _heads_q(proj[:, C_Q:C_Q + 1024])
    kh = _heads_kv(proj[:, C_K:C_K + 256])
    vh = _heads_kv(proj[:, C_V:C_V + 256])
    sinks4 = p["sinks"].reshape(ATT_KV_HEADS, ATT_GROUP, 1, 1)
    dqh, dkh, dvh, dsk = _attn_bwd(qh, kh, vh, sinks4, _heads_q(do_a))
    g["sinks"] = dsk.reshape(ATT_KV_HEADS * ATT_GROUP)
    s = proj.shape[0]
    dproj = jnp.concatenate([
        _unheads_q(dqh).astype(BF16), ddq, ddk, ddv, ddz, dga, dgb,
        _unheads_kv(dkh[:, WINDOW:, :]).astype(BF16), _unheads_kv(dvh[:, WINDOW:, :]).astype(BF16),
        dba.astype(BF16), jnp.zeros((s, D_IN_P - C_BA - LANES), BF16)], axis=1)
    du = _mm(dproj, p["w_in"], "nt", F32, "du", tm=256)
    g["w_in"] = _mm(sv["u"], dproj, "tn", F32, "dw_in")
    (dx,), (dsc1, dsh1) = _rowwise_bwd(_f_mod, [_whole(sv["x"])], [sc1, sh1], [du], [F32], "mod1_bwd", add=(0, dx_a))
    g["mod"] = jnp.stack([dsh1, dsc1, dgt1, dsh2, dsc2, dgt2])
    return dx, g


def _permute_w_in(w):
    pad = jnp.zeros(w.shape[:-1] + (D_IN_P - D_IN,), w.dtype)
    return jnp.concatenate([w[..., 0:1024], w[..., 1536:5632], w[..., 5648:7696], w[..., 1024:1536],
                            w[..., 5632:5648], pad], axis=-1)


def _unpermute_w_in(g):
    return jnp.concatenate([g[..., 0:1024], g[..., C_K:C_K + 512], g[..., 1024:5120], g[..., C_BA:C_BA + 16],
                            g[..., 5120:7168]], axis=-1)


def _cols_from_chips(t):
    c, l, r, n = t.shape
    return t.transpose(1, 2, 0, 3).reshape(l, r, c * n)


def _cols_to_chips(t):
    l, r, n4 = t.shape
    return t.reshape(l, r, N_CHIPS, n4 // N_CHIPS).transpose(2, 0, 1, 3)


def _rows_from_chips(t):
    c, l, r, n = t.shape
    return t.transpose(1, 0, 2, 3).reshape(l, c * r, n)


def _rows_to_chips(t):
    l, r4, n = t.shape
    return t.reshape(l, N_CHIPS, r4 // N_CHIPS, n).transpose(1, 0, 2, 3)


_REPLICATED = ("b_ada", "a_log", "dt_bias", "sinks", "dn_norm_w", "ln1_g", "ln1_b", "b_ff1", "b_ff2", "ln2_g", "ln2_b")
_SMALL = _REPLICATED + ("conv_w",)
_PACK_W = 1024
_WEIGHT_ORDER = ("w_ada", "b_ada", "w_in", "conv_w", "a_log", "dt_bias", "sinks", "dn_norm_w", "w_oa", "w_ob", "w_out",
                 "ln1_g", "ln1_b", "w_ff1", "b_ff1", "w_ff2", "b_ff2", "ln2_g", "ln2_b")


def _pack_small(d):
    flat = jnp.concatenate([d[k].reshape(-1) for k in _SMALL])
    rows = -(-flat.shape[0] // (_PACK_W * 8)) * 8
    return jnp.pad(flat, (0, rows * _PACK_W - flat.shape[0])).reshape(rows, _PACK_W)


def _unpack_small(packed, shapes):
    flat = packed.reshape(-1)
    out, off = {}, 0
    for k in _SMALL:
        n = 1
        for d_ in shapes[k]:
            n *= d_
        out[k] = flat[off:off + n].reshape(shapes[k])
        off += n
    return out


def kernel(x, c, w_ada, b_ada, w_in, conv_w, a_log, dt_bias, sinks, dn_norm_w, w_oa, w_ob, w_out, ln1_g, ln1_b, w_ff1, b_ff1, w_ff2, b_ff2, ln2_g, ln2_b, loss_target, m_w_ada, m_b_ada, m_w_in, m_conv_w, m_a_log, m_dt_bias, m_sinks, m_dn_norm_w, m_w_oa, m_w_ob, m_w_out, m_ln1_g, m_ln1_b, m_w_ff1, m_b_ff1, m_w_ff2, m_b_ff2, m_ln2_g, m_ln2_b, v_w_ada, v_b_ada, v_w_in, v_conv_w, v_a_log, v_dt_bias, v_sinks, v_dn_norm_w, v_w_oa, v_w_ob, v_w_out, v_ln1_g, v_ln1_b, v_w_ff1, v_b_ff1, v_w_ff2, v_b_ff2, v_ln2_g, v_ln2_b):
    ix, iy, ic = _place()
    chip = 2 * ix + iy
    dev = 4 * ix + 2 * iy + ic
    weights = dict(w_ada=w_ada, b_ada=b_ada, w_in=w_in, conv_w=conv_w, a_log=a_log, dt_bias=dt_bias, sinks=sinks,
                   dn_norm_w=dn_norm_w, w_oa=w_oa, w_ob=w_ob, w_out=w_out, ln1_g=ln1_g, ln1_b=ln1_b, w_ff1=w_ff1,
                   b_ff1=b_ff1, w_ff2=w_ff2, b_ff2=b_ff2, ln2_g=ln2_g, ln2_b=ln2_b)
    mom_m = dict(w_ada=m_w_ada, b_ada=m_b_ada, w_in=m_w_in, conv_w=m_conv_w, a_log=m_a_log, dt_bias=m_dt_bias,
                 sinks=m_sinks, dn_norm_w=m_dn_norm_w, w_oa=m_w_oa, w_ob=m_w_ob, w_out=m_w_out, ln1_g=m_ln1_g,
                 ln1_b=m_ln1_b, w_ff1=m_w_ff1, b_ff1=m_b_ff1, w_ff2=m_w_ff2, b_ff2=m_b_ff2, ln2_g=m_ln2_g, ln2_b=m_ln2_b)
    mom_v = dict(w_ada=v_w_ada, b_ada=v_b_ada, w_in=v_w_in, conv_w=v_conv_w, a_log=v_a_log, dt_bias=v_dt_bias,
                 sinks=v_sinks, dn_norm_w=v_dn_norm_w, w_oa=v_w_oa, w_ob=v_w_ob, w_out=v_w_out, ln1_g=v_ln1_g,
                 ln1_b=v_ln1_b, w_ff1=v_w_ff1, b_ff1=v_b_ff1, w_ff2=v_w_ff2, b_ff2=v_b_ff2, ln2_g=v_ln2_g, ln2_b=v_ln2_b)

    n_ada = w_ada.shape[2]
    c_all = _all_gather8(jnp.pad(c, ((0, 7), (0, 0))))[:, 0, :]
    b_shard = lax.dynamic_slice_in_dim(b_ada, chip * n_ada, n_ada, axis=1).reshape(DEPTH, 1, n_ada)
    mod_t = _ada_fwd(c_all, w_ada, b_shard)
    mod_all = _all_gather8(mod_t.reshape(DEPTH * N_DEV, n_ada)).reshape(N_DEV, DEPTH, N_DEV, n_ada)
    mod_mine = lax.dynamic_index_in_dim(mod_all[0::2], dev, axis=2, keepdims=False)
    mod = mod_mine.transpose(1, 0, 2).reshape(DEPTH, 6, 1, D_MODEL)

    n_cw = conv_w.shape[2]
    cw_all = _all_gather8(conv_w.reshape(DEPTH * CONV_K, n_cw))[0::2]
    conv_full = cw_all.transpose(1, 0, 2).reshape(DEPTH, CONV_K, N_CHIPS * n_cw)

    big = ("w_in", "w_oa", "w_ob", "w_out", "w_ff1", "w_ff2")
    gathered = _chip_exchange([weights[k].astype(BF16) for k in big], True, "gather_weights")
    gw = dict(zip(big, gathered))
    full = dict(w_in=_permute_w_in(_cols_from_chips(gw["w_in"])), w_ff1=_cols_from_chips(gw["w_ff1"]),
                w_oa=_rows_from_chips(gw["w_oa"]), w_ob=_rows_from_chips(gw["w_ob"]),
                w_out=_rows_from_chips(gw["w_out"]), w_ff2=_rows_from_chips(gw["w_ff2"]))

    def layer_params(l):
        p = {k: full[k][l] for k in big}
        p["mod"] = mod[l]
        p["conv_w"] = conv_full[l]
        for k in ("a_log", "dt_bias", "sinks", "dn_norm_w"):
            p[k] = weights[k][l]
        for k in ("ln1_g", "ln1_b", "b_ff1", "b_ff2", "ln2_g", "ln2_b"):
            p[k] = weights[k][l].reshape(1, -1)
        return p

    xs = x[0]
    saved = []
    for l in range(DEPTH):
        xs, sv = _layer_fwd(xs, layer_params(l))
        saved.append(sv)
    dy, loss_local = _loss_head(xs, loss_target[0])
    loss = lax.psum(loss_local[0, 0], ("x", "y", "c"))
    grads = [None] * DEPTH
    dx = dy
    for l in reversed(range(DEPTH)):
        dx, grads[l] = _layer_bwd(dx, layer_params(l), saved[l])
    grad_x = dx[None]
    gstack = {k: jnp.stack([grads[l][k] for l in range(DEPTH)]) for k in grads[0]}

    dmod = gstack["mod"].reshape(DEPTH, 6 * D_MODEL)
    small_g = dict(b_ada=dmod, a_log=gstack["a_log"], dt_bias=gstack["dt_bias"], sinks=gstack["sinks"],
                   dn_norm_w=gstack["dn_norm_w"], ln1_g=gstack["ln1_g"], ln1_b=gstack["ln1_b"], b_ff1=gstack["b_ff1"],
                   b_ff2=gstack["b_ff2"], ln2_g=gstack["ln2_g"], ln2_b=gstack["ln2_b"], conv_w=gstack["conv_w"])
    shapes = {k: weights[k].shape for k in _REPLICATED}
    shapes["conv_w"] = small_g["conv_w"].shape
    g_all = _all_gather8(_pack_small(small_g))
    no_conv = jnp.zeros(shapes["conv_w"], F32)
    small_out = _small_adam(g_all, _pack_small(dict(weights, conv_w=no_conv)), _pack_small(dict(mom_m, conv_w=no_conv)),
                            _pack_small(dict(mom_v, conv_w=no_conv)))
    small_res = [_unpack_small(t, shapes) for t in small_out]
    g_conv = lax.dynamic_slice_in_dim(small_res[0]["conv_w"], chip * n_cw, n_cw, axis=2)
    res = {"conv_w": _adam_call([g_conv], conv_w, m_conv_w, v_conv_w, "adam_conv_w", tile=16)}

    dmod_all = g_all.reshape(N_DEV, -1)[:, :DEPTH * 6 * D_MODEL].reshape(N_DEV, DEPTH, 6 * D_MODEL)
    dmod_shard = lax.dynamic_slice_in_dim(dmod_all, chip * n_ada, n_ada, axis=2).transpose(1, 0, 2)
    g_w_ada = _ada_bwd(c_all, dmod_shard)
    res["w_ada"] = _adam_call([g_w_ada], w_ada, m_w_ada, v_w_ada, "adam_w_ada")

    to_chips = dict(w_in=_cols_to_chips(_unpermute_w_in(gstack["w_in"])), w_ff1=_cols_to_chips(gstack["w_ff1"]),
                    w_oa=_rows_to_chips(gstack["w_oa"]), w_ob=_rows_to_chips(gstack["w_ob"]),
                    w_out=_rows_to_chips(gstack["w_out"]), w_ff2=_rows_to_chips(gstack["w_ff2"]))
    received = _chip_exchange([to_chips[k].astype(BF16) for k in big], False, "scatter_grads")
    partial = [_sum_slots(r.reshape(N_CHIPS, -1, r.shape[-1]), "sum_" + k) for k, r in zip(big, received)]
    theirs = _sibling_exchange(partial, "sibling_grads")
    for k, mine, other in zip(big, partial, theirs):
        shape = weights[k].shape
        res[k] = _adam_call([mine.reshape(shape), other.reshape(shape)], weights[k], mom_m[k], mom_v[k], "adam_" + k)
    for k in _REPLICATED:
        res[k] = [small_res[i][k] for i in range(4)]

    outs = [loss, grad_x]
    for i in range(4):
        outs += [res[k][i] for k in _WEIGHT_ORDER]
    return tuple(outs)
```

```python
import functools

import jax
import jax.numpy as jnp
from jax import lax
from jax.experimental import pallas as pl
from jax.experimental.pallas import tpu as pltpu

F32, BF16 = jnp.float32, jnp.bfloat16
HI = lax.Precision.HIGHEST
MESH = pl.DeviceIdType.MESH

D_MODEL = 1024
DEPTH = 4
ATT_KV_HEADS, ATT_GROUP, ATT_HEAD_DIM, WINDOW = 4, 4, 64, 128
DN_HEADS, DN_HEAD_DIM, CONV_K, CHUNK = 8, 128, 4, 64
D_FF = 4 * D_MODEL
D_IN = 7696
ALPHA = (2 * DEPTH) ** 0.25
LN_EPS = 1e-5
RMS_EPS = 1e-6
ADAM_LR, ADAM_B1, ADAM_B2, ADAM_EPS, ADAM_WD, ADAM_STEP = 0.001, 0.9, 0.999, 1e-08, 0.01, 10

N_CHIPS = 4
N_DEV = 8
LANES = 128
D_IN_P = 8192
C_Q, C_DQ, C_DK, C_DV, C_Z, C_GA, C_GB, C_K, C_V, C_BA = 0, 1024, 2048, 3072, 4096, 5120, 6144, 7168, 7424, 7680
NEG = -1e30
VMEM_LIMIT = 56 << 20


def _pc(body, **kw):
    return pl.pallas_call(body, **kw)


def _cparams(sem=None):
    if sem is None:
        return pltpu.CompilerParams(vmem_limit_bytes=VMEM_LIMIT)
    return pltpu.CompilerParams(vmem_limit_bytes=VMEM_LIMIT, dimension_semantics=sem)


def _mm(a, b, mode, out_dtype, name, tm=512, tn=512):
    if mode == "nn":
        (m, k), (_, n) = a.shape, b.shape
        dims = (((1,), (0,)), ((), ()))
    elif mode == "nt":
        (m, k), (n, _) = a.shape, b.shape
        dims = (((1,), (1,)), ((), ()))
    else:
        (k, m), (_, n) = a.shape, b.shape
        dims = (((0,), (0,)), ((), ()))
    tm, tn = min(tm, m), min(tn, n)
    assert m % tm == 0 and n % tn == 0, (name, m, n, tm, tn)
    a_spec = pl.BlockSpec((k, tm), lambda i, j: (0, i)) if mode == "tn" else pl.BlockSpec((tm, k), lambda i, j: (i, 0))
    b_spec = pl.BlockSpec((tn, k), lambda i, j: (j, 0)) if mode == "nt" else pl.BlockSpec((k, tn), lambda i, j: (0, j))

    def body(a_ref, b_ref, o_ref):
        o_ref[...] = lax.dot_general(a_ref[...], b_ref[...], dims, preferred_element_type=F32).astype(o_ref.dtype)

    return _pc(body, name=name, grid=(m // tm, n // tn), in_specs=[a_spec, b_spec],
               out_specs=pl.BlockSpec((tm, tn), lambda i, j: (i, j)),
               out_shape=jax.ShapeDtypeStruct((m, n), out_dtype), compiler_params=_cparams())(a, b)


def _row_specs(rows, tile):
    return [pl.BlockSpec((tile, w), functools.partial(lambda i, cb: (i, cb), cb=cb)) for (_, cb, w) in rows]


def _vec_specs(vecs):
    return [pl.BlockSpec(v.shape, lambda i: (0, 0)) for v in vecs]


def _rowwise(fn, rows, vecs, outs, name, tile=256):
    n = rows[0][0].shape[0]
    tile = min(tile, n)
    nr, nv = len(rows), len(vecs)

    def body(*refs):
        rv = [r[...].astype(F32) for r in refs[:nr]]
        vv = [r[...] for r in refs[nr:nr + nv]]
        for o_ref, val in zip(refs[nr + nv:], fn(*rv, *vv)):
            o_ref[...] = val.astype(o_ref.dtype)

    res = _pc(body, name=name, grid=(n // tile,), in_specs=_row_specs(rows, tile) + _vec_specs(vecs),
              out_specs=[pl.BlockSpec((tile, w), lambda i: (i, 0)) for (w, _) in outs],
              out_shape=[jax.ShapeDtypeStruct((n, w), dt) for (w, dt) in outs],
              compiler_params=_cparams())(*[r[0] for r in rows], *vecs)
    return res


def _rowwise_bwd(fn, rows, vecs, cts, row_dtypes, name, tile=256, add=None):
    n = rows[0][0].shape[0]
    tile = min(tile, n)
    nr, nv, nc = len(rows), len(vecs), len(cts)
    want = [i for i, dt in enumerate(row_dtypes) if dt is not None]
    n_add = 0 if add is None else 1

    def body(*refs):
        rv = [r[...].astype(F32) for r in refs[:nr]]
        vv = [r[...] for r in refs[nr:nr + nv]]
        cv = [r[...].astype(F32) for r in refs[nr + nv:nr + nv + nc]]
        pos = nr + nv + nc
        add_ref = refs[pos] if n_add else None
        pos += n_add
        row_out = refs[pos:pos + len(want)]
        vec_out = refs[pos + len(want):]
        _, vjp = jax.vjp(fn, *rv, *vv)
        grads = vjp(tuple(cv))
        for o_ref, i in zip(row_out, want):
            gval = grads[i]
            if n_add and add[0] == i:
                gval = gval + add_ref[...]
            o_ref[...] = gval.astype(o_ref.dtype)

        @pl.when(pl.program_id(0) == 0)
        def _():
            for o_ref in vec_out:
                o_ref[...] = jnp.zeros_like(o_ref)

        for o_ref, gval in zip(vec_out, grads[nr:]):
            o_ref[...] += gval

    ct_rows = [(c, 0, c.shape[1]) for c in cts]
    add_rows = [(add[1], 0, add[1].shape[1])] if n_add else []
    res = _pc(body, name=name, grid=(n // tile,),
              in_specs=_row_specs(rows, tile) + _vec_specs(vecs) + _row_specs(ct_rows + add_rows, tile),
              out_specs=[pl.BlockSpec((tile, rows[i][2]), lambda i_: (i_, 0)) for i in want] + _vec_specs(vecs),
              out_shape=[jax.ShapeDtypeStruct((n, rows[i][2]), row_dtypes[i]) for i in want]
              + [jax.ShapeDtypeStruct(v.shape, F32) for v in vecs],
              compiler_params=_cparams(("arbitrary",)))(*[r[0] for r in rows], *vecs, *cts, *[a[0] for a in add_rows])
    return res[:len(want)], res[len(want):]


def _whole(a, cb=0, w=None):
    return (a, cb, a.shape[1] if w is None else w)


def _ln(x, g, b):
    mu = jnp.mean(x, axis=-1, keepdims=True)
    var = jnp.mean(jnp.square(x - mu), axis=-1, keepdims=True)
    return (x - mu) * lax.rsqrt(var + LN_EPS) * g + b


def _silu(x):
    return x * jax.nn.sigmoid(x)


def _softplus(x):
    return jnp.maximum(x, 0.0) + jnp.log(1.0 + jnp.exp(-jnp.abs(x)))


def _f_mod(x, sc, sh):
    return (x * (1.0 + sc) + sh,)


def _f_gate(ga, gb, ya, yb):
    return (jax.nn.sigmoid(ga) * ya + jax.nn.sigmoid(gb) * yb,)


def _f_post1(x, mixed, gt, g1, b1, sc2, sh2):
    x1 = _ln(ALPHA * x + (1.0 + gt) * mixed, g1, b1)
    return x1, x1 * (1.0 + sc2) + sh2


def _f_act(hpre, b):
    return (jnp.square(jnp.maximum(hpre + b, 0.0)),)


def _f_post2(x1, ff, gt, bff2, g2, b2):
    return (_ln(ALPHA * x1 + (1.0 + gt) * (ff + bff2), g2, b2),)


def _attn_valid(n):
    qi = lax.broadcasted_iota(jnp.int32, (WINDOW, 2 * WINDOW), 0)
    si = lax.broadcasted_iota(jnp.int32, (WINDOW, 2 * WINDOW), 1)
    diff = qi + WINDOW - si
    return (diff >= 0) & (diff < WINDOW) & (n * WINDOW + si - WINDOW >= 0)


def _attn_block(q4, kp, kc, vp, vc, sk4, valid):
    kband = jnp.concatenate([kp, kc], axis=0).astype(BF16)
    vband = jnp.concatenate([vp, vc], axis=0).astype(BF16)
    outs = []
    for g in range(ATT_GROUP):
        s = lax.dot_general(q4[g].astype(BF16), kband, (((1,), (1,)), ((), ())), preferred_element_type=F32)
        s = jnp.where(valid, s * (ATT_HEAD_DIM ** -0.5), NEG)
        sink = sk4[g]
        m = lax.stop_gradient(jnp.maximum(jnp.max(s, axis=-1, keepdims=True), sink))
        p = jnp.exp(s - m)
        denom = jnp.sum(p, axis=-1, keepdims=True) + jnp.exp(sink - m)
        probs = (p / denom).astype(BF16)
        outs.append(jnp.dot(probs, vband, preferred_element_type=F32))
    return jnp.stack(outs)


def _attn_specs(s):
    nb = s // WINDOW
    q_spec = pl.BlockSpec((1, ATT_GROUP, WINDOW, ATT_HEAD_DIM), lambda h, n: (h, 0, n, 0))
    prev = pl.BlockSpec((1, WINDOW, ATT_HEAD_DIM), lambda h, n: (h, jnp.maximum(n - 1, 0), 0))
    cur = pl.BlockSpec((1, WINDOW, ATT_HEAD_DIM), lambda h, n: (h, n, 0))
    sk = pl.BlockSpec((1, ATT_GROUP, 1, 1), lambda h, n: (h, 0, 0, 0))
    return nb, q_spec, prev, cur, sk


def _attn_fwd(qh, kh, vh, sinks4):
    s = qh.shape[2]
    nb, q_spec, prev, cur, sk = _attn_specs(s)

    def body(q_ref, kp_ref, kc_ref, vp_ref, vc_ref, sk_ref, o_ref):
        valid = _attn_valid(pl.program_id(1))
        o = _attn_block(q_ref[0], kp_ref[0], kc_ref[0], vp_ref[0], vc_ref[0], sk_ref[0], valid)
        o_ref[0] = o.astype(o_ref.dtype)

    return _pc(body, name="attn_fwd", grid=(ATT_KV_HEADS, nb), in_specs=[q_spec, prev, cur, prev, cur, sk],
               out_specs=q_spec, out_shape=jax.ShapeDtypeStruct(qh.shape, BF16), compiler_params=_cparams())(
                   qh, kh, kh, vh, vh, sinks4)


def _attn_bwd(qh, kh, vh, sinks4, doh):
    s = qh.shape[2]
    nb, q_spec, prev, cur, sk = _attn_specs(s)
    acc = pl.BlockSpec((1, s + WINDOW, ATT_HEAD_DIM), lambda h, n: (h, 0, 0))

    def body(q_ref, kp_ref, kc_ref, vp_ref, vc_ref, sk_ref, do_ref, dq_ref, dk_ref, dv_ref, dsk_ref):
        n = pl.program_id(1)
        valid = _attn_valid(n)
        fn = functools.partial(_attn_block, valid=valid)
        _, vjp = jax.vjp(fn, q_ref[0], kp_ref[0], kc_ref[0], vp_ref[0], vc_ref[0], sk_ref[0])
        dq, dkp, dkc, dvp, dvc, dsk = vjp(do_ref[0].astype(F32))
        dq_ref[0] = dq

        @pl.when(n == 0)
        def _():
            dk_ref[...] = jnp.zeros_like(dk_ref)
            dv_ref[...] = jnp.zeros_like(dv_ref)
            dsk_ref[...] = jnp.zeros_like(dsk_ref)

        band = pl.ds(pl.multiple_of(n * WINDOW, WINDOW), 2 * WINDOW)
        dk_ref[0, band, :] += jnp.concatenate([dkp, dkc], axis=0)
        dv_ref[0, band, :] += jnp.concatenate([dvp, dvc], axis=0)
        dsk_ref[0] += dsk

    kv_shape = jax.ShapeDtypeStruct((ATT_KV_HEADS, s + WINDOW, ATT_HEAD_DIM), F32)
    return _pc(body, name="attn_bwd", grid=(ATT_KV_HEADS, nb), in_specs=[q_spec, prev, cur, prev, cur, sk, q_spec],
               out_specs=[q_spec, acc, acc, sk],
               out_shape=[jax.ShapeDtypeStruct(qh.shape, F32), kv_shape, kv_shape, jax.ShapeDtypeStruct(sinks4.shape, F32)],
               compiler_params=_cparams(("arbitrary", "arbitrary")))(qh, kh, kh, vh, vh, sinks4, doh)


def _bdot(a, b, dims=(((1,), (0,)), ((), ()))):
    return lax.dot_general(a.astype(BF16), b.astype(BF16), dims, preferred_element_type=F32)


def _hdot(a, b, dims=(((1,), (0,)), ((), ()))):
    return lax.dot_general(a, b, dims, precision=HI, preferred_element_type=F32)


_NT = (((1,), (1,)), ((), ()))
_TN = (((0,), (0,)), ((), ()))


def _chunk_masks():
    r = lax.broadcasted_iota(jnp.int32, (CHUNK, CHUNK), 0)
    c = lax.broadcasted_iota(jnp.int32, (CHUNK, CHUNK), 1)
    return r >= c, r > c, (r == c).astype(F32)


def _dn_local(qs, ks, vs, bs, gs, masks):
    causal, strict, eye = masks
    rng = range(len(qs))
    gb = [jnp.broadcast_to(gs[i], (CHUNK, CHUNK)) for i in rng]
    decay = [jnp.exp(jnp.where(causal, gb[i] - gb[i].T, NEG)) for i in rng]
    kb = [ks[i] * bs[i] for i in rng]
    vb = [vs[i] * bs[i] for i in rng]
    kk = [_bdot(kb[i], ks[i], _NT) for i in rng]
    p = [-jnp.where(strict, kk[i] * decay[i], 0.0) for i in rng]
    t = [eye + p[i] for i in rng]
    for _ in range(5):
        p = [_hdot(p[i], p[i]) for i in rng]
        t = [t[i] + _hdot(p[i], t[i]) for i in rng]
    eg = [jnp.exp(gs[i]) for i in rng]
    u = [_hdot(t[i], vb[i]) for i in rng]
    w = [_hdot(t[i], kb[i] * eg[i]) for i in rng]
    qk = [_bdot(qs[i], ks[i], _NT) for i in rng]
    intra = [qk[i] * decay[i] for i in rng]
    q_dec = [qs[i] * eg[i] for i in rng]
    k_dec = [ks[i] * jnp.exp(gs[i][CHUNK - 1:CHUNK, :] - gs[i]) for i in rng]
    return u, w, intra, q_dec, k_dec


def _dn_state(u, w, intra, q_dec, k_dec, gcum, state):
    v_new = u - _bdot(w, state)
    o = _bdot(q_dec, state) + _bdot(intra, v_new)
    new_state = state * jnp.exp(gcum[CHUNK - 1:CHUNK, :]) + _bdot(k_dec, v_new, _TN)
    return o, new_state


def _l2norm(t):
    return t * lax.rsqrt(jnp.sum(jnp.square(t), axis=-1, keepdims=True) + RMS_EPS)


def _dn_pre(aq, ak, av, ba, alog, dtb, h):
    lane = lax.broadcasted_iota(jnp.int32, (1, LANES), 1)
    pick = lambda t, i: jnp.sum(jnp.where(lane == i, t, 0.0), axis=1, keepdims=True)
    q = _l2norm(_silu(aq)) * (DN_HEAD_DIM ** -0.5)
    k = _l2norm(_silu(ak))
    v = _silu(av)
    beta = jax.nn.sigmoid(pick(ba, h))
    g = -jnp.exp(pick(alog, h)) * _softplus(pick(ba, h + DN_HEADS) + pick(dtb, h))
    return q, k, v, beta, g


def _dn_post(o, z, nw):
    o = o * lax.rsqrt(jnp.mean(jnp.square(o), axis=-1, keepdims=True) + RMS_EPS) * nw
    return o * _silu(z)


_PAD = 8
_TOK_TILE = 512


def _pad_front(pad_ref, x_ref, s):
    pad_ref[pl.ds(0, _PAD), :] = jnp.zeros((_PAD, pad_ref.shape[1]), F32)
    pad_ref[pl.ds(_PAD, s), :] = x_ref[...]


def _conv_tile(pad_ref, w4, r0, n):
    acc = None
    for j in range(CONV_K):
        term = pad_ref[pl.ds(r0 + _PAD - (CONV_K - 1) + j, n), :] * w4[j:j + 1, :]
        acc = term if acc is None else acc + term
    return acc


def _conv_tile_bwd(pad_ref, da_ref, w4, r0, n):
    dx, dw = None, []
    da = da_ref[pl.ds(r0, n), :]
    for j in range(CONV_K):
        term = da_ref[pl.ds(r0 + CONV_K - 1 - j, n), :] * w4[j:j + 1, :]
        dx = term if dx is None else dx + term
        dw.append(jnp.sum(da * pad_ref[pl.ds(r0 + _PAD - (CONV_K - 1) + j, n), :], axis=0, keepdims=True))
    return dx, jnp.concatenate(dw, axis=0)


def _dn_gcum(g_c, causal_f):
    return _hdot(causal_f, jnp.broadcast_to(g_c, (CHUNK, LANES)))[:, 0:1]


def _dn_in_specs(s):
    col = lambda base: pl.BlockSpec((s, DN_HEAD_DIM), functools.partial(lambda h, b: (0, b + h), b=base // DN_HEAD_DIM))
    cw = lambda base: pl.BlockSpec((CONV_K, DN_HEAD_DIM), functools.partial(lambda h, b: (0, b + h), b=base))
    row = pl.BlockSpec((1, LANES), lambda h: (0, 0))
    ba = pl.BlockSpec((s, LANES), lambda h: (0, C_BA // LANES))
    return [col(C_DQ), col(C_DK), col(C_DV), col(C_Z), ba, cw(0), cw(DN_HEADS), cw(2 * DN_HEADS), row, row, row]


def _chunk_rows(c):
    return pl.ds(pl.multiple_of(c * CHUNK, CHUNK), CHUNK)


def _group(nchunk, want):
    g = min(want, nchunk)
    assert nchunk % g == 0
    return g


def _dn_forward_scan(q_s, k_s, v_s, b_s, g_s, gc_s, loc, o_s, states_ref, s):
    masks = _chunk_masks()
    causal_f = masks[0].astype(F32)
    nchunk = s // CHUNK
    grp = _group(nchunk, 4)
    u_s, w_s, in_s, qd_s, kd_s = loc

    def local_step(i, carry):
        rows = [_chunk_rows(i * grp + j) for j in range(grp)]
        gcum = [_dn_gcum(g_s[r, :], causal_f) for r in rows]
        u, w, intra, q_dec, k_dec = _dn_local([q_s[r, :] for r in rows], [k_s[r, :] for r in rows],
                                              [v_s[r, :] for r in rows], [b_s[r, :] for r in rows], gcum, masks)
        for j, r in enumerate(rows):
            gc_s[r, :] = gcum[j]
            u_s[r, :] = u[j]
            w_s[r, :] = w[j].astype(w_s.dtype)
            in_s[r, :] = intra[j].astype(in_s.dtype)
            qd_s[r, :] = q_dec[j].astype(qd_s.dtype)
            kd_s[r, :] = k_dec[j].astype(kd_s.dtype)
        return carry

    lax.fori_loop(0, nchunk // grp, local_step, 0)

    def state_step(i, state):
        rows = _chunk_rows(i)
        if states_ref is not None:
            states_ref[i] = state
        o, state = _dn_state(u_s[rows, :], w_s[rows, :], in_s[rows, :], qd_s[rows, :], kd_s[rows, :], gc_s[rows, :], state)
        o_s[rows, :] = o
        return state

    lax.fori_loop(0, nchunk, state_step, jnp.zeros((DN_HEAD_DIM, DN_HEAD_DIM), F32))


def _dn_local_scratch(s):
    d = DN_HEAD_DIM
    return [pltpu.VMEM((s, d), F32), pltpu.VMEM((s, d), BF16), pltpu.VMEM((s, CHUNK), BF16), pltpu.VMEM((s, d), BF16),
            pltpu.VMEM((s, d), BF16)]


def _dn_fwd(proj, conv_w, alog, dtb, nw):
    s = proj.shape[0]
    d = DN_HEAD_DIM

    tt = min(_TOK_TILE, s)

    def body(xq, xk, xv, z, ba, wq, wk, wv, alog_r, dtb_r, nw_r, o_ref,
             padq, padk, padv, q_s, k_s, v_s, b_s, g_s, gc_s, o_s, *loc):
        h = pl.program_id(0)
        _pad_front(padq, xq, s)
        _pad_front(padk, xk, s)
        _pad_front(padv, xv, s)
        for r0 in range(0, s, tt):
            rows = pl.ds(r0, tt)
            aq, ak, av = _conv_tile(padq, wq[...], r0, tt), _conv_tile(padk, wk[...], r0, tt), _conv_tile(padv, wv[...], r0, tt)
            q_s[rows, :], k_s[rows, :], v_s[rows, :], b_s[rows, :], g_s[rows, :] = _dn_pre(
                aq, ak, av, ba[rows, :], alog_r[...], dtb_r[...], h)
        _dn_forward_scan(q_s, k_s, v_s, b_s, g_s, gc_s, loc, o_s, None, s)
        for r0 in range(0, s, tt):
            rows = pl.ds(r0, tt)
            o_ref[rows, :] = _dn_post(o_s[rows, :], z[rows, :], nw_r[...]).astype(o_ref.dtype)

    big = pltpu.VMEM((s, d), F32)
    thin = pltpu.VMEM((s, 1), F32)
    padded = pltpu.VMEM((s + _PAD, d), F32)
    return _pc(body, name="dn_fwd", grid=(DN_HEADS,), in_specs=_dn_in_specs(s),
               out_specs=pl.BlockSpec((s, d), lambda h: (0, h)),
               out_shape=jax.ShapeDtypeStruct((s, DN_HEADS * d), BF16),
               scratch_shapes=[padded, padded, padded, big, big, big, thin, thin, thin, big] + _dn_local_scratch(s),
               compiler_params=_cparams())(proj, proj, proj, proj, proj, conv_w, conv_w, conv_w, alog, dtb, nw)


def _dn_bwd(proj, conv_w, alog, dtb, nw, dob):
    s = proj.shape[0]
    d = DN_HEAD_DIM
    nchunk = s // CHUNK

    tt = min(_TOK_TILE, s)

    def body(xq, xk, xv, z, ba, wq, wk, wv, alog_r, dtb_r, nw_r, dob_ref,
             dxq, dxk, dxv, dz, dba, dwq, dwk, dwv, dalog, ddtb, dnw,
             padq, padk, padv, q_s, k_s, v_s, b_s, g_s, gc_s, o_s, states, dq_s, dk_s, dv_s, db_s, dg_s,
             dkd_s, din_s, dgc_s, *loc):
        h = pl.program_id(0)
        masks = _chunk_masks()
        causal_f = masks[0].astype(F32)
        pre = functools.partial(_dn_pre, h=h)
        _pad_front(padq, xq, s)
        _pad_front(padk, xk, s)
        _pad_front(padv, xv, s)

        def conv_tiles(r0):
            return _conv_tile(padq, wq[...], r0, tt), _conv_tile(padk, wk[...], r0, tt), _conv_tile(padv, wv[...], r0, tt)

        for r0 in range(0, s, tt):
            rows = pl.ds(r0, tt)
            q_s[rows, :], k_s[rows, :], v_s[rows, :], b_s[rows, :], g_s[rows, :] = pre(
                *conv_tiles(r0), ba[rows, :], alog_r[...], dtb_r[...])
        _dn_forward_scan(q_s, k_s, v_s, b_s, g_s, gc_s, loc, o_s, states, s)
        u_s, w_s, in_s, qd_s, kd_s = loc
        dnw_v = jnp.zeros((1, LANES), F32)
        for r0 in range(0, s, tt):
            rows = pl.ds(r0, tt)
            _, post_vjp = jax.vjp(_dn_post, o_s[rows, :], z[rows, :], nw_r[...])
            do_raw, dz_v, dnw_t = post_vjp(dob_ref[rows, :].astype(F32))
            dz[rows, :] = dz_v.astype(dz.dtype)
            o_s[rows, :] = do_raw
            dnw_v = dnw_v + dnw_t

        def state_step(i, dstate):
            c = nchunk - 1 - i
            rows = _chunk_rows(c)
            _, vjp = jax.vjp(_dn_state, u_s[rows, :], w_s[rows, :].astype(F32), in_s[rows, :].astype(F32),
                             qd_s[rows, :].astype(F32), kd_s[rows, :].astype(F32), gc_s[rows, :], states[c])
            du, dw, din, dqd, dkd, dgc, dstate = vjp((o_s[rows, :], dstate))
            dq_s[rows, :] = du
            dk_s[rows, :] = dw
            dv_s[rows, :] = dqd
            dkd_s[rows, :] = dkd
            din_s[rows, :] = din
            dgc_s[rows, :] = dgc
            return dstate

        lax.fori_loop(0, nchunk, state_step, jnp.zeros((d, d), F32))
        local = functools.partial(_dn_local, masks=masks)
        grp = _group(nchunk, 4)

        def local_step(i, carry):
            rows = [_chunk_rows(i * grp + j) for j in range(grp)]
            get = lambda ref: [ref[r, :] for r in rows]
            _, vjp = jax.vjp(local, get(q_s), get(k_s), get(v_s), get(b_s), get(gc_s))
            dq_c, dk_c, dv_c, db_c, dgc_c = vjp((get(dq_s), get(dk_s), get(din_s), get(dv_s), get(dkd_s)))
            dgc_c = [dgc_c[j] + dgc_s[r, :] for j, r in enumerate(rows)]
            dg_c = [_hdot(causal_f, jnp.broadcast_to(t, (CHUNK, LANES)), _TN)[:, 0:1] for t in dgc_c]
            for j, r in enumerate(rows):
                dq_s[r, :] = dq_c[j]
                dk_s[r, :] = dk_c[j]
                dv_s[r, :] = dv_c[j]
                db_s[r, :] = db_c[j]
                dg_s[r, :] = dg_c[j]
            return carry

        lax.fori_loop(0, nchunk // grp, local_step, 0)

        @pl.when(h == 0)
        def _():
            dba[...] = jnp.zeros_like(dba)
            dalog[...] = jnp.zeros_like(dalog)
            ddtb[...] = jnp.zeros_like(ddtb)
            dnw[...] = jnp.zeros_like(dnw)

        dalog_v = jnp.zeros((1, LANES), F32)
        ddtb_v = jnp.zeros((1, LANES), F32)
        for r0 in range(0, s, tt):
            rows = pl.ds(r0, tt)
            _, pre_vjp = jax.vjp(pre, *conv_tiles(r0), ba[rows, :], alog_r[...], dtb_r[...])
            daq, dak, dav, dba_t, dalog_t, ddtb_t = pre_vjp(
                (dq_s[rows, :], dk_s[rows, :], dv_s[rows, :], db_s[rows, :], dg_s[rows, :]))
            dq_s[rows, :], dk_s[rows, :], dv_s[rows, :] = daq, dak, dav
            dba[rows, :] += dba_t
            dalog_v = dalog_v + dalog_t
            ddtb_v = ddtb_v + ddtb_t
        tail = pl.ds(s, _PAD)
        dq_s[tail, :] = dk_s[tail, :] = dv_s[tail, :] = jnp.zeros((_PAD, d), F32)
        for pad, da_s, w_ref, dx_ref, dw_ref in ((padq, dq_s, wq, dxq, dwq), (padk, dk_s, wk, dxk, dwk), (padv, dv_s, wv, dxv, dwv)):
            dw_acc = jnp.zeros((CONV_K, d), F32)
            for r0 in range(0, s, tt):
                dx_t, dw_t = _conv_tile_bwd(pad, da_s, w_ref[...], r0, tt)
                dx_ref[pl.ds(r0, tt), :] = dx_t.astype(dx_ref.dtype)
                dw_acc = dw_acc + dw_t
            dw_ref[...] = dw_acc
        dalog[...] += dalog_v
        ddtb[...] += ddtb_v
        dnw[...] += dnw_v

    big = pltpu.VMEM((s, d), F32)
    thin = pltpu.VMEM((s, 1), F32)
    padded = pltpu.VMEM((s + _PAD, d), F32)
    w_all = DN_HEADS * d
    col_out = lambda: pl.BlockSpec((s, d), lambda h: (0, h))
    cw_out = lambda: pl.BlockSpec((CONV_K, d), lambda h: (0, h))
    row = lambda: pl.BlockSpec((1, LANES), lambda h: (0, 0))
    big_out = jax.ShapeDtypeStruct((s, w_all), BF16)
    cw_shape = jax.ShapeDtypeStruct((CONV_K, w_all), F32)
    row_shape = jax.ShapeDtypeStruct((1, LANES), F32)
    return _pc(body, name="dn_bwd", grid=(DN_HEADS,),
               in_specs=_dn_in_specs(s) + [pl.BlockSpec((s, d), lambda h: (0, h))],
               out_specs=[col_out(), col_out(), col_out(), col_out(), pl.BlockSpec((s, LANES), lambda h: (0, 0)),
                          cw_out(), cw_out(), cw_out(), row(), row(), row()],
               out_shape=[big_out, big_out, big_out, big_out, jax.ShapeDtypeStruct((s, LANES), F32),
                          cw_shape, cw_shape, cw_shape, row_shape, row_shape, row_shape],
               scratch_shapes=[padded, padded, padded, big, big, big, thin, thin, thin, big,
                               pltpu.VMEM((nchunk, d, d), F32), padded, padded, padded, thin, thin,
                               big, pltpu.VMEM((s, CHUNK), F32), thin] + _dn_local_scratch(s),
               compiler_params=_cparams(("arbitrary",)))(
                   proj, proj, proj, proj, proj, conv_w, conv_w, conv_w, alog, dtb, nw, dob)


def _loss_head(y, target, tile=256):
    n, dm = y.shape
    tile = min(tile, n)

    def body(y_ref, t_ref, dy_ref, loss_ref):
        err = y_ref[...] - t_ref[...]
        dy_ref[...] = err * (1.0 / dm)

        @pl.when(pl.program_id(0) == 0)
        def _():
            loss_ref[...] = jnp.zeros_like(loss_ref)

        loss_ref[...] += 0.5 * jnp.sum(jnp.mean(jnp.square(err), axis=-1, keepdims=True), axis=0, keepdims=True)

    blk = pl.BlockSpec((tile, dm), lambda i: (i, 0))
    return _pc(body, name="loss_head", grid=(n // tile,), in_specs=[blk, blk],
               out_specs=[blk, pl.BlockSpec((1, 1), lambda i: (0, 0))],
               out_shape=[jax.ShapeDtypeStruct((n, dm), F32), jax.ShapeDtypeStruct((1, 1), F32)],
               compiler_params=_cparams(("arbitrary",)))(y, target)


def _ada_fwd(c_all, w_ada, b_shard):
    nl, dm, n = w_ada.shape

    def body(c_ref, w_ref, b_ref, o_ref):
        ca = _silu(c_ref[...]).astype(BF16)
        o_ref[0] = jnp.dot(ca, w_ref[0].astype(BF16), preferred_element_type=F32) + b_ref[0]

    return _pc(body, name="ada_fwd", grid=(nl,),
               in_specs=[pl.BlockSpec((N_DEV, dm), lambda l: (0, 0)), pl.BlockSpec((1, dm, n), lambda l: (l, 0, 0)),
                         pl.BlockSpec((1, 1, n), lambda l: (l, 0, 0))],
               out_specs=pl.BlockSpec((1, N_DEV, n), lambda l: (l, 0, 0)),
               out_shape=jax.ShapeDtypeStruct((nl, N_DEV, n), F32), compiler_params=_cparams())(c_all, w_ada, b_shard)


def _ada_bwd(c_all, dmod):
    nl, _, n = dmod.shape
    dm = c_all.shape[1]

    def body(c_ref, d_ref, o_ref):
        o_ref[0] = _hdot(_silu(c_ref[...]), d_ref[0], _TN)

    return _pc(body, name="ada_bwd", grid=(nl,),
               in_specs=[pl.BlockSpec((N_DEV, dm), lambda l: (0, 0)), pl.BlockSpec((1, N_DEV, n), lambda l: (l, 0, 0))],
               out_specs=pl.BlockSpec((1, dm, n), lambda l: (l, 0, 0)),
               out_shape=jax.ShapeDtypeStruct((nl, dm, n), F32), compiler_params=_cparams())(c_all, dmod)


def _adamw(g, w, m, v):
    m = ADAM_B1 * m + (1.0 - ADAM_B1) * g
    v = ADAM_B2 * v + (1.0 - ADAM_B2) * jnp.square(g)
    m_hat = m / (1.0 - ADAM_B1 ** ADAM_STEP)
    v_hat = v / (1.0 - ADAM_B2 ** ADAM_STEP)
    delta = -ADAM_LR * (m_hat / (jnp.sqrt(v_hat) + ADAM_EPS) + ADAM_WD * w)
    return delta, m, v


def _adam_call(parts, w, m, v, name, tile=128):
    shape = w.shape
    flat = lambda t: t.reshape(-1, shape[-1])
    width = shape[-1]

    def fn(*vals):
        g = vals[0] if len(parts) == 1 else vals[0] + vals[1]
        return (g,) + _adamw(g, *vals[len(parts):])

    rows = [_whole(flat(t)) for t in (*parts, w, m, v)]
    outs = _rowwise(fn, rows, [], [(width, F32)] * 4, name, tile=tile)
    return [o.reshape(shape) for o in outs]


def _sum_slots(r, name, tile=128):
    _, n, width = r.shape
    tile = min(tile, n)

    def body(r_ref, o_ref):
        acc = r_ref[0].astype(F32)
        for j in range(1, N_CHIPS):
            acc = acc + r_ref[j].astype(F32)
        o_ref[...] = acc

    return _pc(body, name=name, grid=(n // tile,), in_specs=[pl.BlockSpec((N_CHIPS, tile, width), lambda i: (0, i, 0))],
               out_specs=pl.BlockSpec((tile, width), lambda i: (i, 0)),
               out_shape=jax.ShapeDtypeStruct((n, width), F32), compiler_params=_cparams())(r)


def _small_adam(g_all, w, m, v):
    def body(g_ref, w_ref, m_ref, v_ref, og, od, om, ov):
        g = g_ref[0]
        for j in range(1, N_DEV):
            g = g + g_ref[j]
        og[...] = g
        od[...], om[...], ov[...] = _adamw(g, w_ref[...], m_ref[...], v_ref[...])

    vm = pl.BlockSpec(memory_space=pltpu.VMEM)
    shp = jax.ShapeDtypeStruct(w.shape, F32)
    return _pc(body, name="small_adam", in_specs=[vm] * 4, out_specs=[vm] * 4, out_shape=[shp] * 4,
               compiler_params=_cparams())(g_all, w, m, v)


def _place():
    return lax.axis_index("x"), lax.axis_index("y"), lax.axis_index("c")


def _flip(v, bit):
    return 1 - v if bit else v


def _all_gather8(a):
    r, n = a.shape

    def body(a_ref, o_ref, send_sems, recv_sems):
        x, y, c = _place()
        me = 4 * x + 2 * y + c
        o_ref[me] = a_ref[...]
        copies = []
        for k in range(1, N_DEV):
            peer = (_flip(x, k & 4), _flip(y, k & 2), _flip(c, k & 1))
            copies.append(pltpu.make_async_remote_copy(
                src_ref=a_ref, dst_ref=o_ref.at[me], send_sem=send_sems.at[k - 1], recv_sem=recv_sems.at[k - 1],
                device_id=peer, device_id_type=MESH))
        for cp in copies:
            cp.start()
        for k in range(1, N_DEV):
            px, py, pc_ = _flip(x, k & 4), _flip(y, k & 2), _flip(c, k & 1)
            pltpu.make_async_remote_copy(
                src_ref=a_ref, dst_ref=o_ref.at[4 * px + 2 * py + pc_], send_sem=send_sems.at[k - 1],
                recv_sem=recv_sems.at[k - 1], device_id=(px, py, pc_), device_id_type=MESH).wait_recv()
        for cp in copies:
            cp.wait_send()

    vm = pl.BlockSpec(memory_space=pltpu.VMEM)
    return _pc(body, name="all_gather8", in_specs=[vm], out_specs=vm,
               out_shape=jax.ShapeDtypeStruct((N_DEV, r, n), a.dtype),
               scratch_shapes=[pltpu.SemaphoreType.DMA((N_DEV - 1,)), pltpu.SemaphoreType.DMA((N_DEV - 1,))],
               compiler_params=_cparams())(a)


def _chip_exchange(arrays, gather, name):
    na = len(arrays)

    def body(*refs):
        ins, outs = refs[:na], refs[na:2 * na]
        send_sems, recv_sems, local_sems = refs[2 * na:]
        x, y, c = _place()
        me = 2 * x + y
        locals_, started = [], []
        for i in range(na):
            src_own = ins[i] if gather else ins[i].at[me]
            local = pltpu.make_async_copy(src_own, outs[i].at[me], local_sems.at[i])
            local.start()
            locals_.append(local)
            for j in range(1, N_CHIPS):
                px, py = _flip(x, j & 2), _flip(y, j & 1)
                src = ins[i] if gather else ins[i].at[2 * px + py]
                cp = pltpu.make_async_remote_copy(
                    src_ref=src, dst_ref=outs[i].at[me], send_sem=send_sems.at[i * 3 + j - 1],
                    recv_sem=recv_sems.at[i * 3 + j - 1], device_id=(px, py, c), device_id_type=MESH)
                cp.start()
                started.append(cp)
        for i in range(na):
            for j in range(1, N_CHIPS):
                px, py = _flip(x, j & 2), _flip(y, j & 1)
                src = ins[i] if gather else ins[i].at[me]
                pltpu.make_async_remote_copy(
                    src_ref=src, dst_ref=outs[i].at[2 * px + py], send_sem=send_sems.at[i * 3 + j - 1],
                    recv_sem=recv_sems.at[i * 3 + j - 1], device_id=(px, py, c), device_id_type=MESH).wait_recv()
        for cp in started:
            cp.wait_send()
        for cp in locals_:
            cp.wait()

    hbm = pl.BlockSpec(memory_space=pl.ANY)
    out_shape = [jax.ShapeDtypeStruct(((N_CHIPS,) + a.shape) if gather else a.shape, a.dtype) for a in arrays]
    return _pc(body, name=name, in_specs=[hbm] * na, out_specs=[hbm] * na, out_shape=out_shape,
               scratch_shapes=[pltpu.SemaphoreType.DMA((3 * na,)), pltpu.SemaphoreType.DMA((3 * na,)),
                               pltpu.SemaphoreType.DMA((na,))],
               compiler_params=_cparams())(*arrays)


def _sibling_exchange(arrays, name):
    na = len(arrays)

    def body(*refs):
        ins, outs = refs[:na], refs[na:2 * na]
        send_sems, recv_sems = refs[2 * na:]
        x, y, c = _place()
        copies = [pltpu.make_async_remote_copy(
            src_ref=ins[i], dst_ref=outs[i], send_sem=send_sems.at[i], recv_sem=recv_sems.at[i],
            device_id=(x, y, 1 - c), device_id_type=MESH) for i in range(na)]
        for cp in copies:
            cp.start()
        for cp in copies:
            cp.wait()

    hbm = pl.BlockSpec(memory_space=pl.ANY)
    return _pc(body, name=name, in_specs=[hbm] * na, out_specs=[hbm] * na,
               out_shape=[jax.ShapeDtypeStruct(a.shape, a.dtype) for a in arrays],
               scratch_shapes=[pltpu.SemaphoreType.DMA((na,)), pltpu.SemaphoreType.DMA((na,))],
               compiler_params=_cparams())(*arrays)


def _heads_q(t):
    s = t.shape[0]
    return t.reshape(s, ATT_KV_HEADS, ATT_GROUP, ATT_HEAD_DIM).transpose(1, 2, 0, 3)


def _unheads_q(t):
    s = t.shape[2]
    return t.transpose(2, 0, 1, 3).reshape(s, ATT_KV_HEADS * ATT_GROUP * ATT_HEAD_DIM)


def _heads_kv(t):
    s = t.shape[0]
    return t.reshape(s, ATT_KV_HEADS, ATT_HEAD_DIM).transpose(1, 0, 2)


def _unheads_kv(t):
    s = t.shape[1]
    return t.transpose(1, 0, 2).reshape(s, ATT_KV_HEADS * ATT_HEAD_DIM)


def _row128(v):
    return jnp.pad(v, (0, LANES - v.shape[0])).reshape(1, LANES)


def _layer_fwd(x, p):
    sh1, sc1, gt1, sh2, sc2, gt2 = [p["mod"][i] for i in range(6)]
    (u,) = _rowwise(_f_mod, [_whole(x)], [sc1, sh1], [(D_MODEL, BF16)], "mod1")
    proj = _mm(u, p["w_in"], "nn", F32, "proj", tm=512, tn=512)
    qh = _heads_q(proj[:, C_Q:C_Q + 1024])
    kh = _heads_kv(proj[:, C_K:C_K + 256])
    vh = _heads_kv(proj[:, C_V:C_V + 256])
    sinks4 = p["sinks"].reshape(ATT_KV_HEADS, ATT_GROUP, 1, 1)
    o_a = _unheads_q(_attn_fwd(qh, kh, vh, sinks4))
    o_b = _dn_fwd(proj, p["conv_w"], _row128(p["a_log"]), _row128(p["dt_bias"]), p["dn_norm_w"].reshape(1, LANES))
    y_a = _mm(o_a, p["w_oa"], "nn", F32, "y_a")
    y_b = _mm(o_b, p["w_ob"], "nn", F32, "y_b")
    (gm,) = _rowwise(_f_gate, [(proj, C_GA // 1024, 1024), (proj, C_GB // 1024, 1024), _whole(y_a), _whole(y_b)], [],
                     [(D_MODEL, BF16)], "gate")
    mixed = _mm(gm, p["w_out"], "nn", F32, "mixed")
    x1, u2 = _rowwise(_f_post1, [_whole(x), _whole(mixed)], [gt1, p["ln1_g"], p["ln1_b"], sc2, sh2],
                      [(D_MODEL, F32), (D_MODEL, BF16)], "post1")
    hpre = _mm(u2, p["w_ff1"], "nn", F32, "ff1")
    (h,) = _rowwise(_f_act, [_whole(hpre)], [p["b_ff1"]], [(D_FF, BF16)], "act")
    ff = _mm(h, p["w_ff2"], "nn", F32, "ff2", tm=256)
    (x2,) = _rowwise(_f_post2, [_whole(x1), _whole(ff)], [gt2, p["b_ff2"], p["ln2_g"], p["ln2_b"]],
                     [(D_MODEL, F32)], "post2")
    saved = dict(x=x, u=u, proj=proj, o_a=o_a, o_b=o_b, y_a=y_a, y_b=y_b, gm=gm, mixed=mixed, x1=x1, u2=u2,
                 hpre=hpre, h=h, ff=ff)
    return x2, saved


def _layer_bwd(dx2, p, sv):
    sh1, sc1, gt1, sh2, sc2, gt2 = [p["mod"][i] for i in range(6)]
    g = {}
    (dx1_a, dff), (dgt2, g["b_ff2"], g["ln2_g"], g["ln2_b"]) = _rowwise_bwd(
        _f_post2, [_whole(sv["x1"]), _whole(sv["ff"])], [gt2, p["b_ff2"], p["ln2_g"], p["ln2_b"]], [dx2],
        [F32, BF16], "post2_bwd")
    dh = _mm(dff, p["w_ff2"], "nt", F32, "dh")
    g["w_ff2"] = _mm(sv["h"], dff, "tn", F32, "dw_ff2")
    (dhpre,), (g["b_ff1"],) = _rowwise_bwd(_f_act, [_whole(sv["hpre"])], [p["b_ff1"]], [dh], [BF16], "act_bwd")
    du2 = _mm(dhpre, p["w_ff1"], "nt", F32, "du2", tm=256)
    g["w_ff1"] = _mm(sv["u2"], dhpre, "tn", F32, "dw_ff1")
    (dx_a, dmixed), (dgt1, g["ln1_g"], g["ln1_b"], dsc2, dsh2) = _rowwise_bwd(
        _f_post1, [_whole(sv["x"]), _whole(sv["mixed"])], [gt1, p["ln1_g"], p["ln1_b"], sc2, sh2], [dx1_a, du2],
        [F32, BF16], "post1_bwd")
    dgm = _mm(dmixed, p["w_out"], "nt", F32, "dgm")
    g["w_out"] = _mm(sv["gm"], dmixed, "tn", F32, "dw_out")
    proj = sv["proj"]
    (dga, dgb, dya, dyb), _ = _rowwise_bwd(
        _f_gate, [(proj, C_GA // 1024, 1024), (proj, C_GB // 1024, 1024), _whole(sv["y_a"]), _whole(sv["y_b"])], [],
        [dgm], [BF16, BF16, BF16, BF16], "gate_bwd")
    do_a = _mm(dya, p["w_oa"], "nt", F32, "do_a")
    g["w_oa"] = _mm(sv["o_a"], dya, "tn", F32, "dw_oa")
    do_b = _mm(dyb, p["w_ob"], "nt", F32, "do_b")
    g["w_ob"] = _mm(sv["o_b"], dyb, "tn", F32, "dw_ob")
    ddq, ddk, ddv, ddz, dba, dwq, dwk, dwv, dalog, ddtb, dnw = _dn_bwd(
        proj, p["conv_w"], _row128(p["a_log"]), _row128(p["dt_bias"]), p["dn_norm_w"].reshape(1, LANES), do_b)
    g["conv_w"] = jnp.concatenate([dwq, dwk, dwv], axis=1)
    g["a_log"], g["dt_bias"], g["dn_norm_w"] = dalog[0, :DN_HEADS], ddtb[0, :DN_HEADS], dnw[0]
    qh = _heads_q(proj[:, C_Q:C_Q + 1024])
    kh = _heads_kv(proj[:, C_K:C_K + 256])
    vh = _heads_kv(proj[:, C_V:C_V + 256])
    sinks4 = p["sinks"].reshape(ATT_KV_HEADS, ATT_GROUP, 1, 1)
    dqh, dkh, dvh, dsk = _attn_bwd(qh, kh, vh, sinks4, _heads_q(do_a))
    g["sinks"] = dsk.reshape(ATT_KV_HEADS * ATT_GROUP)
    s = proj.shape[0]
    dproj = jnp.concatenate([
        _unheads_q(dqh).astype(BF16), ddq, ddk, ddv, ddz, dga, dgb,
        _unheads_kv(dkh[:, WINDOW:, :]).astype(BF16), _unheads_kv(dvh[:, WINDOW:, :]).astype(BF16),
        dba.astype(BF16), jnp.zeros((s, D_IN_P - C_BA - LANES), BF16)], axis=1)
    du = _mm(dproj, p["w_in"], "nt", F32, "du", tm=256)
    g["w_in"] = _mm(sv["u"], dproj, "tn", F32, "dw_in")
    (dx,), (dsc1, dsh1) = _rowwise_bwd(_f_mod, [_whole(sv["x"])], [sc1, sh1], [du], [F32], "mod1_bwd", add=(0, dx_a))
    g["mod"] = jnp.stack([dsh1, dsc1, dgt1, dsh2, dsc2, dgt2])
    return dx, g


def _permute_w_in(w):
    pad = jnp.zeros(w.shape[:-1] + (D_IN_P - D_IN,), w.dtype)
    return jnp.concatenate([w[..., 0:1024], w[..., 1536:5632], w[..., 5648:7696], w[..., 1024:1536],
                            w[..., 5632:5648], pad], axis=-1)


def _unpermute_w_in(g):
    return jnp.concatenate([g[..., 0:1024], g[..., C_K:C_K + 512], g[..., 1024:5120], g[..., C_BA:C_BA + 16],
                            g[..., 5120:7168]], axis=-1)


def _cols_from_chips(t):
    c, l, r, n = t.shape
    return t.transpose(1, 2, 0, 3).reshape(l, r, c * n)


def _cols_to_chips(t):
    l, r, n4 = t.shape
    return t.reshape(l, r, N_CHIPS, n4 // N_CHIPS).transpose(2, 0, 1, 3)


def _rows_from_chips(t):
    c, l, r, n = t.shape
    return t.transpose(1, 0, 2, 3).reshape(l, c * r, n)


def _rows_to_chips(t):
    l, r4, n = t.shape
    return t.reshape(l, N_CHIPS, r4 // N_CHIPS, n).transpose(1, 0, 2, 3)


_REPLICATED = ("b_ada", "a_log", "dt_bias", "sinks", "dn_norm_w", "ln1_g", "ln1_b", "b_ff1", "b_ff2", "ln2_g", "ln2_b")
_SMALL = _REPLICATED + ("conv_w",)
_PACK_W = 1024
_WEIGHT_ORDER = ("w_ada", "b_ada", "w_in", "conv_w", "a_log", "dt_bias", "sinks", "dn_norm_w", "w_oa", "w_ob", "w_out",
                 "ln1_g", "ln1_b", "w_ff1", "b_ff1", "w_ff2", "b_ff2", "ln2_g", "ln2_b")


def _pack_small(d):
    flat = jnp.concatenate([d[k].reshape(-1) for k in _SMALL])
    rows = -(-flat.shape[0] // (_PACK_W * 8)) * 8
    return jnp.pad(flat, (0, rows * _PACK_W - flat.shape[0])).reshape(rows, _PACK_W)


def _unpack_small(packed, shapes):
    flat = packed.reshape(-1)
    out, off = {}, 0
    for k in _SMALL:
        n = 1
        for d_ in shapes[k]:
            n *= d_
        out[k] = flat[off:off + n].reshape(shapes[k])
        off += n
    return out


def kernel(x, c, w_ada, b_ada, w_in, conv_w, a_log, dt_bias, sinks, dn_norm_w, w_oa, w_ob, w_out, ln1_g, ln1_b, w_ff1, b_ff1, w_ff2, b_ff2, ln2_g, ln2_b, loss_target, m_w_ada, m_b_ada, m_w_in, m_conv_w, m_a_log, m_dt_bias, m_sinks, m_dn_norm_w, m_w_oa, m_w_ob, m_w_out, m_ln1_g, m_ln1_b, m_w_ff1, m_b_ff1, m_w_ff2, m_b_ff2, m_ln2_g, m_ln2_b, v_w_ada, v_b_ada, v_w_in, v_conv_w, v_a_log, v_dt_bias, v_sinks, v_dn_norm_w, v_w_oa, v_w_ob, v_w_out, v_ln1_g, v_ln1_b, v_w_ff1, v_b_ff1, v_w_ff2, v_b_ff2, v_ln2_g, v_ln2_b):
    ix, iy, ic = _place()
    chip = 2 * ix + iy
    dev = 4 * ix + 2 * iy + ic
    weights = dict(w_ada=w_ada, b_ada=b_ada, w_in=w_in, conv_w=conv_w, a_log=a_log, dt_bias=dt_bias, sinks=sinks,
                   dn_norm_w=dn_norm_w, w_oa=w_oa, w_ob=w_ob, w_out=w_out, ln1_g=ln1_g, ln1_b=ln1_b, w_ff1=w_ff1,
                   b_ff1=b_ff1, w_ff2=w_ff2, b_ff2=b_ff2, ln2_g=ln2_g, ln2_b=ln2_b)
    mom_m = dict(w_ada=m_w_ada, b_ada=m_b_ada, w_in=m_w_in, conv_w=m_conv_w, a_log=m_a_log, dt_bias=m_dt_bias,
                 sinks=m_sinks, dn_norm_w=m_dn_norm_w, w_oa=m_w_oa, w_ob=m_w_ob, w_out=m_w_out, ln1_g=m_ln1_g,
                 ln1_b=m_ln1_b, w_ff1=m_w_ff1, b_ff1=m_b_ff1, w_ff2=m_w_ff2, b_ff2=m_b_ff2, ln2_g=m_ln2_g, ln2_b=m_ln2_b)
    mom_v = dict(w_ada=v_w_ada, b_ada=v_b_ada, w_in=v_w_in, conv_w=v_conv_w, a_log=v_a_log, dt_bias=v_dt_bias,
                 sinks=v_sinks, dn_norm_w=v_dn_norm_w, w_oa=v_w_oa, w_ob=v_w_ob, w_out=v_w_out, ln1_g=v_ln1_g,
                 ln1_b=v_ln1_b, w_ff1=v_w_ff1, b_ff1=v_b_ff1, w_ff2=v_w_ff2, b_ff2=v_b_ff2, ln2_g=v_ln2_g, ln2_b=v_ln2_b)

    n_ada = w_ada.shape[2]
    c_all = _all_gather8(jnp.pad(c, ((0, 7), (0, 0))))[:, 0, :]
    b_shard = lax.dynamic_slice_in_dim(b_ada, chip * n_ada, n_ada, axis=1).reshape(DEPTH, 1, n_ada)
    mod_t = _ada_fwd(c_all, w_ada, b_shard)
    mod_all = _all_gather8(mod_t.reshape(DEPTH * N_DEV, n_ada)).reshape(N_DEV, DEPTH, N_DEV, n_ada)
    mod_mine = lax.dynamic_index_in_dim(mod_all[0::2], dev, axis=2, keepdims=False)
    mod = mod_mine.transpose(1, 0, 2).reshape(DEPTH, 6, 1, D_MODEL)

    n_cw = conv_w.shape[2]
    cw_all = _all_gather8(conv_w.reshape(DEPTH * CONV_K, n_cw))[0::2]
    conv_full = cw_all.transpose(1, 0, 2).reshape(DEPTH, CONV_K, N_CHIPS * n_cw)

    big = ("w_in", "w_oa", "w_ob", "w_out", "w_ff1", "w_ff2")
    gathered = _chip_exchange([weights[k].astype(BF16) for k in big], True, "gather_weights")
    gw = dict(zip(big, gathered))
    full = dict(w_in=_permute_w_in(_cols_from_chips(gw["w_in"])), w_ff1=_cols_from_chips(gw["w_ff1"]),
                w_oa=_rows_from_chips(gw["w_oa"]), w_ob=_rows_from_chips(gw["w_ob"]),
                w_out=_rows_from_chips(gw["w_out"]), w_ff2=_rows_from_chips(gw["w_ff2"]))

    def layer_params(l):
        p = {k: full[k][l] for k in big}
        p["mod"] = mod[l]
        p["conv_w"] = conv_full[l]
        for k in ("a_log", "dt_bias", "sinks", "dn_norm_w"):
            p[k] = weights[k][l]
        for k in ("ln1_g", "ln1_b", "b_ff1", "b_ff2", "ln2_g", "ln2_b"):
            p[k] = weights[k][l].reshape(1, -1)
        return p

    xs = x[0]
    saved = []
    for l in range(DEPTH):
        xs, sv = _layer_fwd(xs, layer_params(l))
        saved.append(sv)
    dy, loss_local = _loss_head(xs, loss_target[0])
    loss = lax.psum(loss_local[0, 0], ("x", "y", "c"))
    grads = [None] * DEPTH
    dx = dy
    for l in reversed(range(DEPTH)):
        dx, grads[l] = _layer_bwd(dx, layer_params(l), saved[l])
    grad_x = dx[None]
    gstack = {k: jnp.stack([grads[l][k] for l in range(DEPTH)]) for k in grads[0]}

    dmod = gstack["mod"].reshape(DEPTH, 6 * D_MODEL)
    small_g = dict(b_ada=dmod, a_log=gstack["a_log"], dt_bias=gstack["dt_bias"], sinks=gstack["sinks"],
                   dn_norm_w=gstack["dn_norm_w"], ln1_g=gstack["ln1_g"], ln1_b=gstack["ln1_b"], b_ff1=gstack["b_ff1"],
                   b_ff2=gstack["b_ff2"], ln2_g=gstack["ln2_g"], ln2_b=gstack["ln2_b"], conv_w=gstack["conv_w"])
    shapes = {k: weights[k].shape for k in _REPLICATED}
    shapes["conv_w"] = small_g["conv_w"].shape
    g_all = _all_gather8(_pack_small(small_g))
    no_conv = jnp.zeros(shapes["conv_w"], F32)
    small_out = _small_adam(g_all, _pack_small(dict(weights, conv_w=no_conv)), _pack_small(dict(mom_m, conv_w=no_conv)),
                            _pack_small(dict(mom_v, conv_w=no_conv)))
    small_res = [_unpack_small(t, shapes) for t in small_out]
    g_conv = lax.dynamic_slice_in_dim(small_res[0]["conv_w"], chip * n_cw, n_cw, axis=2)
    res = {"conv_w": _adam_call([g_conv], conv_w, m_conv_w, v_conv_w, "adam_conv_w", tile=16)}

    dmod_all = g_all.reshape(N_DEV, -1)[:, :DEPTH * 6 * D_MODEL].reshape(N_DEV, DEPTH, 6 * D_MODEL)
    dmod_shard = lax.dynamic_slice_in_dim(dmod_all, chip * n_ada, n_ada, axis=2).transpose(1, 0, 2)
    g_w_ada = _ada_bwd(c_all, dmod_shard)
    res["w_ada"] = _adam_call([g_w_ada], w_ada, m_w_ada, v_w_ada, "adam_w_ada")

    to_chips = dict(w_in=_cols_to_chips(_unpermute_w_in(gstack["w_in"])), w_ff1=_cols_to_chips(gstack["w_ff1"]),
                    w_oa=_rows_to_chips(gstack["w_oa"]), w_ob=_rows_to_chips(gstack["w_ob"]),
                    w_out=_rows_to_chips(gstack["w_out"]), w_ff2=_rows_to_chips(gstack["w_ff2"]))
    received = _chip_exchange([to_chips[k].astype(BF16) for k in big], False, "scatter_grads")
    partial = [_sum_slots(r.reshape(N_CHIPS, -1, r.shape[-1]), "sum_" + k) for k, r in zip(big, received)]
    theirs = _sibling_exchange(partial, "sibling_grads")
    for k, mine, other in zip(big, partial, theirs):
        shape = weights[k].shape
        res[k] = _adam_call([mine.reshape(shape), other.reshape(shape)], weights[k], mom_m[k], mom_v[k], "adam_" + k)
    for k in _REPLICATED:
        res[k] = [small_res[i][k] for i in range(4)]

    outs = [loss, grad_x]
    for i in range(4):
        outs += [res[k][i] for k in _WEIGHT_ORDER]
    return tuple(outs)
```

```python
import functools

import jax
import jax.numpy as jnp
from jax import lax
from jax.experimental import pallas as pl
from jax.experimental.pallas import tpu as pltpu

F32, BF16 = jnp.float32, jnp.bfloat16
HI = lax.Precision.HIGHEST
MESH = pl.DeviceIdType.MESH

D_MODEL = 1024
DEPTH = 4
ATT_KV_HEADS, ATT_GROUP, ATT_HEAD_DIM, WINDOW = 4, 4, 64, 128
DN_HEADS, DN_HEAD_DIM, CONV_K, CHUNK = 8, 128, 4, 64
D_FF = 4 * D_MODEL
D_IN = 7696
ALPHA = (2 * DEPTH) ** 0.25
LN_EPS = 1e-5
RMS_EPS = 1e-6
ADAM_LR, ADAM_B1, ADAM_B2, ADAM_EPS, ADAM_WD, ADAM_STEP = 0.001, 0.9, 0.999, 1e-08, 0.01, 10

N_CHIPS = 4
N_DEV = 8
LANES = 128
D_IN_P = 8192
C_Q, C_DQ, C_DK, C_DV, C_Z, C_GA, C_GB, C_K, C_V, C_BA = 0, 1024, 2048, 3072, 4096, 5120, 6144, 7168, 7424, 7680
NEG = -1e30
VMEM_LIMIT = 56 << 20


def _pc(body, **kw):
    return pl.pallas_call(body, **kw)


def _cparams(sem=None):
    if sem is None:
        return pltpu.CompilerParams(vmem_limit_bytes=VMEM_LIMIT)
    return pltpu.CompilerParams(vmem_limit_bytes=VMEM_LIMIT, dimension_semantics=sem)


def _mm(a, b, mode, out_dtype, name, tm=512, tn=512):
    if mode == "nn":
        (m, k), (_, n) = a.shape, b.shape
        dims = (((1,), (0,)), ((), ()))
    elif mode == "nt":
        (m, k), (n, _) = a.shape, b.shape
        dims = (((1,), (1,)), ((), ()))
    else:
        (k, m), (_, n) = a.shape, b.shape
        dims = (((0,), (0,)), ((), ()))
    tm, tn = min(tm, m), min(tn, n)
    assert m % tm == 0 and n % tn == 0, (name, m, n, tm, tn)
    a_spec = pl.BlockSpec((k, tm), lambda i, j: (0, i)) if mode == "tn" else pl.BlockSpec((tm, k), lambda i, j: (i, 0))
    b_spec = pl.BlockSpec((tn, k), lambda i, j: (j, 0)) if mode == "nt" else pl.BlockSpec((k, tn), lambda i, j: (0, j))

    def body(a_ref, b_ref, o_ref):
        o_ref[...] = lax.dot_general(a_ref[...], b_ref[...], dims, preferred_element_type=F32).astype(o_ref.dtype)

    return _pc(body, name=name, grid=(m // tm, n // tn), in_specs=[a_spec, b_spec],
               out_specs=pl.BlockSpec((tm, tn), lambda i, j: (i, j)),
               out_shape=jax.ShapeDtypeStruct((m, n), out_dtype), compiler_params=_cparams())(a, b)


def _row_specs(rows, tile):
    return [pl.BlockSpec((tile, w), functools.partial(lambda i, cb: (i, cb), cb=cb)) for (_, cb, w) in rows]


def _vec_specs(vecs):
    return [pl.BlockSpec(v.shape, lambda i: (0, 0)) for v in vecs]


def _rowwise(fn, rows, vecs, outs, name, tile=256):
    n = rows[0][0].shape[0]
    tile = min(tile, n)
    nr, nv = len(rows), len(vecs)

    def body(*refs):
        rv = [r[...].astype(F32) for r in refs[:nr]]
        vv = [r[...] for r in refs[nr:nr + nv]]
        for o_ref, val in zip(refs[nr + nv:], fn(*rv, *vv)):
            o_ref[...] = val.astype(o_ref.dtype)

    res = _pc(body, name=name, grid=(n // tile,), in_specs=_row_specs(rows, tile) + _vec_specs(vecs),
              out_specs=[pl.BlockSpec((tile, w), lambda i: (i, 0)) for (w, _) in outs],
              out_shape=[jax.ShapeDtypeStruct((n, w), dt) for (w, dt) in outs],
              compiler_params=_cparams())(*[r[0] for r in rows], *vecs)
    return res


def _rowwise_bwd(fn, rows, vecs, cts, row_dtypes, name, tile=256, add=None):
    n = rows[0][0].shape[0]
    tile = min(tile, n)
    nr, nv, nc = len(rows), len(vecs), len(cts)
    want = [i for i, dt in enumerate(row_dtypes) if dt is not None]
    n_add = 0 if add is None else 1

    def body(*refs):
        rv = [r[...].astype(F32) for r in refs[:nr]]
        vv = [r[...] for r in refs[nr:nr + nv]]
        cv = [r[...].astype(F32) for r in refs[nr + nv:nr + nv + nc]]
        pos = nr + nv + nc
        add_ref = refs[pos] if n_add else None
        pos += n_add
        row_out = refs[pos:pos + len(want)]
        vec_out = refs[pos + len(want):]
        _, vjp = jax.vjp(fn, *rv, *vv)
        grads = vjp(tuple(cv))
        for o_ref, i in zip(row_out, want):
            gval = grads[i]
            if n_add and add[0] == i:
                gval = gval + add_ref[...]
            o_ref[...] = gval.astype(o_ref.dtype)

        @pl.when(pl.program_id(0) == 0)
        def _():
            for o_ref in vec_out:
                o_ref[...] = jnp.zeros_like(o_ref)

        for o_ref, gval in zip(vec_out, grads[nr:]):
            o_ref[...] += gval

    ct_rows = [(c, 0, c.shape[1]) for c in cts]
    add_rows = [(add[1], 0, add[1].shape[1])] if n_add else []
    res = _pc(body, name=name, grid=(n // tile,),
              in_specs=_row_specs(rows, tile) + _vec_specs(vecs) + _row_specs(ct_rows + add_rows, tile),
              out_specs=[pl.BlockSpec((tile, rows[i][2]), lambda i_: (i_, 0)) for i in want] + _vec_specs(vecs),
              out_shape=[jax.ShapeDtypeStruct((n, rows[i][2]), row_dtypes[i]) for i in want]
              + [jax.ShapeDtypeStruct(v.shape, F32) for v in vecs],
              compiler_params=_cparams(("arbitrary",)))(*[r[0] for r in rows], *vecs, *cts, *[a[0] for a in add_rows])
    return res[:len(want)], res[len(want):]


def _whole(a, cb=0, w=None):
    return (a, cb, a.shape[1] if w is None else w)


def _ln(x, g, b):
    mu = jnp.mean(x, axis=-1, keepdims=True)
    var = jnp.mean(jnp.square(x - mu), axis=-1, keepdims=True)
    return (x - mu) * lax.rsqrt(var + LN_EPS) * g + b


def _silu(x):
    return x * jax.nn.sigmoid(x)


def _softplus(x):
    return jnp.maximum(x, 0.0) + jnp.log(1.0 + jnp.exp(-jnp.abs(x)))


def _f_mod(x, sc, sh):
    return (x * (1.0 + sc) + sh,)


def _f_gate(ga, gb, ya, yb):
    return (jax.nn.sigmoid(ga) * ya + jax.nn.sigmoid(gb) * yb,)


def _f_post1(x, mixed, gt, g1, b1, sc2, sh2):
    x1 = _ln(ALPHA * x + (1.0 + gt) * mixed, g1, b1)
    return x1, x1 * (1.0 + sc2) + sh2


def _f_act(hpre, b):
    return (jnp.square(jnp.maximum(hpre + b, 0.0)),)


def _f_post2(x1, ff, gt, bff2, g2, b2):
    return (_ln(ALPHA * x1 + (1.0 + gt) * (ff + bff2), g2, b2),)


def _attn_valid(n):
    qi = lax.broadcasted_iota(jnp.int32, (WINDOW, 2 * WINDOW), 0)
    si = lax.broadcasted_iota(jnp.int32, (WINDOW, 2 * WINDOW), 1)
    diff = qi + WINDOW - si
    return (diff >= 0) & (diff < WINDOW) & (n * WINDOW + si - WINDOW >= 0)


def _attn_block(q4, kp, kc, vp, vc, sk4, valid):
    kband = jnp.concatenate([kp, kc], axis=0).astype(BF16)
    vband = jnp.concatenate([vp, vc], axis=0).astype(BF16)
    outs = []
    for g in range(ATT_GROUP):
        s = lax.dot_general(q4[g].astype(BF16), kband, (((1,), (1,)), ((), ())), preferred_element_type=F32)
        s = jnp.where(valid, s * (ATT_HEAD_DIM ** -0.5), NEG)
        sink = sk4[g]
        m = lax.stop_gradient(jnp.maximum(jnp.max(s, axis=-1, keepdims=True), sink))
        p = jnp.exp(s - m)
        denom = jnp.sum(p, axis=-1, keepdims=True) + jnp.exp(sink - m)
        probs = (p / denom).astype(BF16)
        outs.append(jnp.dot(probs, vband, preferred_element_type=F32))
    return jnp.stack(outs)


def _attn_specs(s):
    nb = s // WINDOW
    q_spec = pl.BlockSpec((1, ATT_GROUP, WINDOW, ATT_HEAD_DIM), lambda h, n: (h, 0, n, 0))
    prev = pl.BlockSpec((1, WINDOW, ATT_HEAD_DIM), lambda h, n: (h, jnp.maximum(n - 1, 0), 0))
    cur = pl.BlockSpec((1, WINDOW, ATT_HEAD_DIM), lambda h, n: (h, n, 0))
    sk = pl.BlockSpec((1, ATT_GROUP, 1, 1), lambda h, n: (h, 0, 0, 0))
    return nb, q_spec, prev, cur, sk


def _attn_fwd(qh, kh, vh, sinks4):
    s = qh.shape[2]
    nb, q_spec, prev, cur, sk = _attn_specs(s)

    def body(q_ref, kp_ref, kc_ref, vp_ref, vc_ref, sk_ref, o_ref):
        valid = _attn_valid(pl.program_id(1))
        o = _attn_block(q_ref[0], kp_ref[0], kc_ref[0], vp_ref[0], vc_ref[0], sk_ref[0], valid)
        o_ref[0] = o.astype(o_ref.dtype)

    return _pc(body, name="attn_fwd", grid=(ATT_KV_HEADS, nb), in_specs=[q_spec, prev, cur, prev, cur, sk],
               out_specs=q_spec, out_shape=jax.ShapeDtypeStruct(qh.shape, BF16), compiler_params=_cparams())(
                   qh, kh, kh, vh, vh, sinks4)


def _attn_bwd(qh, kh, vh, sinks4, doh):
    s = qh.shape[2]
    nb, q_spec, prev, cur, sk = _attn_specs(s)
    acc = pl.BlockSpec((1, s + WINDOW, ATT_HEAD_DIM), lambda h, n: (h, 0, 0))

    def body(q_ref, kp_ref, kc_ref, vp_ref, vc_ref, sk_ref, do_ref, dq_ref, dk_ref, dv_ref, dsk_ref):
        n = pl.program_id(1)
        valid = _attn_valid(n)
        fn = functools.partial(_attn_block, valid=valid)
        _, vjp = jax.vjp(fn, q_ref[0], kp_ref[0], kc_ref[0], vp_ref[0], vc_ref[0], sk_ref[0])
        dq, dkp, dkc, dvp, dvc, dsk = vjp(do_ref[0].astype(F32))
        dq_ref[0] = dq

        @pl.when(n == 0)
        def _():
            dk_ref[...] = jnp.zeros_like(dk_ref)
            dv_ref[...] = jnp.zeros_like(dv_ref)
            dsk_ref[...] = jnp.zeros_like(dsk_ref)

        band = pl.ds(pl.multiple_of(n * WINDOW, WINDOW), 2 * WINDOW)
        dk_ref[0, band, :] += jnp.concatenate([dkp, dkc], axis=0)
        dv_ref[0, band, :] += jnp.concatenate([dvp, dvc], axis=0)
        dsk_ref[0] += dsk

    kv_shape = jax.ShapeDtypeStruct((ATT_KV_HEADS, s + WINDOW, ATT_HEAD_DIM), F32)
    return _pc(body, name="attn_bwd", grid=(ATT_KV_HEADS, nb), in_specs=[q_spec, prev, cur, prev, cur, sk, q_spec],
               out_specs=[q_spec, acc, acc, sk],
               out_shape=[jax.ShapeDtypeStruct(qh.shape, F32), kv_shape, kv_shape, jax.ShapeDtypeStruct(sinks4.shape, F32)],
               compiler_params=_cparams(("arbitrary", "arbitrary")))(qh, kh, kh, vh, vh, sinks4, doh)


def _bdot(a, b, dims=(((1,), (0,)), ((), ()))):
    return lax.dot_general(a.astype(BF16), b.astype(BF16), dims, preferred_element_type=F32)


def _hdot(a, b, dims=(((1,), (0,)), ((), ()))):
    return lax.dot_general(a, b, dims, precision=HI, preferred_element_type=F32)


_NT = (((1,), (1,)), ((), ()))
_TN = (((0,), (0,)), ((), ()))


def _chunk_masks():
    r = lax.broadcasted_iota(jnp.int32, (CHUNK, CHUNK), 0)
    c = lax.broadcasted_iota(jnp.int32, (CHUNK, CHUNK), 1)
    return r >= c, r > c, (r == c).astype(F32)


def _dn_local(qs, ks, vs, bs, gs, masks):
    causal, strict, eye = masks
    rng = range(len(qs))
    gb = [jnp.broadcast_to(gs[i], (CHUNK, CHUNK)) for i in rng]
    decay = [jnp.exp(jnp.where(causal, gb[i] - gb[i].T, NEG)) for i in rng]
    kb = [ks[i] * bs[i] for i in rng]
    vb = [vs[i] * bs[i] for i in rng]
    kk = [_bdot(kb[i], ks[i], _NT) for i in rng]
    p = [-jnp.where(strict, kk[i] * decay[i], 0.0) for i in rng]
    t = [eye + p[i] for i in rng]
    for _ in range(5):
        p = [_hdot(p[i], p[i]) for i in rng]
        t = [t[i] + _hdot(p[i], t[i]) for i in rng]
    eg = [jnp.exp(gs[i]) for i in rng]
    u = [_hdot(t[i], vb[i]) for i in rng]
    w = [_hdot(t[i], kb[i] * eg[i]) for i in rng]
    qk = [_bdot(qs[i], ks[i], _NT) for i in rng]
    intra = [qk[i] * decay[i] for i in rng]
    q_dec = [qs[i] * eg[i] for i in rng]
    k_dec = [ks[i] * jnp.exp(gs[i][CHUNK - 1:CHUNK, :] - gs[i]) for i in rng]
    return u, w, intra, q_dec, k_dec


def _dn_state(u, w, intra, q_dec, k_dec, gcum, state):
    v_new = u - _bdot(w, state)
    o = _bdot(q_dec, state) + _bdot(intra, v_new)
    new_state = state * jnp.exp(gcum[CHUNK - 1:CHUNK, :]) + _bdot(k_dec, v_new, _TN)
    return o, new_state


def _l2norm(t):
    return t * lax.rsqrt(jnp.sum(jnp.square(t), axis=-1, keepdims=True) + RMS_EPS)


def _dn_pre(aq, ak, av, ba, alog, dtb, h):
    lane = lax.broadcasted_iota(jnp.int32, (1, LANES), 1)
    pick = lambda t, i: jnp.sum(jnp.where(lane == i, t, 0.0), axis=1, keepdims=True)
    q = _l2norm(_silu(aq)) * (DN_HEAD_DIM ** -0.5)
    k = _l2norm(_silu(ak))
    v = _silu(av)
    beta = jax.nn.sigmoid(pick(ba, h))
    g = -jnp.exp(pick(alog, h)) * _softplus(pick(ba, h + DN_HEADS) + pick(dtb, h))
    return q, k, v, beta, g


def _dn_post(o, z, nw):
    o = o * lax.rsqrt(jnp.mean(jnp.square(o), axis=-1, keepdims=True) + RMS_EPS) * nw
    return o * _silu(z)


_PAD = 8
_TOK_TILE = 512


def _pad_front(pad_ref, x_ref, s):
    pad_ref[pl.ds(0, _PAD), :] = jnp.zeros((_PAD, pad_ref.shape[1]), F32)
    pad_ref[pl.ds(_PAD, s), :] = x_ref[...]


def _conv_tile(pad_ref, w4, r0, n):
    acc = None
    for j in range(CONV_K):
        term = pad_ref[pl.ds(r0 + _PAD - (CONV_K - 1) + j, n), :] * w4[j:j + 1, :]
        acc = term if acc is None else acc + term
    return acc


def _conv_tile_bwd(pad_ref, da_ref, w4, r0, n):
    dx, dw = None, []
    da = da_ref[pl.ds(r0, n), :]
    for j in range(CONV_K):
        term = da_ref[pl.ds(r0 + CONV_K - 1 - j, n), :] * w4[j:j + 1, :]
        dx = term if dx is None else dx + term
        dw.append(jnp.sum(da * pad_ref[pl.ds(r0 + _PAD - (CONV_K - 1) + j, n), :], axis=0, keepdims=True))
    return dx, jnp.concatenate(dw, axis=0)


def _dn_gcum(g_c, causal_f):
    return _hdot(causal_f, jnp.broadcast_to(g_c, (CHUNK, LANES)))[:, 0:1]


def _dn_in_specs(s):
    col = lambda base: pl.BlockSpec((s, DN_HEAD_DIM), functools.partial(lambda h, b: (0, b + h), b=base // DN_HEAD_DIM))
    cw = lambda base: pl.BlockSpec((CONV_K, DN_HEAD_DIM), functools.partial(lambda h, b: (0, b + h), b=base))
    row = pl.BlockSpec((1, LANES), lambda h: (0, 0))
    ba = pl.BlockSpec((s, LANES), lambda h: (0, C_BA // LANES))
    return [col(C_DQ), col(C_DK), col(C_DV), col(C_Z), ba, cw(0), cw(DN_HEADS), cw(2 * DN_HEADS), row, row, row]


def _chunk_rows(c):
    return pl.ds(pl.multiple_of(c * CHUNK, CHUNK), CHUNK)


def _group(nchunk, want):
    g = min(want, nchunk)
    assert nchunk % g == 0
    return g


def _dn_forward_scan(q_s, k_s, v_s, b_s, g_s, gc_s, loc, o_s, states_ref, s):
    masks = _chunk_masks()
    causal_f = masks[0].astype(F32)
    nchunk = s // CHUNK
    grp = _group(nchunk, 4)
    u_s, w_s, in_s, qd_s, kd_s = loc

    def local_step(i, carry):
        rows = [_chunk_rows(i * grp + j) for j in range(grp)]
        gcum = [_dn_gcum(g_s[r, :], causal_f) for r in rows]
        u, w, intra, q_dec, k_dec = _dn_local([q_s[r, :] for r in rows], [k_s[r, :] for r in rows],
                                              [v_s[r, :] for r in rows], [b_s[r, :] for r in rows], gcum, masks)
        for j, r in enumerate(rows):
            gc_s[r, :] = gcum[j]
            u_s[r, :] = u[j]
            w_s[r, :] = w[j].astype(w_s.dtype)
            in_s[r, :] = intra[j].astype(in_s.dtype)
            qd_s[r, :] = q_dec[j].astype(qd_s.dtype)
            kd_s[r, :] = k_dec[j].astype(kd_s.dtype)
        return carry

    lax.fori_loop(0, nchunk // grp, local_step, 0)

    def state_step(i, state):
        rows = _chunk_rows(i)
        if states_ref is not None:
            states_ref[i] = state
        o, state = _dn_state(u_s[rows, :], w_s[rows, :], in_s[rows, :], qd_s[rows, :], kd_s[rows, :], gc_s[rows, :], state)
        o_s[rows, :] = o
        return state

    lax.fori_loop(0, nchunk, state_step, jnp.zeros((DN_HEAD_DIM, DN_HEAD_DIM), F32))


def _dn_local_scratch(s):
    d = DN_HEAD_DIM
    return [pltpu.VMEM((s, d), F32), pltpu.VMEM((s, d), BF16), pltpu.VMEM((s, CHUNK), BF16), pltpu.VMEM((s, d), BF16),
            pltpu.VMEM((s, d), BF16)]


def _call_with_exchange(body, name, steps, in_specs, out_specs, out_shape, scratch, args, exchange):
    if exchange is None:
        res = _pc(body, name=name, grid=(steps,), in_specs=in_specs, out_specs=out_specs, out_shape=out_shape,
                  scratch_shapes=scratch, compiler_params=_cparams(("arbitrary",)))(*args)
        return res, None
    arrays, gather = exchange
    x_in, x_out, x_shape, x_scratch = _exchange_specs(arrays, gather)
    wrapped = _carry_exchange(body, len(in_specs), len(out_specs), len(scratch), len(arrays), gather, steps)
    res = _pc(wrapped, name=name + "_x", grid=(steps,), in_specs=in_specs + x_in, out_specs=out_specs + x_out,
              out_shape=out_shape + x_shape, scratch_shapes=scratch + x_scratch,
              compiler_params=_cparams(("arbitrary",)))(*args, *arrays)
    return res[:len(out_specs)], res[len(out_specs):]


def _dn_fwd(proj, conv_w, alog, dtb, nw, exchange=None):
    s = proj.shape[0]
    d = DN_HEAD_DIM

    tt = min(_TOK_TILE, s)

    def body(xq, xk, xv, z, ba, wq, wk, wv, alog_r, dtb_r, nw_r, o_ref,
             padq, padk, padv, q_s, k_s, v_s, b_s, g_s, gc_s, o_s, *loc):
        h = pl.program_id(0)
        _pad_front(padq, xq, s)
        _pad_front(padk, xk, s)
        _pad_front(padv, xv, s)
        for r0 in range(0, s, tt):
            rows = pl.ds(r0, tt)
            aq, ak, av = _conv_tile(padq, wq[...], r0, tt), _conv_tile(padk, wk[...], r0, tt), _conv_tile(padv, wv[...], r0, tt)
            q_s[rows, :], k_s[rows, :], v_s[rows, :], b_s[rows, :], g_s[rows, :] = _dn_pre(
                aq, ak, av, ba[rows, :], alog_r[...], dtb_r[...], h)
        _dn_forward_scan(q_s, k_s, v_s, b_s, g_s, gc_s, loc, o_s, None, s)
        for r0 in range(0, s, tt):
            rows = pl.ds(r0, tt)
            o_ref[rows, :] = _dn_post(o_s[rows, :], z[rows, :], nw_r[...]).astype(o_ref.dtype)

    big = pltpu.VMEM((s, d), F32)
    thin = pltpu.VMEM((s, 1), F32)
    padded = pltpu.VMEM((s + _PAD, d), F32)
    return _call_with_exchange(
        body, "dn_fwd", DN_HEADS, _dn_in_specs(s), [pl.BlockSpec((s, d), lambda h: (0, h))],
        [jax.ShapeDtypeStruct((s, DN_HEADS * d), BF16)],
        [padded, padded, padded, big, big, big, thin, thin, thin, big] + _dn_local_scratch(s),
        (proj, proj, proj, proj, proj, conv_w, conv_w, conv_w, alog, dtb, nw), exchange)


def _dn_bwd(proj, conv_w, alog, dtb, nw, dob, exchange=None):
    s = proj.shape[0]
    d = DN_HEAD_DIM
    nchunk = s // CHUNK

    tt = min(_TOK_TILE, s)

    def body(xq, xk, xv, z, ba, wq, wk, wv, alog_r, dtb_r, nw_r, dob_ref,
             dxq, dxk, dxv, dz, dba, dwq, dwk, dwv, dalog, ddtb, dnw,
             padq, padk, padv, q_s, k_s, v_s, b_s, g_s, gc_s, o_s, states, dq_s, dk_s, dv_s, db_s, dg_s,
             dkd_s, din_s, dgc_s, *loc):
        h = pl.program_id(0)
        masks = _chunk_masks()
        causal_f = masks[0].astype(F32)
        pre = functools.partial(_dn_pre, h=h)
        _pad_front(padq, xq, s)
        _pad_front(padk, xk, s)
        _pad_front(padv, xv, s)

        def conv_tiles(r0):
            return _conv_tile(padq, wq[...], r0, tt), _conv_tile(padk, wk[...], r0, tt), _conv_tile(padv, wv[...], r0, tt)

        for r0 in range(0, s, tt):
            rows = pl.ds(r0, tt)
            q_s[rows, :], k_s[rows, :], v_s[rows, :], b_s[rows, :], g_s[rows, :] = pre(
                *conv_tiles(r0), ba[rows, :], alog_r[...], dtb_r[...])
        _dn_forward_scan(q_s, k_s, v_s, b_s, g_s, gc_s, loc, o_s, states, s)
        u_s, w_s, in_s, qd_s, kd_s = loc
        dnw_v = jnp.zeros((1, LANES), F32)
        for r0 in range(0, s, tt):
            rows = pl.ds(r0, tt)
            _, post_vjp = jax.vjp(_dn_post, o_s[rows, :], z[rows, :], nw_r[...])
            do_raw, dz_v, dnw_t = post_vjp(dob_ref[rows, :].astype(F32))
            dz[rows, :] = dz_v.astype(dz.dtype)
            o_s[rows, :] = do_raw
            dnw_v = dnw_v + dnw_t

        def state_step(i, dstate):
            c = nchunk - 1 - i
            rows = _chunk_rows(c)
            _, vjp = jax.vjp(_dn_state, u_s[rows, :], w_s[rows, :].astype(F32), in_s[rows, :].astype(F32),
                             qd_s[rows, :].astype(F32), kd_s[rows, :].astype(F32), gc_s[rows, :], states[c])
            du, dw, din, dqd, dkd, dgc, dstate = vjp((o_s[rows, :], dstate))
            dq_s[rows, :] = du
            dk_s[rows, :] = dw
            dv_s[rows, :] = dqd
            dkd_s[rows, :] = dkd
            din_s[rows, :] = din
            dgc_s[rows, :] = dgc
            return dstate

        lax.fori_loop(0, nchunk, state_step, jnp.zeros((d, d), F32))
        local = functools.partial(_dn_local, masks=masks)
        grp = _group(nchunk, 4)

        def local_step(i, carry):
            rows = [_chunk_rows(i * grp + j) for j in range(grp)]
            get = lambda ref: [ref[r, :] for r in rows]
            _, vjp = jax.vjp(local, get(q_s), get(k_s), get(v_s), get(b_s), get(gc_s))
            dq_c, dk_c, dv_c, db_c, dgc_c = vjp((get(dq_s), get(dk_s), get(din_s), get(dv_s), get(dkd_s)))
            dgc_c = [dgc_c[j] + dgc_s[r, :] for j, r in enumerate(rows)]
            dg_c = [_hdot(causal_f, jnp.broadcast_to(t, (CHUNK, LANES)), _TN)[:, 0:1] for t in dgc_c]
            for j, r in enumerate(rows):
                dq_s[r, :] = dq_c[j]
                dk_s[r, :] = dk_c[j]
                dv_s[r, :] = dv_c[j]
                db_s[r, :] = db_c[j]
                dg_s[r, :] = dg_c[j]
            return carry

        lax.fori_loop(0, nchunk // grp, local_step, 0)

        @pl.when(h == 0)
        def _():
            dba[...] = jnp.zeros_like(dba)
            dalog[...] = jnp.zeros_like(dalog)
            ddtb[...] = jnp.zeros_like(ddtb)
            dnw[...] = jnp.zeros_like(dnw)

        dalog_v = jnp.zeros((1, LANES), F32)
        ddtb_v = jnp.zeros((1, LANES), F32)
        for r0 in range(0, s, tt):
            rows = pl.ds(r0, tt)
            _, pre_vjp = jax.vjp(pre, *conv_tiles(r0), ba[rows, :], alog_r[...], dtb_r[...])
            daq, dak, dav, dba_t, dalog_t, ddtb_t = pre_vjp(
                (dq_s[rows, :], dk_s[rows, :], dv_s[rows, :], db_s[rows, :], dg_s[rows, :]))
            dq_s[rows, :], dk_s[rows, :], dv_s[rows, :] = daq, dak, dav
            dba[rows, :] += dba_t
            dalog_v = dalog_v + dalog_t
            ddtb_v = ddtb_v + ddtb_t
        tail = pl.ds(s, _PAD)
        dq_s[tail, :] = dk_s[tail, :] = dv_s[tail, :] = jnp.zeros((_PAD, d), F32)
        for pad, da_s, w_ref, dx_ref, dw_ref in ((padq, dq_s, wq, dxq, dwq), (padk, dk_s, wk, dxk, dwk), (padv, dv_s, wv, dxv, dwv)):
            dw_acc = jnp.zeros((CONV_K, d), F32)
            for r0 in range(0, s, tt):
                dx_t, dw_t = _conv_tile_bwd(pad, da_s, w_ref[...], r0, tt)
                dx_ref[pl.ds(r0, tt), :] = dx_t.astype(dx_ref.dtype)
                dw_acc = dw_acc + dw_t
            dw_ref[...] = dw_acc
        dalog[...] += dalog_v
        ddtb[...] += ddtb_v
        dnw[...] += dnw_v

    big = pltpu.VMEM((s, d), F32)
    thin = pltpu.VMEM((s, 1), F32)
    padded = pltpu.VMEM((s + _PAD, d), F32)
    w_all = DN_HEADS * d
    col_out = lambda: pl.BlockSpec((s, d), lambda h: (0, h))
    cw_out = lambda: pl.BlockSpec((CONV_K, d), lambda h: (0, h))
    row = lambda: pl.BlockSpec((1, LANES), lambda h: (0, 0))
    big_out = jax.ShapeDtypeStruct((s, w_all), BF16)
    cw_shape = jax.ShapeDtypeStruct((CONV_K, w_all), F32)
    row_shape = jax.ShapeDtypeStruct((1, LANES), F32)
    return _call_with_exchange(
        body, "dn_bwd", DN_HEADS, _dn_in_specs(s) + [pl.BlockSpec((s, d), lambda h: (0, h))],
        [col_out(), col_out(), col_out(), col_out(), pl.BlockSpec((s, LANES), lambda h: (0, 0)),
         cw_out(), cw_out(), cw_out(), row(), row(), row()],
        [big_out, big_out, big_out, big_out, jax.ShapeDtypeStruct((s, LANES), F32),
         cw_shape, cw_shape, cw_shape, row_shape, row_shape, row_shape],
        [padded, padded, padded, big, big, big, thin, thin, thin, big,
         pltpu.VMEM((nchunk, d, d), F32), padded, padded, padded, thin, thin,
         big, pltpu.VMEM((s, CHUNK), F32), thin] + _dn_local_scratch(s),
        (proj, proj, proj, proj, proj, conv_w, conv_w, conv_w, alog, dtb, nw, dob), exchange)


def _loss_head(y, target, tile=256):
    n, dm = y.shape
    tile = min(tile, n)

    def body(y_ref, t_ref, dy_ref, loss_ref):
        err = y_ref[...] - t_ref[...]
        dy_ref[...] = err * (1.0 / dm)

        @pl.when(pl.program_id(0) == 0)
        def _():
            loss_ref[...] = jnp.zeros_like(loss_ref)

        loss_ref[...] += 0.5 * jnp.sum(jnp.mean(jnp.square(err), axis=-1, keepdims=True), axis=0, keepdims=True)

    blk = pl.BlockSpec((tile, dm), lambda i: (i, 0))
    return _pc(body, name="loss_head", grid=(n // tile,), in_specs=[blk, blk],
               out_specs=[blk, pl.BlockSpec((1, 1), lambda i: (0, 0))],
               out_shape=[jax.ShapeDtypeStruct((n, dm), F32), jax.ShapeDtypeStruct((1, 1), F32)],
               compiler_params=_cparams(("arbitrary",)))(y, target)


def _ada_fwd(c_all, w_ada, b_shard):
    nl, dm, n = w_ada.shape

    def body(c_ref, w_ref, b_ref, o_ref):
        ca = _silu(c_ref[...]).astype(BF16)
        o_ref[0] = jnp.dot(ca, w_ref[0].astype(BF16), preferred_element_type=F32) + b_ref[0]

    return _pc(body, name="ada_fwd", grid=(nl,),
               in_specs=[pl.BlockSpec((N_DEV, dm), lambda l: (0, 0)), pl.BlockSpec((1, dm, n), lambda l: (l, 0, 0)),
                         pl.BlockSpec((1, 1, n), lambda l: (l, 0, 0))],
               out_specs=pl.BlockSpec((1, N_DEV, n), lambda l: (l, 0, 0)),
               out_shape=jax.ShapeDtypeStruct((nl, N_DEV, n), F32), compiler_params=_cparams())(c_all, w_ada, b_shard)


def _ada_bwd(c_all, dmod):
    nl, _, n = dmod.shape
    dm = c_all.shape[1]

    def body(c_ref, d_ref, o_ref):
        o_ref[0] = _hdot(_silu(c_ref[...]), d_ref[0], _TN)

    return _pc(body, name="ada_bwd", grid=(nl,),
               in_specs=[pl.BlockSpec((N_DEV, dm), lambda l: (0, 0)), pl.BlockSpec((1, N_DEV, n), lambda l: (l, 0, 0))],
               out_specs=pl.BlockSpec((1, dm, n), lambda l: (l, 0, 0)),
               out_shape=jax.ShapeDtypeStruct((nl, dm, n), F32), compiler_params=_cparams())(c_all, dmod)


def _adamw(g, w, m, v):
    m = ADAM_B1 * m + (1.0 - ADAM_B1) * g
    v = ADAM_B2 * v + (1.0 - ADAM_B2) * jnp.square(g)
    m_hat = m / (1.0 - ADAM_B1 ** ADAM_STEP)
    v_hat = v / (1.0 - ADAM_B2 ** ADAM_STEP)
    delta = -ADAM_LR * (m_hat / (jnp.sqrt(v_hat) + ADAM_EPS) + ADAM_WD * w)
    return delta, m, v


def _adam_call(parts, w, m, v, name, tile=128):
    shape = w.shape
    flat = lambda t: t.reshape(-1, shape[-1])
    width = shape[-1]

    def fn(*vals):
        g = vals[0] if len(parts) == 1 else vals[0] + vals[1]
        return (g,) + _adamw(g, *vals[len(parts):])

    rows = [_whole(flat(t)) for t in (*parts, w, m, v)]
    outs = _rowwise(fn, rows, [], [(width, F32)] * 4, name, tile=tile)
    return [o.reshape(shape) for o in outs]


def _sum_slots(r, name, tile=128):
    _, n, width = r.shape
    tile = min(tile, n)

    def body(r_ref, o_ref):
        acc = r_ref[0].astype(F32)
        for j in range(1, N_CHIPS):
            acc = acc + r_ref[j].astype(F32)
        o_ref[...] = acc

    return _pc(body, name=name, grid=(n // tile,), in_specs=[pl.BlockSpec((N_CHIPS, tile, width), lambda i: (0, i, 0))],
               out_specs=pl.BlockSpec((tile, width), lambda i: (i, 0)),
               out_shape=jax.ShapeDtypeStruct((n, width), F32), compiler_params=_cparams())(r)


def _small_adam(g_all, w, m, v):
    def body(g_ref, w_ref, m_ref, v_ref, og, od, om, ov):
        g = g_ref[0]
        for j in range(1, N_DEV):
            g = g + g_ref[j]
        og[...] = g
        od[...], om[...], ov[...] = _adamw(g, w_ref[...], m_ref[...], v_ref[...])

    vm = pl.BlockSpec(memory_space=pltpu.VMEM)
    shp = jax.ShapeDtypeStruct(w.shape, F32)
    return _pc(body, name="small_adam", in_specs=[vm] * 4, out_specs=[vm] * 4, out_shape=[shp] * 4,
               compiler_params=_cparams())(g_all, w, m, v)


def _place():
    return lax.axis_index("x"), lax.axis_index("y"), lax.axis_index("c")


def _flip(v, bit):
    return 1 - v if bit else v


def _all_gather8(a):
    r, n = a.shape

    def body(a_ref, o_ref, send_sems, recv_sems):
        x, y, c = _place()
        me = 4 * x + 2 * y + c
        o_ref[me] = a_ref[...]
        copies = []
        for k in range(1, N_DEV):
            peer = (_flip(x, k & 4), _flip(y, k & 2), _flip(c, k & 1))
            copies.append(pltpu.make_async_remote_copy(
                src_ref=a_ref, dst_ref=o_ref.at[me], send_sem=send_sems.at[k - 1], recv_sem=recv_sems.at[k - 1],
                device_id=peer, device_id_type=MESH))
        for cp in copies:
            cp.start()
        for k in range(1, N_DEV):
            px, py, pc_ = _flip(x, k & 4), _flip(y, k & 2), _flip(c, k & 1)
            pltpu.make_async_remote_copy(
                src_ref=a_ref, dst_ref=o_ref.at[4 * px + 2 * py + pc_], send_sem=send_sems.at[k - 1],
                recv_sem=recv_sems.at[k - 1], device_id=(px, py, pc_), device_id_type=MESH).wait_recv()
        for cp in copies:
            cp.wait_send()

    vm = pl.BlockSpec(memory_space=pltpu.VMEM)
    return _pc(body, name="all_gather8", in_specs=[vm], out_specs=vm,
               out_shape=jax.ShapeDtypeStruct((N_DEV, r, n), a.dtype),
               scratch_shapes=[pltpu.SemaphoreType.DMA((N_DEV - 1,)), pltpu.SemaphoreType.DMA((N_DEV - 1,))],
               compiler_params=_cparams())(a)


def _chip_exchange(arrays, gather, name):
    na = len(arrays)

    def body(*refs):
        ins, outs, sems = refs[:na], refs[na:2 * na], refs[2 * na:]
        _exchange_copies(ins, outs, sems, gather, start=True)
        _exchange_copies(ins, outs, sems, gather, start=False)

    in_specs, out_specs, out_shape, scratch = _exchange_specs(arrays, gather)
    return _pc(body, name=name, in_specs=in_specs, out_specs=out_specs, out_shape=out_shape, scratch_shapes=scratch,
               compiler_params=_cparams())(*arrays)


def _exchange_specs(arrays, gather):
    na = len(arrays)
    hbm = pl.BlockSpec(memory_space=pl.ANY)
    out_shape = [jax.ShapeDtypeStruct(((N_CHIPS,) + a.shape) if gather else a.shape, a.dtype) for a in arrays]
    scratch = [pltpu.SemaphoreType.DMA((3 * na,)), pltpu.SemaphoreType.DMA((3 * na,)), pltpu.SemaphoreType.DMA((na,))]
    return [hbm] * na, [hbm] * na, out_shape, scratch


def _exchange_copies(ins, outs, sems, gather, start):
    send_sems, recv_sems, local_sems = sems
    x, y, c = _place()
    me = 2 * x + y
    sends, arrivals, locals_ = [], [], []
    for i in range(len(ins)):
        locals_.append(pltpu.make_async_copy(ins[i] if gather else ins[i].at[me], outs[i].at[me], local_sems.at[i]))
        for j in range(1, N_CHIPS):
            px, py = _flip(x, j & 2), _flip(y, j & 1)
            peer = 2 * px + py
            pair = dict(send_sem=send_sems.at[i * 3 + j - 1], recv_sem=recv_sems.at[i * 3 + j - 1],
                        device_id=(px, py, c), device_id_type=MESH)
            sends.append(pltpu.make_async_remote_copy(
                src_ref=ins[i] if gather else ins[i].at[peer], dst_ref=outs[i].at[me], **pair))
            arrivals.append(pltpu.make_async_remote_copy(
                src_ref=ins[i] if gather else ins[i].at[me], dst_ref=outs[i].at[peer], **pair))
    if start:
        for cp in locals_ + sends:
            cp.start()
    else:
        for cp in arrivals:
            cp.wait_recv()
        for cp in sends:
            cp.wait_send()
        for cp in locals_:
            cp.wait()


def _carry_exchange(body, n_in, n_out, n_scratch, n_arrays, gather, steps):
    def wrapped(*refs):
        na = n_arrays
        ins, xin = refs[:n_in], refs[n_in:n_in + na]
        outs = refs[n_in + na:n_in + na + n_out]
        xout = refs[n_in + na + n_out:n_in + 2 * na + n_out]
        rest = refs[n_in + 2 * na + n_out:]
        scratch, sems = rest[:n_scratch], rest[n_scratch:]
        step = pl.program_id(0)

        @pl.when(step == 0)
        def _():
            _exchange_copies(xin, xout, sems, gather, start=True)

        body(*ins, *outs, *scratch)

        @pl.when(step == steps - 1)
        def _():
            _exchange_copies(xin, xout, sems, gather, start=False)

    return wrapped


def _sibling_exchange(arrays, name):
    na = len(arrays)

    def body(*refs):
        ins, outs = refs[:na], refs[na:2 * na]
        send_sems, recv_sems = refs[2 * na:]
        x, y, c = _place()
        copies = [pltpu.make_async_remote_copy(
            src_ref=ins[i], dst_ref=outs[i], send_sem=send_sems.at[i], recv_sem=recv_sems.at[i],
            device_id=(x, y, 1 - c), device_id_type=MESH) for i in range(na)]
        for cp in copies:
            cp.start()
        for cp in copies:
            cp.wait()

    hbm = pl.BlockSpec(memory_space=pl.ANY)
    return _pc(body, name=name, in_specs=[hbm] * na, out_specs=[hbm] * na,
               out_shape=[jax.ShapeDtypeStruct(a.shape, a.dtype) for a in arrays],
               scratch_shapes=[pltpu.SemaphoreType.DMA((na,)), pltpu.SemaphoreType.DMA((na,))],
               compiler_params=_cparams())(*arrays)


def _heads_q(t):
    s = t.shape[0]
    return t.reshape(s, ATT_KV_HEADS, ATT_GROUP, ATT_HEAD_DIM).transpose(1, 2, 0, 3)


def _unheads_q(t):
    s = t.shape[2]
    return t.transpose(2, 0, 1, 3).reshape(s, ATT_KV_HEADS * ATT_GROUP * ATT_HEAD_DIM)


def _heads_kv(t):
    s = t.shape[0]
    return t.reshape(s, ATT_KV_HEADS, ATT_HEAD_DIM).transpose(1, 0, 2)


def _unheads_kv(t):
    s = t.shape[1]
    return t.transpose(1, 0, 2).reshape(s, ATT_KV_HEADS * ATT_HEAD_DIM)


def _row128(v):
    return jnp.pad(v, (0, LANES - v.shape[0])).reshape(1, LANES)


def _layer_fwd(x, p, exchange=None):
    sh1, sc1, gt1, sh2, sc2, gt2 = [p["mod"][i] for i in range(6)]
    (u,) = _rowwise(_f_mod, [_whole(x)], [sc1, sh1], [(D_MODEL, BF16)], "mod1")
    proj = _mm(u, p["w_in"], "nn", F32, "proj", tm=512, tn=512)
    qh = _heads_q(proj[:, C_Q:C_Q + 1024])
    kh = _heads_kv(proj[:, C_K:C_K + 256])
    vh = _heads_kv(proj[:, C_V:C_V + 256])
    sinks4 = p["sinks"].reshape(ATT_KV_HEADS, ATT_GROUP, 1, 1)
    o_a = _unheads_q(_attn_fwd(qh, kh, vh, sinks4))
    (o_b,), exchanged = _dn_fwd(proj, p["conv_w"], _row128(p["a_log"]), _row128(p["dt_bias"]),
                                p["dn_norm_w"].reshape(1, LANES), exchange)
    y_a = _mm(o_a, p["w_oa"], "nn", F32, "y_a")
    y_b = _mm(o_b, p["w_ob"], "nn", F32, "y_b")
    (gm,) = _rowwise(_f_gate, [(proj, C_GA // 1024, 1024), (proj, C_GB // 1024, 1024), _whole(y_a), _whole(y_b)], [],
                     [(D_MODEL, BF16)], "gate")
    mixed = _mm(gm, p["w_out"], "nn", F32, "mixed")
    x1, u2 = _rowwise(_f_post1, [_whole(x), _whole(mixed)], [gt1, p["ln1_g"], p["ln1_b"], sc2, sh2],
                      [(D_MODEL, F32), (D_MODEL, BF16)], "post1")
    hpre = _mm(u2, p["w_ff1"], "nn", F32, "ff1")
    (h,) = _rowwise(_f_act, [_whole(hpre)], [p["b_ff1"]], [(D_FF, BF16)], "act")
    ff = _mm(h, p["w_ff2"], "nn", F32, "ff2", tm=256)
    (x2,) = _rowwise(_f_post2, [_whole(x1), _whole(ff)], [gt2, p["b_ff2"], p["ln2_g"], p["ln2_b"]],
                     [(D_MODEL, F32)], "post2")
    saved = dict(x=x, u=u, proj=proj, o_a=o_a, o_b=o_b, y_a=y_a, y_b=y_b, gm=gm, mixed=mixed, x1=x1, u2=u2,
                 hpre=hpre, h=h, ff=ff)
    return x2, saved, exchanged


def _layer_bwd(dx2, p, sv, exchange=None):
    sh1, sc1, gt1, sh2, sc2, gt2 = [p["mod"][i] for i in range(6)]
    g = {}
    (dx1_a, dff), (dgt2, g["b_ff2"], g["ln2_g"], g["ln2_b"]) = _rowwise_bwd(
        _f_post2, [_whole(sv["x1"]), _whole(sv["ff"])], [gt2, p["b_ff2"], p["ln2_g"], p["ln2_b"]], [dx2],
        [F32, BF16], "post2_bwd")
    dh = _mm(dff, p["w_ff2"], "nt", F32, "dh")
    g["w_ff2"] = _mm(sv["h"], dff, "tn", F32, "dw_ff2")
    (dhpre,), (g["b_ff1"],) = _rowwise_bwd(_f_act, [_whole(sv["hpre"])], [p["b_ff1"]], [dh], [BF16], "act_bwd")
    du2 = _mm(dhpre, p["w_ff1"], "nt", F32, "du2", tm=256)
    g["w_ff1"] = _mm(sv["u2"], dhpre, "tn", F32, "dw_ff1")
    (dx_a, dmixed), (dgt1, g["ln1_g"], g["ln1_b"], dsc2, dsh2) = _rowwise_bwd(
        _f_post1, [_whole(sv["x"]), _whole(sv["mixed"])], [gt1, p["ln1_g"], p["ln1_b"], sc2, sh2], [dx1_a, du2],
        [F32, BF16], "post1_bwd")
    dgm = _mm(dmixed, p["w_out"], "nt", F32, "dgm")
    g["w_out"] = _mm(sv["gm"], dmixed, "tn", F32, "dw_out")
    proj = sv["proj"]
    (dga, dgb, dya, dyb), _ = _rowwise_bwd(
        _f_gate, [(proj, C_GA // 1024, 1024), (proj, C_GB // 1024, 1024), _whole(sv["y_a"]), _whole(sv["y_b"])], [],
        [dgm], [BF16, BF16, BF16, BF16], "gate_bwd")
    do_a = _mm(dya, p["w_oa"], "nt", F32, "do_a")
    g["w_oa"] = _mm(sv["o_a"], dya, "tn", F32, "dw_oa")
    do_b = _mm(dyb, p["w_ob"], "nt", F32, "do_b")
    g["w_ob"] = _mm(sv["o_b"], dyb, "tn", F32, "dw_ob")
    (ddq, ddk, ddv, ddz, dba, dwq, dwk, dwv, dalog, ddtb, dnw), exchanged = _dn_bwd(
        proj, p["conv_w"], _row128(p["a_log"]), _row128(p["dt_bias"]), p["dn_norm_w"].reshape(1, LANES), do_b, exchange)
    g["conv_w"] = jnp.concatenate([dwq, dwk, dwv], axis=1)
    g["a_log"], g["dt_bias"], g["dn_norm_w"] = dalog[0, :DN_HEADS], ddtb[0, :DN_HEADS], dnw[0]
    qh = _heads_q(proj[:, C_Q:C_Q + 1024])
    kh = _heads_kv(proj[:, C_K:C_K + 256])
    vh = _heads_kv(proj[:, C_V:C_V + 256])
    sinks4 = p["sinks"].reshape(ATT_KV_HEADS, ATT_GROUP, 1, 1)
    dqh, dkh, dvh, dsk = _attn_bwd(qh, kh, vh, sinks4, _heads_q(do_a))
    g["sinks"] = dsk.reshape(ATT_KV_HEADS * ATT_GROUP)
    s = proj.shape[0]
    dproj = jnp.concatenate([
        _unheads_q(dqh).astype(BF16), ddq, ddk, ddv, ddz, dga, dgb,
        _unheads_kv(dkh[:, WINDOW:, :]).astype(BF16), _unheads_kv(dvh[:, WINDOW:, :]).astype(BF16),
        dba.astype(BF16), jnp.zeros((s, D_IN_P - C_BA - LANES), BF16)], axis=1)
    du = _mm(dproj, p["w_in"], "nt", F32, "du", tm=256)
    g["w_in"] = _mm(sv["u"], dproj, "tn", F32, "dw_in")
    (dx,), (dsc1, dsh1) = _rowwise_bwd(_f_mod, [_whole(sv["x"])], [sc1, sh1], [du], [F32], "mod1_bwd", add=(0, dx_a))
    g["mod"] = jnp.stack([dsh1, dsc1, dgt1, dsh2, dsc2, dgt2])
    return dx, g, exchanged


def _permute_w_in(w):
    pad = jnp.zeros(w.shape[:-1] + (D_IN_P - D_IN,), w.dtype)
    return jnp.concatenate([w[..., 0:1024], w[..., 1536:5632], w[..., 5648:7696], w[..., 1024:1536],
                            w[..., 5632:5648], pad], axis=-1)


def _unpermute_w_in(g):
    return jnp.concatenate([g[..., 0:1024], g[..., C_K:C_K + 512], g[..., 1024:5120], g[..., C_BA:C_BA + 16],
                            g[..., 5120:7168]], axis=-1)


def _cols_from_chips(t):
    c, l, r, n = t.shape
    return t.transpose(1, 2, 0, 3).reshape(l, r, c * n)


def _cols_to_chips(t):
    l, r, n4 = t.shape
    return t.reshape(l, r, N_CHIPS, n4 // N_CHIPS).transpose(2, 0, 1, 3)


def _rows_from_chips(t):
    c, l, r, n = t.shape
    return t.transpose(1, 0, 2, 3).reshape(l, c * r, n)


def _rows_to_chips(t):
    l, r4, n = t.shape
    return t.reshape(l, N_CHIPS, r4 // N_CHIPS, n).transpose(1, 0, 2, 3)


_REPLICATED = ("b_ada", "a_log", "dt_bias", "sinks", "dn_norm_w", "ln1_g", "ln1_b", "b_ff1", "b_ff2", "ln2_g", "ln2_b")
_SMALL = _REPLICATED + ("conv_w",)
_PACK_W = 1024
_WEIGHT_ORDER = ("w_ada", "b_ada", "w_in", "conv_w", "a_log", "dt_bias", "sinks", "dn_norm_w", "w_oa", "w_ob", "w_out",
                 "ln1_g", "ln1_b", "w_ff1", "b_ff1", "w_ff2", "b_ff2", "ln2_g", "ln2_b")


def _pack_small(d):
    flat = jnp.concatenate([d[k].reshape(-1) for k in _SMALL])
    rows = -(-flat.shape[0] // (_PACK_W * 8)) * 8
    return jnp.pad(flat, (0, rows * _PACK_W - flat.shape[0])).reshape(rows, _PACK_W)


def _unpack_small(packed, shapes):
    flat = packed.reshape(-1)
    out, off = {}, 0
    for k in _SMALL:
        n = 1
        for d_ in shapes[k]:
            n *= d_
        out[k] = flat[off:off + n].reshape(shapes[k])
        off += n
    return out


def kernel(x, c, w_ada, b_ada, w_in, conv_w, a_log, dt_bias, sinks, dn_norm_w, w_oa, w_ob, w_out, ln1_g, ln1_b, w_ff1, b_ff1, w_ff2, b_ff2, ln2_g, ln2_b, loss_target, m_w_ada, m_b_ada, m_w_in, m_conv_w, m_a_log, m_dt_bias, m_sinks, m_dn_norm_w, m_w_oa, m_w_ob, m_w_out, m_ln1_g, m_ln1_b, m_w_ff1, m_b_ff1, m_w_ff2, m_b_ff2, m_ln2_g, m_ln2_b, v_w_ada, v_b_ada, v_w_in, v_conv_w, v_a_log, v_dt_bias, v_sinks, v_dn_norm_w, v_w_oa, v_w_ob, v_w_out, v_ln1_g, v_ln1_b, v_w_ff1, v_b_ff1, v_w_ff2, v_b_ff2, v_ln2_g, v_ln2_b):
    ix, iy, ic = _place()
    chip = 2 * ix + iy
    dev = 4 * ix + 2 * iy + ic
    weights = dict(w_ada=w_ada, b_ada=b_ada, w_in=w_in, conv_w=conv_w, a_log=a_log, dt_bias=dt_bias, sinks=sinks,
                   dn_norm_w=dn_norm_w, w_oa=w_oa, w_ob=w_ob, w_out=w_out, ln1_g=ln1_g, ln1_b=ln1_b, w_ff1=w_ff1,
                   b_ff1=b_ff1, w_ff2=w_ff2, b_ff2=b_ff2, ln2_g=ln2_g, ln2_b=ln2_b)
    mom_m = dict(w_ada=m_w_ada, b_ada=m_b_ada, w_in=m_w_in, conv_w=m_conv_w, a_log=m_a_log, dt_bias=m_dt_bias,
                 sinks=m_sinks, dn_norm_w=m_dn_norm_w, w_oa=m_w_oa, w_ob=m_w_ob, w_out=m_w_out, ln1_g=m_ln1_g,
                 ln1_b=m_ln1_b, w_ff1=m_w_ff1, b_ff1=m_b_ff1, w_ff2=m_w_ff2, b_ff2=m_b_ff2, ln2_g=m_ln2_g, ln2_b=m_ln2_b)
    mom_v = dict(w_ada=v_w_ada, b_ada=v_b_ada, w_in=v_w_in, conv_w=v_conv_w, a_log=v_a_log, dt_bias=v_dt_bias,
                 sinks=v_sinks, dn_norm_w=v_dn_norm_w, w_oa=v_w_oa, w_ob=v_w_ob, w_out=v_w_out, ln1_g=v_ln1_g,
                 ln1_b=v_ln1_b, w_ff1=v_w_ff1, b_ff1=v_b_ff1, w_ff2=v_w_ff2, b_ff2=v_b_ff2, ln2_g=v_ln2_g, ln2_b=v_ln2_b)

    n_ada = w_ada.shape[2]
    c_all = _all_gather8(jnp.pad(c, ((0, 7), (0, 0))))[:, 0, :]
    b_shard = lax.dynamic_slice_in_dim(b_ada, chip * n_ada, n_ada, axis=1).reshape(DEPTH, 1, n_ada)
    mod_t = _ada_fwd(c_all, w_ada, b_shard)
    mod_all = _all_gather8(mod_t.reshape(DEPTH * N_DEV, n_ada)).reshape(N_DEV, DEPTH, N_DEV, n_ada)
    mod_mine = lax.dynamic_index_in_dim(mod_all[0::2], dev, axis=2, keepdims=False)
    mod = mod_mine.transpose(1, 0, 2).reshape(DEPTH, 6, 1, D_MODEL)

    n_cw = conv_w.shape[2]
    cw_all = _all_gather8(conv_w.reshape(DEPTH * CONV_K, n_cw))[0::2]
    conv_full = cw_all.transpose(1, 0, 2).reshape(DEPTH, CONV_K, N_CHIPS * n_cw)

    big = ("w_in", "w_oa", "w_ob", "w_out", "w_ff1", "w_ff2")
    w16 = {k: weights[k].astype(BF16) for k in big}
    shards = lambda l: [w16[k][l] for k in big]

    def assemble(gathered):
        gw = {k: t[:, None] for k, t in zip(big, gathered)}
        full = dict(w_in=_permute_w_in(_cols_from_chips(gw["w_in"])), w_ff1=_cols_from_chips(gw["w_ff1"]),
                    w_oa=_rows_from_chips(gw["w_oa"]), w_ob=_rows_from_chips(gw["w_ob"]),
                    w_out=_rows_from_chips(gw["w_out"]), w_ff2=_rows_from_chips(gw["w_ff2"]))
        return {k: t[0] for k, t in full.items()}

    def slices_for_chips(g):
        one = {k: g[k][None] for k in big}
        to = dict(w_in=_cols_to_chips(_unpermute_w_in(one["w_in"])), w_ff1=_cols_to_chips(one["w_ff1"]),
                  w_oa=_rows_to_chips(one["w_oa"]), w_ob=_rows_to_chips(one["w_ob"]),
                  w_out=_rows_to_chips(one["w_out"]), w_ff2=_rows_to_chips(one["w_ff2"]))
        return [to[k][:, 0].astype(BF16) for k in big]

    full = [None] * DEPTH
    full[0] = assemble(_chip_exchange(shards(0), True, "gather_weights"))

    def layer_params(l):
        p = dict(full[l])
        p["mod"] = mod[l]
        p["conv_w"] = conv_full[l]
        for k in ("a_log", "dt_bias", "sinks", "dn_norm_w"):
            p[k] = weights[k][l]
        for k in ("ln1_g", "ln1_b", "b_ff1", "b_ff2", "ln2_g", "ln2_b"):
            p[k] = weights[k][l].reshape(1, -1)
        return p

    xs = x[0]
    saved = []
    for l in range(DEPTH):
        nxt = (shards(l + 1), True) if l + 1 < DEPTH else None
        xs, sv, gathered = _layer_fwd(xs, layer_params(l), nxt)
        if nxt is not None:
            full[l + 1] = assemble(gathered)
        saved.append(sv)
    dy, loss_local = _loss_head(xs, loss_target[0])
    loss = lax.psum(loss_local[0, 0], ("x", "y", "c"))
    grads = [None] * DEPTH
    received = [None] * DEPTH
    dx = dy
    pending = None
    for l in reversed(range(DEPTH)):
        dx, grads[l], got = _layer_bwd(dx, layer_params(l), saved[l], None if pending is None else (pending, False))
        if pending is not None:
            received[l + 1] = got
        pending = slices_for_chips(grads[l])
    received[0] = _chip_exchange(pending, False, "scatter_grads")
    grad_x = dx[None]
    gstack = {k: jnp.stack([grads[l][k] for l in range(DEPTH)]) for k in grads[0] if k not in big}

    dmod = gstack["mod"].reshape(DEPTH, 6 * D_MODEL)
    small_g = dict(b_ada=dmod, a_log=gstack["a_log"], dt_bias=gstack["dt_bias"], sinks=gstack["sinks"],
                   dn_norm_w=gstack["dn_norm_w"], ln1_g=gstack["ln1_g"], ln1_b=gstack["ln1_b"], b_ff1=gstack["b_ff1"],
                   b_ff2=gstack["b_ff2"], ln2_g=gstack["ln2_g"], ln2_b=gstack["ln2_b"], conv_w=gstack["conv_w"])
    shapes = {k: weights[k].shape for k in _REPLICATED}
    shapes["conv_w"] = small_g["conv_w"].shape
    g_all = _all_gather8(_pack_small(small_g))
    no_conv = jnp.zeros(shapes["conv_w"], F32)
    small_out = _small_adam(g_all, _pack_small(dict(weights, conv_w=no_conv)), _pack_small(dict(mom_m, conv_w=no_conv)),
                            _pack_small(dict(mom_v, conv_w=no_conv)))
    small_res = [_unpack_small(t, shapes) for t in small_out]
    g_conv = lax.dynamic_slice_in_dim(small_res[0]["conv_w"], chip * n_cw, n_cw, axis=2)
    res = {"conv_w": _adam_call([g_conv], conv_w, m_conv_w, v_conv_w, "adam_conv_w", tile=16)}

    dmod_all = g_all.reshape(N_DEV, -1)[:, :DEPTH * 6 * D_MODEL].reshape(N_DEV, DEPTH, 6 * D_MODEL)
    dmod_shard = lax.dynamic_slice_in_dim(dmod_all, chip * n_ada, n_ada, axis=2).transpose(1, 0, 2)
    g_w_ada = _ada_bwd(c_all, dmod_shard)
    res["w_ada"] = _adam_call([g_w_ada], w_ada, m_w_ada, v_w_ada, "adam_w_ada")

    by_weight = [jnp.stack([received[l][i] for l in range(DEPTH)], axis=1) for i in range(len(big))]
    partial = [_sum_slots(r.reshape(N_CHIPS, -1, r.shape[-1]), "sum_" + k) for k, r in zip(big, by_weight)]
    theirs = _sibling_exchange(partial, "sibling_grads")
    for k, mine, other in zip(big, partial, theirs):
        shape = weights[k].shape
        res[k] = _adam_call([mine.reshape(shape), other.reshape(shape)], weights[k], mom_m[k], mom_v[k], "adam_" + k)
    for k in _REPLICATED:
        res[k] = [small_res[i][k] for i in range(4)]

    outs = [loss, grad_x]
    for i in range(4):
        outs += [res[k][i] for k in _WEIGHT_ORDER]
    return tuple(outs)
```

```python
import functools

import jax
import jax.numpy as jnp
from jax import lax
from jax.experimental import pallas as pl
from jax.experimental.pallas import tpu as pltpu

F32, BF16 = jnp.float32, jnp.bfloat16
HI = lax.Precision.HIGHEST
MESH = pl.DeviceIdType.MESH

D_MODEL = 1024
DEPTH = 4
ATT_KV_HEADS, ATT_GROUP, ATT_HEAD_DIM, WINDOW = 4, 4, 64, 128
DN_HEADS, DN_HEAD_DIM, CONV_K, CHUNK = 8, 128, 4, 64
D_FF = 4 * D_MODEL
D_IN = 7696
ALPHA = (2 * DEPTH) ** 0.25
LN_EPS = 1e-5
RMS_EPS = 1e-6
ADAM_LR, ADAM_B1, ADAM_B2, ADAM_EPS, ADAM_WD, ADAM_STEP = 0.001, 0.9, 0.999, 1e-08, 0.01, 10

N_CHIPS = 4
N_DEV = 8
LANES = 128
D_IN_P = 8192
C_Q, C_DQ, C_DK, C_DV, C_Z, C_GA, C_GB, C_K, C_V, C_BA = 0, 1024, 2048, 3072, 4096, 5120, 6144, 7168, 7424, 7680
NEG = -1e30
VMEM_LIMIT = 56 << 20


def _pc(body, **kw):
    return pl.pallas_call(body, **kw)


def _cparams(sem=None):
    if sem is None:
        return pltpu.CompilerParams(vmem_limit_bytes=VMEM_LIMIT)
    return pltpu.CompilerParams(vmem_limit_bytes=VMEM_LIMIT, dimension_semantics=sem)


def _mm(a, b, mode, out_dtype, name, tm=512, tn=512):
    if mode == "nn":
        (m, k), (_, n) = a.shape, b.shape
        dims = (((1,), (0,)), ((), ()))
    elif mode == "nt":
        (m, k), (n, _) = a.shape, b.shape
        dims = (((1,), (1,)), ((), ()))
    else:
        (k, m), (_, n) = a.shape, b.shape
        dims = (((0,), (0,)), ((), ()))
    tm, tn = min(tm, m), min(tn, n)
    assert m % tm == 0 and n % tn == 0, (name, m, n, tm, tn)
    a_spec = pl.BlockSpec((k, tm), lambda i, j: (0, i)) if mode == "tn" else pl.BlockSpec((tm, k), lambda i, j: (i, 0))
    b_spec = pl.BlockSpec((tn, k), lambda i, j: (j, 0)) if mode == "nt" else pl.BlockSpec((k, tn), lambda i, j: (0, j))

    def body(a_ref, b_ref, o_ref):
        o_ref[...] = lax.dot_general(a_ref[...], b_ref[...], dims, preferred_element_type=F32).astype(o_ref.dtype)

    return _pc(body, name=name, grid=(m // tm, n // tn), in_specs=[a_spec, b_spec],
               out_specs=pl.BlockSpec((tm, tn), lambda i, j: (i, j)),
               out_shape=jax.ShapeDtypeStruct((m, n), out_dtype), compiler_params=_cparams())(a, b)


def _row_specs(rows, tile):
    return [pl.BlockSpec((tile, w), functools.partial(lambda i, cb: (i, cb), cb=cb)) for (_, cb, w) in rows]


def _vec_specs(vecs):
    return [pl.BlockSpec(v.shape, lambda i: (0, 0)) for v in vecs]


def _rowwise(fn, rows, vecs, outs, name, tile=256):
    n = rows[0][0].shape[0]
    tile = min(tile, n)
    nr, nv = len(rows), len(vecs)

    def body(*refs):
        rv = [r[...].astype(F32) for r in refs[:nr]]
        vv = [r[...] for r in refs[nr:nr + nv]]
        for o_ref, val in zip(refs[nr + nv:], fn(*rv, *vv)):
            o_ref[...] = val.astype(o_ref.dtype)

    res = _pc(body, name=name, grid=(n // tile,), in_specs=_row_specs(rows, tile) + _vec_specs(vecs),
              out_specs=[pl.BlockSpec((tile, w), lambda i: (i, 0)) for (w, _) in outs],
              out_shape=[jax.ShapeDtypeStruct((n, w), dt) for (w, dt) in outs],
              compiler_params=_cparams())(*[r[0] for r in rows], *vecs)
    return res


def _rowwise_bwd(fn, rows, vecs, cts, row_dtypes, name, tile=256, add=None):
    n = rows[0][0].shape[0]
    tile = min(tile, n)
    nr, nv, nc = len(rows), len(vecs), len(cts)
    want = [i for i, dt in enumerate(row_dtypes) if dt is not None]
    n_add = 0 if add is None else 1

    def body(*refs):
        rv = [r[...].astype(F32) for r in refs[:nr]]
        vv = [r[...] for r in refs[nr:nr + nv]]
        cv = [r[...].astype(F32) for r in refs[nr + nv:nr + nv + nc]]
        pos = nr + nv + nc
        add_ref = refs[pos] if n_add else None
        pos += n_add
        row_out = refs[pos:pos + len(want)]
        vec_out = refs[pos + len(want):]
        _, vjp = jax.vjp(fn, *rv, *vv)
        grads = vjp(tuple(cv))
        for o_ref, i in zip(row_out, want):
            gval = grads[i]
            if n_add and add[0] == i:
                gval = gval + add_ref[...]
            o_ref[...] = gval.astype(o_ref.dtype)

        @pl.when(pl.program_id(0) == 0)
        def _():
            for o_ref in vec_out:
                o_ref[...] = jnp.zeros_like(o_ref)

        for o_ref, gval in zip(vec_out, grads[nr:]):
            o_ref[...] += gval

    ct_rows = [(c, 0, c.shape[1]) for c in cts]
    add_rows = [(add[1], 0, add[1].shape[1])] if n_add else []
    res = _pc(body, name=name, grid=(n // tile,),
              in_specs=_row_specs(rows, tile) + _vec_specs(vecs) + _row_specs(ct_rows + add_rows, tile),
              out_specs=[pl.BlockSpec((tile, rows[i][2]), lambda i_: (i_, 0)) for i in want] + _vec_specs(vecs),
              out_shape=[jax.ShapeDtypeStruct((n, rows[i][2]), row_dtypes[i]) for i in want]
              + [jax.ShapeDtypeStruct(v.shape, F32) for v in vecs],
              compiler_params=_cparams(("arbitrary",)))(*[r[0] for r in rows], *vecs, *cts, *[a[0] for a in add_rows])
    return res[:len(want)], res[len(want):]


def _whole(a, cb=0, w=None):
    return (a, cb, a.shape[1] if w is None else w)


def _ln(x, g, b):
    mu = jnp.mean(x, axis=-1, keepdims=True)
    var = jnp.mean(jnp.square(x - mu), axis=-1, keepdims=True)
    return (x - mu) * lax.rsqrt(var + LN_EPS) * g + b


def _silu(x):
    return x * jax.nn.sigmoid(x)


def _softplus(x):
    return jnp.maximum(x, 0.0) + jnp.log(1.0 + jnp.exp(-jnp.abs(x)))


def _f_mod(x, sc, sh):
    return (x * (1.0 + sc) + sh,)


def _f_gate(ga, gb, ya, yb):
    return (jax.nn.sigmoid(ga) * ya + jax.nn.sigmoid(gb) * yb,)


def _f_post1(x, mixed, gt, g1, b1, sc2, sh2):
    x1 = _ln(ALPHA * x + (1.0 + gt) * mixed, g1, b1)
    return x1, x1 * (1.0 + sc2) + sh2


def _f_act(hpre, b):
    return (jnp.square(jnp.maximum(hpre + b, 0.0)),)


def _f_post2(x1, ff, gt, bff2, g2, b2):
    return (_ln(ALPHA * x1 + (1.0 + gt) * (ff + bff2), g2, b2),)


def _attn_valid(n):
    qi = lax.broadcasted_iota(jnp.int32, (WINDOW, 2 * WINDOW), 0)
    si = lax.broadcasted_iota(jnp.int32, (WINDOW, 2 * WINDOW), 1)
    diff = qi + WINDOW - si
    return (diff >= 0) & (diff < WINDOW) & (n * WINDOW + si - WINDOW >= 0)


def _attn_block(q4, kp, kc, vp, vc, sk4, valid):
    kband = jnp.concatenate([kp, kc], axis=0).astype(BF16)
    vband = jnp.concatenate([vp, vc], axis=0).astype(BF16)
    outs = []
    for g in range(ATT_GROUP):
        s = lax.dot_general(q4[g].astype(BF16), kband, (((1,), (1,)), ((), ())), preferred_element_type=F32)
        s = jnp.where(valid, s * (ATT_HEAD_DIM ** -0.5), NEG)
        sink = sk4[g]
        m = lax.stop_gradient(jnp.maximum(jnp.max(s, axis=-1, keepdims=True), sink))
        p = jnp.exp(s - m)
        denom = jnp.sum(p, axis=-1, keepdims=True) + jnp.exp(sink - m)
        probs = (p / denom).astype(BF16)
        outs.append(jnp.dot(probs, vband, preferred_element_type=F32))
    return jnp.stack(outs)


def _attn_specs(s):
    nb = s // WINDOW
    q_spec = pl.BlockSpec((1, ATT_GROUP, WINDOW, ATT_HEAD_DIM), lambda h, n: (h, 0, n, 0))
    prev = pl.BlockSpec((1, WINDOW, ATT_HEAD_DIM), lambda h, n: (h, jnp.maximum(n - 1, 0), 0))
    cur = pl.BlockSpec((1, WINDOW, ATT_HEAD_DIM), lambda h, n: (h, n, 0))
    sk = pl.BlockSpec((1, ATT_GROUP, 1, 1), lambda h, n: (h, 0, 0, 0))
    return nb, q_spec, prev, cur, sk


def _attn_fwd(qh, kh, vh, sinks4):
    s = qh.shape[2]
    nb, q_spec, prev, cur, sk = _attn_specs(s)

    def body(q_ref, kp_ref, kc_ref, vp_ref, vc_ref, sk_ref, o_ref):
        valid = _attn_valid(pl.program_id(1))
        o = _attn_block(q_ref[0], kp_ref[0], kc_ref[0], vp_ref[0], vc_ref[0], sk_ref[0], valid)
        o_ref[0] = o.astype(o_ref.dtype)

    return _pc(body, name="attn_fwd", grid=(ATT_KV_HEADS, nb), in_specs=[q_spec, prev, cur, prev, cur, sk],
               out_specs=q_spec, out_shape=jax.ShapeDtypeStruct(qh.shape, BF16), compiler_params=_cparams())(
                   qh, kh, kh, vh, vh, sinks4)


def _attn_bwd(qh, kh, vh, sinks4, doh):
    s = qh.shape[2]
    nb, q_spec, prev, cur, sk = _attn_specs(s)
    acc = pl.BlockSpec((1, s + WINDOW, ATT_HEAD_DIM), lambda h, n: (h, 0, 0))

    def body(q_ref, kp_ref, kc_ref, vp_ref, vc_ref, sk_ref, do_ref, dq_ref, dk_ref, dv_ref, dsk_ref):
        n = pl.program_id(1)
        valid = _attn_valid(n)
        fn = functools.partial(_attn_block, valid=valid)
        _, vjp = jax.vjp(fn, q_ref[0], kp_ref[0], kc_ref[0], vp_ref[0], vc_ref[0], sk_ref[0])
        dq, dkp, dkc, dvp, dvc, dsk = vjp(do_ref[0].astype(F32))
        dq_ref[0] = dq

        @pl.when(n == 0)
        def _():
            dk_ref[...] = jnp.zeros_like(dk_ref)
            dv_ref[...] = jnp.zeros_like(dv_ref)
            dsk_ref[...] = jnp.zeros_like(dsk_ref)

        band = pl.ds(pl.multiple_of(n * WINDOW, WINDOW), 2 * WINDOW)
        dk_ref[0, band, :] += jnp.concatenate([dkp, dkc], axis=0)
        dv_ref[0, band, :] += jnp.concatenate([dvp, dvc], axis=0)
        dsk_ref[0] += dsk

    kv_shape = jax.ShapeDtypeStruct((ATT_KV_HEADS, s + WINDOW, ATT_HEAD_DIM), F32)
    return _pc(body, name="attn_bwd", grid=(ATT_KV_HEADS, nb), in_specs=[q_spec, prev, cur, prev, cur, sk, q_spec],
               out_specs=[q_spec, acc, acc, sk],
               out_shape=[jax.ShapeDtypeStruct(qh.shape, F32), kv_shape, kv_shape, jax.ShapeDtypeStruct(sinks4.shape, F32)],
               compiler_params=_cparams(("arbitrary", "arbitrary")))(qh, kh, kh, vh, vh, sinks4, doh)


def _bdot(a, b, dims=(((1,), (0,)), ((), ()))):
    return lax.dot_general(a.astype(BF16), b.astype(BF16), dims, preferred_element_type=F32)


def _hdot(a, b, dims=(((1,), (0,)), ((), ()))):
    return lax.dot_general(a, b, dims, precision=HI, preferred_element_type=F32)


_NN = (((1,), (0,)), ((), ()))
_NT = (((1,), (1,)), ((), ()))
_TN = (((0,), (0,)), ((), ()))


def _split2(a):
    hi = a.astype(BF16)
    return hi, (a - hi.astype(F32)).astype(BF16)


def _dot3(a, b, dims):
    ah, al = _split2(a)
    bh, bl = _split2(b)
    d = lambda p, q: lax.dot_general(p, q, dims, preferred_element_type=F32)
    return d(ah, bh) + (d(ah, bl) + d(al, bh))


@jax.custom_vjp
def _xdot(a, b):
    return _dot3(a, b, _NN)


def _xdot_fwd(a, b):
    return _dot3(a, b, _NN), (a, b)


def _xdot_bwd(res, g):
    a, b = res
    return _dot3(g, b, _NT), _dot3(a, g, _TN)


_xdot.defvjp(_xdot_fwd, _xdot_bwd)


def _mask_dot(mask16, b, dims):
    hi = b.astype(BF16)
    r = b - hi.astype(F32)
    mid = r.astype(BF16)
    lo = (r - mid.astype(F32)).astype(BF16)
    d = lambda q: lax.dot_general(mask16, q, dims, preferred_element_type=F32)
    return d(hi) + (d(mid) + d(lo))


def _chunk_masks():
    r = lax.broadcasted_iota(jnp.int32, (CHUNK, CHUNK), 0)
    c = lax.broadcasted_iota(jnp.int32, (CHUNK, CHUNK), 1)
    return r >= c, r > c, (r == c).astype(F32)


def _dn_local(qs, ks, vs, bs, gs, masks):
    causal, strict, eye = masks
    rng = range(len(qs))
    gb = [jnp.broadcast_to(gs[i], (CHUNK, CHUNK)) for i in rng]
    decay = [jnp.exp(jnp.where(causal, gb[i] - gb[i].T, NEG)) for i in rng]
    kb = [ks[i] * bs[i] for i in rng]
    vb = [vs[i] * bs[i] for i in rng]
    kk = [_bdot(kb[i], ks[i], _NT) for i in rng]
    p = [-jnp.where(strict, kk[i] * decay[i], 0.0) for i in rng]
    t = [eye + p[i] for i in rng]
    for _ in range(5):
        p = [_xdot(p[i], p[i]) for i in rng]
        t = [t[i] + _xdot(p[i], t[i]) for i in rng]
    eg = [jnp.exp(gs[i]) for i in rng]
    u = [_xdot(t[i], vb[i]) for i in rng]
    w = [_xdot(t[i], kb[i] * eg[i]) for i in rng]
    qk = [_bdot(qs[i], ks[i], _NT) for i in rng]
    intra = [qk[i] * decay[i] for i in rng]
    q_dec = [qs[i] * eg[i] for i in rng]
    k_dec = [ks[i] * jnp.exp(gs[i][CHUNK - 1:CHUNK, :] - gs[i]) for i in rng]
    return u, w, intra, q_dec, k_dec


def _dn_state(u, w, intra, q_dec, k_dec, gcum, state):
    v_new = u - _bdot(w, state)
    o = _bdot(q_dec, state) + _bdot(intra, v_new)
    new_state = state * jnp.exp(gcum[CHUNK - 1:CHUNK, :]) + _bdot(k_dec, v_new, _TN)
    return o, new_state


def _l2norm(t):
    return t * lax.rsqrt(jnp.sum(jnp.square(t), axis=-1, keepdims=True) + RMS_EPS)


def _dn_pre(aq, ak, av, ba, alog, dtb, h):
    lane = lax.broadcasted_iota(jnp.int32, (1, LANES), 1)
    pick = lambda t, i: jnp.sum(jnp.where(lane == i, t, 0.0), axis=1, keepdims=True)
    q = _l2norm(_silu(aq)) * (DN_HEAD_DIM ** -0.5)
    k = _l2norm(_silu(ak))
    v = _silu(av)
    beta = jax.nn.sigmoid(pick(ba, h))
    g = -jnp.exp(pick(alog, h)) * _softplus(pick(ba, h + DN_HEADS) + pick(dtb, h))
    return q, k, v, beta, g


def _dn_post(o, z, nw):
    o = o * lax.rsqrt(jnp.mean(jnp.square(o), axis=-1, keepdims=True) + RMS_EPS) * nw
    return o * _silu(z)


_PAD = 8
_TOK_TILE = 512


def _pad_front(pad_ref, x_ref, s):
    pad_ref[pl.ds(0, _PAD), :] = jnp.zeros((_PAD, pad_ref.shape[1]), F32)
    pad_ref[pl.ds(_PAD, s), :] = x_ref[...]


def _conv_tile(pad_ref, w4, r0, n):
    acc = None
    for j in range(CONV_K):
        term = pad_ref[pl.ds(r0 + _PAD - (CONV_K - 1) + j, n), :] * w4[j:j + 1, :]
        acc = term if acc is None else acc + term
    return acc


def _conv_tile_bwd(pad_ref, da_ref, w4, r0, n):
    dx, dw = None, []
    da = da_ref[pl.ds(r0, n), :]
    for j in range(CONV_K):
        term = da_ref[pl.ds(r0 + CONV_K - 1 - j, n), :] * w4[j:j + 1, :]
        dx = term if dx is None else dx + term
        dw.append(jnp.sum(da * pad_ref[pl.ds(r0 + _PAD - (CONV_K - 1) + j, n), :], axis=0, keepdims=True))
    return dx, jnp.concatenate(dw, axis=0)


def _dn_gcum(g_c, causal_f):
    return _mask_dot(causal_f, jnp.broadcast_to(g_c, (CHUNK, LANES)), _NN)[:, 0:1]


def _dn_in_specs(s):
    col = lambda base: pl.BlockSpec((s, DN_HEAD_DIM), functools.partial(lambda h, b: (0, b + h), b=base // DN_HEAD_DIM))
    cw = lambda base: pl.BlockSpec((CONV_K, DN_HEAD_DIM), functools.partial(lambda h, b: (0, b + h), b=base))
    row = pl.BlockSpec((1, LANES), lambda h: (0, 0))
    ba = pl.BlockSpec((s, LANES), lambda h: (0, C_BA // LANES))
    return [col(C_DQ), col(C_DK), col(C_DV), col(C_Z), ba, cw(0), cw(DN_HEADS), cw(2 * DN_HEADS), row, row, row]


def _chunk_rows(c):
    return pl.ds(pl.multiple_of(c * CHUNK, CHUNK), CHUNK)


def _group(nchunk, want):
    g = min(want, nchunk)
    assert nchunk % g == 0
    return g


def _dn_forward_scan(q_s, k_s, v_s, b_s, g_s, gc_s, loc, o_s, states_ref, s):
    masks = _chunk_masks()
    causal_f = masks[0].astype(BF16)
    nchunk = s // CHUNK
    grp = _group(nchunk, 4)
    u_s, w_s, in_s, qd_s, kd_s = loc

    def local_step(i, carry):
        rows = [_chunk_rows(i * grp + j) for j in range(grp)]
        gcum = [_dn_gcum(g_s[r, :], causal_f) for r in rows]
        u, w, intra, q_dec, k_dec = _dn_local([q_s[r, :] for r in rows], [k_s[r, :] for r in rows],
                                              [v_s[r, :] for r in rows], [b_s[r, :] for r in rows], gcum, masks)
        for j, r in enumerate(rows):
            gc_s[r, :] = gcum[j]
            u_s[r, :] = u[j]
            w_s[r, :] = w[j].astype(w_s.dtype)
            in_s[r, :] = intra[j].astype(in_s.dtype)
            qd_s[r, :] = q_dec[j].astype(qd_s.dtype)
            kd_s[r, :] = k_dec[j].astype(kd_s.dtype)
        return carry

    lax.fori_loop(0, nchunk // grp, local_step, 0)

    def state_step(i, state):
        rows = _chunk_rows(i)
        if states_ref is not None:
            states_ref[i] = state
        o, state = _dn_state(u_s[rows, :], w_s[rows, :], in_s[rows, :], qd_s[rows, :], kd_s[rows, :], gc_s[rows, :], state)
        o_s[rows, :] = o
        return state

    lax.fori_loop(0, nchunk, state_step, jnp.zeros((DN_HEAD_DIM, DN_HEAD_DIM), F32))


def _dn_local_scratch(s):
    d = DN_HEAD_DIM
    return [pltpu.VMEM((s, d), F32), pltpu.VMEM((s, d), BF16), pltpu.VMEM((s, CHUNK), BF16), pltpu.VMEM((s, d), BF16),
            pltpu.VMEM((s, d), BF16)]


def _call_with_exchange(body, name, steps, in_specs, out_specs, out_shape, scratch, args, exchange):
    if exchange is None:
        res = _pc(body, name=name, grid=(steps,), in_specs=in_specs, out_specs=out_specs, out_shape=out_shape,
                  scratch_shapes=scratch, compiler_params=_cparams(("arbitrary",)))(*args)
        return res, None
    arrays, gather = exchange
    x_in, x_out, x_shape, x_scratch = _exchange_specs(arrays, gather)
    wrapped = _carry_exchange(body, len(in_specs), len(out_specs), len(scratch), len(arrays), gather, steps)
    res = _pc(wrapped, name=name + "_x", grid=(steps,), in_specs=in_specs + x_in, out_specs=out_specs + x_out,
              out_shape=out_shape + x_shape, scratch_shapes=scratch + x_scratch,
              compiler_params=_cparams(("arbitrary",)))(*args, *arrays)
    return res[:len(out_specs)], res[len(out_specs):]


def _dn_fwd(proj, conv_w, alog, dtb, nw, exchange=None):
    s = proj.shape[0]
    d = DN_HEAD_DIM

    tt = min(_TOK_TILE, s)

    def body(xq, xk, xv, z, ba, wq, wk, wv, alog_r, dtb_r, nw_r, o_ref,
             padq, padk, padv, q_s, k_s, v_s, b_s, g_s, gc_s, o_s, *loc):
        h = pl.program_id(0)
        _pad_front(padq, xq, s)
        _pad_front(padk, xk, s)
        _pad_front(padv, xv, s)
        for r0 in range(0, s, tt):
            rows = pl.ds(r0, tt)
            aq, ak, av = _conv_tile(padq, wq[...], r0, tt), _conv_tile(padk, wk[...], r0, tt), _conv_tile(padv, wv[...], r0, tt)
            q_s[rows, :], k_s[rows, :], v_s[rows, :], b_s[rows, :], g_s[rows, :] = _dn_pre(
                aq, ak, av, ba[rows, :], alog_r[...], dtb_r[...], h)
        _dn_forward_scan(q_s, k_s, v_s, b_s, g_s, gc_s, loc, o_s, None, s)
        for r0 in range(0, s, tt):
            rows = pl.ds(r0, tt)
            o_ref[rows, :] = _dn_post(o_s[rows, :], z[rows, :], nw_r[...]).astype(o_ref.dtype)

    big = pltpu.VMEM((s, d), F32)
    thin = pltpu.VMEM((s, 1), F32)
    padded = pltpu.VMEM((s + _PAD, d), F32)
    return _call_with_exchange(
        body, "dn_fwd", DN_HEADS, _dn_in_specs(s), [pl.BlockSpec((s, d), lambda h: (0, h))],
        [jax.ShapeDtypeStruct((s, DN_HEADS * d), BF16)],
        [padded, padded, padded, big, big, big, thin, thin, thin, big] + _dn_local_scratch(s),
        (proj, proj, proj, proj, proj, conv_w, conv_w, conv_w, alog, dtb, nw), exchange)


def _dn_bwd(proj, conv_w, alog, dtb, nw, dob, exchange=None):
    s = proj.shape[0]
    d = DN_HEAD_DIM
    nchunk = s // CHUNK

    tt = min(_TOK_TILE, s)

    def body(xq, xk, xv, z, ba, wq, wk, wv, alog_r, dtb_r, nw_r, dob_ref,
             dxq, dxk, dxv, dz, dba, dwq, dwk, dwv, dalog, ddtb, dnw,
             padq, padk, padv, q_s, k_s, v_s, b_s, g_s, gc_s, o_s, states, dq_s, dk_s, dv_s, db_s, dg_s,
             dkd_s, din_s, dgc_s, *loc):
        h = pl.program_id(0)
        masks = _chunk_masks()
        causal_f = masks[0].astype(BF16)
        pre = functools.partial(_dn_pre, h=h)
        _pad_front(padq, xq, s)
        _pad_front(padk, xk, s)
        _pad_front(padv, xv, s)

        def conv_tiles(r0):
            return _conv_tile(padq, wq[...], r0, tt), _conv_tile(padk, wk[...], r0, tt), _conv_tile(padv, wv[...], r0, tt)

        for r0 in range(0, s, tt):
            rows = pl.ds(r0, tt)
            q_s[rows, :], k_s[rows, :], v_s[rows, :], b_s[rows, :], g_s[rows, :] = pre(
                *conv_tiles(r0), ba[rows, :], alog_r[...], dtb_r[...])
        _dn_forward_scan(q_s, k_s, v_s, b_s, g_s, gc_s, loc, o_s, states, s)
        u_s, w_s, in_s, qd_s, kd_s = loc
        dnw_v = jnp.zeros((1, LANES), F32)
        for r0 in range(0, s, tt):
            rows = pl.ds(r0, tt)
            _, post_vjp = jax.vjp(_dn_post, o_s[rows, :], z[rows, :], nw_r[...])
            do_raw, dz_v, dnw_t = post_vjp(dob_ref[rows, :].astype(F32))
            dz[rows, :] = dz_v.astype(dz.dtype)
            o_s[rows, :] = do_raw
            dnw_v = dnw_v + dnw_t

        def state_step(i, dstate):
            c = nchunk - 1 - i
            rows = _chunk_rows(c)
            _, vjp = jax.vjp(_dn_state, u_s[rows, :], w_s[rows, :].astype(F32), in_s[rows, :].astype(F32),
                             qd_s[rows, :].astype(F32), kd_s[rows, :].astype(F32), gc_s[rows, :], states[c])
            du, dw, din, dqd, dkd, dgc, dstate = vjp((o_s[rows, :], dstate))
            dq_s[rows, :] = du
            dk_s[rows, :] = dw
            dv_s[rows, :] = dqd
            dkd_s[rows, :] = dkd
            din_s[rows, :] = din
            dgc_s[rows, :] = dgc
            return dstate

        lax.fori_loop(0, nchunk, state_step, jnp.zeros((d, d), F32))
        local = functools.partial(_dn_local, masks=masks)
        grp = _group(nchunk, 4)

        def local_step(i, carry):
            rows = [_chunk_rows(i * grp + j) for j in range(grp)]
            get = lambda ref: [ref[r, :] for r in rows]
            _, vjp = jax.vjp(local, get(q_s), get(k_s), get(v_s), get(b_s), get(gc_s))
            dq_c, dk_c, dv_c, db_c, dgc_c = vjp((get(dq_s), get(dk_s), get(din_s), get(dv_s), get(dkd_s)))
            dgc_c = [dgc_c[j] + dgc_s[r, :] for j, r in enumerate(rows)]
            dg_c = [_mask_dot(causal_f, jnp.broadcast_to(t, (CHUNK, LANES)), _TN)[:, 0:1] for t in dgc_c]
            for j, r in enumerate(rows):
                dq_s[r, :] = dq_c[j]
                dk_s[r, :] = dk_c[j]
                dv_s[r, :] = dv_c[j]
                db_s[r, :] = db_c[j]
                dg_s[r, :] = dg_c[j]
            return carry

        lax.fori_loop(0, nchunk // grp, local_step, 0)

        @pl.when(h == 0)
        def _():
            dba[...] = jnp.zeros_like(dba)
            dalog[...] = jnp.zeros_like(dalog)
            ddtb[...] = jnp.zeros_like(ddtb)
            dnw[...] = jnp.zeros_like(dnw)

        dalog_v = jnp.zeros((1, LANES), F32)
        ddtb_v = jnp.zeros((1, LANES), F32)
        for r0 in range(0, s, tt):
            rows = pl.ds(r0, tt)
            _, pre_vjp = jax.vjp(pre, *conv_tiles(r0), ba[rows, :], alog_r[...], dtb_r[...])
            daq, dak, dav, dba_t, dalog_t, ddtb_t = pre_vjp(
                (dq_s[rows, :], dk_s[rows, :], dv_s[rows, :], db_s[rows, :], dg_s[rows, :]))
            dq_s[rows, :], dk_s[rows, :], dv_s[rows, :] = daq, dak, dav
            dba[rows, :] += dba_t
            dalog_v = dalog_v + dalog_t
            ddtb_v = ddtb_v + ddtb_t
        tail = pl.ds(s, _PAD)
        dq_s[tail, :] = dk_s[tail, :] = dv_s[tail, :] = jnp.zeros((_PAD, d), F32)
        for pad, da_s, w_ref, dx_ref, dw_ref in ((padq, dq_s, wq, dxq, dwq), (padk, dk_s, wk, dxk, dwk), (padv, dv_s, wv, dxv, dwv)):
            dw_acc = jnp.zeros((CONV_K, d), F32)
            for r0 in range(0, s, tt):
                dx_t, dw_t = _conv_tile_bwd(pad, da_s, w_ref[...], r0, tt)
                dx_ref[pl.ds(r0, tt), :] = dx_t.astype(dx_ref.dtype)
                dw_acc = dw_acc + dw_t
            dw_ref[...] = dw_acc
        dalog[...] += dalog_v
        ddtb[...] += ddtb_v
        dnw[...] += dnw_v

    big = pltpu.VMEM((s, d), F32)
    thin = pltpu.VMEM((s, 1), F32)
    padded = pltpu.VMEM((s + _PAD, d), F32)
    w_all = DN_HEADS * d
    col_out = lambda: pl.BlockSpec((s, d), lambda h: (0, h))
    cw_out = lambda: pl.BlockSpec((CONV_K, d), lambda h: (0, h))
    row = lambda: pl.BlockSpec((1, LANES), lambda h: (0, 0))
    big_out = jax.ShapeDtypeStruct((s, w_all), BF16)
    cw_shape = jax.ShapeDtypeStruct((CONV_K, w_all), F32)
    row_shape = jax.ShapeDtypeStruct((1, LANES), F32)
    return _call_with_exchange(
        body, "dn_bwd", DN_HEADS, _dn_in_specs(s) + [pl.BlockSpec((s, d), lambda h: (0, h))],
        [col_out(), col_out(), col_out(), col_out(), pl.BlockSpec((s, LANES), lambda h: (0, 0)),
         cw_out(), cw_out(), cw_out(), row(), row(), row()],
        [big_out, big_out, big_out, big_out, jax.ShapeDtypeStruct((s, LANES), F32),
         cw_shape, cw_shape, cw_shape, row_shape, row_shape, row_shape],
        [padded, padded, padded, big, big, big, thin, thin, thin, big,
         pltpu.VMEM((nchunk, d, d), F32), padded, padded, padded, thin, thin,
         big, pltpu.VMEM((s, CHUNK), F32), thin] + _dn_local_scratch(s),
        (proj, proj, proj, proj, proj, conv_w, conv_w, conv_w, alog, dtb, nw, dob), exchange)


def _loss_head(y, target, tile=256):
    n, dm = y.shape
    tile = min(tile, n)

    def body(y_ref, t_ref, dy_ref, loss_ref):
        err = y_ref[...] - t_ref[...]
        dy_ref[...] = err * (1.0 / dm)

        @pl.when(pl.program_id(0) == 0)
        def _():
            loss_ref[...] = jnp.zeros_like(loss_ref)

        loss_ref[...] += 0.5 * jnp.sum(jnp.mean(jnp.square(err), axis=-1, keepdims=True), axis=0, keepdims=True)

    blk = pl.BlockSpec((tile, dm), lambda i: (i, 0))
    return _pc(body, name="loss_head", grid=(n // tile,), in_specs=[blk, blk],
               out_specs=[blk, pl.BlockSpec((1, 1), lambda i: (0, 0))],
               out_shape=[jax.ShapeDtypeStruct((n, dm), F32), jax.ShapeDtypeStruct((1, 1), F32)],
               compiler_params=_cparams(("arbitrary",)))(y, target)


def _ada_fwd(c_all, w_ada, b_shard):
    nl, dm, n = w_ada.shape

    def body(c_ref, w_ref, b_ref, o_ref):
        ca = _silu(c_ref[...]).astype(BF16)
        o_ref[0] = jnp.dot(ca, w_ref[0].astype(BF16), preferred_element_type=F32) + b_ref[0]

    return _pc(body, name="ada_fwd", grid=(nl,),
               in_specs=[pl.BlockSpec((N_DEV, dm), lambda l: (0, 0)), pl.BlockSpec((1, dm, n), lambda l: (l, 0, 0)),
                         pl.BlockSpec((1, 1, n), lambda l: (l, 0, 0))],
               out_specs=pl.BlockSpec((1, N_DEV, n), lambda l: (l, 0, 0)),
               out_shape=jax.ShapeDtypeStruct((nl, N_DEV, n), F32), compiler_params=_cparams())(c_all, w_ada, b_shard)


def _ada_bwd(c_all, dmod):
    nl, _, n = dmod.shape
    dm = c_all.shape[1]

    def body(c_ref, d_ref, o_ref):
        o_ref[0] = _hdot(_silu(c_ref[...]), d_ref[0], _TN)

    return _pc(body, name="ada_bwd", grid=(nl,),
               in_specs=[pl.BlockSpec((N_DEV, dm), lambda l: (0, 0)), pl.BlockSpec((1, N_DEV, n), lambda l: (l, 0, 0))],
               out_specs=pl.BlockSpec((1, dm, n), lambda l: (l, 0, 0)),
               out_shape=jax.ShapeDtypeStruct((nl, dm, n), F32), compiler_params=_cparams())(c_all, dmod)


def _adamw(g, w, m, v):
    m = ADAM_B1 * m + (1.0 - ADAM_B1) * g
    v = ADAM_B2 * v + (1.0 - ADAM_B2) * jnp.square(g)
    m_hat = m / (1.0 - ADAM_B1 ** ADAM_STEP)
    v_hat = v / (1.0 - ADAM_B2 ** ADAM_STEP)
    delta = -ADAM_LR * (m_hat / (jnp.sqrt(v_hat) + ADAM_EPS) + ADAM_WD * w)
    return delta, m, v


def _adam_call(parts, w, m, v, name, tile=128):
    shape = w.shape
    flat = lambda t: t.reshape(-1, shape[-1])
    width = shape[-1]

    def fn(*vals):
        g = vals[0] if len(parts) == 1 else vals[0] + vals[1]
        return (g,) + _adamw(g, *vals[len(parts):])

    rows = [_whole(flat(t)) for t in (*parts, w, m, v)]
    outs = _rowwise(fn, rows, [], [(width, F32)] * 4, name, tile=tile)
    return [o.reshape(shape) for o in outs]


def _sum_slots(r, name, tile=128):
    _, n, width = r.shape
    tile = min(tile, n)

    def body(r_ref, o_ref):
        acc = r_ref[0].astype(F32)
        for j in range(1, N_CHIPS):
            acc = acc + r_ref[j].astype(F32)
        o_ref[...] = acc

    return _pc(body, name=name, grid=(n // tile,), in_specs=[pl.BlockSpec((N_CHIPS, tile, width), lambda i: (0, i, 0))],
               out_specs=pl.BlockSpec((tile, width), lambda i: (i, 0)),
               out_shape=jax.ShapeDtypeStruct((n, width), F32), compiler_params=_cparams())(r)


def _small_adam(g_all, w, m, v):
    def body(g_ref, w_ref, m_ref, v_ref, og, od, om, ov):
        g = g_ref[0]
        for j in range(1, N_DEV):
            g = g + g_ref[j]
        og[...] = g
        od[...], om[...], ov[...] = _adamw(g, w_ref[...], m_ref[...], v_ref[...])

    vm = pl.BlockSpec(memory_space=pltpu.VMEM)
    shp = jax.ShapeDtypeStruct(w.shape, F32)
    return _pc(body, name="small_adam", in_specs=[vm] * 4, out_specs=[vm] * 4, out_shape=[shp] * 4,
               compiler_params=_cparams())(g_all, w, m, v)


def _place():
    return lax.axis_index("x"), lax.axis_index("y"), lax.axis_index("c")


def _flip(v, bit):
    return 1 - v if bit else v


def _all_gather8(a):
    r, n = a.shape

    def body(a_ref, o_ref, send_sems, recv_sems):
        x, y, c = _place()
        me = 4 * x + 2 * y + c
        o_ref[me] = a_ref[...]
        copies = []
        for k in range(1, N_DEV):
            peer = (_flip(x, k & 4), _flip(y, k & 2), _flip(c, k & 1))
            copies.append(pltpu.make_async_remote_copy(
                src_ref=a_ref, dst_ref=o_ref.at[me], send_sem=send_sems.at[k - 1], recv_sem=recv_sems.at[k - 1],
                device_id=peer, device_id_type=MESH))
        for cp in copies:
            cp.start()
        for k in range(1, N_DEV):
            px, py, pc_ = _flip(x, k & 4), _flip(y, k & 2), _flip(c, k & 1)
            pltpu.make_async_remote_copy(
                src_ref=a_ref, dst_ref=o_ref.at[4 * px + 2 * py + pc_], send_sem=send_sems.at[k - 1],
                recv_sem=recv_sems.at[k - 1], device_id=(px, py, pc_), device_id_type=MESH).wait_recv()
        for cp in copies:
            cp.wait_send()

    vm = pl.BlockSpec(memory_space=pltpu.VMEM)
    return _pc(body, name="all_gather8", in_specs=[vm], out_specs=vm,
               out_shape=jax.ShapeDtypeStruct((N_DEV, r, n), a.dtype),
               scratch_shapes=[pltpu.SemaphoreType.DMA((N_DEV - 1,)), pltpu.SemaphoreType.DMA((N_DEV - 1,))],
               compiler_params=_cparams())(a)


def _chip_exchange(arrays, gather, name):
    na = len(arrays)

    def body(*refs):
        ins, outs, sems = refs[:na], refs[na:2 * na], refs[2 * na:]
        _exchange_copies(ins, outs, sems, gather, start=True)
        _exchange_copies(ins, outs, sems, gather, start=False)

    in_specs, out_specs, out_shape, scratch = _exchange_specs(arrays, gather)
    return _pc(body, name=name, in_specs=in_specs, out_specs=out_specs, out_shape=out_shape, scratch_shapes=scratch,
               compiler_params=_cparams())(*arrays)


def _exchange_specs(arrays, gather):
    na = len(arrays)
    hbm = pl.BlockSpec(memory_space=pl.ANY)
    out_shape = [jax.ShapeDtypeStruct(((N_CHIPS,) + a.shape) if gather else a.shape, a.dtype) for a in arrays]
    n_remote = 4 if gather else 2
    scratch = [pltpu.SemaphoreType.DMA((3 * na,))] * n_remote + [pltpu.SemaphoreType.DMA((na,))]
    return [hbm] * na, [hbm] * na, out_shape, scratch


def _gather_copies(ins, outs, sems, start):
    send_i, recv_i, send_d, recv_d, local_sems = sems
    x, y, c = _place()
    me = 2 * x + y
    sibling = (x, y, 1 - c)
    ici_sends, ici_arrivals, hand_ons, hand_arrivals, locals_ = [], [], [], [], []
    for i in range(len(ins)):
        half = ins[i].shape[0] // 2
        mine, other = pl.ds(c * half, half), pl.ds((1 - c) * half, half)
        locals_.append(pltpu.make_async_copy(ins[i], outs[i].at[me], local_sems.at[i]))
        for j in range(1, N_CHIPS):
            px, py = _flip(x, j & 2), _flip(y, j & 1)
            peer = 2 * px + py
            k = i * 3 + j - 1
            ici = dict(send_sem=send_i.at[k], recv_sem=recv_i.at[k], device_id=(px, py, c), device_id_type=MESH)
            d2d = dict(send_sem=send_d.at[k], recv_sem=recv_d.at[k], device_id=sibling, device_id_type=MESH)
            ici_sends.append(pltpu.make_async_remote_copy(src_ref=ins[i].at[mine], dst_ref=outs[i].at[me, mine], **ici))
            ici_arrivals.append(pltpu.make_async_remote_copy(src_ref=ins[i].at[mine], dst_ref=outs[i].at[peer, mine], **ici))
            hand_ons.append(pltpu.make_async_remote_copy(
                src_ref=outs[i].at[peer, mine], dst_ref=outs[i].at[peer, mine], **d2d))
            hand_arrivals.append(pltpu.make_async_remote_copy(
                src_ref=outs[i].at[peer, other], dst_ref=outs[i].at[peer, other], **d2d))
    if start:
        for cp in locals_ + ici_sends:
            cp.start()
    else:
        for arrival, hand_on in zip(ici_arrivals, hand_ons):
            arrival.wait_recv()
            hand_on.start()
        for cp in hand_arrivals:
            cp.wait_recv()
        for cp in ici_sends + hand_ons:
            cp.wait_send()
        for cp in locals_:
            cp.wait()


def _exchange_copies(ins, outs, sems, gather, start):
    if gather:
        return _gather_copies(ins, outs, sems, start)
    send_sems, recv_sems, local_sems = sems
    x, y, c = _place()
    me = 2 * x + y
    sends, arrivals, locals_ = [], [], []
    for i in range(len(ins)):
        locals_.append(pltpu.make_async_copy(ins[i] if gather else ins[i].at[me], outs[i].at[me], local_sems.at[i]))
        for j in range(1, N_CHIPS):
            px, py = _flip(x, j & 2), _flip(y, j & 1)
            peer = 2 * px + py
            pair = dict(send_sem=send_sems.at[i * 3 + j - 1], recv_sem=recv_sems.at[i * 3 + j - 1],
                        device_id=(px, py, c), device_id_type=MESH)
            sends.append(pltpu.make_async_remote_copy(
                src_ref=ins[i] if gather else ins[i].at[peer], dst_ref=outs[i].at[me], **pair))
            arrivals.append(pltpu.make_async_remote_copy(
                src_ref=ins[i] if gather else ins[i].at[me], dst_ref=outs[i].at[peer], **pair))
    if start:
        for cp in locals_ + sends:
            cp.start()
    else:
        for cp in arrivals:
            cp.wait_recv()
        for cp in sends:
            cp.wait_send()
        for cp in locals_:
            cp.wait()


def _carry_exchange(body, n_in, n_out, n_scratch, n_arrays, gather, steps):
    def wrapped(*refs):
        na = n_arrays
        ins, xin = refs[:n_in], refs[n_in:n_in + na]
        outs = refs[n_in + na:n_in + na + n_out]
        xout = refs[n_in + na + n_out:n_in + 2 * na + n_out]
        rest = refs[n_in + 2 * na + n_out:]
        scratch, sems = rest[:n_scratch], rest[n_scratch:]
        step = pl.program_id(0)

        @pl.when(step == 0)
        def _():
            _exchange_copies(xin, xout, sems, gather, start=True)

        body(*ins, *outs, *scratch)

        @pl.when(step == steps - 1)
        def _():
            _exchange_copies(xin, xout, sems, gather, start=False)

    return wrapped


def _sibling_exchange(arrays, name):
    na = len(arrays)

    def body(*refs):
        ins, outs = refs[:na], refs[na:2 * na]
        send_sems, recv_sems = refs[2 * na:]
        x, y, c = _place()
        copies = [pltpu.make_async_remote_copy(
            src_ref=ins[i], dst_ref=outs[i], send_sem=send_sems.at[i], recv_sem=recv_sems.at[i],
            device_id=(x, y, 1 - c), device_id_type=MESH) for i in range(na)]
        for cp in copies:
            cp.start()
        for cp in copies:
            cp.wait()

    hbm = pl.BlockSpec(memory_space=pl.ANY)
    return _pc(body, name=name, in_specs=[hbm] * na, out_specs=[hbm] * na,
               out_shape=[jax.ShapeDtypeStruct(a.shape, a.dtype) for a in arrays],
               scratch_shapes=[pltpu.SemaphoreType.DMA((na,)), pltpu.SemaphoreType.DMA((na,))],
               compiler_params=_cparams())(*arrays)


def _heads_q(t):
    s = t.shape[0]
    return t.reshape(s, ATT_KV_HEADS, ATT_GROUP, ATT_HEAD_DIM).transpose(1, 2, 0, 3)


def _unheads_q(t):
    s = t.shape[2]
    return t.transpose(2, 0, 1, 3).reshape(s, ATT_KV_HEADS * ATT_GROUP * ATT_HEAD_DIM)


def _heads_kv(t):
    s = t.shape[0]
    return t.reshape(s, ATT_KV_HEADS, ATT_HEAD_DIM).transpose(1, 0, 2)


def _unheads_kv(t):
    s = t.shape[1]
    return t.transpose(1, 0, 2).reshape(s, ATT_KV_HEADS * ATT_HEAD_DIM)


def _row128(v):
    return jnp.pad(v, (0, LANES - v.shape[0])).reshape(1, LANES)


def _layer_fwd(x, p, exchange=None):
    sh1, sc1, gt1, sh2, sc2, gt2 = [p["mod"][i] for i in range(6)]
    (u,) = _rowwise(_f_mod, [_whole(x)], [sc1, sh1], [(D_MODEL, BF16)], "mod1")
    proj = _mm(u, p["w_in"], "nn", F32, "proj", tm=512, tn=512)
    qh = _heads_q(proj[:, C_Q:C_Q + 1024])
    kh = _heads_kv(proj[:, C_K:C_K + 256])
    vh = _heads_kv(proj[:, C_V:C_V + 256])
    sinks4 = p["sinks"].reshape(ATT_KV_HEADS, ATT_GROUP, 1, 1)
    o_a = _unheads_q(_attn_fwd(qh, kh, vh, sinks4))
    (o_b,), exchanged = _dn_fwd(proj, p["conv_w"], _row128(p["a_log"]), _row128(p["dt_bias"]),
                                p["dn_norm_w"].reshape(1, LANES), exchange)
    y_a = _mm(o_a, p["w_oa"], "nn", F32, "y_a")
    y_b = _mm(o_b, p["w_ob"], "nn", F32, "y_b")
    (gm,) = _rowwise(_f_gate, [(proj, C_GA // 1024, 1024), (proj, C_GB // 1024, 1024), _whole(y_a), _whole(y_b)], [],
                     [(D_MODEL, BF16)], "gate")
    mixed = _mm(gm, p["w_out"], "nn", F32, "mixed")
    x1, u2 = _rowwise(_f_post1, [_whole(x), _whole(mixed)], [gt1, p["ln1_g"], p["ln1_b"], sc2, sh2],
                      [(D_MODEL, F32), (D_MODEL, BF16)], "post1")
    hpre = _mm(u2, p["w_ff1"], "nn", F32, "ff1")
    (h,) = _rowwise(_f_act, [_whole(hpre)], [p["b_ff1"]], [(D_FF, BF16)], "act")
    ff = _mm(h, p["w_ff2"], "nn", F32, "ff2", tm=256)
    (x2,) = _rowwise(_f_post2, [_whole(x1), _whole(ff)], [gt2, p["b_ff2"], p["ln2_g"], p["ln2_b"]],
                     [(D_MODEL, F32)], "post2")
    saved = dict(x=x, u=u, proj=proj, o_a=o_a, o_b=o_b, y_a=y_a, y_b=y_b, gm=gm, mixed=mixed, x1=x1, u2=u2,
                 hpre=hpre, h=h, ff=ff)
    return x2, saved, exchanged


def _layer_bwd(dx2, p, sv, carry=None):
    sh1, sc1, gt1, sh2, sc2, gt2 = [p["mod"][i] for i in range(6)]
    g = {}
    (dx1_a, dff), (dgt2, g["b_ff2"], g["ln2_g"], g["ln2_b"]) = _rowwise_bwd(
        _f_post2, [_whole(sv["x1"]), _whole(sv["ff"])], [gt2, p["b_ff2"], p["ln2_g"], p["ln2_b"]], [dx2],
        [F32, BF16], "post2_bwd")
    dh = _mm(dff, p["w_ff2"], "nt", F32, "dh")
    g["w_ff2"] = _mm(sv["h"], dff, "tn", F32, "dw_ff2")
    (dhpre,), (g["b_ff1"],) = _rowwise_bwd(_f_act, [_whole(sv["hpre"])], [p["b_ff1"]], [dh], [BF16], "act_bwd")
    du2 = _mm(dhpre, p["w_ff1"], "nt", F32, "du2", tm=256)
    g["w_ff1"] = _mm(sv["u2"], dhpre, "tn", F32, "dw_ff1")
    (dx_a, dmixed), (dgt1, g["ln1_g"], g["ln1_b"], dsc2, dsh2) = _rowwise_bwd(
        _f_post1, [_whole(sv["x"]), _whole(sv["mixed"])], [gt1, p["ln1_g"], p["ln1_b"], sc2, sh2], [dx1_a, du2],
        [F32, BF16], "post1_bwd")
    dgm = _mm(dmixed, p["w_out"], "nt", F32, "dgm")
    g["w_out"] = _mm(sv["gm"], dmixed, "tn", F32, "dw_out")
    proj = sv["proj"]
    (dga, dgb, dya, dyb), _ = _rowwise_bwd(
        _f_gate, [(proj, C_GA // 1024, 1024), (proj, C_GB // 1024, 1024), _whole(sv["y_a"]), _whole(sv["y_b"])], [],
        [dgm], [BF16, BF16, BF16, BF16], "gate_bwd")
    do_a = _mm(dya, p["w_oa"], "nt", F32, "do_a")
    g["w_oa"] = _mm(sv["o_a"], dya, "tn", F32, "dw_oa")
    do_b = _mm(dyb, p["w_ob"], "nt", F32, "do_b")
    g["w_ob"] = _mm(sv["o_b"], dyb, "tn", F32, "dw_ob")
    exchange = None if carry is None else (carry(g), False)
    (ddq, ddk, ddv, ddz, dba, dwq, dwk, dwv, dalog, ddtb, dnw), exchanged = _dn_bwd(
        proj, p["conv_w"], _row128(p["a_log"]), _row128(p["dt_bias"]), p["dn_norm_w"].reshape(1, LANES), do_b, exchange)
    g["conv_w"] = jnp.concatenate([dwq, dwk, dwv], axis=1)
    g["a_log"], g["dt_bias"], g["dn_norm_w"] = dalog[0, :DN_HEADS], ddtb[0, :DN_HEADS], dnw[0]
    qh = _heads_q(proj[:, C_Q:C_Q + 1024])
    kh = _heads_kv(proj[:, C_K:C_K + 256])
    vh = _heads_kv(proj[:, C_V:C_V + 256])
    sinks4 = p["sinks"].reshape(ATT_KV_HEADS, ATT_GROUP, 1, 1)
    dqh, dkh, dvh, dsk = _attn_bwd(qh, kh, vh, sinks4, _heads_q(do_a))
    g["sinks"] = dsk.reshape(ATT_KV_HEADS * ATT_GROUP)
    s = proj.shape[0]
    dproj = jnp.concatenate([
        _unheads_q(dqh).astype(BF16), ddq, ddk, ddv, ddz, dga, dgb,
        _unheads_kv(dkh[:, WINDOW:, :]).astype(BF16), _unheads_kv(dvh[:, WINDOW:, :]).astype(BF16),
        dba.astype(BF16), jnp.zeros((s, D_IN_P - C_BA - LANES), BF16)], axis=1)
    du = _mm(dproj, p["w_in"], "nt", F32, "du", tm=256)
    g["w_in"] = _mm(sv["u"], dproj, "tn", F32, "dw_in")
    (dx,), (dsc1, dsh1) = _rowwise_bwd(_f_mod, [_whole(sv["x"])], [sc1, sh1], [du], [F32], "mod1_bwd", add=(0, dx_a))
    g["mod"] = jnp.stack([dsh1, dsc1, dgt1, dsh2, dsc2, dgt2])
    return dx, g, exchanged


def _permute_w_in(w):
    pad = jnp.zeros(w.shape[:-1] + (D_IN_P - D_IN,), w.dtype)
    return jnp.concatenate([w[..., 0:1024], w[..., 1536:5632], w[..., 5648:7696], w[..., 1024:1536],
                            w[..., 5632:5648], pad], axis=-1)


def _unpermute_w_in(g):
    return jnp.concatenate([g[..., 0:1024], g[..., C_K:C_K + 512], g[..., 1024:5120], g[..., C_BA:C_BA + 16],
                            g[..., 5120:7168]], axis=-1)


def _cols_from_chips(t):
    c, l, r, n = t.shape
    return t.transpose(1, 2, 0, 3).reshape(l, r, c * n)


def _cols_to_chips(t):
    l, r, n4 = t.shape
    return t.reshape(l, r, N_CHIPS, n4 // N_CHIPS).transpose(2, 0, 1, 3)


def _rows_from_chips(t):
    c, l, r, n = t.shape
    return t.transpose(1, 0, 2, 3).reshape(l, c * r, n)


def _rows_to_chips(t):
    l, r4, n = t.shape
    return t.reshape(l, N_CHIPS, r4 // N_CHIPS, n).transpose(1, 0, 2, 3)


_REPLICATED = ("b_ada", "a_log", "dt_bias", "sinks", "dn_norm_w", "ln1_g", "ln1_b", "b_ff1", "b_ff2", "ln2_g", "ln2_b")
_SMALL = _REPLICATED + ("conv_w",)
_PACK_W = 1024
_WEIGHT_ORDER = ("w_ada", "b_ada", "w_in", "conv_w", "a_log", "dt_bias", "sinks", "dn_norm_w", "w_oa", "w_ob", "w_out",
                 "ln1_g", "ln1_b", "w_ff1", "b_ff1", "w_ff2", "b_ff2", "ln2_g", "ln2_b")


def _pack_small(d):
    flat = jnp.concatenate([d[k].reshape(-1) for k in _SMALL])
    rows = -(-flat.shape[0] // (_PACK_W * 8)) * 8
    return jnp.pad(flat, (0, rows * _PACK_W - flat.shape[0])).reshape(rows, _PACK_W)


def _unpack_small(packed, shapes):
    flat = packed.reshape(-1)
    out, off = {}, 0
    for k in _SMALL:
        n = 1
        for d_ in shapes[k]:
            n *= d_
        out[k] = flat[off:off + n].reshape(shapes[k])
        off += n
    return out


def kernel(x, c, w_ada, b_ada, w_in, conv_w, a_log, dt_bias, sinks, dn_norm_w, w_oa, w_ob, w_out, ln1_g, ln1_b, w_ff1, b_ff1, w_ff2, b_ff2, ln2_g, ln2_b, loss_target, m_w_ada, m_b_ada, m_w_in, m_conv_w, m_a_log, m_dt_bias, m_sinks, m_dn_norm_w, m_w_oa, m_w_ob, m_w_out, m_ln1_g, m_ln1_b, m_w_ff1, m_b_ff1, m_w_ff2, m_b_ff2, m_ln2_g, m_ln2_b, v_w_ada, v_b_ada, v_w_in, v_conv_w, v_a_log, v_dt_bias, v_sinks, v_dn_norm_w, v_w_oa, v_w_ob, v_w_out, v_ln1_g, v_ln1_b, v_w_ff1, v_b_ff1, v_w_ff2, v_b_ff2, v_ln2_g, v_ln2_b):
    ix, iy, ic = _place()
    chip = 2 * ix + iy
    dev = 4 * ix + 2 * iy + ic
    weights = dict(w_ada=w_ada, b_ada=b_ada, w_in=w_in, conv_w=conv_w, a_log=a_log, dt_bias=dt_bias, sinks=sinks,
                   dn_norm_w=dn_norm_w, w_oa=w_oa, w_ob=w_ob, w_out=w_out, ln1_g=ln1_g, ln1_b=ln1_b, w_ff1=w_ff1,
                   b_ff1=b_ff1, w_ff2=w_ff2, b_ff2=b_ff2, ln2_g=ln2_g, ln2_b=ln2_b)
    mom_m = dict(w_ada=m_w_ada, b_ada=m_b_ada, w_in=m_w_in, conv_w=m_conv_w, a_log=m_a_log, dt_bias=m_dt_bias,
                 sinks=m_sinks, dn_norm_w=m_dn_norm_w, w_oa=m_w_oa, w_ob=m_w_ob, w_out=m_w_out, ln1_g=m_ln1_g,
                 ln1_b=m_ln1_b, w_ff1=m_w_ff1, b_ff1=m_b_ff1, w_ff2=m_w_ff2, b_ff2=m_b_ff2, ln2_g=m_ln2_g, ln2_b=m_ln2_b)
    mom_v = dict(w_ada=v_w_ada, b_ada=v_b_ada, w_in=v_w_in, conv_w=v_conv_w, a_log=v_a_log, dt_bias=v_dt_bias,
                 sinks=v_sinks, dn_norm_w=v_dn_norm_w, w_oa=v_w_oa, w_ob=v_w_ob, w_out=v_w_out, ln1_g=v_ln1_g,
                 ln1_b=v_ln1_b, w_ff1=v_w_ff1, b_ff1=v_b_ff1, w_ff2=v_w_ff2, b_ff2=v_b_ff2, ln2_g=v_ln2_g, ln2_b=v_ln2_b)

    n_ada = w_ada.shape[2]
    c_all = _all_gather8(jnp.pad(c, ((0, 7), (0, 0))))[:, 0, :]
    b_shard = lax.dynamic_slice_in_dim(b_ada, chip * n_ada, n_ada, axis=1).reshape(DEPTH, 1, n_ada)
    mod_t = _ada_fwd(c_all, w_ada, b_shard)
    mod_all = _all_gather8(mod_t.reshape(DEPTH * N_DEV, n_ada)).reshape(N_DEV, DEPTH, N_DEV, n_ada)
    mod_mine = lax.dynamic_index_in_dim(mod_all[0::2], dev, axis=2, keepdims=False)
    mod = mod_mine.transpose(1, 0, 2).reshape(DEPTH, 6, 1, D_MODEL)

    n_cw = conv_w.shape[2]
    cw_all = _all_gather8(conv_w.reshape(DEPTH * CONV_K, n_cw))[0::2]
    conv_full = cw_all.transpose(1, 0, 2).reshape(DEPTH, CONV_K, N_CHIPS * n_cw)

    big = ("w_in", "w_oa", "w_ob", "w_out", "w_ff1", "w_ff2")
    w16 = {k: weights[k].astype(BF16) for k in big}
    shards = lambda l: [w16[k][l] for k in big]

    def assemble(gathered):
        gw = {k: t[:, None] for k, t in zip(big, gathered)}
        full = dict(w_in=_permute_w_in(_cols_from_chips(gw["w_in"])), w_ff1=_cols_from_chips(gw["w_ff1"]),
                    w_oa=_rows_from_chips(gw["w_oa"]), w_ob=_rows_from_chips(gw["w_ob"]),
                    w_out=_rows_from_chips(gw["w_out"]), w_ff2=_rows_from_chips(gw["w_ff2"]))
        return {k: t[0] for k, t in full.items()}

    to_chips = dict(w_in=lambda t: _cols_to_chips(_unpermute_w_in(t)), w_ff1=_cols_to_chips, w_oa=_rows_to_chips,
                    w_ob=_rows_to_chips, w_out=_rows_to_chips, w_ff2=_rows_to_chips)

    def slices_for_chips(g, keys):
        return [to_chips[k](g[k][None])[:, 0].astype(BF16) for k in keys]

    full = [None] * DEPTH
    full[0] = assemble(_chip_exchange(shards(0), True, "gather_weights"))

    def layer_params(l):
        p = dict(full[l])
        p["mod"] = mod[l]
        p["conv_w"] = conv_full[l]
        for k in ("a_log", "dt_bias", "sinks", "dn_norm_w"):
            p[k] = weights[k][l]
        for k in ("ln1_g", "ln1_b", "b_ff1", "b_ff2", "ln2_g", "ln2_b"):
            p[k] = weights[k][l].reshape(1, -1)
        return p

    xs = x[0]
    saved = []
    for l in range(DEPTH):
        nxt = (shards(l + 1), True) if l + 1 < DEPTH else None
        xs, sv, gathered = _layer_fwd(xs, layer_params(l), nxt)
        if nxt is not None:
            full[l + 1] = assemble(gathered)
        saved.append(sv)
    dy, loss_local = _loss_head(xs, loss_target[0])
    loss = lax.psum(loss_local[0, 0], ("x", "y", "c"))
    early = tuple(k for k in big if k != "w_in")
    grads = [None] * DEPTH
    received = [dict() for _ in range(DEPTH)]
    dx = dy
    pending = []
    for l in reversed(range(DEPTH)):
        carry = functools.partial(lambda g, first: first + slices_for_chips(g, early), first=pending)
        dx, grads[l], got = _layer_bwd(dx, layer_params(l), saved[l], carry)
        if pending:
            received[l + 1]["w_in"] = got[0]
        received[l].update(zip(early, got[len(pending):]))
        pending = slices_for_chips(grads[l], ("w_in",))
    received[0]["w_in"] = _chip_exchange(pending, False, "scatter_grads")[0]
    grad_x = dx[None]
    gstack = {k: jnp.stack([grads[l][k] for l in range(DEPTH)]) for k in grads[0] if k not in big}

    dmod = gstack["mod"].reshape(DEPTH, 6 * D_MODEL)
    small_g = dict(b_ada=dmod, a_log=gstack["a_log"], dt_bias=gstack["dt_bias"], sinks=gstack["sinks"],
                   dn_norm_w=gstack["dn_norm_w"], ln1_g=gstack["ln1_g"], ln1_b=gstack["ln1_b"], b_ff1=gstack["b_ff1"],
                   b_ff2=gstack["b_ff2"], ln2_g=gstack["ln2_g"], ln2_b=gstack["ln2_b"], conv_w=gstack["conv_w"])
    shapes = {k: weights[k].shape for k in _REPLICATED}
    shapes["conv_w"] = small_g["conv_w"].shape
    g_all = _all_gather8(_pack_small(small_g))
    no_conv = jnp.zeros(shapes["conv_w"], F32)
    small_out = _small_adam(g_all, _pack_small(dict(weights, conv_w=no_conv)), _pack_small(dict(mom_m, conv_w=no_conv)),
                            _pack_small(dict(mom_v, conv_w=no_conv)))
    small_res = [_unpack_small(t, shapes) for t in small_out]
    g_conv = lax.dynamic_slice_in_dim(small_res[0]["conv_w"], chip * n_cw, n_cw, axis=2)
    res = {"conv_w": _adam_call([g_conv], conv_w, m_conv_w, v_conv_w, "adam_conv_w", tile=16)}

    dmod_all = g_all.reshape(N_DEV, -1)[:, :DEPTH * 6 * D_MODEL].reshape(N_DEV, DEPTH, 6 * D_MODEL)
    dmod_shard = lax.dynamic_slice_in_dim(dmod_all, chip * n_ada, n_ada, axis=2).transpose(1, 0, 2)
    g_w_ada = _ada_bwd(c_all, dmod_shard)
    res["w_ada"] = _adam_call([g_w_ada], w_ada, m_w_ada, v_w_ada, "adam_w_ada")

    by_weight = [jnp.stack([received[l][k] for l in range(DEPTH)], axis=1) for k in big]
    partial = [_sum_slots(r.reshape(N_CHIPS, -1, r.shape[-1]), "sum_" + k) for k, r in zip(big, by_weight)]
    theirs = _sibling_exchange(partial, "sibling_grads")
    for k, mine, other in zip(big, partial, theirs):
        shape = weights[k].shape
        res[k] = _adam_call([mine.reshape(shape), other.reshape(shape)], weights[k], mom_m[k], mom_v[k], "adam_" + k)
    for k in _REPLICATED:
        res[k] = [small_res[i][k] for i in range(4)]

    outs = [loss, grad_x]
    for i in range(4):
        outs += [res[k][i] for k in _WEIGHT_ORDER]
    return tuple(outs)
```

```python
import functools

import jax
import jax.numpy as jnp
from jax import lax
from jax.experimental import pallas as pl
from jax.experimental.pallas import tpu as pltpu

F32, BF16 = jnp.float32, jnp.bfloat16
HI = lax.Precision.HIGHEST
MESH = pl.DeviceIdType.MESH

D_MODEL = 1024
DEPTH = 4
ATT_KV_HEADS, ATT_GROUP, ATT_HEAD_DIM, WINDOW = 4, 4, 64, 128
DN_HEADS, DN_HEAD_DIM, CONV_K, CHUNK = 8, 128, 4, 64
D_FF = 4 * D_MODEL
D_IN = 7696
ALPHA = (2 * DEPTH) ** 0.25
LN_EPS = 1e-5
RMS_EPS = 1e-6
ADAM_LR, ADAM_B1, ADAM_B2, ADAM_EPS, ADAM_WD, ADAM_STEP = 0.001, 0.9, 0.999, 1e-08, 0.01, 10

N_CHIPS = 4
N_DEV = 8
LANES = 128
D_IN_P = 8192
C_Q, C_DQ, C_DK, C_DV, C_Z, C_GA, C_GB, C_K, C_V, C_BA = 0, 1024, 2048, 3072, 4096, 5120, 6144, 7168, 7424, 7680
NEG = -1e30
VMEM_LIMIT = 56 << 20


def _pc(body, **kw):
    return pl.pallas_call(body, **kw)


def _cparams(sem=None):
    if sem is None:
        return pltpu.CompilerParams(vmem_limit_bytes=VMEM_LIMIT)
    return pltpu.CompilerParams(vmem_limit_bytes=VMEM_LIMIT, dimension_semantics=sem)


_MM_VMEM_BUDGET = 44 << 20
_MM_MIN_TILE = 256


def _mm_tiles(m, n, k, out_bytes):
    def halvings(d):
        out = [d]
        while out[-1] % 2 == 0 and out[-1] // 2 >= _MM_MIN_TILE:
            out.append(out[-1] // 2)
        return out

    best = None
    for tm in halvings(m):
        for tn in halvings(n):
            if 2 * (2 * tm * k + 2 * tn * k + out_bytes * tm * tn) > _MM_VMEM_BUDGET:
                continue
            cost = (2 * m * k + (m // tm) * 2 * n * k, (m // tm) * (n // tn))
            if best is None or cost < best[0]:
                best = (cost, tm, tn)
    assert best is not None, (m, n, k)
    return best[1], best[2]


def _mm(a, b, mode, out_dtype, name, tm=None, tn=None):
    if mode == "nn":
        (m, k), (_, n) = a.shape, b.shape
        dims = (((1,), (0,)), ((), ()))
    elif mode == "nt":
        (m, k), (n, _) = a.shape, b.shape
        dims = (((1,), (1,)), ((), ()))
    else:
        (k, m), (_, n) = a.shape, b.shape
        dims = (((0,), (0,)), ((), ()))
    if tm is None:
        tm, tn = _mm_tiles(m, n, k, jnp.dtype(out_dtype).itemsize)
    tm, tn = min(tm, m), min(tn, n)
    assert m % tm == 0 and n % tn == 0, (name, m, n, tm, tn)
    a_spec = pl.BlockSpec((k, tm), lambda i, j: (0, i)) if mode == "tn" else pl.BlockSpec((tm, k), lambda i, j: (i, 0))
    b_spec = pl.BlockSpec((tn, k), lambda i, j: (j, 0)) if mode == "nt" else pl.BlockSpec((k, tn), lambda i, j: (0, j))

    def body(a_ref, b_ref, o_ref):
        o_ref[...] = lax.dot_general(a_ref[...], b_ref[...], dims, preferred_element_type=F32).astype(o_ref.dtype)

    return _pc(body, name=name, grid=(m // tm, n // tn), in_specs=[a_spec, b_spec],
               out_specs=pl.BlockSpec((tm, tn), lambda i, j: (i, j)),
               out_shape=jax.ShapeDtypeStruct((m, n), out_dtype), compiler_params=_cparams())(a, b)


def _row_specs(rows, tile):
    return [pl.BlockSpec((tile, w), functools.partial(lambda i, cb: (i, cb), cb=cb)) for (_, cb, w) in rows]


def _vec_specs(vecs):
    return [pl.BlockSpec(v.shape, lambda i: (0, 0)) for v in vecs]


def _rowwise(fn, rows, vecs, outs, name, tile=256):
    n = rows[0][0].shape[0]
    tile = min(tile, n)
    nr, nv = len(rows), len(vecs)

    def body(*refs):
        rv = [r[...].astype(F32) for r in refs[:nr]]
        vv = [r[...] for r in refs[nr:nr + nv]]
        for o_ref, val in zip(refs[nr + nv:], fn(*rv, *vv)):
            o_ref[...] = val.astype(o_ref.dtype)

    res = _pc(body, name=name, grid=(n // tile,), in_specs=_row_specs(rows, tile) + _vec_specs(vecs),
              out_specs=[pl.BlockSpec((tile, w), lambda i: (i, 0)) for (w, _) in outs],
              out_shape=[jax.ShapeDtypeStruct((n, w), dt) for (w, dt) in outs],
              compiler_params=_cparams())(*[r[0] for r in rows], *vecs)
    return res


def _rowwise_bwd(fn, rows, vecs, cts, row_dtypes, name, tile=256, add=None):
    n = rows[0][0].shape[0]
    tile = min(tile, n)
    nr, nv, nc = len(rows), len(vecs), len(cts)
    want = [i for i, dt in enumerate(row_dtypes) if dt is not None]
    n_add = 0 if add is None else 1

    def body(*refs):
        rv = [r[...].astype(F32) for r in refs[:nr]]
        vv = [r[...] for r in refs[nr:nr + nv]]
        cv = [r[...].astype(F32) for r in refs[nr + nv:nr + nv + nc]]
        pos = nr + nv + nc
        add_ref = refs[pos] if n_add else None
        pos += n_add
        row_out = refs[pos:pos + len(want)]
        vec_out = refs[pos + len(want):]
        _, vjp = jax.vjp(fn, *rv, *vv)
        grads = vjp(tuple(cv))
        for o_ref, i in zip(row_out, want):
            gval = grads[i]
            if n_add and add[0] == i:
                gval = gval + add_ref[...]
            o_ref[...] = gval.astype(o_ref.dtype)

        @pl.when(pl.program_id(0) == 0)
        def _():
            for o_ref in vec_out:
                o_ref[...] = jnp.zeros_like(o_ref)

        for o_ref, gval in zip(vec_out, grads[nr:]):
            o_ref[...] += gval

    ct_rows = [(c, 0, c.shape[1]) for c in cts]
    add_rows = [(add[1], 0, add[1].shape[1])] if n_add else []
    res = _pc(body, name=name, grid=(n // tile,),
              in_specs=_row_specs(rows, tile) + _vec_specs(vecs) + _row_specs(ct_rows + add_rows, tile),
              out_specs=[pl.BlockSpec((tile, rows[i][2]), lambda i_: (i_, 0)) for i in want] + _vec_specs(vecs),
              out_shape=[jax.ShapeDtypeStruct((n, rows[i][2]), row_dtypes[i]) for i in want]
              + [jax.ShapeDtypeStruct(v.shape, F32) for v in vecs],
              compiler_params=_cparams(("arbitrary",)))(*[r[0] for r in rows], *vecs, *cts, *[a[0] for a in add_rows])
    return res[:len(want)], res[len(want):]


def _whole(a, cb=0, w=None):
    return (a, cb, a.shape[1] if w is None else w)


def _ln(x, g, b):
    mu = jnp.mean(x, axis=-1, keepdims=True)
    var = jnp.mean(jnp.square(x - mu), axis=-1, keepdims=True)
    return (x - mu) * lax.rsqrt(var + LN_EPS) * g + b


def _silu(x):
    return x * jax.nn.sigmoid(x)


def _softplus(x):
    return jnp.maximum(x, 0.0) + jnp.log(1.0 + jnp.exp(-jnp.abs(x)))


def _f_mod(x, sc, sh):
    return (x * (1.0 + sc) + sh,)


def _f_gate(ga, gb, ya, yb):
    return (jax.nn.sigmoid(ga) * ya + jax.nn.sigmoid(gb) * yb,)


def _f_post1(x, mixed, gt, g1, b1, sc2, sh2):
    x1 = _ln(ALPHA * x + (1.0 + gt) * mixed, g1, b1)
    return x1, x1 * (1.0 + sc2) + sh2


def _f_act(hpre, b):
    return (jnp.square(jnp.maximum(hpre + b, 0.0)),)


def _f_post2(x1, ff, gt, bff2, g2, b2):
    return (_ln(ALPHA * x1 + (1.0 + gt) * (ff + bff2), g2, b2),)


def _attn_valid(n):
    qi = lax.broadcasted_iota(jnp.int32, (WINDOW, 2 * WINDOW), 0)
    si = lax.broadcasted_iota(jnp.int32, (WINDOW, 2 * WINDOW), 1)
    diff = qi + WINDOW - si
    return (diff >= 0) & (diff < WINDOW) & (n * WINDOW + si - WINDOW >= 0)


def _attn_block(qs, kp, kc, vp, vc, sinks, valid):
    kband = jnp.concatenate([kp, kc], axis=0).astype(BF16)
    vband = jnp.concatenate([vp, vc], axis=0).astype(BF16)
    rng = range(len(qs))
    s = [lax.dot_general(qs[g].astype(BF16), kband, (((1,), (1,)), ((), ())), preferred_element_type=F32) for g in rng]
    s = [jnp.where(valid, s[g] * (ATT_HEAD_DIM ** -0.5), NEG) for g in rng]
    m = [lax.stop_gradient(jnp.maximum(jnp.max(s[g], axis=-1, keepdims=True), sinks[g])) for g in rng]
    p = [jnp.exp(s[g] - m[g]) for g in rng]
    denom = [jnp.sum(p[g], axis=-1, keepdims=True) + jnp.exp(sinks[g] - m[g]) for g in rng]
    probs = [(p[g] / denom[g]).astype(BF16) for g in rng]
    return [jnp.dot(probs[g], vband, preferred_element_type=F32) for g in rng]


def _attn_specs(s):
    nb = s // WINDOW
    q_spec = pl.BlockSpec((1, ATT_GROUP, WINDOW, ATT_HEAD_DIM), lambda h, n: (h, 0, n, 0))
    prev = pl.BlockSpec((1, WINDOW, ATT_HEAD_DIM), lambda h, n: (h, jnp.maximum(n - 1, 0), 0))
    cur = pl.BlockSpec((1, WINDOW, ATT_HEAD_DIM), lambda h, n: (h, n, 0))
    sk = pl.BlockSpec((1, ATT_GROUP, 1, 1), lambda h, n: (h, 0, 0, 0))
    return nb, q_spec, prev, cur, sk


def _attn_fwd(qh, kh, vh, sinks4):
    s = qh.shape[2]
    nb, q_spec, prev, cur, sk = _attn_specs(s)

    def body(q_ref, kp_ref, kc_ref, vp_ref, vc_ref, sk_ref, o_ref):
        valid = _attn_valid(pl.program_id(1))
        heads = range(ATT_GROUP)
        o = _attn_block([q_ref[0, g] for g in heads], kp_ref[0], kc_ref[0], vp_ref[0], vc_ref[0],
                        [sk_ref[0, g] for g in heads], valid)
        for g in heads:
            o_ref[0, g] = o[g].astype(o_ref.dtype)

    return _pc(body, name="attn_fwd", grid=(ATT_KV_HEADS, nb), in_specs=[q_spec, prev, cur, prev, cur, sk],
               out_specs=q_spec, out_shape=jax.ShapeDtypeStruct(qh.shape, BF16), compiler_params=_cparams())(
                   qh, kh, kh, vh, vh, sinks4)


def _attn_bwd(qh, kh, vh, sinks4, doh):
    s = qh.shape[2]
    nb, q_spec, prev, cur, sk = _attn_specs(s)
    acc = pl.BlockSpec((1, s + WINDOW, ATT_HEAD_DIM), lambda h, n: (h, 0, 0))

    def body(q_ref, kp_ref, kc_ref, vp_ref, vc_ref, sk_ref, do_ref, dq_ref, dk_ref, dv_ref, dsk_ref):
        n = pl.program_id(1)
        valid = _attn_valid(n)
        fn = functools.partial(_attn_block, valid=valid)
        heads = range(ATT_GROUP)
        _, vjp = jax.vjp(fn, [q_ref[0, g] for g in heads], kp_ref[0], kc_ref[0], vp_ref[0], vc_ref[0],
                         [sk_ref[0, g] for g in heads])
        dq, dkp, dkc, dvp, dvc, dsk = vjp([do_ref[0, g].astype(F32) for g in heads])
        for g in heads:
            dq_ref[0, g] = dq[g]

        @pl.when(n == 0)
        def _():
            dk_ref[...] = jnp.zeros_like(dk_ref)
            dv_ref[...] = jnp.zeros_like(dv_ref)
            dsk_ref[...] = jnp.zeros_like(dsk_ref)

        band = pl.ds(pl.multiple_of(n * WINDOW, WINDOW), 2 * WINDOW)
        dk_ref[0, band, :] += jnp.concatenate([dkp, dkc], axis=0)
        dv_ref[0, band, :] += jnp.concatenate([dvp, dvc], axis=0)
        for g in heads:
            dsk_ref[0, g] += dsk[g]

    kv_shape = jax.ShapeDtypeStruct((ATT_KV_HEADS, s + WINDOW, ATT_HEAD_DIM), F32)
    return _pc(body, name="attn_bwd", grid=(ATT_KV_HEADS, nb), in_specs=[q_spec, prev, cur, prev, cur, sk, q_spec],
               out_specs=[q_spec, acc, acc, sk],
               out_shape=[jax.ShapeDtypeStruct(qh.shape, F32), kv_shape, kv_shape, jax.ShapeDtypeStruct(sinks4.shape, F32)],
               compiler_params=_cparams(("arbitrary", "arbitrary")))(qh, kh, kh, vh, vh, sinks4, doh)


def _bdot(a, b, dims=(((1,), (0,)), ((), ()))):
    return lax.dot_general(a.astype(BF16), b.astype(BF16), dims, preferred_element_type=F32)


def _hdot(a, b, dims=(((1,), (0,)), ((), ()))):
    return lax.dot_general(a, b, dims, precision=HI, preferred_element_type=F32)


_NN = (((1,), (0,)), ((), ()))
_NT = (((1,), (1,)), ((), ()))
_TN = (((0,), (0,)), ((), ()))


def _split2(a):
    hi = a.astype(BF16)
    return hi, (a - hi.astype(F32)).astype(BF16)


def _dot3(a, b, dims):
    ah, al = _split2(a)
    bh, bl = _split2(b)
    d = lambda p, q: lax.dot_general(p, q, dims, preferred_element_type=F32)
    return d(ah, bh) + (d(ah, bl) + d(al, bh))


@jax.custom_vjp
def _xdot(a, b):
    return _dot3(a, b, _NN)


def _xdot_fwd(a, b):
    return _dot3(a, b, _NN), (a, b)


def _xdot_bwd(res, g):
    a, b = res
    return _dot3(g, b, _NT), _dot3(a, g, _TN)


_xdot.defvjp(_xdot_fwd, _xdot_bwd)


def _mask_dot(mask16, b, dims):
    hi = b.astype(BF16)
    r = b - hi.astype(F32)
    mid = r.astype(BF16)
    lo = (r - mid.astype(F32)).astype(BF16)
    d = lambda q: lax.dot_general(mask16, q, dims, preferred_element_type=F32)
    return d(hi) + (d(mid) + d(lo))


def _chunk_masks():
    r = lax.broadcasted_iota(jnp.int32, (CHUNK, CHUNK), 0)
    c = lax.broadcasted_iota(jnp.int32, (CHUNK, CHUNK), 1)
    return r >= c, r > c, (r == c).astype(F32)


def _dn_local(qs, ks, vs, bs, gs, masks):
    causal, strict, eye = masks
    rng = range(len(qs))
    gb = [jnp.broadcast_to(gs[i], (CHUNK, CHUNK)) for i in rng]
    decay = [jnp.exp(jnp.where(causal, gb[i] - gb[i].T, NEG)) for i in rng]
    kb = [ks[i] * bs[i] for i in rng]
    vb = [vs[i] * bs[i] for i in rng]
    kk = [_bdot(kb[i], ks[i], _NT) for i in rng]
    p = [-jnp.where(strict, kk[i] * decay[i], 0.0) for i in rng]
    t = [eye + p[i] for i in rng]
    for _ in range(5):
        p = [_xdot(p[i], p[i]) for i in rng]
        t = [t[i] + _xdot(p[i], t[i]) for i in rng]
    eg = [jnp.exp(gs[i]) for i in rng]
    u = [_xdot(t[i], vb[i]) for i in rng]
    w = [_xdot(t[i], kb[i] * eg[i]) for i in rng]
    qk = [_bdot(qs[i], ks[i], _NT) for i in rng]
    intra = [qk[i] * decay[i] for i in rng]
    q_dec = [qs[i] * eg[i] for i in rng]
    k_dec = [ks[i] * jnp.exp(gs[i][CHUNK - 1:CHUNK, :] - gs[i]) for i in rng]
    return u, w, intra, q_dec, k_dec


def _dn_state(u, w, intra, q_dec, k_dec, gcum, state):
    v_new = u - _bdot(w, state)
    o = _bdot(q_dec, state) + _bdot(intra, v_new)
    new_state = state * jnp.exp(gcum[CHUNK - 1:CHUNK, :]) + _bdot(k_dec, v_new, _TN)
    return o, new_state


def _l2norm(t):
    return t * lax.rsqrt(jnp.sum(jnp.square(t), axis=-1, keepdims=True) + RMS_EPS)


def _dn_pre(aq, ak, av, ba, alog, dtb, h):
    lane = lax.broadcasted_iota(jnp.int32, (1, LANES), 1)
    pick = lambda t, i: jnp.sum(jnp.where(lane == i, t, 0.0), axis=1, keepdims=True)
    q = _l2norm(_silu(aq)) * (DN_HEAD_DIM ** -0.5)
    k = _l2norm(_silu(ak))
    v = _silu(av)
    beta = jax.nn.sigmoid(pick(ba, h))
    g = -jnp.exp(pick(alog, h)) * _softplus(pick(ba, h + DN_HEADS) + pick(dtb, h))
    return q, k, v, beta, g


def _dn_post(o, z, nw):
    o = o * lax.rsqrt(jnp.mean(jnp.square(o), axis=-1, keepdims=True) + RMS_EPS) * nw
    return o * _silu(z)


_PAD = 8
_TOK_TILE = 512


def _pad_front(pad_ref, x_ref, s):
    pad_ref[pl.ds(0, _PAD), :] = jnp.zeros((_PAD, pad_ref.shape[1]), F32)
    pad_ref[pl.ds(_PAD, s), :] = x_ref[...]


def _conv_tile(pad_ref, w4, r0, n):
    acc = None
    for j in range(CONV_K):
        term = pad_ref[pl.ds(r0 + _PAD - (CONV_K - 1) + j, n), :] * w4[j:j + 1, :]
        acc = term if acc is None else acc + term
    return acc


def _conv_tile_bwd(pad_ref, da_ref, w4, r0, n):
    dx, dw = None, []
    da = da_ref[pl.ds(r0, n), :]
    for j in range(CONV_K):
        term = da_ref[pl.ds(r0 + CONV_K - 1 - j, n), :] * w4[j:j + 1, :]
        dx = term if dx is None else dx + term
        dw.append(jnp.sum(da * pad_ref[pl.ds(r0 + _PAD - (CONV_K - 1) + j, n), :], axis=0, keepdims=True))
    return dx, jnp.concatenate(dw, axis=0)


def _dn_gcum(g_c, causal_f):
    return _mask_dot(causal_f, jnp.broadcast_to(g_c, (CHUNK, LANES)), _NN)[:, 0:1]


def _dn_in_specs(s):
    col = lambda base: pl.BlockSpec((s, DN_HEAD_DIM), functools.partial(lambda h, b: (0, b + h), b=base // DN_HEAD_DIM))
    cw = lambda base: pl.BlockSpec((CONV_K, DN_HEAD_DIM), functools.partial(lambda h, b: (0, b + h), b=base))
    row = pl.BlockSpec((1, LANES), lambda h: (0, 0))
    ba = pl.BlockSpec((s, LANES), lambda h: (0, C_BA // LANES))
    return [col(C_DQ), col(C_DK), col(C_DV), col(C_Z), ba, cw(0), cw(DN_HEADS), cw(2 * DN_HEADS), row, row, row]


def _chunk_rows(c):
    return pl.ds(pl.multiple_of(c * CHUNK, CHUNK), CHUNK)


def _group(nchunk, want):
    g = min(want, nchunk)
    assert nchunk % g == 0
    return g


def _dn_forward_scan(q_s, k_s, v_s, b_s, g_s, gc_s, loc, o_s, states_ref, s):
    masks = _chunk_masks()
    causal_f = masks[0].astype(BF16)
    nchunk = s // CHUNK
    grp = _group(nchunk, 4)
    u_s, w_s, in_s, qd_s, kd_s = loc

    def local_step(i, carry):
        rows = [_chunk_rows(i * grp + j) for j in range(grp)]
        gcum = [_dn_gcum(g_s[r, :], causal_f) for r in rows]
        u, w, intra, q_dec, k_dec = _dn_local([q_s[r, :] for r in rows], [k_s[r, :] for r in rows],
                                              [v_s[r, :] for r in rows], [b_s[r, :] for r in rows], gcum, masks)
        for j, r in enumerate(rows):
            gc_s[r, :] = gcum[j]
            u_s[r, :] = u[j]
            w_s[r, :] = w[j].astype(w_s.dtype)
            in_s[r, :] = intra[j].astype(in_s.dtype)
            qd_s[r, :] = q_dec[j].astype(qd_s.dtype)
            kd_s[r, :] = k_dec[j].astype(kd_s.dtype)
        return carry

    lax.fori_loop(0, nchunk // grp, local_step, 0)

    def state_step(i, state):
        rows = _chunk_rows(i)
        if states_ref is not None:
            states_ref[i] = state
        o, state = _dn_state(u_s[rows, :], w_s[rows, :], in_s[rows, :], qd_s[rows, :], kd_s[rows, :], gc_s[rows, :], state)
        o_s[rows, :] = o
        return state

    lax.fori_loop(0, nchunk, state_step, jnp.zeros((DN_HEAD_DIM, DN_HEAD_DIM), F32))


def _dn_local_scratch(s):
    d = DN_HEAD_DIM
    return [pltpu.VMEM((s, d), F32), pltpu.VMEM((s, d), BF16), pltpu.VMEM((s, CHUNK), BF16), pltpu.VMEM((s, d), BF16),
            pltpu.VMEM((s, d), BF16)]


def _call_with_exchange(body, name, steps, in_specs, out_specs, out_shape, scratch, args, exchange):
    if exchange is None:
        res = _pc(body, name=name, grid=(steps,), in_specs=in_specs, out_specs=out_specs, out_shape=out_shape,
                  scratch_shapes=scratch, compiler_params=_cparams(("arbitrary",)))(*args)
        return res, None
    arrays, gather = exchange
    x_in, x_out, x_shape, x_scratch = _exchange_specs(arrays, gather)
    wrapped = _carry_exchange(body, len(in_specs), len(out_specs), len(scratch), len(arrays), gather, steps)
    res = _pc(wrapped, name=name + "_x", grid=(steps,), in_specs=in_specs + x_in, out_specs=out_specs + x_out,
              out_shape=out_shape + x_shape, scratch_shapes=scratch + x_scratch,
              compiler_params=_cparams(("arbitrary",)))(*args, *arrays)
    return res[:len(out_specs)], res[len(out_specs):]


def _dn_fwd(proj, conv_w, alog, dtb, nw, exchange=None):
    s = proj.shape[0]
    d = DN_HEAD_DIM

    tt = min(_TOK_TILE, s)

    def body(xq, xk, xv, z, ba, wq, wk, wv, alog_r, dtb_r, nw_r, o_ref,
             padq, padk, padv, q_s, k_s, v_s, b_s, g_s, gc_s, o_s, *loc):
        h = pl.program_id(0)
        _pad_front(padq, xq, s)
        _pad_front(padk, xk, s)
        _pad_front(padv, xv, s)
        for r0 in range(0, s, tt):
            rows = pl.ds(r0, tt)
            aq, ak, av = _conv_tile(padq, wq[...], r0, tt), _conv_tile(padk, wk[...], r0, tt), _conv_tile(padv, wv[...], r0, tt)
            q_s[rows, :], k_s[rows, :], v_s[rows, :], b_s[rows, :], g_s[rows, :] = _dn_pre(
                aq, ak, av, ba[rows, :], alog_r[...], dtb_r[...], h)
        _dn_forward_scan(q_s, k_s, v_s, b_s, g_s, gc_s, loc, o_s, None, s)
        for r0 in range(0, s, tt):
            rows = pl.ds(r0, tt)
            o_ref[rows, :] = _dn_post(o_s[rows, :], z[rows, :], nw_r[...]).astype(o_ref.dtype)

    big = pltpu.VMEM((s, d), F32)
    thin = pltpu.VMEM((s, 1), F32)
    padded = pltpu.VMEM((s + _PAD, d), F32)
    return _call_with_exchange(
        body, "dn_fwd", DN_HEADS, _dn_in_specs(s), [pl.BlockSpec((s, d), lambda h: (0, h))],
        [jax.ShapeDtypeStruct((s, DN_HEADS * d), BF16)],
        [padded, padded, padded, big, big, big, thin, thin, thin, big] + _dn_local_scratch(s),
        (proj, proj, proj, proj, proj, conv_w, conv_w, conv_w, alog, dtb, nw), exchange)


def _dn_bwd(proj, conv_w, alog, dtb, nw, dob, exchange=None):
    s = proj.shape[0]
    d = DN_HEAD_DIM
    nchunk = s // CHUNK

    tt = min(_TOK_TILE, s)

    def body(xq, xk, xv, z, ba, wq, wk, wv, alog_r, dtb_r, nw_r, dob_ref,
             dxq, dxk, dxv, dz, dba, dwq, dwk, dwv, dalog, ddtb, dnw,
             padq, padk, padv, q_s, k_s, v_s, b_s, g_s, gc_s, o_s, states, dq_s, dk_s, dv_s, db_s, dg_s,
             dkd_s, din_s, dgc_s, *loc):
        h = pl.program_id(0)
        masks = _chunk_masks()
        causal_f = masks[0].astype(BF16)
        pre = functools.partial(_dn_pre, h=h)
        _pad_front(padq, xq, s)
        _pad_front(padk, xk, s)
        _pad_front(padv, xv, s)

        def conv_tiles(r0):
            return _conv_tile(padq, wq[...], r0, tt), _conv_tile(padk, wk[...], r0, tt), _conv_tile(padv, wv[...], r0, tt)

        for r0 in range(0, s, tt):
            rows = pl.ds(r0, tt)
            q_s[rows, :], k_s[rows, :], v_s[rows, :], b_s[rows, :], g_s[rows, :] = pre(
                *conv_tiles(r0), ba[rows, :], alog_r[...], dtb_r[...])
        _dn_forward_scan(q_s, k_s, v_s, b_s, g_s, gc_s, loc, o_s, states, s)
        u_s, w_s, in_s, qd_s, kd_s = loc
        dnw_v = jnp.zeros((1, LANES), F32)
        for r0 in range(0, s, tt):
            rows = pl.ds(r0, tt)
            _, post_vjp = jax.vjp(_dn_post, o_s[rows, :], z[rows, :], nw_r[...])
            do_raw, dz_v, dnw_t = post_vjp(dob_ref[rows, :].astype(F32))
            dz[rows, :] = dz_v.astype(dz.dtype)
            o_s[rows, :] = do_raw
            dnw_v = dnw_v + dnw_t

        def state_step(i, dstate):
            c = nchunk - 1 - i
            rows = _chunk_rows(c)
            _, vjp = jax.vjp(_dn_state, u_s[rows, :], w_s[rows, :].astype(F32), in_s[rows, :].astype(F32),
                             qd_s[rows, :].astype(F32), kd_s[rows, :].astype(F32), gc_s[rows, :], states[c])
            du, dw, din, dqd, dkd, dgc, dstate = vjp((o_s[rows, :], dstate))
            dq_s[rows, :] = du
            dk_s[rows, :] = dw
            dv_s[rows, :] = dqd
            dkd_s[rows, :] = dkd
            din_s[rows, :] = din
            dgc_s[rows, :] = dgc
            return dstate

        lax.fori_loop(0, nchunk, state_step, jnp.zeros((d, d), F32))
        local = functools.partial(_dn_local, masks=masks)
        grp = _group(nchunk, 4)

        def local_step(i, carry):
            rows = [_chunk_rows(i * grp + j) for j in range(grp)]
            get = lambda ref: [ref[r, :] for r in rows]
            _, vjp = jax.vjp(local, get(q_s), get(k_s), get(v_s), get(b_s), get(gc_s))
            dq_c, dk_c, dv_c, db_c, dgc_c = vjp((get(dq_s), get(dk_s), get(din_s), get(dv_s), get(dkd_s)))
            dgc_c = [dgc_c[j] + dgc_s[r, :] for j, r in enumerate(rows)]
            dg_c = [_mask_dot(causal_f, jnp.broadcast_to(t, (CHUNK, LANES)), _TN)[:, 0:1] for t in dgc_c]
            for j, r in enumerate(rows):
                dq_s[r, :] = dq_c[j]
                dk_s[r, :] = dk_c[j]
                dv_s[r, :] = dv_c[j]
                db_s[r, :] = db_c[j]
                dg_s[r, :] = dg_c[j]
            return carry

        lax.fori_loop(0, nchunk // grp, local_step, 0)

        @pl.when(h == 0)
        def _():
            dba[...] = jnp.zeros_like(dba)
            dalog[...] = jnp.zeros_like(dalog)
            ddtb[...] = jnp.zeros_like(ddtb)
            dnw[...] = jnp.zeros_like(dnw)

        dalog_v = jnp.zeros((1, LANES), F32)
        ddtb_v = jnp.zeros((1, LANES), F32)
        for r0 in range(0, s, tt):
            rows = pl.ds(r0, tt)
            _, pre_vjp = jax.vjp(pre, *conv_tiles(r0), ba[rows, :], alog_r[...], dtb_r[...])
            daq, dak, dav, dba_t, dalog_t, ddtb_t = pre_vjp(
                (dq_s[rows, :], dk_s[rows, :], dv_s[rows, :], db_s[rows, :], dg_s[rows, :]))
            dq_s[rows, :], dk_s[rows, :], dv_s[rows, :] = daq, dak, dav
            dba[rows, :] += dba_t
            dalog_v = dalog_v + dalog_t
            ddtb_v = ddtb_v + ddtb_t
        tail = pl.ds(s, _PAD)
        dq_s[tail, :] = dk_s[tail, :] = dv_s[tail, :] = jnp.zeros((_PAD, d), F32)
        for pad, da_s, w_ref, dx_ref, dw_ref in ((padq, dq_s, wq, dxq, dwq), (padk, dk_s, wk, dxk, dwk), (padv, dv_s, wv, dxv, dwv)):
            dw_acc = jnp.zeros((CONV_K, d), F32)
            for r0 in range(0, s, tt):
                dx_t, dw_t = _conv_tile_bwd(pad, da_s, w_ref[...], r0, tt)
                dx_ref[pl.ds(r0, tt), :] = dx_t.astype(dx_ref.dtype)
                dw_acc = dw_acc + dw_t
            dw_ref[...] = dw_acc
        dalog[...] += dalog_v
        ddtb[...] += ddtb_v
        dnw[...] += dnw_v

    big = pltpu.VMEM((s, d), F32)
    thin = pltpu.VMEM((s, 1), F32)
    padded = pltpu.VMEM((s + _PAD, d), F32)
    w_all = DN_HEADS * d
    col_out = lambda: pl.BlockSpec((s, d), lambda h: (0, h))
    cw_out = lambda: pl.BlockSpec((CONV_K, d), lambda h: (0, h))
    row = lambda: pl.BlockSpec((1, LANES), lambda h: (0, 0))
    big_out = jax.ShapeDtypeStruct((s, w_all), BF16)
    cw_shape = jax.ShapeDtypeStruct((CONV_K, w_all), F32)
    row_shape = jax.ShapeDtypeStruct((1, LANES), F32)
    return _call_with_exchange(
        body, "dn_bwd", DN_HEADS, _dn_in_specs(s) + [pl.BlockSpec((s, d), lambda h: (0, h))],
        [col_out(), col_out(), col_out(), col_out(), pl.BlockSpec((s, LANES), lambda h: (0, 0)),
         cw_out(), cw_out(), cw_out(), row(), row(), row()],
        [big_out, big_out, big_out, big_out, jax.ShapeDtypeStruct((s, LANES), F32),
         cw_shape, cw_shape, cw_shape, row_shape, row_shape, row_shape],
        [padded, padded, padded, big, big, big, thin, thin, thin, big,
         pltpu.VMEM((nchunk, d, d), F32), padded, padded, padded, thin, thin,
         big, pltpu.VMEM((s, CHUNK), F32), thin] + _dn_local_scratch(s),
        (proj, proj, proj, proj, proj, conv_w, conv_w, conv_w, alog, dtb, nw, dob), exchange)


def _loss_head(y, target, tile=256):
    n, dm = y.shape
    tile = min(tile, n)

    def body(y_ref, t_ref, dy_ref, loss_ref):
        err = y_ref[...] - t_ref[...]
        dy_ref[...] = err * (1.0 / dm)

        @pl.when(pl.program_id(0) == 0)
        def _():
            loss_ref[...] = jnp.zeros_like(loss_ref)

        loss_ref[...] += 0.5 * jnp.sum(jnp.mean(jnp.square(err), axis=-1, keepdims=True), axis=0, keepdims=True)

    blk = pl.BlockSpec((tile, dm), lambda i: (i, 0))
    return _pc(body, name="loss_head", grid=(n // tile,), in_specs=[blk, blk],
               out_specs=[blk, pl.BlockSpec((1, 1), lambda i: (0, 0))],
               out_shape=[jax.ShapeDtypeStruct((n, dm), F32), jax.ShapeDtypeStruct((1, 1), F32)],
               compiler_params=_cparams(("arbitrary",)))(y, target)


def _ada_fwd(c_all, w_ada, b_shard):
    nl, dm, n = w_ada.shape

    def body(c_ref, w_ref, b_ref, o_ref):
        ca = _silu(c_ref[...]).astype(BF16)
        o_ref[0] = jnp.dot(ca, w_ref[0].astype(BF16), preferred_element_type=F32) + b_ref[0]

    return _pc(body, name="ada_fwd", grid=(nl,),
               in_specs=[pl.BlockSpec((N_DEV, dm), lambda l: (0, 0)), pl.BlockSpec((1, dm, n), lambda l: (l, 0, 0)),
                         pl.BlockSpec((1, 1, n), lambda l: (l, 0, 0))],
               out_specs=pl.BlockSpec((1, N_DEV, n), lambda l: (l, 0, 0)),
               out_shape=jax.ShapeDtypeStruct((nl, N_DEV, n), F32), compiler_params=_cparams())(c_all, w_ada, b_shard)


def _ada_bwd(c_all, dmod):
    nl, _, n = dmod.shape
    dm = c_all.shape[1]

    def body(c_ref, d_ref, o_ref):
        o_ref[0] = _hdot(_silu(c_ref[...]), d_ref[0], _TN)

    return _pc(body, name="ada_bwd", grid=(nl,),
               in_specs=[pl.BlockSpec((N_DEV, dm), lambda l: (0, 0)), pl.BlockSpec((1, N_DEV, n), lambda l: (l, 0, 0))],
               out_specs=pl.BlockSpec((1, dm, n), lambda l: (l, 0, 0)),
               out_shape=jax.ShapeDtypeStruct((nl, dm, n), F32), compiler_params=_cparams())(c_all, dmod)


def _adamw(g, w, m, v):
    m = ADAM_B1 * m + (1.0 - ADAM_B1) * g
    v = ADAM_B2 * v + (1.0 - ADAM_B2) * jnp.square(g)
    m_hat = m / (1.0 - ADAM_B1 ** ADAM_STEP)
    v_hat = v / (1.0 - ADAM_B2 ** ADAM_STEP)
    delta = -ADAM_LR * (m_hat / (jnp.sqrt(v_hat) + ADAM_EPS) + ADAM_WD * w)
    return delta, m, v


def _adam_call(parts, w, m, v, name, tile=128):
    shape = w.shape
    flat = lambda t: t.reshape(-1, shape[-1])
    width = shape[-1]

    def fn(*vals):
        g = vals[0] if len(parts) == 1 else vals[0] + vals[1]
        return (g,) + _adamw(g, *vals[len(parts):])

    rows = [_whole(flat(t)) for t in (*parts, w, m, v)]
    outs = _rowwise(fn, rows, [], [(width, F32)] * 4, name, tile=tile)
    return [o.reshape(shape) for o in outs]


def _sum_slots(r, name, tile=128):
    _, n, width = r.shape
    tile = min(tile, n)

    def body(r_ref, o_ref):
        acc = r_ref[0].astype(F32)
        for j in range(1, N_CHIPS):
            acc = acc + r_ref[j].astype(F32)
        o_ref[...] = acc

    return _pc(body, name=name, grid=(n // tile,), in_specs=[pl.BlockSpec((N_CHIPS, tile, width), lambda i: (0, i, 0))],
               out_specs=pl.BlockSpec((tile, width), lambda i: (i, 0)),
               out_shape=jax.ShapeDtypeStruct((n, width), F32), compiler_params=_cparams())(r)


def _small_adam(g_all, w, m, v):
    def body(g_ref, w_ref, m_ref, v_ref, og, od, om, ov):
        g = g_ref[0]
        for j in range(1, N_DEV):
            g = g + g_ref[j]
        og[...] = g
        od[...], om[...], ov[...] = _adamw(g, w_ref[...], m_ref[...], v_ref[...])

    vm = pl.BlockSpec(memory_space=pltpu.VMEM)
    shp = jax.ShapeDtypeStruct(w.shape, F32)
    return _pc(body, name="small_adam", in_specs=[vm] * 4, out_specs=[vm] * 4, out_shape=[shp] * 4,
               compiler_params=_cparams())(g_all, w, m, v)


def _place():
    return lax.axis_index("x"), lax.axis_index("y"), lax.axis_index("c")


def _flip(v, bit):
    return 1 - v if bit else v


def _all_gather8(a):
    r, n = a.shape

    def body(a_ref, o_ref, send_sems, recv_sems):
        x, y, c = _place()
        me = 4 * x + 2 * y + c
        o_ref[me] = a_ref[...]
        copies = []
        for k in range(1, N_DEV):
            peer = (_flip(x, k & 4), _flip(y, k & 2), _flip(c, k & 1))
            copies.append(pltpu.make_async_remote_copy(
                src_ref=a_ref, dst_ref=o_ref.at[me], send_sem=send_sems.at[k - 1], recv_sem=recv_sems.at[k - 1],
                device_id=peer, device_id_type=MESH))
        for cp in copies:
            cp.start()
        for k in range(1, N_DEV):
            px, py, pc_ = _flip(x, k & 4), _flip(y, k & 2), _flip(c, k & 1)
            pltpu.make_async_remote_copy(
                src_ref=a_ref, dst_ref=o_ref.at[4 * px + 2 * py + pc_], send_sem=send_sems.at[k - 1],
                recv_sem=recv_sems.at[k - 1], device_id=(px, py, pc_), device_id_type=MESH).wait_recv()
        for cp in copies:
            cp.wait_send()

    vm = pl.BlockSpec(memory_space=pltpu.VMEM)
    return _pc(body, name="all_gather8", in_specs=[vm], out_specs=vm,
               out_shape=jax.ShapeDtypeStruct((N_DEV, r, n), a.dtype),
               scratch_shapes=[pltpu.SemaphoreType.DMA((N_DEV - 1,)), pltpu.SemaphoreType.DMA((N_DEV - 1,))],
               compiler_params=_cparams())(a)


def _chip_exchange(arrays, gather, name):
    na = len(arrays)

    def body(*refs):
        ins, outs, sems = refs[:na], refs[na:2 * na], refs[2 * na:]
        _exchange_copies(ins, outs, sems, gather, start=True)
        _exchange_copies(ins, outs, sems, gather, start=False)

    in_specs, out_specs, out_shape, scratch = _exchange_specs(arrays, gather)
    return _pc(body, name=name, in_specs=in_specs, out_specs=out_specs, out_shape=out_shape, scratch_shapes=scratch,
               compiler_params=_cparams())(*arrays)


def _exchange_specs(arrays, gather):
    na = len(arrays)
    hbm = pl.BlockSpec(memory_space=pl.ANY)
    out_shape = [jax.ShapeDtypeStruct(((N_CHIPS,) + a.shape) if gather else a.shape, a.dtype) for a in arrays]
    n_remote = 4 if gather else 2
    scratch = [pltpu.SemaphoreType.DMA((3 * na,))] * n_remote + [pltpu.SemaphoreType.DMA((na,))]
    return [hbm] * na, [hbm] * na, out_shape, scratch


def _gather_copies(ins, outs, sems, start):
    send_i, recv_i, send_d, recv_d, local_sems = sems
    x, y, c = _place()
    me = 2 * x + y
    sibling = (x, y, 1 - c)
    ici_sends, ici_arrivals, hand_ons, hand_arrivals, locals_ = [], [], [], [], []
    for i in range(len(ins)):
        half = ins[i].shape[0] // 2
        mine, other = pl.ds(c * half, half), pl.ds((1 - c) * half, half)
        locals_.append(pltpu.make_async_copy(ins[i], outs[i].at[me], local_sems.at[i]))
        for j in range(1, N_CHIPS):
            px, py = _flip(x, j & 2), _flip(y, j & 1)
            peer = 2 * px + py
            k = i * 3 + j - 1
            ici = dict(send_sem=send_i.at[k], recv_sem=recv_i.at[k], device_id=(px, py, c), device_id_type=MESH)
            d2d = dict(send_sem=send_d.at[k], recv_sem=recv_d.at[k], device_id=sibling, device_id_type=MESH)
            ici_sends.append(pltpu.make_async_remote_copy(src_ref=ins[i].at[mine], dst_ref=outs[i].at[me, mine], **ici))
            ici_arrivals.append(pltpu.make_async_remote_copy(src_ref=ins[i].at[mine], dst_ref=outs[i].at[peer, mine], **ici))
            hand_ons.append(pltpu.make_async_remote_copy(
                src_ref=outs[i].at[peer, mine], dst_ref=outs[i].at[peer, mine], **d2d))
            hand_arrivals.append(pltpu.make_async_remote_copy(
                src_ref=outs[i].at[peer, other], dst_ref=outs[i].at[peer, other], **d2d))
    if start:
        for cp in locals_ + ici_sends:
            cp.start()
    else:
        for arrival, hand_on in zip(ici_arrivals, hand_ons):
            arrival.wait_recv()
            hand_on.start()
        for cp in hand_arrivals:
            cp.wait_recv()
        for cp in ici_sends + hand_ons:
            cp.wait_send()
        for cp in locals_:
            cp.wait()


def _exchange_copies(ins, outs, sems, gather, start):
    if gather:
        return _gather_copies(ins, outs, sems, start)
    send_sems, recv_sems, local_sems = sems
    x, y, c = _place()
    me = 2 * x + y
    sends, arrivals, locals_ = [], [], []
    for i in range(len(ins)):
        locals_.append(pltpu.make_async_copy(ins[i] if gather else ins[i].at[me], outs[i].at[me], local_sems.at[i]))
        for j in range(1, N_CHIPS):
            px, py = _flip(x, j & 2), _flip(y, j & 1)
            peer = 2 * px + py
            pair = dict(send_sem=send_sems.at[i * 3 + j - 1], recv_sem=recv_sems.at[i * 3 + j - 1],
                        device_id=(px, py, c), device_id_type=MESH)
            sends.append(pltpu.make_async_remote_copy(
                src_ref=ins[i] if gather else ins[i].at[peer], dst_ref=outs[i].at[me], **pair))
            arrivals.append(pltpu.make_async_remote_copy(
                src_ref=ins[i] if gather else ins[i].at[me], dst_ref=outs[i].at[peer], **pair))
    if start:
        for cp in locals_ + sends:
            cp.start()
    else:
        for cp in arrivals:
            cp.wait_recv()
        for cp in sends:
            cp.wait_send()
        for cp in locals_:
            cp.wait()


def _carry_exchange(body, n_in, n_out, n_scratch, n_arrays, gather, steps):
    def wrapped(*refs):
        na = n_arrays
        ins, xin = refs[:n_in], refs[n_in:n_in + na]
        outs = refs[n_in + na:n_in + na + n_out]
        xout = refs[n_in + na + n_out:n_in + 2 * na + n_out]
        rest = refs[n_in + 2 * na + n_out:]
        scratch, sems = rest[:n_scratch], rest[n_scratch:]
        step = pl.program_id(0)

        @pl.when(step == 0)
        def _():
            _exchange_copies(xin, xout, sems, gather, start=True)

        body(*ins, *outs, *scratch)

        @pl.when(step == steps - 1)
        def _():
            _exchange_copies(xin, xout, sems, gather, start=False)

    return wrapped


def _sibling_exchange(arrays, name):
    na = len(arrays)

    def body(*refs):
        ins, outs = refs[:na], refs[na:2 * na]
        send_sems, recv_sems = refs[2 * na:]
        x, y, c = _place()
        copies = [pltpu.make_async_remote_copy(
            src_ref=ins[i], dst_ref=outs[i], send_sem=send_sems.at[i], recv_sem=recv_sems.at[i],
            device_id=(x, y, 1 - c), device_id_type=MESH) for i in range(na)]
        for cp in copies:
            cp.start()
        for cp in copies:
            cp.wait()

    hbm = pl.BlockSpec(memory_space=pl.ANY)
    return _pc(body, name=name, in_specs=[hbm] * na, out_specs=[hbm] * na,
               out_shape=[jax.ShapeDtypeStruct(a.shape, a.dtype) for a in arrays],
               scratch_shapes=[pltpu.SemaphoreType.DMA((na,)), pltpu.SemaphoreType.DMA((na,))],
               compiler_params=_cparams())(*arrays)


def _heads_q(t):
    s = t.shape[0]
    return t.reshape(s, ATT_KV_HEADS, ATT_GROUP, ATT_HEAD_DIM).transpose(1, 2, 0, 3)


def _unheads_q(t):
    s = t.shape[2]
    return t.transpose(2, 0, 1, 3).reshape(s, ATT_KV_HEADS * ATT_GROUP * ATT_HEAD_DIM)


def _heads_kv(t):
    s = t.shape[0]
    return t.reshape(s, ATT_KV_HEADS, ATT_HEAD_DIM).transpose(1, 0, 2)


def _unheads_kv(t):
    s = t.shape[1]
    return t.transpose(1, 0, 2).reshape(s, ATT_KV_HEADS * ATT_HEAD_DIM)


def _row128(v):
    return jnp.pad(v, (0, LANES - v.shape[0])).reshape(1, LANES)


def _layer_fwd(x, p, exchange=None):
    sh1, sc1, gt1, sh2, sc2, gt2 = [p["mod"][i] for i in range(6)]
    (u,) = _rowwise(_f_mod, [_whole(x)], [sc1, sh1], [(D_MODEL, BF16)], "mod1")
    proj = _mm(u, p["w_in"], "nn", F32, "proj")
    qh = _heads_q(proj[:, C_Q:C_Q + 1024])
    kh = _heads_kv(proj[:, C_K:C_K + 256])
    vh = _heads_kv(proj[:, C_V:C_V + 256])
    sinks4 = p["sinks"].reshape(ATT_KV_HEADS, ATT_GROUP, 1, 1)
    o_a = _unheads_q(_attn_fwd(qh, kh, vh, sinks4))
    (o_b,), exchanged = _dn_fwd(proj, p["conv_w"], _row128(p["a_log"]), _row128(p["dt_bias"]),
                                p["dn_norm_w"].reshape(1, LANES), exchange)
    y_a = _mm(o_a, p["w_oa"], "nn", F32, "y_a")
    y_b = _mm(o_b, p["w_ob"], "nn", F32, "y_b")
    (gm,) = _rowwise(_f_gate, [(proj, C_GA // 1024, 1024), (proj, C_GB // 1024, 1024), _whole(y_a), _whole(y_b)], [],
                     [(D_MODEL, BF16)], "gate")
    mixed = _mm(gm, p["w_out"], "nn", F32, "mixed")
    x1, u2 = _rowwise(_f_post1, [_whole(x), _whole(mixed)], [gt1, p["ln1_g"], p["ln1_b"], sc2, sh2],
                      [(D_MODEL, F32), (D_MODEL, BF16)], "post1")
    hpre = _mm(u2, p["w_ff1"], "nn", F32, "ff1")
    (h,) = _rowwise(_f_act, [_whole(hpre)], [p["b_ff1"]], [(D_FF, BF16)], "act")
    ff = _mm(h, p["w_ff2"], "nn", F32, "ff2")
    (x2,) = _rowwise(_f_post2, [_whole(x1), _whole(ff)], [gt2, p["b_ff2"], p["ln2_g"], p["ln2_b"]],
                     [(D_MODEL, F32)], "post2")
    saved = dict(x=x, u=u, proj=proj, o_a=o_a, o_b=o_b, y_a=y_a, y_b=y_b, gm=gm, mixed=mixed, x1=x1, u2=u2,
                 hpre=hpre, h=h, ff=ff)
    return x2, saved, exchanged


def _layer_bwd(dx2, p, sv, carry=None):
    sh1, sc1, gt1, sh2, sc2, gt2 = [p["mod"][i] for i in range(6)]
    g = {}
    (dx1_a, dff), (dgt2, g["b_ff2"], g["ln2_g"], g["ln2_b"]) = _rowwise_bwd(
        _f_post2, [_whole(sv["x1"]), _whole(sv["ff"])], [gt2, p["b_ff2"], p["ln2_g"], p["ln2_b"]], [dx2],
        [F32, BF16], "post2_bwd")
    dh = _mm(dff, p["w_ff2"], "nt", F32, "dh")
    g["w_ff2"] = _mm(sv["h"], dff, "tn", F32, "dw_ff2")
    (dhpre,), (g["b_ff1"],) = _rowwise_bwd(_f_act, [_whole(sv["hpre"])], [p["b_ff1"]], [dh], [BF16], "act_bwd")
    du2 = _mm(dhpre, p["w_ff1"], "nt", F32, "du2")
    g["w_ff1"] = _mm(sv["u2"], dhpre, "tn", F32, "dw_ff1")
    (dx_a, dmixed), (dgt1, g["ln1_g"], g["ln1_b"], dsc2, dsh2) = _rowwise_bwd(
        _f_post1, [_whole(sv["x"]), _whole(sv["mixed"])], [gt1, p["ln1_g"], p["ln1_b"], sc2, sh2], [dx1_a, du2],
        [F32, BF16], "post1_bwd")
    dgm = _mm(dmixed, p["w_out"], "nt", F32, "dgm")
    g["w_out"] = _mm(sv["gm"], dmixed, "tn", F32, "dw_out")
    proj = sv["proj"]
    (dga, dgb, dya, dyb), _ = _rowwise_bwd(
        _f_gate, [(proj, C_GA // 1024, 1024), (proj, C_GB // 1024, 1024), _whole(sv["y_a"]), _whole(sv["y_b"])], [],
        [dgm], [BF16, BF16, BF16, BF16], "gate_bwd")
    do_a = _mm(dya, p["w_oa"], "nt", F32, "do_a")
    g["w_oa"] = _mm(sv["o_a"], dya, "tn", F32, "dw_oa")
    do_b = _mm(dyb, p["w_ob"], "nt", F32, "do_b")
    g["w_ob"] = _mm(sv["o_b"], dyb, "tn", F32, "dw_ob")
    exchange = None if carry is None else (carry(g), False)
    (ddq, ddk, ddv, ddz, dba, dwq, dwk, dwv, dalog, ddtb, dnw), exchanged = _dn_bwd(
        proj, p["conv_w"], _row128(p["a_log"]), _row128(p["dt_bias"]), p["dn_norm_w"].reshape(1, LANES), do_b, exchange)
    g["conv_w"] = jnp.concatenate([dwq, dwk, dwv], axis=1)
    g["a_log"], g["dt_bias"], g["dn_norm_w"] = dalog[0, :DN_HEADS], ddtb[0, :DN_HEADS], dnw[0]
    qh = _heads_q(proj[:, C_Q:C_Q + 1024])
    kh = _heads_kv(proj[:, C_K:C_K + 256])
    vh = _heads_kv(proj[:, C_V:C_V + 256])
    sinks4 = p["sinks"].reshape(ATT_KV_HEADS, ATT_GROUP, 1, 1)
    dqh, dkh, dvh, dsk = _attn_bwd(qh, kh, vh, sinks4, _heads_q(do_a))
    g["sinks"] = dsk.reshape(ATT_KV_HEADS * ATT_GROUP)
    s = proj.shape[0]
    dproj = jnp.concatenate([
        _unheads_q(dqh).astype(BF16), ddq, ddk, ddv, ddz, dga, dgb,
        _unheads_kv(dkh[:, WINDOW:, :]).astype(BF16), _unheads_kv(dvh[:, WINDOW:, :]).astype(BF16),
        dba.astype(BF16), jnp.zeros((s, D_IN_P - C_BA - LANES), BF16)], axis=1)
    du = _mm(dproj, p["w_in"], "nt", F32, "du")
    g["w_in"] = _mm(sv["u"], dproj, "tn", F32, "dw_in")
    (dx,), (dsc1, dsh1) = _rowwise_bwd(_f_mod, [_whole(sv["x"])], [sc1, sh1], [du], [F32], "mod1_bwd", add=(0, dx_a))
    g["mod"] = jnp.stack([dsh1, dsc1, dgt1, dsh2, dsc2, dgt2])
    return dx, g, exchanged


def _permute_w_in(w):
    pad = jnp.zeros(w.shape[:-1] + (D_IN_P - D_IN,), w.dtype)
    return jnp.concatenate([w[..., 0:1024], w[..., 1536:5632], w[..., 5648:7696], w[..., 1024:1536],
                            w[..., 5632:5648], pad], axis=-1)


def _unpermute_w_in(g):
    return jnp.concatenate([g[..., 0:1024], g[..., C_K:C_K + 512], g[..., 1024:5120], g[..., C_BA:C_BA + 16],
                            g[..., 5120:7168]], axis=-1)


def _cols_from_chips(t):
    c, l, r, n = t.shape
    return t.transpose(1, 2, 0, 3).reshape(l, r, c * n)


def _cols_to_chips(t):
    l, r, n4 = t.shape
    return t.reshape(l, r, N_CHIPS, n4 // N_CHIPS).transpose(2, 0, 1, 3)


def _rows_from_chips(t):
    c, l, r, n = t.shape
    return t.transpose(1, 0, 2, 3).reshape(l, c * r, n)


def _rows_to_chips(t):
    l, r4, n = t.shape
    return t.reshape(l, N_CHIPS, r4 // N_CHIPS, n).transpose(1, 0, 2, 3)


_REPLICATED = ("b_ada", "a_log", "dt_bias", "sinks", "dn_norm_w", "ln1_g", "ln1_b", "b_ff1", "b_ff2", "ln2_g", "ln2_b")
_SMALL = _REPLICATED + ("conv_w",)
_PACK_W = 1024
_WEIGHT_ORDER = ("w_ada", "b_ada", "w_in", "conv_w", "a_log", "dt_bias", "sinks", "dn_norm_w", "w_oa", "w_ob", "w_out",
                 "ln1_g", "ln1_b", "w_ff1", "b_ff1", "w_ff2", "b_ff2", "ln2_g", "ln2_b")


def _pack_small(d):
    flat = jnp.concatenate([d[k].reshape(-1) for k in _SMALL])
    rows = -(-flat.shape[0] // (_PACK_W * 8)) * 8
    return jnp.pad(flat, (0, rows * _PACK_W - flat.shape[0])).reshape(rows, _PACK_W)


def _unpack_small(packed, shapes):
    flat = packed.reshape(-1)
    out, off = {}, 0
    for k in _SMALL:
        n = 1
        for d_ in shapes[k]:
            n *= d_
        out[k] = flat[off:off + n].reshape(shapes[k])
        off += n
    return out


def kernel(x, c, w_ada, b_ada, w_in, conv_w, a_log, dt_bias, sinks, dn_norm_w, w_oa, w_ob, w_out, ln1_g, ln1_b, w_ff1, b_ff1, w_ff2, b_ff2, ln2_g, ln2_b, loss_target, m_w_ada, m_b_ada, m_w_in, m_conv_w, m_a_log, m_dt_bias, m_sinks, m_dn_norm_w, m_w_oa, m_w_ob, m_w_out, m_ln1_g, m_ln1_b, m_w_ff1, m_b_ff1, m_w_ff2, m_b_ff2, m_ln2_g, m_ln2_b, v_w_ada, v_b_ada, v_w_in, v_conv_w, v_a_log, v_dt_bias, v_sinks, v_dn_norm_w, v_w_oa, v_w_ob, v_w_out, v_ln1_g, v_ln1_b, v_w_ff1, v_b_ff1, v_w_ff2, v_b_ff2, v_ln2_g, v_ln2_b):
    ix, iy, ic = _place()
    chip = 2 * ix + iy
    dev = 4 * ix + 2 * iy + ic
    weights = dict(w_ada=w_ada, b_ada=b_ada, w_in=w_in, conv_w=conv_w, a_log=a_log, dt_bias=dt_bias, sinks=sinks,
                   dn_norm_w=dn_norm_w, w_oa=w_oa, w_ob=w_ob, w_out=w_out, ln1_g=ln1_g, ln1_b=ln1_b, w_ff1=w_ff1,
                   b_ff1=b_ff1, w_ff2=w_ff2, b_ff2=b_ff2, ln2_g=ln2_g, ln2_b=ln2_b)
    mom_m = dict(w_ada=m_w_ada, b_ada=m_b_ada, w_in=m_w_in, conv_w=m_conv_w, a_log=m_a_log, dt_bias=m_dt_bias,
                 sinks=m_sinks, dn_norm_w=m_dn_norm_w, w_oa=m_w_oa, w_ob=m_w_ob, w_out=m_w_out, ln1_g=m_ln1_g,
                 ln1_b=m_ln1_b, w_ff1=m_w_ff1, b_ff1=m_b_ff1, w_ff2=m_w_ff2, b_ff2=m_b_ff2, ln2_g=m_ln2_g, ln2_b=m_ln2_b)
    mom_v = dict(w_ada=v_w_ada, b_ada=v_b_ada, w_in=v_w_in, conv_w=v_conv_w, a_log=v_a_log, dt_bias=v_dt_bias,
                 sinks=v_sinks, dn_norm_w=v_dn_norm_w, w_oa=v_w_oa, w_ob=v_w_ob, w_out=v_w_out, ln1_g=v_ln1_g,
                 ln1_b=v_ln1_b, w_ff1=v_w_ff1, b_ff1=v_b_ff1, w_ff2=v_w_ff2, b_ff2=v_b_ff2, ln2_g=v_ln2_g, ln2_b=v_ln2_b)

    n_ada = w_ada.shape[2]
    c_all = _all_gather8(jnp.pad(c, ((0, 7), (0, 0))))[:, 0, :]
    b_shard = lax.dynamic_slice_in_dim(b_ada, chip * n_ada, n_ada, axis=1).reshape(DEPTH, 1, n_ada)
    mod_t = _ada_fwd(c_all, w_ada, b_shard)
    mod_all = _all_gather8(mod_t.reshape(DEPTH * N_DEV, n_ada)).reshape(N_DEV, DEPTH, N_DEV, n_ada)
    mod_mine = lax.dynamic_index_in_dim(mod_all[0::2], dev, axis=2, keepdims=False)
    mod = mod_mine.transpose(1, 0, 2).reshape(DEPTH, 6, 1, D_MODEL)

    n_cw = conv_w.shape[2]
    cw_all = _all_gather8(conv_w.reshape(DEPTH * CONV_K, n_cw))[0::2]
    conv_full = cw_all.transpose(1, 0, 2).reshape(DEPTH, CONV_K, N_CHIPS * n_cw)

    big = ("w_in", "w_oa", "w_ob", "w_out", "w_ff1", "w_ff2")
    w16 = {k: weights[k].astype(BF16) for k in big}
    shards = lambda l: [w16[k][l] for k in big]

    def assemble(gathered):
        gw = {k: t[:, None] for k, t in zip(big, gathered)}
        full = dict(w_in=_permute_w_in(_cols_from_chips(gw["w_in"])), w_ff1=_cols_from_chips(gw["w_ff1"]),
                    w_oa=_rows_from_chips(gw["w_oa"]), w_ob=_rows_from_chips(gw["w_ob"]),
                    w_out=_rows_from_chips(gw["w_out"]), w_ff2=_rows_from_chips(gw["w_ff2"]))
        return {k: t[0] for k, t in full.items()}

    to_chips = dict(w_in=lambda t: _cols_to_chips(_unpermute_w_in(t)), w_ff1=_cols_to_chips, w_oa=_rows_to_chips,
                    w_ob=_rows_to_chips, w_out=_rows_to_chips, w_ff2=_rows_to_chips)

    def slices_for_chips(g, keys):
        return [to_chips[k](g[k][None])[:, 0].astype(BF16) for k in keys]

    full = [None] * DEPTH
    full[0] = assemble(_chip_exchange(shards(0), True, "gather_weights"))

    def layer_params(l):
        p = dict(full[l])
        p["mod"] = mod[l]
        p["conv_w"] = conv_full[l]
        for k in ("a_log", "dt_bias", "sinks", "dn_norm_w"):
            p[k] = weights[k][l]
        for k in ("ln1_g", "ln1_b", "b_ff1", "b_ff2", "ln2_g", "ln2_b"):
            p[k] = weights[k][l].reshape(1, -1)
        return p

    xs = x[0]
    saved = []
    for l in range(DEPTH):
        nxt = (shards(l + 1), True) if l + 1 < DEPTH else None
        xs, sv, gathered = _layer_fwd(xs, layer_params(l), nxt)
        if nxt is not None:
            full[l + 1] = assemble(gathered)
        saved.append(sv)
    dy, loss_local = _loss_head(xs, loss_target[0])
    loss = lax.psum(loss_local[0, 0], ("x", "y", "c"))
    early = tuple(k for k in big if k != "w_in")
    grads = [None] * DEPTH
    received = [dict() for _ in range(DEPTH)]
    dx = dy
    pending = []
    for l in reversed(range(DEPTH)):
        carry = functools.partial(lambda g, first: first + slices_for_chips(g, early), first=pending)
        dx, grads[l], got = _layer_bwd(dx, layer_params(l), saved[l], carry)
        if pending:
            received[l + 1]["w_in"] = got[0]
        received[l].update(zip(early, got[len(pending):]))
        pending = slices_for_chips(grads[l], ("w_in",))
    received[0]["w_in"] = _chip_exchange(pending, False, "scatter_grads")[0]
    grad_x = dx[None]
    gstack = {k: jnp.stack([grads[l][k] for l in range(DEPTH)]) for k in grads[0] if k not in big}

    dmod = gstack["mod"].reshape(DEPTH, 6 * D_MODEL)
    small_g = dict(b_ada=dmod, a_log=gstack["a_log"], dt_bias=gstack["dt_bias"], sinks=gstack["sinks"],
                   dn_norm_w=gstack["dn_norm_w"], ln1_g=gstack["ln1_g"], ln1_b=gstack["ln1_b"], b_ff1=gstack["b_ff1"],
                   b_ff2=gstack["b_ff2"], ln2_g=gstack["ln2_g"], ln2_b=gstack["ln2_b"], conv_w=gstack["conv_w"])
    shapes = {k: weights[k].shape for k in _REPLICATED}
    shapes["conv_w"] = small_g["conv_w"].shape
    g_all = _all_gather8(_pack_small(small_g))
    no_conv = jnp.zeros(shapes["conv_w"], F32)
    small_out = _small_adam(g_all, _pack_small(dict(weights, conv_w=no_conv)), _pack_small(dict(mom_m, conv_w=no_conv)),
                            _pack_small(dict(mom_v, conv_w=no_conv)))
    small_res = [_unpack_small(t, shapes) for t in small_out]
    g_conv = lax.dynamic_slice_in_dim(small_res[0]["conv_w"], chip * n_cw, n_cw, axis=2)
    res = {"conv_w": _adam_call([g_conv], conv_w, m_conv_w, v_conv_w, "adam_conv_w", tile=16)}

    dmod_all = g_all.reshape(N_DEV, -1)[:, :DEPTH * 6 * D_MODEL].reshape(N_DEV, DEPTH, 6 * D_MODEL)
    dmod_shard = lax.dynamic_slice_in_dim(dmod_all, chip * n_ada, n_ada, axis=2).transpose(1, 0, 2)
    g_w_ada = _ada_bwd(c_all, dmod_shard)
    res["w_ada"] = _adam_call([g_w_ada], w_ada, m_w_ada, v_w_ada, "adam_w_ada")

    by_weight = [jnp.stack([received[l][k] for l in range(DEPTH)], axis=1) for k in big]
    partial = [_sum_slots(r.reshape(N_CHIPS, -1, r.shape[-1]), "sum_" + k) for k, r in zip(big, by_weight)]
    theirs = _sibling_exchange(partial, "sibling_grads")
    for k, mine, other in zip(big, partial, theirs):
        shape = weights[k].shape
        res[k] = _adam_call([mine.reshape(shape), other.reshape(shape)], weights[k], mom_m[k], mom_v[k], "adam_" + k)
    for k in _REPLICATED:
        res[k] = [small_res[i][k] for i in range(4)]

    outs = [loss, grad_x]
    for i in range(4):
        outs += [res[k][i] for k in _WEIGHT_ORDER]
    return tuple(outs)
```

```python
import functools

import jax
import jax.numpy as jnp
from jax import lax
from jax.experimental import pallas as pl
from jax.experimental.pallas import tpu as pltpu

F32, BF16 = jnp.float32, jnp.bfloat16
HI = lax.Precision.HIGHEST
MESH = pl.DeviceIdType.MESH

D_MODEL = 1024
DEPTH = 4
ATT_KV_HEADS, ATT_GROUP, ATT_HEAD_DIM, WINDOW = 4, 4, 64, 128
DN_HEADS, DN_HEAD_DIM, CONV_K, CHUNK = 8, 128, 4, 64
D_FF = 4 * D_MODEL
D_IN = 7696
ALPHA = (2 * DEPTH) ** 0.25
LN_EPS = 1e-5
RMS_EPS = 1e-6
ADAM_LR, ADAM_B1, ADAM_B2, ADAM_EPS, ADAM_WD, ADAM_STEP = 0.001, 0.9, 0.999, 1e-08, 0.01, 10

N_CHIPS = 4
N_DEV = 8
LANES = 128
D_IN_P = 8192
C_Q, C_DQ, C_DK, C_DV, C_Z, C_GA, C_GB, C_K, C_V, C_BA = 0, 1024, 2048, 3072, 4096, 5120, 6144, 7168, 7424, 7680
NEG = -1e30
VMEM_LIMIT = 56 << 20


def _pc(body, **kw):
    return pl.pallas_call(body, **kw)


def _cparams(sem=None):
    if sem is None:
        return pltpu.CompilerParams(vmem_limit_bytes=VMEM_LIMIT)
    return pltpu.CompilerParams(vmem_limit_bytes=VMEM_LIMIT, dimension_semantics=sem)


_MM_VMEM_BUDGET = 44 << 20
_MM_MIN_TILE = 256


def _mm_tiles(m, n, k, out_bytes):
    def halvings(d):
        out = [d]
        while out[-1] % 2 == 0 and out[-1] // 2 >= _MM_MIN_TILE:
            out.append(out[-1] // 2)
        return out

    best = None
    for tm in halvings(m):
        for tn in halvings(n):
            if 2 * (2 * tm * k + 2 * tn * k + out_bytes * tm * tn) > _MM_VMEM_BUDGET:
                continue
            cost = (2 * m * k + (m // tm) * 2 * n * k, (m // tm) * (n // tn))
            if best is None or cost < best[0]:
                best = (cost, tm, tn)
    assert best is not None, (m, n, k)
    return best[1], best[2]


def _mm(a, b, mode, out_dtype, name, tm=None, tn=None):
    if mode == "nn":
        (m, k), (_, n) = a.shape, b.shape
        dims = (((1,), (0,)), ((), ()))
    elif mode == "nt":
        (m, k), (n, _) = a.shape, b.shape
        dims = (((1,), (1,)), ((), ()))
    else:
        (k, m), (_, n) = a.shape, b.shape
        dims = (((0,), (0,)), ((), ()))
    if tm is None:
        tm, tn = _mm_tiles(m, n, k, jnp.dtype(out_dtype).itemsize)
    tm, tn = min(tm, m), min(tn, n)
    assert m % tm == 0 and n % tn == 0, (name, m, n, tm, tn)
    a_spec = pl.BlockSpec((k, tm), lambda i, j: (0, i)) if mode == "tn" else pl.BlockSpec((tm, k), lambda i, j: (i, 0))
    b_spec = pl.BlockSpec((tn, k), lambda i, j: (j, 0)) if mode == "nt" else pl.BlockSpec((k, tn), lambda i, j: (0, j))

    def body(a_ref, b_ref, o_ref):
        o_ref[...] = lax.dot_general(a_ref[...], b_ref[...], dims, preferred_element_type=F32).astype(o_ref.dtype)

    return _pc(body, name=name, grid=(m // tm, n // tn), in_specs=[a_spec, b_spec],
               out_specs=pl.BlockSpec((tm, tn), lambda i, j: (i, j)),
               out_shape=jax.ShapeDtypeStruct((m, n), out_dtype), compiler_params=_cparams())(a, b)


def _row_specs(rows, tile):
    return [pl.BlockSpec((tile, w), functools.partial(lambda i, cb: (i, cb), cb=cb)) for (_, cb, w) in rows]


def _vec_specs(vecs):
    return [pl.BlockSpec(v.shape, lambda i: (0, 0)) for v in vecs]


def _rowwise(fn, rows, vecs, outs, name, tile=256):
    n = rows[0][0].shape[0]
    tile = min(tile, n)
    nr, nv = len(rows), len(vecs)

    def body(*refs):
        rv = [r[...].astype(F32) for r in refs[:nr]]
        vv = [r[...] for r in refs[nr:nr + nv]]
        for o_ref, val in zip(refs[nr + nv:], fn(*rv, *vv)):
            o_ref[...] = val.astype(o_ref.dtype)

    res = _pc(body, name=name, grid=(n // tile,), in_specs=_row_specs(rows, tile) + _vec_specs(vecs),
              out_specs=[pl.BlockSpec((tile, w), lambda i: (i, 0)) for (w, _) in outs],
              out_shape=[jax.ShapeDtypeStruct((n, w), dt) for (w, dt) in outs],
              compiler_params=_cparams())(*[r[0] for r in rows], *vecs)
    return res


def _rowwise_bwd(fn, rows, vecs, cts, row_dtypes, name, tile=256, add=None):
    n = rows[0][0].shape[0]
    tile = min(tile, n)
    nr, nv, nc = len(rows), len(vecs), len(cts)
    want = [i for i, dt in enumerate(row_dtypes) if dt is not None]
    n_add = 0 if add is None else 1

    def body(*refs):
        rv = [r[...].astype(F32) for r in refs[:nr]]
        vv = [r[...] for r in refs[nr:nr + nv]]
        cv = [r[...].astype(F32) for r in refs[nr + nv:nr + nv + nc]]
        pos = nr + nv + nc
        add_ref = refs[pos] if n_add else None
        pos += n_add
        row_out = refs[pos:pos + len(want)]
        vec_out = refs[pos + len(want):]
        _, vjp = jax.vjp(fn, *rv, *vv)
        grads = vjp(tuple(cv))
        for o_ref, i in zip(row_out, want):
            gval = grads[i]
            if n_add and add[0] == i:
                gval = gval + add_ref[...]
            o_ref[...] = gval.astype(o_ref.dtype)

        @pl.when(pl.program_id(0) == 0)
        def _():
            for o_ref in vec_out:
                o_ref[...] = jnp.zeros_like(o_ref)

        for o_ref, gval in zip(vec_out, grads[nr:]):
            o_ref[...] += gval

    ct_rows = [(c, 0, c.shape[1]) for c in cts]
    add_rows = [(add[1], 0, add[1].shape[1])] if n_add else []
    res = _pc(body, name=name, grid=(n // tile,),
              in_specs=_row_specs(rows, tile) + _vec_specs(vecs) + _row_specs(ct_rows + add_rows, tile),
              out_specs=[pl.BlockSpec((tile, rows[i][2]), lambda i_: (i_, 0)) for i in want] + _vec_specs(vecs),
              out_shape=[jax.ShapeDtypeStruct((n, rows[i][2]), row_dtypes[i]) for i in want]
              + [jax.ShapeDtypeStruct(v.shape, F32) for v in vecs],
              compiler_params=_cparams(("arbitrary",)))(*[r[0] for r in rows], *vecs, *cts, *[a[0] for a in add_rows])
    return res[:len(want)], res[len(want):]


def _whole(a, cb=0, w=None):
    return (a, cb, a.shape[1] if w is None else w)


def _ln(x, g, b):
    mu = jnp.mean(x, axis=-1, keepdims=True)
    var = jnp.mean(jnp.square(x - mu), axis=-1, keepdims=True)
    return (x - mu) * lax.rsqrt(var + LN_EPS) * g + b


def _silu(x):
    return x * jax.nn.sigmoid(x)


def _softplus(x):
    return jnp.maximum(x, 0.0) + jnp.log(1.0 + jnp.exp(-jnp.abs(x)))


def _f_mod(x, sc, sh):
    return (x * (1.0 + sc) + sh,)


def _f_gate(ga, gb, ya, yb):
    return (jax.nn.sigmoid(ga) * ya + jax.nn.sigmoid(gb) * yb,)


def _f_post1(x, mixed, gt, g1, b1, sc2, sh2):
    x1 = _ln(ALPHA * x + (1.0 + gt) * mixed, g1, b1)
    return x1, x1 * (1.0 + sc2) + sh2


def _f_act(hpre, b):
    return (jnp.square(jnp.maximum(hpre + b, 0.0)),)


def _f_post2(x1, ff, gt, bff2, g2, b2):
    return (_ln(ALPHA * x1 + (1.0 + gt) * (ff + bff2), g2, b2),)


def _attn_valid(n):
    qi = lax.broadcasted_iota(jnp.int32, (WINDOW, 2 * WINDOW), 0)
    si = lax.broadcasted_iota(jnp.int32, (WINDOW, 2 * WINDOW), 1)
    diff = qi + WINDOW - si
    return (diff >= 0) & (diff < WINDOW) & (n * WINDOW + si - WINDOW >= 0)


def _attn_block(qs, kp, kc, vp, vc, sinks, valid):
    kband = jnp.concatenate([kp, kc], axis=0).astype(BF16)
    vband = jnp.concatenate([vp, vc], axis=0).astype(BF16)
    rng = range(len(qs))
    s = [lax.dot_general(qs[g].astype(BF16), kband, (((1,), (1,)), ((), ())), preferred_element_type=F32) for g in rng]
    s = [jnp.where(valid, s[g] * (ATT_HEAD_DIM ** -0.5), NEG) for g in rng]
    m = [lax.stop_gradient(jnp.maximum(jnp.max(s[g], axis=-1, keepdims=True), sinks[g])) for g in rng]
    p = [jnp.exp(s[g] - m[g]) for g in rng]
    denom = [jnp.sum(p[g], axis=-1, keepdims=True) + jnp.exp(sinks[g] - m[g]) for g in rng]
    probs = [(p[g] / denom[g]).astype(BF16) for g in rng]
    return [jnp.dot(probs[g], vband, preferred_element_type=F32) for g in rng]


def _attn_specs(s):
    nb = s // WINDOW
    q_spec = pl.BlockSpec((1, ATT_GROUP, WINDOW, ATT_HEAD_DIM), lambda h, n: (h, 0, n, 0))
    prev = pl.BlockSpec((1, WINDOW, ATT_HEAD_DIM), lambda h, n: (h, jnp.maximum(n - 1, 0), 0))
    cur = pl.BlockSpec((1, WINDOW, ATT_HEAD_DIM), lambda h, n: (h, n, 0))
    sk = pl.BlockSpec((1, ATT_GROUP, 1, 1), lambda h, n: (h, 0, 0, 0))
    return nb, q_spec, prev, cur, sk


def _attn_fwd(qh, kh, vh, sinks4, exchange=None):
    s = qh.shape[2]
    nb, q_spec, prev, cur, sk = _attn_specs(s)

    def body(q_ref, kp_ref, kc_ref, vp_ref, vc_ref, sk_ref, o_ref):
        valid = _attn_valid(pl.program_id(1))
        heads = range(ATT_GROUP)
        o = _attn_block([q_ref[0, g] for g in heads], kp_ref[0], kc_ref[0], vp_ref[0], vc_ref[0],
                        [sk_ref[0, g] for g in heads], valid)
        for g in heads:
            o_ref[0, g] = o[g].astype(o_ref.dtype)

    (o,), exchanged = _call_with_exchange(
        body, "attn_fwd", (ATT_KV_HEADS, nb), [q_spec, prev, cur, prev, cur, sk], [q_spec],
        [jax.ShapeDtypeStruct(qh.shape, BF16)], [], (qh, kh, kh, vh, vh, sinks4), exchange)
    return o, exchanged


def _attn_bwd(qh, kh, vh, sinks4, doh):
    s = qh.shape[2]
    nb, q_spec, prev, cur, sk = _attn_specs(s)
    acc = pl.BlockSpec((1, s + WINDOW, ATT_HEAD_DIM), lambda h, n: (h, 0, 0))

    def body(q_ref, kp_ref, kc_ref, vp_ref, vc_ref, sk_ref, do_ref, dq_ref, dk_ref, dv_ref, dsk_ref):
        n = pl.program_id(1)
        valid = _attn_valid(n)
        fn = functools.partial(_attn_block, valid=valid)
        heads = range(ATT_GROUP)
        _, vjp = jax.vjp(fn, [q_ref[0, g] for g in heads], kp_ref[0], kc_ref[0], vp_ref[0], vc_ref[0],
                         [sk_ref[0, g] for g in heads])
        dq, dkp, dkc, dvp, dvc, dsk = vjp([do_ref[0, g].astype(F32) for g in heads])
        for g in heads:
            dq_ref[0, g] = dq[g]

        @pl.when(n == 0)
        def _():
            dk_ref[...] = jnp.zeros_like(dk_ref)
            dv_ref[...] = jnp.zeros_like(dv_ref)
            dsk_ref[...] = jnp.zeros_like(dsk_ref)

        band = pl.ds(pl.multiple_of(n * WINDOW, WINDOW), 2 * WINDOW)
        dk_ref[0, band, :] += jnp.concatenate([dkp, dkc], axis=0)
        dv_ref[0, band, :] += jnp.concatenate([dvp, dvc], axis=0)
        for g in heads:
            dsk_ref[0, g] += dsk[g]

    kv_shape = jax.ShapeDtypeStruct((ATT_KV_HEADS, s + WINDOW, ATT_HEAD_DIM), F32)
    return _pc(body, name="attn_bwd", grid=(ATT_KV_HEADS, nb), in_specs=[q_spec, prev, cur, prev, cur, sk, q_spec],
               out_specs=[q_spec, acc, acc, sk],
               out_shape=[jax.ShapeDtypeStruct(qh.shape, F32), kv_shape, kv_shape, jax.ShapeDtypeStruct(sinks4.shape, F32)],
               compiler_params=_cparams(("arbitrary", "arbitrary")))(qh, kh, kh, vh, vh, sinks4, doh)


def _bdot(a, b, dims=(((1,), (0,)), ((), ()))):
    return lax.dot_general(a.astype(BF16), b.astype(BF16), dims, preferred_element_type=F32)


def _hdot(a, b, dims=(((1,), (0,)), ((), ()))):
    return lax.dot_general(a, b, dims, precision=HI, preferred_element_type=F32)


_NN = (((1,), (0,)), ((), ()))
_NT = (((1,), (1,)), ((), ()))
_TN = (((0,), (0,)), ((), ()))


def _split2(a):
    hi = a.astype(BF16)
    return hi, (a - hi.astype(F32)).astype(BF16)


def _dot3(a, b, dims):
    ah, al = _split2(a)
    bh, bl = _split2(b)
    d = lambda p, q: lax.dot_general(p, q, dims, preferred_element_type=F32)
    return d(ah, bh) + (d(ah, bl) + d(al, bh))


@jax.custom_vjp
def _xdot(a, b):
    return _dot3(a, b, _NN)


def _xdot_fwd(a, b):
    return _dot3(a, b, _NN), (a, b)


def _xdot_bwd(res, g):
    a, b = res
    return _dot3(g, b, _NT), _dot3(a, g, _TN)


_xdot.defvjp(_xdot_fwd, _xdot_bwd)


def _mask_dot(mask16, b, dims):
    hi = b.astype(BF16)
    r = b - hi.astype(F32)
    mid = r.astype(BF16)
    lo = (r - mid.astype(F32)).astype(BF16)
    d = lambda q: lax.dot_general(mask16, q, dims, preferred_element_type=F32)
    return d(hi) + (d(mid) + d(lo))


def _chunk_masks():
    r = lax.broadcasted_iota(jnp.int32, (CHUNK, CHUNK), 0)
    c = lax.broadcasted_iota(jnp.int32, (CHUNK, CHUNK), 1)
    return r >= c, r > c, (r == c).astype(F32)


def _dn_local(qs, ks, vs, bs, gs, masks):
    causal, strict, eye = masks
    rng = range(len(qs))
    gb = [jnp.broadcast_to(gs[i], (CHUNK, CHUNK)) for i in rng]
    decay = [jnp.exp(jnp.where(causal, gb[i] - gb[i].T, NEG)) for i in rng]
    kb = [ks[i] * bs[i] for i in rng]
    vb = [vs[i] * bs[i] for i in rng]
    kk = [_bdot(kb[i], ks[i], _NT) for i in rng]
    p = [-jnp.where(strict, kk[i] * decay[i], 0.0) for i in rng]
    t = [eye + p[i] for i in rng]
    for _ in range(5):
        p = [_xdot(p[i], p[i]) for i in rng]
        t = [t[i] + _xdot(p[i], t[i]) for i in rng]
    eg = [jnp.exp(gs[i]) for i in rng]
    u = [_xdot(t[i], vb[i]) for i in rng]
    w = [_xdot(t[i], kb[i] * eg[i]) for i in rng]
    qk = [_bdot(qs[i], ks[i], _NT) for i in rng]
    intra = [qk[i] * decay[i] for i in rng]
    q_dec = [qs[i] * eg[i] for i in rng]
    k_dec = [ks[i] * jnp.exp(gs[i][CHUNK - 1:CHUNK, :] - gs[i]) for i in rng]
    return u, w, intra, q_dec, k_dec


def _dn_state(u, w, intra, q_dec, k_dec, gcum, state):
    v_new = u - _bdot(w, state)
    o = _bdot(q_dec, state) + _bdot(intra, v_new)
    new_state = state * jnp.exp(gcum[CHUNK - 1:CHUNK, :]) + _bdot(k_dec, v_new, _TN)
    return o, new_state


def _l2norm(t):
    return t * lax.rsqrt(jnp.sum(jnp.square(t), axis=-1, keepdims=True) + RMS_EPS)


def _dn_pre(aq, ak, av, ba, alog, dtb, h):
    lane = lax.broadcasted_iota(jnp.int32, (1, LANES), 1)
    pick = lambda t, i: jnp.sum(jnp.where(lane == i, t, 0.0), axis=1, keepdims=True)
    q = _l2norm(_silu(aq)) * (DN_HEAD_DIM ** -0.5)
    k = _l2norm(_silu(ak))
    v = _silu(av)
    beta = jax.nn.sigmoid(pick(ba, h))
    g = -jnp.exp(pick(alog, h)) * _softplus(pick(ba, h + DN_HEADS) + pick(dtb, h))
    return q, k, v, beta, g


def _dn_post(o, z, nw):
    o = o * lax.rsqrt(jnp.mean(jnp.square(o), axis=-1, keepdims=True) + RMS_EPS) * nw
    return o * _silu(z)


_PAD = 8
_TOK_TILE = 512


def _pad_front(pad_ref, x_ref, s):
    pad_ref[pl.ds(0, _PAD), :] = jnp.zeros((_PAD, pad_ref.shape[1]), F32)
    pad_ref[pl.ds(_PAD, s), :] = x_ref[...]


def _conv_tile(pad_ref, w4, r0, n):
    acc = None
    for j in range(CONV_K):
        term = pad_ref[pl.ds(r0 + _PAD - (CONV_K - 1) + j, n), :] * w4[j:j + 1, :]
        acc = term if acc is None else acc + term
    return acc


def _conv_tile_bwd(pad_ref, da_ref, w4, r0, n):
    dx, dw = None, []
    da = da_ref[pl.ds(r0, n), :]
    for j in range(CONV_K):
        term = da_ref[pl.ds(r0 + CONV_K - 1 - j, n), :] * w4[j:j + 1, :]
        dx = term if dx is None else dx + term
        dw.append(jnp.sum(da * pad_ref[pl.ds(r0 + _PAD - (CONV_K - 1) + j, n), :], axis=0, keepdims=True))
    return dx, jnp.concatenate(dw, axis=0)


def _dn_gcum(g_c, causal_f):
    return _mask_dot(causal_f, jnp.broadcast_to(g_c, (CHUNK, LANES)), _NN)[:, 0:1]


def _dn_in_specs(s):
    col = lambda base: pl.BlockSpec((s, DN_HEAD_DIM), functools.partial(lambda h, b: (0, b + h), b=base // DN_HEAD_DIM))
    cw = lambda base: pl.BlockSpec((CONV_K, DN_HEAD_DIM), functools.partial(lambda h, b: (0, b + h), b=base))
    row = pl.BlockSpec((1, LANES), lambda h: (0, 0))
    ba = pl.BlockSpec((s, LANES), lambda h: (0, C_BA // LANES))
    return [col(C_DQ), col(C_DK), col(C_DV), col(C_Z), ba, cw(0), cw(DN_HEADS), cw(2 * DN_HEADS), row, row, row]


def _chunk_rows(c):
    return pl.ds(pl.multiple_of(c * CHUNK, CHUNK), CHUNK)


def _group(nchunk, want):
    g = min(want, nchunk)
    assert nchunk % g == 0
    return g


def _dn_forward_scan(q_s, k_s, v_s, b_s, g_s, gc_s, loc, o_s, states_ref, s):
    masks = _chunk_masks()
    causal_f = masks[0].astype(BF16)
    nchunk = s // CHUNK
    grp = _group(nchunk, 4)
    u_s, w_s, in_s, qd_s, kd_s = loc

    def local_step(i, carry):
        rows = [_chunk_rows(i * grp + j) for j in range(grp)]
        gcum = [_dn_gcum(g_s[r, :], causal_f) for r in rows]
        u, w, intra, q_dec, k_dec = _dn_local([q_s[r, :] for r in rows], [k_s[r, :] for r in rows],
                                              [v_s[r, :] for r in rows], [b_s[r, :] for r in rows], gcum, masks)
        for j, r in enumerate(rows):
            gc_s[r, :] = gcum[j]
            u_s[r, :] = u[j]
            w_s[r, :] = w[j].astype(w_s.dtype)
            in_s[r, :] = intra[j].astype(in_s.dtype)
            qd_s[r, :] = q_dec[j].astype(qd_s.dtype)
            kd_s[r, :] = k_dec[j].astype(kd_s.dtype)
        return carry

    lax.fori_loop(0, nchunk // grp, local_step, 0)

    def state_step(i, state):
        rows = _chunk_rows(i)
        if states_ref is not None:
            states_ref[i] = state
        o, state = _dn_state(u_s[rows, :], w_s[rows, :], in_s[rows, :], qd_s[rows, :], kd_s[rows, :], gc_s[rows, :], state)
        o_s[rows, :] = o
        return state

    lax.fori_loop(0, nchunk, state_step, jnp.zeros((DN_HEAD_DIM, DN_HEAD_DIM), F32))


def _dn_saved_shapes(s):
    d, h = DN_HEAD_DIM, DN_HEADS
    shapes = [((h, s, d), F32), ((h, s, d), BF16), ((h, s, CHUNK), BF16), ((h, s, d), BF16), ((h, s, d), BF16),
              ((h, s, 1), F32), ((h, s // CHUNK, d, d), F32), ((h, s, d), F32)]
    return [jax.ShapeDtypeStruct(shp, dt) for shp, dt in shapes]


def _dn_saved_specs(s, **kw):
    return [pl.BlockSpec((1,) + t.shape[1:], functools.partial(lambda h, nd: (h,) + (0,) * nd, nd=len(t.shape) - 1), **kw)
            for t in _dn_saved_shapes(s)]


def _call_with_exchange(body, name, steps, in_specs, out_specs, out_shape, scratch, args, exchange):
    steps = (steps,) if isinstance(steps, int) else tuple(steps)
    params = _cparams(("arbitrary",) * len(steps))
    if exchange is None:
        res = _pc(body, name=name, grid=steps, in_specs=in_specs, out_specs=out_specs, out_shape=out_shape,
                  scratch_shapes=scratch, compiler_params=params)(*args)
        return res, None
    arrays, gather = exchange
    x_in, x_out, x_shape, x_scratch = _exchange_specs(arrays, gather)
    wrapped = _carry_exchange(body, len(in_specs), len(out_specs), len(scratch), len(arrays), gather, steps)
    res = _pc(wrapped, name=name + "_x", grid=steps, in_specs=in_specs + x_in, out_specs=out_specs + x_out,
              out_shape=out_shape + x_shape, scratch_shapes=scratch + x_scratch, compiler_params=params)(*args, *arrays)
    return res[:len(out_specs)], res[len(out_specs):]


def _dn_fwd(proj, conv_w, alog, dtb, nw, exchange=None):
    s = proj.shape[0]
    d = DN_HEAD_DIM

    tt = min(_TOK_TILE, s)

    def body(xq, xk, xv, z, ba, wq, wk, wv, alog_r, dtb_r, nw_r, o_ref, u_o, w_o, in_o, qd_o, kd_o, gc_o, st_o, oraw_o,
             padq, padk, padv, q_s, k_s, v_s, b_s, g_s):
        h = pl.program_id(0)
        loc = [r.at[0] for r in (u_o, w_o, in_o, qd_o, kd_o)]
        gc_s, states, o_s = gc_o.at[0], st_o.at[0], oraw_o.at[0]
        _pad_front(padq, xq, s)
        _pad_front(padk, xk, s)
        _pad_front(padv, xv, s)
        for r0 in range(0, s, tt):
            rows = pl.ds(r0, tt)
            aq, ak, av = _conv_tile(padq, wq[...], r0, tt), _conv_tile(padk, wk[...], r0, tt), _conv_tile(padv, wv[...], r0, tt)
            q_s[rows, :], k_s[rows, :], v_s[rows, :], b_s[rows, :], g_s[rows, :] = _dn_pre(
                aq, ak, av, ba[rows, :], alog_r[...], dtb_r[...], h)
        _dn_forward_scan(q_s, k_s, v_s, b_s, g_s, gc_s, loc, o_s, states, s)
        for r0 in range(0, s, tt):
            rows = pl.ds(r0, tt)
            o_ref[rows, :] = _dn_post(o_s[rows, :], z[rows, :], nw_r[...]).astype(o_ref.dtype)

    big = pltpu.VMEM((s, d), F32)
    thin = pltpu.VMEM((s, 1), F32)
    padded = pltpu.VMEM((s + _PAD, d), F32)
    return _call_with_exchange(
        body, "dn_fwd", DN_HEADS, _dn_in_specs(s), [pl.BlockSpec((s, d), lambda h: (0, h))] + _dn_saved_specs(s),
        [jax.ShapeDtypeStruct((s, DN_HEADS * d), BF16)] + _dn_saved_shapes(s),
        [padded, padded, padded, big, big, big, thin, thin],
        (proj, proj, proj, proj, proj, conv_w, conv_w, conv_w, alog, dtb, nw), exchange)


def _dn_bwd(proj, conv_w, alog, dtb, nw, dob, kept, exchange=None):
    s = proj.shape[0]
    d = DN_HEAD_DIM
    nchunk = s // CHUNK

    tt = min(_TOK_TILE, s)

    def body(xq, xk, xv, z, ba, wq, wk, wv, alog_r, dtb_r, nw_r, dob_ref, u_i, w_i, in_i, qd_i, kd_i, gc_i, st_i, oraw_i,
             dxq, dxk, dxv, dz, dba, dwq, dwk, dwv, dalog, ddtb, dnw,
             padq, padk, padv, q_s, k_s, v_s, b_s, g_s, o_s, dq_s, dk_s, dv_s, db_s, dg_s, dkd_s, din_s, dgc_s):
        h = pl.program_id(0)
        u_s, w_s, in_s, qd_s, kd_s = [r.at[0] for r in (u_i, w_i, in_i, qd_i, kd_i)]
        gc_s, states, oraw = gc_i.at[0], st_i.at[0], oraw_i.at[0]
        masks = _chunk_masks()
        causal_f = masks[0].astype(BF16)
        pre = functools.partial(_dn_pre, h=h)
        _pad_front(padq, xq, s)
        _pad_front(padk, xk, s)
        _pad_front(padv, xv, s)

        def conv_tiles(r0):
            return _conv_tile(padq, wq[...], r0, tt), _conv_tile(padk, wk[...], r0, tt), _conv_tile(padv, wv[...], r0, tt)

        for r0 in range(0, s, tt):
            rows = pl.ds(r0, tt)
            q_s[rows, :], k_s[rows, :], v_s[rows, :], b_s[rows, :], g_s[rows, :] = pre(
                *conv_tiles(r0), ba[rows, :], alog_r[...], dtb_r[...])
        dnw_v = jnp.zeros((1, LANES), F32)
        for r0 in range(0, s, tt):
            rows = pl.ds(r0, tt)
            _, post_vjp = jax.vjp(_dn_post, oraw[rows, :], z[rows, :], nw_r[...])
            do_raw, dz_v, dnw_t = post_vjp(dob_ref[rows, :].astype(F32))
            dz[rows, :] = dz_v.astype(dz.dtype)
            o_s[rows, :] = do_raw
            dnw_v = dnw_v + dnw_t

        def state_step(i, dstate):
            c = nchunk - 1 - i
            rows = _chunk_rows(c)
            _, vjp = jax.vjp(_dn_state, u_s[rows, :], w_s[rows, :].astype(F32), in_s[rows, :].astype(F32),
                             qd_s[rows, :].astype(F32), kd_s[rows, :].astype(F32), gc_s[rows, :], states[c])
            du, dw, din, dqd, dkd, dgc, dstate = vjp((o_s[rows, :], dstate))
            dq_s[rows, :] = du
            dk_s[rows, :] = dw
            dv_s[rows, :] = dqd
            dkd_s[rows, :] = dkd
            din_s[rows, :] = din
            dgc_s[rows, :] = dgc
            return dstate

        lax.fori_loop(0, nchunk, state_step, jnp.zeros((d, d), F32))
        local = functools.partial(_dn_local, masks=masks)
        grp = _group(nchunk, 4)

        def local_step(i, carry):
            rows = [_chunk_rows(i * grp + j) for j in range(grp)]
            get = lambda ref: [ref[r, :] for r in rows]
            _, vjp = jax.vjp(local, get(q_s), get(k_s), get(v_s), get(b_s), get(gc_s))
            dq_c, dk_c, dv_c, db_c, dgc_c = vjp((get(dq_s), get(dk_s), get(din_s), get(dv_s), get(dkd_s)))
            dgc_c = [dgc_c[j] + dgc_s[r, :] for j, r in enumerate(rows)]
            dg_c = [_mask_dot(causal_f, jnp.broadcast_to(t, (CHUNK, LANES)), _TN)[:, 0:1] for t in dgc_c]
            for j, r in enumerate(rows):
                dq_s[r, :] = dq_c[j]
                dk_s[r, :] = dk_c[j]
                dv_s[r, :] = dv_c[j]
                db_s[r, :] = db_c[j]
                dg_s[r, :] = dg_c[j]
            return carry

        lax.fori_loop(0, nchunk // grp, local_step, 0)

        @pl.when(h == 0)
        def _():
            dba[...] = jnp.zeros_like(dba)
            dalog[...] = jnp.zeros_like(dalog)
            ddtb[...] = jnp.zeros_like(ddtb)
            dnw[...] = jnp.zeros_like(dnw)

        dalog_v = jnp.zeros((1, LANES), F32)
        ddtb_v = jnp.zeros((1, LANES), F32)
        for r0 in range(0, s, tt):
            rows = pl.ds(r0, tt)
            _, pre_vjp = jax.vjp(pre, *conv_tiles(r0), ba[rows, :], alog_r[...], dtb_r[...])
            daq, dak, dav, dba_t, dalog_t, ddtb_t = pre_vjp(
                (dq_s[rows, :], dk_s[rows, :], dv_s[rows, :], db_s[rows, :], dg_s[rows, :]))
            dq_s[rows, :], dk_s[rows, :], dv_s[rows, :] = daq, dak, dav
            dba[rows, :] += dba_t
            dalog_v = dalog_v + dalog_t
            ddtb_v = ddtb_v + ddtb_t
        tail = pl.ds(s, _PAD)
        dq_s[tail, :] = dk_s[tail, :] = dv_s[tail, :] = jnp.zeros((_PAD, d), F32)
        for pad, da_s, w_ref, dx_ref, dw_ref in ((padq, dq_s, wq, dxq, dwq), (padk, dk_s, wk, dxk, dwk), (padv, dv_s, wv, dxv, dwv)):
            dw_acc = jnp.zeros((CONV_K, d), F32)
            for r0 in range(0, s, tt):
                dx_t, dw_t = _conv_tile_bwd(pad, da_s, w_ref[...], r0, tt)
                dx_ref[pl.ds(r0, tt), :] = dx_t.astype(dx_ref.dtype)
                dw_acc = dw_acc + dw_t
            dw_ref[...] = dw_acc
        dalog[...] += dalog_v
        ddtb[...] += ddtb_v
        dnw[...] += dnw_v

    big = pltpu.VMEM((s, d), F32)
    thin = pltpu.VMEM((s, 1), F32)
    padded = pltpu.VMEM((s + _PAD, d), F32)
    w_all = DN_HEADS * d
    col_out = lambda: pl.BlockSpec((s, d), lambda h: (0, h))
    cw_out = lambda: pl.BlockSpec((CONV_K, d), lambda h: (0, h))
    row = lambda: pl.BlockSpec((1, LANES), lambda h: (0, 0))
    big_out = jax.ShapeDtypeStruct((s, w_all), BF16)
    cw_shape = jax.ShapeDtypeStruct((CONV_K, w_all), F32)
    row_shape = jax.ShapeDtypeStruct((1, LANES), F32)
    return _call_with_exchange(
        body, "dn_bwd", DN_HEADS,
        _dn_in_specs(s) + [pl.BlockSpec((s, d), lambda h: (0, h))] + _dn_saved_specs(s, pipeline_mode=pl.Buffered(1)),
        [col_out(), col_out(), col_out(), col_out(), pl.BlockSpec((s, LANES), lambda h: (0, 0)),
         cw_out(), cw_out(), cw_out(), row(), row(), row()],
        [big_out, big_out, big_out, big_out, jax.ShapeDtypeStruct((s, LANES), F32),
         cw_shape, cw_shape, cw_shape, row_shape, row_shape, row_shape],
        [padded, padded, padded, big, big, big, thin, thin, big,
         padded, padded, padded, thin, thin, big, pltpu.VMEM((s, CHUNK), F32), thin],
        (proj, proj, proj, proj, proj, conv_w, conv_w, conv_w, alog, dtb, nw, dob, *kept), exchange)


def _loss_head(y, target, tile=256):
    n, dm = y.shape
    tile = min(tile, n)

    def body(y_ref, t_ref, dy_ref, loss_ref):
        err = y_ref[...] - t_ref[...]
        dy_ref[...] = err * (1.0 / dm)

        @pl.when(pl.program_id(0) == 0)
        def _():
            loss_ref[...] = jnp.zeros_like(loss_ref)

        loss_ref[...] += 0.5 * jnp.sum(jnp.mean(jnp.square(err), axis=-1, keepdims=True), axis=0, keepdims=True)

    blk = pl.BlockSpec((tile, dm), lambda i: (i, 0))
    return _pc(body, name="loss_head", grid=(n // tile,), in_specs=[blk, blk],
               out_specs=[blk, pl.BlockSpec((1, 1), lambda i: (0, 0))],
               out_shape=[jax.ShapeDtypeStruct((n, dm), F32), jax.ShapeDtypeStruct((1, 1), F32)],
               compiler_params=_cparams(("arbitrary",)))(y, target)


def _ada_fwd(c_all, w_ada, b_shard):
    nl, dm, n = w_ada.shape

    def body(c_ref, w_ref, b_ref, o_ref):
        ca = _silu(c_ref[...]).astype(BF16)
        o_ref[0] = jnp.dot(ca, w_ref[0].astype(BF16), preferred_element_type=F32) + b_ref[0]

    return _pc(body, name="ada_fwd", grid=(nl,),
               in_specs=[pl.BlockSpec((N_DEV, dm), lambda l: (0, 0)), pl.BlockSpec((1, dm, n), lambda l: (l, 0, 0)),
                         pl.BlockSpec((1, 1, n), lambda l: (l, 0, 0))],
               out_specs=pl.BlockSpec((1, N_DEV, n), lambda l: (l, 0, 0)),
               out_shape=jax.ShapeDtypeStruct((nl, N_DEV, n), F32), compiler_params=_cparams())(c_all, w_ada, b_shard)


def _ada_bwd(c_all, dmod):
    nl, _, n = dmod.shape
    dm = c_all.shape[1]

    def body(c_ref, d_ref, o_ref):
        o_ref[0] = _hdot(_silu(c_ref[...]), d_ref[0], _TN)

    return _pc(body, name="ada_bwd", grid=(nl,),
               in_specs=[pl.BlockSpec((N_DEV, dm), lambda l: (0, 0)), pl.BlockSpec((1, N_DEV, n), lambda l: (l, 0, 0))],
               out_specs=pl.BlockSpec((1, dm, n), lambda l: (l, 0, 0)),
               out_shape=jax.ShapeDtypeStruct((nl, dm, n), F32), compiler_params=_cparams())(c_all, dmod)


def _adamw(g, w, m, v):
    m = ADAM_B1 * m + (1.0 - ADAM_B1) * g
    v = ADAM_B2 * v + (1.0 - ADAM_B2) * jnp.square(g)
    m_hat = m / (1.0 - ADAM_B1 ** ADAM_STEP)
    v_hat = v / (1.0 - ADAM_B2 ** ADAM_STEP)
    delta = -ADAM_LR * (m_hat / (jnp.sqrt(v_hat) + ADAM_EPS) + ADAM_WD * w)
    return delta, m, v


def _adam_call(parts, w, m, v, name, tile=128):
    shape = w.shape
    flat = lambda t: t.reshape(-1, shape[-1])
    width = shape[-1]

    def fn(*vals):
        g = vals[0] if len(parts) == 1 else vals[0] + vals[1]
        return (g,) + _adamw(g, *vals[len(parts):])

    rows = [_whole(flat(t)) for t in (*parts, w, m, v)]
    outs = _rowwise(fn, rows, [], [(width, F32)] * 4, name, tile=tile)
    return [o.reshape(shape) for o in outs]


def _sum_slots(r, name, tile=128):
    _, n, width = r.shape
    tile = min(tile, n)

    def body(r_ref, o_ref):
        acc = r_ref[0].astype(F32)
        for j in range(1, N_CHIPS):
            acc = acc + r_ref[j].astype(F32)
        o_ref[...] = acc

    return _pc(body, name=name, grid=(n // tile,), in_specs=[pl.BlockSpec((N_CHIPS, tile, width), lambda i: (0, i, 0))],
               out_specs=pl.BlockSpec((tile, width), lambda i: (i, 0)),
               out_shape=jax.ShapeDtypeStruct((n, width), F32), compiler_params=_cparams())(r)


def _small_adam(g_all, w, m, v):
    def body(g_ref, w_ref, m_ref, v_ref, og, od, om, ov):
        g = g_ref[0]
        for j in range(1, N_DEV):
            g = g + g_ref[j]
        og[...] = g
        od[...], om[...], ov[...] = _adamw(g, w_ref[...], m_ref[...], v_ref[...])

    vm = pl.BlockSpec(memory_space=pltpu.VMEM)
    shp = jax.ShapeDtypeStruct(w.shape, F32)
    return _pc(body, name="small_adam", in_specs=[vm] * 4, out_specs=[vm] * 4, out_shape=[shp] * 4,
               compiler_params=_cparams())(g_all, w, m, v)


def _place():
    return lax.axis_index("x"), lax.axis_index("y"), lax.axis_index("c")


def _flip(v, bit):
    return 1 - v if bit else v


def _all_gather8(a):
    r, n = a.shape

    def body(a_ref, o_ref, send_sems, recv_sems):
        x, y, c = _place()
        me = 4 * x + 2 * y + c
        o_ref[me] = a_ref[...]
        copies = []
        for k in range(1, N_DEV):
            peer = (_flip(x, k & 4), _flip(y, k & 2), _flip(c, k & 1))
            copies.append(pltpu.make_async_remote_copy(
                src_ref=a_ref, dst_ref=o_ref.at[me], send_sem=send_sems.at[k - 1], recv_sem=recv_sems.at[k - 1],
                device_id=peer, device_id_type=MESH))
        for cp in copies:
            cp.start()
        for k in range(1, N_DEV):
            px, py, pc_ = _flip(x, k & 4), _flip(y, k & 2), _flip(c, k & 1)
            pltpu.make_async_remote_copy(
                src_ref=a_ref, dst_ref=o_ref.at[4 * px + 2 * py + pc_], send_sem=send_sems.at[k - 1],
                recv_sem=recv_sems.at[k - 1], device_id=(px, py, pc_), device_id_type=MESH).wait_recv()
        for cp in copies:
            cp.wait_send()

    vm = pl.BlockSpec(memory_space=pltpu.VMEM)
    return _pc(body, name="all_gather8", in_specs=[vm], out_specs=vm,
               out_shape=jax.ShapeDtypeStruct((N_DEV, r, n), a.dtype),
               scratch_shapes=[pltpu.SemaphoreType.DMA((N_DEV - 1,)), pltpu.SemaphoreType.DMA((N_DEV - 1,))],
               compiler_params=_cparams())(a)


def _chip_exchange(arrays, gather, name):
    na = len(arrays)

    def body(*refs):
        ins, outs, sems = refs[:na], refs[na:2 * na], refs[2 * na:]
        _exchange_copies(ins, outs, sems, gather, start=True)
        _exchange_copies(ins, outs, sems, gather, start=False)

    in_specs, out_specs, out_shape, scratch = _exchange_specs(arrays, gather)
    return _pc(body, name=name, in_specs=in_specs, out_specs=out_specs, out_shape=out_shape, scratch_shapes=scratch,
               compiler_params=_cparams())(*arrays)


def _exchange_specs(arrays, gather):
    na = len(arrays)
    hbm = pl.BlockSpec(memory_space=pl.ANY)
    out_shape = [jax.ShapeDtypeStruct(((N_CHIPS,) + a.shape) if gather else a.shape, a.dtype) for a in arrays]
    n_remote = 4 if gather else 2
    scratch = [pltpu.SemaphoreType.DMA((3 * na,))] * n_remote + [pltpu.SemaphoreType.DMA((na,))]
    return [hbm] * na, [hbm] * na, out_shape, scratch


def _gather_copies(ins, outs, sems, start):
    send_i, recv_i, send_d, recv_d, local_sems = sems
    x, y, c = _place()
    me = 2 * x + y
    sibling = (x, y, 1 - c)
    ici_sends, ici_arrivals, hand_ons, hand_arrivals, locals_ = [], [], [], [], []
    for i in range(len(ins)):
        half = ins[i].shape[0] // 2
        mine, other = pl.ds(c * half, half), pl.ds((1 - c) * half, half)
        locals_.append(pltpu.make_async_copy(ins[i], outs[i].at[me], local_sems.at[i]))
        for j in range(1, N_CHIPS):
            px, py = _flip(x, j & 2), _flip(y, j & 1)
            peer = 2 * px + py
            k = i * 3 + j - 1
            ici = dict(send_sem=send_i.at[k], recv_sem=recv_i.at[k], device_id=(px, py, c), device_id_type=MESH)
            d2d = dict(send_sem=send_d.at[k], recv_sem=recv_d.at[k], device_id=sibling, device_id_type=MESH)
            ici_sends.append(pltpu.make_async_remote_copy(src_ref=ins[i].at[mine], dst_ref=outs[i].at[me, mine], **ici))
            ici_arrivals.append(pltpu.make_async_remote_copy(src_ref=ins[i].at[mine], dst_ref=outs[i].at[peer, mine], **ici))
            hand_ons.append(pltpu.make_async_remote_copy(
                src_ref=outs[i].at[peer, mine], dst_ref=outs[i].at[peer, mine], **d2d))
            hand_arrivals.append(pltpu.make_async_remote_copy(
                src_ref=outs[i].at[peer, other], dst_ref=outs[i].at[peer, other], **d2d))
    if start:
        for cp in locals_ + ici_sends:
            cp.start()
    else:
        for arrival, hand_on in zip(ici_arrivals, hand_ons):
            arrival.wait_recv()
            hand_on.start()
        for cp in hand_arrivals:
            cp.wait_recv()
        for cp in ici_sends + hand_ons:
            cp.wait_send()
        for cp in locals_:
            cp.wait()


def _exchange_copies(ins, outs, sems, gather, start):
    if gather:
        return _gather_copies(ins, outs, sems, start)
    send_sems, recv_sems, local_sems = sems
    x, y, c = _place()
    me = 2 * x + y
    sends, arrivals, locals_ = [], [], []
    for i in range(len(ins)):
        locals_.append(pltpu.make_async_copy(ins[i] if gather else ins[i].at[me], outs[i].at[me], local_sems.at[i]))
        for j in range(1, N_CHIPS):
            px, py = _flip(x, j & 2), _flip(y, j & 1)
            peer = 2 * px + py
            pair = dict(send_sem=send_sems.at[i * 3 + j - 1], recv_sem=recv_sems.at[i * 3 + j - 1],
                        device_id=(px, py, c), device_id_type=MESH)
            sends.append(pltpu.make_async_remote_copy(
                src_ref=ins[i] if gather else ins[i].at[peer], dst_ref=outs[i].at[me], **pair))
            arrivals.append(pltpu.make_async_remote_copy(
                src_ref=ins[i] if gather else ins[i].at[me], dst_ref=outs[i].at[peer], **pair))
    if start:
        for cp in locals_ + sends:
            cp.start()
    else:
        for cp in arrivals:
            cp.wait_recv()
        for cp in sends:
            cp.wait_send()
        for cp in locals_:
            cp.wait()


def _carry_exchange(body, n_in, n_out, n_scratch, n_arrays, gather, steps):
    def wrapped(*refs):
        na = n_arrays
        ins, xin = refs[:n_in], refs[n_in:n_in + na]
        outs = refs[n_in + na:n_in + na + n_out]
        xout = refs[n_in + na + n_out:n_in + 2 * na + n_out]
        rest = refs[n_in + 2 * na + n_out:]
        scratch, sems = rest[:n_scratch], rest[n_scratch:]
        first, last = True, True
        for axis, n in enumerate(steps):
            first = jnp.logical_and(first, pl.program_id(axis) == 0)
            last = jnp.logical_and(last, pl.program_id(axis) == n - 1)

        @pl.when(first)
        def _():
            _exchange_copies(xin, xout, sems, gather, start=True)

        body(*ins, *outs, *scratch)

        @pl.when(last)
        def _():
            _exchange_copies(xin, xout, sems, gather, start=False)

    return wrapped


def _sibling_exchange(arrays, name):
    na = len(arrays)

    def body(*refs):
        ins, outs = refs[:na], refs[na:2 * na]
        send_sems, recv_sems = refs[2 * na:]
        x, y, c = _place()
        copies = [pltpu.make_async_remote_copy(
            src_ref=ins[i], dst_ref=outs[i], send_sem=send_sems.at[i], recv_sem=recv_sems.at[i],
            device_id=(x, y, 1 - c), device_id_type=MESH) for i in range(na)]
        for cp in copies:
            cp.start()
        for cp in copies:
            cp.wait()

    hbm = pl.BlockSpec(memory_space=pl.ANY)
    return _pc(body, name=name, in_specs=[hbm] * na, out_specs=[hbm] * na,
               out_shape=[jax.ShapeDtypeStruct(a.shape, a.dtype) for a in arrays],
               scratch_shapes=[pltpu.SemaphoreType.DMA((na,)), pltpu.SemaphoreType.DMA((na,))],
               compiler_params=_cparams())(*arrays)


def _heads_q(t):
    s = t.shape[0]
    return t.reshape(s, ATT_KV_HEADS, ATT_GROUP, ATT_HEAD_DIM).transpose(1, 2, 0, 3)


def _unheads_q(t):
    s = t.shape[2]
    return t.transpose(2, 0, 1, 3).reshape(s, ATT_KV_HEADS * ATT_GROUP * ATT_HEAD_DIM)


def _heads_kv(t):
    s = t.shape[0]
    return t.reshape(s, ATT_KV_HEADS, ATT_HEAD_DIM).transpose(1, 0, 2)


def _unheads_kv(t):
    s = t.shape[1]
    return t.transpose(1, 0, 2).reshape(s, ATT_KV_HEADS * ATT_HEAD_DIM)


def _row128(v):
    return jnp.pad(v, (0, LANES - v.shape[0])).reshape(1, LANES)


def _layer_fwd(x, p, exchange=None):
    sh1, sc1, gt1, sh2, sc2, gt2 = [p["mod"][i] for i in range(6)]
    (u,) = _rowwise(_f_mod, [_whole(x)], [sc1, sh1], [(D_MODEL, BF16)], "mod1")
    proj = _mm(u, p["w_in"], "nn", F32, "proj")
    qh = _heads_q(proj[:, C_Q:C_Q + 1024])
    kh = _heads_kv(proj[:, C_K:C_K + 256])
    vh = _heads_kv(proj[:, C_V:C_V + 256])
    sinks4 = p["sinks"].reshape(ATT_KV_HEADS, ATT_GROUP, 1, 1)
    behind_attn = None if exchange is None else (exchange[0][-1:], exchange[1])
    behind_dn = None if exchange is None else (exchange[0][:-1], exchange[1])
    o_heads, got_attn = _attn_fwd(qh, kh, vh, sinks4, behind_attn)
    o_a = _unheads_q(o_heads)
    (o_b, *dn_kept), got_dn = _dn_fwd(proj, p["conv_w"], _row128(p["a_log"]), _row128(p["dt_bias"]),
                                      p["dn_norm_w"].reshape(1, LANES), behind_dn)
    exchanged = None if exchange is None else list(got_dn) + list(got_attn)
    y_a = _mm(o_a, p["w_oa"], "nn", F32, "y_a")
    y_b = _mm(o_b, p["w_ob"], "nn", F32, "y_b")
    (gm,) = _rowwise(_f_gate, [(proj, C_GA // 1024, 1024), (proj, C_GB // 1024, 1024), _whole(y_a), _whole(y_b)], [],
                     [(D_MODEL, BF16)], "gate")
    mixed = _mm(gm, p["w_out"], "nn", F32, "mixed")
    x1, u2 = _rowwise(_f_post1, [_whole(x), _whole(mixed)], [gt1, p["ln1_g"], p["ln1_b"], sc2, sh2],
                      [(D_MODEL, F32), (D_MODEL, BF16)], "post1")
    hpre = _mm(u2, p["w_ff1"], "nn", F32, "ff1")
    (h,) = _rowwise(_f_act, [_whole(hpre)], [p["b_ff1"]], [(D_FF, BF16)], "act")
    ff = _mm(h, p["w_ff2"], "nn", F32, "ff2")
    (x2,) = _rowwise(_f_post2, [_whole(x1), _whole(ff)], [gt2, p["b_ff2"], p["ln2_g"], p["ln2_b"]],
                     [(D_MODEL, F32)], "post2")
    saved = dict(x=x, u=u, proj=proj, o_a=o_a, o_b=o_b, y_a=y_a, y_b=y_b, gm=gm, mixed=mixed, x1=x1, u2=u2,
                 hpre=hpre, h=h, ff=ff, dn_kept=dn_kept, heads=(qh, kh, vh))
    return x2, saved, exchanged


def _layer_bwd(dx2, p, sv, carry=None):
    sh1, sc1, gt1, sh2, sc2, gt2 = [p["mod"][i] for i in range(6)]
    g = {}
    (dx1_a, dff), (dgt2, g["b_ff2"], g["ln2_g"], g["ln2_b"]) = _rowwise_bwd(
        _f_post2, [_whole(sv["x1"]), _whole(sv["ff"])], [gt2, p["b_ff2"], p["ln2_g"], p["ln2_b"]], [dx2],
        [F32, BF16], "post2_bwd")
    dh = _mm(dff, p["w_ff2"], "nt", F32, "dh")
    g["w_ff2"] = _mm(sv["h"], dff, "tn", F32, "dw_ff2")
    (dhpre,), (g["b_ff1"],) = _rowwise_bwd(_f_act, [_whole(sv["hpre"])], [p["b_ff1"]], [dh], [BF16], "act_bwd")
    du2 = _mm(dhpre, p["w_ff1"], "nt", F32, "du2")
    g["w_ff1"] = _mm(sv["u2"], dhpre, "tn", F32, "dw_ff1")
    (dx_a, dmixed), (dgt1, g["ln1_g"], g["ln1_b"], dsc2, dsh2) = _rowwise_bwd(
        _f_post1, [_whole(sv["x"]), _whole(sv["mixed"])], [gt1, p["ln1_g"], p["ln1_b"], sc2, sh2], [dx1_a, du2],
        [F32, BF16], "post1_bwd")
    dgm = _mm(dmixed, p["w_out"], "nt", F32, "dgm")
    g["w_out"] = _mm(sv["gm"], dmixed, "tn", F32, "dw_out")
    proj = sv["proj"]
    (dga, dgb, dya, dyb), _ = _rowwise_bwd(
        _f_gate, [(proj, C_GA // 1024, 1024), (proj, C_GB // 1024, 1024), _whole(sv["y_a"]), _whole(sv["y_b"])], [],
        [dgm], [BF16, BF16, BF16, BF16], "gate_bwd")
    do_a = _mm(dya, p["w_oa"], "nt", F32, "do_a")
    g["w_oa"] = _mm(sv["o_a"], dya, "tn", F32, "dw_oa")
    do_b = _mm(dyb, p["w_ob"], "nt", F32, "do_b")
    g["w_ob"] = _mm(sv["o_b"], dyb, "tn", F32, "dw_ob")
    exchange = None if carry is None else (carry(g), False)
    (ddq, ddk, ddv, ddz, dba, dwq, dwk, dwv, dalog, ddtb, dnw), exchanged = _dn_bwd(
        proj, p["conv_w"], _row128(p["a_log"]), _row128(p["dt_bias"]), p["dn_norm_w"].reshape(1, LANES), do_b,
        sv["dn_kept"], exchange)
    g["conv_w"] = jnp.concatenate([dwq, dwk, dwv], axis=1)
    g["a_log"], g["dt_bias"], g["dn_norm_w"] = dalog[0, :DN_HEADS], ddtb[0, :DN_HEADS], dnw[0]
    qh, kh, vh = sv["heads"]
    sinks4 = p["sinks"].reshape(ATT_KV_HEADS, ATT_GROUP, 1, 1)
    dqh, dkh, dvh, dsk = _attn_bwd(qh, kh, vh, sinks4, _heads_q(do_a))
    g["sinks"] = dsk.reshape(ATT_KV_HEADS * ATT_GROUP)
    s = proj.shape[0]
    dproj = jnp.concatenate([
        _unheads_q(dqh).astype(BF16), ddq, ddk, ddv, ddz, dga, dgb,
        _unheads_kv(dkh[:, WINDOW:, :]).astype(BF16), _unheads_kv(dvh[:, WINDOW:, :]).astype(BF16),
        dba.astype(BF16), jnp.zeros((s, D_IN_P - C_BA - LANES), BF16)], axis=1)
    du = _mm(dproj, p["w_in"], "nt", F32, "du")
    g["w_in"] = _mm(sv["u"], dproj, "tn", F32, "dw_in")
    (dx,), (dsc1, dsh1) = _rowwise_bwd(_f_mod, [_whole(sv["x"])], [sc1, sh1], [du], [F32], "mod1_bwd", add=(0, dx_a))
    g["mod"] = jnp.stack([dsh1, dsc1, dgt1, dsh2, dsc2, dgt2])
    return dx, g, exchanged


def _permute_w_in(w):
    pad = jnp.zeros(w.shape[:-1] + (D_IN_P - D_IN,), w.dtype)
    return jnp.concatenate([w[..., 0:1024], w[..., 1536:5632], w[..., 5648:7696], w[..., 1024:1536],
                            w[..., 5632:5648], pad], axis=-1)


def _unpermute_w_in(g):
    return jnp.concatenate([g[..., 0:1024], g[..., C_K:C_K + 512], g[..., 1024:5120], g[..., C_BA:C_BA + 16],
                            g[..., 5120:7168]], axis=-1)


def _cols_from_chips(t):
    c, l, r, n = t.shape
    return t.transpose(1, 2, 0, 3).reshape(l, r, c * n)


def _cols_to_chips(t):
    l, r, n4 = t.shape
    return t.reshape(l, r, N_CHIPS, n4 // N_CHIPS).transpose(2, 0, 1, 3)


def _rows_from_chips(t):
    c, l, r, n = t.shape
    return t.transpose(1, 0, 2, 3).reshape(l, c * r, n)


def _rows_to_chips(t):
    l, r4, n = t.shape
    return t.reshape(l, N_CHIPS, r4 // N_CHIPS, n).transpose(1, 0, 2, 3)


_REPLICATED = ("b_ada", "a_log", "dt_bias", "sinks", "dn_norm_w", "ln1_g", "ln1_b", "b_ff1", "b_ff2", "ln2_g", "ln2_b")
_SMALL = _REPLICATED + ("conv_w",)
_PACK_W = 1024
_WEIGHT_ORDER = ("w_ada", "b_ada", "w_in", "conv_w", "a_log", "dt_bias", "sinks", "dn_norm_w", "w_oa", "w_ob", "w_out",
                 "ln1_g", "ln1_b", "w_ff1", "b_ff1", "w_ff2", "b_ff2", "ln2_g", "ln2_b")


def _pack_small(d):
    flat = jnp.concatenate([d[k].reshape(-1) for k in _SMALL])
    rows = -(-flat.shape[0] // (_PACK_W * 8)) * 8
    return jnp.pad(flat, (0, rows * _PACK_W - flat.shape[0])).reshape(rows, _PACK_W)


def _unpack_small(packed, shapes):
    flat = packed.reshape(-1)
    out, off = {}, 0
    for k in _SMALL:
        n = 1
        for d_ in shapes[k]:
            n *= d_
        out[k] = flat[off:off + n].reshape(shapes[k])
        off += n
    return out


def kernel(x, c, w_ada, b_ada, w_in, conv_w, a_log, dt_bias, sinks, dn_norm_w, w_oa, w_ob, w_out, ln1_g, ln1_b, w_ff1, b_ff1, w_ff2, b_ff2, ln2_g, ln2_b, loss_target, m_w_ada, m_b_ada, m_w_in, m_conv_w, m_a_log, m_dt_bias, m_sinks, m_dn_norm_w, m_w_oa, m_w_ob, m_w_out, m_ln1_g, m_ln1_b, m_w_ff1, m_b_ff1, m_w_ff2, m_b_ff2, m_ln2_g, m_ln2_b, v_w_ada, v_b_ada, v_w_in, v_conv_w, v_a_log, v_dt_bias, v_sinks, v_dn_norm_w, v_w_oa, v_w_ob, v_w_out, v_ln1_g, v_ln1_b, v_w_ff1, v_b_ff1, v_w_ff2, v_b_ff2, v_ln2_g, v_ln2_b):
    ix, iy, ic = _place()
    chip = 2 * ix + iy
    dev = 4 * ix + 2 * iy + ic
    weights = dict(w_ada=w_ada, b_ada=b_ada, w_in=w_in, conv_w=conv_w, a_log=a_log, dt_bias=dt_bias, sinks=sinks,
                   dn_norm_w=dn_norm_w, w_oa=w_oa, w_ob=w_ob, w_out=w_out, ln1_g=ln1_g, ln1_b=ln1_b, w_ff1=w_ff1,
                   b_ff1=b_ff1, w_ff2=w_ff2, b_ff2=b_ff2, ln2_g=ln2_g, ln2_b=ln2_b)
    mom_m = dict(w_ada=m_w_ada, b_ada=m_b_ada, w_in=m_w_in, conv_w=m_conv_w, a_log=m_a_log, dt_bias=m_dt_bias,
                 sinks=m_sinks, dn_norm_w=m_dn_norm_w, w_oa=m_w_oa, w_ob=m_w_ob, w_out=m_w_out, ln1_g=m_ln1_g,
                 ln1_b=m_ln1_b, w_ff1=m_w_ff1, b_ff1=m_b_ff1, w_ff2=m_w_ff2, b_ff2=m_b_ff2, ln2_g=m_ln2_g, ln2_b=m_ln2_b)
    mom_v = dict(w_ada=v_w_ada, b_ada=v_b_ada, w_in=v_w_in, conv_w=v_conv_w, a_log=v_a_log, dt_bias=v_dt_bias,
                 sinks=v_sinks, dn_norm_w=v_dn_norm_w, w_oa=v_w_oa, w_ob=v_w_ob, w_out=v_w_out, ln1_g=v_ln1_g,
                 ln1_b=v_ln1_b, w_ff1=v_w_ff1, b_ff1=v_b_ff1, w_ff2=v_w_ff2, b_ff2=v_b_ff2, ln2_g=v_ln2_g, ln2_b=v_ln2_b)

    n_ada = w_ada.shape[2]
    c_all = _all_gather8(jnp.pad(c, ((0, 7), (0, 0))))[:, 0, :]
    b_shard = lax.dynamic_slice_in_dim(b_ada, chip * n_ada, n_ada, axis=1).reshape(DEPTH, 1, n_ada)
    mod_t = _ada_fwd(c_all, w_ada, b_shard)
    mod_all = _all_gather8(mod_t.reshape(DEPTH * N_DEV, n_ada)).reshape(N_DEV, DEPTH, N_DEV, n_ada)
    mod_mine = lax.dynamic_index_in_dim(mod_all[0::2], dev, axis=2, keepdims=False)
    mod = mod_mine.transpose(1, 0, 2).reshape(DEPTH, 6, 1, D_MODEL)

    n_cw = conv_w.shape[2]
    cw_all = _all_gather8(conv_w.reshape(DEPTH * CONV_K, n_cw))[0::2]
    conv_full = cw_all.transpose(1, 0, 2).reshape(DEPTH, CONV_K, N_CHIPS * n_cw)

    big = ("w_in", "w_oa", "w_ob", "w_out", "w_ff1", "w_ff2")
    w16 = {k: weights[k].astype(BF16) for k in big}
    shards = lambda l: [w16[k][l] for k in big]

    def assemble(gathered):
        gw = {k: t[:, None] for k, t in zip(big, gathered)}
        full = dict(w_in=_permute_w_in(_cols_from_chips(gw["w_in"])), w_ff1=_cols_from_chips(gw["w_ff1"]),
                    w_oa=_rows_from_chips(gw["w_oa"]), w_ob=_rows_from_chips(gw["w_ob"]),
                    w_out=_rows_from_chips(gw["w_out"]), w_ff2=_rows_from_chips(gw["w_ff2"]))
        return {k: t[0] for k, t in full.items()}

    to_chips = dict(w_in=lambda t: _cols_to_chips(_unpermute_w_in(t)), w_ff1=_cols_to_chips, w_oa=_rows_to_chips,
                    w_ob=_rows_to_chips, w_out=_rows_to_chips, w_ff2=_rows_to_chips)

    def slices_for_chips(g, keys):
        return [to_chips[k](g[k][None])[:, 0].astype(BF16) for k in keys]

    full = [None] * DEPTH
    full[0] = assemble(_chip_exchange(shards(0), True, "gather_weights"))

    def layer_params(l):
        p = dict(full[l])
        p["mod"] = mod[l]
        p["conv_w"] = conv_full[l]
        for k in ("a_log", "dt_bias", "sinks", "dn_norm_w"):
            p[k] = weights[k][l]
        for k in ("ln1_g", "ln1_b", "b_ff1", "b_ff2", "ln2_g", "ln2_b"):
            p[k] = weights[k][l].reshape(1, -1)
        return p

    xs = x[0]
    saved = []
    for l in range(DEPTH):
        nxt = (shards(l + 1), True) if l + 1 < DEPTH else None
        xs, sv, gathered = _layer_fwd(xs, layer_params(l), nxt)
        if nxt is not None:
            full[l + 1] = assemble(gathered)
        saved.append(sv)
    dy, loss_local = _loss_head(xs, loss_target[0])
    loss = lax.psum(loss_local[0, 0], ("x", "y", "c"))
    early = tuple(k for k in big if k != "w_in")
    grads = [None] * DEPTH
    received = [dict() for _ in range(DEPTH)]
    dx = dy
    pending = []
    for l in reversed(range(DEPTH)):
        carry = functools.partial(lambda g, first: first + slices_for_chips(g, early), first=pending)
        dx, grads[l], got = _layer_bwd(dx, layer_params(l), saved[l], carry)
        if pending:
            received[l + 1]["w_in"] = got[0]
        received[l].update(zip(early, got[len(pending):]))
        pending = slices_for_chips(grads[l], ("w_in",))
    received[0]["w_in"] = _chip_exchange(pending, False, "scatter_grads")[0]
    grad_x = dx[None]
    gstack = {k: jnp.stack([grads[l][k] for l in range(DEPTH)]) for k in grads[0] if k not in big}

    dmod = gstack["mod"].reshape(DEPTH, 6 * D_MODEL)
    small_g = dict(b_ada=dmod, a_log=gstack["a_log"], dt_bias=gstack["dt_bias"], sinks=gstack["sinks"],
                   dn_norm_w=gstack["dn_norm_w"], ln1_g=gstack["ln1_g"], ln1_b=gstack["ln1_b"], b_ff1=gstack["b_ff1"],
                   b_ff2=gstack["b_ff2"], ln2_g=gstack["ln2_g"], ln2_b=gstack["ln2_b"], conv_w=gstack["conv_w"])
    shapes = {k: weights[k].shape for k in _REPLICATED}
    shapes["conv_w"] = small_g["conv_w"].shape
    g_all = _all_gather8(_pack_small(small_g))
    no_conv = jnp.zeros(shapes["conv_w"], F32)
    small_out = _small_adam(g_all, _pack_small(dict(weights, conv_w=no_conv)), _pack_small(dict(mom_m, conv_w=no_conv)),
                            _pack_small(dict(mom_v, conv_w=no_conv)))
    small_res = [_unpack_small(t, shapes) for t in small_out]
    g_conv = lax.dynamic_slice_in_dim(small_res[0]["conv_w"], chip * n_cw, n_cw, axis=2)
    res = {"conv_w": _adam_call([g_conv], conv_w, m_conv_w, v_conv_w, "adam_conv_w", tile=16)}

    dmod_all = g_all.reshape(N_DEV, -1)[:, :DEPTH * 6 * D_MODEL].reshape(N_DEV, DEPTH, 6 * D_MODEL)
    dmod_shard = lax.dynamic_slice_in_dim(dmod_all, chip * n_ada, n_ada, axis=2).transpose(1, 0, 2)
    g_w_ada = _ada_bwd(c_all, dmod_shard)
    res["w_ada"] = _adam_call([g_w_ada], w_ada, m_w_ada, v_w_ada, "adam_w_ada")

    by_weight = [jnp.stack([received[l][k] for l in range(DEPTH)], axis=1) for k in big]
    partial = [_sum_slots(r.reshape(N_CHIPS, -1, r.shape[-1]), "sum_" + k) for k, r in zip(big, by_weight)]
    theirs = _sibling_exchange(partial, "sibling_grads")
    for k, mine, other in zip(big, partial, theirs):
        shape = weights[k].shape
        res[k] = _adam_call([mine.reshape(shape), other.reshape(shape)], weights[k], mom_m[k], mom_v[k], "adam_" + k)
    for k in _REPLICATED:
        res[k] = [small_res[i][k] for i in range(4)]

    outs = [loss, grad_x]
    for i in range(4):
        outs += [res[k][i] for k in _WEIGHT_ORDER]
    return tuple(outs)
```

```python
import functools

import jax
import jax.numpy as jnp
from jax import lax
from jax.experimental import pallas as pl
from jax.experimental.pallas import tpu as pltpu

F32, BF16 = jnp.float32, jnp.bfloat16
HI = lax.Precision.HIGHEST
MESH = pl.DeviceIdType.MESH

D_MODEL = 1024
DEPTH = 4
ATT_KV_HEADS, ATT_GROUP, ATT_HEAD_DIM, WINDOW = 4, 4, 64, 128
DN_HEADS, DN_HEAD_DIM, CONV_K, CHUNK = 8, 128, 4, 64
D_FF = 4 * D_MODEL
D_IN = 7696
ALPHA = (2 * DEPTH) ** 0.25
LN_EPS = 1e-5
RMS_EPS = 1e-6
ADAM_LR, ADAM_B1, ADAM_B2, ADAM_EPS, ADAM_WD, ADAM_STEP = 0.001, 0.9, 0.999, 1e-08, 0.01, 10

N_CHIPS = 4
N_DEV = 8
LANES = 128
D_IN_P = 8192
C_Q, C_DQ, C_DK, C_DV, C_Z, C_GA, C_GB, C_K, C_V, C_BA = 0, 1024, 2048, 3072, 4096, 5120, 6144, 7168, 7424, 7680
NEG = -1e30
VMEM_LIMIT = 56 << 20


def _pc(body, **kw):
    return pl.pallas_call(body, **kw)


def _cparams(sem=None):
    if sem is None:
        return pltpu.CompilerParams(vmem_limit_bytes=VMEM_LIMIT)
    return pltpu.CompilerParams(vmem_limit_bytes=VMEM_LIMIT, dimension_semantics=sem)


_MM_VMEM_BUDGET = 44 << 20
_MM_MIN_TILE = 256


def _mm_tiles(m, n, k, out_bytes):
    def halvings(d):
        out = [d]
        while out[-1] % 2 == 0 and out[-1] // 2 >= _MM_MIN_TILE:
            out.append(out[-1] // 2)
        return out

    best = None
    for tm in halvings(m):
        for tn in halvings(n):
            if 2 * (2 * tm * k + 2 * tn * k + out_bytes * tm * tn) > _MM_VMEM_BUDGET:
                continue
            cost = (2 * m * k + (m // tm) * 2 * n * k, (m // tm) * (n // tn))
            if best is None or cost < best[0]:
                best = (cost, tm, tn)
    assert best is not None, (m, n, k)
    return best[1], best[2]


def _mm(a, b, mode, out_dtype, name, tm=None, tn=None):
    if mode == "nn":
        (m, k), (_, n) = a.shape, b.shape
        dims = (((1,), (0,)), ((), ()))
    elif mode == "nt":
        (m, k), (n, _) = a.shape, b.shape
        dims = (((1,), (1,)), ((), ()))
    else:
        (k, m), (_, n) = a.shape, b.shape
        dims = (((0,), (0,)), ((), ()))
    if tm is None:
        tm, tn = _mm_tiles(m, n, k, jnp.dtype(out_dtype).itemsize)
    tm, tn = min(tm, m), min(tn, n)
    assert m % tm == 0 and n % tn == 0, (name, m, n, tm, tn)
    a_spec = pl.BlockSpec((k, tm), lambda i, j: (0, i)) if mode == "tn" else pl.BlockSpec((tm, k), lambda i, j: (i, 0))
    b_spec = pl.BlockSpec((tn, k), lambda i, j: (j, 0)) if mode == "nt" else pl.BlockSpec((k, tn), lambda i, j: (0, j))

    def body(a_ref, b_ref, o_ref):
        o_ref[...] = lax.dot_general(a_ref[...], b_ref[...], dims, preferred_element_type=F32).astype(o_ref.dtype)

    return _pc(body, name=name, grid=(m // tm, n // tn), in_specs=[a_spec, b_spec],
               out_specs=pl.BlockSpec((tm, tn), lambda i, j: (i, j)),
               out_shape=jax.ShapeDtypeStruct((m, n), out_dtype), compiler_params=_cparams())(a, b)


def _row_specs(rows, tile):
    return [pl.BlockSpec((tile, w), functools.partial(lambda i, cb: (i, cb), cb=cb)) for (_, cb, w) in rows]


def _vec_specs(vecs):
    return [pl.BlockSpec(v.shape, lambda i: (0, 0)) for v in vecs]


def _rowwise(fn, rows, vecs, outs, name, tile=256):
    n = rows[0][0].shape[0]
    tile = min(tile, n)
    nr, nv = len(rows), len(vecs)

    def body(*refs):
        rv = [r[...].astype(F32) for r in refs[:nr]]
        vv = [r[...] for r in refs[nr:nr + nv]]
        for o_ref, val in zip(refs[nr + nv:], fn(*rv, *vv)):
            o_ref[...] = val.astype(o_ref.dtype)

    res = _pc(body, name=name, grid=(n // tile,), in_specs=_row_specs(rows, tile) + _vec_specs(vecs),
              out_specs=[pl.BlockSpec((tile, w), lambda i: (i, 0)) for (w, _) in outs],
              out_shape=[jax.ShapeDtypeStruct((n, w), dt) for (w, dt) in outs],
              compiler_params=_cparams())(*[r[0] for r in rows], *vecs)
    return res


def _rowwise_bwd(fn, rows, vecs, cts, row_dtypes, name, tile=256, add=None):
    n = rows[0][0].shape[0]
    tile = min(tile, n)
    nr, nv, nc = len(rows), len(vecs), len(cts)
    want = [i for i, dt in enumerate(row_dtypes) if dt is not None]
    n_add = 0 if add is None else 1

    def body(*refs):
        rv = [r[...].astype(F32) for r in refs[:nr]]
        vv = [r[...] for r in refs[nr:nr + nv]]
        cv = [r[...].astype(F32) for r in refs[nr + nv:nr + nv + nc]]
        pos = nr + nv + nc
        add_ref = refs[pos] if n_add else None
        pos += n_add
        row_out = refs[pos:pos + len(want)]
        vec_out = refs[pos + len(want):]
        _, vjp = jax.vjp(fn, *rv, *vv)
        grads = vjp(tuple(cv))
        for o_ref, i in zip(row_out, want):
            gval = grads[i]
            if n_add and add[0] == i:
                gval = gval + add_ref[...]
            o_ref[...] = gval.astype(o_ref.dtype)

        @pl.when(pl.program_id(0) == 0)
        def _():
            for o_ref in vec_out:
                o_ref[...] = jnp.zeros_like(o_ref)

        for o_ref, gval in zip(vec_out, grads[nr:]):
            o_ref[...] += gval

    ct_rows = [(c, 0, c.shape[1]) for c in cts]
    add_rows = [(add[1], 0, add[1].shape[1])] if n_add else []
    res = _pc(body, name=name, grid=(n // tile,),
              in_specs=_row_specs(rows, tile) + _vec_specs(vecs) + _row_specs(ct_rows + add_rows, tile),
              out_specs=[pl.BlockSpec((tile, rows[i][2]), lambda i_: (i_, 0)) for i in want] + _vec_specs(vecs),
              out_shape=[jax.ShapeDtypeStruct((n, rows[i][2]), row_dtypes[i]) for i in want]
              + [jax.ShapeDtypeStruct(v.shape, F32) for v in vecs],
              compiler_params=_cparams(("arbitrary",)))(*[r[0] for r in rows], *vecs, *cts, *[a[0] for a in add_rows])
    return res[:len(want)], res[len(want):]


def _whole(a, cb=0, w=None):
    return (a, cb, a.shape[1] if w is None else w)


def _ln(x, g, b):
    mu = jnp.mean(x, axis=-1, keepdims=True)
    var = jnp.mean(jnp.square(x - mu), axis=-1, keepdims=True)
    return (x - mu) * lax.rsqrt(var + LN_EPS) * g + b


def _silu(x):
    return x * jax.nn.sigmoid(x)


def _softplus(x):
    return jnp.maximum(x, 0.0) + jnp.log(1.0 + jnp.exp(-jnp.abs(x)))


def _f_mod(x, sc, sh):
    return (x * (1.0 + sc) + sh,)


def _f_gate(ga, gb, ya, yb):
    return (jax.nn.sigmoid(ga) * ya + jax.nn.sigmoid(gb) * yb,)


def _f_post1(x, mixed, gt, g1, b1, sc2, sh2):
    x1 = _ln(ALPHA * x + (1.0 + gt) * mixed, g1, b1)
    return x1, x1 * (1.0 + sc2) + sh2


def _f_act(hpre, b):
    return (jnp.square(jnp.maximum(hpre + b, 0.0)),)


def _f_post2(x1, ff, gt, bff2, g2, b2):
    return (_ln(ALPHA * x1 + (1.0 + gt) * (ff + bff2), g2, b2),)


def _attn_valid(n):
    qi = lax.broadcasted_iota(jnp.int32, (WINDOW, 2 * WINDOW), 0)
    si = lax.broadcasted_iota(jnp.int32, (WINDOW, 2 * WINDOW), 1)
    diff = qi + WINDOW - si
    return (diff >= 0) & (diff < WINDOW) & (n * WINDOW + si - WINDOW >= 0)


def _attn_block(qs, kp, kc, vp, vc, sinks, valid):
    kband = jnp.concatenate([kp, kc], axis=0).astype(BF16)
    vband = jnp.concatenate([vp, vc], axis=0).astype(BF16)
    rng = range(len(qs))
    s = [lax.dot_general(qs[g].astype(BF16), kband, (((1,), (1,)), ((), ())), preferred_element_type=F32) for g in rng]
    s = [jnp.where(valid, s[g] * (ATT_HEAD_DIM ** -0.5), NEG) for g in rng]
    m = [lax.stop_gradient(jnp.maximum(jnp.max(s[g], axis=-1, keepdims=True), sinks[g])) for g in rng]
    p = [jnp.exp(s[g] - m[g]) for g in rng]
    denom = [jnp.sum(p[g], axis=-1, keepdims=True) + jnp.exp(sinks[g] - m[g]) for g in rng]
    probs = [(p[g] / denom[g]).astype(BF16) for g in rng]
    return [jnp.dot(probs[g], vband, preferred_element_type=F32) for g in rng]


def _attn_specs(s):
    nb = s // WINDOW
    q_spec = pl.BlockSpec((1, ATT_GROUP, WINDOW, ATT_HEAD_DIM), lambda h, n: (h, 0, n, 0))
    prev = pl.BlockSpec((1, WINDOW, ATT_HEAD_DIM), lambda h, n: (h, jnp.maximum(n - 1, 0), 0))
    cur = pl.BlockSpec((1, WINDOW, ATT_HEAD_DIM), lambda h, n: (h, n, 0))
    sk = pl.BlockSpec((1, ATT_GROUP, 1, 1), lambda h, n: (h, 0, 0, 0))
    return nb, q_spec, prev, cur, sk


def _attn_fwd(qh, kh, vh, sinks4, exchange=None):
    s = qh.shape[2]
    nb, q_spec, prev, cur, sk = _attn_specs(s)

    def body(q_ref, kp_ref, kc_ref, vp_ref, vc_ref, sk_ref, o_ref):
        valid = _attn_valid(pl.program_id(1))
        heads = range(ATT_GROUP)
        o = _attn_block([q_ref[0, g] for g in heads], kp_ref[0], kc_ref[0], vp_ref[0], vc_ref[0],
                        [sk_ref[0, g] for g in heads], valid)
        for g in heads:
            o_ref[0, g] = o[g].astype(o_ref.dtype)

    (o,), exchanged = _call_with_exchange(
        body, "attn_fwd", (ATT_KV_HEADS, nb), [q_spec, prev, cur, prev, cur, sk], [q_spec],
        [jax.ShapeDtypeStruct(qh.shape, BF16)], [], (qh, kh, kh, vh, vh, sinks4), exchange)
    return o, exchanged


def _attn_bwd(qh, kh, vh, sinks4, doh):
    s = qh.shape[2]
    nb, q_spec, prev, cur, sk = _attn_specs(s)
    acc = pl.BlockSpec((1, s + WINDOW, ATT_HEAD_DIM), lambda h, n: (h, 0, 0))

    def body(q_ref, kp_ref, kc_ref, vp_ref, vc_ref, sk_ref, do_ref, dq_ref, dk_ref, dv_ref, dsk_ref):
        n = pl.program_id(1)
        valid = _attn_valid(n)
        fn = functools.partial(_attn_block, valid=valid)
        heads = range(ATT_GROUP)
        _, vjp = jax.vjp(fn, [q_ref[0, g] for g in heads], kp_ref[0], kc_ref[0], vp_ref[0], vc_ref[0],
                         [sk_ref[0, g] for g in heads])
        dq, dkp, dkc, dvp, dvc, dsk = vjp([do_ref[0, g].astype(F32) for g in heads])
        for g in heads:
            dq_ref[0, g] = dq[g]

        @pl.when(n == 0)
        def _():
            dk_ref[...] = jnp.zeros_like(dk_ref)
            dv_ref[...] = jnp.zeros_like(dv_ref)
            dsk_ref[...] = jnp.zeros_like(dsk_ref)

        band = pl.ds(pl.multiple_of(n * WINDOW, WINDOW), 2 * WINDOW)
        dk_ref[0, band, :] += jnp.concatenate([dkp, dkc], axis=0)
        dv_ref[0, band, :] += jnp.concatenate([dvp, dvc], axis=0)
        for g in heads:
            dsk_ref[0, g] += dsk[g]

    kv_shape = jax.ShapeDtypeStruct((ATT_KV_HEADS, s + WINDOW, ATT_HEAD_DIM), F32)
    return _pc(body, name="attn_bwd", grid=(ATT_KV_HEADS, nb), in_specs=[q_spec, prev, cur, prev, cur, sk, q_spec],
               out_specs=[q_spec, acc, acc, sk],
               out_shape=[jax.ShapeDtypeStruct(qh.shape, F32), kv_shape, kv_shape, jax.ShapeDtypeStruct(sinks4.shape, F32)],
               compiler_params=_cparams(("arbitrary", "arbitrary")))(qh, kh, kh, vh, vh, sinks4, doh)


def _bdot(a, b, dims=(((1,), (0,)), ((), ()))):
    return lax.dot_general(a.astype(BF16), b.astype(BF16), dims, preferred_element_type=F32)


def _hdot(a, b, dims=(((1,), (0,)), ((), ()))):
    return lax.dot_general(a, b, dims, precision=HI, preferred_element_type=F32)


_NN = (((1,), (0,)), ((), ()))
_NT = (((1,), (1,)), ((), ()))
_TN = (((0,), (0,)), ((), ()))


def _split2(a):
    hi = a.astype(BF16)
    return hi, (a - hi.astype(F32)).astype(BF16)


def _dot3(a, b, dims):
    ah, al = _split2(a)
    bh, bl = _split2(b)
    d = lambda p, q: lax.dot_general(p, q, dims, preferred_element_type=F32)
    return d(ah, bh) + (d(ah, bl) + d(al, bh))


@jax.custom_vjp
def _xdot(a, b):
    return _dot3(a, b, _NN)


def _xdot_fwd(a, b):
    return _dot3(a, b, _NN), (a, b)


def _xdot_bwd(res, g):
    a, b = res
    return _dot3(g, b, _NT), _dot3(a, g, _TN)


_xdot.defvjp(_xdot_fwd, _xdot_bwd)


def _mask_dot(mask16, b, dims):
    hi = b.astype(BF16)
    r = b - hi.astype(F32)
    mid = r.astype(BF16)
    lo = (r - mid.astype(F32)).astype(BF16)
    d = lambda q: lax.dot_general(mask16, q, dims, preferred_element_type=F32)
    return d(hi) + (d(mid) + d(lo))


def _chunk_masks():
    r = lax.broadcasted_iota(jnp.int32, (CHUNK, CHUNK), 0)
    c = lax.broadcasted_iota(jnp.int32, (CHUNK, CHUNK), 1)
    return r >= c, r > c, (r == c).astype(F32)


def _dn_local(qs, ks, vs, bs, gs, masks):
    causal, strict, eye = masks
    rng = range(len(qs))
    gb = [jnp.broadcast_to(gs[i], (CHUNK, CHUNK)) for i in rng]
    decay = [jnp.exp(jnp.where(causal, gb[i] - gb[i].T, NEG)) for i in rng]
    kb = [ks[i] * bs[i] for i in rng]
    vb = [vs[i] * bs[i] for i in rng]
    kk = [_bdot(kb[i], ks[i], _NT) for i in rng]
    p = [-jnp.where(strict, kk[i] * decay[i], 0.0) for i in rng]
    t = [eye + p[i] for i in rng]
    for _ in range(5):
        p = [_xdot(p[i], p[i]) for i in rng]
        t = [t[i] + _xdot(p[i], t[i]) for i in rng]
    eg = [jnp.exp(gs[i]) for i in rng]
    u = [_xdot(t[i], vb[i]) for i in rng]
    w = [_xdot(t[i], kb[i] * eg[i]) for i in rng]
    qk = [_bdot(qs[i], ks[i], _NT) for i in rng]
    intra = [qk[i] * decay[i] for i in rng]
    q_dec = [qs[i] * eg[i] for i in rng]
    k_dec = [ks[i] * jnp.exp(gs[i][CHUNK - 1:CHUNK, :] - gs[i]) for i in rng]
    return u, w, intra, q_dec, k_dec


def _dn_state(u, w, intra, q_dec, k_dec, gcum, state):
    v_new = u - _bdot(w, state)
    o = _bdot(q_dec, state) + _bdot(intra, v_new)
    new_state = state * jnp.exp(gcum[CHUNK - 1:CHUNK, :]) + _bdot(k_dec, v_new, _TN)
    return o, new_state


def _l2norm(t):
    return t * lax.rsqrt(jnp.sum(jnp.square(t), axis=-1, keepdims=True) + RMS_EPS)


def _dn_pre(aq, ak, av, ba, alog, dtb, h):
    lane = lax.broadcasted_iota(jnp.int32, (1, LANES), 1)
    pick = lambda t, i: jnp.sum(jnp.where(lane == i, t, 0.0), axis=1, keepdims=True)
    q = _l2norm(_silu(aq)) * (DN_HEAD_DIM ** -0.5)
    k = _l2norm(_silu(ak))
    v = _silu(av)
    beta = jax.nn.sigmoid(pick(ba, h))
    g = -jnp.exp(pick(alog, h)) * _softplus(pick(ba, h + DN_HEADS) + pick(dtb, h))
    return q, k, v, beta, g


def _dn_post(o, z, nw):
    o = o * lax.rsqrt(jnp.mean(jnp.square(o), axis=-1, keepdims=True) + RMS_EPS) * nw
    return o * _silu(z)


_PAD = 8
_TOK_TILE = 512


def _pad_front(pad_ref, x_ref, s):
    pad_ref[pl.ds(0, _PAD), :] = jnp.zeros((_PAD, pad_ref.shape[1]), F32)
    pad_ref[pl.ds(_PAD, s), :] = x_ref[...]


def _conv_tile(pad_ref, w4, r0, n):
    acc = None
    for j in range(CONV_K):
        term = pad_ref[pl.ds(r0 + _PAD - (CONV_K - 1) + j, n), :] * w4[j:j + 1, :]
        acc = term if acc is None else acc + term
    return acc


def _conv_tile_bwd(pad_ref, da_ref, w4, r0, n):
    dx, dw = None, []
    da = da_ref[pl.ds(r0, n), :]
    for j in range(CONV_K):
        term = da_ref[pl.ds(r0 + CONV_K - 1 - j, n), :] * w4[j:j + 1, :]
        dx = term if dx is None else dx + term
        dw.append(jnp.sum(da * pad_ref[pl.ds(r0 + _PAD - (CONV_K - 1) + j, n), :], axis=0, keepdims=True))
    return dx, jnp.concatenate(dw, axis=0)


def _dn_gcum(g_c, causal_f):
    return _mask_dot(causal_f, jnp.broadcast_to(g_c, (CHUNK, LANES)), _NN)[:, 0:1]


def _dn_in_specs(s):
    col = lambda base: pl.BlockSpec((s, DN_HEAD_DIM), functools.partial(lambda h, b: (0, b + h), b=base // DN_HEAD_DIM))
    cw = lambda base: pl.BlockSpec((CONV_K, DN_HEAD_DIM), functools.partial(lambda h, b: (0, b + h), b=base))
    row = pl.BlockSpec((1, LANES), lambda h: (0, 0))
    ba = pl.BlockSpec((s, LANES), lambda h: (0, C_BA // LANES))
    return [col(C_DQ), col(C_DK), col(C_DV), col(C_Z), ba, cw(0), cw(DN_HEADS), cw(2 * DN_HEADS), row, row, row]


def _chunk_rows(c):
    return pl.ds(pl.multiple_of(c * CHUNK, CHUNK), CHUNK)


def _group(nchunk, want):
    g = min(want, nchunk)
    assert nchunk % g == 0
    return g


def _dn_forward_scan(q_s, k_s, v_s, b_s, g_s, gc_s, loc, o_s, states_ref, s):
    masks = _chunk_masks()
    causal_f = masks[0].astype(BF16)
    nchunk = s // CHUNK
    grp = _group(nchunk, 4)
    u_s, w_s, in_s, qd_s, kd_s = loc

    def local_step(i, carry):
        rows = [_chunk_rows(i * grp + j) for j in range(grp)]
        gcum = [_dn_gcum(g_s[r, :], causal_f) for r in rows]
        u, w, intra, q_dec, k_dec = _dn_local([q_s[r, :] for r in rows], [k_s[r, :] for r in rows],
                                              [v_s[r, :] for r in rows], [b_s[r, :] for r in rows], gcum, masks)
        for j, r in enumerate(rows):
            gc_s[r, :] = gcum[j]
            u_s[r, :] = u[j]
            w_s[r, :] = w[j].astype(w_s.dtype)
            in_s[r, :] = intra[j].astype(in_s.dtype)
            qd_s[r, :] = q_dec[j].astype(qd_s.dtype)
            kd_s[r, :] = k_dec[j].astype(kd_s.dtype)
        return carry

    lax.fori_loop(0, nchunk // grp, local_step, 0)

    def state_step(i, state):
        rows = _chunk_rows(i)
        if states_ref is not None:
            states_ref[i] = state
        o, state = _dn_state(u_s[rows, :], w_s[rows, :], in_s[rows, :], qd_s[rows, :], kd_s[rows, :], gc_s[rows, :], state)
        o_s[rows, :] = o
        return state

    lax.fori_loop(0, nchunk, state_step, jnp.zeros((DN_HEAD_DIM, DN_HEAD_DIM), F32))


def _dn_saved_shapes(s):
    d, h = DN_HEAD_DIM, DN_HEADS
    shapes = [((h, s, d), F32), ((h, s, d), BF16), ((h, s, CHUNK), BF16), ((h, s, d), BF16), ((h, s, d), BF16),
              ((h, s, 1), F32), ((h, s // CHUNK, d, d), F32), ((h, s, d), F32)]
    return [jax.ShapeDtypeStruct(shp, dt) for shp, dt in shapes]


def _dn_saved_specs(s, **kw):
    return [pl.BlockSpec((1,) + t.shape[1:], functools.partial(lambda h, nd: (h,) + (0,) * nd, nd=len(t.shape) - 1), **kw)
            for t in _dn_saved_shapes(s)]


def _call_with_exchange(body, name, steps, in_specs, out_specs, out_shape, scratch, args, exchange):
    steps = (steps,) if isinstance(steps, int) else tuple(steps)
    params = _cparams(("arbitrary",) * len(steps))
    if exchange is None:
        res = _pc(body, name=name, grid=steps, in_specs=in_specs, out_specs=out_specs, out_shape=out_shape,
                  scratch_shapes=scratch, compiler_params=params)(*args)
        return res, None
    arrays, gather = exchange
    x_in, x_out, x_shape, x_scratch = _exchange_specs(arrays, gather)
    wrapped = _carry_exchange(body, len(in_specs), len(out_specs), len(scratch), len(arrays), gather, steps)
    res = _pc(wrapped, name=name + "_x", grid=steps, in_specs=in_specs + x_in, out_specs=out_specs + x_out,
              out_shape=out_shape + x_shape, scratch_shapes=scratch + x_scratch, compiler_params=params)(*args, *arrays)
    return res[:len(out_specs)], res[len(out_specs):]


def _dn_fwd(proj, conv_w, alog, dtb, nw, exchange=None):
    s = proj.shape[0]
    d = DN_HEAD_DIM

    tt = min(_TOK_TILE, s)

    def body(xq, xk, xv, z, ba, wq, wk, wv, alog_r, dtb_r, nw_r, o_ref, u_o, w_o, in_o, qd_o, kd_o, gc_o, st_o, oraw_o,
             padq, padk, padv, q_s, k_s, v_s, b_s, g_s):
        h = pl.program_id(0)
        loc = [r.at[0] for r in (u_o, w_o, in_o, qd_o, kd_o)]
        gc_s, states, o_s = gc_o.at[0], st_o.at[0], oraw_o.at[0]
        _pad_front(padq, xq, s)
        _pad_front(padk, xk, s)
        _pad_front(padv, xv, s)
        for r0 in range(0, s, tt):
            rows = pl.ds(r0, tt)
            aq, ak, av = _conv_tile(padq, wq[...], r0, tt), _conv_tile(padk, wk[...], r0, tt), _conv_tile(padv, wv[...], r0, tt)
            q_s[rows, :], k_s[rows, :], v_s[rows, :], b_s[rows, :], g_s[rows, :] = _dn_pre(
                aq, ak, av, ba[rows, :], alog_r[...], dtb_r[...], h)
        _dn_forward_scan(q_s, k_s, v_s, b_s, g_s, gc_s, loc, o_s, states, s)
        for r0 in range(0, s, tt):
            rows = pl.ds(r0, tt)
            o_ref[rows, :] = _dn_post(o_s[rows, :], z[rows, :], nw_r[...]).astype(o_ref.dtype)

    big = pltpu.VMEM((s, d), F32)
    thin = pltpu.VMEM((s, 1), F32)
    padded = pltpu.VMEM((s + _PAD, d), F32)
    return _call_with_exchange(
        body, "dn_fwd", DN_HEADS, _dn_in_specs(s), [pl.BlockSpec((s, d), lambda h: (0, h))] + _dn_saved_specs(s),
        [jax.ShapeDtypeStruct((s, DN_HEADS * d), BF16)] + _dn_saved_shapes(s),
        [padded, padded, padded, big, big, big, thin, thin],
        (proj, proj, proj, proj, proj, conv_w, conv_w, conv_w, alog, dtb, nw), exchange)


def _dn_bwd(proj, conv_w, alog, dtb, nw, dob, kept, exchange=None):
    s = proj.shape[0]
    d = DN_HEAD_DIM
    nchunk = s // CHUNK

    tt = min(_TOK_TILE, s)

    def body(xq, xk, xv, z, ba, wq, wk, wv, alog_r, dtb_r, nw_r, dob_ref, u_i, w_i, in_i, qd_i, kd_i, gc_i, st_i, oraw_i,
             dxq, dxk, dxv, dz, dba, dwq, dwk, dwv, dalog, ddtb, dnw,
             padq, padk, padv, q_s, k_s, v_s, b_s, g_s, o_s, dq_s, dk_s, dv_s, db_s, dg_s, dkd_s, din_s, dgc_s):
        h = pl.program_id(0)
        u_s, w_s, in_s, qd_s, kd_s = [r.at[0] for r in (u_i, w_i, in_i, qd_i, kd_i)]
        gc_s, states, oraw = gc_i.at[0], st_i.at[0], oraw_i.at[0]
        masks = _chunk_masks()
        causal_f = masks[0].astype(BF16)
        pre = functools.partial(_dn_pre, h=h)
        _pad_front(padq, xq, s)
        _pad_front(padk, xk, s)
        _pad_front(padv, xv, s)

        def conv_tiles(r0):
            return _conv_tile(padq, wq[...], r0, tt), _conv_tile(padk, wk[...], r0, tt), _conv_tile(padv, wv[...], r0, tt)

        for r0 in range(0, s, tt):
            rows = pl.ds(r0, tt)
            q_s[rows, :], k_s[rows, :], v_s[rows, :], b_s[rows, :], g_s[rows, :] = pre(
                *conv_tiles(r0), ba[rows, :], alog_r[...], dtb_r[...])
        dnw_v = jnp.zeros((1, LANES), F32)
        for r0 in range(0, s, tt):
            rows = pl.ds(r0, tt)
            _, post_vjp = jax.vjp(_dn_post, oraw[rows, :], z[rows, :], nw_r[...])
            do_raw, dz_v, dnw_t = post_vjp(dob_ref[rows, :].astype(F32))
            dz[rows, :] = dz_v.astype(dz.dtype)
            o_s[rows, :] = do_raw
            dnw_v = dnw_v + dnw_t

        def state_step(i, dstate):
            c = nchunk - 1 - i
            rows = _chunk_rows(c)
            _, vjp = jax.vjp(_dn_state, u_s[rows, :], w_s[rows, :].astype(F32), in_s[rows, :].astype(F32),
                             qd_s[rows, :].astype(F32), kd_s[rows, :].astype(F32), gc_s[rows, :], states[c])
            du, dw, din, dqd, dkd, dgc, dstate = vjp((o_s[rows, :], dstate))
            dq_s[rows, :] = du
            dk_s[rows, :] = dw
            dv_s[rows, :] = dqd
            dkd_s[rows, :] = dkd
            din_s[rows, :] = din
            dgc_s[rows, :] = dgc
            return dstate

        lax.fori_loop(0, nchunk, state_step, jnp.zeros((d, d), F32))
        local = functools.partial(_dn_local, masks=masks)
        grp = _group(nchunk, 4)

        def local_step(i, carry):
            rows = [_chunk_rows(i * grp + j) for j in range(grp)]
            get = lambda ref: [ref[r, :] for r in rows]
            _, vjp = jax.vjp(local, get(q_s), get(k_s), get(v_s), get(b_s), get(gc_s))
            dq_c, dk_c, dv_c, db_c, dgc_c = vjp((get(dq_s), get(dk_s), get(din_s), get(dv_s), get(dkd_s)))
            dgc_c = [dgc_c[j] + dgc_s[r, :] for j, r in enumerate(rows)]
            dg_c = [_mask_dot(causal_f, jnp.broadcast_to(t, (CHUNK, LANES)), _TN)[:, 0:1] for t in dgc_c]
            for j, r in enumerate(rows):
                dq_s[r, :] = dq_c[j]
                dk_s[r, :] = dk_c[j]
                dv_s[r, :] = dv_c[j]
                db_s[r, :] = db_c[j]
                dg_s[r, :] = dg_c[j]
            return carry

        lax.fori_loop(0, nchunk // grp, local_step, 0)

        @pl.when(h == 0)
        def _():
            dba[...] = jnp.zeros_like(dba)
            dalog[...] = jnp.zeros_like(dalog)
            ddtb[...] = jnp.zeros_like(ddtb)
            dnw[...] = jnp.zeros_like(dnw)

        dalog_v = jnp.zeros((1, LANES), F32)
        ddtb_v = jnp.zeros((1, LANES), F32)
        for r0 in range(0, s, tt):
            rows = pl.ds(r0, tt)
            _, pre_vjp = jax.vjp(pre, *conv_tiles(r0), ba[rows, :], alog_r[...], dtb_r[...])
            daq, dak, dav, dba_t, dalog_t, ddtb_t = pre_vjp(
                (dq_s[rows, :], dk_s[rows, :], dv_s[rows, :], db_s[rows, :], dg_s[rows, :]))
            dq_s[rows, :], dk_s[rows, :], dv_s[rows, :] = daq, dak, dav
            dba[rows, :] += dba_t
            dalog_v = dalog_v + dalog_t
            ddtb_v = ddtb_v + ddtb_t
        tail = pl.ds(s, _PAD)
        dq_s[tail, :] = dk_s[tail, :] = dv_s[tail, :] = jnp.zeros((_PAD, d), F32)
        for pad, da_s, w_ref, dx_ref, dw_ref in ((padq, dq_s, wq, dxq, dwq), (padk, dk_s, wk, dxk, dwk), (padv, dv_s, wv, dxv, dwv)):
            dw_acc = jnp.zeros((CONV_K, d), F32)
            for r0 in range(0, s, tt):
                dx_t, dw_t = _conv_tile_bwd(pad, da_s, w_ref[...], r0, tt)
                dx_ref[pl.ds(r0, tt), :] = dx_t.astype(dx_ref.dtype)
                dw_acc = dw_acc + dw_t
            dw_ref[...] = dw_acc
        dalog[...] += dalog_v
        ddtb[...] += ddtb_v
        dnw[...] += dnw_v

    big = pltpu.VMEM((s, d), F32)
    thin = pltpu.VMEM((s, 1), F32)
    padded = pltpu.VMEM((s + _PAD, d), F32)
    w_all = DN_HEADS * d
    col_out = lambda: pl.BlockSpec((s, d), lambda h: (0, h))
    cw_out = lambda: pl.BlockSpec((CONV_K, d), lambda h: (0, h))
    row = lambda: pl.BlockSpec((1, LANES), lambda h: (0, 0))
    big_out = jax.ShapeDtypeStruct((s, w_all), BF16)
    cw_shape = jax.ShapeDtypeStruct((CONV_K, w_all), F32)
    row_shape = jax.ShapeDtypeStruct((1, LANES), F32)
    return _call_with_exchange(
        body, "dn_bwd", DN_HEADS,
        _dn_in_specs(s) + [pl.BlockSpec((s, d), lambda h: (0, h))] + _dn_saved_specs(s, pipeline_mode=pl.Buffered(1)),
        [col_out(), col_out(), col_out(), col_out(), pl.BlockSpec((s, LANES), lambda h: (0, 0)),
         cw_out(), cw_out(), cw_out(), row(), row(), row()],
        [big_out, big_out, big_out, big_out, jax.ShapeDtypeStruct((s, LANES), F32),
         cw_shape, cw_shape, cw_shape, row_shape, row_shape, row_shape],
        [padded, padded, padded, big, big, big, thin, thin, big,
         padded, padded, padded, thin, thin, big, pltpu.VMEM((s, CHUNK), F32), thin],
        (proj, proj, proj, proj, proj, conv_w, conv_w, conv_w, alog, dtb, nw, dob, *kept), exchange)


def _loss_head(y, target, tile=256):
    n, dm = y.shape
    tile = min(tile, n)

    def body(y_ref, t_ref, dy_ref, loss_ref):
        err = y_ref[...] - t_ref[...]
        dy_ref[...] = err * (1.0 / dm)

        @pl.when(pl.program_id(0) == 0)
        def _():
            loss_ref[...] = jnp.zeros_like(loss_ref)

        loss_ref[...] += 0.5 * jnp.sum(jnp.mean(jnp.square(err), axis=-1, keepdims=True), axis=0, keepdims=True)

    blk = pl.BlockSpec((tile, dm), lambda i: (i, 0))
    return _pc(body, name="loss_head", grid=(n // tile,), in_specs=[blk, blk],
               out_specs=[blk, pl.BlockSpec((1, 1), lambda i: (0, 0))],
               out_shape=[jax.ShapeDtypeStruct((n, dm), F32), jax.ShapeDtypeStruct((1, 1), F32)],
               compiler_params=_cparams(("arbitrary",)))(y, target)


def _ada_fwd(c_all, w_ada, b_shard):
    nl, dm, n = w_ada.shape

    def body(c_ref, w_ref, b_ref, o_ref):
        ca = _silu(c_ref[...]).astype(BF16)
        o_ref[0] = jnp.dot(ca, w_ref[0].astype(BF16), preferred_element_type=F32) + b_ref[0]

    return _pc(body, name="ada_fwd", grid=(nl,),
               in_specs=[pl.BlockSpec((N_DEV, dm), lambda l: (0, 0)), pl.BlockSpec((1, dm, n), lambda l: (l, 0, 0)),
                         pl.BlockSpec((1, 1, n), lambda l: (l, 0, 0))],
               out_specs=pl.BlockSpec((1, N_DEV, n), lambda l: (l, 0, 0)),
               out_shape=jax.ShapeDtypeStruct((nl, N_DEV, n), F32), compiler_params=_cparams())(c_all, w_ada, b_shard)


def _ada_bwd(c_all, dmod):
    nl, _, n = dmod.shape
    dm = c_all.shape[1]

    def body(c_ref, d_ref, o_ref):
        o_ref[0] = _hdot(_silu(c_ref[...]), d_ref[0], _TN)

    return _pc(body, name="ada_bwd", grid=(nl,),
               in_specs=[pl.BlockSpec((N_DEV, dm), lambda l: (0, 0)), pl.BlockSpec((1, N_DEV, n), lambda l: (l, 0, 0))],
               out_specs=pl.BlockSpec((1, dm, n), lambda l: (l, 0, 0)),
               out_shape=jax.ShapeDtypeStruct((nl, dm, n), F32), compiler_params=_cparams())(c_all, dmod)


def _adamw(g, w, m, v):
    m = ADAM_B1 * m + (1.0 - ADAM_B1) * g
    v = ADAM_B2 * v + (1.0 - ADAM_B2) * jnp.square(g)
    m_hat = m / (1.0 - ADAM_B1 ** ADAM_STEP)
    v_hat = v / (1.0 - ADAM_B2 ** ADAM_STEP)
    delta = -ADAM_LR * (m_hat / (jnp.sqrt(v_hat) + ADAM_EPS) + ADAM_WD * w)
    return delta, m, v


def _adam_call(parts, w, m, v, name, tile=128):
    shape = w.shape
    flat = lambda t: t.reshape(-1, shape[-1])
    width = shape[-1]

    def fn(*vals):
        g = vals[0] if len(parts) == 1 else vals[0] + vals[1]
        return (g,) + _adamw(g, *vals[len(parts):])

    rows = [_whole(flat(t)) for t in (*parts, w, m, v)]
    outs = _rowwise(fn, rows, [], [(width, F32)] * 4, name, tile=tile)
    return [o.reshape(shape) for o in outs]


def _sum_slots(r, name, tile=128):
    _, n, width = r.shape
    tile = min(tile, n)

    def body(r_ref, o_ref):
        acc = r_ref[0].astype(F32)
        for j in range(1, N_CHIPS):
            acc = acc + r_ref[j].astype(F32)
        o_ref[...] = acc

    return _pc(body, name=name, grid=(n // tile,), in_specs=[pl.BlockSpec((N_CHIPS, tile, width), lambda i: (0, i, 0))],
               out_specs=pl.BlockSpec((tile, width), lambda i: (i, 0)),
               out_shape=jax.ShapeDtypeStruct((n, width), F32), compiler_params=_cparams())(r)


def _small_adam(g_all, w, m, v):
    def body(g_ref, w_ref, m_ref, v_ref, og, od, om, ov):
        g = g_ref[0]
        for j in range(1, N_DEV):
            g = g + g_ref[j]
        og[...] = g
        od[...], om[...], ov[...] = _adamw(g, w_ref[...], m_ref[...], v_ref[...])

    vm = pl.BlockSpec(memory_space=pltpu.VMEM)
    shp = jax.ShapeDtypeStruct(w.shape, F32)
    return _pc(body, name="small_adam", in_specs=[vm] * 4, out_specs=[vm] * 4, out_shape=[shp] * 4,
               compiler_params=_cparams())(g_all, w, m, v)


def _place():
    return lax.axis_index("x"), lax.axis_index("y"), lax.axis_index("c")


def _flip(v, bit):
    return 1 - v if bit else v


def _all_gather8(a):
    r, n = a.shape

    def body(a_ref, o_ref, send_sems, recv_sems):
        x, y, c = _place()
        me = 4 * x + 2 * y + c
        o_ref[me] = a_ref[...]
        copies = []
        for k in range(1, N_DEV):
            peer = (_flip(x, k & 4), _flip(y, k & 2), _flip(c, k & 1))
            copies.append(pltpu.make_async_remote_copy(
                src_ref=a_ref, dst_ref=o_ref.at[me], send_sem=send_sems.at[k - 1], recv_sem=recv_sems.at[k - 1],
                device_id=peer, device_id_type=MESH))
        for cp in copies:
            cp.start()
        for k in range(1, N_DEV):
            px, py, pc_ = _flip(x, k & 4), _flip(y, k & 2), _flip(c, k & 1)
            pltpu.make_async_remote_copy(
                src_ref=a_ref, dst_ref=o_ref.at[4 * px + 2 * py + pc_], send_sem=send_sems.at[k - 1],
                recv_sem=recv_sems.at[k - 1], device_id=(px, py, pc_), device_id_type=MESH).wait_recv()
        for cp in copies:
            cp.wait_send()

    vm = pl.BlockSpec(memory_space=pltpu.VMEM)
    return _pc(body, name="all_gather8", in_specs=[vm], out_specs=vm,
               out_shape=jax.ShapeDtypeStruct((N_DEV, r, n), a.dtype),
               scratch_shapes=[pltpu.SemaphoreType.DMA((N_DEV - 1,)), pltpu.SemaphoreType.DMA((N_DEV - 1,))],
               compiler_params=_cparams())(a)


def _chip_exchange(arrays, gather, name):
    na = len(arrays)

    def body(*refs):
        ins, outs, sems = refs[:na], refs[na:2 * na], refs[2 * na:]
        _exchange_copies(ins, outs, sems, gather, start=True)
        _exchange_copies(ins, outs, sems, gather, start=False)

    in_specs, out_specs, out_shape, scratch = _exchange_specs(arrays, gather)
    return _pc(body, name=name, in_specs=in_specs, out_specs=out_specs, out_shape=out_shape, scratch_shapes=scratch,
               compiler_params=_cparams())(*arrays)


def _exchange_specs(arrays, gather):
    na = len(arrays)
    hbm = pl.BlockSpec(memory_space=pl.ANY)
    out_shape = [jax.ShapeDtypeStruct(((N_CHIPS,) + a.shape) if gather else a.shape, a.dtype) for a in arrays]
    n_remote = 4 if gather else 2
    scratch = [pltpu.SemaphoreType.DMA((3 * na,))] * n_remote + [pltpu.SemaphoreType.DMA((na,))]
    return [hbm] * na, [hbm] * na, out_shape, scratch


def _gather_copies(ins, outs, sems, start):
    send_i, recv_i, send_d, recv_d, local_sems = sems
    x, y, c = _place()
    me = 2 * x + y
    sibling = (x, y, 1 - c)
    ici_sends, ici_arrivals, hand_ons, hand_arrivals, locals_ = [], [], [], [], []
    for i in range(len(ins)):
        half = ins[i].shape[0] // 2
        mine, other = pl.ds(c * half, half), pl.ds((1 - c) * half, half)
        locals_.append(pltpu.make_async_copy(ins[i], outs[i].at[me], local_sems.at[i]))
        for j in range(1, N_CHIPS):
            px, py = _flip(x, j & 2), _flip(y, j & 1)
            peer = 2 * px + py
            k = i * 3 + j - 1
            ici = dict(send_sem=send_i.at[k], recv_sem=recv_i.at[k], device_id=(px, py, c), device_id_type=MESH)
            d2d = dict(send_sem=send_d.at[k], recv_sem=recv_d.at[k], device_id=sibling, device_id_type=MESH)
            ici_sends.append(pltpu.make_async_remote_copy(src_ref=ins[i].at[mine], dst_ref=outs[i].at[me, mine], **ici))
            ici_arrivals.append(pltpu.make_async_remote_copy(src_ref=ins[i].at[mine], dst_ref=outs[i].at[peer, mine], **ici))
            hand_ons.append(pltpu.make_async_remote_copy(
                src_ref=outs[i].at[peer, mine], dst_ref=outs[i].at[peer, mine], **d2d))
            hand_arrivals.append(pltpu.make_async_remote_copy(
                src_ref=outs[i].at[peer, other], dst_ref=outs[i].at[peer, other], **d2d))
    if start:
        for cp in locals_ + ici_sends:
            cp.start()
    else:
        for arrival, hand_on in zip(ici_arrivals, hand_ons):
            arrival.wait_recv()
            hand_on.start()
        for cp in hand_arrivals:
            cp.wait_recv()
        for cp in ici_sends + hand_ons:
            cp.wait_send()
        for cp in locals_:
            cp.wait()


def _exchange_copies(ins, outs, sems, gather, start):
    if gather:
        return _gather_copies(ins, outs, sems, start)
    send_sems, recv_sems, local_sems = sems
    x, y, c = _place()
    me = 2 * x + y
    sends, arrivals, locals_ = [], [], []
    for i in range(len(ins)):
        locals_.append(pltpu.make_async_copy(ins[i] if gather else ins[i].at[me], outs[i].at[me], local_sems.at[i]))
        for j in range(1, N_CHIPS):
            px, py = _flip(x, j & 2), _flip(y, j & 1)
            peer = 2 * px + py
            pair = dict(send_sem=send_sems.at[i * 3 + j - 1], recv_sem=recv_sems.at[i * 3 + j - 1],
                        device_id=(px, py, c), device_id_type=MESH)
            sends.append(pltpu.make_async_remote_copy(
                src_ref=ins[i] if gather else ins[i].at[peer], dst_ref=outs[i].at[me], **pair))
            arrivals.append(pltpu.make_async_remote_copy(
                src_ref=ins[i] if gather else ins[i].at[me], dst_ref=outs[i].at[peer], **pair))
    if start:
        for cp in locals_ + sends:
            cp.start()
    else:
        for cp in arrivals:
            cp.wait_recv()
        for cp in sends:
            cp.wait_send()
        for cp in locals_:
            cp.wait()


def _carry_exchange(body, n_in, n_out, n_scratch, n_arrays, gather, steps):
    def wrapped(*refs):
        na = n_arrays
        ins, xin = refs[:n_in], refs[n_in:n_in + na]
        outs = refs[n_in + na:n_in + na + n_out]
        xout = refs[n_in + na + n_out:n_in + 2 * na + n_out]
        rest = refs[n_in + 2 * na + n_out:]
        scratch, sems = rest[:n_scratch], rest[n_scratch:]
        first, last = True, True
        for axis, n in enumerate(steps):
            first = jnp.logical_and(first, pl.program_id(axis) == 0)
            last = jnp.logical_and(last, pl.program_id(axis) == n - 1)

        @pl.when(first)
        def _():
            _exchange_copies(xin, xout, sems, gather, start=True)

        body(*ins, *outs, *scratch)

        @pl.when(last)
        def _():
            _exchange_copies(xin, xout, sems, gather, start=False)

    return wrapped


def _sibling_exchange(arrays, name):
    na = len(arrays)

    def body(*refs):
        ins, outs = refs[:na], refs[na:2 * na]
        send_sems, recv_sems = refs[2 * na:]
        x, y, c = _place()
        copies = [pltpu.make_async_remote_copy(
            src_ref=ins[i], dst_ref=outs[i], send_sem=send_sems.at[i], recv_sem=recv_sems.at[i],
            device_id=(x, y, 1 - c), device_id_type=MESH) for i in range(na)]
        for cp in copies:
            cp.start()
        for cp in copies:
            cp.wait()

    hbm = pl.BlockSpec(memory_space=pl.ANY)
    return _pc(body, name=name, in_specs=[hbm] * na, out_specs=[hbm] * na,
               out_shape=[jax.ShapeDtypeStruct(a.shape, a.dtype) for a in arrays],
               scratch_shapes=[pltpu.SemaphoreType.DMA((na,)), pltpu.SemaphoreType.DMA((na,))],
               compiler_params=_cparams())(*arrays)


def _heads_q(t):
    s = t.shape[0]
    return t.reshape(s, ATT_KV_HEADS, ATT_GROUP, ATT_HEAD_DIM).transpose(1, 2, 0, 3)


def _unheads_q(t):
    s = t.shape[2]
    return t.transpose(2, 0, 1, 3).reshape(s, ATT_KV_HEADS * ATT_GROUP * ATT_HEAD_DIM)


def _heads_kv(t):
    s = t.shape[0]
    return t.reshape(s, ATT_KV_HEADS, ATT_HEAD_DIM).transpose(1, 0, 2)


def _unheads_kv(t):
    s = t.shape[1]
    return t.transpose(1, 0, 2).reshape(s, ATT_KV_HEADS * ATT_HEAD_DIM)


def _row128(v):
    return jnp.pad(v, (0, LANES - v.shape[0])).reshape(1, LANES)


def _layer_fwd(x, p, exchange=None):
    sh1, sc1, gt1, sh2, sc2, gt2 = [p["mod"][i] for i in range(6)]
    (u,) = _rowwise(_f_mod, [_whole(x)], [sc1, sh1], [(D_MODEL, BF16)], "mod1")
    proj = _mm(u, p["w_in"], "nn", F32, "proj")
    qh = _heads_q(proj[:, C_Q:C_Q + 1024])
    kh = _heads_kv(proj[:, C_K:C_K + 256])
    vh = _heads_kv(proj[:, C_V:C_V + 256])
    sinks4 = p["sinks"].reshape(ATT_KV_HEADS, ATT_GROUP, 1, 1)
    behind_attn = None if exchange is None else (exchange[0][-1:], exchange[1])
    behind_dn = None if exchange is None else (exchange[0][:-1], exchange[1])
    o_heads, got_attn = _attn_fwd(qh, kh, vh, sinks4, behind_attn)
    o_a = _unheads_q(o_heads)
    (o_b, *dn_kept), got_dn = _dn_fwd(proj, p["conv_w"], _row128(p["a_log"]), _row128(p["dt_bias"]),
                                      p["dn_norm_w"].reshape(1, LANES), behind_dn)
    exchanged = None if exchange is None else list(got_dn) + list(got_attn)
    y_a = _mm(o_a, p["w_oa"], "nn", BF16, "y_a")
    y_b = _mm(o_b, p["w_ob"], "nn", BF16, "y_b")
    (gm,) = _rowwise(_f_gate, [(proj, C_GA // 1024, 1024), (proj, C_GB // 1024, 1024), _whole(y_a), _whole(y_b)], [],
                     [(D_MODEL, BF16)], "gate")
    mixed = _mm(gm, p["w_out"], "nn", BF16, "mixed")
    x1, u2 = _rowwise(_f_post1, [_whole(x), _whole(mixed)], [gt1, p["ln1_g"], p["ln1_b"], sc2, sh2],
                      [(D_MODEL, F32), (D_MODEL, BF16)], "post1")
    hpre = _mm(u2, p["w_ff1"], "nn", BF16, "ff1")
    (h,) = _rowwise(_f_act, [_whole(hpre)], [p["b_ff1"]], [(D_FF, BF16)], "act")
    ff = _mm(h, p["w_ff2"], "nn", BF16, "ff2")
    (x2,) = _rowwise(_f_post2, [_whole(x1), _whole(ff)], [gt2, p["b_ff2"], p["ln2_g"], p["ln2_b"]],
                     [(D_MODEL, F32)], "post2")
    saved = dict(x=x, u=u, proj=proj, o_a=o_a, o_b=o_b, y_a=y_a, y_b=y_b, gm=gm, mixed=mixed, x1=x1, u2=u2,
                 hpre=hpre, h=h, ff=ff, dn_kept=dn_kept, heads=(qh, kh, vh))
    return x2, saved, exchanged


def _layer_bwd(dx2, p, sv, carry=None):
    sh1, sc1, gt1, sh2, sc2, gt2 = [p["mod"][i] for i in range(6)]
    g = {}
    (dx1_a, dff), (dgt2, g["b_ff2"], g["ln2_g"], g["ln2_b"]) = _rowwise_bwd(
        _f_post2, [_whole(sv["x1"]), _whole(sv["ff"])], [gt2, p["b_ff2"], p["ln2_g"], p["ln2_b"]], [dx2],
        [F32, BF16], "post2_bwd")
    dh = _mm(dff, p["w_ff2"], "nt", BF16,"dh")
    g["w_ff2"] = _mm(sv["h"], dff, "tn", BF16,"dw_ff2")
    (dhpre,), (g["b_ff1"],) = _rowwise_bwd(_f_act, [_whole(sv["hpre"])], [p["b_ff1"]], [dh], [BF16], "act_bwd")
    du2 = _mm(dhpre, p["w_ff1"], "nt", BF16,"du2")
    g["w_ff1"] = _mm(sv["u2"], dhpre, "tn", BF16,"dw_ff1")
    (dx_a, dmixed), (dgt1, g["ln1_g"], g["ln1_b"], dsc2, dsh2) = _rowwise_bwd(
        _f_post1, [_whole(sv["x"]), _whole(sv["mixed"])], [gt1, p["ln1_g"], p["ln1_b"], sc2, sh2], [dx1_a, du2],
        [F32, BF16], "post1_bwd")
    dgm = _mm(dmixed, p["w_out"], "nt", BF16,"dgm")
    g["w_out"] = _mm(sv["gm"], dmixed, "tn", BF16,"dw_out")
    proj = sv["proj"]
    (dga, dgb, dya, dyb), _ = _rowwise_bwd(
        _f_gate, [(proj, C_GA // 1024, 1024), (proj, C_GB // 1024, 1024), _whole(sv["y_a"]), _whole(sv["y_b"])], [],
        [dgm], [BF16, BF16, BF16, BF16], "gate_bwd")
    do_a = _mm(dya, p["w_oa"], "nt", BF16,"do_a")
    g["w_oa"] = _mm(sv["o_a"], dya, "tn", BF16,"dw_oa")
    do_b = _mm(dyb, p["w_ob"], "nt", BF16,"do_b")
    g["w_ob"] = _mm(sv["o_b"], dyb, "tn", BF16,"dw_ob")
    exchange = None if carry is None else (carry(g), False)
    (ddq, ddk, ddv, ddz, dba, dwq, dwk, dwv, dalog, ddtb, dnw), exchanged = _dn_bwd(
        proj, p["conv_w"], _row128(p["a_log"]), _row128(p["dt_bias"]), p["dn_norm_w"].reshape(1, LANES), do_b,
        sv["dn_kept"], exchange)
    g["conv_w"] = jnp.concatenate([dwq, dwk, dwv], axis=1)
    g["a_log"], g["dt_bias"], g["dn_norm_w"] = dalog[0, :DN_HEADS], ddtb[0, :DN_HEADS], dnw[0]
    qh, kh, vh = sv["heads"]
    sinks4 = p["sinks"].reshape(ATT_KV_HEADS, ATT_GROUP, 1, 1)
    dqh, dkh, dvh, dsk = _attn_bwd(qh, kh, vh, sinks4, _heads_q(do_a))
    g["sinks"] = dsk.reshape(ATT_KV_HEADS * ATT_GROUP)
    s = proj.shape[0]
    dproj = jnp.concatenate([
        _unheads_q(dqh).astype(BF16), ddq, ddk, ddv, ddz, dga, dgb,
        _unheads_kv(dkh[:, WINDOW:, :]).astype(BF16), _unheads_kv(dvh[:, WINDOW:, :]).astype(BF16),
        dba.astype(BF16), jnp.zeros((s, D_IN_P - C_BA - LANES), BF16)], axis=1)
    du = _mm(dproj, p["w_in"], "nt", BF16,"du")
    g["w_in"] = _mm(sv["u"], dproj, "tn", BF16,"dw_in")
    (dx,), (dsc1, dsh1) = _rowwise_bwd(_f_mod, [_whole(sv["x"])], [sc1, sh1], [du], [F32], "mod1_bwd", add=(0, dx_a))
    g["mod"] = jnp.stack([dsh1, dsc1, dgt1, dsh2, dsc2, dgt2])
    return dx, g, exchanged


def _permute_w_in(w):
    pad = jnp.zeros(w.shape[:-1] + (D_IN_P - D_IN,), w.dtype)
    return jnp.concatenate([w[..., 0:1024], w[..., 1536:5632], w[..., 5648:7696], w[..., 1024:1536],
                            w[..., 5632:5648], pad], axis=-1)


def _unpermute_w_in(g):
    return jnp.concatenate([g[..., 0:1024], g[..., C_K:C_K + 512], g[..., 1024:5120], g[..., C_BA:C_BA + 16],
                            g[..., 5120:7168]], axis=-1)


def _cols_from_chips(t):
    c, l, r, n = t.shape
    return t.transpose(1, 2, 0, 3).reshape(l, r, c * n)


def _cols_to_chips(t):
    l, r, n4 = t.shape
    return t.reshape(l, r, N_CHIPS, n4 // N_CHIPS).transpose(2, 0, 1, 3)


def _rows_from_chips(t):
    c, l, r, n = t.shape
    return t.transpose(1, 0, 2, 3).reshape(l, c * r, n)


def _rows_to_chips(t):
    l, r4, n = t.shape
    return t.reshape(l, N_CHIPS, r4 // N_CHIPS, n).transpose(1, 0, 2, 3)


_REPLICATED = ("b_ada", "a_log", "dt_bias", "sinks", "dn_norm_w", "ln1_g", "ln1_b", "b_ff1", "b_ff2", "ln2_g", "ln2_b")
_SMALL = _REPLICATED + ("conv_w",)
_PACK_W = 1024
_WEIGHT_ORDER = ("w_ada", "b_ada", "w_in", "conv_w", "a_log", "dt_bias", "sinks", "dn_norm_w", "w_oa", "w_ob", "w_out",
                 "ln1_g", "ln1_b", "w_ff1", "b_ff1", "w_ff2", "b_ff2", "ln2_g", "ln2_b")


def _pack_small(d):
    flat = jnp.concatenate([d[k].reshape(-1) for k in _SMALL])
    rows = -(-flat.shape[0] // (_PACK_W * 8)) * 8
    return jnp.pad(flat, (0, rows * _PACK_W - flat.shape[0])).reshape(rows, _PACK_W)


def _unpack_small(packed, shapes):
    flat = packed.reshape(-1)
    out, off = {}, 0
    for k in _SMALL:
        n = 1
        for d_ in shapes[k]:
            n *= d_
        out[k] = flat[off:off + n].reshape(shapes[k])
        off += n
    return out


def kernel(x, c, w_ada, b_ada, w_in, conv_w, a_log, dt_bias, sinks, dn_norm_w, w_oa, w_ob, w_out, ln1_g, ln1_b, w_ff1, b_ff1, w_ff2, b_ff2, ln2_g, ln2_b, loss_target, m_w_ada, m_b_ada, m_w_in, m_conv_w, m_a_log, m_dt_bias, m_sinks, m_dn_norm_w, m_w_oa, m_w_ob, m_w_out, m_ln1_g, m_ln1_b, m_w_ff1, m_b_ff1, m_w_ff2, m_b_ff2, m_ln2_g, m_ln2_b, v_w_ada, v_b_ada, v_w_in, v_conv_w, v_a_log, v_dt_bias, v_sinks, v_dn_norm_w, v_w_oa, v_w_ob, v_w_out, v_ln1_g, v_ln1_b, v_w_ff1, v_b_ff1, v_w_ff2, v_b_ff2, v_ln2_g, v_ln2_b):
    ix, iy, ic = _place()
    chip = 2 * ix + iy
    dev = 4 * ix + 2 * iy + ic
    weights = dict(w_ada=w_ada, b_ada=b_ada, w_in=w_in, conv_w=conv_w, a_log=a_log, dt_bias=dt_bias, sinks=sinks,
                   dn_norm_w=dn_norm_w, w_oa=w_oa, w_ob=w_ob, w_out=w_out, ln1_g=ln1_g, ln1_b=ln1_b, w_ff1=w_ff1,
                   b_ff1=b_ff1, w_ff2=w_ff2, b_ff2=b_ff2, ln2_g=ln2_g, ln2_b=ln2_b)
    mom_m = dict(w_ada=m_w_ada, b_ada=m_b_ada, w_in=m_w_in, conv_w=m_conv_w, a_log=m_a_log, dt_bias=m_dt_bias,
                 sinks=m_sinks, dn_norm_w=m_dn_norm_w, w_oa=m_w_oa, w_ob=m_w_ob, w_out=m_w_out, ln1_g=m_ln1_g,
                 ln1_b=m_ln1_b, w_ff1=m_w_ff1, b_ff1=m_b_ff1, w_ff2=m_w_ff2, b_ff2=m_b_ff2, ln2_g=m_ln2_g, ln2_b=m_ln2_b)
    mom_v = dict(w_ada=v_w_ada, b_ada=v_b_ada, w_in=v_w_in, conv_w=v_conv_w, a_log=v_a_log, dt_bias=v_dt_bias,
                 sinks=v_sinks, dn_norm_w=v_dn_norm_w, w_oa=v_w_oa, w_ob=v_w_ob, w_out=v_w_out, ln1_g=v_ln1_g,
                 ln1_b=v_ln1_b, w_ff1=v_w_ff1, b_ff1=v_b_ff1, w_ff2=v_w_ff2, b_ff2=v_b_ff2, ln2_g=v_ln2_g, ln2_b=v_ln2_b)

    n_ada = w_ada.shape[2]
    c_all = _all_gather8(jnp.pad(c, ((0, 7), (0, 0))))[:, 0, :]
    b_shard = lax.dynamic_slice_in_dim(b_ada, chip * n_ada, n_ada, axis=1).reshape(DEPTH, 1, n_ada)
    mod_t = _ada_fwd(c_all, w_ada, b_shard)
    mod_all = _all_gather8(mod_t.reshape(DEPTH * N_DEV, n_ada)).reshape(N_DEV, DEPTH, N_DEV, n_ada)
    mod_mine = lax.dynamic_index_in_dim(mod_all[0::2], dev, axis=2, keepdims=False)
    mod = mod_mine.transpose(1, 0, 2).reshape(DEPTH, 6, 1, D_MODEL)

    n_cw = conv_w.shape[2]
    cw_all = _all_gather8(conv_w.reshape(DEPTH * CONV_K, n_cw))[0::2]
    conv_full = cw_all.transpose(1, 0, 2).reshape(DEPTH, CONV_K, N_CHIPS * n_cw)

    big = ("w_in", "w_oa", "w_ob", "w_out", "w_ff1", "w_ff2")
    w16 = {k: weights[k].astype(BF16) for k in big}
    shards = lambda l: [w16[k][l] for k in big]

    def assemble(gathered):
        gw = {k: t[:, None] for k, t in zip(big, gathered)}
        full = dict(w_in=_permute_w_in(_cols_from_chips(gw["w_in"])), w_ff1=_cols_from_chips(gw["w_ff1"]),
                    w_oa=_rows_from_chips(gw["w_oa"]), w_ob=_rows_from_chips(gw["w_ob"]),
                    w_out=_rows_from_chips(gw["w_out"]), w_ff2=_rows_from_chips(gw["w_ff2"]))
        return {k: t[0] for k, t in full.items()}

    to_chips = dict(w_in=lambda t: _cols_to_chips(_unpermute_w_in(t)), w_ff1=_cols_to_chips, w_oa=_rows_to_chips,
                    w_ob=_rows_to_chips, w_out=_rows_to_chips, w_ff2=_rows_to_chips)

    def slices_for_chips(g, keys):
        return [to_chips[k](g[k][None])[:, 0].astype(BF16) for k in keys]

    full = [None] * DEPTH
    full[0] = assemble(_chip_exchange(shards(0), True, "gather_weights"))

    def layer_params(l):
        p = dict(full[l])
        p["mod"] = mod[l]
        p["conv_w"] = conv_full[l]
        for k in ("a_log", "dt_bias", "sinks", "dn_norm_w"):
            p[k] = weights[k][l]
        for k in ("ln1_g", "ln1_b", "b_ff1", "b_ff2", "ln2_g", "ln2_b"):
            p[k] = weights[k][l].reshape(1, -1)
        return p

    xs = x[0]
    saved = []
    for l in range(DEPTH):
        nxt = (shards(l + 1), True) if l + 1 < DEPTH else None
        xs, sv, gathered = _layer_fwd(xs, layer_params(l), nxt)
        if nxt is not None:
            full[l + 1] = assemble(gathered)
        saved.append(sv)
    dy, loss_local = _loss_head(xs, loss_target[0])
    loss = lax.psum(loss_local[0, 0], ("x", "y", "c"))
    early = tuple(k for k in big if k != "w_in")
    grads = [None] * DEPTH
    received = [dict() for _ in range(DEPTH)]
    dx = dy
    pending = []
    for l in reversed(range(DEPTH)):
        carry = functools.partial(lambda g, first: first + slices_for_chips(g, early), first=pending)
        dx, grads[l], got = _layer_bwd(dx, layer_params(l), saved[l], carry)
        if pending:
            received[l + 1]["w_in"] = got[0]
        received[l].update(zip(early, got[len(pending):]))
        pending = slices_for_chips(grads[l], ("w_in",))
    received[0]["w_in"] = _chip_exchange(pending, False, "scatter_grads")[0]
    grad_x = dx[None]
    gstack = {k: jnp.stack([grads[l][k] for l in range(DEPTH)]) for k in grads[0] if k not in big}

    dmod = gstack["mod"].reshape(DEPTH, 6 * D_MODEL)
    small_g = dict(b_ada=dmod, a_log=gstack["a_log"], dt_bias=gstack["dt_bias"], sinks=gstack["sinks"],
                   dn_norm_w=gstack["dn_norm_w"], ln1_g=gstack["ln1_g"], ln1_b=gstack["ln1_b"], b_ff1=gstack["b_ff1"],
                   b_ff2=gstack["b_ff2"], ln2_g=gstack["ln2_g"], ln2_b=gstack["ln2_b"], conv_w=gstack["conv_w"])
    shapes = {k: weights[k].shape for k in _REPLICATED}
    shapes["conv_w"] = small_g["conv_w"].shape
    g_all = _all_gather8(_pack_small(small_g))
    no_conv = jnp.zeros(shapes["conv_w"], F32)
    small_out = _small_adam(g_all, _pack_small(dict(weights, conv_w=no_conv)), _pack_small(dict(mom_m, conv_w=no_conv)),
                            _pack_small(dict(mom_v, conv_w=no_conv)))
    small_res = [_unpack_small(t, shapes) for t in small_out]
    g_conv = lax.dynamic_slice_in_dim(small_res[0]["conv_w"], chip * n_cw, n_cw, axis=2)
    res = {"conv_w": _adam_call([g_conv], conv_w, m_conv_w, v_conv_w, "adam_conv_w", tile=16)}

    dmod_all = g_all.reshape(N_DEV, -1)[:, :DEPTH * 6 * D_MODEL].reshape(N_DEV, DEPTH, 6 * D_MODEL)
    dmod_shard = lax.dynamic_slice_in_dim(dmod_all, chip * n_ada, n_ada, axis=2).transpose(1, 0, 2)
    g_w_ada = _ada_bwd(c_all, dmod_shard)
    res["w_ada"] = _adam_call([g_w_ada], w_ada, m_w_ada, v_w_ada, "adam_w_ada")

    by_weight = [jnp.stack([received[l][k] for l in range(DEPTH)], axis=1) for k in big]
    partial = [_sum_slots(r.reshape(N_CHIPS, -1, r.shape[-1]), "sum_" + k) for k, r in zip(big, by_weight)]
    theirs = _sibling_exchange(partial, "sibling_grads")
    for k, mine, other in zip(big, partial, theirs):
        shape = weights[k].shape
        res[k] = _adam_call([mine.reshape(shape), other.reshape(shape)], weights[k], mom_m[k], mom_v[k], "adam_" + k)
    for k in _REPLICATED:
        res[k] = [small_res[i][k] for i in range(4)]

    outs = [loss, grad_x]
    for i in range(4):
        outs += [res[k][i] for k in _WEIGHT_ORDER]
    return tuple(outs)
```

```python
import functools

import jax
import jax.numpy as jnp
from jax import lax
from jax.experimental import pallas as pl
from jax.experimental.pallas import tpu as pltpu

F32, BF16 = jnp.float32, jnp.bfloat16
HI = lax.Precision.HIGHEST
MESH = pl.DeviceIdType.MESH

D_MODEL = 1024
DEPTH = 4
ATT_KV_HEADS, ATT_GROUP, ATT_HEAD_DIM, WINDOW = 4, 4, 64, 128
DN_HEADS, DN_HEAD_DIM, CONV_K, CHUNK = 8, 128, 4, 64
D_FF = 4 * D_MODEL
D_IN = 7696
ALPHA = (2 * DEPTH) ** 0.25
LN_EPS = 1e-5
RMS_EPS = 1e-6
ADAM_LR, ADAM_B1, ADAM_B2, ADAM_EPS, ADAM_WD, ADAM_STEP = 0.001, 0.9, 0.999, 1e-08, 0.01, 10

N_CHIPS = 4
N_DEV = 8
LANES = 128
D_IN_P = 8192
C_Q, C_DQ, C_DK, C_DV, C_Z, C_GA, C_GB, C_K, C_V, C_BA = 0, 1024, 2048, 3072, 4096, 5120, 6144, 7168, 7424, 7680
NEG = -1e30
VMEM_LIMIT = 56 << 20


def _pc(body, **kw):
    return pl.pallas_call(body, **kw)


def _cparams(sem=None):
    if sem is None:
        return pltpu.CompilerParams(vmem_limit_bytes=VMEM_LIMIT)
    return pltpu.CompilerParams(vmem_limit_bytes=VMEM_LIMIT, dimension_semantics=sem)


_MM_VMEM_BUDGET = 44 << 20
_MM_MIN_TILE = 256


def _mm_tiles(m, n, k, out_bytes):
    def halvings(d):
        out = [d]
        while out[-1] % 2 == 0 and out[-1] // 2 >= _MM_MIN_TILE:
            out.append(out[-1] // 2)
        return out

    best = None
    for tm in halvings(m):
        for tn in halvings(n):
            if 2 * (2 * tm * k + 2 * tn * k + out_bytes * tm * tn) > _MM_VMEM_BUDGET:
                continue
            cost = (2 * m * k + (m // tm) * 2 * n * k, (m // tm) * (n // tn))
            if best is None or cost < best[0]:
                best = (cost, tm, tn)
    assert best is not None, (m, n, k)
    return best[1], best[2]


def _mm(a, b, mode, out_dtype, name, tm=None, tn=None):
    if mode == "nn":
        (m, k), (_, n) = a.shape, b.shape
        dims = (((1,), (0,)), ((), ()))
    elif mode == "nt":
        (m, k), (n, _) = a.shape, b.shape
        dims = (((1,), (1,)), ((), ()))
    else:
        (k, m), (_, n) = a.shape, b.shape
        dims = (((0,), (0,)), ((), ()))
    if tm is None:
        tm, tn = _mm_tiles(m, n, k, jnp.dtype(out_dtype).itemsize)
    tm, tn = min(tm, m), min(tn, n)
    assert m % tm == 0 and n % tn == 0, (name, m, n, tm, tn)
    a_spec = pl.BlockSpec((k, tm), lambda i, j: (0, i)) if mode == "tn" else pl.BlockSpec((tm, k), lambda i, j: (i, 0))
    b_spec = pl.BlockSpec((tn, k), lambda i, j: (j, 0)) if mode == "nt" else pl.BlockSpec((k, tn), lambda i, j: (0, j))

    def body(a_ref, b_ref, o_ref):
        o_ref[...] = lax.dot_general(a_ref[...], b_ref[...], dims, preferred_element_type=F32).astype(o_ref.dtype)

    return _pc(body, name=name, grid=(m // tm, n // tn), in_specs=[a_spec, b_spec],
               out_specs=pl.BlockSpec((tm, tn), lambda i, j: (i, j)),
               out_shape=jax.ShapeDtypeStruct((m, n), out_dtype), compiler_params=_cparams())(a, b)


def _row_specs(rows, tile):
    return [pl.BlockSpec((tile, w), functools.partial(lambda i, cb: (i, cb), cb=cb)) for (_, cb, w) in rows]


def _vec_specs(vecs):
    return [pl.BlockSpec(v.shape, lambda i: (0, 0)) for v in vecs]


def _rowwise(fn, rows, vecs, outs, name, tile=256):
    n = rows[0][0].shape[0]
    tile = min(tile, n)
    nr, nv = len(rows), len(vecs)

    def body(*refs):
        rv = [r[...].astype(F32) for r in refs[:nr]]
        vv = [r[...] for r in refs[nr:nr + nv]]
        for o_ref, val in zip(refs[nr + nv:], fn(*rv, *vv)):
            o_ref[...] = val.astype(o_ref.dtype)

    res = _pc(body, name=name, grid=(n // tile,), in_specs=_row_specs(rows, tile) + _vec_specs(vecs),
              out_specs=[pl.BlockSpec((tile, w), lambda i: (i, 0)) for (w, _) in outs],
              out_shape=[jax.ShapeDtypeStruct((n, w), dt) for (w, dt) in outs],
              compiler_params=_cparams())(*[r[0] for r in rows], *vecs)
    return res


def _rowwise_bwd(fn, rows, vecs, cts, row_dtypes, name, tile=256, add=None):
    n = rows[0][0].shape[0]
    tile = min(tile, n)
    nr, nv, nc = len(rows), len(vecs), len(cts)
    want = [i for i, dt in enumerate(row_dtypes) if dt is not None]
    n_add = 0 if add is None else 1

    def body(*refs):
        rv = [r[...].astype(F32) for r in refs[:nr]]
        vv = [r[...] for r in refs[nr:nr + nv]]
        cv = [r[...].astype(F32) for r in refs[nr + nv:nr + nv + nc]]
        pos = nr + nv + nc
        add_ref = refs[pos] if n_add else None
        pos += n_add
        row_out = refs[pos:pos + len(want)]
        vec_out = refs[pos + len(want):]
        _, vjp = jax.vjp(fn, *rv, *vv)
        grads = vjp(tuple(cv))
        for o_ref, i in zip(row_out, want):
            gval = grads[i]
            if n_add and add[0] == i:
                gval = gval + add_ref[...]
            o_ref[...] = gval.astype(o_ref.dtype)

        @pl.when(pl.program_id(0) == 0)
        def _():
            for o_ref in vec_out:
                o_ref[...] = jnp.zeros_like(o_ref)

        for o_ref, gval in zip(vec_out, grads[nr:]):
            o_ref[...] += gval

    ct_rows = [(c, 0, c.shape[1]) for c in cts]
    add_rows = [(add[1], 0, add[1].shape[1])] if n_add else []
    res = _pc(body, name=name, grid=(n // tile,),
              in_specs=_row_specs(rows, tile) + _vec_specs(vecs) + _row_specs(ct_rows + add_rows, tile),
              out_specs=[pl.BlockSpec((tile, rows[i][2]), lambda i_: (i_, 0)) for i in want] + _vec_specs(vecs),
              out_shape=[jax.ShapeDtypeStruct((n, rows[i][2]), row_dtypes[i]) for i in want]
              + [jax.ShapeDtypeStruct(v.shape, F32) for v in vecs],
              compiler_params=_cparams(("arbitrary",)))(*[r[0] for r in rows], *vecs, *cts, *[a[0] for a in add_rows])
    return res[:len(want)], res[len(want):]


def _whole(a, cb=0, w=None):
    return (a, cb, a.shape[1] if w is None else w)


def _ln(x, g, b):
    mu = jnp.mean(x, axis=-1, keepdims=True)
    var = jnp.mean(jnp.square(x - mu), axis=-1, keepdims=True)
    return (x - mu) * lax.rsqrt(var + LN_EPS) * g + b


def _silu(x):
    return x * jax.nn.sigmoid(x)


def _softplus(x):
    return jnp.maximum(x, 0.0) + jnp.log(1.0 + jnp.exp(-jnp.abs(x)))


def _f_mod(x, sc, sh):
    return (x * (1.0 + sc) + sh,)


def _f_gate(ga, gb, ya, yb):
    return (jax.nn.sigmoid(ga) * ya + jax.nn.sigmoid(gb) * yb,)


def _f_post1(x, mixed, gt, g1, b1, sc2, sh2):
    x1 = _ln(ALPHA * x + (1.0 + gt) * mixed, g1, b1)
    return x1, x1 * (1.0 + sc2) + sh2


def _f_act(hpre, b):
    return (jnp.square(jnp.maximum(hpre + b, 0.0)),)


def _f_post2(x1, ff, gt, bff2, g2, b2):
    return (_ln(ALPHA * x1 + (1.0 + gt) * (ff + bff2), g2, b2),)


def _attn_valid(n):
    qi = lax.broadcasted_iota(jnp.int32, (WINDOW, 2 * WINDOW), 0)
    si = lax.broadcasted_iota(jnp.int32, (WINDOW, 2 * WINDOW), 1)
    diff = qi + WINDOW - si
    return (diff >= 0) & (diff < WINDOW) & (n * WINDOW + si - WINDOW >= 0)


def _attn_block(qs, kp, kc, vp, vc, sinks, valid):
    kband = jnp.concatenate([kp, kc], axis=0).astype(BF16)
    vband = jnp.concatenate([vp, vc], axis=0).astype(BF16)
    rng = range(len(qs))
    s = [lax.dot_general(qs[g].astype(BF16), kband, (((1,), (1,)), ((), ())), preferred_element_type=F32) for g in rng]
    s = [jnp.where(valid, s[g] * (ATT_HEAD_DIM ** -0.5), NEG) for g in rng]
    m = [lax.stop_gradient(jnp.maximum(jnp.max(s[g], axis=-1, keepdims=True), sinks[g])) for g in rng]
    p = [jnp.exp(s[g] - m[g]) for g in rng]
    denom = [jnp.sum(p[g], axis=-1, keepdims=True) + jnp.exp(sinks[g] - m[g]) for g in rng]
    probs = [(p[g] / denom[g]).astype(BF16) for g in rng]
    return [jnp.dot(probs[g], vband, preferred_element_type=F32) for g in rng]


def _attn_specs(s):
    nb = s // WINDOW
    q_spec = pl.BlockSpec((1, ATT_GROUP, WINDOW, ATT_HEAD_DIM), lambda h, n: (h, 0, n, 0))
    prev = pl.BlockSpec((1, WINDOW, ATT_HEAD_DIM), lambda h, n: (h, jnp.maximum(n - 1, 0), 0))
    cur = pl.BlockSpec((1, WINDOW, ATT_HEAD_DIM), lambda h, n: (h, n, 0))
    sk = pl.BlockSpec((1, ATT_GROUP, 1, 1), lambda h, n: (h, 0, 0, 0))
    return nb, q_spec, prev, cur, sk


def _attn_fwd(qh, kh, vh, sinks4, exchange=None):
    s = qh.shape[2]
    nb, q_spec, prev, cur, sk = _attn_specs(s)

    def body(q_ref, kp_ref, kc_ref, vp_ref, vc_ref, sk_ref, o_ref):
        valid = _attn_valid(pl.program_id(1))
        heads = range(ATT_GROUP)
        o = _attn_block([q_ref[0, g] for g in heads], kp_ref[0], kc_ref[0], vp_ref[0], vc_ref[0],
                        [sk_ref[0, g] for g in heads], valid)
        for g in heads:
            o_ref[0, g] = o[g].astype(o_ref.dtype)

    (o,), exchanged = _call_with_exchange(
        body, "attn_fwd", (ATT_KV_HEADS, nb), [q_spec, prev, cur, prev, cur, sk], [q_spec],
        [jax.ShapeDtypeStruct(qh.shape, BF16)], [], (qh, kh, kh, vh, vh, sinks4), exchange)
    return o, exchanged


def _attn_bwd(qh, kh, vh, sinks4, doh):
    s = qh.shape[2]
    nb, q_spec, prev, cur, sk = _attn_specs(s)
    acc = pl.BlockSpec((1, s + WINDOW, ATT_HEAD_DIM), lambda h, n: (h, 0, 0))

    def body(q_ref, kp_ref, kc_ref, vp_ref, vc_ref, sk_ref, do_ref, dq_ref, dk_ref, dv_ref, dsk_ref):
        n = pl.program_id(1)
        valid = _attn_valid(n)
        fn = functools.partial(_attn_block, valid=valid)
        heads = range(ATT_GROUP)
        _, vjp = jax.vjp(fn, [q_ref[0, g] for g in heads], kp_ref[0], kc_ref[0], vp_ref[0], vc_ref[0],
                         [sk_ref[0, g] for g in heads])
        dq, dkp, dkc, dvp, dvc, dsk = vjp([do_ref[0, g].astype(F32) for g in heads])
        for g in heads:
            dq_ref[0, g] = dq[g]

        @pl.when(n == 0)
        def _():
            dk_ref[...] = jnp.zeros_like(dk_ref)
            dv_ref[...] = jnp.zeros_like(dv_ref)
            dsk_ref[...] = jnp.zeros_like(dsk_ref)

        band = pl.ds(pl.multiple_of(n * WINDOW, WINDOW), 2 * WINDOW)
        dk_ref[0, band, :] += jnp.concatenate([dkp, dkc], axis=0)
        dv_ref[0, band, :] += jnp.concatenate([dvp, dvc], axis=0)
        for g in heads:
            dsk_ref[0, g] += dsk[g]

    kv_shape = jax.ShapeDtypeStruct((ATT_KV_HEADS, s + WINDOW, ATT_HEAD_DIM), F32)
    return _pc(body, name="attn_bwd", grid=(ATT_KV_HEADS, nb), in_specs=[q_spec, prev, cur, prev, cur, sk, q_spec],
               out_specs=[q_spec, acc, acc, sk],
               out_shape=[jax.ShapeDtypeStruct(qh.shape, F32), kv_shape, kv_shape, jax.ShapeDtypeStruct(sinks4.shape, F32)],
               compiler_params=_cparams(("arbitrary", "arbitrary")))(qh, kh, kh, vh, vh, sinks4, doh)


def _bdot(a, b, dims=(((1,), (0,)), ((), ()))):
    return lax.dot_general(a.astype(BF16), b.astype(BF16), dims, preferred_element_type=F32)


def _hdot(a, b, dims=(((1,), (0,)), ((), ()))):
    return lax.dot_general(a, b, dims, precision=HI, preferred_element_type=F32)


_NN = (((1,), (0,)), ((), ()))
_NT = (((1,), (1,)), ((), ()))
_TN = (((0,), (0,)), ((), ()))


def _split2(a):
    hi = a.astype(BF16)
    return hi, (a - hi.astype(F32)).astype(BF16)


def _dot3(a, b, dims):
    ah, al = _split2(a)
    bh, bl = _split2(b)
    d = lambda p, q: lax.dot_general(p, q, dims, preferred_element_type=F32)
    return d(ah, bh) + (d(ah, bl) + d(al, bh))


@jax.custom_vjp
def _xdot(a, b):
    return _dot3(a, b, _NN)


def _xdot_fwd(a, b):
    return _dot3(a, b, _NN), (a, b)


def _xdot_bwd(res, g):
    a, b = res
    return _dot3(g, b, _NT), _dot3(a, g, _TN)


_xdot.defvjp(_xdot_fwd, _xdot_bwd)


def _mask_dot(mask16, b, dims):
    hi = b.astype(BF16)
    r = b - hi.astype(F32)
    mid = r.astype(BF16)
    lo = (r - mid.astype(F32)).astype(BF16)
    d = lambda q: lax.dot_general(mask16, q, dims, preferred_element_type=F32)
    return d(hi) + (d(mid) + d(lo))


def _chunk_masks():
    r = lax.broadcasted_iota(jnp.int32, (CHUNK, CHUNK), 0)
    c = lax.broadcasted_iota(jnp.int32, (CHUNK, CHUNK), 1)
    return r >= c, r > c, (r == c).astype(F32)


def _dn_local(qs, ks, vs, bs, gs, masks):
    causal, strict, eye = masks
    rng = range(len(qs))
    gb = [jnp.broadcast_to(gs[i], (CHUNK, CHUNK)) for i in rng]
    decay = [jnp.exp(jnp.where(causal, gb[i] - gb[i].T, NEG)) for i in rng]
    kb = [ks[i] * bs[i] for i in rng]
    vb = [vs[i] * bs[i] for i in rng]
    kk = [_bdot(kb[i], ks[i], _NT) for i in rng]
    p = [-jnp.where(strict, kk[i] * decay[i], 0.0) for i in rng]
    t = [eye + p[i] for i in rng]
    for _ in range(5):
        p = [_xdot(p[i], p[i]) for i in rng]
        t = [t[i] + _xdot(p[i], t[i]) for i in rng]
    eg = [jnp.exp(gs[i]) for i in rng]
    u = [_xdot(t[i], vb[i]) for i in rng]
    w = [_xdot(t[i], kb[i] * eg[i]) for i in rng]
    qk = [_bdot(qs[i], ks[i], _NT) for i in rng]
    intra = [qk[i] * decay[i] for i in rng]
    q_dec = [qs[i] * eg[i] for i in rng]
    k_dec = [ks[i] * jnp.exp(gs[i][CHUNK - 1:CHUNK, :] - gs[i]) for i in rng]
    return u, w, intra, q_dec, k_dec


def _dn_state_bwd(u, w, intra, q_dec, k_dec, gcum, state, do, dnext):
    last = jnp.exp(gcum[CHUNK - 1:CHUNK, :])
    x = _bdot(k_dec, dnext)
    t1 = _bdot(intra, do, _TN)
    v_new = u - _bdot(w, state)
    dqd = _bdot(do, state, _NT)
    din = _bdot(do, v_new, _NT)
    dkd = _bdot(v_new, dnext, _NT)
    base = _bdot(q_dec, do, _TN) - _bdot(w, t1, _TN)
    d_vnew = t1 + x
    dw = -_bdot(d_vnew, state, _NT)
    dstate = dnext * last + base - _bdot(w, x, _TN)
    dlast = jnp.sum(jnp.sum(state * dnext, axis=1, keepdims=True), axis=0, keepdims=True)
    row = lax.broadcasted_iota(jnp.int32, (CHUNK, 1), 0)
    dgc = jnp.where(row == CHUNK - 1, dlast * last, 0.0)
    return d_vnew, dw, din, dqd, dkd, dgc, dstate


def _l2norm(t):
    return t * lax.rsqrt(jnp.sum(jnp.square(t), axis=-1, keepdims=True) + RMS_EPS)


def _dn_pre(aq, ak, av, ba, alog, dtb, h):
    lane = lax.broadcasted_iota(jnp.int32, (1, LANES), 1)
    pick = lambda t, i: jnp.sum(jnp.where(lane == i, t, 0.0), axis=1, keepdims=True)
    q = _l2norm(_silu(aq)) * (DN_HEAD_DIM ** -0.5)
    k = _l2norm(_silu(ak))
    v = _silu(av)
    beta = jax.nn.sigmoid(pick(ba, h))
    g = -jnp.exp(pick(alog, h)) * _softplus(pick(ba, h + DN_HEADS) + pick(dtb, h))
    return q, k, v, beta, g


def _dn_post(o, z, nw):
    o = o * lax.rsqrt(jnp.mean(jnp.square(o), axis=-1, keepdims=True) + RMS_EPS) * nw
    return o * _silu(z)


_PAD = 8
_TOK_TILE = 512


def _pad_front(pad_ref, x_ref, s):
    pad_ref[pl.ds(0, _PAD), :] = jnp.zeros((_PAD, pad_ref.shape[1]), F32)
    pad_ref[pl.ds(_PAD, s), :] = x_ref[...]


def _conv_tile(pad_ref, w4, r0, n):
    acc = None
    for j in range(CONV_K):
        term = pad_ref[pl.ds(r0 + _PAD - (CONV_K - 1) + j, n), :] * w4[j:j + 1, :]
        acc = term if acc is None else acc + term
    return acc


def _conv_tile_bwd(pad_ref, da_ref, w4, r0, n):
    dx, dw = None, []
    da = da_ref[pl.ds(r0, n), :]
    for j in range(CONV_K):
        term = da_ref[pl.ds(r0 + CONV_K - 1 - j, n), :] * w4[j:j + 1, :]
        dx = term if dx is None else dx + term
        dw.append(jnp.sum(da * pad_ref[pl.ds(r0 + _PAD - (CONV_K - 1) + j, n), :], axis=0, keepdims=True))
    return dx, jnp.concatenate(dw, axis=0)


def _dn_gcum(g_c, causal_f):
    return _mask_dot(causal_f, jnp.broadcast_to(g_c, (CHUNK, LANES)), _NN)[:, 0:1]


def _dn_in_specs(s):
    col = lambda base: pl.BlockSpec((s, DN_HEAD_DIM), functools.partial(lambda h, b: (0, b + h), b=base // DN_HEAD_DIM))
    cw = lambda base: pl.BlockSpec((CONV_K, DN_HEAD_DIM), functools.partial(lambda h, b: (0, b + h), b=base))
    row = pl.BlockSpec((1, LANES), lambda h: (0, 0))
    ba = pl.BlockSpec((s, LANES), lambda h: (0, C_BA // LANES))
    return [col(C_DQ), col(C_DK), col(C_DV), col(C_Z), ba, cw(0), cw(DN_HEADS), cw(2 * DN_HEADS), row, row, row]


def _chunk_rows(c):
    return pl.ds(pl.multiple_of(c * CHUNK, CHUNK), CHUNK)


def _group(nchunk, want):
    g = min(want, nchunk)
    assert nchunk % g == 0
    return g


def _dn_forward_scan(q_s, k_s, v_s, b_s, g_s, gc_s, loc, o_s, states_ref, kw_s, ku_s, s):
    masks = _chunk_masks()
    causal_f = masks[0].astype(BF16)
    nchunk = s // CHUNK
    grp = _group(nchunk, 4)
    u_s, w_s, in_s, qd_s, kd_s = loc

    def local_step(i, carry):
        rows = [_chunk_rows(i * grp + j) for j in range(grp)]
        gcum = [_dn_gcum(g_s[r, :], causal_f) for r in rows]
        u, w, intra, q_dec, k_dec = _dn_local([q_s[r, :] for r in rows], [k_s[r, :] for r in rows],
                                              [v_s[r, :] for r in rows], [b_s[r, :] for r in rows], gcum, masks)
        kw = [_bdot(k_dec[j], w[j], _TN) for j in range(grp)]
        ku = [_bdot(k_dec[j], u[j], _TN) for j in range(grp)]
        for j, r in enumerate(rows):
            kw_s[i * grp + j] = kw[j].astype(kw_s.dtype)
            ku_s[i * grp + j] = ku[j]
            gc_s[r, :] = gcum[j]
            u_s[r, :] = u[j]
            w_s[r, :] = w[j].astype(w_s.dtype)
            in_s[r, :] = intra[j].astype(in_s.dtype)
            qd_s[r, :] = q_dec[j].astype(qd_s.dtype)
            kd_s[r, :] = k_dec[j].astype(kd_s.dtype)
        return carry

    lax.fori_loop(0, nchunk // grp, local_step, 0)

    def state_step(i, state):
        rows = _chunk_rows(i)
        if states_ref is not None:
            states_ref[i] = state
        v_new = u_s[rows, :] - _bdot(w_s[rows, :], state)
        o_s[rows, :] = _bdot(qd_s[rows, :], state) + _bdot(in_s[rows, :], v_new)
        last = jnp.exp(gc_s[rows, :][CHUNK - 1:CHUNK, :])
        return state * last - _bdot(kw_s[i], state) + ku_s[i]

    lax.fori_loop(0, nchunk, state_step, jnp.zeros((DN_HEAD_DIM, DN_HEAD_DIM), F32))


def _dn_saved_shapes(s):
    d, h = DN_HEAD_DIM, DN_HEADS
    shapes = [((h, s, d), F32), ((h, s, d), BF16), ((h, s, CHUNK), BF16), ((h, s, d), BF16), ((h, s, d), BF16),
              ((h, s, 1), F32), ((h, s // CHUNK, d, d), F32), ((h, s, d), F32)]
    return [jax.ShapeDtypeStruct(shp, dt) for shp, dt in shapes]


def _dn_saved_specs(s, **kw):
    return [pl.BlockSpec((1,) + t.shape[1:], functools.partial(lambda h, nd: (h,) + (0,) * nd, nd=len(t.shape) - 1), **kw)
            for t in _dn_saved_shapes(s)]


def _call_with_exchange(body, name, steps, in_specs, out_specs, out_shape, scratch, args, exchange):
    steps = (steps,) if isinstance(steps, int) else tuple(steps)
    params = _cparams(("arbitrary",) * len(steps))
    if exchange is None:
        res = _pc(body, name=name, grid=steps, in_specs=in_specs, out_specs=out_specs, out_shape=out_shape,
                  scratch_shapes=scratch, compiler_params=params)(*args)
        return res, None
    arrays, gather = exchange
    x_in, x_out, x_shape, x_scratch = _exchange_specs(arrays, gather)
    wrapped = _carry_exchange(body, len(in_specs), len(out_specs), len(scratch), len(arrays), gather, steps)
    res = _pc(wrapped, name=name + "_x", grid=steps, in_specs=in_specs + x_in, out_specs=out_specs + x_out,
              out_shape=out_shape + x_shape, scratch_shapes=scratch + x_scratch, compiler_params=params)(*args, *arrays)
    return res[:len(out_specs)], res[len(out_specs):]


def _dn_fwd(proj, conv_w, alog, dtb, nw, exchange=None):
    s = proj.shape[0]
    d = DN_HEAD_DIM

    tt = min(_TOK_TILE, s)

    def body(xq, xk, xv, z, ba, wq, wk, wv, alog_r, dtb_r, nw_r, o_ref, u_o, w_o, in_o, qd_o, kd_o, gc_o, st_o, oraw_o,
             padq, padk, padv, q_s, k_s, v_s, b_s, g_s, kw_s, ku_s):
        h = pl.program_id(0)
        loc = [r.at[0] for r in (u_o, w_o, in_o, qd_o, kd_o)]
        gc_s, states, o_s = gc_o.at[0], st_o.at[0], oraw_o.at[0]
        _pad_front(padq, xq, s)
        _pad_front(padk, xk, s)
        _pad_front(padv, xv, s)
        for r0 in range(0, s, tt):
            rows = pl.ds(r0, tt)
            aq, ak, av = _conv_tile(padq, wq[...], r0, tt), _conv_tile(padk, wk[...], r0, tt), _conv_tile(padv, wv[...], r0, tt)
            q_s[rows, :], k_s[rows, :], v_s[rows, :], b_s[rows, :], g_s[rows, :] = _dn_pre(
                aq, ak, av, ba[rows, :], alog_r[...], dtb_r[...], h)
        _dn_forward_scan(q_s, k_s, v_s, b_s, g_s, gc_s, loc, o_s, states, kw_s, ku_s, s)
        for r0 in range(0, s, tt):
            rows = pl.ds(r0, tt)
            o_ref[rows, :] = _dn_post(o_s[rows, :], z[rows, :], nw_r[...]).astype(o_ref.dtype)

    big = pltpu.VMEM((s, d), F32)
    thin = pltpu.VMEM((s, 1), F32)
    padded = pltpu.VMEM((s + _PAD, d), F32)
    return _call_with_exchange(
        body, "dn_fwd", DN_HEADS, _dn_in_specs(s), [pl.BlockSpec((s, d), lambda h: (0, h))] + _dn_saved_specs(s),
        [jax.ShapeDtypeStruct((s, DN_HEADS * d), BF16)] + _dn_saved_shapes(s),
        [padded, padded, padded, big, big, big, thin, thin,
         pltpu.VMEM((s // CHUNK, d, d), BF16), pltpu.VMEM((s // CHUNK, d, d), F32)],
        (proj, proj, proj, proj, proj, conv_w, conv_w, conv_w, alog, dtb, nw), exchange)


def _dn_bwd(proj, conv_w, alog, dtb, nw, dob, kept, exchange=None):
    s = proj.shape[0]
    d = DN_HEAD_DIM
    nchunk = s // CHUNK

    tt = min(_TOK_TILE, s)

    def body(xq, xk, xv, z, ba, wq, wk, wv, alog_r, dtb_r, nw_r, dob_ref, u_i, w_i, in_i, qd_i, kd_i, gc_i, st_i, oraw_i,
             dxq, dxk, dxv, dz, dba, dwq, dwk, dwv, dalog, ddtb, dnw,
             padq, padk, padv, q_s, k_s, v_s, b_s, g_s, o_s, dq_s, dk_s, dv_s, db_s, dg_s, dkd_s, din_s, dgc_s):
        h = pl.program_id(0)
        u_s, w_s, in_s, qd_s, kd_s = [r.at[0] for r in (u_i, w_i, in_i, qd_i, kd_i)]
        gc_s, states, oraw = gc_i.at[0], st_i.at[0], oraw_i.at[0]
        masks = _chunk_masks()
        causal_f = masks[0].astype(BF16)
        pre = functools.partial(_dn_pre, h=h)
        _pad_front(padq, xq, s)
        _pad_front(padk, xk, s)
        _pad_front(padv, xv, s)

        def conv_tiles(r0):
            return _conv_tile(padq, wq[...], r0, tt), _conv_tile(padk, wk[...], r0, tt), _conv_tile(padv, wv[...], r0, tt)

        for r0 in range(0, s, tt):
            rows = pl.ds(r0, tt)
            q_s[rows, :], k_s[rows, :], v_s[rows, :], b_s[rows, :], g_s[rows, :] = pre(
                *conv_tiles(r0), ba[rows, :], alog_r[...], dtb_r[...])
        dnw_v = jnp.zeros((1, LANES), F32)
        for r0 in range(0, s, tt):
            rows = pl.ds(r0, tt)
            _, post_vjp = jax.vjp(_dn_post, oraw[rows, :], z[rows, :], nw_r[...])
            do_raw, dz_v, dnw_t = post_vjp(dob_ref[rows, :].astype(F32))
            dz[rows, :] = dz_v.astype(dz.dtype)
            o_s[rows, :] = do_raw
            dnw_v = dnw_v + dnw_t

        def state_step(i, dstate):
            c = nchunk - 1 - i
            rows = _chunk_rows(c)
            du, dw, din, dqd, dkd, dgc, dstate = _dn_state_bwd(
                u_s[rows, :], w_s[rows, :], in_s[rows, :], qd_s[rows, :], kd_s[rows, :], gc_s[rows, :], states[c],
                o_s[rows, :], dstate)
            dq_s[rows, :] = du
            dk_s[rows, :] = dw
            dv_s[rows, :] = dqd
            dkd_s[rows, :] = dkd
            din_s[rows, :] = din
            dgc_s[rows, :] = dgc
            return dstate

        lax.fori_loop(0, nchunk, state_step, jnp.zeros((d, d), F32))
        local = functools.partial(_dn_local, masks=masks)
        grp = _group(nchunk, 4)

        def local_step(i, carry):
            rows = [_chunk_rows(i * grp + j) for j in range(grp)]
            get = lambda ref: [ref[r, :] for r in rows]
            _, vjp = jax.vjp(local, get(q_s), get(k_s), get(v_s), get(b_s), get(gc_s))
            dq_c, dk_c, dv_c, db_c, dgc_c = vjp((get(dq_s), get(dk_s), get(din_s), get(dv_s), get(dkd_s)))
            dgc_c = [dgc_c[j] + dgc_s[r, :] for j, r in enumerate(rows)]
            dg_c = [_mask_dot(causal_f, jnp.broadcast_to(t, (CHUNK, LANES)), _TN)[:, 0:1] for t in dgc_c]
            for j, r in enumerate(rows):
                dq_s[r, :] = dq_c[j]
                dk_s[r, :] = dk_c[j]
                dv_s[r, :] = dv_c[j]
                db_s[r, :] = db_c[j]
                dg_s[r, :] = dg_c[j]
            return carry

        lax.fori_loop(0, nchunk // grp, local_step, 0)

        @pl.when(h == 0)
        def _():
            dba[...] = jnp.zeros_like(dba)
            dalog[...] = jnp.zeros_like(dalog)
            ddtb[...] = jnp.zeros_like(ddtb)
            dnw[...] = jnp.zeros_like(dnw)

        dalog_v = jnp.zeros((1, LANES), F32)
        ddtb_v = jnp.zeros((1, LANES), F32)
        for r0 in range(0, s, tt):
            rows = pl.ds(r0, tt)
            _, pre_vjp = jax.vjp(pre, *conv_tiles(r0), ba[rows, :], alog_r[...], dtb_r[...])
            daq, dak, dav, dba_t, dalog_t, ddtb_t = pre_vjp(
                (dq_s[rows, :], dk_s[rows, :], dv_s[rows, :], db_s[rows, :], dg_s[rows, :]))
            dq_s[rows, :], dk_s[rows, :], dv_s[rows, :] = daq, dak, dav
            dba[rows, :] += dba_t
            dalog_v = dalog_v + dalog_t
            ddtb_v = ddtb_v + ddtb_t
        tail = pl.ds(s, _PAD)
        dq_s[tail, :] = dk_s[tail, :] = dv_s[tail, :] = jnp.zeros((_PAD, d), F32)
        for pad, da_s, w_ref, dx_ref, dw_ref in ((padq, dq_s, wq, dxq, dwq), (padk, dk_s, wk, dxk, dwk), (padv, dv_s, wv, dxv, dwv)):
            dw_acc = jnp.zeros((CONV_K, d), F32)
            for r0 in range(0, s, tt):
                dx_t, dw_t = _conv_tile_bwd(pad, da_s, w_ref[...], r0, tt)
                dx_ref[pl.ds(r0, tt), :] = dx_t.astype(dx_ref.dtype)
                dw_acc = dw_acc + dw_t
            dw_ref[...] = dw_acc
        dalog[...] += dalog_v
        ddtb[...] += ddtb_v
        dnw[...] += dnw_v

    big = pltpu.VMEM((s, d), F32)
    thin = pltpu.VMEM((s, 1), F32)
    padded = pltpu.VMEM((s + _PAD, d), F32)
    w_all = DN_HEADS * d
    col_out = lambda: pl.BlockSpec((s, d), lambda h: (0, h))
    cw_out = lambda: pl.BlockSpec((CONV_K, d), lambda h: (0, h))
    row = lambda: pl.BlockSpec((1, LANES), lambda h: (0, 0))
    big_out = jax.ShapeDtypeStruct((s, w_all), BF16)
    cw_shape = jax.ShapeDtypeStruct((CONV_K, w_all), F32)
    row_shape = jax.ShapeDtypeStruct((1, LANES), F32)
    return _call_with_exchange(
        body, "dn_bwd", DN_HEADS,
        _dn_in_specs(s) + [pl.BlockSpec((s, d), lambda h: (0, h))] + _dn_saved_specs(s, pipeline_mode=pl.Buffered(1)),
        [col_out(), col_out(), col_out(), col_out(), pl.BlockSpec((s, LANES), lambda h: (0, 0)),
         cw_out(), cw_out(), cw_out(), row(), row(), row()],
        [big_out, big_out, big_out, big_out, jax.ShapeDtypeStruct((s, LANES), F32),
         cw_shape, cw_shape, cw_shape, row_shape, row_shape, row_shape],
        [padded, padded, padded, big, big, big, thin, thin, big,
         padded, padded, padded, thin, thin, big, pltpu.VMEM((s, CHUNK), F32), thin],
        (proj, proj, proj, proj, proj, conv_w, conv_w, conv_w, alog, dtb, nw, dob, *kept), exchange)


def _loss_head(y, target, tile=256):
    n, dm = y.shape
    tile = min(tile, n)

    def body(y_ref, t_ref, dy_ref, loss_ref):
        err = y_ref[...] - t_ref[...]
        dy_ref[...] = err * (1.0 / dm)

        @pl.when(pl.program_id(0) == 0)
        def _():
            loss_ref[...] = jnp.zeros_like(loss_ref)

        loss_ref[...] += 0.5 * jnp.sum(jnp.mean(jnp.square(err), axis=-1, keepdims=True), axis=0, keepdims=True)

    blk = pl.BlockSpec((tile, dm), lambda i: (i, 0))
    return _pc(body, name="loss_head", grid=(n // tile,), in_specs=[blk, blk],
               out_specs=[blk, pl.BlockSpec((1, 1), lambda i: (0, 0))],
               out_shape=[jax.ShapeDtypeStruct((n, dm), F32), jax.ShapeDtypeStruct((1, 1), F32)],
               compiler_params=_cparams(("arbitrary",)))(y, target)


def _ada_fwd(c_all, w_ada, b_shard):
    nl, dm, n = w_ada.shape

    def body(c_ref, w_ref, b_ref, o_ref):
        ca = _silu(c_ref[...]).astype(BF16)
        o_ref[0] = jnp.dot(ca, w_ref[0].astype(BF16), preferred_element_type=F32) + b_ref[0]

    return _pc(body, name="ada_fwd", grid=(nl,),
               in_specs=[pl.BlockSpec((N_DEV, dm), lambda l: (0, 0)), pl.BlockSpec((1, dm, n), lambda l: (l, 0, 0)),
                         pl.BlockSpec((1, 1, n), lambda l: (l, 0, 0))],
               out_specs=pl.BlockSpec((1, N_DEV, n), lambda l: (l, 0, 0)),
               out_shape=jax.ShapeDtypeStruct((nl, N_DEV, n), F32), compiler_params=_cparams())(c_all, w_ada, b_shard)


def _ada_bwd(c_all, dmod):
    nl, _, n = dmod.shape
    dm = c_all.shape[1]

    def body(c_ref, d_ref, o_ref):
        o_ref[0] = _hdot(_silu(c_ref[...]), d_ref[0], _TN)

    return _pc(body, name="ada_bwd", grid=(nl,),
               in_specs=[pl.BlockSpec((N_DEV, dm), lambda l: (0, 0)), pl.BlockSpec((1, N_DEV, n), lambda l: (l, 0, 0))],
               out_specs=pl.BlockSpec((1, dm, n), lambda l: (l, 0, 0)),
               out_shape=jax.ShapeDtypeStruct((nl, dm, n), F32), compiler_params=_cparams())(c_all, dmod)


def _adamw(g, w, m, v):
    m = ADAM_B1 * m + (1.0 - ADAM_B1) * g
    v = ADAM_B2 * v + (1.0 - ADAM_B2) * jnp.square(g)
    m_hat = m / (1.0 - ADAM_B1 ** ADAM_STEP)
    v_hat = v / (1.0 - ADAM_B2 ** ADAM_STEP)
    delta = -ADAM_LR * (m_hat / (jnp.sqrt(v_hat) + ADAM_EPS) + ADAM_WD * w)
    return delta, m, v


def _adam_call(parts, w, m, v, name, tile=128):
    shape = w.shape
    flat = lambda t: t.reshape(-1, shape[-1])
    width = shape[-1]

    def fn(*vals):
        g = vals[0] if len(parts) == 1 else vals[0] + vals[1]
        return (g,) + _adamw(g, *vals[len(parts):])

    rows = [_whole(flat(t)) for t in (*parts, w, m, v)]
    outs = _rowwise(fn, rows, [], [(width, F32)] * 4, name, tile=tile)
    return [o.reshape(shape) for o in outs]


def _sum_slots(per_layer, name, tile=128):
    nl = len(per_layer)
    _, n, width = per_layer[0].shape
    tile = min(tile, n)
    nt = n // tile

    def body(*refs):
        o_ref = refs[nl]
        for lp in range(nl):
            @pl.when(pl.program_id(0) == lp)
            def _(r_ref=refs[lp]):
                acc = r_ref[0].astype(F32)
                for j in range(1, N_CHIPS):
                    acc = acc + r_ref[j].astype(F32)
                o_ref[...] = acc

    in_specs = [pl.BlockSpec((N_CHIPS, tile, width), functools.partial(lambda l, t, lp: (0, jnp.where(l == lp, t, 0), 0), lp=lp))
                for lp in range(nl)]
    return _pc(body, name=name, grid=(nl, nt), in_specs=in_specs,
               out_specs=pl.BlockSpec((tile, width), lambda l, t: (l * nt + t, 0)),
               out_shape=jax.ShapeDtypeStruct((nl * n, width), F32), compiler_params=_cparams())(*per_layer)


def _small_adam(g_all, w, m, v):
    def body(g_ref, w_ref, m_ref, v_ref, og, od, om, ov):
        g = g_ref[0]
        for j in range(1, N_DEV):
            g = g + g_ref[j]
        og[...] = g
        od[...], om[...], ov[...] = _adamw(g, w_ref[...], m_ref[...], v_ref[...])

    vm = pl.BlockSpec(memory_space=pltpu.VMEM)
    shp = jax.ShapeDtypeStruct(w.shape, F32)
    return _pc(body, name="small_adam", in_specs=[vm] * 4, out_specs=[vm] * 4, out_shape=[shp] * 4,
               compiler_params=_cparams())(g_all, w, m, v)


def _place():
    return lax.axis_index("x"), lax.axis_index("y"), lax.axis_index("c")


def _flip(v, bit):
    return 1 - v if bit else v


def _all_gather8(a):
    r, n = a.shape

    def body(a_ref, o_ref, send_sems, recv_sems):
        x, y, c = _place()
        me = 4 * x + 2 * y + c
        o_ref[me] = a_ref[...]
        copies = []
        for k in range(1, N_DEV):
            peer = (_flip(x, k & 4), _flip(y, k & 2), _flip(c, k & 1))
            copies.append(pltpu.make_async_remote_copy(
                src_ref=a_ref, dst_ref=o_ref.at[me], send_sem=send_sems.at[k - 1], recv_sem=recv_sems.at[k - 1],
                device_id=peer, device_id_type=MESH))
        for cp in copies:
            cp.start()
        for k in range(1, N_DEV):
            px, py, pc_ = _flip(x, k & 4), _flip(y, k & 2), _flip(c, k & 1)
            pltpu.make_async_remote_copy(
                src_ref=a_ref, dst_ref=o_ref.at[4 * px + 2 * py + pc_], send_sem=send_sems.at[k - 1],
                recv_sem=recv_sems.at[k - 1], device_id=(px, py, pc_), device_id_type=MESH).wait_recv()
        for cp in copies:
            cp.wait_send()

    vm = pl.BlockSpec(memory_space=pltpu.VMEM)
    return _pc(body, name="all_gather8", in_specs=[vm], out_specs=vm,
               out_shape=jax.ShapeDtypeStruct((N_DEV, r, n), a.dtype),
               scratch_shapes=[pltpu.SemaphoreType.DMA((N_DEV - 1,)), pltpu.SemaphoreType.DMA((N_DEV - 1,))],
               compiler_params=_cparams())(a)


def _chip_exchange(arrays, gather, name):
    na = len(arrays)

    def body(*refs):
        ins, outs, sems = refs[:na], refs[na:2 * na], refs[2 * na:]
        _exchange_copies(ins, outs, sems, gather, start=True)
        _exchange_copies(ins, outs, sems, gather, start=False)

    in_specs, out_specs, out_shape, scratch = _exchange_specs(arrays, gather)
    return _pc(body, name=name, in_specs=in_specs, out_specs=out_specs, out_shape=out_shape, scratch_shapes=scratch,
               compiler_params=_cparams())(*arrays)


def _exchange_specs(arrays, gather):
    na = len(arrays)
    hbm = pl.BlockSpec(memory_space=pl.ANY)
    out_shape = [jax.ShapeDtypeStruct(((N_CHIPS,) + a.shape) if gather else a.shape, a.dtype) for a in arrays]
    n_remote = 4 if gather else 2
    scratch = [pltpu.SemaphoreType.DMA((3 * na,))] * n_remote + [pltpu.SemaphoreType.DMA((na,))]
    return [hbm] * na, [hbm] * na, out_shape, scratch


def _gather_copies(ins, outs, sems, start):
    send_i, recv_i, send_d, recv_d, local_sems = sems
    x, y, c = _place()
    me = 2 * x + y
    sibling = (x, y, 1 - c)
    ici_sends, ici_arrivals, hand_ons, hand_arrivals, locals_ = [], [], [], [], []
    for i in range(len(ins)):
        half = ins[i].shape[0] // 2
        mine, other = pl.ds(c * half, half), pl.ds((1 - c) * half, half)
        locals_.append(pltpu.make_async_copy(ins[i], outs[i].at[me], local_sems.at[i]))
        for j in range(1, N_CHIPS):
            px, py = _flip(x, j & 2), _flip(y, j & 1)
            peer = 2 * px + py
            k = i * 3 + j - 1
            ici = dict(send_sem=send_i.at[k], recv_sem=recv_i.at[k], device_id=(px, py, c), device_id_type=MESH)
            d2d = dict(send_sem=send_d.at[k], recv_sem=recv_d.at[k], device_id=sibling, device_id_type=MESH)
            ici_sends.append(pltpu.make_async_remote_copy(src_ref=ins[i].at[mine], dst_ref=outs[i].at[me, mine], **ici))
            ici_arrivals.append(pltpu.make_async_remote_copy(src_ref=ins[i].at[mine], dst_ref=outs[i].at[peer, mine], **ici))
            hand_ons.append(pltpu.make_async_remote_copy(
                src_ref=outs[i].at[peer, mine], dst_ref=outs[i].at[peer, mine], **d2d))
            hand_arrivals.append(pltpu.make_async_remote_copy(
                src_ref=outs[i].at[peer, other], dst_ref=outs[i].at[peer, other], **d2d))
    if start:
        for cp in locals_ + ici_sends:
            cp.start()
    else:
        for arrival, hand_on in zip(ici_arrivals, hand_ons):
            arrival.wait_recv()
            hand_on.start()
        for cp in hand_arrivals:
            cp.wait_recv()
        for cp in ici_sends + hand_ons:
            cp.wait_send()
        for cp in locals_:
            cp.wait()


def _exchange_copies(ins, outs, sems, gather, start):
    if gather:
        return _gather_copies(ins, outs, sems, start)
    send_sems, recv_sems, local_sems = sems
    x, y, c = _place()
    me = 2 * x + y
    sends, arrivals, locals_ = [], [], []
    for i in range(len(ins)):
        locals_.append(pltpu.make_async_copy(ins[i] if gather else ins[i].at[me], outs[i].at[me], local_sems.at[i]))
        for j in range(1, N_CHIPS):
            px, py = _flip(x, j & 2), _flip(y, j & 1)
            peer = 2 * px + py
            pair = dict(send_sem=send_sems.at[i * 3 + j - 1], recv_sem=recv_sems.at[i * 3 + j - 1],
                        device_id=(px, py, c), device_id_type=MESH)
            sends.append(pltpu.make_async_remote_copy(
                src_ref=ins[i] if gather else ins[i].at[peer], dst_ref=outs[i].at[me], **pair))
            arrivals.append(pltpu.make_async_remote_copy(
                src_ref=ins[i] if gather else ins[i].at[me], dst_ref=outs[i].at[peer], **pair))
    if start:
        for cp in locals_ + sends:
            cp.start()
    else:
        for cp in arrivals:
            cp.wait_recv()
        for cp in sends:
            cp.wait_send()
        for cp in locals_:
            cp.wait()


def _carry_exchange(body, n_in, n_out, n_scratch, n_arrays, gather, steps):
    def wrapped(*refs):
        na = n_arrays
        ins, xin = refs[:n_in], refs[n_in:n_in + na]
        outs = refs[n_in + na:n_in + na + n_out]
        xout = refs[n_in + na + n_out:n_in + 2 * na + n_out]
        rest = refs[n_in + 2 * na + n_out:]
        scratch, sems = rest[:n_scratch], rest[n_scratch:]
        first, last = True, True
        for axis, n in enumerate(steps):
            first = jnp.logical_and(first, pl.program_id(axis) == 0)
            last = jnp.logical_and(last, pl.program_id(axis) == n - 1)

        @pl.when(first)
        def _():
            _exchange_copies(xin, xout, sems, gather, start=True)

        body(*ins, *outs, *scratch)

        @pl.when(last)
        def _():
            _exchange_copies(xin, xout, sems, gather, start=False)

    return wrapped


def _sibling_exchange(arrays, name):
    na = len(arrays)

    def body(*refs):
        ins, outs = refs[:na], refs[na:2 * na]
        send_sems, recv_sems = refs[2 * na:]
        x, y, c = _place()
        copies = [pltpu.make_async_remote_copy(
            src_ref=ins[i], dst_ref=outs[i], send_sem=send_sems.at[i], recv_sem=recv_sems.at[i],
            device_id=(x, y, 1 - c), device_id_type=MESH) for i in range(na)]
        for cp in copies:
            cp.start()
        for cp in copies:
            cp.wait()

    hbm = pl.BlockSpec(memory_space=pl.ANY)
    return _pc(body, name=name, in_specs=[hbm] * na, out_specs=[hbm] * na,
               out_shape=[jax.ShapeDtypeStruct(a.shape, a.dtype) for a in arrays],
               scratch_shapes=[pltpu.SemaphoreType.DMA((na,)), pltpu.SemaphoreType.DMA((na,))],
               compiler_params=_cparams())(*arrays)


def _heads_q(t):
    s = t.shape[0]
    return t.reshape(s, ATT_KV_HEADS, ATT_GROUP, ATT_HEAD_DIM).transpose(1, 2, 0, 3)


def _unheads_q(t):
    s = t.shape[2]
    return t.transpose(2, 0, 1, 3).reshape(s, ATT_KV_HEADS * ATT_GROUP * ATT_HEAD_DIM)


def _heads_kv(t):
    s = t.shape[0]
    return t.reshape(s, ATT_KV_HEADS, ATT_HEAD_DIM).transpose(1, 0, 2)


def _unheads_kv(t):
    s = t.shape[1]
    return t.transpose(1, 0, 2).reshape(s, ATT_KV_HEADS * ATT_HEAD_DIM)


def _row128(v):
    return jnp.pad(v, (0, LANES - v.shape[0])).reshape(1, LANES)


def _layer_fwd(x, p, exchange=None):
    sh1, sc1, gt1, sh2, sc2, gt2 = [p["mod"][i] for i in range(6)]
    (u,) = _rowwise(_f_mod, [_whole(x)], [sc1, sh1], [(D_MODEL, BF16)], "mod1")
    proj = _mm(u, p["w_in"], "nn", F32, "proj")
    qh = _heads_q(proj[:, C_Q:C_Q + 1024])
    kh = _heads_kv(proj[:, C_K:C_K + 256])
    vh = _heads_kv(proj[:, C_V:C_V + 256])
    sinks4 = p["sinks"].reshape(ATT_KV_HEADS, ATT_GROUP, 1, 1)
    behind_attn = None if exchange is None else (exchange[0][-1:], exchange[1])
    behind_dn = None if exchange is None else (exchange[0][:-1], exchange[1])
    o_heads, got_attn = _attn_fwd(qh, kh, vh, sinks4, behind_attn)
    o_a = _unheads_q(o_heads)
    (o_b, *dn_kept), got_dn = _dn_fwd(proj, p["conv_w"], _row128(p["a_log"]), _row128(p["dt_bias"]),
                                      p["dn_norm_w"].reshape(1, LANES), behind_dn)
    exchanged = None if exchange is None else list(got_dn) + list(got_attn)
    y_a = _mm(o_a, p["w_oa"], "nn", BF16, "y_a")
    y_b = _mm(o_b, p["w_ob"], "nn", BF16, "y_b")
    (gm,) = _rowwise(_f_gate, [(proj, C_GA // 1024, 1024), (proj, C_GB // 1024, 1024), _whole(y_a), _whole(y_b)], [],
                     [(D_MODEL, BF16)], "gate")
    mixed = _mm(gm, p["w_out"], "nn", BF16, "mixed")
    x1, u2 = _rowwise(_f_post1, [_whole(x), _whole(mixed)], [gt1, p["ln1_g"], p["ln1_b"], sc2, sh2],
                      [(D_MODEL, F32), (D_MODEL, BF16)], "post1")
    hpre = _mm(u2, p["w_ff1"], "nn", BF16, "ff1")
    (h,) = _rowwise(_f_act, [_whole(hpre)], [p["b_ff1"]], [(D_FF, BF16)], "act")
    ff = _mm(h, p["w_ff2"], "nn", BF16, "ff2")
    (x2,) = _rowwise(_f_post2, [_whole(x1), _whole(ff)], [gt2, p["b_ff2"], p["ln2_g"], p["ln2_b"]],
                     [(D_MODEL, F32)], "post2")
    saved = dict(x=x, u=u, proj=proj, o_a=o_a, o_b=o_b, y_a=y_a, y_b=y_b, gm=gm, mixed=mixed, x1=x1, u2=u2,
                 hpre=hpre, h=h, ff=ff, dn_kept=dn_kept, heads=(qh, kh, vh))
    return x2, saved, exchanged


def _layer_bwd(dx2, p, sv, carry=None):
    sh1, sc1, gt1, sh2, sc2, gt2 = [p["mod"][i] for i in range(6)]
    g = {}
    (dx1_a, dff), (dgt2, g["b_ff2"], g["ln2_g"], g["ln2_b"]) = _rowwise_bwd(
        _f_post2, [_whole(sv["x1"]), _whole(sv["ff"])], [gt2, p["b_ff2"], p["ln2_g"], p["ln2_b"]], [dx2],
        [F32, BF16], "post2_bwd")
    dh = _mm(dff, p["w_ff2"], "nt", BF16,"dh")
    g["w_ff2"] = _mm(sv["h"], dff, "tn", BF16,"dw_ff2")
    (dhpre,), (g["b_ff1"],) = _rowwise_bwd(_f_act, [_whole(sv["hpre"])], [p["b_ff1"]], [dh], [BF16], "act_bwd")
    du2 = _mm(dhpre, p["w_ff1"], "nt", BF16,"du2")
    g["w_ff1"] = _mm(sv["u2"], dhpre, "tn", BF16,"dw_ff1")
    (dx_a, dmixed), (dgt1, g["ln1_g"], g["ln1_b"], dsc2, dsh2) = _rowwise_bwd(
        _f_post1, [_whole(sv["x"]), _whole(sv["mixed"])], [gt1, p["ln1_g"], p["ln1_b"], sc2, sh2], [dx1_a, du2],
        [F32, BF16], "post1_bwd")
    dgm = _mm(dmixed, p["w_out"], "nt", BF16,"dgm")
    g["w_out"] = _mm(sv["gm"], dmixed, "tn", BF16,"dw_out")
    proj = sv["proj"]
    (dga, dgb, dya, dyb), _ = _rowwise_bwd(
        _f_gate, [(proj, C_GA // 1024, 1024), (proj, C_GB // 1024, 1024), _whole(sv["y_a"]), _whole(sv["y_b"])], [],
        [dgm], [BF16, BF16, BF16, BF16], "gate_bwd")
    do_a = _mm(dya, p["w_oa"], "nt", BF16,"do_a")
    g["w_oa"] = _mm(sv["o_a"], dya, "tn", BF16,"dw_oa")
    do_b = _mm(dyb, p["w_ob"], "nt", BF16,"do_b")
    g["w_ob"] = _mm(sv["o_b"], dyb, "tn", BF16,"dw_ob")
    exchange = None if carry is None else (carry(g), False)
    (ddq, ddk, ddv, ddz, dba, dwq, dwk, dwv, dalog, ddtb, dnw), exchanged = _dn_bwd(
        proj, p["conv_w"], _row128(p["a_log"]), _row128(p["dt_bias"]), p["dn_norm_w"].reshape(1, LANES), do_b,
        sv["dn_kept"], exchange)
    g["conv_w"] = jnp.concatenate([dwq, dwk, dwv], axis=1)
    g["a_log"], g["dt_bias"], g["dn_norm_w"] = dalog[0, :DN_HEADS], ddtb[0, :DN_HEADS], dnw[0]
    qh, kh, vh = sv["heads"]
    sinks4 = p["sinks"].reshape(ATT_KV_HEADS, ATT_GROUP, 1, 1)
    dqh, dkh, dvh, dsk = _attn_bwd(qh, kh, vh, sinks4, _heads_q(do_a))
    g["sinks"] = dsk.reshape(ATT_KV_HEADS * ATT_GROUP)
    s = proj.shape[0]
    dproj = jnp.concatenate([
        _unheads_q(dqh).astype(BF16), ddq, ddk, ddv, ddz, dga, dgb,
        _unheads_kv(dkh[:, WINDOW:, :]).astype(BF16), _unheads_kv(dvh[:, WINDOW:, :]).astype(BF16),
        dba.astype(BF16), jnp.zeros((s, D_IN_P - C_BA - LANES), BF16)], axis=1)
    du = _mm(dproj, p["w_in"], "nt", BF16,"du")
    g["w_in"] = _mm(sv["u"], dproj, "tn", BF16,"dw_in")
    (dx,), (dsc1, dsh1) = _rowwise_bwd(_f_mod, [_whole(sv["x"])], [sc1, sh1], [du], [F32], "mod1_bwd", add=(0, dx_a))
    g["mod"] = jnp.stack([dsh1, dsc1, dgt1, dsh2, dsc2, dgt2])
    return dx, g, exchanged


def _permute_w_in(w):
    pad = jnp.zeros(w.shape[:-1] + (D_IN_P - D_IN,), w.dtype)
    return jnp.concatenate([w[..., 0:1024], w[..., 1536:5632], w[..., 5648:7696], w[..., 1024:1536],
                            w[..., 5632:5648], pad], axis=-1)


def _unpermute_w_in(g):
    return jnp.concatenate([g[..., 0:1024], g[..., C_K:C_K + 512], g[..., 1024:5120], g[..., C_BA:C_BA + 16],
                            g[..., 5120:7168]], axis=-1)


def _cols_from_chips(t):
    c, l, r, n = t.shape
    return t.transpose(1, 2, 0, 3).reshape(l, r, c * n)


def _cols_to_chips(t):
    l, r, n4 = t.shape
    return t.reshape(l, r, N_CHIPS, n4 // N_CHIPS).transpose(2, 0, 1, 3)


def _rows_from_chips(t):
    c, l, r, n = t.shape
    return t.transpose(1, 0, 2, 3).reshape(l, c * r, n)


def _rows_to_chips(t):
    l, r4, n = t.shape
    return t.reshape(l, N_CHIPS, r4 // N_CHIPS, n).transpose(1, 0, 2, 3)


_REPLICATED = ("b_ada", "a_log", "dt_bias", "sinks", "dn_norm_w", "ln1_g", "ln1_b", "b_ff1", "b_ff2", "ln2_g", "ln2_b")
_SMALL = _REPLICATED + ("conv_w",)
_PACK_W = 1024
_WEIGHT_ORDER = ("w_ada", "b_ada", "w_in", "conv_w", "a_log", "dt_bias", "sinks", "dn_norm_w", "w_oa", "w_ob", "w_out",
                 "ln1_g", "ln1_b", "w_ff1", "b_ff1", "w_ff2", "b_ff2", "ln2_g", "ln2_b")


def _pack_small(d):
    flat = jnp.concatenate([d[k].reshape(-1) for k in _SMALL])
    rows = -(-flat.shape[0] // (_PACK_W * 8)) * 8
    return jnp.pad(flat, (0, rows * _PACK_W - flat.shape[0])).reshape(rows, _PACK_W)


def _unpack_small(packed, shapes):
    flat = packed.reshape(-1)
    out, off = {}, 0
    for k in _SMALL:
        n = 1
        for d_ in shapes[k]:
            n *= d_
        out[k] = flat[off:off + n].reshape(shapes[k])
        off += n
    return out


def kernel(x, c, w_ada, b_ada, w_in, conv_w, a_log, dt_bias, sinks, dn_norm_w, w_oa, w_ob, w_out, ln1_g, ln1_b, w_ff1, b_ff1, w_ff2, b_ff2, ln2_g, ln2_b, loss_target, m_w_ada, m_b_ada, m_w_in, m_conv_w, m_a_log, m_dt_bias, m_sinks, m_dn_norm_w, m_w_oa, m_w_ob, m_w_out, m_ln1_g, m_ln1_b, m_w_ff1, m_b_ff1, m_w_ff2, m_b_ff2, m_ln2_g, m_ln2_b, v_w_ada, v_b_ada, v_w_in, v_conv_w, v_a_log, v_dt_bias, v_sinks, v_dn_norm_w, v_w_oa, v_w_ob, v_w_out, v_ln1_g, v_ln1_b, v_w_ff1, v_b_ff1, v_w_ff2, v_b_ff2, v_ln2_g, v_ln2_b):
    ix, iy, ic = _place()
    chip = 2 * ix + iy
    dev = 4 * ix + 2 * iy + ic
    weights = dict(w_ada=w_ada, b_ada=b_ada, w_in=w_in, conv_w=conv_w, a_log=a_log, dt_bias=dt_bias, sinks=sinks,
                   dn_norm_w=dn_norm_w, w_oa=w_oa, w_ob=w_ob, w_out=w_out, ln1_g=ln1_g, ln1_b=ln1_b, w_ff1=w_ff1,
                   b_ff1=b_ff1, w_ff2=w_ff2, b_ff2=b_ff2, ln2_g=ln2_g, ln2_b=ln2_b)
    mom_m = dict(w_ada=m_w_ada, b_ada=m_b_ada, w_in=m_w_in, conv_w=m_conv_w, a_log=m_a_log, dt_bias=m_dt_bias,
                 sinks=m_sinks, dn_norm_w=m_dn_norm_w, w_oa=m_w_oa, w_ob=m_w_ob, w_out=m_w_out, ln1_g=m_ln1_g,
                 ln1_b=m_ln1_b, w_ff1=m_w_ff1, b_ff1=m_b_ff1, w_ff2=m_w_ff2, b_ff2=m_b_ff2, ln2_g=m_ln2_g, ln2_b=m_ln2_b)
    mom_v = dict(w_ada=v_w_ada, b_ada=v_b_ada, w_in=v_w_in, conv_w=v_conv_w, a_log=v_a_log, dt_bias=v_dt_bias,
                 sinks=v_sinks, dn_norm_w=v_dn_norm_w, w_oa=v_w_oa, w_ob=v_w_ob, w_out=v_w_out, ln1_g=v_ln1_g,
                 ln1_b=v_ln1_b, w_ff1=v_w_ff1, b_ff1=v_b_ff1, w_ff2=v_w_ff2, b_ff2=v_b_ff2, ln2_g=v_ln2_g, ln2_b=v_ln2_b)

    n_ada = w_ada.shape[2]
    c_all = _all_gather8(jnp.pad(c, ((0, 7), (0, 0))))[:, 0, :]
    b_shard = lax.dynamic_slice_in_dim(b_ada, chip * n_ada, n_ada, axis=1).reshape(DEPTH, 1, n_ada)
    mod_t = _ada_fwd(c_all, w_ada, b_shard)
    mod_all = _all_gather8(mod_t.reshape(DEPTH * N_DEV, n_ada)).reshape(N_DEV, DEPTH, N_DEV, n_ada)
    mod_mine = lax.dynamic_index_in_dim(mod_all[0::2], dev, axis=2, keepdims=False)
    mod = mod_mine.transpose(1, 0, 2).reshape(DEPTH, 6, 1, D_MODEL)

    n_cw = conv_w.shape[2]
    cw_all = _all_gather8(conv_w.reshape(DEPTH * CONV_K, n_cw))[0::2]
    conv_full = cw_all.transpose(1, 0, 2).reshape(DEPTH, CONV_K, N_CHIPS * n_cw)

    big = ("w_in", "w_oa", "w_ob", "w_out", "w_ff1", "w_ff2")
    w16 = {k: weights[k].astype(BF16) for k in big}
    shards = lambda l: [w16[k][l] for k in big]

    def assemble(gathered):
        gw = {k: t[:, None] for k, t in zip(big, gathered)}
        full = dict(w_in=_permute_w_in(_cols_from_chips(gw["w_in"])), w_ff1=_cols_from_chips(gw["w_ff1"]),
                    w_oa=_rows_from_chips(gw["w_oa"]), w_ob=_rows_from_chips(gw["w_ob"]),
                    w_out=_rows_from_chips(gw["w_out"]), w_ff2=_rows_from_chips(gw["w_ff2"]))
        return {k: t[0] for k, t in full.items()}

    to_chips = dict(w_in=lambda t: _cols_to_chips(_unpermute_w_in(t)), w_ff1=_cols_to_chips, w_oa=_rows_to_chips,
                    w_ob=_rows_to_chips, w_out=_rows_to_chips, w_ff2=_rows_to_chips)

    def slices_for_chips(g, keys):
        return [to_chips[k](g[k][None])[:, 0].astype(BF16) for k in keys]

    full = [None] * DEPTH
    full[0] = assemble(_chip_exchange(shards(0), True, "gather_weights"))

    def layer_params(l):
        p = dict(full[l])
        p["mod"] = mod[l]
        p["conv_w"] = conv_full[l]
        for k in ("a_log", "dt_bias", "sinks", "dn_norm_w"):
            p[k] = weights[k][l]
        for k in ("ln1_g", "ln1_b", "b_ff1", "b_ff2", "ln2_g", "ln2_b"):
            p[k] = weights[k][l].reshape(1, -1)
        return p

    xs = x[0]
    saved = []
    for l in range(DEPTH):
        nxt = (shards(l + 1), True) if l + 1 < DEPTH else None
        xs, sv, gathered = _layer_fwd(xs, layer_params(l), nxt)
        if nxt is not None:
            full[l + 1] = assemble(gathered)
        saved.append(sv)
    dy, loss_local = _loss_head(xs, loss_target[0])
    loss = lax.psum(loss_local[0, 0], ("x", "y", "c"))
    early = tuple(k for k in big if k != "w_in")
    grads = [None] * DEPTH
    received = [dict() for _ in range(DEPTH)]
    dx = dy
    pending = []
    for l in reversed(range(DEPTH)):
        carry = functools.partial(lambda g, first: first + slices_for_chips(g, early), first=pending)
        dx, grads[l], got = _layer_bwd(dx, layer_params(l), saved[l], carry)
        if pending:
            received[l + 1]["w_in"] = got[0]
        received[l].update(zip(early, got[len(pending):]))
        pending = slices_for_chips(grads[l], ("w_in",))
    received[0]["w_in"] = _chip_exchange(pending, False, "scatter_grads")[0]
    grad_x = dx[None]
    gstack = {k: jnp.stack([grads[l][k] for l in range(DEPTH)]) for k in grads[0] if k not in big}

    dmod = gstack["mod"].reshape(DEPTH, 6 * D_MODEL)
    small_g = dict(b_ada=dmod, a_log=gstack["a_log"], dt_bias=gstack["dt_bias"], sinks=gstack["sinks"],
                   dn_norm_w=gstack["dn_norm_w"], ln1_g=gstack["ln1_g"], ln1_b=gstack["ln1_b"], b_ff1=gstack["b_ff1"],
                   b_ff2=gstack["b_ff2"], ln2_g=gstack["ln2_g"], ln2_b=gstack["ln2_b"], conv_w=gstack["conv_w"])
    shapes = {k: weights[k].shape for k in _REPLICATED}
    shapes["conv_w"] = small_g["conv_w"].shape
    g_all = _all_gather8(_pack_small(small_g))
    no_conv = jnp.zeros(shapes["conv_w"], F32)
    small_out = _small_adam(g_all, _pack_small(dict(weights, conv_w=no_conv)), _pack_small(dict(mom_m, conv_w=no_conv)),
                            _pack_small(dict(mom_v, conv_w=no_conv)))
    small_res = [_unpack_small(t, shapes) for t in small_out]
    g_conv = lax.dynamic_slice_in_dim(small_res[0]["conv_w"], chip * n_cw, n_cw, axis=2)
    res = {"conv_w": _adam_call([g_conv], conv_w, m_conv_w, v_conv_w, "adam_conv_w", tile=16)}

    dmod_all = g_all.reshape(N_DEV, -1)[:, :DEPTH * 6 * D_MODEL].reshape(N_DEV, DEPTH, 6 * D_MODEL)
    dmod_shard = lax.dynamic_slice_in_dim(dmod_all, chip * n_ada, n_ada, axis=2).transpose(1, 0, 2)
    g_w_ada = _ada_bwd(c_all, dmod_shard)
    res["w_ada"] = _adam_call([g_w_ada], w_ada, m_w_ada, v_w_ada, "adam_w_ada")

    partial = [_sum_slots([received[l][k] for l in range(DEPTH)], "sum_" + k) for k in big]
    theirs = _sibling_exchange(partial, "sibling_grads")
    for k, mine, other in zip(big, partial, theirs):
        shape = weights[k].shape
        res[k] = _adam_call([mine.reshape(shape), other.reshape(shape)], weights[k], mom_m[k], mom_v[k], "adam_" + k)
    for k in _REPLICATED:
        res[k] = [small_res[i][k] for i in range(4)]

    outs = [loss, grad_x]
    for i in range(4):
        outs += [res[k][i] for k in _WEIGHT_ORDER]
    return tuple(outs)
```

```python
import functools

import jax
import jax.numpy as jnp
from jax import lax
from jax.experimental import pallas as pl
from jax.experimental.pallas import tpu as pltpu

F32, BF16 = jnp.float32, jnp.bfloat16
HI = lax.Precision.HIGHEST
MESH = pl.DeviceIdType.MESH

D_MODEL = 1024
DEPTH = 4
ATT_KV_HEADS, ATT_GROUP, ATT_HEAD_DIM, WINDOW = 4, 4, 64, 128
DN_HEADS, DN_HEAD_DIM, CONV_K, CHUNK = 8, 128, 4, 64
D_FF = 4 * D_MODEL
D_IN = 7696
ALPHA = (2 * DEPTH) ** 0.25
LN_EPS = 1e-5
RMS_EPS = 1e-6
ADAM_LR, ADAM_B1, ADAM_B2, ADAM_EPS, ADAM_WD, ADAM_STEP = 0.001, 0.9, 0.999, 1e-08, 0.01, 10

N_CHIPS = 4
N_DEV = 8
LANES = 128
D_IN_P = 8192
C_Q, C_DQ, C_DK, C_DV, C_Z, C_GA, C_GB, C_K, C_V, C_BA = 0, 1024, 2048, 3072, 4096, 5120, 6144, 7168, 7424, 7680
NEG = -1e30
VMEM_LIMIT = 56 << 20


def _pc(body, **kw):
    return pl.pallas_call(body, **kw)


def _cparams(sem=None):
    if sem is None:
        return pltpu.CompilerParams(vmem_limit_bytes=VMEM_LIMIT)
    return pltpu.CompilerParams(vmem_limit_bytes=VMEM_LIMIT, dimension_semantics=sem)


_MM_VMEM_BUDGET = 44 << 20
_MM_MIN_TILE = 256


def _mm_tiles(m, n, k, out_bytes):
    def halvings(d):
        out = [d]
        while out[-1] % 2 == 0 and out[-1] // 2 >= _MM_MIN_TILE:
            out.append(out[-1] // 2)
        return out

    best = None
    for tm in halvings(m):
        for tn in halvings(n):
            if 2 * (2 * tm * k + 2 * tn * k + out_bytes * tm * tn) > _MM_VMEM_BUDGET:
                continue
            cost = (2 * m * k + (m // tm) * 2 * n * k, (m // tm) * (n // tn))
            if best is None or cost < best[0]:
                best = (cost, tm, tn)
    assert best is not None, (m, n, k)
    return best[1], best[2]


def _mm(a, b, mode, out_dtype, name, tm=None, tn=None):
    if mode == "nn":
        (m, k), (_, n) = a.shape, b.shape
        dims = (((1,), (0,)), ((), ()))
    elif mode == "nt":
        (m, k), (n, _) = a.shape, b.shape
        dims = (((1,), (1,)), ((), ()))
    else:
        (k, m), (_, n) = a.shape, b.shape
        dims = (((0,), (0,)), ((), ()))
    if tm is None:
        tm, tn = _mm_tiles(m, n, k, jnp.dtype(out_dtype).itemsize)
    tm, tn = min(tm, m), min(tn, n)
    assert m % tm == 0 and n % tn == 0, (name, m, n, tm, tn)
    a_spec = pl.BlockSpec((k, tm), lambda i, j: (0, i)) if mode == "tn" else pl.BlockSpec((tm, k), lambda i, j: (i, 0))
    b_spec = pl.BlockSpec((tn, k), lambda i, j: (j, 0)) if mode == "nt" else pl.BlockSpec((k, tn), lambda i, j: (0, j))

    def body(a_ref, b_ref, o_ref):
        o_ref[...] = lax.dot_general(a_ref[...], b_ref[...], dims, preferred_element_type=F32).astype(o_ref.dtype)

    return _pc(body, name=name, grid=(m // tm, n // tn), in_specs=[a_spec, b_spec],
               out_specs=pl.BlockSpec((tm, tn), lambda i, j: (i, j)),
               out_shape=jax.ShapeDtypeStruct((m, n), out_dtype), compiler_params=_cparams())(a, b)


def _row_specs(rows, tile):
    return [pl.BlockSpec((tile, w), functools.partial(lambda i, cb: (i, cb), cb=cb)) for (_, cb, w) in rows]


def _vec_specs(vecs):
    return [pl.BlockSpec(v.shape, lambda i: (0, 0)) for v in vecs]


def _rowwise(fn, rows, vecs, outs, name, tile=256):
    n = rows[0][0].shape[0]
    tile = min(tile, n)
    nr, nv = len(rows), len(vecs)

    def body(*refs):
        rv = [r[...].astype(F32) for r in refs[:nr]]
        vv = [r[...] for r in refs[nr:nr + nv]]
        for o_ref, val in zip(refs[nr + nv:], fn(*rv, *vv)):
            o_ref[...] = val.astype(o_ref.dtype)

    res = _pc(body, name=name, grid=(n // tile,), in_specs=_row_specs(rows, tile) + _vec_specs(vecs),
              out_specs=[pl.BlockSpec((tile, w), lambda i: (i, 0)) for (w, _) in outs],
              out_shape=[jax.ShapeDtypeStruct((n, w), dt) for (w, dt) in outs],
              compiler_params=_cparams())(*[r[0] for r in rows], *vecs)
    return res


def _rowwise_bwd(fn, rows, vecs, cts, row_dtypes, name, tile=256, add=None):
    n = rows[0][0].shape[0]
    tile = min(tile, n)
    nr, nv, nc = len(rows), len(vecs), len(cts)
    want = [i for i, dt in enumerate(row_dtypes) if dt is not None]
    n_add = 0 if add is None else 1

    def body(*refs):
        rv = [r[...].astype(F32) for r in refs[:nr]]
        vv = [r[...] for r in refs[nr:nr + nv]]
        cv = [r[...].astype(F32) for r in refs[nr + nv:nr + nv + nc]]
        pos = nr + nv + nc
        add_ref = refs[pos] if n_add else None
        pos += n_add
        row_out = refs[pos:pos + len(want)]
        vec_out = refs[pos + len(want):]
        _, vjp = jax.vjp(fn, *rv, *vv)
        grads = vjp(tuple(cv))
        for o_ref, i in zip(row_out, want):
            gval = grads[i]
            if n_add and add[0] == i:
                gval = gval + add_ref[...]
            o_ref[...] = gval.astype(o_ref.dtype)

        @pl.when(pl.program_id(0) == 0)
        def _():
            for o_ref in vec_out:
                o_ref[...] = jnp.zeros_like(o_ref)

        for o_ref, gval in zip(vec_out, grads[nr:]):
            o_ref[...] += gval

    ct_rows = [(c, 0, c.shape[1]) for c in cts]
    add_rows = [(add[1], 0, add[1].shape[1])] if n_add else []
    res = _pc(body, name=name, grid=(n // tile,),
              in_specs=_row_specs(rows, tile) + _vec_specs(vecs) + _row_specs(ct_rows + add_rows, tile),
              out_specs=[pl.BlockSpec((tile, rows[i][2]), lambda i_: (i_, 0)) for i in want] + _vec_specs(vecs),
              out_shape=[jax.ShapeDtypeStruct((n, rows[i][2]), row_dtypes[i]) for i in want]
              + [jax.ShapeDtypeStruct(v.shape, F32) for v in vecs],
              compiler_params=_cparams(("arbitrary",)))(*[r[0] for r in rows], *vecs, *cts, *[a[0] for a in add_rows])
    return res[:len(want)], res[len(want):]


def _whole(a, cb=0, w=None):
    return (a, cb, a.shape[1] if w is None else w)


def _ln(x, g, b):
    mu = jnp.mean(x, axis=-1, keepdims=True)
    var = jnp.mean(jnp.square(x - mu), axis=-1, keepdims=True)
    return (x - mu) * lax.rsqrt(var + LN_EPS) * g + b


def _silu(x):
    return x * jax.nn.sigmoid(x)


def _softplus(x):
    return jnp.maximum(x, 0.0) + jnp.log(1.0 + jnp.exp(-jnp.abs(x)))


def _f_mod(x, sc, sh):
    return (x * (1.0 + sc) + sh,)


def _f_gate(ga, gb, ya, yb):
    return (jax.nn.sigmoid(ga) * ya + jax.nn.sigmoid(gb) * yb,)


def _f_post1(x, mixed, gt, g1, b1, sc2, sh2):
    x1 = _ln(ALPHA * x + (1.0 + gt) * mixed, g1, b1)
    return x1, x1 * (1.0 + sc2) + sh2


def _f_act(hpre, b):
    return (jnp.square(jnp.maximum(hpre + b, 0.0)),)


def _f_post2(x1, ff, gt, bff2, g2, b2):
    return (_ln(ALPHA * x1 + (1.0 + gt) * (ff + bff2), g2, b2),)


def _attn_valid(n):
    qi = lax.broadcasted_iota(jnp.int32, (WINDOW, 2 * WINDOW), 0)
    si = lax.broadcasted_iota(jnp.int32, (WINDOW, 2 * WINDOW), 1)
    diff = qi + WINDOW - si
    return (diff >= 0) & (diff < WINDOW) & (n * WINDOW + si - WINDOW >= 0)


def _attn_block(qs, kp, kc, vp, vc, sinks, valid):
    kband = jnp.concatenate([kp, kc], axis=0).astype(BF16)
    vband = jnp.concatenate([vp, vc], axis=0).astype(BF16)
    rng = range(len(qs))
    s = [lax.dot_general(qs[g].astype(BF16), kband, (((1,), (1,)), ((), ())), preferred_element_type=F32) for g in rng]
    s = [jnp.where(valid, s[g] * (ATT_HEAD_DIM ** -0.5), NEG) for g in rng]
    m = [lax.stop_gradient(jnp.maximum(jnp.max(s[g], axis=-1, keepdims=True), sinks[g])) for g in rng]
    p = [jnp.exp(s[g] - m[g]) for g in rng]
    denom = [jnp.sum(p[g], axis=-1, keepdims=True) + jnp.exp(sinks[g] - m[g]) for g in rng]
    probs = [(p[g] / denom[g]).astype(BF16) for g in rng]
    return [jnp.dot(probs[g], vband, preferred_element_type=F32) for g in rng]


def _attn_specs(s):
    nb = s // WINDOW
    q_spec = pl.BlockSpec((1, ATT_GROUP, WINDOW, ATT_HEAD_DIM), lambda h, n: (h, 0, n, 0))
    prev = pl.BlockSpec((1, WINDOW, ATT_HEAD_DIM), lambda h, n: (h, jnp.maximum(n - 1, 0), 0))
    cur = pl.BlockSpec((1, WINDOW, ATT_HEAD_DIM), lambda h, n: (h, n, 0))
    sk = pl.BlockSpec((1, ATT_GROUP, 1, 1), lambda h, n: (h, 0, 0, 0))
    return nb, q_spec, prev, cur, sk


def _attn_fwd(qh, kh, vh, sinks4, exchange=None):
    s = qh.shape[2]
    nb, q_spec, prev, cur, sk = _attn_specs(s)

    def body(q_ref, kp_ref, kc_ref, vp_ref, vc_ref, sk_ref, o_ref):
        valid = _attn_valid(pl.program_id(1))
        heads = range(ATT_GROUP)
        o = _attn_block([q_ref[0, g] for g in heads], kp_ref[0], kc_ref[0], vp_ref[0], vc_ref[0],
                        [sk_ref[0, g] for g in heads], valid)
        for g in heads:
            o_ref[0, g] = o[g].astype(o_ref.dtype)

    (o,), exchanged = _call_with_exchange(
        body, "attn_fwd", (ATT_KV_HEADS, nb), [q_spec, prev, cur, prev, cur, sk], [q_spec],
        [jax.ShapeDtypeStruct(qh.shape, BF16)], [], (qh, kh, kh, vh, vh, sinks4), exchange)
    return o, exchanged


def _attn_bwd(qh, kh, vh, sinks4, doh):
    s = qh.shape[2]
    nb, q_spec, prev, cur, sk = _attn_specs(s)
    acc = pl.BlockSpec((1, s + WINDOW, ATT_HEAD_DIM), lambda h, n: (h, 0, 0))

    def body(q_ref, kp_ref, kc_ref, vp_ref, vc_ref, sk_ref, do_ref, dq_ref, dk_ref, dv_ref, dsk_ref):
        n = pl.program_id(1)
        valid = _attn_valid(n)
        fn = functools.partial(_attn_block, valid=valid)
        heads = range(ATT_GROUP)
        _, vjp = jax.vjp(fn, [q_ref[0, g] for g in heads], kp_ref[0], kc_ref[0], vp_ref[0], vc_ref[0],
                         [sk_ref[0, g] for g in heads])
        dq, dkp, dkc, dvp, dvc, dsk = vjp([do_ref[0, g].astype(F32) for g in heads])
        for g in heads:
            dq_ref[0, g] = dq[g]

        @pl.when(n == 0)
        def _():
            dk_ref[...] = jnp.zeros_like(dk_ref)
            dv_ref[...] = jnp.zeros_like(dv_ref)
            dsk_ref[...] = jnp.zeros_like(dsk_ref)

        band = pl.ds(pl.multiple_of(n * WINDOW, WINDOW), 2 * WINDOW)
        dk_ref[0, band, :] += jnp.concatenate([dkp, dkc], axis=0)
        dv_ref[0, band, :] += jnp.concatenate([dvp, dvc], axis=0)
        for g in heads:
            dsk_ref[0, g] += dsk[g]

    kv_shape = jax.ShapeDtypeStruct((ATT_KV_HEADS, s + WINDOW, ATT_HEAD_DIM), F32)
    return _pc(body, name="attn_bwd", grid=(ATT_KV_HEADS, nb), in_specs=[q_spec, prev, cur, prev, cur, sk, q_spec],
               out_specs=[q_spec, acc, acc, sk],
               out_shape=[jax.ShapeDtypeStruct(qh.shape, F32), kv_shape, kv_shape, jax.ShapeDtypeStruct(sinks4.shape, F32)],
               compiler_params=_cparams(("arbitrary", "arbitrary")))(qh, kh, kh, vh, vh, sinks4, doh)


def _bdot(a, b, dims=(((1,), (0,)), ((), ()))):
    return lax.dot_general(a.astype(BF16), b.astype(BF16), dims, preferred_element_type=F32)


def _hdot(a, b, dims=(((1,), (0,)), ((), ()))):
    return lax.dot_general(a, b, dims, precision=HI, preferred_element_type=F32)


_NN = (((1,), (0,)), ((), ()))
_NT = (((1,), (1,)), ((), ()))
_TN = (((0,), (0,)), ((), ()))


def _split2(a):
    hi = a.astype(BF16)
    return hi, (a - hi.astype(F32)).astype(BF16)


def _dot3(a, b, dims):
    ah, al = _split2(a)
    bh, bl = _split2(b)
    d = lambda p, q: lax.dot_general(p, q, dims, preferred_element_type=F32)
    return d(ah, bh) + (d(ah, bl) + d(al, bh))


@jax.custom_vjp
def _xdot(a, b):
    return _dot3(a, b, _NN)


def _xdot_fwd(a, b):
    return _dot3(a, b, _NN), (a, b)


def _xdot_bwd(res, g):
    a, b = res
    return _dot3(g, b, _NT), _dot3(a, g, _TN)


_xdot.defvjp(_xdot_fwd, _xdot_bwd)


def _mask_dot(mask16, b, dims):
    hi = b.astype(BF16)
    r = b - hi.astype(F32)
    mid = r.astype(BF16)
    lo = (r - mid.astype(F32)).astype(BF16)
    d = lambda q: lax.dot_general(mask16, q, dims, preferred_element_type=F32)
    return d(hi) + (d(mid) + d(lo))


def _chunk_masks():
    r = lax.broadcasted_iota(jnp.int32, (CHUNK, CHUNK), 0)
    c = lax.broadcasted_iota(jnp.int32, (CHUNK, CHUNK), 1)
    return r >= c, r > c, (r == c).astype(F32)


def _dn_local(qs, ks, vs, bs, gs, masks):
    causal, strict, eye = masks
    rng = range(len(qs))
    gb = [jnp.broadcast_to(gs[i], (CHUNK, CHUNK)) for i in rng]
    decay = [jnp.exp(jnp.where(causal, gb[i] - gb[i].T, NEG)) for i in rng]
    kb = [ks[i] * bs[i] for i in rng]
    vb = [vs[i] * bs[i] for i in rng]
    kk = [_bdot(kb[i], ks[i], _NT) for i in rng]
    p = [-jnp.where(strict, kk[i] * decay[i], 0.0) for i in rng]
    t = [eye + p[i] for i in rng]
    for _ in range(5):
        p = [_xdot(p[i], p[i]) for i in rng]
        t = [t[i] + _xdot(p[i], t[i]) for i in rng]
    eg = [jnp.exp(gs[i]) for i in rng]
    u = [_xdot(t[i], vb[i]) for i in rng]
    w = [_xdot(t[i], kb[i] * eg[i]) for i in rng]
    qk = [_bdot(qs[i], ks[i], _NT) for i in rng]
    intra = [qk[i] * decay[i] for i in rng]
    q_dec = [qs[i] * eg[i] for i in rng]
    k_dec = [ks[i] * jnp.exp(gs[i][CHUNK - 1:CHUNK, :] - gs[i]) for i in rng]
    return u, w, intra, q_dec, k_dec


def _dn_state_bwd(u, w, intra, q_dec, k_dec, gcum, state, do, dnext):
    last = jnp.exp(gcum[CHUNK - 1:CHUNK, :])
    x = _bdot(k_dec, dnext)
    t1 = _bdot(intra, do, _TN)
    v_new = u - _bdot(w, state)
    dqd = _bdot(do, state, _NT)
    din = _bdot(do, v_new, _NT)
    dkd = _bdot(v_new, dnext, _NT)
    base = _bdot(q_dec, do, _TN) - _bdot(w, t1, _TN)
    d_vnew = t1 + x
    dw = -_bdot(d_vnew, state, _NT)
    dstate = dnext * last + base - _bdot(w, x, _TN)
    dlast = jnp.sum(jnp.sum(state * dnext, axis=1, keepdims=True), axis=0, keepdims=True)
    row = lax.broadcasted_iota(jnp.int32, (CHUNK, 1), 0)
    dgc = jnp.where(row == CHUNK - 1, dlast * last, 0.0)
    return d_vnew, dw, din, dqd, dkd, dgc, dstate


def _l2norm(t):
    return t * lax.rsqrt(jnp.sum(jnp.square(t), axis=-1, keepdims=True) + RMS_EPS)


def _dn_pre(aq, ak, av, ba, alog, dtb, h):
    lane = lax.broadcasted_iota(jnp.int32, (1, LANES), 1)
    pick = lambda t, i: jnp.sum(jnp.where(lane == i, t, 0.0), axis=1, keepdims=True)
    q = _l2norm(_silu(aq)) * (DN_HEAD_DIM ** -0.5)
    k = _l2norm(_silu(ak))
    v = _silu(av)
    beta = jax.nn.sigmoid(pick(ba, h))
    g = -jnp.exp(pick(alog, h)) * _softplus(pick(ba, h + DN_HEADS) + pick(dtb, h))
    return q, k, v, beta, g


def _dn_post(o, z, nw):
    o = o * lax.rsqrt(jnp.mean(jnp.square(o), axis=-1, keepdims=True) + RMS_EPS) * nw
    return o * _silu(z)


_PAD = 8
_TOK_TILE = 512


def _pad_front(pad_ref, x_ref, s):
    pad_ref[pl.ds(0, _PAD), :] = jnp.zeros((_PAD, pad_ref.shape[1]), F32)
    pad_ref[pl.ds(_PAD, s), :] = x_ref[...]


def _conv_tile(pad_ref, w4, r0, n):
    acc = None
    for j in range(CONV_K):
        term = pad_ref[pl.ds(r0 + _PAD - (CONV_K - 1) + j, n), :] * w4[j:j + 1, :]
        acc = term if acc is None else acc + term
    return acc


def _conv_tile_bwd(pad_ref, da_ref, w4, r0, n):
    dx, dw = None, []
    da = da_ref[pl.ds(r0, n), :]
    for j in range(CONV_K):
        term = da_ref[pl.ds(r0 + CONV_K - 1 - j, n), :] * w4[j:j + 1, :]
        dx = term if dx is None else dx + term
        dw.append(jnp.sum(da * pad_ref[pl.ds(r0 + _PAD - (CONV_K - 1) + j, n), :], axis=0, keepdims=True))
    return dx, jnp.concatenate(dw, axis=0)


def _dn_gcum(g_c, causal_f):
    return _mask_dot(causal_f, jnp.broadcast_to(g_c, (CHUNK, LANES)), _NN)[:, 0:1]


def _dn_in_specs(s):
    col = lambda base: pl.BlockSpec((s, DN_HEAD_DIM), functools.partial(lambda h, b: (0, b + h), b=base // DN_HEAD_DIM))
    cw = lambda base: pl.BlockSpec((CONV_K, DN_HEAD_DIM), functools.partial(lambda h, b: (0, b + h), b=base))
    row = pl.BlockSpec((1, LANES), lambda h: (0, 0))
    ba = pl.BlockSpec((s, LANES), lambda h: (0, C_BA // LANES))
    return [col(C_DQ), col(C_DK), col(C_DV), col(C_Z), ba, cw(0), cw(DN_HEADS), cw(2 * DN_HEADS), row, row, row]


def _chunk_rows(c):
    return pl.ds(pl.multiple_of(c * CHUNK, CHUNK), CHUNK)


def _group(nchunk, want):
    g = min(want, nchunk)
    assert nchunk % g == 0
    return g


def _dn_forward_scan(q_s, k_s, v_s, b_s, g_s, gc_s, loc, o_s, states_ref, s):
    masks = _chunk_masks()
    causal_f = masks[0].astype(BF16)
    nchunk = s // CHUNK
    grp = _group(nchunk, 8)
    u_s, w_s, in_s, qd_s, kd_s = loc

    def local_step(i, carry):
        rows = [_chunk_rows(i * grp + j) for j in range(grp)]
        gcum = [_dn_gcum(g_s[r, :], causal_f) for r in rows]
        u, w, intra, q_dec, k_dec = _dn_local([q_s[r, :] for r in rows], [k_s[r, :] for r in rows],
                                              [v_s[r, :] for r in rows], [b_s[r, :] for r in rows], gcum, masks)
        for j, r in enumerate(rows):
            gc_s[r, :] = gcum[j]
            u_s[r, :] = u[j]
            w_s[r, :] = w[j].astype(w_s.dtype)
            in_s[r, :] = intra[j].astype(in_s.dtype)
            qd_s[r, :] = q_dec[j].astype(qd_s.dtype)
            kd_s[r, :] = k_dec[j].astype(kd_s.dtype)
        return carry

    lax.fori_loop(0, nchunk // grp, local_step, 0)

    def state_step(i, state):
        rows = _chunk_rows(i)
        states_ref[i] = state
        v_new = u_s[rows, :] - _bdot(w_s[rows, :], state)
        o_s[rows, :] = _bdot(qd_s[rows, :], state) + _bdot(in_s[rows, :], v_new)
        last = jnp.exp(gc_s[rows, :][CHUNK - 1:CHUNK, :])
        return state * last + _bdot(kd_s[rows, :], v_new, _TN)

    lax.fori_loop(0, nchunk, state_step, jnp.zeros((DN_HEAD_DIM, DN_HEAD_DIM), F32))


def _dn_saved_shapes(s):
    d, h = DN_HEAD_DIM, DN_HEADS
    shapes = [((h, s, d), F32), ((h, s, d), BF16), ((h, s, CHUNK), BF16), ((h, s, d), BF16), ((h, s, d), BF16),
              ((h, s, 1), F32), ((h, s // CHUNK, d, d), F32), ((h, s, d), F32)]
    return [jax.ShapeDtypeStruct(shp, dt) for shp, dt in shapes]


def _dn_saved_specs(s, **kw):
    return [pl.BlockSpec((1,) + t.shape[1:], functools.partial(lambda h, nd: (h,) + (0,) * nd, nd=len(t.shape) - 1), **kw)
            for t in _dn_saved_shapes(s)]


def _call_with_exchange(body, name, steps, in_specs, out_specs, out_shape, scratch, args, exchange):
    steps = (steps,) if isinstance(steps, int) else tuple(steps)
    params = _cparams(("arbitrary",) * len(steps))
    if exchange is None:
        res = _pc(body, name=name, grid=steps, in_specs=in_specs, out_specs=out_specs, out_shape=out_shape,
                  scratch_shapes=scratch, compiler_params=params)(*args)
        return res, None
    arrays, gather = exchange
    x_in, x_out, x_shape, x_scratch = _exchange_specs(arrays, gather)
    wrapped = _carry_exchange(body, len(in_specs), len(out_specs), len(scratch), len(arrays), gather, steps)
    res = _pc(wrapped, name=name + "_x", grid=steps, in_specs=in_specs + x_in, out_specs=out_specs + x_out,
              out_shape=out_shape + x_shape, scratch_shapes=scratch + x_scratch, compiler_params=params)(*args, *arrays)
    return res[:len(out_specs)], res[len(out_specs):]


def _dn_fwd(proj, conv_w, alog, dtb, nw, exchange=None):
    s = proj.shape[0]
    d = DN_HEAD_DIM

    tt = min(_TOK_TILE, s)

    def body(xq, xk, xv, z, ba, wq, wk, wv, alog_r, dtb_r, nw_r, o_ref, u_o, w_o, in_o, qd_o, kd_o, gc_o, st_o, oraw_o,
             padq, padk, padv, q_s, k_s, v_s, b_s, g_s):
        h = pl.program_id(0)
        loc = [r.at[0] for r in (u_o, w_o, in_o, qd_o, kd_o)]
        gc_s, states, o_s = gc_o.at[0], st_o.at[0], oraw_o.at[0]
        _pad_front(padq, xq, s)
        _pad_front(padk, xk, s)
        _pad_front(padv, xv, s)
        for r0 in range(0, s, tt):
            rows = pl.ds(r0, tt)
            aq, ak, av = _conv_tile(padq, wq[...], r0, tt), _conv_tile(padk, wk[...], r0, tt), _conv_tile(padv, wv[...], r0, tt)
            q_s[rows, :], k_s[rows, :], v_s[rows, :], b_s[rows, :], g_s[rows, :] = _dn_pre(
                aq, ak, av, ba[rows, :], alog_r[...], dtb_r[...], h)
        _dn_forward_scan(q_s, k_s, v_s, b_s, g_s, gc_s, loc, o_s, states, s)
        for r0 in range(0, s, tt):
            rows = pl.ds(r0, tt)
            o_ref[rows, :] = _dn_post(o_s[rows, :], z[rows, :], nw_r[...]).astype(o_ref.dtype)

    big = pltpu.VMEM((s, d), F32)
    thin = pltpu.VMEM((s, 1), F32)
    padded = pltpu.VMEM((s + _PAD, d), F32)
    return _call_with_exchange(
        body, "dn_fwd", DN_HEADS, _dn_in_specs(s), [pl.BlockSpec((s, d), lambda h: (0, h))] + _dn_saved_specs(s),
        [jax.ShapeDtypeStruct((s, DN_HEADS * d), BF16)] + _dn_saved_shapes(s),
        [padded, padded, padded, big, big, big, thin, thin],
        (proj, proj, proj, proj, proj, conv_w, conv_w, conv_w, alog, dtb, nw), exchange)


def _dn_bwd(proj, conv_w, alog, dtb, nw, dob, kept, exchange=None):
    s = proj.shape[0]
    d = DN_HEAD_DIM
    nchunk = s // CHUNK

    tt = min(_TOK_TILE, s)

    def body(xq, xk, xv, z, ba, wq, wk, wv, alog_r, dtb_r, nw_r, dob_ref, u_i, w_i, in_i, qd_i, kd_i, gc_i, st_i, oraw_i,
             dxq, dxk, dxv, dz, dba, dwq, dwk, dwv, dalog, ddtb, dnw,
             padq, padk, padv, q_s, k_s, v_s, b_s, g_s, o_s, dq_s, dk_s, dv_s, db_s, dg_s, dkd_s, din_s, dgc_s):
        h = pl.program_id(0)
        u_s, w_s, in_s, qd_s, kd_s = [r.at[0] for r in (u_i, w_i, in_i, qd_i, kd_i)]
        gc_s, states, oraw = gc_i.at[0], st_i.at[0], oraw_i.at[0]
        masks = _chunk_masks()
        causal_f = masks[0].astype(BF16)
        pre = functools.partial(_dn_pre, h=h)
        _pad_front(padq, xq, s)
        _pad_front(padk, xk, s)
        _pad_front(padv, xv, s)

        def conv_tiles(r0):
            return _conv_tile(padq, wq[...], r0, tt), _conv_tile(padk, wk[...], r0, tt), _conv_tile(padv, wv[...], r0, tt)

        for r0 in range(0, s, tt):
            rows = pl.ds(r0, tt)
            q_s[rows, :], k_s[rows, :], v_s[rows, :], b_s[rows, :], g_s[rows, :] = pre(
                *conv_tiles(r0), ba[rows, :], alog_r[...], dtb_r[...])
        dnw_v = jnp.zeros((1, LANES), F32)
        for r0 in range(0, s, tt):
            rows = pl.ds(r0, tt)
            _, post_vjp = jax.vjp(_dn_post, oraw[rows, :], z[rows, :], nw_r[...])
            do_raw, dz_v, dnw_t = post_vjp(dob_ref[rows, :].astype(F32))
            dz[rows, :] = dz_v.astype(dz.dtype)
            o_s[rows, :] = do_raw
            dnw_v = dnw_v + dnw_t

        def state_step(i, dstate):
            c = nchunk - 1 - i
            rows = _chunk_rows(c)
            du, dw, din, dqd, dkd, dgc, dstate = _dn_state_bwd(
                u_s[rows, :], w_s[rows, :], in_s[rows, :], qd_s[rows, :], kd_s[rows, :], gc_s[rows, :], states[c],
                o_s[rows, :], dstate)
            dq_s[rows, :] = du
            dk_s[rows, :] = dw
            dv_s[rows, :] = dqd
            dkd_s[rows, :] = dkd
            din_s[rows, :] = din
            dgc_s[rows, :] = dgc
            return dstate

        lax.fori_loop(0, nchunk, state_step, jnp.zeros((d, d), F32))
        local = functools.partial(_dn_local, masks=masks)
        grp = _group(nchunk, 8)

        def local_step(i, carry):
            rows = [_chunk_rows(i * grp + j) for j in range(grp)]
            get = lambda ref: [ref[r, :] for r in rows]
            _, vjp = jax.vjp(local, get(q_s), get(k_s), get(v_s), get(b_s), get(gc_s))
            dq_c, dk_c, dv_c, db_c, dgc_c = vjp((get(dq_s), get(dk_s), get(din_s), get(dv_s), get(dkd_s)))
            dgc_c = [dgc_c[j] + dgc_s[r, :] for j, r in enumerate(rows)]
            dg_c = [_mask_dot(causal_f, jnp.broadcast_to(t, (CHUNK, LANES)), _TN)[:, 0:1] for t in dgc_c]
            for j, r in enumerate(rows):
                dq_s[r, :] = dq_c[j]
                dk_s[r, :] = dk_c[j]
                dv_s[r, :] = dv_c[j]
                db_s[r, :] = db_c[j]
                dg_s[r, :] = dg_c[j]
            return carry

        lax.fori_loop(0, nchunk // grp, local_step, 0)

        @pl.when(h == 0)
        def _():
            dba[...] = jnp.zeros_like(dba)
            dalog[...] = jnp.zeros_like(dalog)
            ddtb[...] = jnp.zeros_like(ddtb)
            dnw[...] = jnp.zeros_like(dnw)

        dalog_v = jnp.zeros((1, LANES), F32)
        ddtb_v = jnp.zeros((1, LANES), F32)
        for r0 in range(0, s, tt):
            rows = pl.ds(r0, tt)
            _, pre_vjp = jax.vjp(pre, *conv_tiles(r0), ba[rows, :], alog_r[...], dtb_r[...])
            daq, dak, dav, dba_t, dalog_t, ddtb_t = pre_vjp(
                (dq_s[rows, :], dk_s[rows, :], dv_s[rows, :], db_s[rows, :], dg_s[rows, :]))
            dq_s[rows, :], dk_s[rows, :], dv_s[rows, :] = daq, dak, dav
            dba[rows, :] += dba_t
            dalog_v = dalog_v + dalog_t
            ddtb_v = ddtb_v + ddtb_t
        tail = pl.ds(s, _PAD)
        dq_s[tail, :] = dk_s[tail, :] = dv_s[tail, :] = jnp.zeros((_PAD, d), F32)
        for pad, da_s, w_ref, dx_ref, dw_ref in ((padq, dq_s, wq, dxq, dwq), (padk, dk_s, wk, dxk, dwk), (padv, dv_s, wv, dxv, dwv)):
            dw_acc = jnp.zeros((CONV_K, d), F32)
            for r0 in range(0, s, tt):
                dx_t, dw_t = _conv_tile_bwd(pad, da_s, w_ref[...], r0, tt)
                dx_ref[pl.ds(r0, tt), :] = dx_t.astype(dx_ref.dtype)
                dw_acc = dw_acc + dw_t
            dw_ref[...] = dw_acc
        dalog[...] += dalog_v
        ddtb[...] += ddtb_v
        dnw[...] += dnw_v

    big = pltpu.VMEM((s, d), F32)
    thin = pltpu.VMEM((s, 1), F32)
    padded = pltpu.VMEM((s + _PAD, d), F32)
    w_all = DN_HEADS * d
    col_out = lambda: pl.BlockSpec((s, d), lambda h: (0, h))
    cw_out = lambda: pl.BlockSpec((CONV_K, d), lambda h: (0, h))
    row = lambda: pl.BlockSpec((1, LANES), lambda h: (0, 0))
    big_out = jax.ShapeDtypeStruct((s, w_all), BF16)
    cw_shape = jax.ShapeDtypeStruct((CONV_K, w_all), F32)
    row_shape = jax.ShapeDtypeStruct((1, LANES), F32)
    return _call_with_exchange(
        body, "dn_bwd", DN_HEADS,
        _dn_in_specs(s) + [pl.BlockSpec((s, d), lambda h: (0, h))] + _dn_saved_specs(s, pipeline_mode=pl.Buffered(1)),
        [col_out(), col_out(), col_out(), col_out(), pl.BlockSpec((s, LANES), lambda h: (0, 0)),
         cw_out(), cw_out(), cw_out(), row(), row(), row()],
        [big_out, big_out, big_out, big_out, jax.ShapeDtypeStruct((s, LANES), F32),
         cw_shape, cw_shape, cw_shape, row_shape, row_shape, row_shape],
        [padded, padded, padded, big, big, big, thin, thin, big,
         padded, padded, padded, thin, thin, big, pltpu.VMEM((s, CHUNK), F32), thin],
        (proj, proj, proj, proj, proj, conv_w, conv_w, conv_w, alog, dtb, nw, dob, *kept), exchange)


def _loss_head(y, target, tile=256):
    n, dm = y.shape
    tile = min(tile, n)

    def body(y_ref, t_ref, dy_ref, loss_ref):
        err = y_ref[...] - t_ref[...]
        dy_ref[...] = err * (1.0 / dm)

        @pl.when(pl.program_id(0) == 0)
        def _():
            loss_ref[...] = jnp.zeros_like(loss_ref)

        loss_ref[...] += 0.5 * jnp.sum(jnp.mean(jnp.square(err), axis=-1, keepdims=True), axis=0, keepdims=True)

    blk = pl.BlockSpec((tile, dm), lambda i: (i, 0))
    return _pc(body, name="loss_head", grid=(n // tile,), in_specs=[blk, blk],
               out_specs=[blk, pl.BlockSpec((1, 1), lambda i: (0, 0))],
               out_shape=[jax.ShapeDtypeStruct((n, dm), F32), jax.ShapeDtypeStruct((1, 1), F32)],
               compiler_params=_cparams(("arbitrary",)))(y, target)


def _ada_fwd(c_all, w_ada, b_shard):
    nl, dm, n = w_ada.shape

    def body(c_ref, w_ref, b_ref, o_ref):
        ca = _silu(c_ref[...]).astype(BF16)
        o_ref[0] = jnp.dot(ca, w_ref[0].astype(BF16), preferred_element_type=F32) + b_ref[0]

    return _pc(body, name="ada_fwd", grid=(nl,),
               in_specs=[pl.BlockSpec((N_DEV, dm), lambda l: (0, 0)), pl.BlockSpec((1, dm, n), lambda l: (l, 0, 0)),
                         pl.BlockSpec((1, 1, n), lambda l: (l, 0, 0))],
               out_specs=pl.BlockSpec((1, N_DEV, n), lambda l: (l, 0, 0)),
               out_shape=jax.ShapeDtypeStruct((nl, N_DEV, n), F32), compiler_params=_cparams())(c_all, w_ada, b_shard)


def _ada_bwd(c_all, dmod):
    nl, _, n = dmod.shape
    dm = c_all.shape[1]

    def body(c_ref, d_ref, o_ref):
        o_ref[0] = _hdot(_silu(c_ref[...]), d_ref[0], _TN)

    return _pc(body, name="ada_bwd", grid=(nl,),
               in_specs=[pl.BlockSpec((N_DEV, dm), lambda l: (0, 0)), pl.BlockSpec((1, N_DEV, n), lambda l: (l, 0, 0))],
               out_specs=pl.BlockSpec((1, dm, n), lambda l: (l, 0, 0)),
               out_shape=jax.ShapeDtypeStruct((nl, dm, n), F32), compiler_params=_cparams())(c_all, dmod)


def _adamw(g, w, m, v):
    m = ADAM_B1 * m + (1.0 - ADAM_B1) * g
    v = ADAM_B2 * v + (1.0 - ADAM_B2) * jnp.square(g)
    m_hat = m / (1.0 - ADAM_B1 ** ADAM_STEP)
    v_hat = v / (1.0 - ADAM_B2 ** ADAM_STEP)
    delta = -ADAM_LR * (m_hat / (jnp.sqrt(v_hat) + ADAM_EPS) + ADAM_WD * w)
    return delta, m, v


def _adam_call(parts, w, m, v, name, tile=128):
    shape = w.shape
    flat = lambda t: t.reshape(-1, shape[-1])
    width = shape[-1]

    def fn(*vals):
        g = vals[0] if len(parts) == 1 else vals[0] + vals[1]
        return (g,) + _adamw(g, *vals[len(parts):])

    rows = [_whole(flat(t)) for t in (*parts, w, m, v)]
    outs = _rowwise(fn, rows, [], [(width, F32)] * 4, name, tile=tile)
    return [o.reshape(shape) for o in outs]


def _sum_slots(per_layer, name, tile=128):
    nl = len(per_layer)
    _, n, width = per_layer[0].shape
    tile = min(tile, n)
    nt = n // tile

    def body(*refs):
        o_ref = refs[nl]
        for lp in range(nl):
            @pl.when(pl.program_id(0) == lp)
            def _(r_ref=refs[lp]):
                acc = r_ref[0].astype(F32)
                for j in range(1, N_CHIPS):
                    acc = acc + r_ref[j].astype(F32)
                o_ref[...] = acc

    in_specs = [pl.BlockSpec((N_CHIPS, tile, width), functools.partial(lambda l, t, lp: (0, jnp.where(l == lp, t, 0), 0), lp=lp))
                for lp in range(nl)]
    return _pc(body, name=name, grid=(nl, nt), in_specs=in_specs,
               out_specs=pl.BlockSpec((tile, width), lambda l, t: (l * nt + t, 0)),
               out_shape=jax.ShapeDtypeStruct((nl * n, width), F32), compiler_params=_cparams())(*per_layer)


def _small_adam(g_all, w, m, v):
    def body(g_ref, w_ref, m_ref, v_ref, og, od, om, ov):
        g = g_ref[0]
        for j in range(1, N_DEV):
            g = g + g_ref[j]
        og[...] = g
        od[...], om[...], ov[...] = _adamw(g, w_ref[...], m_ref[...], v_ref[...])

    vm = pl.BlockSpec(memory_space=pltpu.VMEM)
    shp = jax.ShapeDtypeStruct(w.shape, F32)
    return _pc(body, name="small_adam", in_specs=[vm] * 4, out_specs=[vm] * 4, out_shape=[shp] * 4,
               compiler_params=_cparams())(g_all, w, m, v)


def _place():
    return lax.axis_index("x"), lax.axis_index("y"), lax.axis_index("c")


def _flip(v, bit):
    return 1 - v if bit else v


def _all_gather8(a):
    r, n = a.shape

    def body(a_ref, o_ref, send_sems, recv_sems):
        x, y, c = _place()
        me = 4 * x + 2 * y + c
        o_ref[me] = a_ref[...]
        copies = []
        for k in range(1, N_DEV):
            peer = (_flip(x, k & 4), _flip(y, k & 2), _flip(c, k & 1))
            copies.append(pltpu.make_async_remote_copy(
                src_ref=a_ref, dst_ref=o_ref.at[me], send_sem=send_sems.at[k - 1], recv_sem=recv_sems.at[k - 1],
                device_id=peer, device_id_type=MESH))
        for cp in copies:
            cp.start()
        for k in range(1, N_DEV):
            px, py, pc_ = _flip(x, k & 4), _flip(y, k & 2), _flip(c, k & 1)
            pltpu.make_async_remote_copy(
                src_ref=a_ref, dst_ref=o_ref.at[4 * px + 2 * py + pc_], send_sem=send_sems.at[k - 1],
                recv_sem=recv_sems.at[k - 1], device_id=(px, py, pc_), device_id_type=MESH).wait_recv()
        for cp in copies:
            cp.wait_send()

    vm = pl.BlockSpec(memory_space=pltpu.VMEM)
    return _pc(body, name="all_gather8", in_specs=[vm], out_specs=vm,
               out_shape=jax.ShapeDtypeStruct((N_DEV, r, n), a.dtype),
               scratch_shapes=[pltpu.SemaphoreType.DMA((N_DEV - 1,)), pltpu.SemaphoreType.DMA((N_DEV - 1,))],
               compiler_params=_cparams())(a)


def _chip_exchange(arrays, gather, name):
    na = len(arrays)

    def body(*refs):
        ins, outs, sems = refs[:na], refs[na:2 * na], refs[2 * na:]
        _exchange_copies(ins, outs, sems, gather, start=True)
        _exchange_copies(ins, outs, sems, gather, start=False)

    in_specs, out_specs, out_shape, scratch = _exchange_specs(arrays, gather)
    return _pc(body, name=name, in_specs=in_specs, out_specs=out_specs, out_shape=out_shape, scratch_shapes=scratch,
               compiler_params=_cparams())(*arrays)


def _exchange_specs(arrays, gather):
    na = len(arrays)
    hbm = pl.BlockSpec(memory_space=pl.ANY)
    out_shape = [jax.ShapeDtypeStruct(((N_CHIPS,) + a.shape) if gather else a.shape, a.dtype) for a in arrays]
    n_remote = 4 if gather else 2
    scratch = [pltpu.SemaphoreType.DMA((3 * na,))] * n_remote + [pltpu.SemaphoreType.DMA((na,))]
    return [hbm] * na, [hbm] * na, out_shape, scratch


def _gather_copies(ins, outs, sems, start):
    send_i, recv_i, send_d, recv_d, local_sems = sems
    x, y, c = _place()
    me = 2 * x + y
    sibling = (x, y, 1 - c)
    ici_sends, ici_arrivals, hand_ons, hand_arrivals, locals_ = [], [], [], [], []
    for i in range(len(ins)):
        half = ins[i].shape[0] // 2
        mine, other = pl.ds(c * half, half), pl.ds((1 - c) * half, half)
        locals_.append(pltpu.make_async_copy(ins[i], outs[i].at[me], local_sems.at[i]))
        for j in range(1, N_CHIPS):
            px, py = _flip(x, j & 2), _flip(y, j & 1)
            peer = 2 * px + py
            k = i * 3 + j - 1
            ici = dict(send_sem=send_i.at[k], recv_sem=recv_i.at[k], device_id=(px, py, c), device_id_type=MESH)
            d2d = dict(send_sem=send_d.at[k], recv_sem=recv_d.at[k], device_id=sibling, device_id_type=MESH)
            ici_sends.append(pltpu.make_async_remote_copy(src_ref=ins[i].at[mine], dst_ref=outs[i].at[me, mine], **ici))
            ici_arrivals.append(pltpu.make_async_remote_copy(src_ref=ins[i].at[mine], dst_ref=outs[i].at[peer, mine], **ici))
            hand_ons.append(pltpu.make_async_remote_copy(
                src_ref=outs[i].at[peer, mine], dst_ref=outs[i].at[peer, mine], **d2d))
            hand_arrivals.append(pltpu.make_async_remote_copy(
                src_ref=outs[i].at[peer, other], dst_ref=outs[i].at[peer, other], **d2d))
    if start:
        for cp in locals_ + ici_sends:
            cp.start()
    else:
        for arrival, hand_on in zip(ici_arrivals, hand_ons):
            arrival.wait_recv()
            hand_on.start()
        for cp in hand_arrivals:
            cp.wait_recv()
        for cp in ici_sends + hand_ons:
            cp.wait_send()
        for cp in locals_:
            cp.wait()


def _exchange_copies(ins, outs, sems, gather, start):
    if gather:
        return _gather_copies(ins, outs, sems, start)
    send_sems, recv_sems, local_sems = sems
    x, y, c = _place()
    me = 2 * x + y
    sends, arrivals, locals_ = [], [], []
    for i in range(len(ins)):
        locals_.append(pltpu.make_async_copy(ins[i] if gather else ins[i].at[me], outs[i].at[me], local_sems.at[i]))
        for j in range(1, N_CHIPS):
            px, py = _flip(x, j & 2), _flip(y, j & 1)
            peer = 2 * px + py
            pair = dict(send_sem=send_sems.at[i * 3 + j - 1], recv_sem=recv_sems.at[i * 3 + j - 1],
                        device_id=(px, py, c), device_id_type=MESH)
            sends.append(pltpu.make_async_remote_copy(
                src_ref=ins[i] if gather else ins[i].at[peer], dst_ref=outs[i].at[me], **pair))
            arrivals.append(pltpu.make_async_remote_copy(
                src_ref=ins[i] if gather else ins[i].at[me], dst_ref=outs[i].at[peer], **pair))
    if start:
        for cp in locals_ + sends:
            cp.start()
    else:
        for cp in arrivals:
            cp.wait_recv()
        for cp in sends:
            cp.wait_send()
        for cp in locals_:
            cp.wait()


def _carry_exchange(body, n_in, n_out, n_scratch, n_arrays, gather, steps):
    def wrapped(*refs):
        na = n_arrays
        ins, xin = refs[:n_in], refs[n_in:n_in + na]
        outs = refs[n_in + na:n_in + na + n_out]
        xout = refs[n_in + na + n_out:n_in + 2 * na + n_out]
        rest = refs[n_in + 2 * na + n_out:]
        scratch, sems = rest[:n_scratch], rest[n_scratch:]
        first, last = True, True
        for axis, n in enumerate(steps):
            first = jnp.logical_and(first, pl.program_id(axis) == 0)
            last = jnp.logical_and(last, pl.program_id(axis) == n - 1)

        @pl.when(first)
        def _():
            _exchange_copies(xin, xout, sems, gather, start=True)

        body(*ins, *outs, *scratch)

        @pl.when(last)
        def _():
            _exchange_copies(xin, xout, sems, gather, start=False)

    return wrapped


def _sibling_exchange(arrays, name):
    na = len(arrays)

    def body(*refs):
        ins, outs = refs[:na], refs[na:2 * na]
        send_sems, recv_sems = refs[2 * na:]
        x, y, c = _place()
        copies = [pltpu.make_async_remote_copy(
            src_ref=ins[i], dst_ref=outs[i], send_sem=send_sems.at[i], recv_sem=recv_sems.at[i],
            device_id=(x, y, 1 - c), device_id_type=MESH) for i in range(na)]
        for cp in copies:
            cp.start()
        for cp in copies:
            cp.wait()

    hbm = pl.BlockSpec(memory_space=pl.ANY)
    return _pc(body, name=name, in_specs=[hbm] * na, out_specs=[hbm] * na,
               out_shape=[jax.ShapeDtypeStruct(a.shape, a.dtype) for a in arrays],
               scratch_shapes=[pltpu.SemaphoreType.DMA((na,)), pltpu.SemaphoreType.DMA((na,))],
               compiler_params=_cparams())(*arrays)


def _heads_q(t):
    s = t.shape[0]
    return t.reshape(s, ATT_KV_HEADS, ATT_GROUP, ATT_HEAD_DIM).transpose(1, 2, 0, 3)


def _unheads_q(t):
    s = t.shape[2]
    return t.transpose(2, 0, 1, 3).reshape(s, ATT_KV_HEADS * ATT_GROUP * ATT_HEAD_DIM)


def _heads_kv(t):
    s = t.shape[0]
    return t.reshape(s, ATT_KV_HEADS, ATT_HEAD_DIM).transpose(1, 0, 2)


def _unheads_kv(t):
    s = t.shape[1]
    return t.transpose(1, 0, 2).reshape(s, ATT_KV_HEADS * ATT_HEAD_DIM)


def _row128(v):
    return jnp.pad(v, (0, LANES - v.shape[0])).reshape(1, LANES)


def _layer_fwd(x, p, exchange=None):
    sh1, sc1, gt1, sh2, sc2, gt2 = [p["mod"][i] for i in range(6)]
    (u,) = _rowwise(_f_mod, [_whole(x)], [sc1, sh1], [(D_MODEL, BF16)], "mod1")
    proj = _mm(u, p["w_in"], "nn", F32, "proj")
    qh = _heads_q(proj[:, C_Q:C_Q + 1024])
    kh = _heads_kv(proj[:, C_K:C_K + 256])
    vh = _heads_kv(proj[:, C_V:C_V + 256])
    sinks4 = p["sinks"].reshape(ATT_KV_HEADS, ATT_GROUP, 1, 1)
    behind_attn = None if exchange is None else (exchange[0][-1:], exchange[1])
    behind_dn = None if exchange is None else (exchange[0][:-1], exchange[1])
    o_heads, got_attn = _attn_fwd(qh, kh, vh, sinks4, behind_attn)
    o_a = _unheads_q(o_heads)
    (o_b, *dn_kept), got_dn = _dn_fwd(proj, p["conv_w"], _row128(p["a_log"]), _row128(p["dt_bias"]),
                                      p["dn_norm_w"].reshape(1, LANES), behind_dn)
    exchanged = None if exchange is None else list(got_dn) + list(got_attn)
    y_a = _mm(o_a, p["w_oa"], "nn", BF16, "y_a")
    y_b = _mm(o_b, p["w_ob"], "nn", BF16, "y_b")
    (gm,) = _rowwise(_f_gate, [(proj, C_GA // 1024, 1024), (proj, C_GB // 1024, 1024), _whole(y_a), _whole(y_b)], [],
                     [(D_MODEL, BF16)], "gate")
    mixed = _mm(gm, p["w_out"], "nn", BF16, "mixed")
    x1, u2 = _rowwise(_f_post1, [_whole(x), _whole(mixed)], [gt1, p["ln1_g"], p["ln1_b"], sc2, sh2],
                      [(D_MODEL, F32), (D_MODEL, BF16)], "post1")
    hpre = _mm(u2, p["w_ff1"], "nn", BF16, "ff1")
    (h,) = _rowwise(_f_act, [_whole(hpre)], [p["b_ff1"]], [(D_FF, BF16)], "act")
    ff = _mm(h, p["w_ff2"], "nn", BF16, "ff2")
    (x2,) = _rowwise(_f_post2, [_whole(x1), _whole(ff)], [gt2, p["b_ff2"], p["ln2_g"], p["ln2_b"]],
                     [(D_MODEL, F32)], "post2")
    saved = dict(x=x, u=u, proj=proj, o_a=o_a, o_b=o_b, y_a=y_a, y_b=y_b, gm=gm, mixed=mixed, x1=x1, u2=u2,
                 hpre=hpre, h=h, ff=ff, dn_kept=dn_kept, heads=(qh, kh, vh))
    return x2, saved, exchanged


def _layer_bwd(dx2, p, sv, carry=None):
    sh1, sc1, gt1, sh2, sc2, gt2 = [p["mod"][i] for i in range(6)]
    g = {}
    (dx1_a, dff), (dgt2, g["b_ff2"], g["ln2_g"], g["ln2_b"]) = _rowwise_bwd(
        _f_post2, [_whole(sv["x1"]), _whole(sv["ff"])], [gt2, p["b_ff2"], p["ln2_g"], p["ln2_b"]], [dx2],
        [F32, BF16], "post2_bwd")
    dh = _mm(dff, p["w_ff2"], "nt", BF16,"dh")
    g["w_ff2"] = _mm(sv["h"], dff, "tn", BF16,"dw_ff2")
    (dhpre,), (g["b_ff1"],) = _rowwise_bwd(_f_act, [_whole(sv["hpre"])], [p["b_ff1"]], [dh], [BF16], "act_bwd")
    du2 = _mm(dhpre, p["w_ff1"], "nt", BF16,"du2")
    g["w_ff1"] = _mm(sv["u2"], dhpre, "tn", BF16,"dw_ff1")
    (dx_a, dmixed), (dgt1, g["ln1_g"], g["ln1_b"], dsc2, dsh2) = _rowwise_bwd(
        _f_post1, [_whole(sv["x"]), _whole(sv["mixed"])], [gt1, p["ln1_g"], p["ln1_b"], sc2, sh2], [dx1_a, du2],
        [F32, BF16], "post1_bwd")
    dgm = _mm(dmixed, p["w_out"], "nt", BF16,"dgm")
    g["w_out"] = _mm(sv["gm"], dmixed, "tn", BF16,"dw_out")
    proj = sv["proj"]
    (dga, dgb, dya, dyb), _ = _rowwise_bwd(
        _f_gate, [(proj, C_GA // 1024, 1024), (proj, C_GB // 1024, 1024), _whole(sv["y_a"]), _whole(sv["y_b"])], [],
        [dgm], [BF16, BF16, BF16, BF16], "gate_bwd")
    do_a = _mm(dya, p["w_oa"], "nt", BF16,"do_a")
    g["w_oa"] = _mm(sv["o_a"], dya, "tn", BF16,"dw_oa")
    do_b = _mm(dyb, p["w_ob"], "nt", BF16,"do_b")
    g["w_ob"] = _mm(sv["o_b"], dyb, "tn", BF16,"dw_ob")
    exchange = None if carry is None else (carry(g), False)
    (ddq, ddk, ddv, ddz, dba, dwq, dwk, dwv, dalog, ddtb, dnw), exchanged = _dn_bwd(
        proj, p["conv_w"], _row128(p["a_log"]), _row128(p["dt_bias"]), p["dn_norm_w"].reshape(1, LANES), do_b,
        sv["dn_kept"], exchange)
    g["conv_w"] = jnp.concatenate([dwq, dwk, dwv], axis=1)
    g["a_log"], g["dt_bias"], g["dn_norm_w"] = dalog[0, :DN_HEADS], ddtb[0, :DN_HEADS], dnw[0]
    qh, kh, vh = sv["heads"]
    sinks4 = p["sinks"].reshape(ATT_KV_HEADS, ATT_GROUP, 1, 1)
    dqh, dkh, dvh, dsk = _attn_bwd(qh, kh, vh, sinks4, _heads_q(do_a))
    g["sinks"] = dsk.reshape(ATT_KV_HEADS * ATT_GROUP)
    s = proj.shape[0]
    dproj = jnp.concatenate([
        _unheads_q(dqh).astype(BF16), ddq, ddk, ddv, ddz, dga, dgb,
        _unheads_kv(dkh[:, WINDOW:, :]).astype(BF16), _unheads_kv(dvh[:, WINDOW:, :]).astype(BF16),
        dba.astype(BF16), jnp.zeros((s, D_IN_P - C_BA - LANES), BF16)], axis=1)
    du = _mm(dproj, p["w_in"], "nt", BF16,"du")
    g["w_in"] = _mm(sv["u"], dproj, "tn", BF16,"dw_in")
    (dx,), (dsc1, dsh1) = _rowwise_bwd(_f_mod, [_whole(sv["x"])], [sc1, sh1], [du], [F32], "mod1_bwd", add=(0, dx_a))
    g["mod"] = jnp.stack([dsh1, dsc1, dgt1, dsh2, dsc2, dgt2])
    return dx, g, exchanged


def _permute_w_in(w):
    pad = jnp.zeros(w.shape[:-1] + (D_IN_P - D_IN,), w.dtype)
    return jnp.concatenate([w[..., 0:1024], w[..., 1536:5632], w[..., 5648:7696], w[..., 1024:1536],
                            w[..., 5632:5648], pad], axis=-1)


def _unpermute_w_in(g):
    return jnp.concatenate([g[..., 0:1024], g[..., C_K:C_K + 512], g[..., 1024:5120], g[..., C_BA:C_BA + 16],
                            g[..., 5120:7168]], axis=-1)


def _cols_from_chips(t):
    c, l, r, n = t.shape
    return t.transpose(1, 2, 0, 3).reshape(l, r, c * n)


def _cols_to_chips(t):
    l, r, n4 = t.shape
    return t.reshape(l, r, N_CHIPS, n4 // N_CHIPS).transpose(2, 0, 1, 3)


def _rows_from_chips(t):
    c, l, r, n = t.shape
    return t.transpose(1, 0, 2, 3).reshape(l, c * r, n)


def _rows_to_chips(t):
    l, r4, n = t.shape
    return t.reshape(l, N_CHIPS, r4 // N_CHIPS, n).transpose(1, 0, 2, 3)


_REPLICATED = ("b_ada", "a_log", "dt_bias", "sinks", "dn_norm_w", "ln1_g", "ln1_b", "b_ff1", "b_ff2", "ln2_g", "ln2_b")
_SMALL = _REPLICATED + ("conv_w",)
_PACK_W = 1024
_WEIGHT_ORDER = ("w_ada", "b_ada", "w_in", "conv_w", "a_log", "dt_bias", "sinks", "dn_norm_w", "w_oa", "w_ob", "w_out",
                 "ln1_g", "ln1_b", "w_ff1", "b_ff1", "w_ff2", "b_ff2", "ln2_g", "ln2_b")


def _pack_small(d):
    flat = jnp.concatenate([d[k].reshape(-1) for k in _SMALL])
    rows = -(-flat.shape[0] // (_PACK_W * 8)) * 8
    return jnp.pad(flat, (0, rows * _PACK_W - flat.shape[0])).reshape(rows, _PACK_W)


def _unpack_small(packed, shapes):
    flat = packed.reshape(-1)
    out, off = {}, 0
    for k in _SMALL:
        n = 1
        for d_ in shapes[k]:
            n *= d_
        out[k] = flat[off:off + n].reshape(shapes[k])
        off += n
    return out


def kernel(x, c, w_ada, b_ada, w_in, conv_w, a_log, dt_bias, sinks, dn_norm_w, w_oa, w_ob, w_out, ln1_g, ln1_b, w_ff1, b_ff1, w_ff2, b_ff2, ln2_g, ln2_b, loss_target, m_w_ada, m_b_ada, m_w_in, m_conv_w, m_a_log, m_dt_bias, m_sinks, m_dn_norm_w, m_w_oa, m_w_ob, m_w_out, m_ln1_g, m_ln1_b, m_w_ff1, m_b_ff1, m_w_ff2, m_b_ff2, m_ln2_g, m_ln2_b, v_w_ada, v_b_ada, v_w_in, v_conv_w, v_a_log, v_dt_bias, v_sinks, v_dn_norm_w, v_w_oa, v_w_ob, v_w_out, v_ln1_g, v_ln1_b, v_w_ff1, v_b_ff1, v_w_ff2, v_b_ff2, v_ln2_g, v_ln2_b):
    ix, iy, ic = _place()
    chip = 2 * ix + iy
    dev = 4 * ix + 2 * iy + ic
    weights = dict(w_ada=w_ada, b_ada=b_ada, w_in=w_in, conv_w=conv_w, a_log=a_log, dt_bias=dt_bias, sinks=sinks,
                   dn_norm_w=dn_norm_w, w_oa=w_oa, w_ob=w_ob, w_out=w_out, ln1_g=ln1_g, ln1_b=ln1_b, w_ff1=w_ff1,
                   b_ff1=b_ff1, w_ff2=w_ff2, b_ff2=b_ff2, ln2_g=ln2_g, ln2_b=ln2_b)
    mom_m = dict(w_ada=m_w_ada, b_ada=m_b_ada, w_in=m_w_in, conv_w=m_conv_w, a_log=m_a_log, dt_bias=m_dt_bias,
                 sinks=m_sinks, dn_norm_w=m_dn_norm_w, w_oa=m_w_oa, w_ob=m_w_ob, w_out=m_w_out, ln1_g=m_ln1_g,
                 ln1_b=m_ln1_b, w_ff1=m_w_ff1, b_ff1=m_b_ff1, w_ff2=m_w_ff2, b_ff2=m_b_ff2, ln2_g=m_ln2_g, ln2_b=m_ln2_b)
    mom_v = dict(w_ada=v_w_ada, b_ada=v_b_ada, w_in=v_w_in, conv_w=v_conv_w, a_log=v_a_log, dt_bias=v_dt_bias,
                 sinks=v_sinks, dn_norm_w=v_dn_norm_w, w_oa=v_w_oa, w_ob=v_w_ob, w_out=v_w_out, ln1_g=v_ln1_g,
                 ln1_b=v_ln1_b, w_ff1=v_w_ff1, b_ff1=v_b_ff1, w_ff2=v_w_ff2, b_ff2=v_b_ff2, ln2_g=v_ln2_g, ln2_b=v_ln2_b)

    n_ada = w_ada.shape[2]
    n_cw = conv_w.shape[2]
    taps = jnp.pad(conv_w.reshape(DEPTH * CONV_K, n_cw), ((0, 0), (0, D_MODEL - n_cw)))
    first = _all_gather8(jnp.concatenate([jnp.pad(c, ((0, 7), (0, 0))), taps], axis=0))
    c_all = first[:, 0, :]
    b_shard = lax.dynamic_slice_in_dim(b_ada, chip * n_ada, n_ada, axis=1).reshape(DEPTH, 1, n_ada)
    mod_t = _ada_fwd(c_all, w_ada, b_shard)
    mod_all = _all_gather8(mod_t.reshape(DEPTH * N_DEV, n_ada)).reshape(N_DEV, DEPTH, N_DEV, n_ada)
    mod_mine = lax.dynamic_index_in_dim(mod_all[0::2], dev, axis=2, keepdims=False)
    mod = mod_mine.transpose(1, 0, 2).reshape(DEPTH, 6, 1, D_MODEL)

    cw_all = first[0::2, 8:, :n_cw]
    conv_full = cw_all.transpose(1, 0, 2).reshape(DEPTH, CONV_K, N_CHIPS * n_cw)

    big = ("w_in", "w_oa", "w_ob", "w_out", "w_ff1", "w_ff2")
    w16 = {k: weights[k].astype(BF16) for k in big}
    shards = lambda l: [w16[k][l] for k in big]

    def assemble(gathered):
        gw = {k: t[:, None] for k, t in zip(big, gathered)}
        full = dict(w_in=_permute_w_in(_cols_from_chips(gw["w_in"])), w_ff1=_cols_from_chips(gw["w_ff1"]),
                    w_oa=_rows_from_chips(gw["w_oa"]), w_ob=_rows_from_chips(gw["w_ob"]),
                    w_out=_rows_from_chips(gw["w_out"]), w_ff2=_rows_from_chips(gw["w_ff2"]))
        return {k: t[0] for k, t in full.items()}

    to_chips = dict(w_in=lambda t: _cols_to_chips(_unpermute_w_in(t)), w_ff1=_cols_to_chips, w_oa=_rows_to_chips,
                    w_ob=_rows_to_chips, w_out=_rows_to_chips, w_ff2=_rows_to_chips)

    def slices_for_chips(g, keys):
        return [to_chips[k](g[k][None])[:, 0].astype(BF16) for k in keys]

    full = [None] * DEPTH
    full[0] = assemble(_chip_exchange(shards(0), True, "gather_weights"))

    def layer_params(l):
        p = dict(full[l])
        p["mod"] = mod[l]
        p["conv_w"] = conv_full[l]
        for k in ("a_log", "dt_bias", "sinks", "dn_norm_w"):
            p[k] = weights[k][l]
        for k in ("ln1_g", "ln1_b", "b_ff1", "b_ff2", "ln2_g", "ln2_b"):
            p[k] = weights[k][l].reshape(1, -1)
        return p

    xs = x[0]
    saved = []
    for l in range(DEPTH):
        nxt = (shards(l + 1), True) if l + 1 < DEPTH else None
        xs, sv, gathered = _layer_fwd(xs, layer_params(l), nxt)
        if nxt is not None:
            full[l + 1] = assemble(gathered)
        saved.append(sv)
    dy, loss_local = _loss_head(xs, loss_target[0])
    loss = lax.psum(loss_local[0, 0], ("x", "y", "c"))
    early = tuple(k for k in big if k != "w_in")
    grads = [None] * DEPTH
    received = [dict() for _ in range(DEPTH)]
    dx = dy
    pending = []
    for l in reversed(range(DEPTH)):
        carry = functools.partial(lambda g, first: first + slices_for_chips(g, early), first=pending)
        dx, grads[l], got = _layer_bwd(dx, layer_params(l), saved[l], carry)
        if pending:
            received[l + 1]["w_in"] = got[0]
        received[l].update(zip(early, got[len(pending):]))
        pending = slices_for_chips(grads[l], ("w_in",))
    received[0]["w_in"] = _chip_exchange(pending, False, "scatter_grads")[0]
    grad_x = dx[None]
    gstack = {k: jnp.stack([grads[l][k] for l in range(DEPTH)]) for k in grads[0] if k not in big}

    dmod = gstack["mod"].reshape(DEPTH, 6 * D_MODEL)
    small_g = dict(b_ada=dmod, a_log=gstack["a_log"], dt_bias=gstack["dt_bias"], sinks=gstack["sinks"],
                   dn_norm_w=gstack["dn_norm_w"], ln1_g=gstack["ln1_g"], ln1_b=gstack["ln1_b"], b_ff1=gstack["b_ff1"],
                   b_ff2=gstack["b_ff2"], ln2_g=gstack["ln2_g"], ln2_b=gstack["ln2_b"], conv_w=gstack["conv_w"])
    shapes = {k: weights[k].shape for k in _REPLICATED}
    shapes["conv_w"] = small_g["conv_w"].shape
    g_all = _all_gather8(_pack_small(small_g))
    no_conv = jnp.zeros(shapes["conv_w"], F32)
    small_out = _small_adam(g_all, _pack_small(dict(weights, conv_w=no_conv)), _pack_small(dict(mom_m, conv_w=no_conv)),
                            _pack_small(dict(mom_v, conv_w=no_conv)))
    small_res = [_unpack_small(t, shapes) for t in small_out]
    g_conv = lax.dynamic_slice_in_dim(small_res[0]["conv_w"], chip * n_cw, n_cw, axis=2)
    res = {"conv_w": _adam_call([g_conv], conv_w, m_conv_w, v_conv_w, "adam_conv_w", tile=16)}

    dmod_all = g_all.reshape(N_DEV, -1)[:, :DEPTH * 6 * D_MODEL].reshape(N_DEV, DEPTH, 6 * D_MODEL)
    dmod_shard = lax.dynamic_slice_in_dim(dmod_all, chip * n_ada, n_ada, axis=2).transpose(1, 0, 2)
    g_w_ada = _ada_bwd(c_all, dmod_shard)
    res["w_ada"] = _adam_call([g_w_ada], w_ada, m_w_ada, v_w_ada, "adam_w_ada")

    partial = [_sum_slots([received[l][k] for l in range(DEPTH)], "sum_" + k) for k in big]
    theirs = _sibling_exchange(partial, "sibling_grads")
    for k, mine, other in zip(big, partial, theirs):
        shape = weights[k].shape
        res[k] = _adam_call([mine.reshape(shape), other.reshape(shape)], weights[k], mom_m[k], mom_v[k], "adam_" + k)
    for k in _REPLICATED:
        res[k] = [small_res[i][k] for i in range(4)]

    outs = [loss, grad_x]
    for i in range(4):
        outs += [res[k][i] for k in _WEIGHT_ORDER]
    return tuple(outs)
```

```python
import functools

import jax
import jax.numpy as jnp
from jax import lax
from jax.experimental import pallas as pl
from jax.experimental.pallas import tpu as pltpu

F32, BF16 = jnp.float32, jnp.bfloat16
HI = lax.Precision.HIGHEST
MESH = pl.DeviceIdType.MESH

D_MODEL = 1024
DEPTH = 4
ATT_KV_HEADS, ATT_GROUP, ATT_HEAD_DIM, WINDOW = 4, 4, 64, 128
DN_HEADS, DN_HEAD_DIM, CONV_K, CHUNK = 8, 128, 4, 64
D_FF = 4 * D_MODEL
D_IN = 7696
ALPHA = (2 * DEPTH) ** 0.25
LN_EPS = 1e-5
RMS_EPS = 1e-6
ADAM_LR, ADAM_B1, ADAM_B2, ADAM_EPS, ADAM_WD, ADAM_STEP = 0.001, 0.9, 0.999, 1e-08, 0.01, 10

N_CHIPS = 4
N_DEV = 8
LANES = 128
D_IN_P = 8192
C_Q, C_DQ, C_DK, C_DV, C_Z, C_GA, C_GB, C_K, C_V, C_BA = 0, 1024, 2048, 3072, 4096, 5120, 6144, 7168, 7424, 7680
NEG = -1e30
VMEM_LIMIT = 56 << 20


def _pc(body, **kw):
    return pl.pallas_call(body, **kw)


def _cparams(sem=None):
    if sem is None:
        return pltpu.CompilerParams(vmem_limit_bytes=VMEM_LIMIT)
    return pltpu.CompilerParams(vmem_limit_bytes=VMEM_LIMIT, dimension_semantics=sem)


_MM_VMEM_BUDGET = 44 << 20
_MM_MIN_TILE = 256


def _mm_tiles(m, n, k, out_bytes):
    def halvings(d):
        out = [d]
        while out[-1] % 2 == 0 and out[-1] // 2 >= _MM_MIN_TILE:
            out.append(out[-1] // 2)
        return out

    best = None
    for tm in halvings(m):
        for tn in halvings(n):
            if 2 * (2 * tm * k + 2 * tn * k + out_bytes * tm * tn) > _MM_VMEM_BUDGET:
                continue
            cost = (2 * m * k + (m // tm) * 2 * n * k, (m // tm) * (n // tn))
            if best is None or cost < best[0]:
                best = (cost, tm, tn)
    assert best is not None, (m, n, k)
    return best[1], best[2]


def _mm(a, b, mode, out_dtype, name, tm=None, tn=None, exchange=None):
    if mode == "nn":
        (m, k), (_, n) = a.shape, b.shape
        dims = (((1,), (0,)), ((), ()))
    elif mode == "nt":
        (m, k), (n, _) = a.shape, b.shape
        dims = (((1,), (1,)), ((), ()))
    else:
        (k, m), (_, n) = a.shape, b.shape
        dims = (((0,), (0,)), ((), ()))
    if tm is None:
        tm, tn = _mm_tiles(m, n, k, jnp.dtype(out_dtype).itemsize)
    tm, tn = min(tm, m), min(tn, n)
    assert m % tm == 0 and n % tn == 0, (name, m, n, tm, tn)
    a_spec = pl.BlockSpec((k, tm), lambda i, j: (0, i)) if mode == "tn" else pl.BlockSpec((tm, k), lambda i, j: (i, 0))
    b_spec = pl.BlockSpec((tn, k), lambda i, j: (j, 0)) if mode == "nt" else pl.BlockSpec((k, tn), lambda i, j: (0, j))

    def body(a_ref, b_ref, o_ref):
        o_ref[...] = lax.dot_general(a_ref[...], b_ref[...], dims, preferred_element_type=F32).astype(o_ref.dtype)

    out_spec = pl.BlockSpec((tm, tn), lambda i, j: (i, j))
    out_shape = jax.ShapeDtypeStruct((m, n), out_dtype)
    if exchange is None:
        return _pc(body, name=name, grid=(m // tm, n // tn), in_specs=[a_spec, b_spec], out_specs=out_spec,
                   out_shape=out_shape, compiler_params=_cparams())(a, b)
    (out,), exchanged = _call_with_exchange(body, name, (m // tm, n // tn), [a_spec, b_spec], [out_spec], [out_shape],
                                            [], (a, b), exchange)
    return out, exchanged


def _row_specs(rows, tile):
    return [pl.BlockSpec((tile, w), functools.partial(lambda i, cb: (i, cb), cb=cb)) for (_, cb, w) in rows]


def _vec_specs(vecs):
    return [pl.BlockSpec(v.shape, lambda i: (0, 0)) for v in vecs]


def _rowwise(fn, rows, vecs, outs, name, tile=256):
    n = rows[0][0].shape[0]
    tile = min(tile, n)
    nr, nv = len(rows), len(vecs)

    def body(*refs):
        rv = [r[...].astype(F32) for r in refs[:nr]]
        vv = [r[...] for r in refs[nr:nr + nv]]
        for o_ref, val in zip(refs[nr + nv:], fn(*rv, *vv)):
            o_ref[...] = val.astype(o_ref.dtype)

    res = _pc(body, name=name, grid=(n // tile,), in_specs=_row_specs(rows, tile) + _vec_specs(vecs),
              out_specs=[pl.BlockSpec((tile, w), lambda i: (i, 0)) for (w, _) in outs],
              out_shape=[jax.ShapeDtypeStruct((n, w), dt) for (w, dt) in outs],
              compiler_params=_cparams())(*[r[0] for r in rows], *vecs)
    return res


def _rowwise_bwd(fn, rows, vecs, cts, row_dtypes, name, tile=256, add=None):
    n = rows[0][0].shape[0]
    tile = min(tile, n)
    nr, nv, nc = len(rows), len(vecs), len(cts)
    want = [i for i, dt in enumerate(row_dtypes) if dt is not None]
    n_add = 0 if add is None else 1

    def body(*refs):
        rv = [r[...].astype(F32) for r in refs[:nr]]
        vv = [r[...] for r in refs[nr:nr + nv]]
        cv = [r[...].astype(F32) for r in refs[nr + nv:nr + nv + nc]]
        pos = nr + nv + nc
        add_ref = refs[pos] if n_add else None
        pos += n_add
        row_out = refs[pos:pos + len(want)]
        vec_out = refs[pos + len(want):]
        _, vjp = jax.vjp(fn, *rv, *vv)
        grads = vjp(tuple(cv))
        for o_ref, i in zip(row_out, want):
            gval = grads[i]
            if n_add and add[0] == i:
                gval = gval + add_ref[...]
            o_ref[...] = gval.astype(o_ref.dtype)

        @pl.when(pl.program_id(0) == 0)
        def _():
            for o_ref in vec_out:
                o_ref[...] = jnp.zeros_like(o_ref)

        for o_ref, gval in zip(vec_out, grads[nr:]):
            o_ref[...] += gval

    ct_rows = [(c, 0, c.shape[1]) for c in cts]
    add_rows = [(add[1], 0, add[1].shape[1])] if n_add else []
    res = _pc(body, name=name, grid=(n // tile,),
              in_specs=_row_specs(rows, tile) + _vec_specs(vecs) + _row_specs(ct_rows + add_rows, tile),
              out_specs=[pl.BlockSpec((tile, rows[i][2]), lambda i_: (i_, 0)) for i in want] + _vec_specs(vecs),
              out_shape=[jax.ShapeDtypeStruct((n, rows[i][2]), row_dtypes[i]) for i in want]
              + [jax.ShapeDtypeStruct(v.shape, F32) for v in vecs],
              compiler_params=_cparams(("arbitrary",)))(*[r[0] for r in rows], *vecs, *cts, *[a[0] for a in add_rows])
    return res[:len(want)], res[len(want):]


def _whole(a, cb=0, w=None):
    return (a, cb, a.shape[1] if w is None else w)


def _ln(x, g, b):
    mu = jnp.mean(x, axis=-1, keepdims=True)
    var = jnp.mean(jnp.square(x - mu), axis=-1, keepdims=True)
    return (x - mu) * lax.rsqrt(var + LN_EPS) * g + b


def _silu(x):
    return x * jax.nn.sigmoid(x)


def _softplus(x):
    return jnp.maximum(x, 0.0) + jnp.log(1.0 + jnp.exp(-jnp.abs(x)))


def _f_mod(x, sc, sh):
    return (x * (1.0 + sc) + sh,)


def _f_gate(ga, gb, ya, yb):
    return (jax.nn.sigmoid(ga) * ya + jax.nn.sigmoid(gb) * yb,)


def _f_post1(x, mixed, gt, g1, b1, sc2, sh2):
    x1 = _ln(ALPHA * x + (1.0 + gt) * mixed, g1, b1)
    return x1, x1 * (1.0 + sc2) + sh2


def _f_act(hpre, b):
    return (jnp.square(jnp.maximum(hpre + b, 0.0)),)


def _f_post2(x1, ff, gt, bff2, g2, b2):
    return (_ln(ALPHA * x1 + (1.0 + gt) * (ff + bff2), g2, b2),)


def _attn_valid(n):
    qi = lax.broadcasted_iota(jnp.int32, (WINDOW, 2 * WINDOW), 0)
    si = lax.broadcasted_iota(jnp.int32, (WINDOW, 2 * WINDOW), 1)
    diff = qi + WINDOW - si
    return (diff >= 0) & (diff < WINDOW) & (n * WINDOW + si - WINDOW >= 0)


def _attn_block(qs, kp, kc, vp, vc, sinks, valid):
    kband = jnp.concatenate([kp, kc], axis=0).astype(BF16)
    vband = jnp.concatenate([vp, vc], axis=0).astype(BF16)
    rng = range(len(qs))
    s = [lax.dot_general(qs[g].astype(BF16), kband, (((1,), (1,)), ((), ())), preferred_element_type=F32) for g in rng]
    s = [jnp.where(valid, s[g] * (ATT_HEAD_DIM ** -0.5), NEG) for g in rng]
    m = [lax.stop_gradient(jnp.maximum(jnp.max(s[g], axis=-1, keepdims=True), sinks[g])) for g in rng]
    p = [jnp.exp(s[g] - m[g]) for g in rng]
    denom = [jnp.sum(p[g], axis=-1, keepdims=True) + jnp.exp(sinks[g] - m[g]) for g in rng]
    probs = [(p[g] / denom[g]).astype(BF16) for g in rng]
    return [jnp.dot(probs[g], vband, preferred_element_type=F32) for g in rng]


def _attn_specs(s):
    nb = s // WINDOW
    q_spec = pl.BlockSpec((1, ATT_GROUP, WINDOW, ATT_HEAD_DIM), lambda h, n: (h, 0, n, 0))
    prev = pl.BlockSpec((1, WINDOW, ATT_HEAD_DIM), lambda h, n: (h, jnp.maximum(n - 1, 0), 0))
    cur = pl.BlockSpec((1, WINDOW, ATT_HEAD_DIM), lambda h, n: (h, n, 0))
    sk = pl.BlockSpec((1, ATT_GROUP, 1, 1), lambda h, n: (h, 0, 0, 0))
    return nb, q_spec, prev, cur, sk


def _attn_fwd(qh, kh, vh, sinks4, exchange=None):
    s = qh.shape[2]
    nb, q_spec, prev, cur, sk = _attn_specs(s)

    def body(q_ref, kp_ref, kc_ref, vp_ref, vc_ref, sk_ref, o_ref):
        valid = _attn_valid(pl.program_id(1))
        heads = range(ATT_GROUP)
        o = _attn_block([q_ref[0, g] for g in heads], kp_ref[0], kc_ref[0], vp_ref[0], vc_ref[0],
                        [sk_ref[0, g] for g in heads], valid)
        for g in heads:
            o_ref[0, g] = o[g].astype(o_ref.dtype)

    (o,), exchanged = _call_with_exchange(
        body, "attn_fwd", (ATT_KV_HEADS, nb), [q_spec, prev, cur, prev, cur, sk], [q_spec],
        [jax.ShapeDtypeStruct(qh.shape, BF16)], [], (qh, kh, kh, vh, vh, sinks4), exchange)
    return o, exchanged


def _attn_bwd(qh, kh, vh, sinks4, doh):
    s = qh.shape[2]
    nb, q_spec, prev, cur, sk = _attn_specs(s)
    acc = pl.BlockSpec((1, s + WINDOW, ATT_HEAD_DIM), lambda h, n: (h, 0, 0))

    def body(q_ref, kp_ref, kc_ref, vp_ref, vc_ref, sk_ref, do_ref, dq_ref, dk_ref, dv_ref, dsk_ref):
        n = pl.program_id(1)
        valid = _attn_valid(n)
        fn = functools.partial(_attn_block, valid=valid)
        heads = range(ATT_GROUP)
        _, vjp = jax.vjp(fn, [q_ref[0, g] for g in heads], kp_ref[0], kc_ref[0], vp_ref[0], vc_ref[0],
                         [sk_ref[0, g] for g in heads])
        dq, dkp, dkc, dvp, dvc, dsk = vjp([do_ref[0, g].astype(F32) for g in heads])
        for g in heads:
            dq_ref[0, g] = dq[g]

        @pl.when(n == 0)
        def _():
            dk_ref[...] = jnp.zeros_like(dk_ref)
            dv_ref[...] = jnp.zeros_like(dv_ref)
            dsk_ref[...] = jnp.zeros_like(dsk_ref)

        band = pl.ds(pl.multiple_of(n * WINDOW, WINDOW), 2 * WINDOW)
        dk_ref[0, band, :] += jnp.concatenate([dkp, dkc], axis=0)
        dv_ref[0, band, :] += jnp.concatenate([dvp, dvc], axis=0)
        for g in heads:
            dsk_ref[0, g] += dsk[g]

    kv_shape = jax.ShapeDtypeStruct((ATT_KV_HEADS, s + WINDOW, ATT_HEAD_DIM), F32)
    return _pc(body, name="attn_bwd", grid=(ATT_KV_HEADS, nb), in_specs=[q_spec, prev, cur, prev, cur, sk, q_spec],
               out_specs=[q_spec, acc, acc, sk],
               out_shape=[jax.ShapeDtypeStruct(qh.shape, F32), kv_shape, kv_shape, jax.ShapeDtypeStruct(sinks4.shape, F32)],
               compiler_params=_cparams(("arbitrary", "arbitrary")))(qh, kh, kh, vh, vh, sinks4, doh)


def _bdot(a, b, dims=(((1,), (0,)), ((), ()))):
    return lax.dot_general(a.astype(BF16), b.astype(BF16), dims, preferred_element_type=F32)


def _hdot(a, b, dims=(((1,), (0,)), ((), ()))):
    return lax.dot_general(a, b, dims, precision=HI, preferred_element_type=F32)


_NN = (((1,), (0,)), ((), ()))
_NT = (((1,), (1,)), ((), ()))
_TN = (((0,), (0,)), ((), ()))


def _split2(a):
    hi = a.astype(BF16)
    return hi, (a - hi.astype(F32)).astype(BF16)


def _dot3(a, b, dims):
    ah, al = _split2(a)
    bh, bl = _split2(b)
    d = lambda p, q: lax.dot_general(p, q, dims, preferred_element_type=F32)
    return d(ah, bh) + (d(ah, bl) + d(al, bh))


@jax.custom_vjp
def _xdot(a, b):
    return _dot3(a, b, _NN)


def _xdot_fwd(a, b):
    return _dot3(a, b, _NN), (a, b)


def _xdot_bwd(res, g):
    a, b = res
    return _dot3(g, b, _NT), _dot3(a, g, _TN)


_xdot.defvjp(_xdot_fwd, _xdot_bwd)


def _mask_dot(mask16, b, dims):
    hi = b.astype(BF16)
    r = b - hi.astype(F32)
    mid = r.astype(BF16)
    lo = (r - mid.astype(F32)).astype(BF16)
    d = lambda q: lax.dot_general(mask16, q, dims, preferred_element_type=F32)
    return d(hi) + (d(mid) + d(lo))


def _chunk_masks():
    r = lax.broadcasted_iota(jnp.int32, (CHUNK, CHUNK), 0)
    c = lax.broadcasted_iota(jnp.int32, (CHUNK, CHUNK), 1)
    return r >= c, r > c, (r == c).astype(F32)


def _dn_local(qs, ks, vs, bs, gs, masks):
    causal, strict, eye = masks
    rng = range(len(qs))
    gb = [jnp.broadcast_to(gs[i], (CHUNK, CHUNK)) for i in rng]
    decay = [jnp.exp(jnp.where(causal, gb[i] - gb[i].T, NEG)) for i in rng]
    kb = [ks[i] * bs[i] for i in rng]
    vb = [vs[i] * bs[i] for i in rng]
    kk = [_bdot(kb[i], ks[i], _NT) for i in rng]
    p = [-jnp.where(strict, kk[i] * decay[i], 0.0) for i in rng]
    t = [eye + p[i] for i in rng]
    for _ in range(5):
        p = [_xdot(p[i], p[i]) for i in rng]
        t = [t[i] + _xdot(p[i], t[i]) for i in rng]
    eg = [jnp.exp(gs[i]) for i in rng]
    u = [_xdot(t[i], vb[i]) for i in rng]
    w = [_xdot(t[i], kb[i] * eg[i]) for i in rng]
    qk = [_bdot(qs[i], ks[i], _NT) for i in rng]
    intra = [qk[i] * decay[i] for i in rng]
    q_dec = [qs[i] * eg[i] for i in rng]
    k_dec = [ks[i] * jnp.exp(gs[i][CHUNK - 1:CHUNK, :] - gs[i]) for i in rng]
    return u, w, intra, q_dec, k_dec


def _dn_state_bwd(u, w, intra, q_dec, k_dec, gcum, state, do, dnext):
    last = jnp.exp(gcum[CHUNK - 1:CHUNK, :])
    x = _bdot(k_dec, dnext)
    t1 = _bdot(intra, do, _TN)
    v_new = u - _bdot(w, state)
    dqd = _bdot(do, state, _NT)
    din = _bdot(do, v_new, _NT)
    dkd = _bdot(v_new, dnext, _NT)
    base = _bdot(q_dec, do, _TN) - _bdot(w, t1, _TN)
    d_vnew = t1 + x
    dw = -_bdot(d_vnew, state, _NT)
    dstate = dnext * last + base - _bdot(w, x, _TN)
    dlast = jnp.sum(jnp.sum(state * dnext, axis=1, keepdims=True), axis=0, keepdims=True)
    row = lax.broadcasted_iota(jnp.int32, (CHUNK, 1), 0)
    dgc = jnp.where(row == CHUNK - 1, dlast * last, 0.0)
    return d_vnew, dw, din, dqd, dkd, dgc, dstate


def _l2norm(t):
    return t * lax.rsqrt(jnp.sum(jnp.square(t), axis=-1, keepdims=True) + RMS_EPS)


def _dn_pre(aq, ak, av, ba, alog, dtb, h):
    lane = lax.broadcasted_iota(jnp.int32, (1, LANES), 1)
    pick = lambda t, i: jnp.sum(jnp.where(lane == i, t, 0.0), axis=1, keepdims=True)
    q = _l2norm(_silu(aq)) * (DN_HEAD_DIM ** -0.5)
    k = _l2norm(_silu(ak))
    v = _silu(av)
    beta = jax.nn.sigmoid(pick(ba, h))
    g = -jnp.exp(pick(alog, h)) * _softplus(pick(ba, h + DN_HEADS) + pick(dtb, h))
    return q, k, v, beta, g


def _dn_post(o, z, nw):
    o = o * lax.rsqrt(jnp.mean(jnp.square(o), axis=-1, keepdims=True) + RMS_EPS) * nw
    return o * _silu(z)


_PAD = 8
_TOK_TILE = 512


def _pad_front(pad_ref, x_ref, s):
    pad_ref[pl.ds(0, _PAD), :] = jnp.zeros((_PAD, pad_ref.shape[1]), F32)
    pad_ref[pl.ds(_PAD, s), :] = x_ref[...]


def _conv_tile(pad_ref, w4, r0, n):
    acc = None
    for j in range(CONV_K):
        term = pad_ref[pl.ds(r0 + _PAD - (CONV_K - 1) + j, n), :] * w4[j:j + 1, :]
        acc = term if acc is None else acc + term
    return acc


def _conv_tile_bwd(pad_ref, da_ref, w4, r0, n):
    dx, dw = None, []
    da = da_ref[pl.ds(r0, n), :]
    for j in range(CONV_K):
        term = da_ref[pl.ds(r0 + CONV_K - 1 - j, n), :] * w4[j:j + 1, :]
        dx = term if dx is None else dx + term
        dw.append(jnp.sum(da * pad_ref[pl.ds(r0 + _PAD - (CONV_K - 1) + j, n), :], axis=0, keepdims=True))
    return dx, jnp.concatenate(dw, axis=0)


def _dn_gcum(g_c, causal_f):
    return _mask_dot(causal_f, jnp.broadcast_to(g_c, (CHUNK, LANES)), _NN)[:, 0:1]


def _dn_in_specs(s):
    col = lambda base: pl.BlockSpec((s, DN_HEAD_DIM), functools.partial(lambda h, b: (0, b + h), b=base // DN_HEAD_DIM))
    cw = lambda base: pl.BlockSpec((CONV_K, DN_HEAD_DIM), functools.partial(lambda h, b: (0, b + h), b=base))
    row = pl.BlockSpec((1, LANES), lambda h: (0, 0))
    ba = pl.BlockSpec((s, LANES), lambda h: (0, C_BA // LANES))
    return [col(C_DQ), col(C_DK), col(C_DV), col(C_Z), ba, cw(0), cw(DN_HEADS), cw(2 * DN_HEADS), row, row, row]


def _chunk_rows(c):
    return pl.ds(pl.multiple_of(c * CHUNK, CHUNK), CHUNK)


def _group(nchunk, want):
    g = min(want, nchunk)
    assert nchunk % g == 0
    return g


def _dn_forward_scan(q_s, k_s, v_s, b_s, g_s, gc_s, loc, o_s, states_ref, s):
    masks = _chunk_masks()
    causal_f = masks[0].astype(BF16)
    nchunk = s // CHUNK
    grp = _group(nchunk, 8)
    u_s, w_s, in_s, qd_s, kd_s = loc

    def local_step(i, carry):
        rows = [_chunk_rows(i * grp + j) for j in range(grp)]
        gcum = [_dn_gcum(g_s[r, :], causal_f) for r in rows]
        u, w, intra, q_dec, k_dec = _dn_local([q_s[r, :] for r in rows], [k_s[r, :] for r in rows],
                                              [v_s[r, :] for r in rows], [b_s[r, :] for r in rows], gcum, masks)
        for j, r in enumerate(rows):
            gc_s[r, :] = gcum[j]
            u_s[r, :] = u[j]
            w_s[r, :] = w[j].astype(w_s.dtype)
            in_s[r, :] = intra[j].astype(in_s.dtype)
            qd_s[r, :] = q_dec[j].astype(qd_s.dtype)
            kd_s[r, :] = k_dec[j].astype(kd_s.dtype)
        return carry

    lax.fori_loop(0, nchunk // grp, local_step, 0)

    def state_step(i, state):
        rows = _chunk_rows(i)
        states_ref[i] = state
        v_new = u_s[rows, :] - _bdot(w_s[rows, :], state)
        o_s[rows, :] = _bdot(qd_s[rows, :], state) + _bdot(in_s[rows, :], v_new)
        last = jnp.exp(gc_s[rows, :][CHUNK - 1:CHUNK, :])
        return state * last + _bdot(kd_s[rows, :], v_new, _TN)

    lax.fori_loop(0, nchunk, state_step, jnp.zeros((DN_HEAD_DIM, DN_HEAD_DIM), F32))


def _dn_saved_shapes(s):
    d, h = DN_HEAD_DIM, DN_HEADS
    shapes = [((h, s, d), F32), ((h, s, d), BF16), ((h, s, CHUNK), BF16), ((h, s, d), BF16), ((h, s, d), BF16),
              ((h, s, 1), F32), ((h, s // CHUNK, d, d), F32), ((h, s, d), F32)]
    return [jax.ShapeDtypeStruct(shp, dt) for shp, dt in shapes]


def _dn_saved_specs(s, **kw):
    return [pl.BlockSpec((1,) + t.shape[1:], functools.partial(lambda h, nd: (h,) + (0,) * nd, nd=len(t.shape) - 1), **kw)
            for t in _dn_saved_shapes(s)]


def _call_with_exchange(body, name, steps, in_specs, out_specs, out_shape, scratch, args, exchange):
    steps = (steps,) if isinstance(steps, int) else tuple(steps)
    params = _cparams(("arbitrary",) * len(steps))
    if exchange is None:
        res = _pc(body, name=name, grid=steps, in_specs=in_specs, out_specs=out_specs, out_shape=out_shape,
                  scratch_shapes=scratch, compiler_params=params)(*args)
        return res, None
    arrays, gather = exchange
    x_in, x_out, x_shape, x_scratch = _exchange_specs(arrays, gather)
    wrapped = _carry_exchange(body, len(in_specs), len(out_specs), len(scratch), len(arrays), gather, steps)
    res = _pc(wrapped, name=name + "_x", grid=steps, in_specs=in_specs + x_in, out_specs=out_specs + x_out,
              out_shape=out_shape + x_shape, scratch_shapes=scratch + x_scratch, compiler_params=params)(*args, *arrays)
    return res[:len(out_specs)], res[len(out_specs):]


def _dn_fwd(proj, conv_w, alog, dtb, nw, exchange=None):
    s = proj.shape[0]
    d = DN_HEAD_DIM

    tt = min(_TOK_TILE, s)

    def body(xq, xk, xv, z, ba, wq, wk, wv, alog_r, dtb_r, nw_r, o_ref, u_o, w_o, in_o, qd_o, kd_o, gc_o, st_o, oraw_o,
             padq, padk, padv, q_s, k_s, v_s, b_s, g_s):
        h = pl.program_id(0)
        loc = [r.at[0] for r in (u_o, w_o, in_o, qd_o, kd_o)]
        gc_s, states, o_s = gc_o.at[0], st_o.at[0], oraw_o.at[0]
        _pad_front(padq, xq, s)
        _pad_front(padk, xk, s)
        _pad_front(padv, xv, s)
        for r0 in range(0, s, tt):
            rows = pl.ds(r0, tt)
            aq, ak, av = _conv_tile(padq, wq[...], r0, tt), _conv_tile(padk, wk[...], r0, tt), _conv_tile(padv, wv[...], r0, tt)
            q_s[rows, :], k_s[rows, :], v_s[rows, :], b_s[rows, :], g_s[rows, :] = _dn_pre(
                aq, ak, av, ba[rows, :], alog_r[...], dtb_r[...], h)
        _dn_forward_scan(q_s, k_s, v_s, b_s, g_s, gc_s, loc, o_s, states, s)
        for r0 in range(0, s, tt):
            rows = pl.ds(r0, tt)
            o_ref[rows, :] = _dn_post(o_s[rows, :], z[rows, :], nw_r[...]).astype(o_ref.dtype)

    big = pltpu.VMEM((s, d), F32)
    thin = pltpu.VMEM((s, 1), F32)
    padded = pltpu.VMEM((s + _PAD, d), F32)
    return _call_with_exchange(
        body, "dn_fwd", DN_HEADS, _dn_in_specs(s), [pl.BlockSpec((s, d), lambda h: (0, h))] + _dn_saved_specs(s),
        [jax.ShapeDtypeStruct((s, DN_HEADS * d), BF16)] + _dn_saved_shapes(s),
        [padded, padded, padded, big, big, big, thin, thin],
        (proj, proj, proj, proj, proj, conv_w, conv_w, conv_w, alog, dtb, nw), exchange)


def _dn_bwd(proj, conv_w, alog, dtb, nw, dob, kept, exchange=None):
    s = proj.shape[0]
    d = DN_HEAD_DIM
    nchunk = s // CHUNK

    tt = min(_TOK_TILE, s)

    def body(xq, xk, xv, z, ba, wq, wk, wv, alog_r, dtb_r, nw_r, dob_ref, u_i, w_i, in_i, qd_i, kd_i, gc_i, st_i, oraw_i,
             dxq, dxk, dxv, dz, dba, dwq, dwk, dwv, dalog, ddtb, dnw,
             padq, padk, padv, q_s, k_s, v_s, b_s, g_s, o_s, dq_s, dk_s, dv_s, db_s, dg_s, dkd_s, din_s, dgc_s):
        h = pl.program_id(0)
        u_s, w_s, in_s, qd_s, kd_s = [r.at[0] for r in (u_i, w_i, in_i, qd_i, kd_i)]
        gc_s, states, oraw = gc_i.at[0], st_i.at[0], oraw_i.at[0]
        masks = _chunk_masks()
        causal_f = masks[0].astype(BF16)
        pre = functools.partial(_dn_pre, h=h)
        _pad_front(padq, xq, s)
        _pad_front(padk, xk, s)
        _pad_front(padv, xv, s)

        def conv_tiles(r0):
            return _conv_tile(padq, wq[...], r0, tt), _conv_tile(padk, wk[...], r0, tt), _conv_tile(padv, wv[...], r0, tt)

        for r0 in range(0, s, tt):
            rows = pl.ds(r0, tt)
            q_s[rows, :], k_s[rows, :], v_s[rows, :], b_s[rows, :], g_s[rows, :] = pre(
                *conv_tiles(r0), ba[rows, :], alog_r[...], dtb_r[...])
        dnw_v = jnp.zeros((1, LANES), F32)
        for r0 in range(0, s, tt):
            rows = pl.ds(r0, tt)
            _, post_vjp = jax.vjp(_dn_post, oraw[rows, :], z[rows, :], nw_r[...])
            do_raw, dz_v, dnw_t = post_vjp(dob_ref[rows, :].astype(F32))
            dz[rows, :] = dz_v.astype(dz.dtype)
            o_s[rows, :] = do_raw
            dnw_v = dnw_v + dnw_t

        def state_step(i, dstate):
            c = nchunk - 1 - i
            rows = _chunk_rows(c)
            du, dw, din, dqd, dkd, dgc, dstate = _dn_state_bwd(
                u_s[rows, :], w_s[rows, :], in_s[rows, :], qd_s[rows, :], kd_s[rows, :], gc_s[rows, :], states[c],
                o_s[rows, :], dstate)
            dq_s[rows, :] = du
            dk_s[rows, :] = dw
            dv_s[rows, :] = dqd
            dkd_s[rows, :] = dkd
            din_s[rows, :] = din
            dgc_s[rows, :] = dgc
            return dstate

        lax.fori_loop(0, nchunk, state_step, jnp.zeros((d, d), F32))
        local = functools.partial(_dn_local, masks=masks)
        grp = _group(nchunk, 8)

        def local_step(i, carry):
            rows = [_chunk_rows(i * grp + j) for j in range(grp)]
            get = lambda ref: [ref[r, :] for r in rows]
            _, vjp = jax.vjp(local, get(q_s), get(k_s), get(v_s), get(b_s), get(gc_s))
            dq_c, dk_c, dv_c, db_c, dgc_c = vjp((get(dq_s), get(dk_s), get(din_s), get(dv_s), get(dkd_s)))
            dgc_c = [dgc_c[j] + dgc_s[r, :] for j, r in enumerate(rows)]
            dg_c = [_mask_dot(causal_f, jnp.broadcast_to(t, (CHUNK, LANES)), _TN)[:, 0:1] for t in dgc_c]
            for j, r in enumerate(rows):
                dq_s[r, :] = dq_c[j]
                dk_s[r, :] = dk_c[j]
                dv_s[r, :] = dv_c[j]
                db_s[r, :] = db_c[j]
                dg_s[r, :] = dg_c[j]
            return carry

        lax.fori_loop(0, nchunk // grp, local_step, 0)

        @pl.when(h == 0)
        def _():
            dba[...] = jnp.zeros_like(dba)
            dalog[...] = jnp.zeros_like(dalog)
            ddtb[...] = jnp.zeros_like(ddtb)
            dnw[...] = jnp.zeros_like(dnw)

        dalog_v = jnp.zeros((1, LANES), F32)
        ddtb_v = jnp.zeros((1, LANES), F32)
        for r0 in range(0, s, tt):
            rows = pl.ds(r0, tt)
            _, pre_vjp = jax.vjp(pre, *conv_tiles(r0), ba[rows, :], alog_r[...], dtb_r[...])
            daq, dak, dav, dba_t, dalog_t, ddtb_t = pre_vjp(
                (dq_s[rows, :], dk_s[rows, :], dv_s[rows, :], db_s[rows, :], dg_s[rows, :]))
            dq_s[rows, :], dk_s[rows, :], dv_s[rows, :] = daq, dak, dav
            dba[rows, :] += dba_t
            dalog_v = dalog_v + dalog_t
            ddtb_v = ddtb_v + ddtb_t
        tail = pl.ds(s, _PAD)
        dq_s[tail, :] = dk_s[tail, :] = dv_s[tail, :] = jnp.zeros((_PAD, d), F32)
        for pad, da_s, w_ref, dx_ref, dw_ref in ((padq, dq_s, wq, dxq, dwq), (padk, dk_s, wk, dxk, dwk), (padv, dv_s, wv, dxv, dwv)):
            dw_acc = jnp.zeros((CONV_K, d), F32)
            for r0 in range(0, s, tt):
                dx_t, dw_t = _conv_tile_bwd(pad, da_s, w_ref[...], r0, tt)
                dx_ref[pl.ds(r0, tt), :] = dx_t.astype(dx_ref.dtype)
                dw_acc = dw_acc + dw_t
            dw_ref[...] = dw_acc
        dalog[...] += dalog_v
        ddtb[...] += ddtb_v
        dnw[...] += dnw_v

    big = pltpu.VMEM((s, d), F32)
    thin = pltpu.VMEM((s, 1), F32)
    padded = pltpu.VMEM((s + _PAD, d), F32)
    w_all = DN_HEADS * d
    col_out = lambda: pl.BlockSpec((s, d), lambda h: (0, h))
    cw_out = lambda: pl.BlockSpec((CONV_K, d), lambda h: (0, h))
    row = lambda: pl.BlockSpec((1, LANES), lambda h: (0, 0))
    big_out = jax.ShapeDtypeStruct((s, w_all), BF16)
    cw_shape = jax.ShapeDtypeStruct((CONV_K, w_all), F32)
    row_shape = jax.ShapeDtypeStruct((1, LANES), F32)
    return _call_with_exchange(
        body, "dn_bwd", DN_HEADS,
        _dn_in_specs(s) + [pl.BlockSpec((s, d), lambda h: (0, h))] + _dn_saved_specs(s, pipeline_mode=pl.Buffered(1)),
        [col_out(), col_out(), col_out(), col_out(), pl.BlockSpec((s, LANES), lambda h: (0, 0)),
         cw_out(), cw_out(), cw_out(), row(), row(), row()],
        [big_out, big_out, big_out, big_out, jax.ShapeDtypeStruct((s, LANES), F32),
         cw_shape, cw_shape, cw_shape, row_shape, row_shape, row_shape],
        [padded, padded, padded, big, big, big, thin, thin, big,
         padded, padded, padded, thin, thin, big, pltpu.VMEM((s, CHUNK), F32), thin],
        (proj, proj, proj, proj, proj, conv_w, conv_w, conv_w, alog, dtb, nw, dob, *kept), exchange)


def _loss_head(y, target, tile=256):
    n, dm = y.shape
    tile = min(tile, n)

    def body(y_ref, t_ref, dy_ref, loss_ref):
        err = y_ref[...] - t_ref[...]
        dy_ref[...] = err * (1.0 / dm)

        @pl.when(pl.program_id(0) == 0)
        def _():
            loss_ref[...] = jnp.zeros_like(loss_ref)

        loss_ref[...] += 0.5 * jnp.sum(jnp.mean(jnp.square(err), axis=-1, keepdims=True), axis=0, keepdims=True)

    blk = pl.BlockSpec((tile, dm), lambda i: (i, 0))
    return _pc(body, name="loss_head", grid=(n // tile,), in_specs=[blk, blk],
               out_specs=[blk, pl.BlockSpec((1, 1), lambda i: (0, 0))],
               out_shape=[jax.ShapeDtypeStruct((n, dm), F32), jax.ShapeDtypeStruct((1, 1), F32)],
               compiler_params=_cparams(("arbitrary",)))(y, target)


def _ada_fwd(c_all, w_ada, b_shard):
    nl, dm, n = w_ada.shape

    def body(c_ref, w_ref, b_ref, o_ref):
        ca = _silu(c_ref[...]).astype(BF16)
        o_ref[0] = jnp.dot(ca, w_ref[0].astype(BF16), preferred_element_type=F32) + b_ref[0]

    return _pc(body, name="ada_fwd", grid=(nl,),
               in_specs=[pl.BlockSpec((N_DEV, dm), lambda l: (0, 0)), pl.BlockSpec((1, dm, n), lambda l: (l, 0, 0)),
                         pl.BlockSpec((1, 1, n), lambda l: (l, 0, 0))],
               out_specs=pl.BlockSpec((1, N_DEV, n), lambda l: (l, 0, 0)),
               out_shape=jax.ShapeDtypeStruct((nl, N_DEV, n), F32), compiler_params=_cparams())(c_all, w_ada, b_shard)


def _ada_bwd(c_all, dmod):
    nl, _, n = dmod.shape
    dm = c_all.shape[1]

    def body(c_ref, d_ref, o_ref):
        o_ref[0] = _hdot(_silu(c_ref[...]), d_ref[0], _TN)

    return _pc(body, name="ada_bwd", grid=(nl,),
               in_specs=[pl.BlockSpec((N_DEV, dm), lambda l: (0, 0)), pl.BlockSpec((1, N_DEV, n), lambda l: (l, 0, 0))],
               out_specs=pl.BlockSpec((1, dm, n), lambda l: (l, 0, 0)),
               out_shape=jax.ShapeDtypeStruct((nl, dm, n), F32), compiler_params=_cparams())(c_all, dmod)


def _adamw(g, w, m, v):
    m = ADAM_B1 * m + (1.0 - ADAM_B1) * g
    v = ADAM_B2 * v + (1.0 - ADAM_B2) * jnp.square(g)
    m_hat = m / (1.0 - ADAM_B1 ** ADAM_STEP)
    v_hat = v / (1.0 - ADAM_B2 ** ADAM_STEP)
    delta = -ADAM_LR * (m_hat / (jnp.sqrt(v_hat) + ADAM_EPS) + ADAM_WD * w)
    return delta, m, v


def _adam_call(parts, w, m, v, name, tile=128):
    shape = w.shape
    flat = lambda t: t.reshape(-1, shape[-1])
    width = shape[-1]

    def fn(*vals):
        g = vals[0] if len(parts) == 1 else vals[0] + vals[1]
        return (g,) + _adamw(g, *vals[len(parts):])

    rows = [_whole(flat(t)) for t in (*parts, w, m, v)]
    outs = _rowwise(fn, rows, [], [(width, F32)] * 4, name, tile=tile)
    return [o.reshape(shape) for o in outs]


def _sum_slots(per_layer, name, tile=128):
    nl = len(per_layer)
    _, n, width = per_layer[0].shape
    tile = min(tile, n)
    nt = n // tile

    def body(*refs):
        o_ref = refs[nl]
        for lp in range(nl):
            @pl.when(pl.program_id(0) == lp)
            def _(r_ref=refs[lp]):
                acc = r_ref[0].astype(F32)
                for j in range(1, N_CHIPS):
                    acc = acc + r_ref[j].astype(F32)
                o_ref[...] = acc

    in_specs = [pl.BlockSpec((N_CHIPS, tile, width), functools.partial(lambda l, t, lp: (0, jnp.where(l == lp, t, 0), 0), lp=lp))
                for lp in range(nl)]
    return _pc(body, name=name, grid=(nl, nt), in_specs=in_specs,
               out_specs=pl.BlockSpec((tile, width), lambda l, t: (l * nt + t, 0)),
               out_shape=jax.ShapeDtypeStruct((nl * n, width), F32), compiler_params=_cparams())(*per_layer)


def _small_adam(g_all, w, m, v):
    def body(g_ref, w_ref, m_ref, v_ref, og, od, om, ov):
        g = g_ref[0]
        for j in range(1, N_DEV):
            g = g + g_ref[j]
        og[...] = g
        od[...], om[...], ov[...] = _adamw(g, w_ref[...], m_ref[...], v_ref[...])

    vm = pl.BlockSpec(memory_space=pltpu.VMEM)
    shp = jax.ShapeDtypeStruct(w.shape, F32)
    return _pc(body, name="small_adam", in_specs=[vm] * 4, out_specs=[vm] * 4, out_shape=[shp] * 4,
               compiler_params=_cparams())(g_all, w, m, v)


def _place():
    return lax.axis_index("x"), lax.axis_index("y"), lax.axis_index("c")


def _flip(v, bit):
    return 1 - v if bit else v


def _all_gather8(a):
    r, n = a.shape

    def body(a_ref, o_ref, send_sems, recv_sems):
        x, y, c = _place()
        me = 4 * x + 2 * y + c
        o_ref[me] = a_ref[...]
        copies = []
        for k in range(1, N_DEV):
            peer = (_flip(x, k & 4), _flip(y, k & 2), _flip(c, k & 1))
            copies.append(pltpu.make_async_remote_copy(
                src_ref=a_ref, dst_ref=o_ref.at[me], send_sem=send_sems.at[k - 1], recv_sem=recv_sems.at[k - 1],
                device_id=peer, device_id_type=MESH))
        for cp in copies:
            cp.start()
        for k in range(1, N_DEV):
            px, py, pc_ = _flip(x, k & 4), _flip(y, k & 2), _flip(c, k & 1)
            pltpu.make_async_remote_copy(
                src_ref=a_ref, dst_ref=o_ref.at[4 * px + 2 * py + pc_], send_sem=send_sems.at[k - 1],
                recv_sem=recv_sems.at[k - 1], device_id=(px, py, pc_), device_id_type=MESH).wait_recv()
        for cp in copies:
            cp.wait_send()

    vm = pl.BlockSpec(memory_space=pltpu.VMEM)
    return _pc(body, name="all_gather8", in_specs=[vm], out_specs=vm,
               out_shape=jax.ShapeDtypeStruct((N_DEV, r, n), a.dtype),
               scratch_shapes=[pltpu.SemaphoreType.DMA((N_DEV - 1,)), pltpu.SemaphoreType.DMA((N_DEV - 1,))],
               compiler_params=_cparams())(a)


def _chip_exchange(arrays, gather, name):
    na = len(arrays)

    def body(*refs):
        ins, outs, sems = refs[:na], refs[na:2 * na], refs[2 * na:]
        _exchange_copies(ins, outs, sems, gather, start=True)
        _exchange_copies(ins, outs, sems, gather, start=False)

    in_specs, out_specs, out_shape, scratch = _exchange_specs(arrays, gather)
    return _pc(body, name=name, in_specs=in_specs, out_specs=out_specs, out_shape=out_shape, scratch_shapes=scratch,
               compiler_params=_cparams())(*arrays)


def _exchange_specs(arrays, gather):
    na = len(arrays)
    hbm = pl.BlockSpec(memory_space=pl.ANY)
    out_shape = [jax.ShapeDtypeStruct(((N_CHIPS,) + a.shape) if gather else a.shape, a.dtype) for a in arrays]
    n_remote = 4 if gather else 2
    scratch = [pltpu.SemaphoreType.DMA((3 * na,))] * n_remote + [pltpu.SemaphoreType.DMA((na,))]
    return [hbm] * na, [hbm] * na, out_shape, scratch


def _gather_copies(ins, outs, sems, start):
    send_i, recv_i, send_d, recv_d, local_sems = sems
    x, y, c = _place()
    me = 2 * x + y
    sibling = (x, y, 1 - c)
    ici_sends, ici_arrivals, hand_ons, hand_arrivals, locals_ = [], [], [], [], []
    for i in range(len(ins)):
        half = ins[i].shape[0] // 2
        mine, other = pl.ds(c * half, half), pl.ds((1 - c) * half, half)
        locals_.append(pltpu.make_async_copy(ins[i], outs[i].at[me], local_sems.at[i]))
        for j in range(1, N_CHIPS):
            px, py = _flip(x, j & 2), _flip(y, j & 1)
            peer = 2 * px + py
            k = i * 3 + j - 1
            ici = dict(send_sem=send_i.at[k], recv_sem=recv_i.at[k], device_id=(px, py, c), device_id_type=MESH)
            d2d = dict(send_sem=send_d.at[k], recv_sem=recv_d.at[k], device_id=sibling, device_id_type=MESH)
            ici_sends.append(pltpu.make_async_remote_copy(src_ref=ins[i].at[mine], dst_ref=outs[i].at[me, mine], **ici))
            ici_arrivals.append(pltpu.make_async_remote_copy(src_ref=ins[i].at[mine], dst_ref=outs[i].at[peer, mine], **ici))
            hand_ons.append(pltpu.make_async_remote_copy(
                src_ref=outs[i].at[peer, mine], dst_ref=outs[i].at[peer, mine], **d2d))
            hand_arrivals.append(pltpu.make_async_remote_copy(
                src_ref=outs[i].at[peer, other], dst_ref=outs[i].at[peer, other], **d2d))
    if start:
        for cp in locals_ + ici_sends:
            cp.start()
    else:
        for arrival, hand_on in zip(ici_arrivals, hand_ons):
            arrival.wait_recv()
            hand_on.start()
        for cp in hand_arrivals:
            cp.wait_recv()
        for cp in ici_sends + hand_ons:
            cp.wait_send()
        for cp in locals_:
            cp.wait()


def _exchange_copies(ins, outs, sems, gather, start):
    if gather:
        return _gather_copies(ins, outs, sems, start)
    send_sems, recv_sems, local_sems = sems
    x, y, c = _place()
    me = 2 * x + y
    sends, arrivals, locals_ = [], [], []
    for i in range(len(ins)):
        locals_.append(pltpu.make_async_copy(ins[i] if gather else ins[i].at[me], outs[i].at[me], local_sems.at[i]))
        for j in range(1, N_CHIPS):
            px, py = _flip(x, j & 2), _flip(y, j & 1)
            peer = 2 * px + py
            pair = dict(send_sem=send_sems.at[i * 3 + j - 1], recv_sem=recv_sems.at[i * 3 + j - 1],
                        device_id=(px, py, c), device_id_type=MESH)
            sends.append(pltpu.make_async_remote_copy(
                src_ref=ins[i] if gather else ins[i].at[peer], dst_ref=outs[i].at[me], **pair))
            arrivals.append(pltpu.make_async_remote_copy(
                src_ref=ins[i] if gather else ins[i].at[me], dst_ref=outs[i].at[peer], **pair))
    if start:
        for cp in locals_ + sends:
            cp.start()
    else:
        for cp in arrivals:
            cp.wait_recv()
        for cp in sends:
            cp.wait_send()
        for cp in locals_:
            cp.wait()


def _carry_exchange(body, n_in, n_out, n_scratch, n_arrays, gather, steps):
    def wrapped(*refs):
        na = n_arrays
        ins, xin = refs[:n_in], refs[n_in:n_in + na]
        outs = refs[n_in + na:n_in + na + n_out]
        xout = refs[n_in + na + n_out:n_in + 2 * na + n_out]
        rest = refs[n_in + 2 * na + n_out:]
        scratch, sems = rest[:n_scratch], rest[n_scratch:]
        first, last = True, True
        for axis, n in enumerate(steps):
            first = jnp.logical_and(first, pl.program_id(axis) == 0)
            last = jnp.logical_and(last, pl.program_id(axis) == n - 1)

        @pl.when(first)
        def _():
            _exchange_copies(xin, xout, sems, gather, start=True)

        body(*ins, *outs, *scratch)

        @pl.when(last)
        def _():
            _exchange_copies(xin, xout, sems, gather, start=False)

    return wrapped


def _sibling_exchange(arrays, name):
    na = len(arrays)

    def body(*refs):
        ins, outs = refs[:na], refs[na:2 * na]
        send_sems, recv_sems = refs[2 * na:]
        x, y, c = _place()
        copies = [pltpu.make_async_remote_copy(
            src_ref=ins[i], dst_ref=outs[i], send_sem=send_sems.at[i], recv_sem=recv_sems.at[i],
            device_id=(x, y, 1 - c), device_id_type=MESH) for i in range(na)]
        for cp in copies:
            cp.start()
        for cp in copies:
            cp.wait()

    hbm = pl.BlockSpec(memory_space=pl.ANY)
    return _pc(body, name=name, in_specs=[hbm] * na, out_specs=[hbm] * na,
               out_shape=[jax.ShapeDtypeStruct(a.shape, a.dtype) for a in arrays],
               scratch_shapes=[pltpu.SemaphoreType.DMA((na,)), pltpu.SemaphoreType.DMA((na,))],
               compiler_params=_cparams())(*arrays)


def _heads_q(t):
    s = t.shape[0]
    return t.reshape(s, ATT_KV_HEADS, ATT_GROUP, ATT_HEAD_DIM).transpose(1, 2, 0, 3)


def _unheads_q(t):
    s = t.shape[2]
    return t.transpose(2, 0, 1, 3).reshape(s, ATT_KV_HEADS * ATT_GROUP * ATT_HEAD_DIM)


def _heads_kv(t):
    s = t.shape[0]
    return t.reshape(s, ATT_KV_HEADS, ATT_HEAD_DIM).transpose(1, 0, 2)


def _unheads_kv(t):
    s = t.shape[1]
    return t.transpose(1, 0, 2).reshape(s, ATT_KV_HEADS * ATT_HEAD_DIM)


def _row128(v):
    return jnp.pad(v, (0, LANES - v.shape[0])).reshape(1, LANES)


def _layer_fwd(x, p, fetch=None):
    p = dict(p)
    arrived = {}
    sh1, sc1, gt1, sh2, sc2, gt2 = [p["mod"][i] for i in range(6)]
    (u,) = _rowwise(_f_mod, [_whole(x)], [sc1, sh1], [(D_MODEL, BF16)], "mod1")
    if fetch is None:
        proj = _mm(u, p["w_in"], "nn", F32, "proj")
    else:
        proj, got = _mm(u, p["w_in"], "nn", F32, "proj", exchange=(fetch["o"], True))
        for k, t in zip(("w_oa", "w_ob", "w_out"), got):
            arrived[k] = fetch["assemble"](k, t)
    qh = _heads_q(proj[:, C_Q:C_Q + 1024])
    kh = _heads_kv(proj[:, C_K:C_K + 256])
    vh = _heads_kv(proj[:, C_V:C_V + 256])
    sinks4 = p["sinks"].reshape(ATT_KV_HEADS, ATT_GROUP, 1, 1)
    o_heads, got = _attn_fwd(qh, kh, vh, sinks4, None if fetch is None else (fetch["ff1"], True))
    if fetch is not None:
        arrived["w_ff1"] = fetch["assemble"]("w_ff1", got[0])
    o_a = _unheads_q(o_heads)
    (o_b, *dn_kept), got = _dn_fwd(proj, p["conv_w"], _row128(p["a_log"]), _row128(p["dt_bias"]),
                                   p["dn_norm_w"].reshape(1, LANES),
                                   None if fetch is None else (fetch["ff2"] + fetch["w_in_next"], True))
    if fetch is not None:
        arrived["w_ff2"] = fetch["assemble"]("w_ff2", got[0])
        arrived["w_in_next"] = got[1] if fetch["w_in_next"] else None
    p.update({k: v for k, v in arrived.items() if k != "w_in_next"})
    y_a = _mm(o_a, p["w_oa"], "nn", BF16, "y_a")
    y_b = _mm(o_b, p["w_ob"], "nn", BF16, "y_b")
    (gm,) = _rowwise(_f_gate, [(proj, C_GA // 1024, 1024), (proj, C_GB // 1024, 1024), _whole(y_a), _whole(y_b)], [],
                     [(D_MODEL, BF16)], "gate")
    mixed = _mm(gm, p["w_out"], "nn", BF16, "mixed")
    x1, u2 = _rowwise(_f_post1, [_whole(x), _whole(mixed)], [gt1, p["ln1_g"], p["ln1_b"], sc2, sh2],
                      [(D_MODEL, F32), (D_MODEL, BF16)], "post1")
    hpre = _mm(u2, p["w_ff1"], "nn", BF16, "ff1")
    (h,) = _rowwise(_f_act, [_whole(hpre)], [p["b_ff1"]], [(D_FF, BF16)], "act")
    ff = _mm(h, p["w_ff2"], "nn", BF16, "ff2")
    (x2,) = _rowwise(_f_post2, [_whole(x1), _whole(ff)], [gt2, p["b_ff2"], p["ln2_g"], p["ln2_b"]],
                     [(D_MODEL, F32)], "post2")
    saved = dict(x=x, u=u, proj=proj, o_a=o_a, o_b=o_b, y_a=y_a, y_b=y_b, gm=gm, mixed=mixed, x1=x1, u2=u2,
                 hpre=hpre, h=h, ff=ff, dn_kept=dn_kept, heads=(qh, kh, vh))
    return x2, saved, arrived


def _layer_bwd(dx2, p, sv, carry=None):
    sh1, sc1, gt1, sh2, sc2, gt2 = [p["mod"][i] for i in range(6)]
    g = {}
    (dx1_a, dff), (dgt2, g["b_ff2"], g["ln2_g"], g["ln2_b"]) = _rowwise_bwd(
        _f_post2, [_whole(sv["x1"]), _whole(sv["ff"])], [gt2, p["b_ff2"], p["ln2_g"], p["ln2_b"]], [dx2],
        [F32, BF16], "post2_bwd")
    dh = _mm(dff, p["w_ff2"], "nt", BF16,"dh")
    g["w_ff2"] = _mm(sv["h"], dff, "tn", BF16,"dw_ff2")
    (dhpre,), (g["b_ff1"],) = _rowwise_bwd(_f_act, [_whole(sv["hpre"])], [p["b_ff1"]], [dh], [BF16], "act_bwd")
    du2 = _mm(dhpre, p["w_ff1"], "nt", BF16,"du2")
    g["w_ff1"] = _mm(sv["u2"], dhpre, "tn", BF16,"dw_ff1")
    (dx_a, dmixed), (dgt1, g["ln1_g"], g["ln1_b"], dsc2, dsh2) = _rowwise_bwd(
        _f_post1, [_whole(sv["x"]), _whole(sv["mixed"])], [gt1, p["ln1_g"], p["ln1_b"], sc2, sh2], [dx1_a, du2],
        [F32, BF16], "post1_bwd")
    dgm = _mm(dmixed, p["w_out"], "nt", BF16,"dgm")
    g["w_out"] = _mm(sv["gm"], dmixed, "tn", BF16,"dw_out")
    proj = sv["proj"]
    (dga, dgb, dya, dyb), _ = _rowwise_bwd(
        _f_gate, [(proj, C_GA // 1024, 1024), (proj, C_GB // 1024, 1024), _whole(sv["y_a"]), _whole(sv["y_b"])], [],
        [dgm], [BF16, BF16, BF16, BF16], "gate_bwd")
    do_a = _mm(dya, p["w_oa"], "nt", BF16,"do_a")
    g["w_oa"] = _mm(sv["o_a"], dya, "tn", BF16,"dw_oa")
    do_b = _mm(dyb, p["w_ob"], "nt", BF16,"do_b")
    g["w_ob"] = _mm(sv["o_b"], dyb, "tn", BF16,"dw_ob")
    exchange = None if carry is None else (carry(g), False)
    (ddq, ddk, ddv, ddz, dba, dwq, dwk, dwv, dalog, ddtb, dnw), exchanged = _dn_bwd(
        proj, p["conv_w"], _row128(p["a_log"]), _row128(p["dt_bias"]), p["dn_norm_w"].reshape(1, LANES), do_b,
        sv["dn_kept"], exchange)
    g["conv_w"] = jnp.concatenate([dwq, dwk, dwv], axis=1)
    g["a_log"], g["dt_bias"], g["dn_norm_w"] = dalog[0, :DN_HEADS], ddtb[0, :DN_HEADS], dnw[0]
    qh, kh, vh = sv["heads"]
    sinks4 = p["sinks"].reshape(ATT_KV_HEADS, ATT_GROUP, 1, 1)
    dqh, dkh, dvh, dsk = _attn_bwd(qh, kh, vh, sinks4, _heads_q(do_a))
    g["sinks"] = dsk.reshape(ATT_KV_HEADS * ATT_GROUP)
    s = proj.shape[0]
    dproj = jnp.concatenate([
        _unheads_q(dqh).astype(BF16), ddq, ddk, ddv, ddz, dga, dgb,
        _unheads_kv(dkh[:, WINDOW:, :]).astype(BF16), _unheads_kv(dvh[:, WINDOW:, :]).astype(BF16),
        dba.astype(BF16), jnp.zeros((s, D_IN_P - C_BA - LANES), BF16)], axis=1)
    du = _mm(dproj, p["w_in"], "nt", BF16,"du")
    g["w_in"] = _mm(sv["u"], dproj, "tn", BF16,"dw_in")
    (dx,), (dsc1, dsh1) = _rowwise_bwd(_f_mod, [_whole(sv["x"])], [sc1, sh1], [du], [F32], "mod1_bwd", add=(0, dx_a))
    g["mod"] = jnp.stack([dsh1, dsc1, dgt1, dsh2, dsc2, dgt2])
    return dx, g, exchanged


def _permute_w_in(w):
    pad = jnp.zeros(w.shape[:-1] + (D_IN_P - D_IN,), w.dtype)
    return jnp.concatenate([w[..., 0:1024], w[..., 1536:5632], w[..., 5648:7696], w[..., 1024:1536],
                            w[..., 5632:5648], pad], axis=-1)


def _unpermute_w_in(g):
    return jnp.concatenate([g[..., 0:1024], g[..., C_K:C_K + 512], g[..., 1024:5120], g[..., C_BA:C_BA + 16],
                            g[..., 5120:7168]], axis=-1)


def _cols_from_chips(t):
    c, l, r, n = t.shape
    return t.transpose(1, 2, 0, 3).reshape(l, r, c * n)


def _cols_to_chips(t):
    l, r, n4 = t.shape
    return t.reshape(l, r, N_CHIPS, n4 // N_CHIPS).transpose(2, 0, 1, 3)


def _rows_from_chips(t):
    c, l, r, n = t.shape
    return t.transpose(1, 0, 2, 3).reshape(l, c * r, n)


def _rows_to_chips(t):
    l, r4, n = t.shape
    return t.reshape(l, N_CHIPS, r4 // N_CHIPS, n).transpose(1, 0, 2, 3)


_REPLICATED = ("b_ada", "a_log", "dt_bias", "sinks", "dn_norm_w", "ln1_g", "ln1_b", "b_ff1", "b_ff2", "ln2_g", "ln2_b")
_SMALL = _REPLICATED + ("conv_w",)
_PACK_W = 1024
_WEIGHT_ORDER = ("w_ada", "b_ada", "w_in", "conv_w", "a_log", "dt_bias", "sinks", "dn_norm_w", "w_oa", "w_ob", "w_out",
                 "ln1_g", "ln1_b", "w_ff1", "b_ff1", "w_ff2", "b_ff2", "ln2_g", "ln2_b")


def _pack_small(d):
    flat = jnp.concatenate([d[k].reshape(-1) for k in _SMALL])
    rows = -(-flat.shape[0] // (_PACK_W * 8)) * 8
    return jnp.pad(flat, (0, rows * _PACK_W - flat.shape[0])).reshape(rows, _PACK_W)


def _unpack_small(packed, shapes):
    flat = packed.reshape(-1)
    out, off = {}, 0
    for k in _SMALL:
        n = 1
        for d_ in shapes[k]:
            n *= d_
        out[k] = flat[off:off + n].reshape(shapes[k])
        off += n
    return out


def kernel(x, c, w_ada, b_ada, w_in, conv_w, a_log, dt_bias, sinks, dn_norm_w, w_oa, w_ob, w_out, ln1_g, ln1_b, w_ff1, b_ff1, w_ff2, b_ff2, ln2_g, ln2_b, loss_target, m_w_ada, m_b_ada, m_w_in, m_conv_w, m_a_log, m_dt_bias, m_sinks, m_dn_norm_w, m_w_oa, m_w_ob, m_w_out, m_ln1_g, m_ln1_b, m_w_ff1, m_b_ff1, m_w_ff2, m_b_ff2, m_ln2_g, m_ln2_b, v_w_ada, v_b_ada, v_w_in, v_conv_w, v_a_log, v_dt_bias, v_sinks, v_dn_norm_w, v_w_oa, v_w_ob, v_w_out, v_ln1_g, v_ln1_b, v_w_ff1, v_b_ff1, v_w_ff2, v_b_ff2, v_ln2_g, v_ln2_b):
    ix, iy, ic = _place()
    chip = 2 * ix + iy
    dev = 4 * ix + 2 * iy + ic
    weights = dict(w_ada=w_ada, b_ada=b_ada, w_in=w_in, conv_w=conv_w, a_log=a_log, dt_bias=dt_bias, sinks=sinks,
                   dn_norm_w=dn_norm_w, w_oa=w_oa, w_ob=w_ob, w_out=w_out, ln1_g=ln1_g, ln1_b=ln1_b, w_ff1=w_ff1,
                   b_ff1=b_ff1, w_ff2=w_ff2, b_ff2=b_ff2, ln2_g=ln2_g, ln2_b=ln2_b)
    mom_m = dict(w_ada=m_w_ada, b_ada=m_b_ada, w_in=m_w_in, conv_w=m_conv_w, a_log=m_a_log, dt_bias=m_dt_bias,
                 sinks=m_sinks, dn_norm_w=m_dn_norm_w, w_oa=m_w_oa, w_ob=m_w_ob, w_out=m_w_out, ln1_g=m_ln1_g,
                 ln1_b=m_ln1_b, w_ff1=m_w_ff1, b_ff1=m_b_ff1, w_ff2=m_w_ff2, b_ff2=m_b_ff2, ln2_g=m_ln2_g, ln2_b=m_ln2_b)
    mom_v = dict(w_ada=v_w_ada, b_ada=v_b_ada, w_in=v_w_in, conv_w=v_conv_w, a_log=v_a_log, dt_bias=v_dt_bias,
                 sinks=v_sinks, dn_norm_w=v_dn_norm_w, w_oa=v_w_oa, w_ob=v_w_ob, w_out=v_w_out, ln1_g=v_ln1_g,
                 ln1_b=v_ln1_b, w_ff1=v_w_ff1, b_ff1=v_b_ff1, w_ff2=v_w_ff2, b_ff2=v_b_ff2, ln2_g=v_ln2_g, ln2_b=v_ln2_b)

    n_ada = w_ada.shape[2]
    n_cw = conv_w.shape[2]
    taps = jnp.pad(conv_w.reshape(DEPTH * CONV_K, n_cw), ((0, 0), (0, D_MODEL - n_cw)))
    first = _all_gather8(jnp.concatenate([jnp.pad(c, ((0, 7), (0, 0))), taps], axis=0))
    c_all = first[:, 0, :]
    b_shard = lax.dynamic_slice_in_dim(b_ada, chip * n_ada, n_ada, axis=1).reshape(DEPTH, 1, n_ada)
    mod_t = _ada_fwd(c_all, w_ada, b_shard)
    mod_all = _all_gather8(mod_t.reshape(DEPTH * N_DEV, n_ada)).reshape(N_DEV, DEPTH, N_DEV, n_ada)
    mod_mine = lax.dynamic_index_in_dim(mod_all[0::2], dev, axis=2, keepdims=False)
    mod = mod_mine.transpose(1, 0, 2).reshape(DEPTH, 6, 1, D_MODEL)

    cw_all = first[0::2, 8:, :n_cw]
    conv_full = cw_all.transpose(1, 0, 2).reshape(DEPTH, CONV_K, N_CHIPS * n_cw)

    big = ("w_in", "w_oa", "w_ob", "w_out", "w_ff1", "w_ff2")
    w16 = {k: weights[k].astype(BF16) for k in big}
    from_chips = dict(w_in=lambda t: _permute_w_in(_cols_from_chips(t)), w_ff1=_cols_from_chips, w_oa=_rows_from_chips,
                      w_ob=_rows_from_chips, w_out=_rows_from_chips, w_ff2=_rows_from_chips)

    def assemble(name, gathered):
        return from_chips[name](gathered[:, None])[0]

    to_chips = dict(w_in=lambda t: _cols_to_chips(_unpermute_w_in(t)), w_ff1=_cols_to_chips, w_oa=_rows_to_chips,
                    w_ob=_rows_to_chips, w_out=_rows_to_chips, w_ff2=_rows_to_chips)

    def slices_for_chips(g, keys):
        return [to_chips[k](g[k][None])[:, 0].astype(BF16) for k in keys]

    full = [dict() for _ in range(DEPTH)]
    full[0]["w_in"] = assemble("w_in", _chip_exchange([w16["w_in"][0]], True, "gather_weights")[0])

    def layer_params(l):
        p = dict(full[l])
        p["mod"] = mod[l]
        p["conv_w"] = conv_full[l]
        for k in ("a_log", "dt_bias", "sinks", "dn_norm_w"):
            p[k] = weights[k][l]
        for k in ("ln1_g", "ln1_b", "b_ff1", "b_ff2", "ln2_g", "ln2_b"):
            p[k] = weights[k][l].reshape(1, -1)
        return p

    xs = x[0]
    saved = []
    for l in range(DEPTH):
        fetch = dict(o=[w16[k][l] for k in ("w_oa", "w_ob", "w_out")], ff1=[w16["w_ff1"][l]], ff2=[w16["w_ff2"][l]],
                     w_in_next=[w16["w_in"][l + 1]] if l + 1 < DEPTH else [], assemble=assemble)
        xs, sv, arrived = _layer_fwd(xs, layer_params(l), fetch)
        nxt = arrived.pop("w_in_next")
        full[l].update(arrived)
        if nxt is not None:
            full[l + 1]["w_in"] = assemble("w_in", nxt)
        saved.append(sv)
    dy, loss_local = _loss_head(xs, loss_target[0])
    loss = lax.psum(loss_local[0, 0], ("x", "y", "c"))
    early = tuple(k for k in big if k != "w_in")
    grads = [None] * DEPTH
    received = [dict() for _ in range(DEPTH)]
    dx = dy
    pending = []
    for l in reversed(range(DEPTH)):
        carry = functools.partial(lambda g, first: first + slices_for_chips(g, early), first=pending)
        dx, grads[l], got = _layer_bwd(dx, layer_params(l), saved[l], carry)
        if pending:
            received[l + 1]["w_in"] = got[0]
        received[l].update(zip(early, got[len(pending):]))
        pending = slices_for_chips(grads[l], ("w_in",))
    received[0]["w_in"] = _chip_exchange(pending, False, "scatter_grads")[0]
    grad_x = dx[None]
    gstack = {k: jnp.stack([grads[l][k] for l in range(DEPTH)]) for k in grads[0] if k not in big}

    dmod = gstack["mod"].reshape(DEPTH, 6 * D_MODEL)
    small_g = dict(b_ada=dmod, a_log=gstack["a_log"], dt_bias=gstack["dt_bias"], sinks=gstack["sinks"],
                   dn_norm_w=gstack["dn_norm_w"], ln1_g=gstack["ln1_g"], ln1_b=gstack["ln1_b"], b_ff1=gstack["b_ff1"],
                   b_ff2=gstack["b_ff2"], ln2_g=gstack["ln2_g"], ln2_b=gstack["ln2_b"], conv_w=gstack["conv_w"])
    shapes = {k: weights[k].shape for k in _REPLICATED}
    shapes["conv_w"] = small_g["conv_w"].shape
    g_all = _all_gather8(_pack_small(small_g))
    no_conv = jnp.zeros(shapes["conv_w"], F32)
    small_out = _small_adam(g_all, _pack_small(dict(weights, conv_w=no_conv)), _pack_small(dict(mom_m, conv_w=no_conv)),
                            _pack_small(dict(mom_v, conv_w=no_conv)))
    small_res = [_unpack_small(t, shapes) for t in small_out]
    g_conv = lax.dynamic_slice_in_dim(small_res[0]["conv_w"], chip * n_cw, n_cw, axis=2)
    res = {"conv_w": _adam_call([g_conv], conv_w, m_conv_w, v_conv_w, "adam_conv_w", tile=16)}

    dmod_all = g_all.reshape(N_DEV, -1)[:, :DEPTH * 6 * D_MODEL].reshape(N_DEV, DEPTH, 6 * D_MODEL)
    dmod_shard = lax.dynamic_slice_in_dim(dmod_all, chip * n_ada, n_ada, axis=2).transpose(1, 0, 2)
    g_w_ada = _ada_bwd(c_all, dmod_shard)
    res["w_ada"] = _adam_call([g_w_ada], w_ada, m_w_ada, v_w_ada, "adam_w_ada")

    partial = [_sum_slots([received[l][k] for l in range(DEPTH)], "sum_" + k) for k in big]
    theirs = _sibling_exchange(partial, "sibling_grads")
    for k, mine, other in zip(big, partial, theirs):
        shape = weights[k].shape
        res[k] = _adam_call([mine.reshape(shape), other.reshape(shape)], weights[k], mom_m[k], mom_v[k], "adam_" + k)
    for k in _REPLICATED:
        res[k] = [small_res[i][k] for i in range(4)]

    outs = [loss, grad_x]
    for i in range(4):
        outs += [res[k][i] for k in _WEIGHT_ORDER]
    return tuple(outs)
```

```python
import functools

import jax
import jax.numpy as jnp
from jax import lax
from jax.experimental import pallas as pl
from jax.experimental.pallas import tpu as pltpu

F32, BF16 = jnp.float32, jnp.bfloat16
HI = lax.Precision.HIGHEST
MESH = pl.DeviceIdType.MESH

D_MODEL = 1024
DEPTH = 4
ATT_KV_HEADS, ATT_GROUP, ATT_HEAD_DIM, WINDOW = 4, 4, 64, 128
DN_HEADS, DN_HEAD_DIM, CONV_K, CHUNK = 8, 128, 4, 64
D_FF = 4 * D_MODEL
D_IN = 7696
ALPHA = (2 * DEPTH) ** 0.25
LN_EPS = 1e-5
RMS_EPS = 1e-6
ADAM_LR, ADAM_B1, ADAM_B2, ADAM_EPS, ADAM_WD, ADAM_STEP = 0.001, 0.9, 0.999, 1e-08, 0.01, 10

N_CHIPS = 4
N_DEV = 8
LANES = 128
D_IN_P = 8192
C_Q, C_DQ, C_DK, C_DV, C_Z, C_GA, C_GB, C_K, C_V, C_BA = 0, 1024, 2048, 3072, 4096, 5120, 6144, 7168, 7424, 7680
NEG = -1e30
VMEM_LIMIT = 56 << 20


def _pc(body, **kw):
    return pl.pallas_call(body, **kw)


def _cparams(sem=None):
    if sem is None:
        return pltpu.CompilerParams(vmem_limit_bytes=VMEM_LIMIT)
    return pltpu.CompilerParams(vmem_limit_bytes=VMEM_LIMIT, dimension_semantics=sem)


_MM_VMEM_BUDGET = 44 << 20
_MM_MIN_TILE = 256


def _mm_tiles(m, n, k, out_bytes):
    def halvings(d):
        out = [d]
        while out[-1] % 2 == 0 and out[-1] // 2 >= _MM_MIN_TILE:
            out.append(out[-1] // 2)
        return out

    best = None
    for tm in halvings(m):
        for tn in halvings(n):
            if 2 * (2 * tm * k + 2 * tn * k + out_bytes * tm * tn) > _MM_VMEM_BUDGET:
                continue
            cost = (2 * m * k + (m // tm) * 2 * n * k, (m // tm) * (n // tn))
            if best is None or cost < best[0]:
                best = (cost, tm, tn)
    assert best is not None, (m, n, k)
    return best[1], best[2]


def _mm(a, b, mode, out_dtype, name, tm=None, tn=None, exchange=None):
    if mode == "nn":
        (m, k), (_, n) = a.shape, b.shape
        dims = (((1,), (0,)), ((), ()))
    elif mode == "nt":
        (m, k), (n, _) = a.shape, b.shape
        dims = (((1,), (1,)), ((), ()))
    else:
        (k, m), (_, n) = a.shape, b.shape
        dims = (((0,), (0,)), ((), ()))
    if tm is None:
        tm, tn = _mm_tiles(m, n, k, jnp.dtype(out_dtype).itemsize)
    tm, tn = min(tm, m), min(tn, n)
    assert m % tm == 0 and n % tn == 0, (name, m, n, tm, tn)
    a_spec = pl.BlockSpec((k, tm), lambda i, j: (0, i)) if mode == "tn" else pl.BlockSpec((tm, k), lambda i, j: (i, 0))
    b_spec = pl.BlockSpec((tn, k), lambda i, j: (j, 0)) if mode == "nt" else pl.BlockSpec((k, tn), lambda i, j: (0, j))

    def body(a_ref, b_ref, o_ref):
        o_ref[...] = lax.dot_general(a_ref[...], b_ref[...], dims, preferred_element_type=F32).astype(o_ref.dtype)

    out_spec = pl.BlockSpec((tm, tn), lambda i, j: (i, j))
    out_shape = jax.ShapeDtypeStruct((m, n), out_dtype)
    if exchange is None:
        return _pc(body, name=name, grid=(m // tm, n // tn), in_specs=[a_spec, b_spec], out_specs=out_spec,
                   out_shape=out_shape, compiler_params=_cparams())(a, b)
    (out,), exchanged = _call_with_exchange(body, name, (m // tm, n // tn), [a_spec, b_spec], [out_spec], [out_shape],
                                            [], (a, b), exchange)
    return out, exchanged


def _row_specs(rows, tile):
    return [pl.BlockSpec((tile, w), functools.partial(lambda i, cb: (i, cb), cb=cb)) for (_, cb, w) in rows]


def _vec_specs(vecs):
    return [pl.BlockSpec(v.shape, lambda i: (0, 0)) for v in vecs]


def _rowwise(fn, rows, vecs, outs, name, tile=256, exchange=None):
    n = rows[0][0].shape[0]
    tile = min(tile, n)
    nr, nv = len(rows), len(vecs)

    def body(*refs):
        rv = [r[...].astype(F32) for r in refs[:nr]]
        vv = [r[...] for r in refs[nr:nr + nv]]
        for o_ref, val in zip(refs[nr + nv:], fn(*rv, *vv)):
            o_ref[...] = val.astype(o_ref.dtype)

    in_specs = _row_specs(rows, tile) + _vec_specs(vecs)
    out_specs = [pl.BlockSpec((tile, w), lambda i: (i, 0)) for (w, _) in outs]
    out_shape = [jax.ShapeDtypeStruct((n, w), dt) for (w, dt) in outs]
    args = (*[r[0] for r in rows], *vecs)
    if exchange is None:
        return _pc(body, name=name, grid=(n // tile,), in_specs=in_specs, out_specs=out_specs, out_shape=out_shape,
                   compiler_params=_cparams())(*args)
    return _call_with_exchange(body, name, n // tile, in_specs, out_specs, out_shape, [], args, exchange)


def _rowwise_bwd(fn, rows, vecs, cts, row_dtypes, name, tile=256, add=None):
    n = rows[0][0].shape[0]
    tile = min(tile, n)
    nr, nv, nc = len(rows), len(vecs), len(cts)
    want = [i for i, dt in enumerate(row_dtypes) if dt is not None]
    n_add = 0 if add is None else 1

    def body(*refs):
        rv = [r[...].astype(F32) for r in refs[:nr]]
        vv = [r[...] for r in refs[nr:nr + nv]]
        cv = [r[...].astype(F32) for r in refs[nr + nv:nr + nv + nc]]
        pos = nr + nv + nc
        add_ref = refs[pos] if n_add else None
        pos += n_add
        row_out = refs[pos:pos + len(want)]
        vec_out = refs[pos + len(want):]
        _, vjp = jax.vjp(fn, *rv, *vv)
        grads = vjp(tuple(cv))
        for o_ref, i in zip(row_out, want):
            gval = grads[i]
            if n_add and add[0] == i:
                gval = gval + add_ref[...]
            o_ref[...] = gval.astype(o_ref.dtype)

        @pl.when(pl.program_id(0) == 0)
        def _():
            for o_ref in vec_out:
                o_ref[...] = jnp.zeros_like(o_ref)

        for o_ref, gval in zip(vec_out, grads[nr:]):
            o_ref[...] += gval

    ct_rows = [(c, 0, c.shape[1]) for c in cts]
    add_rows = [(add[1], 0, add[1].shape[1])] if n_add else []
    res = _pc(body, name=name, grid=(n // tile,),
              in_specs=_row_specs(rows, tile) + _vec_specs(vecs) + _row_specs(ct_rows + add_rows, tile),
              out_specs=[pl.BlockSpec((tile, rows[i][2]), lambda i_: (i_, 0)) for i in want] + _vec_specs(vecs),
              out_shape=[jax.ShapeDtypeStruct((n, rows[i][2]), row_dtypes[i]) for i in want]
              + [jax.ShapeDtypeStruct(v.shape, F32) for v in vecs],
              compiler_params=_cparams(("arbitrary",)))(*[r[0] for r in rows], *vecs, *cts, *[a[0] for a in add_rows])
    return res[:len(want)], res[len(want):]


def _whole(a, cb=0, w=None):
    return (a, cb, a.shape[1] if w is None else w)


def _ln(x, g, b):
    mu = jnp.mean(x, axis=-1, keepdims=True)
    var = jnp.mean(jnp.square(x - mu), axis=-1, keepdims=True)
    return (x - mu) * lax.rsqrt(var + LN_EPS) * g + b


def _silu(x):
    return x * jax.nn.sigmoid(x)


def _softplus(x):
    return jnp.maximum(x, 0.0) + jnp.log(1.0 + jnp.exp(-jnp.abs(x)))


def _f_mod(x, sc, sh):
    return (x * (1.0 + sc) + sh,)


def _f_gate(ga, gb, ya, yb):
    return (jax.nn.sigmoid(ga) * ya + jax.nn.sigmoid(gb) * yb,)


def _f_post1(x, mixed, gt, g1, b1, sc2, sh2):
    x1 = _ln(ALPHA * x + (1.0 + gt) * mixed, g1, b1)
    return x1, x1 * (1.0 + sc2) + sh2


def _f_act(hpre, b):
    return (jnp.square(jnp.maximum(hpre + b, 0.0)),)


def _f_post2(x1, ff, gt, bff2, g2, b2):
    return (_ln(ALPHA * x1 + (1.0 + gt) * (ff + bff2), g2, b2),)


def _attn_valid(n):
    qi = lax.broadcasted_iota(jnp.int32, (WINDOW, 2 * WINDOW), 0)
    si = lax.broadcasted_iota(jnp.int32, (WINDOW, 2 * WINDOW), 1)
    diff = qi + WINDOW - si
    return (diff >= 0) & (diff < WINDOW) & (n * WINDOW + si - WINDOW >= 0)


def _attn_block(qs, kp, kc, vp, vc, sinks, valid):
    kband = jnp.concatenate([kp, kc], axis=0).astype(BF16)
    vband = jnp.concatenate([vp, vc], axis=0).astype(BF16)
    rng = range(len(qs))
    s = [lax.dot_general(qs[g].astype(BF16), kband, (((1,), (1,)), ((), ())), preferred_element_type=F32) for g in rng]
    s = [jnp.where(valid, s[g] * (ATT_HEAD_DIM ** -0.5), NEG) for g in rng]
    m = [lax.stop_gradient(jnp.maximum(jnp.max(s[g], axis=-1, keepdims=True), sinks[g])) for g in rng]
    p = [jnp.exp(s[g] - m[g]) for g in rng]
    denom = [jnp.sum(p[g], axis=-1, keepdims=True) + jnp.exp(sinks[g] - m[g]) for g in rng]
    probs = [(p[g] / denom[g]).astype(BF16) for g in rng]
    return [jnp.dot(probs[g], vband, preferred_element_type=F32) for g in rng]


def _attn_specs(s):
    nb = s // WINDOW
    q_spec = pl.BlockSpec((1, ATT_GROUP, WINDOW, ATT_HEAD_DIM), lambda h, n: (h, 0, n, 0))
    prev = pl.BlockSpec((1, WINDOW, ATT_HEAD_DIM), lambda h, n: (h, jnp.maximum(n - 1, 0), 0))
    cur = pl.BlockSpec((1, WINDOW, ATT_HEAD_DIM), lambda h, n: (h, n, 0))
    sk = pl.BlockSpec((1, ATT_GROUP, 1, 1), lambda h, n: (h, 0, 0, 0))
    return nb, q_spec, prev, cur, sk


def _attn_fwd(qh, kh, vh, sinks4, exchange=None):
    s = qh.shape[2]
    nb, q_spec, prev, cur, sk = _attn_specs(s)

    def body(q_ref, kp_ref, kc_ref, vp_ref, vc_ref, sk_ref, o_ref):
        valid = _attn_valid(pl.program_id(1))
        heads = range(ATT_GROUP)
        o = _attn_block([q_ref[0, g] for g in heads], kp_ref[0], kc_ref[0], vp_ref[0], vc_ref[0],
                        [sk_ref[0, g] for g in heads], valid)
        for g in heads:
            o_ref[0, g] = o[g].astype(o_ref.dtype)

    (o,), exchanged = _call_with_exchange(
        body, "attn_fwd", (ATT_KV_HEADS, nb), [q_spec, prev, cur, prev, cur, sk], [q_spec],
        [jax.ShapeDtypeStruct(qh.shape, BF16)], [], (qh, kh, kh, vh, vh, sinks4), exchange)
    return o, exchanged


def _attn_bwd(qh, kh, vh, sinks4, doh):
    s = qh.shape[2]
    nb, q_spec, prev, cur, sk = _attn_specs(s)
    acc = pl.BlockSpec((1, s + WINDOW, ATT_HEAD_DIM), lambda h, n: (h, 0, 0))

    def body(q_ref, kp_ref, kc_ref, vp_ref, vc_ref, sk_ref, do_ref, dq_ref, dk_ref, dv_ref, dsk_ref):
        n = pl.program_id(1)
        valid = _attn_valid(n)
        fn = functools.partial(_attn_block, valid=valid)
        heads = range(ATT_GROUP)
        _, vjp = jax.vjp(fn, [q_ref[0, g] for g in heads], kp_ref[0], kc_ref[0], vp_ref[0], vc_ref[0],
                         [sk_ref[0, g] for g in heads])
        dq, dkp, dkc, dvp, dvc, dsk = vjp([do_ref[0, g].astype(F32) for g in heads])
        for g in heads:
            dq_ref[0, g] = dq[g]

        @pl.when(n == 0)
        def _():
            dk_ref[...] = jnp.zeros_like(dk_ref)
            dv_ref[...] = jnp.zeros_like(dv_ref)
            dsk_ref[...] = jnp.zeros_like(dsk_ref)

        band = pl.ds(pl.multiple_of(n * WINDOW, WINDOW), 2 * WINDOW)
        dk_ref[0, band, :] += jnp.concatenate([dkp, dkc], axis=0)
        dv_ref[0, band, :] += jnp.concatenate([dvp, dvc], axis=0)
        for g in heads:
            dsk_ref[0, g] += dsk[g]

    kv_shape = jax.ShapeDtypeStruct((ATT_KV_HEADS, s + WINDOW, ATT_HEAD_DIM), F32)
    return _pc(body, name="attn_bwd", grid=(ATT_KV_HEADS, nb), in_specs=[q_spec, prev, cur, prev, cur, sk, q_spec],
               out_specs=[q_spec, acc, acc, sk],
               out_shape=[jax.ShapeDtypeStruct(qh.shape, F32), kv_shape, kv_shape, jax.ShapeDtypeStruct(sinks4.shape, F32)],
               compiler_params=_cparams(("arbitrary", "arbitrary")))(qh, kh, kh, vh, vh, sinks4, doh)


def _bdot(a, b, dims=(((1,), (0,)), ((), ()))):
    return lax.dot_general(a.astype(BF16), b.astype(BF16), dims, preferred_element_type=F32)


def _hdot(a, b, dims=(((1,), (0,)), ((), ()))):
    return lax.dot_general(a, b, dims, precision=HI, preferred_element_type=F32)


_NN = (((1,), (0,)), ((), ()))
_NT = (((1,), (1,)), ((), ()))
_TN = (((0,), (0,)), ((), ()))


def _split2(a):
    hi = a.astype(BF16)
    return hi, (a - hi.astype(F32)).astype(BF16)


def _dot3(a, b, dims):
    ah, al = _split2(a)
    bh, bl = _split2(b)
    d = lambda p, q: lax.dot_general(p, q, dims, preferred_element_type=F32)
    return d(ah, bh) + (d(ah, bl) + d(al, bh))


@jax.custom_vjp
def _xdot(a, b):
    return _dot3(a, b, _NN)


def _xdot_fwd(a, b):
    return _dot3(a, b, _NN), (a, b)


def _xdot_bwd(res, g):
    a, b = res
    return _dot3(g, b, _NT), _dot3(a, g, _TN)


_xdot.defvjp(_xdot_fwd, _xdot_bwd)


def _mask_dot(mask16, b, dims):
    hi = b.astype(BF16)
    r = b - hi.astype(F32)
    mid = r.astype(BF16)
    lo = (r - mid.astype(F32)).astype(BF16)
    d = lambda q: lax.dot_general(mask16, q, dims, preferred_element_type=F32)
    return d(hi) + (d(mid) + d(lo))


def _chunk_masks():
    r = lax.broadcasted_iota(jnp.int32, (CHUNK, CHUNK), 0)
    c = lax.broadcasted_iota(jnp.int32, (CHUNK, CHUNK), 1)
    return r >= c, r > c, (r == c).astype(F32)


def _dn_local(qs, ks, vs, bs, gs, masks):
    causal, strict, eye = masks
    rng = range(len(qs))
    gb = [jnp.broadcast_to(gs[i], (CHUNK, CHUNK)) for i in rng]
    decay = [jnp.exp(jnp.where(causal, gb[i] - gb[i].T, NEG)) for i in rng]
    kb = [ks[i] * bs[i] for i in rng]
    vb = [vs[i] * bs[i] for i in rng]
    kk = [_bdot(kb[i], ks[i], _NT) for i in rng]
    p = [-jnp.where(strict, kk[i] * decay[i], 0.0) for i in rng]
    t = [eye + p[i] for i in rng]
    for _ in range(5):
        p = [_xdot(p[i], p[i]) for i in rng]
        t = [t[i] + _xdot(p[i], t[i]) for i in rng]
    eg = [jnp.exp(gs[i]) for i in rng]
    u = [_xdot(t[i], vb[i]) for i in rng]
    w = [_xdot(t[i], kb[i] * eg[i]) for i in rng]
    qk = [_bdot(qs[i], ks[i], _NT) for i in rng]
    intra = [qk[i] * decay[i] for i in rng]
    q_dec = [qs[i] * eg[i] for i in rng]
    k_dec = [ks[i] * jnp.exp(gs[i][CHUNK - 1:CHUNK, :] - gs[i]) for i in rng]
    return u, w, intra, q_dec, k_dec


def _dn_state_bwd(u, w, intra, q_dec, k_dec, gcum, state, do, dnext):
    last = jnp.exp(gcum[CHUNK - 1:CHUNK, :])
    x = _bdot(k_dec, dnext)
    t1 = _bdot(intra, do, _TN)
    v_new = u - _bdot(w, state)
    dqd = _bdot(do, state, _NT)
    din = _bdot(do, v_new, _NT)
    dkd = _bdot(v_new, dnext, _NT)
    base = _bdot(q_dec, do, _TN) - _bdot(w, t1, _TN)
    d_vnew = t1 + x
    dw = -_bdot(d_vnew, state, _NT)
    dstate = dnext * last + base - _bdot(w, x, _TN)
    dlast = jnp.sum(jnp.sum(state * dnext, axis=1, keepdims=True), axis=0, keepdims=True)
    row = lax.broadcasted_iota(jnp.int32, (CHUNK, 1), 0)
    dgc = jnp.where(row == CHUNK - 1, dlast * last, 0.0)
    return d_vnew, dw, din, dqd, dkd, dgc, dstate


def _l2norm(t):
    return t * lax.rsqrt(jnp.sum(jnp.square(t), axis=-1, keepdims=True) + RMS_EPS)


def _dn_pre(aq, ak, av, ba, alog, dtb, h):
    lane = lax.broadcasted_iota(jnp.int32, (1, LANES), 1)
    pick = lambda t, i: jnp.sum(jnp.where(lane == i, t, 0.0), axis=1, keepdims=True)
    q = _l2norm(_silu(aq)) * (DN_HEAD_DIM ** -0.5)
    k = _l2norm(_silu(ak))
    v = _silu(av)
    beta = jax.nn.sigmoid(pick(ba, h))
    g = -jnp.exp(pick(alog, h)) * _softplus(pick(ba, h + DN_HEADS) + pick(dtb, h))
    return q, k, v, beta, g


def _dn_post(o, z, nw):
    o = o * lax.rsqrt(jnp.mean(jnp.square(o), axis=-1, keepdims=True) + RMS_EPS) * nw
    return o * _silu(z)


_PAD = 8
_TOK_TILE = 512


def _pad_front(pad_ref, x_ref, s):
    pad_ref[pl.ds(0, _PAD), :] = jnp.zeros((_PAD, pad_ref.shape[1]), F32)
    pad_ref[pl.ds(_PAD, s), :] = x_ref[...]


def _conv_tile(pad_ref, w4, r0, n):
    acc = None
    for j in range(CONV_K):
        term = pad_ref[pl.ds(r0 + _PAD - (CONV_K - 1) + j, n), :] * w4[j:j + 1, :]
        acc = term if acc is None else acc + term
    return acc


def _conv_tile_bwd(pad_ref, da_ref, w4, r0, n):
    dx, dw = None, []
    da = da_ref[pl.ds(r0, n), :]
    for j in range(CONV_K):
        term = da_ref[pl.ds(r0 + CONV_K - 1 - j, n), :] * w4[j:j + 1, :]
        dx = term if dx is None else dx + term
        dw.append(jnp.sum(da * pad_ref[pl.ds(r0 + _PAD - (CONV_K - 1) + j, n), :], axis=0, keepdims=True))
    return dx, jnp.concatenate(dw, axis=0)


def _dn_gcum(g_c, causal_f):
    return _mask_dot(causal_f, jnp.broadcast_to(g_c, (CHUNK, LANES)), _NN)[:, 0:1]


def _dn_in_specs(s):
    col = lambda base: pl.BlockSpec((s, DN_HEAD_DIM), functools.partial(lambda h, b: (0, b + h), b=base // DN_HEAD_DIM))
    cw = lambda base: pl.BlockSpec((CONV_K, DN_HEAD_DIM), functools.partial(lambda h, b: (0, b + h), b=base))
    row = pl.BlockSpec((1, LANES), lambda h: (0, 0))
    ba = pl.BlockSpec((s, LANES), lambda h: (0, C_BA // LANES))
    return [col(C_DQ), col(C_DK), col(C_DV), col(C_Z), ba, cw(0), cw(DN_HEADS), cw(2 * DN_HEADS), row, row, row]


def _chunk_rows(c):
    return pl.ds(pl.multiple_of(c * CHUNK, CHUNK), CHUNK)


def _group(nchunk, want):
    g = min(want, nchunk)
    assert nchunk % g == 0
    return g


def _dn_forward_scan(q_s, k_s, v_s, b_s, g_s, gc_s, loc, o_s, states_ref, s):
    masks = _chunk_masks()
    causal_f = masks[0].astype(BF16)
    nchunk = s // CHUNK
    grp = _group(nchunk, 8)
    u_s, w_s, in_s, qd_s, kd_s = loc

    def local_step(i, carry):
        rows = [_chunk_rows(i * grp + j) for j in range(grp)]
        gcum = [_dn_gcum(g_s[r, :], causal_f) for r in rows]
        u, w, intra, q_dec, k_dec = _dn_local([q_s[r, :] for r in rows], [k_s[r, :] for r in rows],
                                              [v_s[r, :] for r in rows], [b_s[r, :] for r in rows], gcum, masks)
        for j, r in enumerate(rows):
            gc_s[r, :] = gcum[j]
            u_s[r, :] = u[j]
            w_s[r, :] = w[j].astype(w_s.dtype)
            in_s[r, :] = intra[j].astype(in_s.dtype)
            qd_s[r, :] = q_dec[j].astype(qd_s.dtype)
            kd_s[r, :] = k_dec[j].astype(kd_s.dtype)
        return carry

    lax.fori_loop(0, nchunk // grp, local_step, 0)

    def state_step(i, state):
        rows = _chunk_rows(i)
        states_ref[i] = state
        v_new = u_s[rows, :] - _bdot(w_s[rows, :], state)
        o_s[rows, :] = _bdot(qd_s[rows, :], state) + _bdot(in_s[rows, :], v_new)
        last = jnp.exp(gc_s[rows, :][CHUNK - 1:CHUNK, :])
        return state * last + _bdot(kd_s[rows, :], v_new, _TN)

    lax.fori_loop(0, nchunk, state_step, jnp.zeros((DN_HEAD_DIM, DN_HEAD_DIM), F32))


def _dn_saved_shapes(s):
    d, h = DN_HEAD_DIM, DN_HEADS
    shapes = [((h, s, d), F32), ((h, s, d), BF16), ((h, s, CHUNK), BF16), ((h, s, d), BF16), ((h, s, d), BF16),
              ((h, s, 1), F32), ((h, s // CHUNK, d, d), F32), ((h, s, d), F32)]
    return [jax.ShapeDtypeStruct(shp, dt) for shp, dt in shapes]


def _dn_saved_specs(s, **kw):
    return [pl.BlockSpec((1,) + t.shape[1:], functools.partial(lambda h, nd: (h,) + (0,) * nd, nd=len(t.shape) - 1), **kw)
            for t in _dn_saved_shapes(s)]


def _call_with_exchange(body, name, steps, in_specs, out_specs, out_shape, scratch, args, exchange):
    steps = (steps,) if isinstance(steps, int) else tuple(steps)
    params = _cparams(("arbitrary",) * len(steps))
    if exchange is None:
        res = _pc(body, name=name, grid=steps, in_specs=in_specs, out_specs=out_specs, out_shape=out_shape,
                  scratch_shapes=scratch, compiler_params=params)(*args)
        return res, None
    arrays, gather = exchange
    x_in, x_out, x_shape, x_scratch = _exchange_specs(arrays, gather)
    wrapped = _carry_exchange(body, len(in_specs), len(out_specs), len(scratch), len(arrays), gather, steps)
    res = _pc(wrapped, name=name + "_x", grid=steps, in_specs=in_specs + x_in, out_specs=out_specs + x_out,
              out_shape=out_shape + x_shape, scratch_shapes=scratch + x_scratch, compiler_params=params)(*args, *arrays)
    return res[:len(out_specs)], res[len(out_specs):]


def _dn_fwd(proj, conv_w, alog, dtb, nw, exchange=None):
    s = proj.shape[0]
    d = DN_HEAD_DIM

    tt = min(_TOK_TILE, s)

    def body(xq, xk, xv, z, ba, wq, wk, wv, alog_r, dtb_r, nw_r, o_ref, u_o, w_o, in_o, qd_o, kd_o, gc_o, st_o, oraw_o,
             padq, padk, padv, q_s, k_s, v_s, b_s, g_s):
        h = pl.program_id(0)
        loc = [r.at[0] for r in (u_o, w_o, in_o, qd_o, kd_o)]
        gc_s, states, o_s = gc_o.at[0], st_o.at[0], oraw_o.at[0]
        _pad_front(padq, xq, s)
        _pad_front(padk, xk, s)
        _pad_front(padv, xv, s)
        for r0 in range(0, s, tt):
            rows = pl.ds(r0, tt)
            aq, ak, av = _conv_tile(padq, wq[...], r0, tt), _conv_tile(padk, wk[...], r0, tt), _conv_tile(padv, wv[...], r0, tt)
            q_s[rows, :], k_s[rows, :], v_s[rows, :], b_s[rows, :], g_s[rows, :] = _dn_pre(
                aq, ak, av, ba[rows, :], alog_r[...], dtb_r[...], h)
        _dn_forward_scan(q_s, k_s, v_s, b_s, g_s, gc_s, loc, o_s, states, s)
        for r0 in range(0, s, tt):
            rows = pl.ds(r0, tt)
            o_ref[rows, :] = _dn_post(o_s[rows, :], z[rows, :], nw_r[...]).astype(o_ref.dtype)

    big = pltpu.VMEM((s, d), F32)
    thin = pltpu.VMEM((s, 1), F32)
    padded = pltpu.VMEM((s + _PAD, d), F32)
    return _call_with_exchange(
        body, "dn_fwd", DN_HEADS, _dn_in_specs(s), [pl.BlockSpec((s, d), lambda h: (0, h))] + _dn_saved_specs(s),
        [jax.ShapeDtypeStruct((s, DN_HEADS * d), BF16)] + _dn_saved_shapes(s),
        [padded, padded, padded, big, big, big, thin, thin],
        (proj, proj, proj, proj, proj, conv_w, conv_w, conv_w, alog, dtb, nw), exchange)


def _dn_bwd(proj, conv_w, alog, dtb, nw, dob, kept, exchange=None):
    s = proj.shape[0]
    d = DN_HEAD_DIM
    nchunk = s // CHUNK

    tt = min(_TOK_TILE, s)

    def body(xq, xk, xv, z, ba, wq, wk, wv, alog_r, dtb_r, nw_r, dob_ref, u_i, w_i, in_i, qd_i, kd_i, gc_i, st_i, oraw_i,
             dxq, dxk, dxv, dz, dba, dwq, dwk, dwv, dalog, ddtb, dnw,
             padq, padk, padv, q_s, k_s, v_s, b_s, g_s, o_s, dq_s, dk_s, dv_s, db_s, dg_s, dkd_s, din_s, dgc_s):
        h = pl.program_id(0)
        u_s, w_s, in_s, qd_s, kd_s = [r.at[0] for r in (u_i, w_i, in_i, qd_i, kd_i)]
        gc_s, states, oraw = gc_i.at[0], st_i.at[0], oraw_i.at[0]
        masks = _chunk_masks()
        causal_f = masks[0].astype(BF16)
        pre = functools.partial(_dn_pre, h=h)
        _pad_front(padq, xq, s)
        _pad_front(padk, xk, s)
        _pad_front(padv, xv, s)

        def conv_tiles(r0):
            return _conv_tile(padq, wq[...], r0, tt), _conv_tile(padk, wk[...], r0, tt), _conv_tile(padv, wv[...], r0, tt)

        for r0 in range(0, s, tt):
            rows = pl.ds(r0, tt)
            q_s[rows, :], k_s[rows, :], v_s[rows, :], b_s[rows, :], g_s[rows, :] = pre(
                *conv_tiles(r0), ba[rows, :], alog_r[...], dtb_r[...])
        dnw_v = jnp.zeros((1, LANES), F32)
        for r0 in range(0, s, tt):
            rows = pl.ds(r0, tt)
            _, post_vjp = jax.vjp(_dn_post, oraw[rows, :], z[rows, :], nw_r[...])
            do_raw, dz_v, dnw_t = post_vjp(dob_ref[rows, :].astype(F32))
            dz[rows, :] = dz_v.astype(dz.dtype)
            o_s[rows, :] = do_raw
            dnw_v = dnw_v + dnw_t

        def state_step(i, dstate):
            c = nchunk - 1 - i
            rows = _chunk_rows(c)
            du, dw, din, dqd, dkd, dgc, dstate = _dn_state_bwd(
                u_s[rows, :], w_s[rows, :], in_s[rows, :], qd_s[rows, :], kd_s[rows, :], gc_s[rows, :], states[c],
                o_s[rows, :], dstate)
            dq_s[rows, :] = du
            dk_s[rows, :] = dw
            dv_s[rows, :] = dqd
            dkd_s[rows, :] = dkd
            din_s[rows, :] = din
            dgc_s[rows, :] = dgc
            return dstate

        lax.fori_loop(0, nchunk, state_step, jnp.zeros((d, d), F32))
        local = functools.partial(_dn_local, masks=masks)
        grp = _group(nchunk, 8)

        def local_step(i, carry):
            rows = [_chunk_rows(i * grp + j) for j in range(grp)]
            get = lambda ref: [ref[r, :] for r in rows]
            _, vjp = jax.vjp(local, get(q_s), get(k_s), get(v_s), get(b_s), get(gc_s))
            dq_c, dk_c, dv_c, db_c, dgc_c = vjp((get(dq_s), get(dk_s), get(din_s), get(dv_s), get(dkd_s)))
            dgc_c = [dgc_c[j] + dgc_s[r, :] for j, r in enumerate(rows)]
            dg_c = [_mask_dot(causal_f, jnp.broadcast_to(t, (CHUNK, LANES)), _TN)[:, 0:1] for t in dgc_c]
            for j, r in enumerate(rows):
                dq_s[r, :] = dq_c[j]
                dk_s[r, :] = dk_c[j]
                dv_s[r, :] = dv_c[j]
                db_s[r, :] = db_c[j]
                dg_s[r, :] = dg_c[j]
            return carry

        lax.fori_loop(0, nchunk // grp, local_step, 0)

        @pl.when(h == 0)
        def _():
            dba[...] = jnp.zeros_like(dba)
            dalog[...] = jnp.zeros_like(dalog)
            ddtb[...] = jnp.zeros_like(ddtb)
            dnw[...] = jnp.zeros_like(dnw)

        dalog_v = jnp.zeros((1, LANES), F32)
        ddtb_v = jnp.zeros((1, LANES), F32)
        for r0 in range(0, s, tt):
            rows = pl.ds(r0, tt)
            _, pre_vjp = jax.vjp(pre, *conv_tiles(r0), ba[rows, :], alog_r[...], dtb_r[...])
            daq, dak, dav, dba_t, dalog_t, ddtb_t = pre_vjp(
                (dq_s[rows, :], dk_s[rows, :], dv_s[rows, :], db_s[rows, :], dg_s[rows, :]))
            dq_s[rows, :], dk_s[rows, :], dv_s[rows, :] = daq, dak, dav
            dba[rows, :] += dba_t
            dalog_v = dalog_v + dalog_t
            ddtb_v = ddtb_v + ddtb_t
        tail = pl.ds(s, _PAD)
        dq_s[tail, :] = dk_s[tail, :] = dv_s[tail, :] = jnp.zeros((_PAD, d), F32)
        for pad, da_s, w_ref, dx_ref, dw_ref in ((padq, dq_s, wq, dxq, dwq), (padk, dk_s, wk, dxk, dwk), (padv, dv_s, wv, dxv, dwv)):
            dw_acc = jnp.zeros((CONV_K, d), F32)
            for r0 in range(0, s, tt):
                dx_t, dw_t = _conv_tile_bwd(pad, da_s, w_ref[...], r0, tt)
                dx_ref[pl.ds(r0, tt), :] = dx_t.astype(dx_ref.dtype)
                dw_acc = dw_acc + dw_t
            dw_ref[...] = dw_acc
        dalog[...] += dalog_v
        ddtb[...] += ddtb_v
        dnw[...] += dnw_v

    big = pltpu.VMEM((s, d), F32)
    thin = pltpu.VMEM((s, 1), F32)
    padded = pltpu.VMEM((s + _PAD, d), F32)
    w_all = DN_HEADS * d
    col_out = lambda: pl.BlockSpec((s, d), lambda h: (0, h))
    cw_out = lambda: pl.BlockSpec((CONV_K, d), lambda h: (0, h))
    row = lambda: pl.BlockSpec((1, LANES), lambda h: (0, 0))
    big_out = jax.ShapeDtypeStruct((s, w_all), BF16)
    cw_shape = jax.ShapeDtypeStruct((CONV_K, w_all), F32)
    row_shape = jax.ShapeDtypeStruct((1, LANES), F32)
    return _call_with_exchange(
        body, "dn_bwd", DN_HEADS,
        _dn_in_specs(s) + [pl.BlockSpec((s, d), lambda h: (0, h))] + _dn_saved_specs(s, pipeline_mode=pl.Buffered(1)),
        [col_out(), col_out(), col_out(), col_out(), pl.BlockSpec((s, LANES), lambda h: (0, 0)),
         cw_out(), cw_out(), cw_out(), row(), row(), row()],
        [big_out, big_out, big_out, big_out, jax.ShapeDtypeStruct((s, LANES), F32),
         cw_shape, cw_shape, cw_shape, row_shape, row_shape, row_shape],
        [padded, padded, padded, big, big, big, thin, thin, big,
         padded, padded, padded, thin, thin, big, pltpu.VMEM((s, CHUNK), F32), thin],
        (proj, proj, proj, proj, proj, conv_w, conv_w, conv_w, alog, dtb, nw, dob, *kept), exchange)


def _loss_head(y, target, tile=256):
    n, dm = y.shape
    tile = min(tile, n)

    def body(y_ref, t_ref, dy_ref, loss_ref):
        err = y_ref[...] - t_ref[...]
        dy_ref[...] = err * (1.0 / dm)

        @pl.when(pl.program_id(0) == 0)
        def _():
            loss_ref[...] = jnp.zeros_like(loss_ref)

        loss_ref[...] += 0.5 * jnp.sum(jnp.mean(jnp.square(err), axis=-1, keepdims=True), axis=0, keepdims=True)

    blk = pl.BlockSpec((tile, dm), lambda i: (i, 0))
    return _pc(body, name="loss_head", grid=(n // tile,), in_specs=[blk, blk],
               out_specs=[blk, pl.BlockSpec((1, 1), lambda i: (0, 0))],
               out_shape=[jax.ShapeDtypeStruct((n, dm), F32), jax.ShapeDtypeStruct((1, 1), F32)],
               compiler_params=_cparams(("arbitrary",)))(y, target)


def _ada_fwd(c_all, w_ada, b_shard):
    nl, dm, n = w_ada.shape

    def body(c_ref, w_ref, b_ref, o_ref):
        ca = _silu(c_ref[...]).astype(BF16)
        o_ref[0] = jnp.dot(ca, w_ref[0].astype(BF16), preferred_element_type=F32) + b_ref[0]

    return _pc(body, name="ada_fwd", grid=(nl,),
               in_specs=[pl.BlockSpec((N_DEV, dm), lambda l: (0, 0)), pl.BlockSpec((1, dm, n), lambda l: (l, 0, 0)),
                         pl.BlockSpec((1, 1, n), lambda l: (l, 0, 0))],
               out_specs=pl.BlockSpec((1, N_DEV, n), lambda l: (l, 0, 0)),
               out_shape=jax.ShapeDtypeStruct((nl, N_DEV, n), F32), compiler_params=_cparams())(c_all, w_ada, b_shard)


def _ada_bwd(c_all, dmod):
    nl, _, n = dmod.shape
    dm = c_all.shape[1]

    def body(c_ref, d_ref, o_ref):
        o_ref[0] = _hdot(_silu(c_ref[...]), d_ref[0], _TN)

    return _pc(body, name="ada_bwd", grid=(nl,),
               in_specs=[pl.BlockSpec((N_DEV, dm), lambda l: (0, 0)), pl.BlockSpec((1, N_DEV, n), lambda l: (l, 0, 0))],
               out_specs=pl.BlockSpec((1, dm, n), lambda l: (l, 0, 0)),
               out_shape=jax.ShapeDtypeStruct((nl, dm, n), F32), compiler_params=_cparams())(c_all, dmod)


def _adamw(g, w, m, v):
    m = ADAM_B1 * m + (1.0 - ADAM_B1) * g
    v = ADAM_B2 * v + (1.0 - ADAM_B2) * jnp.square(g)
    m_hat = m / (1.0 - ADAM_B1 ** ADAM_STEP)
    v_hat = v / (1.0 - ADAM_B2 ** ADAM_STEP)
    delta = -ADAM_LR * (m_hat / (jnp.sqrt(v_hat) + ADAM_EPS) + ADAM_WD * w)
    return delta, m, v


def _adam_call(parts, w, m, v, name, tile=128, exchange=None):
    shape = w.shape
    flat = lambda t: t.reshape(-1, shape[-1])
    width = shape[-1]

    def fn(*vals):
        g = vals[0] if len(parts) == 1 else vals[0] + vals[1]
        return (g,) + _adamw(g, *vals[len(parts):])

    rows = [_whole(flat(t)) for t in (*parts, w, m, v)]
    if exchange is None:
        return [o.reshape(shape) for o in _rowwise(fn, rows, [], [(width, F32)] * 4, name, tile=tile)]
    outs, exchanged = _rowwise(fn, rows, [], [(width, F32)] * 4, name, tile=tile, exchange=exchange)
    return [o.reshape(shape) for o in outs], exchanged


def _sum_slots(per_layer, name, tile=128):
    nl = len(per_layer)
    _, n, width = per_layer[0].shape
    tile = min(tile, n)
    nt = n // tile

    def body(*refs):
        o_ref = refs[nl]
        for lp in range(nl):
            @pl.when(pl.program_id(0) == lp)
            def _(r_ref=refs[lp]):
                acc = r_ref[0].astype(F32)
                for j in range(1, N_CHIPS):
                    acc = acc + r_ref[j].astype(F32)
                o_ref[...] = acc

    in_specs = [pl.BlockSpec((N_CHIPS, tile, width), functools.partial(lambda l, t, lp: (0, jnp.where(l == lp, t, 0), 0), lp=lp))
                for lp in range(nl)]
    return _pc(body, name=name, grid=(nl, nt), in_specs=in_specs,
               out_specs=pl.BlockSpec((tile, width), lambda l, t: (l * nt + t, 0)),
               out_shape=jax.ShapeDtypeStruct((nl * n, width), F32), compiler_params=_cparams())(*per_layer)


def _small_adam(g_all, w, m, v):
    def body(g_ref, w_ref, m_ref, v_ref, og, od, om, ov):
        g = g_ref[0]
        for j in range(1, N_DEV):
            g = g + g_ref[j]
        og[...] = g
        od[...], om[...], ov[...] = _adamw(g, w_ref[...], m_ref[...], v_ref[...])

    vm = pl.BlockSpec(memory_space=pltpu.VMEM)
    shp = jax.ShapeDtypeStruct(w.shape, F32)
    return _pc(body, name="small_adam", in_specs=[vm] * 4, out_specs=[vm] * 4, out_shape=[shp] * 4,
               compiler_params=_cparams())(g_all, w, m, v)


def _place():
    return lax.axis_index("x"), lax.axis_index("y"), lax.axis_index("c")


def _flip(v, bit):
    return 1 - v if bit else v


def _all_gather8(a):
    r, n = a.shape

    def body(a_ref, o_ref, send_sems, recv_sems):
        x, y, c = _place()
        me = 4 * x + 2 * y + c
        o_ref[me] = a_ref[...]
        copies = []
        for k in range(1, N_DEV):
            peer = (_flip(x, k & 4), _flip(y, k & 2), _flip(c, k & 1))
            copies.append(pltpu.make_async_remote_copy(
                src_ref=a_ref, dst_ref=o_ref.at[me], send_sem=send_sems.at[k - 1], recv_sem=recv_sems.at[k - 1],
                device_id=peer, device_id_type=MESH))
        for cp in copies:
            cp.start()
        for k in range(1, N_DEV):
            px, py, pc_ = _flip(x, k & 4), _flip(y, k & 2), _flip(c, k & 1)
            pltpu.make_async_remote_copy(
                src_ref=a_ref, dst_ref=o_ref.at[4 * px + 2 * py + pc_], send_sem=send_sems.at[k - 1],
                recv_sem=recv_sems.at[k - 1], device_id=(px, py, pc_), device_id_type=MESH).wait_recv()
        for cp in copies:
            cp.wait_send()

    vm = pl.BlockSpec(memory_space=pltpu.VMEM)
    return _pc(body, name="all_gather8", in_specs=[vm], out_specs=vm,
               out_shape=jax.ShapeDtypeStruct((N_DEV, r, n), a.dtype),
               scratch_shapes=[pltpu.SemaphoreType.DMA((N_DEV - 1,)), pltpu.SemaphoreType.DMA((N_DEV - 1,))],
               compiler_params=_cparams())(a)


def _chip_exchange(arrays, gather, name):
    na = len(arrays)

    def body(*refs):
        ins, outs, sems = refs[:na], refs[na:2 * na], refs[2 * na:]
        _exchange_copies(ins, outs, sems, gather, start=True)
        _exchange_copies(ins, outs, sems, gather, start=False)

    in_specs, out_specs, out_shape, scratch = _exchange_specs(arrays, gather)
    return _pc(body, name=name, in_specs=in_specs, out_specs=out_specs, out_shape=out_shape, scratch_shapes=scratch,
               compiler_params=_cparams())(*arrays)


def _exchange_specs(arrays, gather):
    na = len(arrays)
    hbm = pl.BlockSpec(memory_space=pl.ANY)
    out_shape = [jax.ShapeDtypeStruct(((N_CHIPS,) + a.shape) if gather else a.shape, a.dtype) for a in arrays]
    n_remote = 4 if gather else 2
    scratch = [pltpu.SemaphoreType.DMA((3 * na,))] * n_remote + [pltpu.SemaphoreType.DMA((na,))]
    return [hbm] * na, [hbm] * na, out_shape, scratch


def _gather_copies(ins, outs, sems, start):
    send_i, recv_i, send_d, recv_d, local_sems = sems
    x, y, c = _place()
    me = 2 * x + y
    sibling = (x, y, 1 - c)
    ici_sends, ici_arrivals, hand_ons, hand_arrivals, locals_ = [], [], [], [], []
    for i in range(len(ins)):
        half = ins[i].shape[0] // 2
        mine, other = pl.ds(c * half, half), pl.ds((1 - c) * half, half)
        locals_.append(pltpu.make_async_copy(ins[i], outs[i].at[me], local_sems.at[i]))
        for j in range(1, N_CHIPS):
            px, py = _flip(x, j & 2), _flip(y, j & 1)
            peer = 2 * px + py
            k = i * 3 + j - 1
            ici = dict(send_sem=send_i.at[k], recv_sem=recv_i.at[k], device_id=(px, py, c), device_id_type=MESH)
            d2d = dict(send_sem=send_d.at[k], recv_sem=recv_d.at[k], device_id=sibling, device_id_type=MESH)
            ici_sends.append(pltpu.make_async_remote_copy(src_ref=ins[i].at[mine], dst_ref=outs[i].at[me, mine], **ici))
            ici_arrivals.append(pltpu.make_async_remote_copy(src_ref=ins[i].at[mine], dst_ref=outs[i].at[peer, mine], **ici))
            hand_ons.append(pltpu.make_async_remote_copy(
                src_ref=outs[i].at[peer, mine], dst_ref=outs[i].at[peer, mine], **d2d))
            hand_arrivals.append(pltpu.make_async_remote_copy(
                src_ref=outs[i].at[peer, other], dst_ref=outs[i].at[peer, other], **d2d))
    if start:
        for cp in locals_ + ici_sends:
            cp.start()
    else:
        for arrival, hand_on in zip(ici_arrivals, hand_ons):
            arrival.wait_recv()
            hand_on.start()
        for cp in hand_arrivals:
            cp.wait_recv()
        for cp in ici_sends + hand_ons:
            cp.wait_send()
        for cp in locals_:
            cp.wait()


def _exchange_copies(ins, outs, sems, gather, start):
    if gather:
        return _gather_copies(ins, outs, sems, start)
    send_sems, recv_sems, local_sems = sems
    x, y, c = _place()
    me = 2 * x + y
    sends, arrivals, locals_ = [], [], []
    for i in range(len(ins)):
        locals_.append(pltpu.make_async_copy(ins[i] if gather else ins[i].at[me], outs[i].at[me], local_sems.at[i]))
        for j in range(1, N_CHIPS):
            px, py = _flip(x, j & 2), _flip(y, j & 1)
            peer = 2 * px + py
            pair = dict(send_sem=send_sems.at[i * 3 + j - 1], recv_sem=recv_sems.at[i * 3 + j - 1],
                        device_id=(px, py, c), device_id_type=MESH)
            sends.append(pltpu.make_async_remote_copy(
                src_ref=ins[i] if gather else ins[i].at[peer], dst_ref=outs[i].at[me], **pair))
            arrivals.append(pltpu.make_async_remote_copy(
                src_ref=ins[i] if gather else ins[i].at[me], dst_ref=outs[i].at[peer], **pair))
    if start:
        for cp in locals_ + sends:
            cp.start()
    else:
        for cp in arrivals:
            cp.wait_recv()
        for cp in sends:
            cp.wait_send()
        for cp in locals_:
            cp.wait()


def _carry_exchange(body, n_in, n_out, n_scratch, n_arrays, gather, steps):
    def wrapped(*refs):
        na = n_arrays
        ins, xin = refs[:n_in], refs[n_in:n_in + na]
        outs = refs[n_in + na:n_in + na + n_out]
        xout = refs[n_in + na + n_out:n_in + 2 * na + n_out]
        rest = refs[n_in + 2 * na + n_out:]
        scratch, sems = rest[:n_scratch], rest[n_scratch:]
        first, last = True, True
        for axis, n in enumerate(steps):
            first = jnp.logical_and(first, pl.program_id(axis) == 0)
            last = jnp.logical_and(last, pl.program_id(axis) == n - 1)

        @pl.when(first)
        def _():
            _exchange_copies(xin, xout, sems, gather, start=True)

        body(*ins, *outs, *scratch)

        @pl.when(last)
        def _():
            _exchange_copies(xin, xout, sems, gather, start=False)

    return wrapped


def _sibling_exchange(arrays, name):
    na = len(arrays)

    def body(*refs):
        ins, outs = refs[:na], refs[na:2 * na]
        send_sems, recv_sems = refs[2 * na:]
        x, y, c = _place()
        copies = [pltpu.make_async_remote_copy(
            src_ref=ins[i], dst_ref=outs[i], send_sem=send_sems.at[i], recv_sem=recv_sems.at[i],
            device_id=(x, y, 1 - c), device_id_type=MESH) for i in range(na)]
        for cp in copies:
            cp.start()
        for cp in copies:
            cp.wait()

    hbm = pl.BlockSpec(memory_space=pl.ANY)
    return _pc(body, name=name, in_specs=[hbm] * na, out_specs=[hbm] * na,
               out_shape=[jax.ShapeDtypeStruct(a.shape, a.dtype) for a in arrays],
               scratch_shapes=[pltpu.SemaphoreType.DMA((na,)), pltpu.SemaphoreType.DMA((na,))],
               compiler_params=_cparams())(*arrays)


def _heads_q(t):
    s = t.shape[0]
    return t.reshape(s, ATT_KV_HEADS, ATT_GROUP, ATT_HEAD_DIM).transpose(1, 2, 0, 3)


def _unheads_q(t):
    s = t.shape[2]
    return t.transpose(2, 0, 1, 3).reshape(s, ATT_KV_HEADS * ATT_GROUP * ATT_HEAD_DIM)


def _heads_kv(t):
    s = t.shape[0]
    return t.reshape(s, ATT_KV_HEADS, ATT_HEAD_DIM).transpose(1, 0, 2)


def _unheads_kv(t):
    s = t.shape[1]
    return t.transpose(1, 0, 2).reshape(s, ATT_KV_HEADS * ATT_HEAD_DIM)


def _row128(v):
    return jnp.pad(v, (0, LANES - v.shape[0])).reshape(1, LANES)


def _layer_fwd(x, p, fetch=None):
    p = dict(p)
    arrived = {}
    sh1, sc1, gt1, sh2, sc2, gt2 = [p["mod"][i] for i in range(6)]
    (u,) = _rowwise(_f_mod, [_whole(x)], [sc1, sh1], [(D_MODEL, BF16)], "mod1")
    if fetch is None:
        proj = _mm(u, p["w_in"], "nn", F32, "proj")
    else:
        proj, got = _mm(u, p["w_in"], "nn", F32, "proj", exchange=(fetch["o"], True))
        for k, t in zip(("w_oa", "w_ob", "w_out"), got):
            arrived[k] = fetch["assemble"](k, t)
    qh = _heads_q(proj[:, C_Q:C_Q + 1024])
    kh = _heads_kv(proj[:, C_K:C_K + 256])
    vh = _heads_kv(proj[:, C_V:C_V + 256])
    sinks4 = p["sinks"].reshape(ATT_KV_HEADS, ATT_GROUP, 1, 1)
    o_heads, got = _attn_fwd(qh, kh, vh, sinks4, None if fetch is None else (fetch["ff1"], True))
    if fetch is not None:
        arrived["w_ff1"] = fetch["assemble"]("w_ff1", got[0])
    o_a = _unheads_q(o_heads)
    (o_b, *dn_kept), got = _dn_fwd(proj, p["conv_w"], _row128(p["a_log"]), _row128(p["dt_bias"]),
                                   p["dn_norm_w"].reshape(1, LANES),
                                   None if fetch is None else (fetch["ff2"] + fetch["w_in_next"], True))
    if fetch is not None:
        arrived["w_ff2"] = fetch["assemble"]("w_ff2", got[0])
        arrived["w_in_next"] = got[1] if fetch["w_in_next"] else None
    p.update({k: v for k, v in arrived.items() if k != "w_in_next"})
    y_a = _mm(o_a, p["w_oa"], "nn", BF16, "y_a")
    y_b = _mm(o_b, p["w_ob"], "nn", BF16, "y_b")
    (gm,) = _rowwise(_f_gate, [(proj, C_GA // 1024, 1024), (proj, C_GB // 1024, 1024), _whole(y_a), _whole(y_b)], [],
                     [(D_MODEL, BF16)], "gate")
    mixed = _mm(gm, p["w_out"], "nn", BF16, "mixed")
    x1, u2 = _rowwise(_f_post1, [_whole(x), _whole(mixed)], [gt1, p["ln1_g"], p["ln1_b"], sc2, sh2],
                      [(D_MODEL, F32), (D_MODEL, BF16)], "post1")
    hpre = _mm(u2, p["w_ff1"], "nn", BF16, "ff1")
    (h,) = _rowwise(_f_act, [_whole(hpre)], [p["b_ff1"]], [(D_FF, BF16)], "act")
    ff = _mm(h, p["w_ff2"], "nn", BF16, "ff2")
    (x2,) = _rowwise(_f_post2, [_whole(x1), _whole(ff)], [gt2, p["b_ff2"], p["ln2_g"], p["ln2_b"]],
                     [(D_MODEL, F32)], "post2")
    saved = dict(x=x, u=u, proj=proj, o_a=o_a, o_b=o_b, y_a=y_a, y_b=y_b, gm=gm, mixed=mixed, x1=x1, u2=u2,
                 hpre=hpre, h=h, ff=ff, dn_kept=dn_kept, heads=(qh, kh, vh))
    return x2, saved, arrived


def _layer_bwd(dx2, p, sv, carry=None, carry_last=None):
    sh1, sc1, gt1, sh2, sc2, gt2 = [p["mod"][i] for i in range(6)]
    g = {}
    (dx1_a, dff), (dgt2, g["b_ff2"], g["ln2_g"], g["ln2_b"]) = _rowwise_bwd(
        _f_post2, [_whole(sv["x1"]), _whole(sv["ff"])], [gt2, p["b_ff2"], p["ln2_g"], p["ln2_b"]], [dx2],
        [F32, BF16], "post2_bwd")
    dh = _mm(dff, p["w_ff2"], "nt", BF16,"dh")
    g["w_ff2"] = _mm(sv["h"], dff, "tn", BF16,"dw_ff2")
    (dhpre,), (g["b_ff1"],) = _rowwise_bwd(_f_act, [_whole(sv["hpre"])], [p["b_ff1"]], [dh], [BF16], "act_bwd")
    du2 = _mm(dhpre, p["w_ff1"], "nt", BF16,"du2")
    g["w_ff1"] = _mm(sv["u2"], dhpre, "tn", BF16,"dw_ff1")
    (dx_a, dmixed), (dgt1, g["ln1_g"], g["ln1_b"], dsc2, dsh2) = _rowwise_bwd(
        _f_post1, [_whole(sv["x"]), _whole(sv["mixed"])], [gt1, p["ln1_g"], p["ln1_b"], sc2, sh2], [dx1_a, du2],
        [F32, BF16], "post1_bwd")
    dgm = _mm(dmixed, p["w_out"], "nt", BF16,"dgm")
    g["w_out"] = _mm(sv["gm"], dmixed, "tn", BF16,"dw_out")
    proj = sv["proj"]
    (dga, dgb, dya, dyb), _ = _rowwise_bwd(
        _f_gate, [(proj, C_GA // 1024, 1024), (proj, C_GB // 1024, 1024), _whole(sv["y_a"]), _whole(sv["y_b"])], [],
        [dgm], [BF16, BF16, BF16, BF16], "gate_bwd")
    do_a = _mm(dya, p["w_oa"], "nt", BF16,"do_a")
    g["w_oa"] = _mm(sv["o_a"], dya, "tn", BF16,"dw_oa")
    do_b = _mm(dyb, p["w_ob"], "nt", BF16,"do_b")
    g["w_ob"] = _mm(sv["o_b"], dyb, "tn", BF16,"dw_ob")
    exchange = None if carry is None else (carry(g), False)
    (ddq, ddk, ddv, ddz, dba, dwq, dwk, dwv, dalog, ddtb, dnw), exchanged = _dn_bwd(
        proj, p["conv_w"], _row128(p["a_log"]), _row128(p["dt_bias"]), p["dn_norm_w"].reshape(1, LANES), do_b,
        sv["dn_kept"], exchange)
    g["conv_w"] = jnp.concatenate([dwq, dwk, dwv], axis=1)
    g["a_log"], g["dt_bias"], g["dn_norm_w"] = dalog[0, :DN_HEADS], ddtb[0, :DN_HEADS], dnw[0]
    qh, kh, vh = sv["heads"]
    sinks4 = p["sinks"].reshape(ATT_KV_HEADS, ATT_GROUP, 1, 1)
    dqh, dkh, dvh, dsk = _attn_bwd(qh, kh, vh, sinks4, _heads_q(do_a))
    g["sinks"] = dsk.reshape(ATT_KV_HEADS * ATT_GROUP)
    s = proj.shape[0]
    dproj = jnp.concatenate([
        _unheads_q(dqh).astype(BF16), ddq, ddk, ddv, ddz, dga, dgb,
        _unheads_kv(dkh[:, WINDOW:, :]).astype(BF16), _unheads_kv(dvh[:, WINDOW:, :]).astype(BF16),
        dba.astype(BF16), jnp.zeros((s, D_IN_P - C_BA - LANES), BF16)], axis=1)
    g["w_in"] = _mm(sv["u"], dproj, "tn", BF16,"dw_in")
    if carry_last is None:
        du = _mm(dproj, p["w_in"], "nt", BF16,"du")
    else:
        du, got_last = _mm(dproj, p["w_in"], "nt", BF16,"du", exchange=(carry_last(g), False))
        exchanged = list(exchanged or []) + list(got_last)
    (dx,), (dsc1, dsh1) = _rowwise_bwd(_f_mod, [_whole(sv["x"])], [sc1, sh1], [du], [F32], "mod1_bwd", add=(0, dx_a))
    g["mod"] = jnp.stack([dsh1, dsc1, dgt1, dsh2, dsc2, dgt2])
    return dx, g, exchanged


def _permute_w_in(w):
    pad = jnp.zeros(w.shape[:-1] + (D_IN_P - D_IN,), w.dtype)
    return jnp.concatenate([w[..., 0:1024], w[..., 1536:5632], w[..., 5648:7696], w[..., 1024:1536],
                            w[..., 5632:5648], pad], axis=-1)


def _unpermute_w_in(g):
    return jnp.concatenate([g[..., 0:1024], g[..., C_K:C_K + 512], g[..., 1024:5120], g[..., C_BA:C_BA + 16],
                            g[..., 5120:7168]], axis=-1)


def _cols_from_chips(t):
    c, l, r, n = t.shape
    return t.transpose(1, 2, 0, 3).reshape(l, r, c * n)


def _cols_to_chips(t):
    l, r, n4 = t.shape
    return t.reshape(l, r, N_CHIPS, n4 // N_CHIPS).transpose(2, 0, 1, 3)


def _rows_from_chips(t):
    c, l, r, n = t.shape
    return t.transpose(1, 0, 2, 3).reshape(l, c * r, n)


def _rows_to_chips(t):
    l, r4, n = t.shape
    return t.reshape(l, N_CHIPS, r4 // N_CHIPS, n).transpose(1, 0, 2, 3)


_REPLICATED = ("b_ada", "a_log", "dt_bias", "sinks", "dn_norm_w", "ln1_g", "ln1_b", "b_ff1", "b_ff2", "ln2_g", "ln2_b")
_SMALL = _REPLICATED + ("conv_w",)
_PACK_W = 1024
_WEIGHT_ORDER = ("w_ada", "b_ada", "w_in", "conv_w", "a_log", "dt_bias", "sinks", "dn_norm_w", "w_oa", "w_ob", "w_out",
                 "ln1_g", "ln1_b", "w_ff1", "b_ff1", "w_ff2", "b_ff2", "ln2_g", "ln2_b")


def _pack_small(d):
    flat = jnp.concatenate([d[k].reshape(-1) for k in _SMALL])
    rows = -(-flat.shape[0] // (_PACK_W * 8)) * 8
    return jnp.pad(flat, (0, rows * _PACK_W - flat.shape[0])).reshape(rows, _PACK_W)


def _unpack_small(packed, shapes):
    flat = packed.reshape(-1)
    out, off = {}, 0
    for k in _SMALL:
        n = 1
        for d_ in shapes[k]:
            n *= d_
        out[k] = flat[off:off + n].reshape(shapes[k])
        off += n
    return out


def kernel(x, c, w_ada, b_ada, w_in, conv_w, a_log, dt_bias, sinks, dn_norm_w, w_oa, w_ob, w_out, ln1_g, ln1_b, w_ff1, b_ff1, w_ff2, b_ff2, ln2_g, ln2_b, loss_target, m_w_ada, m_b_ada, m_w_in, m_conv_w, m_a_log, m_dt_bias, m_sinks, m_dn_norm_w, m_w_oa, m_w_ob, m_w_out, m_ln1_g, m_ln1_b, m_w_ff1, m_b_ff1, m_w_ff2, m_b_ff2, m_ln2_g, m_ln2_b, v_w_ada, v_b_ada, v_w_in, v_conv_w, v_a_log, v_dt_bias, v_sinks, v_dn_norm_w, v_w_oa, v_w_ob, v_w_out, v_ln1_g, v_ln1_b, v_w_ff1, v_b_ff1, v_w_ff2, v_b_ff2, v_ln2_g, v_ln2_b):
    ix, iy, ic = _place()
    chip = 2 * ix + iy
    dev = 4 * ix + 2 * iy + ic
    weights = dict(w_ada=w_ada, b_ada=b_ada, w_in=w_in, conv_w=conv_w, a_log=a_log, dt_bias=dt_bias, sinks=sinks,
                   dn_norm_w=dn_norm_w, w_oa=w_oa, w_ob=w_ob, w_out=w_out, ln1_g=ln1_g, ln1_b=ln1_b, w_ff1=w_ff1,
                   b_ff1=b_ff1, w_ff2=w_ff2, b_ff2=b_ff2, ln2_g=ln2_g, ln2_b=ln2_b)
    mom_m = dict(w_ada=m_w_ada, b_ada=m_b_ada, w_in=m_w_in, conv_w=m_conv_w, a_log=m_a_log, dt_bias=m_dt_bias,
                 sinks=m_sinks, dn_norm_w=m_dn_norm_w, w_oa=m_w_oa, w_ob=m_w_ob, w_out=m_w_out, ln1_g=m_ln1_g,
                 ln1_b=m_ln1_b, w_ff1=m_w_ff1, b_ff1=m_b_ff1, w_ff2=m_w_ff2, b_ff2=m_b_ff2, ln2_g=m_ln2_g, ln2_b=m_ln2_b)
    mom_v = dict(w_ada=v_w_ada, b_ada=v_b_ada, w_in=v_w_in, conv_w=v_conv_w, a_log=v_a_log, dt_bias=v_dt_bias,
                 sinks=v_sinks, dn_norm_w=v_dn_norm_w, w_oa=v_w_oa, w_ob=v_w_ob, w_out=v_w_out, ln1_g=v_ln1_g,
                 ln1_b=v_ln1_b, w_ff1=v_w_ff1, b_ff1=v_b_ff1, w_ff2=v_w_ff2, b_ff2=v_b_ff2, ln2_g=v_ln2_g, ln2_b=v_ln2_b)

    n_ada = w_ada.shape[2]
    n_cw = conv_w.shape[2]
    taps = jnp.pad(conv_w.reshape(DEPTH * CONV_K, n_cw), ((0, 0), (0, D_MODEL - n_cw)))
    first = _all_gather8(jnp.concatenate([jnp.pad(c, ((0, 7), (0, 0))), taps], axis=0))
    c_all = first[:, 0, :]
    b_shard = lax.dynamic_slice_in_dim(b_ada, chip * n_ada, n_ada, axis=1).reshape(DEPTH, 1, n_ada)
    mod_t = _ada_fwd(c_all, w_ada, b_shard)
    mod_all = _all_gather8(mod_t.reshape(DEPTH * N_DEV, n_ada)).reshape(N_DEV, DEPTH, N_DEV, n_ada)
    mod_mine = lax.dynamic_index_in_dim(mod_all[0::2], dev, axis=2, keepdims=False)
    mod = mod_mine.transpose(1, 0, 2).reshape(DEPTH, 6, 1, D_MODEL)

    cw_all = first[0::2, 8:, :n_cw]
    conv_full = cw_all.transpose(1, 0, 2).reshape(DEPTH, CONV_K, N_CHIPS * n_cw)

    big = ("w_in", "w_oa", "w_ob", "w_out", "w_ff1", "w_ff2")
    w16 = {k: weights[k].astype(BF16) for k in big}
    from_chips = dict(w_in=lambda t: _permute_w_in(_cols_from_chips(t)), w_ff1=_cols_from_chips, w_oa=_rows_from_chips,
                      w_ob=_rows_from_chips, w_out=_rows_from_chips, w_ff2=_rows_from_chips)

    def assemble(name, gathered):
        return from_chips[name](gathered[:, None])[0]

    to_chips = dict(w_in=lambda t: _cols_to_chips(_unpermute_w_in(t)), w_ff1=_cols_to_chips, w_oa=_rows_to_chips,
                    w_ob=_rows_to_chips, w_out=_rows_to_chips, w_ff2=_rows_to_chips)

    def slices_for_chips(g, keys):
        return [to_chips[k](g[k][None])[:, 0].astype(BF16) for k in keys]

    full = [dict() for _ in range(DEPTH)]
    full[0]["w_in"] = assemble("w_in", _chip_exchange([w16["w_in"][0]], True, "gather_weights")[0])

    def layer_params(l):
        p = dict(full[l])
        p["mod"] = mod[l]
        p["conv_w"] = conv_full[l]
        for k in ("a_log", "dt_bias", "sinks", "dn_norm_w"):
            p[k] = weights[k][l]
        for k in ("ln1_g", "ln1_b", "b_ff1", "b_ff2", "ln2_g", "ln2_b"):
            p[k] = weights[k][l].reshape(1, -1)
        return p

    xs = x[0]
    saved = []
    for l in range(DEPTH):
        fetch = dict(o=[w16[k][l] for k in ("w_oa", "w_ob", "w_out")], ff1=[w16["w_ff1"][l]], ff2=[w16["w_ff2"][l]],
                     w_in_next=[w16["w_in"][l + 1]] if l + 1 < DEPTH else [], assemble=assemble)
        xs, sv, arrived = _layer_fwd(xs, layer_params(l), fetch)
        nxt = arrived.pop("w_in_next")
        full[l].update(arrived)
        if nxt is not None:
            full[l + 1]["w_in"] = assemble("w_in", nxt)
        saved.append(sv)
    dy, loss_local = _loss_head(xs, loss_target[0])
    loss = lax.psum(loss_local[0, 0], ("x", "y", "c"))
    early = tuple(k for k in big if k != "w_in")
    half = weights["w_in"].shape[1] // 2
    grads = [None] * DEPTH
    received = [dict() for _ in range(DEPTH)]
    dx = dy
    pending = []
    for l in reversed(range(DEPTH)):
        carry = functools.partial(lambda g, first: first + slices_for_chips(g, early), first=pending)
        last = (lambda g: [slices_for_chips(g, ("w_in",))[0][:, :half]]) if l == 0 else None
        dx, grads[l], got = _layer_bwd(dx, layer_params(l), saved[l], carry, last)
        if pending:
            received[l + 1]["w_in"] = got[0]
        received[l].update(zip(early, got[len(pending):len(pending) + len(early)]))
        pending = slices_for_chips(grads[l], ("w_in",))
    upper_half = got[-1]
    grad_x = dx[None]
    gstack = {k: jnp.stack([grads[l][k] for l in range(DEPTH)]) for k in grads[0] if k not in big}

    dmod = gstack["mod"].reshape(DEPTH, 6 * D_MODEL)
    small_g = dict(b_ada=dmod, a_log=gstack["a_log"], dt_bias=gstack["dt_bias"], sinks=gstack["sinks"],
                   dn_norm_w=gstack["dn_norm_w"], ln1_g=gstack["ln1_g"], ln1_b=gstack["ln1_b"], b_ff1=gstack["b_ff1"],
                   b_ff2=gstack["b_ff2"], ln2_g=gstack["ln2_g"], ln2_b=gstack["ln2_b"], conv_w=gstack["conv_w"])
    shapes = {k: weights[k].shape for k in _REPLICATED}
    shapes["conv_w"] = small_g["conv_w"].shape
    g_all = _all_gather8(_pack_small(small_g))
    no_conv = jnp.zeros(shapes["conv_w"], F32)
    small_out = _small_adam(g_all, _pack_small(dict(weights, conv_w=no_conv)), _pack_small(dict(mom_m, conv_w=no_conv)),
                            _pack_small(dict(mom_v, conv_w=no_conv)))
    small_res = [_unpack_small(t, shapes) for t in small_out]
    g_conv = lax.dynamic_slice_in_dim(small_res[0]["conv_w"], chip * n_cw, n_cw, axis=2)
    res = {"conv_w": _adam_call([g_conv], conv_w, m_conv_w, v_conv_w, "adam_conv_w", tile=16)}

    dmod_all = g_all.reshape(N_DEV, -1)[:, :DEPTH * 6 * D_MODEL].reshape(N_DEV, DEPTH, 6 * D_MODEL)
    dmod_shard = lax.dynamic_slice_in_dim(dmod_all, chip * n_ada, n_ada, axis=2).transpose(1, 0, 2)
    g_w_ada = _ada_bwd(c_all, dmod_shard)
    res["w_ada"], (lower_half,) = _adam_call([g_w_ada], w_ada, m_w_ada, v_w_ada, "adam_w_ada",
                                             exchange=([pending[0][:, half:]], False))
    received[0]["w_in"] = jnp.concatenate([upper_half, lower_half], axis=1)

    partial = [_sum_slots([received[l][k] for l in range(DEPTH)], "sum_" + k) for k in big]
    theirs = _sibling_exchange(partial, "sibling_grads")
    for k, mine, other in zip(big, partial, theirs):
        shape = weights[k].shape
        res[k] = _adam_call([mine.reshape(shape), other.reshape(shape)], weights[k], mom_m[k], mom_v[k], "adam_" + k)
    for k in _REPLICATED:
        res[k] = [small_res[i][k] for i in range(4)]

    outs = [loss, grad_x]
    for i in range(4):
        outs += [res[k][i] for k in _WEIGHT_ORDER]
    return tuple(outs)
```

```python
import functools

import jax
import jax.numpy as jnp
from jax import lax
from jax.experimental import pallas as pl
from jax.experimental.pallas import tpu as pltpu

F32, BF16 = jnp.float32, jnp.bfloat16
HI = lax.Precision.HIGHEST
MESH = pl.DeviceIdType.MESH

D_MODEL = 1024
DEPTH = 4
ATT_KV_HEADS, ATT_GROUP, ATT_HEAD_DIM, WINDOW = 4, 4, 64, 128
DN_HEADS, DN_HEAD_DIM, CONV_K, CHUNK = 8, 128, 4, 64
D_FF = 4 * D_MODEL
D_IN = 7696
ALPHA = (2 * DEPTH) ** 0.25
LN_EPS = 1e-5
RMS_EPS = 1e-6
ADAM_LR, ADAM_B1, ADAM_B2, ADAM_EPS, ADAM_WD, ADAM_STEP = 0.001, 0.9, 0.999, 1e-08, 0.01, 10

N_CHIPS = 4
N_DEV = 8
LANES = 128
D_IN_P = 8192
C_Q, C_DQ, C_DK, C_DV, C_Z, C_GA, C_GB, C_K, C_V, C_BA = 0, 1024, 2048, 3072, 4096, 5120, 6144, 7168, 7424, 7680
NEG = -1e30
VMEM_LIMIT = 56 << 20


def _pc(body, **kw):
    return pl.pallas_call(body, **kw)


def _cparams(sem=None):
    if sem is None:
        return pltpu.CompilerParams(vmem_limit_bytes=VMEM_LIMIT)
    return pltpu.CompilerParams(vmem_limit_bytes=VMEM_LIMIT, dimension_semantics=sem)


_MM_VMEM_BUDGET = 44 << 20
_MM_MIN_TILE = 256


def _mm_tiles(m, n, k, out_bytes):
    def halvings(d):
        out = [d]
        while out[-1] % 2 == 0 and out[-1] // 2 >= _MM_MIN_TILE:
            out.append(out[-1] // 2)
        return out

    best = None
    for tm in halvings(m):
        for tn in halvings(n):
            if 2 * (2 * tm * k + 2 * tn * k + out_bytes * tm * tn) > _MM_VMEM_BUDGET:
                continue
            cost = (2 * m * k + (m // tm) * 2 * n * k, (m // tm) * (n // tn))
            if best is None or cost < best[0]:
                best = (cost, tm, tn)
    assert best is not None, (m, n, k)
    return best[1], best[2]


def _mm(a, b, mode, out_dtype, name, tm=None, tn=None, exchange=None):
    if mode == "nn":
        (m, k), (_, n) = a.shape, b.shape
        dims = (((1,), (0,)), ((), ()))
    elif mode == "nt":
        (m, k), (n, _) = a.shape, b.shape
        dims = (((1,), (1,)), ((), ()))
    else:
        (k, m), (_, n) = a.shape, b.shape
        dims = (((0,), (0,)), ((), ()))
    if tm is None:
        tm, tn = _mm_tiles(m, n, k, jnp.dtype(out_dtype).itemsize)
    tm, tn = min(tm, m), min(tn, n)
    assert m % tm == 0 and n % tn == 0, (name, m, n, tm, tn)
    a_spec = pl.BlockSpec((k, tm), lambda i, j: (0, i)) if mode == "tn" else pl.BlockSpec((tm, k), lambda i, j: (i, 0))
    b_spec = pl.BlockSpec((tn, k), lambda i, j: (j, 0)) if mode == "nt" else pl.BlockSpec((k, tn), lambda i, j: (0, j))

    def body(a_ref, b_ref, o_ref):
        o_ref[...] = lax.dot_general(a_ref[...], b_ref[...], dims, preferred_element_type=F32).astype(o_ref.dtype)

    out_spec = pl.BlockSpec((tm, tn), lambda i, j: (i, j))
    out_shape = jax.ShapeDtypeStruct((m, n), out_dtype)
    if exchange is None:
        return _pc(body, name=name, grid=(m // tm, n // tn), in_specs=[a_spec, b_spec], out_specs=out_spec,
                   out_shape=out_shape, compiler_params=_cparams())(a, b)
    (out,), exchanged = _call_with_exchange(body, name, (m // tm, n // tn), [a_spec, b_spec], [out_spec], [out_shape],
                                            [], (a, b), exchange)
    return out, exchanged


def _row_specs(rows, tile):
    return [pl.BlockSpec((tile, w), functools.partial(lambda i, cb: (i, cb), cb=cb)) for (_, cb, w) in rows]


def _vec_specs(vecs):
    return [pl.BlockSpec(v.shape, lambda i: (0, 0)) for v in vecs]


def _rowwise(fn, rows, vecs, outs, name, tile=512):
    n = rows[0][0].shape[0]
    tile = min(tile, n)
    nr, nv = len(rows), len(vecs)

    def body(*refs):
        rv = [r[...].astype(F32) for r in refs[:nr]]
        vv = [r[...] for r in refs[nr:nr + nv]]
        for o_ref, val in zip(refs[nr + nv:], fn(*rv, *vv)):
            o_ref[...] = val.astype(o_ref.dtype)

    res = _pc(body, name=name, grid=(n // tile,), in_specs=_row_specs(rows, tile) + _vec_specs(vecs),
              out_specs=[pl.BlockSpec((tile, w), lambda i: (i, 0)) for (w, _) in outs],
              out_shape=[jax.ShapeDtypeStruct((n, w), dt) for (w, dt) in outs],
              compiler_params=_cparams())(*[r[0] for r in rows], *vecs)
    return res


def _rowwise_bwd(fn, rows, vecs, cts, row_dtypes, name, tile=512, add=None):
    n = rows[0][0].shape[0]
    tile = min(tile, n)
    nr, nv, nc = len(rows), len(vecs), len(cts)
    want = [i for i, dt in enumerate(row_dtypes) if dt is not None]
    n_add = 0 if add is None else 1

    def body(*refs):
        rv = [r[...].astype(F32) for r in refs[:nr]]
        vv = [r[...] for r in refs[nr:nr + nv]]
        cv = [r[...].astype(F32) for r in refs[nr + nv:nr + nv + nc]]
        pos = nr + nv + nc
        add_ref = refs[pos] if n_add else None
        pos += n_add
        row_out = refs[pos:pos + len(want)]
        vec_out = refs[pos + len(want):]
        _, vjp = jax.vjp(fn, *rv, *vv)
        grads = vjp(tuple(cv))
        for o_ref, i in zip(row_out, want):
            gval = grads[i]
            if n_add and add[0] == i:
                gval = gval + add_ref[...]
            o_ref[...] = gval.astype(o_ref.dtype)

        @pl.when(pl.program_id(0) == 0)
        def _():
            for o_ref in vec_out:
                o_ref[...] = jnp.zeros_like(o_ref)

        for o_ref, gval in zip(vec_out, grads[nr:]):
            o_ref[...] += gval

    ct_rows = [(c, 0, c.shape[1]) for c in cts]
    add_rows = [(add[1], 0, add[1].shape[1])] if n_add else []
    res = _pc(body, name=name, grid=(n // tile,),
              in_specs=_row_specs(rows, tile) + _vec_specs(vecs) + _row_specs(ct_rows + add_rows, tile),
              out_specs=[pl.BlockSpec((tile, rows[i][2]), lambda i_: (i_, 0)) for i in want] + _vec_specs(vecs),
              out_shape=[jax.ShapeDtypeStruct((n, rows[i][2]), row_dtypes[i]) for i in want]
              + [jax.ShapeDtypeStruct(v.shape, F32) for v in vecs],
              compiler_params=_cparams(("arbitrary",)))(*[r[0] for r in rows], *vecs, *cts, *[a[0] for a in add_rows])
    return res[:len(want)], res[len(want):]


def _whole(a, cb=0, w=None):
    return (a, cb, a.shape[1] if w is None else w)


def _ln(x, g, b):
    mu = jnp.mean(x, axis=-1, keepdims=True)
    var = jnp.mean(jnp.square(x - mu), axis=-1, keepdims=True)
    return (x - mu) * lax.rsqrt(var + LN_EPS) * g + b


def _silu(x):
    return x * jax.nn.sigmoid(x)


def _softplus(x):
    return jnp.maximum(x, 0.0) + jnp.log(1.0 + jnp.exp(-jnp.abs(x)))


def _f_mod(x, sc, sh):
    return (x * (1.0 + sc) + sh,)


def _f_gate(ga, gb, ya, yb):
    return (jax.nn.sigmoid(ga) * ya + jax.nn.sigmoid(gb) * yb,)


def _f_post1(x, mixed, gt, g1, b1, sc2, sh2):
    x1 = _ln(ALPHA * x + (1.0 + gt) * mixed, g1, b1)
    return x1, x1 * (1.0 + sc2) + sh2


def _f_act(hpre, b):
    return (jnp.square(jnp.maximum(hpre + b, 0.0)),)


def _f_post2(x1, ff, gt, bff2, g2, b2):
    return (_ln(ALPHA * x1 + (1.0 + gt) * (ff + bff2), g2, b2),)


def _attn_valid(n):
    qi = lax.broadcasted_iota(jnp.int32, (WINDOW, 2 * WINDOW), 0)
    si = lax.broadcasted_iota(jnp.int32, (WINDOW, 2 * WINDOW), 1)
    diff = qi + WINDOW - si
    return (diff >= 0) & (diff < WINDOW) & (n * WINDOW + si - WINDOW >= 0)


def _attn_block(qs, kp, kc, vp, vc, sinks, valid):
    kband = jnp.concatenate([kp, kc], axis=0).astype(BF16)
    vband = jnp.concatenate([vp, vc], axis=0).astype(BF16)
    rng = range(len(qs))
    s = [lax.dot_general(qs[g].astype(BF16), kband, (((1,), (1,)), ((), ())), preferred_element_type=F32) for g in rng]
    s = [jnp.where(valid, s[g] * (ATT_HEAD_DIM ** -0.5), NEG) for g in rng]
    m = [lax.stop_gradient(jnp.maximum(jnp.max(s[g], axis=-1, keepdims=True), sinks[g])) for g in rng]
    p = [jnp.exp(s[g] - m[g]) for g in rng]
    denom = [jnp.sum(p[g], axis=-1, keepdims=True) + jnp.exp(sinks[g] - m[g]) for g in rng]
    probs = [(p[g] / denom[g]).astype(BF16) for g in rng]
    return [jnp.dot(probs[g], vband, preferred_element_type=F32) for g in rng]


def _attn_specs(s):
    nb = s // WINDOW
    q_spec = pl.BlockSpec((1, ATT_GROUP, WINDOW, ATT_HEAD_DIM), lambda h, n: (h, 0, n, 0))
    prev = pl.BlockSpec((1, WINDOW, ATT_HEAD_DIM), lambda h, n: (h, jnp.maximum(n - 1, 0), 0))
    cur = pl.BlockSpec((1, WINDOW, ATT_HEAD_DIM), lambda h, n: (h, n, 0))
    sk = pl.BlockSpec((1, ATT_GROUP, 1, 1), lambda h, n: (h, 0, 0, 0))
    return nb, q_spec, prev, cur, sk


def _attn_fwd(qh, kh, vh, sinks4, exchange=None):
    s = qh.shape[2]
    nb, q_spec, prev, cur, sk = _attn_specs(s)

    def body(q_ref, kp_ref, kc_ref, vp_ref, vc_ref, sk_ref, o_ref):
        valid = _attn_valid(pl.program_id(1))
        heads = range(ATT_GROUP)
        o = _attn_block([q_ref[0, g] for g in heads], kp_ref[0], kc_ref[0], vp_ref[0], vc_ref[0],
                        [sk_ref[0, g] for g in heads], valid)
        for g in heads:
            o_ref[0, g] = o[g].astype(o_ref.dtype)

    (o,), exchanged = _call_with_exchange(
        body, "attn_fwd", (ATT_KV_HEADS, nb), [q_spec, prev, cur, prev, cur, sk], [q_spec],
        [jax.ShapeDtypeStruct(qh.shape, BF16)], [], (qh, kh, kh, vh, vh, sinks4), exchange)
    return o, exchanged


def _attn_bwd(qh, kh, vh, sinks4, doh):
    s = qh.shape[2]
    nb, q_spec, prev, cur, sk = _attn_specs(s)
    acc = pl.BlockSpec((1, s + WINDOW, ATT_HEAD_DIM), lambda h, n: (h, 0, 0))

    def body(q_ref, kp_ref, kc_ref, vp_ref, vc_ref, sk_ref, do_ref, dq_ref, dk_ref, dv_ref, dsk_ref):
        n = pl.program_id(1)
        valid = _attn_valid(n)
        fn = functools.partial(_attn_block, valid=valid)
        heads = range(ATT_GROUP)
        _, vjp = jax.vjp(fn, [q_ref[0, g] for g in heads], kp_ref[0], kc_ref[0], vp_ref[0], vc_ref[0],
                         [sk_ref[0, g] for g in heads])
        dq, dkp, dkc, dvp, dvc, dsk = vjp([do_ref[0, g].astype(F32) for g in heads])
        for g in heads:
            dq_ref[0, g] = dq[g]

        @pl.when(n == 0)
        def _():
            dk_ref[...] = jnp.zeros_like(dk_ref)
            dv_ref[...] = jnp.zeros_like(dv_ref)
            dsk_ref[...] = jnp.zeros_like(dsk_ref)

        band = pl.ds(pl.multiple_of(n * WINDOW, WINDOW), 2 * WINDOW)
        dk_ref[0, band, :] += jnp.concatenate([dkp, dkc], axis=0)
        dv_ref[0, band, :] += jnp.concatenate([dvp, dvc], axis=0)
        for g in heads:
            dsk_ref[0, g] += dsk[g]

    kv_shape = jax.ShapeDtypeStruct((ATT_KV_HEADS, s + WINDOW, ATT_HEAD_DIM), F32)
    return _pc(body, name="attn_bwd", grid=(ATT_KV_HEADS, nb), in_specs=[q_spec, prev, cur, prev, cur, sk, q_spec],
               out_specs=[q_spec, acc, acc, sk],
               out_shape=[jax.ShapeDtypeStruct(qh.shape, F32), kv_shape, kv_shape, jax.ShapeDtypeStruct(sinks4.shape, F32)],
               compiler_params=_cparams(("arbitrary", "arbitrary")))(qh, kh, kh, vh, vh, sinks4, doh)


def _bdot(a, b, dims=(((1,), (0,)), ((), ()))):
    return lax.dot_general(a.astype(BF16), b.astype(BF16), dims, preferred_element_type=F32)


def _hdot(a, b, dims=(((1,), (0,)), ((), ()))):
    return lax.dot_general(a, b, dims, precision=HI, preferred_element_type=F32)


_NN = (((1,), (0,)), ((), ()))
_NT = (((1,), (1,)), ((), ()))
_TN = (((0,), (0,)), ((), ()))


def _split2(a):
    hi = a.astype(BF16)
    return hi, (a - hi.astype(F32)).astype(BF16)


def _dot3(a, b, dims):
    ah, al = _split2(a)
    bh, bl = _split2(b)
    d = lambda p, q: lax.dot_general(p, q, dims, preferred_element_type=F32)
    return d(ah, bh) + (d(ah, bl) + d(al, bh))


@jax.custom_vjp
def _xdot(a, b):
    return _dot3(a, b, _NN)


def _xdot_fwd(a, b):
    return _dot3(a, b, _NN), (a, b)


def _xdot_bwd(res, g):
    a, b = res
    return _dot3(g, b, _NT), _dot3(a, g, _TN)


_xdot.defvjp(_xdot_fwd, _xdot_bwd)


def _mask_dot(mask16, b, dims):
    hi = b.astype(BF16)
    r = b - hi.astype(F32)
    mid = r.astype(BF16)
    lo = (r - mid.astype(F32)).astype(BF16)
    d = lambda q: lax.dot_general(mask16, q, dims, preferred_element_type=F32)
    return d(hi) + (d(mid) + d(lo))


def _chunk_masks():
    r = lax.broadcasted_iota(jnp.int32, (CHUNK, CHUNK), 0)
    c = lax.broadcasted_iota(jnp.int32, (CHUNK, CHUNK), 1)
    return r >= c, r > c, (r == c).astype(F32)


def _dn_local(qs, ks, vs, bs, gs, masks):
    causal, strict, eye = masks
    rng = range(len(qs))
    gb = [jnp.broadcast_to(gs[i], (CHUNK, CHUNK)) for i in rng]
    decay = [jnp.exp(jnp.where(causal, gb[i] - gb[i].T, NEG)) for i in rng]
    kb = [ks[i] * bs[i] for i in rng]
    vb = [vs[i] * bs[i] for i in rng]
    kk = [_bdot(kb[i], ks[i], _NT) for i in rng]
    p = [-jnp.where(strict, kk[i] * decay[i], 0.0) for i in rng]
    t = [eye + p[i] for i in rng]
    for _ in range(5):
        p = [_xdot(p[i], p[i]) for i in rng]
        t = [t[i] + _xdot(p[i], t[i]) for i in rng]
    eg = [jnp.exp(gs[i]) for i in rng]
    u = [_xdot(t[i], vb[i]) for i in rng]
    w = [_xdot(t[i], kb[i] * eg[i]) for i in rng]
    qk = [_bdot(qs[i], ks[i], _NT) for i in rng]
    intra = [qk[i] * decay[i] for i in rng]
    q_dec = [qs[i] * eg[i] for i in rng]
    k_dec = [ks[i] * jnp.exp(gs[i][CHUNK - 1:CHUNK, :] - gs[i]) for i in rng]
    return u, w, intra, q_dec, k_dec


def _dn_state_bwd(u, w, intra, q_dec, k_dec, gcum, state, do, dnext):
    last = jnp.exp(gcum[CHUNK - 1:CHUNK, :])
    x = _bdot(k_dec, dnext)
    t1 = _bdot(intra, do, _TN)
    v_new = u - _bdot(w, state)
    dqd = _bdot(do, state, _NT)
    din = _bdot(do, v_new, _NT)
    dkd = _bdot(v_new, dnext, _NT)
    base = _bdot(q_dec, do, _TN) - _bdot(w, t1, _TN)
    d_vnew = t1 + x
    dw = -_bdot(d_vnew, state, _NT)
    dstate = dnext * last + base - _bdot(w, x, _TN)
    dlast = jnp.sum(jnp.sum(state * dnext, axis=1, keepdims=True), axis=0, keepdims=True)
    row = lax.broadcasted_iota(jnp.int32, (CHUNK, 1), 0)
    dgc = jnp.where(row == CHUNK - 1, dlast * last, 0.0)
    return d_vnew, dw, din, dqd, dkd, dgc, dstate


def _l2norm(t):
    return t * lax.rsqrt(jnp.sum(jnp.square(t), axis=-1, keepdims=True) + RMS_EPS)


def _dn_pre(aq, ak, av, ba, alog, dtb, h):
    lane = lax.broadcasted_iota(jnp.int32, (1, LANES), 1)
    pick = lambda t, i: jnp.sum(jnp.where(lane == i, t, 0.0), axis=1, keepdims=True)
    q = _l2norm(_silu(aq)) * (DN_HEAD_DIM ** -0.5)
    k = _l2norm(_silu(ak))
    v = _silu(av)
    beta = jax.nn.sigmoid(pick(ba, h))
    g = -jnp.exp(pick(alog, h)) * _softplus(pick(ba, h + DN_HEADS) + pick(dtb, h))
    return q, k, v, beta, g


def _dn_post(o, z, nw):
    o = o * lax.rsqrt(jnp.mean(jnp.square(o), axis=-1, keepdims=True) + RMS_EPS) * nw
    return o * _silu(z)


_PAD = 8
_TOK_TILE = 512


def _pad_front(pad_ref, x_ref, s):
    pad_ref[pl.ds(0, _PAD), :] = jnp.zeros((_PAD, pad_ref.shape[1]), F32)
    pad_ref[pl.ds(_PAD, s), :] = x_ref[...]


def _conv_tile(pad_ref, w4, r0, n):
    acc = None
    for j in range(CONV_K):
        term = pad_ref[pl.ds(r0 + _PAD - (CONV_K - 1) + j, n), :] * w4[j:j + 1, :]
        acc = term if acc is None else acc + term
    return acc


def _conv_tile_bwd(pad_ref, da_ref, w4, r0, n):
    dx, dw = None, []
    da = da_ref[pl.ds(r0, n), :]
    for j in range(CONV_K):
        term = da_ref[pl.ds(r0 + CONV_K - 1 - j, n), :] * w4[j:j + 1, :]
        dx = term if dx is None else dx + term
        dw.append(jnp.sum(da * pad_ref[pl.ds(r0 + _PAD - (CONV_K - 1) + j, n), :], axis=0, keepdims=True))
    return dx, jnp.concatenate(dw, axis=0)


def _dn_gcum(g_c, causal_f):
    return _mask_dot(causal_f, jnp.broadcast_to(g_c, (CHUNK, LANES)), _NN)[:, 0:1]


def _dn_in_specs(s):
    col = lambda base: pl.BlockSpec((s, DN_HEAD_DIM), functools.partial(lambda h, b: (0, b + h), b=base // DN_HEAD_DIM))
    cw = lambda base: pl.BlockSpec((CONV_K, DN_HEAD_DIM), functools.partial(lambda h, b: (0, b + h), b=base))
    row = pl.BlockSpec((1, LANES), lambda h: (0, 0))
    ba = pl.BlockSpec((s, LANES), lambda h: (0, C_BA // LANES))
    return [col(C_DQ), col(C_DK), col(C_DV), col(C_Z), ba, cw(0), cw(DN_HEADS), cw(2 * DN_HEADS), row, row, row]


def _chunk_rows(c):
    return pl.ds(pl.multiple_of(c * CHUNK, CHUNK), CHUNK)


def _group(nchunk, want):
    g = min(want, nchunk)
    assert nchunk % g == 0
    return g


def _dn_forward_scan(q_s, k_s, v_s, b_s, g_s, gc_s, loc, o_s, states_ref, s):
    masks = _chunk_masks()
    causal_f = masks[0].astype(BF16)
    nchunk = s // CHUNK
    grp = _group(nchunk, 8)
    u_s, w_s, in_s, qd_s, kd_s = loc

    def local_step(i, carry):
        rows = [_chunk_rows(i * grp + j) for j in range(grp)]
        gcum = [_dn_gcum(g_s[r, :], causal_f) for r in rows]
        u, w, intra, q_dec, k_dec = _dn_local([q_s[r, :] for r in rows], [k_s[r, :] for r in rows],
                                              [v_s[r, :] for r in rows], [b_s[r, :] for r in rows], gcum, masks)
        for j, r in enumerate(rows):
            gc_s[r, :] = gcum[j]
            u_s[r, :] = u[j]
            w_s[r, :] = w[j].astype(w_s.dtype)
            in_s[r, :] = intra[j].astype(in_s.dtype)
            qd_s[r, :] = q_dec[j].astype(qd_s.dtype)
            kd_s[r, :] = k_dec[j].astype(kd_s.dtype)
        return carry

    lax.fori_loop(0, nchunk // grp, local_step, 0)

    def state_step(i, state):
        rows = _chunk_rows(i)
        states_ref[i] = state
        v_new = u_s[rows, :] - _bdot(w_s[rows, :], state)
        o_s[rows, :] = _bdot(qd_s[rows, :], state) + _bdot(in_s[rows, :], v_new)
        last = jnp.exp(gc_s[rows, :][CHUNK - 1:CHUNK, :])
        return state * last + _bdot(kd_s[rows, :], v_new, _TN)

    lax.fori_loop(0, nchunk, state_step, jnp.zeros((DN_HEAD_DIM, DN_HEAD_DIM), F32))


def _dn_saved_shapes(s):
    d, h = DN_HEAD_DIM, DN_HEADS
    shapes = [((h, s, d), F32), ((h, s, d), BF16), ((h, s, CHUNK), BF16), ((h, s, d), BF16), ((h, s, d), BF16),
              ((h, s, 1), F32), ((h, s // CHUNK, d, d), F32), ((h, s, d), F32)]
    return [jax.ShapeDtypeStruct(shp, dt) for shp, dt in shapes]


def _dn_saved_specs(s, **kw):
    return [pl.BlockSpec((1,) + t.shape[1:], functools.partial(lambda h, nd: (h,) + (0,) * nd, nd=len(t.shape) - 1), **kw)
            for t in _dn_saved_shapes(s)]


def _call_with_exchange(body, name, steps, in_specs, out_specs, out_shape, scratch, args, exchange):
    steps = (steps,) if isinstance(steps, int) else tuple(steps)
    params = _cparams(("arbitrary",) * len(steps))
    if exchange is None:
        res = _pc(body, name=name, grid=steps, in_specs=in_specs, out_specs=out_specs, out_shape=out_shape,
                  scratch_shapes=scratch, compiler_params=params)(*args)
        return res, None
    arrays, gather = exchange
    x_in, x_out, x_shape, x_scratch = _exchange_specs(arrays, gather)
    wrapped = _carry_exchange(body, len(in_specs), len(out_specs), len(scratch), len(arrays), gather, steps)
    res = _pc(wrapped, name=name + "_x", grid=steps, in_specs=in_specs + x_in, out_specs=out_specs + x_out,
              out_shape=out_shape + x_shape, scratch_shapes=scratch + x_scratch, compiler_params=params)(*args, *arrays)
    return res[:len(out_specs)], res[len(out_specs):]


def _dn_fwd(proj, conv_w, alog, dtb, nw, exchange=None):
    s = proj.shape[0]
    d = DN_HEAD_DIM

    tt = min(_TOK_TILE, s)

    def body(xq, xk, xv, z, ba, wq, wk, wv, alog_r, dtb_r, nw_r, o_ref, u_o, w_o, in_o, qd_o, kd_o, gc_o, st_o, oraw_o,
             padq, padk, padv, q_s, k_s, v_s, b_s, g_s):
        h = pl.program_id(0)
        loc = [r.at[0] for r in (u_o, w_o, in_o, qd_o, kd_o)]
        gc_s, states, o_s = gc_o.at[0], st_o.at[0], oraw_o.at[0]
        _pad_front(padq, xq, s)
        _pad_front(padk, xk, s)
        _pad_front(padv, xv, s)
        for r0 in range(0, s, tt):
            rows = pl.ds(r0, tt)
            aq, ak, av = _conv_tile(padq, wq[...], r0, tt), _conv_tile(padk, wk[...], r0, tt), _conv_tile(padv, wv[...], r0, tt)
            q_s[rows, :], k_s[rows, :], v_s[rows, :], b_s[rows, :], g_s[rows, :] = _dn_pre(
                aq, ak, av, ba[rows, :], alog_r[...], dtb_r[...], h)
        _dn_forward_scan(q_s, k_s, v_s, b_s, g_s, gc_s, loc, o_s, states, s)
        for r0 in range(0, s, tt):
            rows = pl.ds(r0, tt)
            o_ref[rows, :] = _dn_post(o_s[rows, :], z[rows, :], nw_r[...]).astype(o_ref.dtype)

    big = pltpu.VMEM((s, d), F32)
    thin = pltpu.VMEM((s, 1), F32)
    padded = pltpu.VMEM((s + _PAD, d), F32)
    return _call_with_exchange(
        body, "dn_fwd", DN_HEADS, _dn_in_specs(s), [pl.BlockSpec((s, d), lambda h: (0, h))] + _dn_saved_specs(s),
        [jax.ShapeDtypeStruct((s, DN_HEADS * d), BF16)] + _dn_saved_shapes(s),
        [padded, padded, padded, big, big, big, thin, thin],
        (proj, proj, proj, proj, proj, conv_w, conv_w, conv_w, alog, dtb, nw), exchange)


def _dn_bwd(proj, conv_w, alog, dtb, nw, dob, kept, exchange=None):
    s = proj.shape[0]
    d = DN_HEAD_DIM
    nchunk = s // CHUNK

    tt = min(_TOK_TILE, s)

    def body(xq, xk, xv, z, ba, wq, wk, wv, alog_r, dtb_r, nw_r, dob_ref, u_i, w_i, in_i, qd_i, kd_i, gc_i, st_i, oraw_i,
             dxq, dxk, dxv, dz, dba, dwq, dwk, dwv, dalog, ddtb, dnw,
             padq, padk, padv, q_s, k_s, v_s, b_s, g_s, o_s, dq_s, dk_s, dv_s, db_s, dg_s, dkd_s, din_s, dgc_s):
        h = pl.program_id(0)
        u_s, w_s, in_s, qd_s, kd_s = [r.at[0] for r in (u_i, w_i, in_i, qd_i, kd_i)]
        gc_s, states, oraw = gc_i.at[0], st_i.at[0], oraw_i.at[0]
        masks = _chunk_masks()
        causal_f = masks[0].astype(BF16)
        pre = functools.partial(_dn_pre, h=h)
        _pad_front(padq, xq, s)
        _pad_front(padk, xk, s)
        _pad_front(padv, xv, s)

        def conv_tiles(r0):
            return _conv_tile(padq, wq[...], r0, tt), _conv_tile(padk, wk[...], r0, tt), _conv_tile(padv, wv[...], r0, tt)

        for r0 in range(0, s, tt):
            rows = pl.ds(r0, tt)
            q_s[rows, :], k_s[rows, :], v_s[rows, :], b_s[rows, :], g_s[rows, :] = pre(
                *conv_tiles(r0), ba[rows, :], alog_r[...], dtb_r[...])
        dnw_v = jnp.zeros((1, LANES), F32)
        for r0 in range(0, s, tt):
            rows = pl.ds(r0, tt)
            _, post_vjp = jax.vjp(_dn_post, oraw[rows, :], z[rows, :], nw_r[...])
            do_raw, dz_v, dnw_t = post_vjp(dob_ref[rows, :].astype(F32))
            dz[rows, :] = dz_v.astype(dz.dtype)
            o_s[rows, :] = do_raw
            dnw_v = dnw_v + dnw_t

        def state_step(i, dstate):
            c = nchunk - 1 - i
            rows = _chunk_rows(c)
            du, dw, din, dqd, dkd, dgc, dstate = _dn_state_bwd(
                u_s[rows, :], w_s[rows, :], in_s[rows, :], qd_s[rows, :], kd_s[rows, :], gc_s[rows, :], states[c],
                o_s[rows, :], dstate)
            dq_s[rows, :] = du
            dk_s[rows, :] = dw
            dv_s[rows, :] = dqd
            dkd_s[rows, :] = dkd
            din_s[rows, :] = din
            dgc_s[rows, :] = dgc
            return dstate

        lax.fori_loop(0, nchunk, state_step, jnp.zeros((d, d), F32))
        local = functools.partial(_dn_local, masks=masks)
        grp = _group(nchunk, 8)

        def local_step(i, carry):
            rows = [_chunk_rows(i * grp + j) for j in range(grp)]
            get = lambda ref: [ref[r, :] for r in rows]
            _, vjp = jax.vjp(local, get(q_s), get(k_s), get(v_s), get(b_s), get(gc_s))
            dq_c, dk_c, dv_c, db_c, dgc_c = vjp((get(dq_s), get(dk_s), get(din_s), get(dv_s), get(dkd_s)))
            dgc_c = [dgc_c[j] + dgc_s[r, :] for j, r in enumerate(rows)]
            dg_c = [_mask_dot(causal_f, jnp.broadcast_to(t, (CHUNK, LANES)), _TN)[:, 0:1] for t in dgc_c]
            for j, r in enumerate(rows):
                dq_s[r, :] = dq_c[j]
                dk_s[r, :] = dk_c[j]
                dv_s[r, :] = dv_c[j]
                db_s[r, :] = db_c[j]
                dg_s[r, :] = dg_c[j]
            return carry

        lax.fori_loop(0, nchunk // grp, local_step, 0)

        @pl.when(h == 0)
        def _():
            dba[...] = jnp.zeros_like(dba)
            dalog[...] = jnp.zeros_like(dalog)
            ddtb[...] = jnp.zeros_like(ddtb)
            dnw[...] = jnp.zeros_like(dnw)

        dalog_v = jnp.zeros((1, LANES), F32)
        ddtb_v = jnp.zeros((1, LANES), F32)
        for r0 in range(0, s, tt):
            rows = pl.ds(r0, tt)
            _, pre_vjp = jax.vjp(pre, *conv_tiles(r0), ba[rows, :], alog_r[...], dtb_r[...])
            daq, dak, dav, dba_t, dalog_t, ddtb_t = pre_vjp(
                (dq_s[rows, :], dk_s[rows, :], dv_s[rows, :], db_s[rows, :], dg_s[rows, :]))
            dq_s[rows, :], dk_s[rows, :], dv_s[rows, :] = daq, dak, dav
            dba[rows, :] += dba_t
            dalog_v = dalog_v + dalog_t
            ddtb_v = ddtb_v + ddtb_t
        tail = pl.ds(s, _PAD)
        dq_s[tail, :] = dk_s[tail, :] = dv_s[tail, :] = jnp.zeros((_PAD, d), F32)
        for pad, da_s, w_ref, dx_ref, dw_ref in ((padq, dq_s, wq, dxq, dwq), (padk, dk_s, wk, dxk, dwk), (padv, dv_s, wv, dxv, dwv)):
            dw_acc = jnp.zeros((CONV_K, d), F32)
            for r0 in range(0, s, tt):
                dx_t, dw_t = _conv_tile_bwd(pad, da_s, w_ref[...], r0, tt)
                dx_ref[pl.ds(r0, tt), :] = dx_t.astype(dx_ref.dtype)
                dw_acc = dw_acc + dw_t
            dw_ref[...] = dw_acc
        dalog[...] += dalog_v
        ddtb[...] += ddtb_v
        dnw[...] += dnw_v

    big = pltpu.VMEM((s, d), F32)
    thin = pltpu.VMEM((s, 1), F32)
    padded = pltpu.VMEM((s + _PAD, d), F32)
    w_all = DN_HEADS * d
    col_out = lambda: pl.BlockSpec((s, d), lambda h: (0, h))
    cw_out = lambda: pl.BlockSpec((CONV_K, d), lambda h: (0, h))
    row = lambda: pl.BlockSpec((1, LANES), lambda h: (0, 0))
    big_out = jax.ShapeDtypeStruct((s, w_all), BF16)
    cw_shape = jax.ShapeDtypeStruct((CONV_K, w_all), F32)
    row_shape = jax.ShapeDtypeStruct((1, LANES), F32)
    return _call_with_exchange(
        body, "dn_bwd", DN_HEADS,
        _dn_in_specs(s) + [pl.BlockSpec((s, d), lambda h: (0, h))] + _dn_saved_specs(s, pipeline_mode=pl.Buffered(1)),
        [col_out(), col_out(), col_out(), col_out(), pl.BlockSpec((s, LANES), lambda h: (0, 0)),
         cw_out(), cw_out(), cw_out(), row(), row(), row()],
        [big_out, big_out, big_out, big_out, jax.ShapeDtypeStruct((s, LANES), F32),
         cw_shape, cw_shape, cw_shape, row_shape, row_shape, row_shape],
        [padded, padded, padded, big, big, big, thin, thin, big,
         padded, padded, padded, thin, thin, big, pltpu.VMEM((s, CHUNK), F32), thin],
        (proj, proj, proj, proj, proj, conv_w, conv_w, conv_w, alog, dtb, nw, dob, *kept), exchange)


def _loss_head(y, target, tile=256):
    n, dm = y.shape
    tile = min(tile, n)

    def body(y_ref, t_ref, dy_ref, loss_ref):
        err = y_ref[...] - t_ref[...]
        dy_ref[...] = err * (1.0 / dm)

        @pl.when(pl.program_id(0) == 0)
        def _():
            loss_ref[...] = jnp.zeros_like(loss_ref)

        loss_ref[...] += 0.5 * jnp.sum(jnp.mean(jnp.square(err), axis=-1, keepdims=True), axis=0, keepdims=True)

    blk = pl.BlockSpec((tile, dm), lambda i: (i, 0))
    return _pc(body, name="loss_head", grid=(n // tile,), in_specs=[blk, blk],
               out_specs=[blk, pl.BlockSpec((1, 1), lambda i: (0, 0))],
               out_shape=[jax.ShapeDtypeStruct((n, dm), F32), jax.ShapeDtypeStruct((1, 1), F32)],
               compiler_params=_cparams(("arbitrary",)))(y, target)


def _ada_fwd(c_all, w_ada, b_shard):
    nl, dm, n = w_ada.shape

    def body(c_ref, w_ref, b_ref, o_ref):
        ca = _silu(c_ref[...]).astype(BF16)
        o_ref[0] = jnp.dot(ca, w_ref[0].astype(BF16), preferred_element_type=F32) + b_ref[0]

    return _pc(body, name="ada_fwd", grid=(nl,),
               in_specs=[pl.BlockSpec((N_DEV, dm), lambda l: (0, 0)), pl.BlockSpec((1, dm, n), lambda l: (l, 0, 0)),
                         pl.BlockSpec((1, 1, n), lambda l: (l, 0, 0))],
               out_specs=pl.BlockSpec((1, N_DEV, n), lambda l: (l, 0, 0)),
               out_shape=jax.ShapeDtypeStruct((nl, N_DEV, n), F32), compiler_params=_cparams())(c_all, w_ada, b_shard)


def _ada_bwd(c_all, dmod):
    nl, _, n = dmod.shape
    dm = c_all.shape[1]

    def body(c_ref, d_ref, o_ref):
        o_ref[0] = _hdot(_silu(c_ref[...]), d_ref[0], _TN)

    return _pc(body, name="ada_bwd", grid=(nl,),
               in_specs=[pl.BlockSpec((N_DEV, dm), lambda l: (0, 0)), pl.BlockSpec((1, N_DEV, n), lambda l: (l, 0, 0))],
               out_specs=pl.BlockSpec((1, dm, n), lambda l: (l, 0, 0)),
               out_shape=jax.ShapeDtypeStruct((nl, dm, n), F32), compiler_params=_cparams())(c_all, dmod)


def _adamw(g, w, m, v):
    m = ADAM_B1 * m + (1.0 - ADAM_B1) * g
    v = ADAM_B2 * v + (1.0 - ADAM_B2) * jnp.square(g)
    m_hat = m / (1.0 - ADAM_B1 ** ADAM_STEP)
    v_hat = v / (1.0 - ADAM_B2 ** ADAM_STEP)
    delta = -ADAM_LR * (m_hat / (jnp.sqrt(v_hat) + ADAM_EPS) + ADAM_WD * w)
    return delta, m, v


def _adam_call(parts, w, m, v, name, tile=128):
    shape = w.shape
    flat = lambda t: t.reshape(-1, shape[-1])
    width = shape[-1]

    def fn(*vals):
        g = vals[0] if len(parts) == 1 else vals[0] + vals[1]
        return (g,) + _adamw(g, *vals[len(parts):])

    rows = [_whole(flat(t)) for t in (*parts, w, m, v)]
    outs = _rowwise(fn, rows, [], [(width, F32)] * 4, name, tile=tile)
    return [o.reshape(shape) for o in outs]


def _sum_slots(per_layer, name, tile=128):
    nl = len(per_layer)
    _, n, width = per_layer[0].shape
    tile = min(tile, n)
    nt = n // tile

    def body(*refs):
        o_ref = refs[nl]
        for lp in range(nl):
            @pl.when(pl.program_id(0) == lp)
            def _(r_ref=refs[lp]):
                acc = r_ref[0].astype(F32)
                for j in range(1, N_CHIPS):
                    acc = acc + r_ref[j].astype(F32)
                o_ref[...] = acc

    in_specs = [pl.BlockSpec((N_CHIPS, tile, width), functools.partial(lambda l, t, lp: (0, jnp.where(l == lp, t, 0), 0), lp=lp))
                for lp in range(nl)]
    return _pc(body, name=name, grid=(nl, nt), in_specs=in_specs,
               out_specs=pl.BlockSpec((tile, width), lambda l, t: (l * nt + t, 0)),
               out_shape=jax.ShapeDtypeStruct((nl * n, width), F32), compiler_params=_cparams())(*per_layer)


def _small_adam(g_all, w, m, v):
    def body(g_ref, w_ref, m_ref, v_ref, og, od, om, ov):
        g = g_ref[0]
        for j in range(1, N_DEV):
            g = g + g_ref[j]
        og[...] = g
        od[...], om[...], ov[...] = _adamw(g, w_ref[...], m_ref[...], v_ref[...])

    vm = pl.BlockSpec(memory_space=pltpu.VMEM)
    shp = jax.ShapeDtypeStruct(w.shape, F32)
    return _pc(body, name="small_adam", in_specs=[vm] * 4, out_specs=[vm] * 4, out_shape=[shp] * 4,
               compiler_params=_cparams())(g_all, w, m, v)


def _place():
    return lax.axis_index("x"), lax.axis_index("y"), lax.axis_index("c")


def _flip(v, bit):
    return 1 - v if bit else v


def _all_gather8(a):
    r, n = a.shape

    def body(a_ref, o_ref, send_sems, recv_sems):
        x, y, c = _place()
        me = 4 * x + 2 * y + c
        o_ref[me] = a_ref[...]
        copies = []
        for k in range(1, N_DEV):
            peer = (_flip(x, k & 4), _flip(y, k & 2), _flip(c, k & 1))
            copies.append(pltpu.make_async_remote_copy(
                src_ref=a_ref, dst_ref=o_ref.at[me], send_sem=send_sems.at[k - 1], recv_sem=recv_sems.at[k - 1],
                device_id=peer, device_id_type=MESH))
        for cp in copies:
            cp.start()
        for k in range(1, N_DEV):
            px, py, pc_ = _flip(x, k & 4), _flip(y, k & 2), _flip(c, k & 1)
            pltpu.make_async_remote_copy(
                src_ref=a_ref, dst_ref=o_ref.at[4 * px + 2 * py + pc_], send_sem=send_sems.at[k - 1],
                recv_sem=recv_sems.at[k - 1], device_id=(px, py, pc_), device_id_type=MESH).wait_recv()
        for cp in copies:
            cp.wait_send()

    vm = pl.BlockSpec(memory_space=pltpu.VMEM)
    return _pc(body, name="all_gather8", in_specs=[vm], out_specs=vm,
               out_shape=jax.ShapeDtypeStruct((N_DEV, r, n), a.dtype),
               scratch_shapes=[pltpu.SemaphoreType.DMA((N_DEV - 1,)), pltpu.SemaphoreType.DMA((N_DEV - 1,))],
               compiler_params=_cparams())(a)


def _chip_exchange(arrays, gather, name):
    na = len(arrays)

    def body(*refs):
        ins, outs, sems = refs[:na], refs[na:2 * na], refs[2 * na:]
        _exchange_copies(ins, outs, sems, gather, start=True)
        _exchange_copies(ins, outs, sems, gather, start=False)

    in_specs, out_specs, out_shape, scratch = _exchange_specs(arrays, gather)
    return _pc(body, name=name, in_specs=in_specs, out_specs=out_specs, out_shape=out_shape, scratch_shapes=scratch,
               compiler_params=_cparams())(*arrays)


def _exchange_specs(arrays, gather):
    na = len(arrays)
    hbm = pl.BlockSpec(memory_space=pl.ANY)
    out_shape = [jax.ShapeDtypeStruct(((N_CHIPS,) + a.shape) if gather else a.shape, a.dtype) for a in arrays]
    n_remote = 4 if gather else 2
    scratch = [pltpu.SemaphoreType.DMA((3 * na,))] * n_remote + [pltpu.SemaphoreType.DMA((na,))]
    return [hbm] * na, [hbm] * na, out_shape, scratch


def _gather_copies(ins, outs, sems, start):
    send_i, recv_i, send_d, recv_d, local_sems = sems
    x, y, c = _place()
    me = 2 * x + y
    sibling = (x, y, 1 - c)
    ici_sends, ici_arrivals, hand_ons, hand_arrivals, locals_ = [], [], [], [], []
    for i in range(len(ins)):
        half = ins[i].shape[0] // 2
        mine, other = pl.ds(c * half, half), pl.ds((1 - c) * half, half)
        locals_.append(pltpu.make_async_copy(ins[i], outs[i].at[me], local_sems.at[i]))
        for j in range(1, N_CHIPS):
            px, py = _flip(x, j & 2), _flip(y, j & 1)
            peer = 2 * px + py
            k = i * 3 + j - 1
            ici = dict(send_sem=send_i.at[k], recv_sem=recv_i.at[k], device_id=(px, py, c), device_id_type=MESH)
            d2d = dict(send_sem=send_d.at[k], recv_sem=recv_d.at[k], device_id=sibling, device_id_type=MESH)
            ici_sends.append(pltpu.make_async_remote_copy(src_ref=ins[i].at[mine], dst_ref=outs[i].at[me, mine], **ici))
            ici_arrivals.append(pltpu.make_async_remote_copy(src_ref=ins[i].at[mine], dst_ref=outs[i].at[peer, mine], **ici))
            hand_ons.append(pltpu.make_async_remote_copy(
                src_ref=outs[i].at[peer, mine], dst_ref=outs[i].at[peer, mine], **d2d))
            hand_arrivals.append(pltpu.make_async_remote_copy(
                src_ref=outs[i].at[peer, other], dst_ref=outs[i].at[peer, other], **d2d))
    if start:
        for cp in locals_ + ici_sends:
            cp.start()
    else:
        for arrival, hand_on in zip(ici_arrivals, hand_ons):
            arrival.wait_recv()
            hand_on.start()
        for cp in hand_arrivals:
            cp.wait_recv()
        for cp in ici_sends + hand_ons:
            cp.wait_send()
        for cp in locals_:
            cp.wait()


def _exchange_copies(ins, outs, sems, gather, start):
    if gather:
        return _gather_copies(ins, outs, sems, start)
    send_sems, recv_sems, local_sems = sems
    x, y, c = _place()
    me = 2 * x + y
    sends, arrivals, locals_ = [], [], []
    for i in range(len(ins)):
        locals_.append(pltpu.make_async_copy(ins[i] if gather else ins[i].at[me], outs[i].at[me], local_sems.at[i]))
        for j in range(1, N_CHIPS):
            px, py = _flip(x, j & 2), _flip(y, j & 1)
            peer = 2 * px + py
            pair = dict(send_sem=send_sems.at[i * 3 + j - 1], recv_sem=recv_sems.at[i * 3 + j - 1],
                        device_id=(px, py, c), device_id_type=MESH)
            sends.append(pltpu.make_async_remote_copy(
                src_ref=ins[i] if gather else ins[i].at[peer], dst_ref=outs[i].at[me], **pair))
            arrivals.append(pltpu.make_async_remote_copy(
                src_ref=ins[i] if gather else ins[i].at[me], dst_ref=outs[i].at[peer], **pair))
    if start:
        for cp in locals_ + sends:
            cp.start()
    else:
        for cp in arrivals:
            cp.wait_recv()
        for cp in sends:
            cp.wait_send()
        for cp in locals_:
            cp.wait()


def _carry_exchange(body, n_in, n_out, n_scratch, n_arrays, gather, steps):
    def wrapped(*refs):
        na = n_arrays
        ins, xin = refs[:n_in], refs[n_in:n_in + na]
        outs = refs[n_in + na:n_in + na + n_out]
        xout = refs[n_in + na + n_out:n_in + 2 * na + n_out]
        rest = refs[n_in + 2 * na + n_out:]
        scratch, sems = rest[:n_scratch], rest[n_scratch:]
        first, last = True, True
        for axis, n in enumerate(steps):
            first = jnp.logical_and(first, pl.program_id(axis) == 0)
            last = jnp.logical_and(last, pl.program_id(axis) == n - 1)

        @pl.when(first)
        def _():
            _exchange_copies(xin, xout, sems, gather, start=True)

        body(*ins, *outs, *scratch)

        @pl.when(last)
        def _():
            _exchange_copies(xin, xout, sems, gather, start=False)

    return wrapped


def _sibling_exchange(arrays, name):
    na = len(arrays)

    def body(*refs):
        ins, outs = refs[:na], refs[na:2 * na]
        send_sems, recv_sems = refs[2 * na:]
        x, y, c = _place()
        copies = [pltpu.make_async_remote_copy(
            src_ref=ins[i], dst_ref=outs[i], send_sem=send_sems.at[i], recv_sem=recv_sems.at[i],
            device_id=(x, y, 1 - c), device_id_type=MESH) for i in range(na)]
        for cp in copies:
            cp.start()
        for cp in copies:
            cp.wait()

    hbm = pl.BlockSpec(memory_space=pl.ANY)
    return _pc(body, name=name, in_specs=[hbm] * na, out_specs=[hbm] * na,
               out_shape=[jax.ShapeDtypeStruct(a.shape, a.dtype) for a in arrays],
               scratch_shapes=[pltpu.SemaphoreType.DMA((na,)), pltpu.SemaphoreType.DMA((na,))],
               compiler_params=_cparams())(*arrays)


def _heads_q(t):
    s = t.shape[0]
    return t.reshape(s, ATT_KV_HEADS, ATT_GROUP, ATT_HEAD_DIM).transpose(1, 2, 0, 3)


def _unheads_q(t):
    s = t.shape[2]
    return t.transpose(2, 0, 1, 3).reshape(s, ATT_KV_HEADS * ATT_GROUP * ATT_HEAD_DIM)


def _heads_kv(t):
    s = t.shape[0]
    return t.reshape(s, ATT_KV_HEADS, ATT_HEAD_DIM).transpose(1, 0, 2)


def _unheads_kv(t):
    s = t.shape[1]
    return t.transpose(1, 0, 2).reshape(s, ATT_KV_HEADS * ATT_HEAD_DIM)


def _row128(v):
    return jnp.pad(v, (0, LANES - v.shape[0])).reshape(1, LANES)


def _layer_fwd(x, p, fetch=None):
    p = dict(p)
    arrived = {}
    sh1, sc1, gt1, sh2, sc2, gt2 = [p["mod"][i] for i in range(6)]
    (u,) = _rowwise(_f_mod, [_whole(x)], [sc1, sh1], [(D_MODEL, BF16)], "mod1")
    if fetch is None:
        proj = _mm(u, p["w_in"], "nn", F32, "proj")
    else:
        proj, got = _mm(u, p["w_in"], "nn", F32, "proj", exchange=(fetch["o"], True))
        for k, t in zip(("w_oa", "w_ob", "w_out"), got):
            arrived[k] = fetch["assemble"](k, t)
    qh = _heads_q(proj[:, C_Q:C_Q + 1024])
    kh = _heads_kv(proj[:, C_K:C_K + 256])
    vh = _heads_kv(proj[:, C_V:C_V + 256])
    sinks4 = p["sinks"].reshape(ATT_KV_HEADS, ATT_GROUP, 1, 1)
    o_heads, got = _attn_fwd(qh, kh, vh, sinks4, None if fetch is None else (fetch["ff1"], True))
    if fetch is not None:
        arrived["w_ff1"] = fetch["assemble"]("w_ff1", got[0])
    o_a = _unheads_q(o_heads)
    (o_b, *dn_kept), got = _dn_fwd(proj, p["conv_w"], _row128(p["a_log"]), _row128(p["dt_bias"]),
                                   p["dn_norm_w"].reshape(1, LANES),
                                   None if fetch is None else (fetch["ff2"] + fetch["w_in_next"], True))
    if fetch is not None:
        arrived["w_ff2"] = fetch["assemble"]("w_ff2", got[0])
        arrived["w_in_next"] = got[1] if fetch["w_in_next"] else None
    p.update({k: v for k, v in arrived.items() if k != "w_in_next"})
    y_a = _mm(o_a, p["w_oa"], "nn", BF16, "y_a")
    y_b = _mm(o_b, p["w_ob"], "nn", BF16, "y_b")
    (gm,) = _rowwise(_f_gate, [(proj, C_GA // 1024, 1024), (proj, C_GB // 1024, 1024), _whole(y_a), _whole(y_b)], [],
                     [(D_MODEL, BF16)], "gate")
    mixed = _mm(gm, p["w_out"], "nn", BF16, "mixed")
    x1, u2 = _rowwise(_f_post1, [_whole(x), _whole(mixed)], [gt1, p["ln1_g"], p["ln1_b"], sc2, sh2],
                      [(D_MODEL, F32), (D_MODEL, BF16)], "post1")
    hpre = _mm(u2, p["w_ff1"], "nn", BF16, "ff1")
    (h,) = _rowwise(_f_act, [_whole(hpre)], [p["b_ff1"]], [(D_FF, BF16)], "act")
    ff = _mm(h, p["w_ff2"], "nn", BF16, "ff2")
    (x2,) = _rowwise(_f_post2, [_whole(x1), _whole(ff)], [gt2, p["b_ff2"], p["ln2_g"], p["ln2_b"]],
                     [(D_MODEL, F32)], "post2")
    saved = dict(x=x, u=u, proj=proj, o_a=o_a, o_b=o_b, y_a=y_a, y_b=y_b, gm=gm, mixed=mixed, x1=x1, u2=u2,
                 hpre=hpre, h=h, ff=ff, dn_kept=dn_kept, heads=(qh, kh, vh))
    return x2, saved, arrived


def _layer_bwd(dx2, p, sv, carry=None):
    sh1, sc1, gt1, sh2, sc2, gt2 = [p["mod"][i] for i in range(6)]
    g = {}
    (dx1_a, dff), (dgt2, g["b_ff2"], g["ln2_g"], g["ln2_b"]) = _rowwise_bwd(
        _f_post2, [_whole(sv["x1"]), _whole(sv["ff"])], [gt2, p["b_ff2"], p["ln2_g"], p["ln2_b"]], [dx2],
        [F32, BF16], "post2_bwd")
    dh = _mm(dff, p["w_ff2"], "nt", BF16,"dh")
    g["w_ff2"] = _mm(sv["h"], dff, "tn", BF16,"dw_ff2")
    (dhpre,), (g["b_ff1"],) = _rowwise_bwd(_f_act, [_whole(sv["hpre"])], [p["b_ff1"]], [dh], [BF16], "act_bwd")
    du2 = _mm(dhpre, p["w_ff1"], "nt", BF16,"du2")
    g["w_ff1"] = _mm(sv["u2"], dhpre, "tn", BF16,"dw_ff1")
    (dx_a, dmixed), (dgt1, g["ln1_g"], g["ln1_b"], dsc2, dsh2) = _rowwise_bwd(
        _f_post1, [_whole(sv["x"]), _whole(sv["mixed"])], [gt1, p["ln1_g"], p["ln1_b"], sc2, sh2], [dx1_a, du2],
        [F32, BF16], "post1_bwd")
    dgm = _mm(dmixed, p["w_out"], "nt", BF16,"dgm")
    g["w_out"] = _mm(sv["gm"], dmixed, "tn", BF16,"dw_out")
    proj = sv["proj"]
    (dga, dgb, dya, dyb), _ = _rowwise_bwd(
        _f_gate, [(proj, C_GA // 1024, 1024), (proj, C_GB // 1024, 1024), _whole(sv["y_a"]), _whole(sv["y_b"])], [],
        [dgm], [BF16, BF16, BF16, BF16], "gate_bwd")
    do_a = _mm(dya, p["w_oa"], "nt", BF16,"do_a")
    g["w_oa"] = _mm(sv["o_a"], dya, "tn", BF16,"dw_oa")
    do_b = _mm(dyb, p["w_ob"], "nt", BF16,"do_b")
    g["w_ob"] = _mm(sv["o_b"], dyb, "tn", BF16,"dw_ob")
    exchange = None if carry is None else (carry(g), False)
    (ddq, ddk, ddv, ddz, dba, dwq, dwk, dwv, dalog, ddtb, dnw), exchanged = _dn_bwd(
        proj, p["conv_w"], _row128(p["a_log"]), _row128(p["dt_bias"]), p["dn_norm_w"].reshape(1, LANES), do_b,
        sv["dn_kept"], exchange)
    g["conv_w"] = jnp.concatenate([dwq, dwk, dwv], axis=1)
    g["a_log"], g["dt_bias"], g["dn_norm_w"] = dalog[0, :DN_HEADS], ddtb[0, :DN_HEADS], dnw[0]
    qh, kh, vh = sv["heads"]
    sinks4 = p["sinks"].reshape(ATT_KV_HEADS, ATT_GROUP, 1, 1)
    dqh, dkh, dvh, dsk = _attn_bwd(qh, kh, vh, sinks4, _heads_q(do_a))
    g["sinks"] = dsk.reshape(ATT_KV_HEADS * ATT_GROUP)
    s = proj.shape[0]
    dproj = jnp.concatenate([
        _unheads_q(dqh).astype(BF16), ddq, ddk, ddv, ddz, dga, dgb,
        _unheads_kv(dkh[:, WINDOW:, :]).astype(BF16), _unheads_kv(dvh[:, WINDOW:, :]).astype(BF16),
        dba.astype(BF16), jnp.zeros((s, D_IN_P - C_BA - LANES), BF16)], axis=1)
    du = _mm(dproj, p["w_in"], "nt", BF16,"du")
    g["w_in"] = _mm(sv["u"], dproj, "tn", BF16,"dw_in")
    (dx,), (dsc1, dsh1) = _rowwise_bwd(_f_mod, [_whole(sv["x"])], [sc1, sh1], [du], [F32], "mod1_bwd", add=(0, dx_a))
    g["mod"] = jnp.stack([dsh1, dsc1, dgt1, dsh2, dsc2, dgt2])
    return dx, g, exchanged


def _permute_w_in(w):
    pad = jnp.zeros(w.shape[:-1] + (D_IN_P - D_IN,), w.dtype)
    return jnp.concatenate([w[..., 0:1024], w[..., 1536:5632], w[..., 5648:7696], w[..., 1024:1536],
                            w[..., 5632:5648], pad], axis=-1)


def _unpermute_w_in(g):
    return jnp.concatenate([g[..., 0:1024], g[..., C_K:C_K + 512], g[..., 1024:5120], g[..., C_BA:C_BA + 16],
                            g[..., 5120:7168]], axis=-1)


def _cols_from_chips(t):
    c, l, r, n = t.shape
    return t.transpose(1, 2, 0, 3).reshape(l, r, c * n)


def _cols_to_chips(t):
    l, r, n4 = t.shape
    return t.reshape(l, r, N_CHIPS, n4 // N_CHIPS).transpose(2, 0, 1, 3)


def _rows_from_chips(t):
    c, l, r, n = t.shape
    return t.transpose(1, 0, 2, 3).reshape(l, c * r, n)


def _rows_to_chips(t):
    l, r4, n = t.shape
    return t.reshape(l, N_CHIPS, r4 // N_CHIPS, n).transpose(1, 0, 2, 3)


_REPLICATED = ("b_ada", "a_log", "dt_bias", "sinks", "dn_norm_w", "ln1_g", "ln1_b", "b_ff1", "b_ff2", "ln2_g", "ln2_b")
_SMALL = _REPLICATED + ("conv_w",)
_PACK_W = 1024
_WEIGHT_ORDER = ("w_ada", "b_ada", "w_in", "conv_w", "a_log", "dt_bias", "sinks", "dn_norm_w", "w_oa", "w_ob", "w_out",
                 "ln1_g", "ln1_b", "w_ff1", "b_ff1", "w_ff2", "b_ff2", "ln2_g", "ln2_b")


def _pack_small(d):
    flat = jnp.concatenate([d[k].reshape(-1) for k in _SMALL])
    rows = -(-flat.shape[0] // (_PACK_W * 8)) * 8
    return jnp.pad(flat, (0, rows * _PACK_W - flat.shape[0])).reshape(rows, _PACK_W)


def _unpack_small(packed, shapes):
    flat = packed.reshape(-1)
    out, off = {}, 0
    for k in _SMALL:
        n = 1
        for d_ in shapes[k]:
            n *= d_
        out[k] = flat[off:off + n].reshape(shapes[k])
        off += n
    return out


def kernel(x, c, w_ada, b_ada, w_in, conv_w, a_log, dt_bias, sinks, dn_norm_w, w_oa, w_ob, w_out, ln1_g, ln1_b, w_ff1, b_ff1, w_ff2, b_ff2, ln2_g, ln2_b, loss_target, m_w_ada, m_b_ada, m_w_in, m_conv_w, m_a_log, m_dt_bias, m_sinks, m_dn_norm_w, m_w_oa, m_w_ob, m_w_out, m_ln1_g, m_ln1_b, m_w_ff1, m_b_ff1, m_w_ff2, m_b_ff2, m_ln2_g, m_ln2_b, v_w_ada, v_b_ada, v_w_in, v_conv_w, v_a_log, v_dt_bias, v_sinks, v_dn_norm_w, v_w_oa, v_w_ob, v_w_out, v_ln1_g, v_ln1_b, v_w_ff1, v_b_ff1, v_w_ff2, v_b_ff2, v_ln2_g, v_ln2_b):
    ix, iy, ic = _place()
    chip = 2 * ix + iy
    dev = 4 * ix + 2 * iy + ic
    weights = dict(w_ada=w_ada, b_ada=b_ada, w_in=w_in, conv_w=conv_w, a_log=a_log, dt_bias=dt_bias, sinks=sinks,
                   dn_norm_w=dn_norm_w, w_oa=w_oa, w_ob=w_ob, w_out=w_out, ln1_g=ln1_g, ln1_b=ln1_b, w_ff1=w_ff1,
                   b_ff1=b_ff1, w_ff2=w_ff2, b_ff2=b_ff2, ln2_g=ln2_g, ln2_b=ln2_b)
    mom_m = dict(w_ada=m_w_ada, b_ada=m_b_ada, w_in=m_w_in, conv_w=m_conv_w, a_log=m_a_log, dt_bias=m_dt_bias,
                 sinks=m_sinks, dn_norm_w=m_dn_norm_w, w_oa=m_w_oa, w_ob=m_w_ob, w_out=m_w_out, ln1_g=m_ln1_g,
                 ln1_b=m_ln1_b, w_ff1=m_w_ff1, b_ff1=m_b_ff1, w_ff2=m_w_ff2, b_ff2=m_b_ff2, ln2_g=m_ln2_g, ln2_b=m_ln2_b)
    mom_v = dict(w_ada=v_w_ada, b_ada=v_b_ada, w_in=v_w_in, conv_w=v_conv_w, a_log=v_a_log, dt_bias=v_dt_bias,
                 sinks=v_sinks, dn_norm_w=v_dn_norm_w, w_oa=v_w_oa, w_ob=v_w_ob, w_out=v_w_out, ln1_g=v_ln1_g,
                 ln1_b=v_ln1_b, w_ff1=v_w_ff1, b_ff1=v_b_ff1, w_ff2=v_w_ff2, b_ff2=v_b_ff2, ln2_g=v_ln2_g, ln2_b=v_ln2_b)

    n_ada = w_ada.shape[2]
    n_cw = conv_w.shape[2]
    taps = jnp.pad(conv_w.reshape(DEPTH * CONV_K, n_cw), ((0, 0), (0, D_MODEL - n_cw)))
    first = _all_gather8(jnp.concatenate([jnp.pad(c, ((0, 7), (0, 0))), taps], axis=0))
    c_all = first[:, 0, :]
    b_shard = lax.dynamic_slice_in_dim(b_ada, chip * n_ada, n_ada, axis=1).reshape(DEPTH, 1, n_ada)
    mod_t = _ada_fwd(c_all, w_ada, b_shard)
    mod_all = _all_gather8(mod_t.reshape(DEPTH * N_DEV, n_ada)).reshape(N_DEV, DEPTH, N_DEV, n_ada)
    mod_mine = lax.dynamic_index_in_dim(mod_all[0::2], dev, axis=2, keepdims=False)
    mod = mod_mine.transpose(1, 0, 2).reshape(DEPTH, 6, 1, D_MODEL)

    cw_all = first[0::2, 8:, :n_cw]
    conv_full = cw_all.transpose(1, 0, 2).reshape(DEPTH, CONV_K, N_CHIPS * n_cw)

    big = ("w_in", "w_oa", "w_ob", "w_out", "w_ff1", "w_ff2")
    w16 = {k: weights[k].astype(BF16) for k in big}
    from_chips = dict(w_in=lambda t: _permute_w_in(_cols_from_chips(t)), w_ff1=_cols_from_chips, w_oa=_rows_from_chips,
                      w_ob=_rows_from_chips, w_out=_rows_from_chips, w_ff2=_rows_from_chips)

    def assemble(name, gathered):
        return from_chips[name](gathered[:, None])[0]

    to_chips = dict(w_in=lambda t: _cols_to_chips(_unpermute_w_in(t)), w_ff1=_cols_to_chips, w_oa=_rows_to_chips,
                    w_ob=_rows_to_chips, w_out=_rows_to_chips, w_ff2=_rows_to_chips)

    def slices_for_chips(g, keys):
        return [to_chips[k](g[k][None])[:, 0].astype(BF16) for k in keys]

    full = [dict() for _ in range(DEPTH)]
    full[0]["w_in"] = assemble("w_in", _chip_exchange([w16["w_in"][0]], True, "gather_weights")[0])

    def layer_params(l):
        p = dict(full[l])
        p["mod"] = mod[l]
        p["conv_w"] = conv_full[l]
        for k in ("a_log", "dt_bias", "sinks", "dn_norm_w"):
            p[k] = weights[k][l]
        for k in ("ln1_g", "ln1_b", "b_ff1", "b_ff2", "ln2_g", "ln2_b"):
            p[k] = weights[k][l].reshape(1, -1)
        return p

    xs = x[0]
    saved = []
    for l in range(DEPTH):
        fetch = dict(o=[w16[k][l] for k in ("w_oa", "w_ob", "w_out")], ff1=[w16["w_ff1"][l]], ff2=[w16["w_ff2"][l]],
                     w_in_next=[w16["w_in"][l + 1]] if l + 1 < DEPTH else [], assemble=assemble)
        xs, sv, arrived = _layer_fwd(xs, layer_params(l), fetch)
        nxt = arrived.pop("w_in_next")
        full[l].update(arrived)
        if nxt is not None:
            full[l + 1]["w_in"] = assemble("w_in", nxt)
        saved.append(sv)
    dy, loss_local = _loss_head(xs, loss_target[0])
    loss = lax.psum(loss_local[0, 0], ("x", "y", "c"))
    early = tuple(k for k in big if k != "w_in")
    grads = [None] * DEPTH
    received = [dict() for _ in range(DEPTH)]
    dx = dy
    pending = []
    for l in reversed(range(DEPTH)):
        carry = functools.partial(lambda g, first: first + slices_for_chips(g, early), first=pending)
        dx, grads[l], got = _layer_bwd(dx, layer_params(l), saved[l], carry)
        if pending:
            received[l + 1]["w_in"] = got[0]
        received[l].update(zip(early, got[len(pending):]))
        pending = slices_for_chips(grads[l], ("w_in",))
    received[0]["w_in"] = _chip_exchange(pending, False, "scatter_grads")[0]
    grad_x = dx[None]
    gstack = {k: jnp.stack([grads[l][k] for l in range(DEPTH)]) for k in grads[0] if k not in big}

    dmod = gstack["mod"].reshape(DEPTH, 6 * D_MODEL)
    small_g = dict(b_ada=dmod, a_log=gstack["a_log"], dt_bias=gstack["dt_bias"], sinks=gstack["sinks"],
                   dn_norm_w=gstack["dn_norm_w"], ln1_g=gstack["ln1_g"], ln1_b=gstack["ln1_b"], b_ff1=gstack["b_ff1"],
                   b_ff2=gstack["b_ff2"], ln2_g=gstack["ln2_g"], ln2_b=gstack["ln2_b"], conv_w=gstack["conv_w"])
    shapes = {k: weights[k].shape for k in _REPLICATED}
    shapes["conv_w"] = small_g["conv_w"].shape
    g_all = _all_gather8(_pack_small(small_g))
    no_conv = jnp.zeros(shapes["conv_w"], F32)
    small_out = _small_adam(g_all, _pack_small(dict(weights, conv_w=no_conv)), _pack_small(dict(mom_m, conv_w=no_conv)),
                            _pack_small(dict(mom_v, conv_w=no_conv)))
    small_res = [_unpack_small(t, shapes) for t in small_out]
    g_conv = lax.dynamic_slice_in_dim(small_res[0]["conv_w"], chip * n_cw, n_cw, axis=2)
    res = {"conv_w": _adam_call([g_conv], conv_w, m_conv_w, v_conv_w, "adam_conv_w", tile=16)}

    dmod_all = g_all.reshape(N_DEV, -1)[:, :DEPTH * 6 * D_MODEL].reshape(N_DEV, DEPTH, 6 * D_MODEL)
    dmod_shard = lax.dynamic_slice_in_dim(dmod_all, chip * n_ada, n_ada, axis=2).transpose(1, 0, 2)
    g_w_ada = _ada_bwd(c_all, dmod_shard)
    res["w_ada"] = _adam_call([g_w_ada], w_ada, m_w_ada, v_w_ada, "adam_w_ada")

    partial = [_sum_slots([received[l][k] for l in range(DEPTH)], "sum_" + k) for k in big]
    theirs = _sibling_exchange(partial, "sibling_grads")
    for k, mine, other in zip(big, partial, theirs):
        shape = weights[k].shape
        res[k] = _adam_call([mine.reshape(shape), other.reshape(shape)], weights[k], mom_m[k], mom_v[k], "adam_" + k)
    for k in _REPLICATED:
        res[k] = [small_res[i][k] for i in range(4)]

    outs = [loss, grad_x]
    for i in range(4):
        outs += [res[k][i] for k in _WEIGHT_ORDER]
    return tuple(outs)
```

```python
import functools

import jax
import jax.numpy as jnp
from jax import lax
from jax.experimental import pallas as pl
from jax.experimental.pallas import tpu as pltpu

F32, BF16 = jnp.float32, jnp.bfloat16
HI = lax.Precision.HIGHEST
MESH = pl.DeviceIdType.MESH

D_MODEL = 1024
DEPTH = 4
ATT_KV_HEADS, ATT_GROUP, ATT_HEAD_DIM, WINDOW = 4, 4, 64, 128
DN_HEADS, DN_HEAD_DIM, CONV_K, CHUNK = 8, 128, 4, 64
D_FF = 4 * D_MODEL
D_IN = 7696
ALPHA = (2 * DEPTH) ** 0.25
LN_EPS = 1e-5
RMS_EPS = 1e-6
ADAM_LR, ADAM_B1, ADAM_B2, ADAM_EPS, ADAM_WD, ADAM_STEP = 0.001, 0.9, 0.999, 1e-08, 0.01, 10

N_CHIPS = 4
N_DEV = 8
LANES = 128
D_IN_P = 8192
C_Q, C_DQ, C_DK, C_DV, C_Z, C_GA, C_GB, C_K, C_V, C_BA = 0, 1024, 2048, 3072, 4096, 5120, 6144, 7168, 7424, 7680
NEG = -1e30
VMEM_LIMIT = 56 << 20


def _pc(body, **kw):
    return pl.pallas_call(body, **kw)


def _cparams(sem=None):
    if sem is None:
        return pltpu.CompilerParams(vmem_limit_bytes=VMEM_LIMIT)
    return pltpu.CompilerParams(vmem_limit_bytes=VMEM_LIMIT, dimension_semantics=sem)


_MM_VMEM_BUDGET = 44 << 20
_MM_MIN_TILE = 256


def _mm_tiles(m, n, k, out_bytes):
    def halvings(d):
        out = [d]
        while out[-1] % 2 == 0 and out[-1] // 2 >= _MM_MIN_TILE:
            out.append(out[-1] // 2)
        return out

    best = None
    for tm in halvings(m):
        for tn in halvings(n):
            if 2 * (2 * tm * k + 2 * tn * k + out_bytes * tm * tn) > _MM_VMEM_BUDGET:
                continue
            cost = (2 * m * k + (m // tm) * 2 * n * k, (m // tm) * (n // tn))
            if best is None or cost < best[0]:
                best = (cost, tm, tn)
    assert best is not None, (m, n, k)
    return best[1], best[2]


def _mm(a, b, mode, out_dtype, name, tm=None, tn=None, exchange=None):
    if mode == "nn":
        (m, k), (_, n) = a.shape, b.shape
        dims = (((1,), (0,)), ((), ()))
    elif mode == "nt":
        (m, k), (n, _) = a.shape, b.shape
        dims = (((1,), (1,)), ((), ()))
    else:
        (k, m), (_, n) = a.shape, b.shape
        dims = (((0,), (0,)), ((), ()))
    if tm is None:
        tm, tn = _mm_tiles(m, n, k, jnp.dtype(out_dtype).itemsize)
    tm, tn = min(tm, m), min(tn, n)
    assert m % tm == 0 and n % tn == 0, (name, m, n, tm, tn)
    a_spec = pl.BlockSpec((k, tm), lambda i, j: (0, i)) if mode == "tn" else pl.BlockSpec((tm, k), lambda i, j: (i, 0))
    b_spec = pl.BlockSpec((tn, k), lambda i, j: (j, 0)) if mode == "nt" else pl.BlockSpec((k, tn), lambda i, j: (0, j))

    def body(a_ref, b_ref, o_ref):
        o_ref[...] = lax.dot_general(a_ref[...], b_ref[...], dims, preferred_element_type=F32).astype(o_ref.dtype)

    out_spec = pl.BlockSpec((tm, tn), lambda i, j: (i, j))
    out_shape = jax.ShapeDtypeStruct((m, n), out_dtype)
    if exchange is None:
        return _pc(body, name=name, grid=(m // tm, n // tn), in_specs=[a_spec, b_spec], out_specs=out_spec,
                   out_shape=out_shape, compiler_params=_cparams())(a, b)
    (out,), exchanged = _call_with_exchange(body, name, (m // tm, n // tn), [a_spec, b_spec], [out_spec], [out_shape],
                                            [], (a, b), exchange)
    return out, exchanged


def _row_specs(rows, tile):
    return [pl.BlockSpec((tile, w), functools.partial(lambda i, cb: (i, cb), cb=cb)) for (_, cb, w) in rows]


def _vec_specs(vecs):
    return [pl.BlockSpec(v.shape, lambda i: (0, 0)) for v in vecs]


def _rowwise(fn, rows, vecs, outs, name, tile=512):
    n = rows[0][0].shape[0]
    tile = min(tile, n)
    nr, nv = len(rows), len(vecs)

    def body(*refs):
        rv = [r[...].astype(F32) for r in refs[:nr]]
        vv = [r[...] for r in refs[nr:nr + nv]]
        for o_ref, val in zip(refs[nr + nv:], fn(*rv, *vv)):
            o_ref[...] = val.astype(o_ref.dtype)

    res = _pc(body, name=name, grid=(n // tile,), in_specs=_row_specs(rows, tile) + _vec_specs(vecs),
              out_specs=[pl.BlockSpec((tile, w), lambda i: (i, 0)) for (w, _) in outs],
              out_shape=[jax.ShapeDtypeStruct((n, w), dt) for (w, dt) in outs],
              compiler_params=_cparams())(*[r[0] for r in rows], *vecs)
    return res


def _rowwise_bwd(fn, rows, vecs, cts, row_dtypes, name, tile=512, add=None):
    n = rows[0][0].shape[0]
    tile = min(tile, n)
    nr, nv, nc = len(rows), len(vecs), len(cts)
    want = [i for i, dt in enumerate(row_dtypes) if dt is not None]
    n_add = 0 if add is None else 1

    def body(*refs):
        rv = [r[...].astype(F32) for r in refs[:nr]]
        vv = [r[...] for r in refs[nr:nr + nv]]
        cv = [r[...].astype(F32) for r in refs[nr + nv:nr + nv + nc]]
        pos = nr + nv + nc
        add_ref = refs[pos] if n_add else None
        pos += n_add
        row_out = refs[pos:pos + len(want)]
        vec_out = refs[pos + len(want):]
        _, vjp = jax.vjp(fn, *rv, *vv)
        grads = vjp(tuple(cv))
        for o_ref, i in zip(row_out, want):
            gval = grads[i]
            if n_add and add[0] == i:
                gval = gval + add_ref[...]
            o_ref[...] = gval.astype(o_ref.dtype)

        @pl.when(pl.program_id(0) == 0)
        def _():
            for o_ref in vec_out:
                o_ref[...] = jnp.zeros_like(o_ref)

        for o_ref, gval in zip(vec_out, grads[nr:]):
            o_ref[...] += gval

    ct_rows = [(c, 0, c.shape[1]) for c in cts]
    add_rows = [(add[1], 0, add[1].shape[1])] if n_add else []
    res = _pc(body, name=name, grid=(n // tile,),
              in_specs=_row_specs(rows, tile) + _vec_specs(vecs) + _row_specs(ct_rows + add_rows, tile),
              out_specs=[pl.BlockSpec((tile, rows[i][2]), lambda i_: (i_, 0)) for i in want] + _vec_specs(vecs),
              out_shape=[jax.ShapeDtypeStruct((n, rows[i][2]), row_dtypes[i]) for i in want]
              + [jax.ShapeDtypeStruct(v.shape, F32) for v in vecs],
              compiler_params=_cparams(("arbitrary",)))(*[r[0] for r in rows], *vecs, *cts, *[a[0] for a in add_rows])
    return res[:len(want)], res[len(want):]


def _whole(a, cb=0, w=None):
    return (a, cb, a.shape[1] if w is None else w)


def _ln(x, g, b):
    mu = jnp.mean(x, axis=-1, keepdims=True)
    var = jnp.mean(jnp.square(x - mu), axis=-1, keepdims=True)
    return (x - mu) * lax.rsqrt(var + LN_EPS) * g + b


def _silu(x):
    return x * jax.nn.sigmoid(x)


def _softplus(x):
    return jnp.maximum(x, 0.0) + jnp.log(1.0 + jnp.exp(-jnp.abs(x)))


def _f_mod(x, sc, sh):
    return (x * (1.0 + sc) + sh,)


def _f_gate(ga, gb, ya, yb):
    return (jax.nn.sigmoid(ga) * ya + jax.nn.sigmoid(gb) * yb,)


def _f_post1(x, mixed, gt, g1, b1, sc2, sh2):
    x1 = _ln(ALPHA * x + (1.0 + gt) * mixed, g1, b1)
    return x1, x1 * (1.0 + sc2) + sh2


def _f_act(hpre, b):
    return (jnp.square(jnp.maximum(hpre + b, 0.0)),)


def _f_post2(x1, ff, gt, bff2, g2, b2):
    return (_ln(ALPHA * x1 + (1.0 + gt) * (ff + bff2), g2, b2),)


def _attn_valid(n):
    qi = lax.broadcasted_iota(jnp.int32, (WINDOW, 2 * WINDOW), 0)
    si = lax.broadcasted_iota(jnp.int32, (WINDOW, 2 * WINDOW), 1)
    diff = qi + WINDOW - si
    return (diff >= 0) & (diff < WINDOW) & (n * WINDOW + si - WINDOW >= 0)


def _attn_block(qs, kp, kc, vp, vc, sinks, valid):
    kband = jnp.concatenate([kp, kc], axis=0).astype(BF16)
    vband = jnp.concatenate([vp, vc], axis=0).astype(BF16)
    rng = range(len(qs))
    s = [lax.dot_general(qs[g].astype(BF16), kband, (((1,), (1,)), ((), ())), preferred_element_type=F32) for g in rng]
    s = [jnp.where(valid, s[g] * (ATT_HEAD_DIM ** -0.5), NEG) for g in rng]
    m = [lax.stop_gradient(jnp.maximum(jnp.max(s[g], axis=-1, keepdims=True), sinks[g])) for g in rng]
    p = [jnp.exp(s[g] - m[g]) for g in rng]
    denom = [jnp.sum(p[g], axis=-1, keepdims=True) + jnp.exp(sinks[g] - m[g]) for g in rng]
    probs = [(p[g] / denom[g]).astype(BF16) for g in rng]
    return [jnp.dot(probs[g], vband, preferred_element_type=F32) for g in rng]


def _attn_specs(s):
    nb = s // WINDOW
    q_spec = pl.BlockSpec((1, ATT_GROUP, WINDOW, ATT_HEAD_DIM), lambda h, n: (h, 0, n, 0))
    prev = pl.BlockSpec((1, WINDOW, ATT_HEAD_DIM), lambda h, n: (h, jnp.maximum(n - 1, 0), 0))
    cur = pl.BlockSpec((1, WINDOW, ATT_HEAD_DIM), lambda h, n: (h, n, 0))
    sk = pl.BlockSpec((1, ATT_GROUP, 1, 1), lambda h, n: (h, 0, 0, 0))
    return nb, q_spec, prev, cur, sk


def _attn_fwd(qh, kh, vh, sinks4, exchange=None):
    s = qh.shape[2]
    nb, q_spec, prev, cur, sk = _attn_specs(s)

    def body(q_ref, kp_ref, kc_ref, vp_ref, vc_ref, sk_ref, o_ref):
        valid = _attn_valid(pl.program_id(1))
        heads = range(ATT_GROUP)
        o = _attn_block([q_ref[0, g] for g in heads], kp_ref[0], kc_ref[0], vp_ref[0], vc_ref[0],
                        [sk_ref[0, g] for g in heads], valid)
        for g in heads:
            o_ref[0, g] = o[g].astype(o_ref.dtype)

    (o,), exchanged = _call_with_exchange(
        body, "attn_fwd", (ATT_KV_HEADS, nb), [q_spec, prev, cur, prev, cur, sk], [q_spec],
        [jax.ShapeDtypeStruct(qh.shape, BF16)], [], (qh, kh, kh, vh, vh, sinks4), exchange)
    return o, exchanged


def _attn_bwd(qh, kh, vh, sinks4, doh):
    s = qh.shape[2]
    nb, q_spec, prev, cur, sk = _attn_specs(s)
    acc = pl.BlockSpec((1, s + WINDOW, ATT_HEAD_DIM), lambda h, n: (h, 0, 0))

    def body(q_ref, kp_ref, kc_ref, vp_ref, vc_ref, sk_ref, do_ref, dq_ref, dk_ref, dv_ref, dsk_ref):
        n = pl.program_id(1)
        valid = _attn_valid(n)
        fn = functools.partial(_attn_block, valid=valid)
        heads = range(ATT_GROUP)
        dq, dsk, dkv = [None] * ATT_GROUP, [None] * ATT_GROUP, None
        for pair in (heads[:ATT_GROUP // 2], heads[ATT_GROUP // 2:]):
            _, vjp = jax.vjp(fn, [q_ref[0, g] for g in pair], kp_ref[0], kc_ref[0], vp_ref[0], vc_ref[0],
                             [sk_ref[0, g] for g in pair])
            dq_p, *dkv_p, dsk_p = vjp([do_ref[0, g].astype(F32) for g in pair])
            dkv = dkv_p if dkv is None else [a + b for a, b in zip(dkv, dkv_p)]
            for j, g in enumerate(pair):
                dq[g], dsk[g] = dq_p[j], dsk_p[j]
        dkp, dkc, dvp, dvc = dkv
        for g in heads:
            dq_ref[0, g] = dq[g]

        @pl.when(n == 0)
        def _():
            dk_ref[...] = jnp.zeros_like(dk_ref)
            dv_ref[...] = jnp.zeros_like(dv_ref)
            dsk_ref[...] = jnp.zeros_like(dsk_ref)

        band = pl.ds(pl.multiple_of(n * WINDOW, WINDOW), 2 * WINDOW)
        dk_ref[0, band, :] += jnp.concatenate([dkp, dkc], axis=0)
        dv_ref[0, band, :] += jnp.concatenate([dvp, dvc], axis=0)
        for g in heads:
            dsk_ref[0, g] += dsk[g]

    kv_shape = jax.ShapeDtypeStruct((ATT_KV_HEADS, s + WINDOW, ATT_HEAD_DIM), F32)
    return _pc(body, name="attn_bwd", grid=(ATT_KV_HEADS, nb), in_specs=[q_spec, prev, cur, prev, cur, sk, q_spec],
               out_specs=[q_spec, acc, acc, sk],
               out_shape=[jax.ShapeDtypeStruct(qh.shape, F32), kv_shape, kv_shape, jax.ShapeDtypeStruct(sinks4.shape, F32)],
               compiler_params=_cparams(("arbitrary", "arbitrary")))(qh, kh, kh, vh, vh, sinks4, doh)


def _bdot(a, b, dims=(((1,), (0,)), ((), ()))):
    return lax.dot_general(a.astype(BF16), b.astype(BF16), dims, preferred_element_type=F32)


def _hdot(a, b, dims=(((1,), (0,)), ((), ()))):
    return lax.dot_general(a, b, dims, precision=HI, preferred_element_type=F32)


_NN = (((1,), (0,)), ((), ()))
_NT = (((1,), (1,)), ((), ()))
_TN = (((0,), (0,)), ((), ()))


def _split2(a):
    hi = a.astype(BF16)
    return hi, (a - hi.astype(F32)).astype(BF16)


def _dot3(a, b, dims):
    ah, al = _split2(a)
    bh, bl = _split2(b)
    d = lambda p, q: lax.dot_general(p, q, dims, preferred_element_type=F32)
    return d(ah, bh) + (d(ah, bl) + d(al, bh))


@jax.custom_vjp
def _xdot(a, b):
    return _dot3(a, b, _NN)


def _xdot_fwd(a, b):
    return _dot3(a, b, _NN), (a, b)


def _xdot_bwd(res, g):
    a, b = res
    return _dot3(g, b, _NT), _dot3(a, g, _TN)


_xdot.defvjp(_xdot_fwd, _xdot_bwd)


def _mask_dot(mask16, b, dims):
    hi = b.astype(BF16)
    r = b - hi.astype(F32)
    mid = r.astype(BF16)
    lo = (r - mid.astype(F32)).astype(BF16)
    d = lambda q: lax.dot_general(mask16, q, dims, preferred_element_type=F32)
    return d(hi) + (d(mid) + d(lo))


def _chunk_masks():
    r = lax.broadcasted_iota(jnp.int32, (CHUNK, CHUNK), 0)
    c = lax.broadcasted_iota(jnp.int32, (CHUNK, CHUNK), 1)
    return r >= c, r > c, (r == c).astype(F32)


def _dn_local(qs, ks, vs, bs, gs, masks):
    causal, strict, eye = masks
    rng = range(len(qs))
    gb = [jnp.broadcast_to(gs[i], (CHUNK, CHUNK)) for i in rng]
    decay = [jnp.exp(jnp.where(causal, gb[i] - gb[i].T, NEG)) for i in rng]
    kb = [ks[i] * bs[i] for i in rng]
    vb = [vs[i] * bs[i] for i in rng]
    kk = [_bdot(kb[i], ks[i], _NT) for i in rng]
    p = [-jnp.where(strict, kk[i] * decay[i], 0.0) for i in rng]
    t = [eye + p[i] for i in rng]
    for _ in range(5):
        p = [_xdot(p[i], p[i]) for i in rng]
        t = [t[i] + _xdot(p[i], t[i]) for i in rng]
    eg = [jnp.exp(gs[i]) for i in rng]
    u = [_xdot(t[i], vb[i]) for i in rng]
    w = [_xdot(t[i], kb[i] * eg[i]) for i in rng]
    qk = [_bdot(qs[i], ks[i], _NT) for i in rng]
    intra = [qk[i] * decay[i] for i in rng]
    q_dec = [qs[i] * eg[i] for i in rng]
    k_dec = [ks[i] * jnp.exp(gs[i][CHUNK - 1:CHUNK, :] - gs[i]) for i in rng]
    return u, w, intra, q_dec, k_dec


def _dn_state_bwd(u, w, intra, q_dec, k_dec, gcum, state, do, dnext):
    last = jnp.exp(gcum[CHUNK - 1:CHUNK, :])
    x = _bdot(k_dec, dnext)
    t1 = _bdot(intra, do, _TN)
    v_new = u - _bdot(w, state)
    dqd = _bdot(do, state, _NT)
    din = _bdot(do, v_new, _NT)
    dkd = _bdot(v_new, dnext, _NT)
    base = _bdot(q_dec, do, _TN) - _bdot(w, t1, _TN)
    d_vnew = t1 + x
    dw = -_bdot(d_vnew, state, _NT)
    dstate = dnext * last + base - _bdot(w, x, _TN)
    dlast = jnp.sum(jnp.sum(state * dnext, axis=1, keepdims=True), axis=0, keepdims=True)
    row = lax.broadcasted_iota(jnp.int32, (CHUNK, 1), 0)
    dgc = jnp.where(row == CHUNK - 1, dlast * last, 0.0)
    return d_vnew, dw, din, dqd, dkd, dgc, dstate


def _l2norm(t):
    return t * lax.rsqrt(jnp.sum(jnp.square(t), axis=-1, keepdims=True) + RMS_EPS)


def _dn_pre(aq, ak, av, ba, alog, dtb, h):
    lane = lax.broadcasted_iota(jnp.int32, (1, LANES), 1)
    pick = lambda t, i: jnp.sum(jnp.where(lane == i, t, 0.0), axis=1, keepdims=True)
    q = _l2norm(_silu(aq)) * (DN_HEAD_DIM ** -0.5)
    k = _l2norm(_silu(ak))
    v = _silu(av)
    beta = jax.nn.sigmoid(pick(ba, h))
    g = -jnp.exp(pick(alog, h)) * _softplus(pick(ba, h + DN_HEADS) + pick(dtb, h))
    return q, k, v, beta, g


def _dn_post(o, z, nw):
    o = o * lax.rsqrt(jnp.mean(jnp.square(o), axis=-1, keepdims=True) + RMS_EPS) * nw
    return o * _silu(z)


_PAD = 8
_TOK_TILE = 512


def _pad_front(pad_ref, x_ref, s):
    pad_ref[pl.ds(0, _PAD), :] = jnp.zeros((_PAD, pad_ref.shape[1]), F32)
    pad_ref[pl.ds(_PAD, s), :] = x_ref[...]


def _conv_tile(pad_ref, w4, r0, n):
    acc = None
    for j in range(CONV_K):
        term = pad_ref[pl.ds(r0 + _PAD - (CONV_K - 1) + j, n), :] * w4[j:j + 1, :]
        acc = term if acc is None else acc + term
    return acc


def _conv_tile_bwd(pad_ref, da_ref, w4, r0, n):
    dx, dw = None, []
    da = da_ref[pl.ds(r0, n), :]
    for j in range(CONV_K):
        term = da_ref[pl.ds(r0 + CONV_K - 1 - j, n), :] * w4[j:j + 1, :]
        dx = term if dx is None else dx + term
        dw.append(jnp.sum(da * pad_ref[pl.ds(r0 + _PAD - (CONV_K - 1) + j, n), :], axis=0, keepdims=True))
    return dx, jnp.concatenate(dw, axis=0)


def _dn_gcum(g_c, causal_f):
    return _mask_dot(causal_f, jnp.broadcast_to(g_c, (CHUNK, LANES)), _NN)[:, 0:1]


def _dn_in_specs(s):
    col = lambda base: pl.BlockSpec((s, DN_HEAD_DIM), functools.partial(lambda h, b: (0, b + h), b=base // DN_HEAD_DIM))
    cw = lambda base: pl.BlockSpec((CONV_K, DN_HEAD_DIM), functools.partial(lambda h, b: (0, b + h), b=base))
    row = pl.BlockSpec((1, LANES), lambda h: (0, 0))
    ba = pl.BlockSpec((s, LANES), lambda h: (0, C_BA // LANES))
    return [col(C_DQ), col(C_DK), col(C_DV), col(C_Z), ba, cw(0), cw(DN_HEADS), cw(2 * DN_HEADS), row, row, row]


def _chunk_rows(c):
    return pl.ds(pl.multiple_of(c * CHUNK, CHUNK), CHUNK)


def _group(nchunk, want):
    g = min(want, nchunk)
    assert nchunk % g == 0
    return g


def _dn_forward_scan(q_s, k_s, v_s, b_s, g_s, gc_s, loc, o_s, states_ref, s):
    masks = _chunk_masks()
    causal_f = masks[0].astype(BF16)
    nchunk = s // CHUNK
    grp = _group(nchunk, 8)
    u_s, w_s, in_s, qd_s, kd_s = loc

    def local_step(i, carry):
        rows = [_chunk_rows(i * grp + j) for j in range(grp)]
        gcum = [_dn_gcum(g_s[r, :], causal_f) for r in rows]
        u, w, intra, q_dec, k_dec = _dn_local([q_s[r, :] for r in rows], [k_s[r, :] for r in rows],
                                              [v_s[r, :] for r in rows], [b_s[r, :] for r in rows], gcum, masks)
        for j, r in enumerate(rows):
            gc_s[r, :] = gcum[j]
            u_s[r, :] = u[j]
            w_s[r, :] = w[j].astype(w_s.dtype)
            in_s[r, :] = intra[j].astype(in_s.dtype)
            qd_s[r, :] = q_dec[j].astype(qd_s.dtype)
            kd_s[r, :] = k_dec[j].astype(kd_s.dtype)
        return carry

    lax.fori_loop(0, nchunk // grp, local_step, 0)

    def state_step(i, state):
        rows = _chunk_rows(i)
        states_ref[i] = state
        v_new = u_s[rows, :] - _bdot(w_s[rows, :], state)
        o_s[rows, :] = _bdot(qd_s[rows, :], state) + _bdot(in_s[rows, :], v_new)
        last = jnp.exp(gc_s[rows, :][CHUNK - 1:CHUNK, :])
        return state * last + _bdot(kd_s[rows, :], v_new, _TN)

    lax.fori_loop(0, nchunk, state_step, jnp.zeros((DN_HEAD_DIM, DN_HEAD_DIM), F32))


def _dn_saved_shapes(s):
    d, h = DN_HEAD_DIM, DN_HEADS
    shapes = [((h, s, d), F32), ((h, s, d), BF16), ((h, s, CHUNK), BF16), ((h, s, d), BF16), ((h, s, d), BF16),
              ((h, s, 1), F32), ((h, s // CHUNK, d, d), F32), ((h, s, d), F32)]
    return [jax.ShapeDtypeStruct(shp, dt) for shp, dt in shapes]


def _dn_saved_specs(s, **kw):
    return [pl.BlockSpec((1,) + t.shape[1:], functools.partial(lambda h, nd: (h,) + (0,) * nd, nd=len(t.shape) - 1), **kw)
            for t in _dn_saved_shapes(s)]


def _call_with_exchange(body, name, steps, in_specs, out_specs, out_shape, scratch, args, exchange):
    steps = (steps,) if isinstance(steps, int) else tuple(steps)
    params = _cparams(("arbitrary",) * len(steps))
    if exchange is None:
        res = _pc(body, name=name, grid=steps, in_specs=in_specs, out_specs=out_specs, out_shape=out_shape,
                  scratch_shapes=scratch, compiler_params=params)(*args)
        return res, None
    arrays, gather = exchange
    x_in, x_out, x_shape, x_scratch = _exchange_specs(arrays, gather)
    wrapped = _carry_exchange(body, len(in_specs), len(out_specs), len(scratch), len(arrays), gather, steps)
    res = _pc(wrapped, name=name + "_x", grid=steps, in_specs=in_specs + x_in, out_specs=out_specs + x_out,
              out_shape=out_shape + x_shape, scratch_shapes=scratch + x_scratch, compiler_params=params)(*args, *arrays)
    return res[:len(out_specs)], res[len(out_specs):]


def _dn_fwd(proj, conv_w, alog, dtb, nw, exchange=None):
    s = proj.shape[0]
    d = DN_HEAD_DIM

    tt = min(_TOK_TILE, s)

    def body(xq, xk, xv, z, ba, wq, wk, wv, alog_r, dtb_r, nw_r, o_ref, u_o, w_o, in_o, qd_o, kd_o, gc_o, st_o, oraw_o,
             padq, padk, padv, q_s, k_s, v_s, b_s, g_s):
        h = pl.program_id(0)
        loc = [r.at[0] for r in (u_o, w_o, in_o, qd_o, kd_o)]
        gc_s, states, o_s = gc_o.at[0], st_o.at[0], oraw_o.at[0]
        _pad_front(padq, xq, s)
        _pad_front(padk, xk, s)
        _pad_front(padv, xv, s)
        for r0 in range(0, s, tt):
            rows = pl.ds(r0, tt)
            aq, ak, av = _conv_tile(padq, wq[...], r0, tt), _conv_tile(padk, wk[...], r0, tt), _conv_tile(padv, wv[...], r0, tt)
            q_s[rows, :], k_s[rows, :], v_s[rows, :], b_s[rows, :], g_s[rows, :] = _dn_pre(
                aq, ak, av, ba[rows, :], alog_r[...], dtb_r[...], h)
        _dn_forward_scan(q_s, k_s, v_s, b_s, g_s, gc_s, loc, o_s, states, s)
        for r0 in range(0, s, tt):
            rows = pl.ds(r0, tt)
            o_ref[rows, :] = _dn_post(o_s[rows, :], z[rows, :], nw_r[...]).astype(o_ref.dtype)

    big = pltpu.VMEM((s, d), F32)
    thin = pltpu.VMEM((s, 1), F32)
    padded = pltpu.VMEM((s + _PAD, d), F32)
    return _call_with_exchange(
        body, "dn_fwd", DN_HEADS, _dn_in_specs(s), [pl.BlockSpec((s, d), lambda h: (0, h))] + _dn_saved_specs(s),
        [jax.ShapeDtypeStruct((s, DN_HEADS * d), BF16)] + _dn_saved_shapes(s),
        [padded, padded, padded, big, big, big, thin, thin],
        (proj, proj, proj, proj, proj, conv_w, conv_w, conv_w, alog, dtb, nw), exchange)


def _dn_bwd(proj, conv_w, alog, dtb, nw, dob, kept, exchange=None):
    s = proj.shape[0]
    d = DN_HEAD_DIM
    nchunk = s // CHUNK

    tt = min(_TOK_TILE, s)

    def body(xq, xk, xv, z, ba, wq, wk, wv, alog_r, dtb_r, nw_r, dob_ref, u_i, w_i, in_i, qd_i, kd_i, gc_i, st_i, oraw_i,
             dxq, dxk, dxv, dz, dba, dwq, dwk, dwv, dalog, ddtb, dnw,
             padq, padk, padv, q_s, k_s, v_s, b_s, g_s, o_s, dq_s, dk_s, dv_s, db_s, dg_s, dkd_s, din_s, dgc_s):
        h = pl.program_id(0)
        u_s, w_s, in_s, qd_s, kd_s = [r.at[0] for r in (u_i, w_i, in_i, qd_i, kd_i)]
        gc_s, states, oraw = gc_i.at[0], st_i.at[0], oraw_i.at[0]
        masks = _chunk_masks()
        causal_f = masks[0].astype(BF16)
        pre = functools.partial(_dn_pre, h=h)
        _pad_front(padq, xq, s)
        _pad_front(padk, xk, s)
        _pad_front(padv, xv, s)

        def conv_tiles(r0):
            return _conv_tile(padq, wq[...], r0, tt), _conv_tile(padk, wk[...], r0, tt), _conv_tile(padv, wv[...], r0, tt)

        for r0 in range(0, s, tt):
            rows = pl.ds(r0, tt)
            q_s[rows, :], k_s[rows, :], v_s[rows, :], b_s[rows, :], g_s[rows, :] = pre(
                *conv_tiles(r0), ba[rows, :], alog_r[...], dtb_r[...])
        dnw_v = jnp.zeros((1, LANES), F32)
        for r0 in range(0, s, tt):
            rows = pl.ds(r0, tt)
            _, post_vjp = jax.vjp(_dn_post, oraw[rows, :], z[rows, :], nw_r[...])
            do_raw, dz_v, dnw_t = post_vjp(dob_ref[rows, :].astype(F32))
            dz[rows, :] = dz_v.astype(dz.dtype)
            o_s[rows, :] = do_raw
            dnw_v = dnw_v + dnw_t

        def state_step(i, dstate):
            c = nchunk - 1 - i
            rows = _chunk_rows(c)
            du, dw, din, dqd, dkd, dgc, dstate = _dn_state_bwd(
                u_s[rows, :], w_s[rows, :], in_s[rows, :], qd_s[rows, :], kd_s[rows, :], gc_s[rows, :], states[c],
                o_s[rows, :], dstate)
            dq_s[rows, :] = du
            dk_s[rows, :] = dw
            dv_s[rows, :] = dqd
            dkd_s[rows, :] = dkd
            din_s[rows, :] = din
            dgc_s[rows, :] = dgc
            return dstate

        lax.fori_loop(0, nchunk, state_step, jnp.zeros((d, d), F32))
        local = functools.partial(_dn_local, masks=masks)
        grp = _group(nchunk, 8)

        def local_step(i, carry):
            rows = [_chunk_rows(i * grp + j) for j in range(grp)]
            get = lambda ref: [ref[r, :] for r in rows]
            _, vjp = jax.vjp(local, get(q_s), get(k_s), get(v_s), get(b_s), get(gc_s))
            dq_c, dk_c, dv_c, db_c, dgc_c = vjp((get(dq_s), get(dk_s), get(din_s), get(dv_s), get(dkd_s)))
            dgc_c = [dgc_c[j] + dgc_s[r, :] for j, r in enumerate(rows)]
            dg_c = [_mask_dot(causal_f, jnp.broadcast_to(t, (CHUNK, LANES)), _TN)[:, 0:1] for t in dgc_c]
            for j, r in enumerate(rows):
                dq_s[r, :] = dq_c[j]
                dk_s[r, :] = dk_c[j]
                dv_s[r, :] = dv_c[j]
                db_s[r, :] = db_c[j]
                dg_s[r, :] = dg_c[j]
            return carry

        lax.fori_loop(0, nchunk // grp, local_step, 0)

        @pl.when(h == 0)
        def _():
            dba[...] = jnp.zeros_like(dba)
            dalog[...] = jnp.zeros_like(dalog)
            ddtb[...] = jnp.zeros_like(ddtb)
            dnw[...] = jnp.zeros_like(dnw)

        dalog_v = jnp.zeros((1, LANES), F32)
        ddtb_v = jnp.zeros((1, LANES), F32)
        for r0 in range(0, s, tt):
            rows = pl.ds(r0, tt)
            _, pre_vjp = jax.vjp(pre, *conv_tiles(r0), ba[rows, :], alog_r[...], dtb_r[...])
            daq, dak, dav, dba_t, dalog_t, ddtb_t = pre_vjp(
                (dq_s[rows, :], dk_s[rows, :], dv_s[rows, :], db_s[rows, :], dg_s[rows, :]))
            dq_s[rows, :], dk_s[rows, :], dv_s[rows, :] = daq, dak, dav
            dba[rows, :] += dba_t
            dalog_v = dalog_v + dalog_t
            ddtb_v = ddtb_v + ddtb_t
        tail = pl.ds(s, _PAD)
        dq_s[tail, :] = dk_s[tail, :] = dv_s[tail, :] = jnp.zeros((_PAD, d), F32)
        for pad, da_s, w_ref, dx_ref, dw_ref in ((padq, dq_s, wq, dxq, dwq), (padk, dk_s, wk, dxk, dwk), (padv, dv_s, wv, dxv, dwv)):
            dw_acc = jnp.zeros((CONV_K, d), F32)
            for r0 in range(0, s, tt):
                dx_t, dw_t = _conv_tile_bwd(pad, da_s, w_ref[...], r0, tt)
                dx_ref[pl.ds(r0, tt), :] = dx_t.astype(dx_ref.dtype)
                dw_acc = dw_acc + dw_t
            dw_ref[...] = dw_acc
        dalog[...] += dalog_v
        ddtb[...] += ddtb_v
        dnw[...] += dnw_v

    big = pltpu.VMEM((s, d), F32)
    thin = pltpu.VMEM((s, 1), F32)
    padded = pltpu.VMEM((s + _PAD, d), F32)
    w_all = DN_HEADS * d
    col_out = lambda: pl.BlockSpec((s, d), lambda h: (0, h))
    cw_out = lambda: pl.BlockSpec((CONV_K, d), lambda h: (0, h))
    row = lambda: pl.BlockSpec((1, LANES), lambda h: (0, 0))
    big_out = jax.ShapeDtypeStruct((s, w_all), BF16)
    cw_shape = jax.ShapeDtypeStruct((CONV_K, w_all), F32)
    row_shape = jax.ShapeDtypeStruct((1, LANES), F32)
    return _call_with_exchange(
        body, "dn_bwd", DN_HEADS,
        _dn_in_specs(s) + [pl.BlockSpec((s, d), lambda h: (0, h))] + _dn_saved_specs(s, pipeline_mode=pl.Buffered(1)),
        [col_out(), col_out(), col_out(), col_out(), pl.BlockSpec((s, LANES), lambda h: (0, 0)),
         cw_out(), cw_out(), cw_out(), row(), row(), row()],
        [big_out, big_out, big_out, big_out, jax.ShapeDtypeStruct((s, LANES), F32),
         cw_shape, cw_shape, cw_shape, row_shape, row_shape, row_shape],
        [padded, padded, padded, big, big, big, thin, thin, big,
         padded, padded, padded, thin, thin, big, pltpu.VMEM((s, CHUNK), F32), thin],
        (proj, proj, proj, proj, proj, conv_w, conv_w, conv_w, alog, dtb, nw, dob, *kept), exchange)


def _loss_head(y, target, tile=256):
    n, dm = y.shape
    tile = min(tile, n)

    def body(y_ref, t_ref, dy_ref, loss_ref):
        err = y_ref[...] - t_ref[...]
        dy_ref[...] = err * (1.0 / dm)

        @pl.when(pl.program_id(0) == 0)
        def _():
            loss_ref[...] = jnp.zeros_like(loss_ref)

        loss_ref[...] += 0.5 * jnp.sum(jnp.mean(jnp.square(err), axis=-1, keepdims=True), axis=0, keepdims=True)

    blk = pl.BlockSpec((tile, dm), lambda i: (i, 0))
    return _pc(body, name="loss_head", grid=(n // tile,), in_specs=[blk, blk],
               out_specs=[blk, pl.BlockSpec((1, 1), lambda i: (0, 0))],
               out_shape=[jax.ShapeDtypeStruct((n, dm), F32), jax.ShapeDtypeStruct((1, 1), F32)],
               compiler_params=_cparams(("arbitrary",)))(y, target)


def _ada_fwd(c_all, w_ada, b_shard):
    nl, dm, n = w_ada.shape

    def body(c_ref, w_ref, b_ref, o_ref):
        ca = _silu(c_ref[...]).astype(BF16)
        o_ref[0] = jnp.dot(ca, w_ref[0].astype(BF16), preferred_element_type=F32) + b_ref[0]

    return _pc(body, name="ada_fwd", grid=(nl,),
               in_specs=[pl.BlockSpec((N_DEV, dm), lambda l: (0, 0)), pl.BlockSpec((1, dm, n), lambda l: (l, 0, 0)),
                         pl.BlockSpec((1, 1, n), lambda l: (l, 0, 0))],
               out_specs=pl.BlockSpec((1, N_DEV, n), lambda l: (l, 0, 0)),
               out_shape=jax.ShapeDtypeStruct((nl, N_DEV, n), F32), compiler_params=_cparams())(c_all, w_ada, b_shard)


def _ada_bwd(c_all, dmod):
    nl, _, n = dmod.shape
    dm = c_all.shape[1]

    def body(c_ref, d_ref, o_ref):
        o_ref[0] = _hdot(_silu(c_ref[...]), d_ref[0], _TN)

    return _pc(body, name="ada_bwd", grid=(nl,),
               in_specs=[pl.BlockSpec((N_DEV, dm), lambda l: (0, 0)), pl.BlockSpec((1, N_DEV, n), lambda l: (l, 0, 0))],
               out_specs=pl.BlockSpec((1, dm, n), lambda l: (l, 0, 0)),
               out_shape=jax.ShapeDtypeStruct((nl, dm, n), F32), compiler_params=_cparams())(c_all, dmod)


def _adamw(g, w, m, v):
    m = ADAM_B1 * m + (1.0 - ADAM_B1) * g
    v = ADAM_B2 * v + (1.0 - ADAM_B2) * jnp.square(g)
    m_hat = m / (1.0 - ADAM_B1 ** ADAM_STEP)
    v_hat = v / (1.0 - ADAM_B2 ** ADAM_STEP)
    delta = -ADAM_LR * (m_hat / (jnp.sqrt(v_hat) + ADAM_EPS) + ADAM_WD * w)
    return delta, m, v


def _adam_call(parts, w, m, v, name, tile=128):
    shape = w.shape
    flat = lambda t: t.reshape(-1, shape[-1])
    width = shape[-1]

    def fn(*vals):
        g = vals[0] if len(parts) == 1 else vals[0] + vals[1]
        return (g,) + _adamw(g, *vals[len(parts):])

    rows = [_whole(flat(t)) for t in (*parts, w, m, v)]
    outs = _rowwise(fn, rows, [], [(width, F32)] * 4, name, tile=tile)
    return [o.reshape(shape) for o in outs]


def _sum_slots(per_layer, name, tile=128):
    nl = len(per_layer)
    _, n, width = per_layer[0].shape
    tile = min(tile, n)
    nt = n // tile

    def body(*refs):
        o_ref = refs[nl]
        for lp in range(nl):
            @pl.when(pl.program_id(0) == lp)
            def _(r_ref=refs[lp]):
                acc = r_ref[0].astype(F32)
                for j in range(1, N_CHIPS):
                    acc = acc + r_ref[j].astype(F32)
                o_ref[...] = acc

    in_specs = [pl.BlockSpec((N_CHIPS, tile, width), functools.partial(lambda l, t, lp: (0, jnp.where(l == lp, t, 0), 0), lp=lp))
                for lp in range(nl)]
    return _pc(body, name=name, grid=(nl, nt), in_specs=in_specs,
               out_specs=pl.BlockSpec((tile, width), lambda l, t: (l * nt + t, 0)),
               out_shape=jax.ShapeDtypeStruct((nl * n, width), F32), compiler_params=_cparams())(*per_layer)


def _small_adam(g_all, w, m, v):
    def body(g_ref, w_ref, m_ref, v_ref, og, od, om, ov):
        g = g_ref[0]
        for j in range(1, N_DEV):
            g = g + g_ref[j]
        og[...] = g
        od[...], om[...], ov[...] = _adamw(g, w_ref[...], m_ref[...], v_ref[...])

    vm = pl.BlockSpec(memory_space=pltpu.VMEM)
    shp = jax.ShapeDtypeStruct(w.shape, F32)
    return _pc(body, name="small_adam", in_specs=[vm] * 4, out_specs=[vm] * 4, out_shape=[shp] * 4,
               compiler_params=_cparams())(g_all, w, m, v)


def _place():
    return lax.axis_index("x"), lax.axis_index("y"), lax.axis_index("c")


def _flip(v, bit):
    return 1 - v if bit else v


def _all_gather8(a):
    r, n = a.shape

    def body(a_ref, o_ref, send_sems, recv_sems):
        x, y, c = _place()
        me = 4 * x + 2 * y + c
        o_ref[me] = a_ref[...]
        copies = []
        for k in range(1, N_DEV):
            peer = (_flip(x, k & 4), _flip(y, k & 2), _flip(c, k & 1))
            copies.append(pltpu.make_async_remote_copy(
                src_ref=a_ref, dst_ref=o_ref.at[me], send_sem=send_sems.at[k - 1], recv_sem=recv_sems.at[k - 1],
                device_id=peer, device_id_type=MESH))
        for cp in copies:
            cp.start()
        for k in range(1, N_DEV):
            px, py, pc_ = _flip(x, k & 4), _flip(y, k & 2), _flip(c, k & 1)
            pltpu.make_async_remote_copy(
                src_ref=a_ref, dst_ref=o_ref.at[4 * px + 2 * py + pc_], send_sem=send_sems.at[k - 1],
                recv_sem=recv_sems.at[k - 1], device_id=(px, py, pc_), device_id_type=MESH).wait_recv()
        for cp in copies:
            cp.wait_send()

    vm = pl.BlockSpec(memory_space=pltpu.VMEM)
    return _pc(body, name="all_gather8", in_specs=[vm], out_specs=vm,
               out_shape=jax.ShapeDtypeStruct((N_DEV, r, n), a.dtype),
               scratch_shapes=[pltpu.SemaphoreType.DMA((N_DEV - 1,)), pltpu.SemaphoreType.DMA((N_DEV - 1,))],
               compiler_params=_cparams())(a)


def _chip_exchange(arrays, gather, name):
    na = len(arrays)

    def body(*refs):
        ins, outs, sems = refs[:na], refs[na:2 * na], refs[2 * na:]
        _exchange_copies(ins, outs, sems, gather, start=True)
        _exchange_copies(ins, outs, sems, gather, start=False)

    in_specs, out_specs, out_shape, scratch = _exchange_specs(arrays, gather)
    return _pc(body, name=name, in_specs=in_specs, out_specs=out_specs, out_shape=out_shape, scratch_shapes=scratch,
               compiler_params=_cparams())(*arrays)


def _exchange_specs(arrays, gather):
    na = len(arrays)
    hbm = pl.BlockSpec(memory_space=pl.ANY)
    out_shape = [jax.ShapeDtypeStruct(((N_CHIPS,) + a.shape) if gather else a.shape, a.dtype) for a in arrays]
    n_remote = 4 if gather else 2
    scratch = [pltpu.SemaphoreType.DMA((3 * na,))] * n_remote + [pltpu.SemaphoreType.DMA((na,))]
    return [hbm] * na, [hbm] * na, out_shape, scratch


def _gather_copies(ins, outs, sems, start):
    send_i, recv_i, send_d, recv_d, local_sems = sems
    x, y, c = _place()
    me = 2 * x + y
    sibling = (x, y, 1 - c)
    ici_sends, ici_arrivals, hand_ons, hand_arrivals, locals_ = [], [], [], [], []
    for i in range(len(ins)):
        half = ins[i].shape[0] // 2
        mine, other = pl.ds(c * half, half), pl.ds((1 - c) * half, half)
        locals_.append(pltpu.make_async_copy(ins[i], outs[i].at[me], local_sems.at[i]))
        for j in range(1, N_CHIPS):
            px, py = _flip(x, j & 2), _flip(y, j & 1)
            peer = 2 * px + py
            k = i * 3 + j - 1
            ici = dict(send_sem=send_i.at[k], recv_sem=recv_i.at[k], device_id=(px, py, c), device_id_type=MESH)
            d2d = dict(send_sem=send_d.at[k], recv_sem=recv_d.at[k], device_id=sibling, device_id_type=MESH)
            ici_sends.append(pltpu.make_async_remote_copy(src_ref=ins[i].at[mine], dst_ref=outs[i].at[me, mine], **ici))
            ici_arrivals.append(pltpu.make_async_remote_copy(src_ref=ins[i].at[mine], dst_ref=outs[i].at[peer, mine], **ici))
            hand_ons.append(pltpu.make_async_remote_copy(
                src_ref=outs[i].at[peer, mine], dst_ref=outs[i].at[peer, mine], **d2d))
            hand_arrivals.append(pltpu.make_async_remote_copy(
                src_ref=outs[i].at[peer, other], dst_ref=outs[i].at[peer, other], **d2d))
    if start:
        for cp in locals_ + ici_sends:
            cp.start()
    else:
        for arrival, hand_on in zip(ici_arrivals, hand_ons):
            arrival.wait_recv()
            hand_on.start()
        for cp in hand_arrivals:
            cp.wait_recv()
        for cp in ici_sends + hand_ons:
            cp.wait_send()
        for cp in locals_:
            cp.wait()


def _exchange_copies(ins, outs, sems, gather, start):
    if gather:
        return _gather_copies(ins, outs, sems, start)
    send_sems, recv_sems, local_sems = sems
    x, y, c = _place()
    me = 2 * x + y
    sends, arrivals, locals_ = [], [], []
    for i in range(len(ins)):
        locals_.append(pltpu.make_async_copy(ins[i] if gather else ins[i].at[me], outs[i].at[me], local_sems.at[i]))
        for j in range(1, N_CHIPS):
            px, py = _flip(x, j & 2), _flip(y, j & 1)
            peer = 2 * px + py
            pair = dict(send_sem=send_sems.at[i * 3 + j - 1], recv_sem=recv_sems.at[i * 3 + j - 1],
                        device_id=(px, py, c), device_id_type=MESH)
            sends.append(pltpu.make_async_remote_copy(
                src_ref=ins[i] if gather else ins[i].at[peer], dst_ref=outs[i].at[me], **pair))
            arrivals.append(pltpu.make_async_remote_copy(
                src_ref=ins[i] if gather else ins[i].at[me], dst_ref=outs[i].at[peer], **pair))
    if start:
        for cp in locals_ + sends:
            cp.start()
    else:
        for cp in arrivals:
            cp.wait_recv()
        for cp in sends:
            cp.wait_send()
        for cp in locals_:
            cp.wait()


def _carry_exchange(body, n_in, n_out, n_scratch, n_arrays, gather, steps):
    def wrapped(*refs):
        na = n_arrays
        ins, xin = refs[:n_in], refs[n_in:n_in + na]
        outs = refs[n_in + na:n_in + na + n_out]
        xout = refs[n_in + na + n_out:n_in + 2 * na + n_out]
        rest = refs[n_in + 2 * na + n_out:]
        scratch, sems = rest[:n_scratch], rest[n_scratch:]
        first, last = True, True
        for axis, n in enumerate(steps):
            first = jnp.logical_and(first, pl.program_id(axis) == 0)
            last = jnp.logical_and(last, pl.program_id(axis) == n - 1)

        @pl.when(first)
        def _():
            _exchange_copies(xin, xout, sems, gather, start=True)

        body(*ins, *outs, *scratch)

        @pl.when(last)
        def _():
            _exchange_copies(xin, xout, sems, gather, start=False)

    return wrapped


def _sibling_exchange(arrays, name):
    na = len(arrays)

    def body(*refs):
        ins, outs = refs[:na], refs[na:2 * na]
        send_sems, recv_sems = refs[2 * na:]
        x, y, c = _place()
        copies = [pltpu.make_async_remote_copy(
            src_ref=ins[i], dst_ref=outs[i], send_sem=send_sems.at[i], recv_sem=recv_sems.at[i],
            device_id=(x, y, 1 - c), device_id_type=MESH) for i in range(na)]
        for cp in copies:
            cp.start()
        for cp in copies:
            cp.wait()

    hbm = pl.BlockSpec(memory_space=pl.ANY)
    return _pc(body, name=name, in_specs=[hbm] * na, out_specs=[hbm] * na,
               out_shape=[jax.ShapeDtypeStruct(a.shape, a.dtype) for a in arrays],
               scratch_shapes=[pltpu.SemaphoreType.DMA((na,)), pltpu.SemaphoreType.DMA((na,))],
               compiler_params=_cparams())(*arrays)


def _heads_q(t):
    s = t.shape[0]
    return t.reshape(s, ATT_KV_HEADS, ATT_GROUP, ATT_HEAD_DIM).transpose(1, 2, 0, 3)


def _unheads_q(t):
    s = t.shape[2]
    return t.transpose(2, 0, 1, 3).reshape(s, ATT_KV_HEADS * ATT_GROUP * ATT_HEAD_DIM)


def _heads_kv(t):
    s = t.shape[0]
    return t.reshape(s, ATT_KV_HEADS, ATT_HEAD_DIM).transpose(1, 0, 2)


def _unheads_kv(t):
    s = t.shape[1]
    return t.transpose(1, 0, 2).reshape(s, ATT_KV_HEADS * ATT_HEAD_DIM)


def _row128(v):
    return jnp.pad(v, (0, LANES - v.shape[0])).reshape(1, LANES)


def _layer_fwd(x, p, fetch=None):
    p = dict(p)
    arrived = {}
    sh1, sc1, gt1, sh2, sc2, gt2 = [p["mod"][i] for i in range(6)]
    (u,) = _rowwise(_f_mod, [_whole(x)], [sc1, sh1], [(D_MODEL, BF16)], "mod1")
    if fetch is None:
        proj = _mm(u, p["w_in"], "nn", F32, "proj")
    else:
        proj, got = _mm(u, p["w_in"], "nn", F32, "proj", exchange=(fetch["o"], True))
        for k, t in zip(("w_oa", "w_ob", "w_out"), got):
            arrived[k] = fetch["assemble"](k, t)
    qh = _heads_q(proj[:, C_Q:C_Q + 1024])
    kh = _heads_kv(proj[:, C_K:C_K + 256])
    vh = _heads_kv(proj[:, C_V:C_V + 256])
    sinks4 = p["sinks"].reshape(ATT_KV_HEADS, ATT_GROUP, 1, 1)
    o_heads, got = _attn_fwd(qh, kh, vh, sinks4, None if fetch is None else (fetch["ff1"], True))
    if fetch is not None:
        arrived["w_ff1"] = fetch["assemble"]("w_ff1", got[0])
    o_a = _unheads_q(o_heads)
    (o_b, *dn_kept), got = _dn_fwd(proj, p["conv_w"], _row128(p["a_log"]), _row128(p["dt_bias"]),
                                   p["dn_norm_w"].reshape(1, LANES),
                                   None if fetch is None else (fetch["ff2"] + fetch["w_in_next"], True))
    if fetch is not None:
        arrived["w_ff2"] = fetch["assemble"]("w_ff2", got[0])
        arrived["w_in_next"] = got[1] if fetch["w_in_next"] else None
    p.update({k: v for k, v in arrived.items() if k != "w_in_next"})
    y_a = _mm(o_a, p["w_oa"], "nn", BF16, "y_a")
    y_b = _mm(o_b, p["w_ob"], "nn", BF16, "y_b")
    (gm,) = _rowwise(_f_gate, [(proj, C_GA // 1024, 1024), (proj, C_GB // 1024, 1024), _whole(y_a), _whole(y_b)], [],
                     [(D_MODEL, BF16)], "gate")
    mixed = _mm(gm, p["w_out"], "nn", BF16, "mixed")
    x1, u2 = _rowwise(_f_post1, [_whole(x), _whole(mixed)], [gt1, p["ln1_g"], p["ln1_b"], sc2, sh2],
                      [(D_MODEL, F32), (D_MODEL, BF16)], "post1")
    hpre = _mm(u2, p["w_ff1"], "nn", BF16, "ff1")
    (h,) = _rowwise(_f_act, [_whole(hpre)], [p["b_ff1"]], [(D_FF, BF16)], "act")
    ff = _mm(h, p["w_ff2"], "nn", BF16, "ff2")
    (x2,) = _rowwise(_f_post2, [_whole(x1), _whole(ff)], [gt2, p["b_ff2"], p["ln2_g"], p["ln2_b"]],
                     [(D_MODEL, F32)], "post2")
    saved = dict(x=x, u=u, proj=proj, o_a=o_a, o_b=o_b, y_a=y_a, y_b=y_b, gm=gm, mixed=mixed, x1=x1, u2=u2,
                 hpre=hpre, h=h, ff=ff, dn_kept=dn_kept, heads=(qh, kh, vh))
    return x2, saved, arrived


def _layer_bwd(dx2, p, sv, carry=None):
    sh1, sc1, gt1, sh2, sc2, gt2 = [p["mod"][i] for i in range(6)]
    g = {}
    (dx1_a, dff), (dgt2, g["b_ff2"], g["ln2_g"], g["ln2_b"]) = _rowwise_bwd(
        _f_post2, [_whole(sv["x1"]), _whole(sv["ff"])], [gt2, p["b_ff2"], p["ln2_g"], p["ln2_b"]], [dx2],
        [F32, BF16], "post2_bwd")
    dh = _mm(dff, p["w_ff2"], "nt", BF16,"dh")
    g["w_ff2"] = _mm(sv["h"], dff, "tn", BF16,"dw_ff2")
    (dhpre,), (g["b_ff1"],) = _rowwise_bwd(_f_act, [_whole(sv["hpre"])], [p["b_ff1"]], [dh], [BF16], "act_bwd")
    du2 = _mm(dhpre, p["w_ff1"], "nt", BF16,"du2")
    g["w_ff1"] = _mm(sv["u2"], dhpre, "tn", BF16,"dw_ff1")
    (dx_a, dmixed), (dgt1, g["ln1_g"], g["ln1_b"], dsc2, dsh2) = _rowwise_bwd(
        _f_post1, [_whole(sv["x"]), _whole(sv["mixed"])], [gt1, p["ln1_g"], p["ln1_b"], sc2, sh2], [dx1_a, du2],
        [F32, BF16], "post1_bwd")
    dgm = _mm(dmixed, p["w_out"], "nt", BF16,"dgm")
    g["w_out"] = _mm(sv["gm"], dmixed, "tn", BF16,"dw_out")
    proj = sv["proj"]
    (dga, dgb, dya, dyb), _ = _rowwise_bwd(
        _f_gate, [(proj, C_GA // 1024, 1024), (proj, C_GB // 1024, 1024), _whole(sv["y_a"]), _whole(sv["y_b"])], [],
        [dgm], [BF16, BF16, BF16, BF16], "gate_bwd")
    do_a = _mm(dya, p["w_oa"], "nt", BF16,"do_a")
    g["w_oa"] = _mm(sv["o_a"], dya, "tn", BF16,"dw_oa")
    do_b = _mm(dyb, p["w_ob"], "nt", BF16,"do_b")
    g["w_ob"] = _mm(sv["o_b"], dyb, "tn", BF16,"dw_ob")
    exchange = None if carry is None else (carry(g), False)
    (ddq, ddk, ddv, ddz, dba, dwq, dwk, dwv, dalog, ddtb, dnw), exchanged = _dn_bwd(
        proj, p["conv_w"], _row128(p["a_log"]), _row128(p["dt_bias"]), p["dn_norm_w"].reshape(1, LANES), do_b,
        sv["dn_kept"], exchange)
    g["conv_w"] = jnp.concatenate([dwq, dwk, dwv], axis=1)
    g["a_log"], g["dt_bias"], g["dn_norm_w"] = dalog[0, :DN_HEADS], ddtb[0, :DN_HEADS], dnw[0]
    qh, kh, vh = sv["heads"]
    sinks4 = p["sinks"].reshape(ATT_KV_HEADS, ATT_GROUP, 1, 1)
    dqh, dkh, dvh, dsk = _attn_bwd(qh, kh, vh, sinks4, _heads_q(do_a))
    g["sinks"] = dsk.reshape(ATT_KV_HEADS * ATT_GROUP)
    s = proj.shape[0]
    dproj = jnp.concatenate([
        _unheads_q(dqh).astype(BF16), ddq, ddk, ddv, ddz, dga, dgb,
        _unheads_kv(dkh[:, WINDOW:, :]).astype(BF16), _unheads_kv(dvh[:, WINDOW:, :]).astype(BF16),
        dba.astype(BF16), jnp.zeros((s, D_IN_P - C_BA - LANES), BF16)], axis=1)
    du = _mm(dproj, p["w_in"], "nt", BF16,"du")
    g["w_in"] = _mm(sv["u"], dproj, "tn", BF16,"dw_in")
    (dx,), (dsc1, dsh1) = _rowwise_bwd(_f_mod, [_whole(sv["x"])], [sc1, sh1], [du], [F32], "mod1_bwd", add=(0, dx_a))
    g["mod"] = jnp.stack([dsh1, dsc1, dgt1, dsh2, dsc2, dgt2])
    return dx, g, exchanged


def _permute_w_in(w):
    pad = jnp.zeros(w.shape[:-1] + (D_IN_P - D_IN,), w.dtype)
    return jnp.concatenate([w[..., 0:1024], w[..., 1536:5632], w[..., 5648:7696], w[..., 1024:1536],
                            w[..., 5632:5648], pad], axis=-1)


def _unpermute_w_in(g):
    return jnp.concatenate([g[..., 0:1024], g[..., C_K:C_K + 512], g[..., 1024:5120], g[..., C_BA:C_BA + 16],
                            g[..., 5120:7168]], axis=-1)


def _cols_from_chips(t):
    c, l, r, n = t.shape
    return t.transpose(1, 2, 0, 3).reshape(l, r, c * n)


def _cols_to_chips(t):
    l, r, n4 = t.shape
    return t.reshape(l, r, N_CHIPS, n4 // N_CHIPS).transpose(2, 0, 1, 3)


def _rows_from_chips(t):
    c, l, r, n = t.shape
    return t.transpose(1, 0, 2, 3).reshape(l, c * r, n)


def _rows_to_chips(t):
    l, r4, n = t.shape
    return t.reshape(l, N_CHIPS, r4 // N_CHIPS, n).transpose(1, 0, 2, 3)


_REPLICATED = ("b_ada", "a_log", "dt_bias", "sinks", "dn_norm_w", "ln1_g", "ln1_b", "b_ff1", "b_ff2", "ln2_g", "ln2_b")
_SMALL = _REPLICATED + ("conv_w",)
_PACK_W = 1024
_WEIGHT_ORDER = ("w_ada", "b_ada", "w_in", "conv_w", "a_log", "dt_bias", "sinks", "dn_norm_w", "w_oa", "w_ob", "w_out",
                 "ln1_g", "ln1_b", "w_ff1", "b_ff1", "w_ff2", "b_ff2", "ln2_g", "ln2_b")


def _pack_small(d):
    flat = jnp.concatenate([d[k].reshape(-1) for k in _SMALL])
    rows = -(-flat.shape[0] // (_PACK_W * 8)) * 8
    return jnp.pad(flat, (0, rows * _PACK_W - flat.shape[0])).reshape(rows, _PACK_W)


def _unpack_small(packed, shapes):
    flat = packed.reshape(-1)
    out, off = {}, 0
    for k in _SMALL:
        n = 1
        for d_ in shapes[k]:
            n *= d_
        out[k] = flat[off:off + n].reshape(shapes[k])
        off += n
    return out


def kernel(x, c, w_ada, b_ada, w_in, conv_w, a_log, dt_bias, sinks, dn_norm_w, w_oa, w_ob, w_out, ln1_g, ln1_b, w_ff1, b_ff1, w_ff2, b_ff2, ln2_g, ln2_b, loss_target, m_w_ada, m_b_ada, m_w_in, m_conv_w, m_a_log, m_dt_bias, m_sinks, m_dn_norm_w, m_w_oa, m_w_ob, m_w_out, m_ln1_g, m_ln1_b, m_w_ff1, m_b_ff1, m_w_ff2, m_b_ff2, m_ln2_g, m_ln2_b, v_w_ada, v_b_ada, v_w_in, v_conv_w, v_a_log, v_dt_bias, v_sinks, v_dn_norm_w, v_w_oa, v_w_ob, v_w_out, v_ln1_g, v_ln1_b, v_w_ff1, v_b_ff1, v_w_ff2, v_b_ff2, v_ln2_g, v_ln2_b):
    ix, iy, ic = _place()
    chip = 2 * ix + iy
    dev = 4 * ix + 2 * iy + ic
    weights = dict(w_ada=w_ada, b_ada=b_ada, w_in=w_in, conv_w=conv_w, a_log=a_log, dt_bias=dt_bias, sinks=sinks,
                   dn_norm_w=dn_norm_w, w_oa=w_oa, w_ob=w_ob, w_out=w_out, ln1_g=ln1_g, ln1_b=ln1_b, w_ff1=w_ff1,
                   b_ff1=b_ff1, w_ff2=w_ff2, b_ff2=b_ff2, ln2_g=ln2_g, ln2_b=ln2_b)
    mom_m = dict(w_ada=m_w_ada, b_ada=m_b_ada, w_in=m_w_in, conv_w=m_conv_w, a_log=m_a_log, dt_bias=m_dt_bias,
                 sinks=m_sinks, dn_norm_w=m_dn_norm_w, w_oa=m_w_oa, w_ob=m_w_ob, w_out=m_w_out, ln1_g=m_ln1_g,
                 ln1_b=m_ln1_b, w_ff1=m_w_ff1, b_ff1=m_b_ff1, w_ff2=m_w_ff2, b_ff2=m_b_ff2, ln2_g=m_ln2_g, ln2_b=m_ln2_b)
    mom_v = dict(w_ada=v_w_ada, b_ada=v_b_ada, w_in=v_w_in, conv_w=v_conv_w, a_log=v_a_log, dt_bias=v_dt_bias,
                 sinks=v_sinks, dn_norm_w=v_dn_norm_w, w_oa=v_w_oa, w_ob=v_w_ob, w_out=v_w_out, ln1_g=v_ln1_g,
                 ln1_b=v_ln1_b, w_ff1=v_w_ff1, b_ff1=v_b_ff1, w_ff2=v_w_ff2, b_ff2=v_b_ff2, ln2_g=v_ln2_g, ln2_b=v_ln2_b)

    n_ada = w_ada.shape[2]
    n_cw = conv_w.shape[2]
    taps = jnp.pad(conv_w.reshape(DEPTH * CONV_K, n_cw), ((0, 0), (0, D_MODEL - n_cw)))
    first = _all_gather8(jnp.concatenate([jnp.pad(c, ((0, 7), (0, 0))), taps], axis=0))
    c_all = first[:, 0, :]
    b_shard = lax.dynamic_slice_in_dim(b_ada, chip * n_ada, n_ada, axis=1).reshape(DEPTH, 1, n_ada)
    mod_t = _ada_fwd(c_all, w_ada, b_shard)
    mod_all = _all_gather8(mod_t.reshape(DEPTH * N_DEV, n_ada)).reshape(N_DEV, DEPTH, N_DEV, n_ada)
    mod_mine = lax.dynamic_index_in_dim(mod_all[0::2], dev, axis=2, keepdims=False)
    mod = mod_mine.transpose(1, 0, 2).reshape(DEPTH, 6, 1, D_MODEL)

    cw_all = first[0::2, 8:, :n_cw]
    conv_full = cw_all.transpose(1, 0, 2).reshape(DEPTH, CONV_K, N_CHIPS * n_cw)

    big = ("w_in", "w_oa", "w_ob", "w_out", "w_ff1", "w_ff2")
    w16 = {k: weights[k].astype(BF16) for k in big}
    from_chips = dict(w_in=lambda t: _permute_w_in(_cols_from_chips(t)), w_ff1=_cols_from_chips, w_oa=_rows_from_chips,
                      w_ob=_rows_from_chips, w_out=_rows_from_chips, w_ff2=_rows_from_chips)

    def assemble(name, gathered):
        return from_chips[name](gathered[:, None])[0]

    to_chips = dict(w_in=lambda t: _cols_to_chips(_unpermute_w_in(t)), w_ff1=_cols_to_chips, w_oa=_rows_to_chips,
                    w_ob=_rows_to_chips, w_out=_rows_to_chips, w_ff2=_rows_to_chips)

    def slices_for_chips(g, keys):
        return [to_chips[k](g[k][None])[:, 0].astype(BF16) for k in keys]

    full = [dict() for _ in range(DEPTH)]
    full[0]["w_in"] = assemble("w_in", _chip_exchange([w16["w_in"][0]], True, "gather_weights")[0])

    def layer_params(l):
        p = dict(full[l])
        p["mod"] = mod[l]
        p["conv_w"] = conv_full[l]
        for k in ("a_log", "dt_bias", "sinks", "dn_norm_w"):
            p[k] = weights[k][l]
        for k in ("ln1_g", "ln1_b", "b_ff1", "b_ff2", "ln2_g", "ln2_b"):
            p[k] = weights[k][l].reshape(1, -1)
        return p

    xs = x[0]
    saved = []
    for l in range(DEPTH):
        fetch = dict(o=[w16[k][l] for k in ("w_oa", "w_ob", "w_out")], ff1=[w16["w_ff1"][l]], ff2=[w16["w_ff2"][l]],
                     w_in_next=[w16["w_in"][l + 1]] if l + 1 < DEPTH else [], assemble=assemble)
        xs, sv, arrived = _layer_fwd(xs, layer_params(l), fetch)
        nxt = arrived.pop("w_in_next")
        full[l].update(arrived)
        if nxt is not None:
            full[l + 1]["w_in"] = assemble("w_in", nxt)
        saved.append(sv)
    dy, loss_local = _loss_head(xs, loss_target[0])
    loss = lax.psum(loss_local[0, 0], ("x", "y", "c"))
    early = tuple(k for k in big if k != "w_in")
    grads = [None] * DEPTH
    received = [dict() for _ in range(DEPTH)]
    dx = dy
    pending = []
    for l in reversed(range(DEPTH)):
        carry = functools.partial(lambda g, first: first + slices_for_chips(g, early), first=pending)
        dx, grads[l], got = _layer_bwd(dx, layer_params(l), saved[l], carry)
        if pending:
            received[l + 1]["w_in"] = got[0]
        received[l].update(zip(early, got[len(pending):]))
        pending = slices_for_chips(grads[l], ("w_in",))
    received[0]["w_in"] = _chip_exchange(pending, False, "scatter_grads")[0]
    grad_x = dx[None]
    gstack = {k: jnp.stack([grads[l][k] for l in range(DEPTH)]) for k in grads[0] if k not in big}

    dmod = gstack["mod"].reshape(DEPTH, 6 * D_MODEL)
    small_g = dict(b_ada=dmod, a_log=gstack["a_log"], dt_bias=gstack["dt_bias"], sinks=gstack["sinks"],
                   dn_norm_w=gstack["dn_norm_w"], ln1_g=gstack["ln1_g"], ln1_b=gstack["ln1_b"], b_ff1=gstack["b_ff1"],
                   b_ff2=gstack["b_ff2"], ln2_g=gstack["ln2_g"], ln2_b=gstack["ln2_b"], conv_w=gstack["conv_w"])
    shapes = {k: weights[k].shape for k in _REPLICATED}
    shapes["conv_w"] = small_g["conv_w"].shape
    g_all = _all_gather8(_pack_small(small_g))
    no_conv = jnp.zeros(shapes["conv_w"], F32)
    small_out = _small_adam(g_all, _pack_small(dict(weights, conv_w=no_conv)), _pack_small(dict(mom_m, conv_w=no_conv)),
                            _pack_small(dict(mom_v, conv_w=no_conv)))
    small_res = [_unpack_small(t, shapes) for t in small_out]
    g_conv = lax.dynamic_slice_in_dim(small_res[0]["conv_w"], chip * n_cw, n_cw, axis=2)
    res = {"conv_w": _adam_call([g_conv], conv_w, m_conv_w, v_conv_w, "adam_conv_w", tile=16)}

    dmod_all = g_all.reshape(N_DEV, -1)[:, :DEPTH * 6 * D_MODEL].reshape(N_DEV, DEPTH, 6 * D_MODEL)
    dmod_shard = lax.dynamic_slice_in_dim(dmod_all, chip * n_ada, n_ada, axis=2).transpose(1, 0, 2)
    g_w_ada = _ada_bwd(c_all, dmod_shard)
    res["w_ada"] = _adam_call([g_w_ada], w_ada, m_w_ada, v_w_ada, "adam_w_ada")

    partial = [_sum_slots([received[l][k] for l in range(DEPTH)], "sum_" + k) for k in big]
    theirs = _sibling_exchange(partial, "sibling_grads")
    for k, mine, other in zip(big, partial, theirs):
        shape = weights[k].shape
        res[k] = _adam_call([mine.reshape(shape), other.reshape(shape)], weights[k], mom_m[k], mom_v[k], "adam_" + k)
    for k in _REPLICATED:
        res[k] = [small_res[i][k] for i in range(4)]

    outs = [loss, grad_x]
    for i in range(4):
        outs += [res[k][i] for k in _WEIGHT_ORDER]
    return tuple(outs)
```
